```python
import jax, jax.numpy as jnp
from jax import lax
import numpy as np

D_MODEL = 1024
BATCH = 8
SEQ = 2048
DEPTH = 1

D_FF = 4 * D_MODEL
GMLP_WIDTH = D_MODEL
GMLP_HEADS = 8
GMLP_HEAD_DIM = GMLP_WIDTH // GMLP_HEADS
CHUNK = 128
CONV_WIDTH = D_MODEL
CONV_KERNEL = 31
N_MOD = 9
EPS = 1e-6
SPLIT_SIZES = (GMLP_WIDTH, GMLP_WIDTH, CONV_WIDTH, CONV_WIDTH, D_MODEL, D_MODEL)
D_IN = sum(SPLIT_SIZES)

kernel_name = "hybrid_gmlp_conformer_conv_gated_macaron"


def _split_points(sizes):
    pts, acc = [], 0
    for s in sizes[:-1]:
        acc += s
        pts.append(acc)
    return pts


def rms_norm(x, g):
    xf = x.astype(jnp.float32)
    y = xf * lax.rsqrt(jnp.mean(xf * xf, axis=-1, keepdims=True) + EPS)
    return (y * g.astype(jnp.float32)).astype(x.dtype)


def layer_norm(x, g, b):
    xf = x.astype(jnp.float32)
    mu = jnp.mean(xf, axis=-1, keepdims=True)
    var = jnp.mean(jnp.square(xf - mu), axis=-1, keepdims=True)
    y = (xf - mu) * lax.rsqrt(var + EPS)
    return (y * g.astype(jnp.float32) + b.astype(jnp.float32)).astype(x.dtype)


def modulate(h, shift, scale):
    return h * (1.0 + scale[:, None, :]) + shift[:, None, :]


def swiglu(h, w_gate, w_up, w_down):
    return (jax.nn.silu(h @ w_gate) * (h @ w_up)) @ w_down


def token_mixer(h, w_in, b_in, sgu_ln_g, sgu_ln_b, sgu_w_s, sgu_b_s,
                conv_w, conv_b, conv_ln_g, conv_ln_b, w_branch_a, w_branch_b, w_out):
    bsz, seq, _ = h.shape
    proj = h @ w_in + b_in
    u, v, cv, cg, ga, gb = jnp.split(proj, _split_points(SPLIT_SIZES), axis=-1)

    u = jax.nn.gelu(u)
    v = layer_norm(jax.nn.gelu(v), sgu_ln_g, sgu_ln_b)
    n_chunks = seq // CHUNK
    v = v.reshape(bsz, n_chunks, CHUNK, GMLP_HEADS, GMLP_HEAD_DIM)
    causal = jnp.tril(jnp.ones((CHUNK, CHUNK), dtype=bool))
    w_s = jnp.where(causal[None], sgu_w_s, jnp.zeros_like(sgu_w_s))
    v_mix = jnp.einsum('hts,bcshd->bcthd', w_s, v) + sgu_b_s.T[None, None, :, :, None]
    y_a = (u * v_mix.reshape(bsz, seq, GMLP_WIDTH)) @ w_branch_a

    z = cv * jax.nn.sigmoid(cg)
    z = lax.conv_general_dilated(
        z, conv_w[:, None, :], window_strides=(1,),
        padding=[(CONV_KERNEL - 1, 0)],
        dimension_numbers=('NWC', 'WIO', 'NWC'),
        feature_group_count=CONV_WIDTH) + conv_b
    z = jax.nn.silu(layer_norm(z, conv_ln_g, conv_ln_b))
    y_b = z @ w_branch_b

    merged = jax.nn.sigmoid(ga) * y_a + jax.nn.sigmoid(gb) * y_b
    return merged @ w_out


def _fwd_setup_inputs(seed: int = 0) -> dict:
    key = jax.random.key(seed)
    ks = jax.random.split(key, 32)
    f32 = jnp.float32

    def nrm(k, shape, fan_in, mult=1.0):
        return jax.random.normal(k, shape, f32) * (mult * fan_in ** -0.5)

    def gain(k, shape):
        return 1.0 + 0.05 * jax.random.normal(k, shape, f32)

    def bias(k, shape):
        return 0.02 * jax.random.normal(k, shape, f32)

    L = DEPTH
    return {
        "x": jax.random.normal(ks[0], (BATCH, SEQ, D_MODEL), f32),
        "c": jax.random.normal(ks[1], (BATCH, D_MODEL), f32),
        "ada_w": nrm(ks[2], (L, D_MODEL, N_MOD * D_MODEL), D_MODEL, 0.5),
        "ada_b": bias(ks[3], (L, N_MOD * D_MODEL)),
        "norm_ffn1": gain(ks[4], (L, D_MODEL)),
        "ffn1_w_gate": nrm(ks[5], (L, D_MODEL, D_FF), D_MODEL),
        "ffn1_w_up": nrm(ks[6], (L, D_MODEL, D_FF), D_MODEL),
        "ffn1_w_down": nrm(ks[7], (L, D_FF, D_MODEL), D_FF),
        "norm_mix": gain(ks[8], (L, D_MODEL)),
        "mix_w_in": nrm(ks[9], (L, D_MODEL, D_IN), D_MODEL),
        "mix_b_in": bias(ks[10], (L, D_IN)),
        "sgu_ln_g": gain(ks[11], (L, GMLP_WIDTH)),
        "sgu_ln_b": bias(ks[12], (L, GMLP_WIDTH)),
        "sgu_w_s": nrm(ks[13], (L, GMLP_HEADS, CHUNK, CHUNK), CHUNK),
        "sgu_b_s": gain(ks[14], (L, GMLP_HEADS, CHUNK)),
        "conv_w": nrm(ks[15], (L, CONV_KERNEL, CONV_WIDTH), CONV_KERNEL),
        "conv_b": bias(ks[16], (L, CONV_WIDTH)),
        "conv_ln_g": gain(ks[17], (L, CONV_WIDTH)),
        "conv_ln_b": bias(ks[18], (L, CONV_WIDTH)),
        "w_branch_a": nrm(ks[19], (L, GMLP_WIDTH, D_MODEL), GMLP_WIDTH),
        "w_branch_b": nrm(ks[20], (L, CONV_WIDTH, D_MODEL), CONV_WIDTH),
        "w_out": nrm(ks[21], (L, D_MODEL, D_MODEL), D_MODEL),
        "norm_ffn2": gain(ks[22], (L, D_MODEL)),
        "ffn2_w_gate": nrm(ks[23], (L, D_MODEL, D_FF), D_MODEL),
        "ffn2_w_up": nrm(ks[24], (L, D_MODEL, D_FF), D_MODEL),
        "ffn2_w_down": nrm(ks[25], (L, D_FF, D_MODEL), D_FF),
        "norm_final": gain(ks[26], (D_MODEL,)),
    }


def _fwd_reference(x, c, ada_w, ada_b, norm_ffn1, ffn1_w_gate, ffn1_w_up, ffn1_w_down,
              norm_mix, mix_w_in, mix_b_in, sgu_ln_g, sgu_ln_b, sgu_w_s, sgu_b_s,
              conv_w, conv_b, conv_ln_g, conv_ln_b, w_branch_a, w_branch_b, w_out,
              norm_ffn2, ffn2_w_gate, ffn2_w_up, ffn2_w_down, norm_final):
    c_act = jax.nn.silu(c)
    for l in range(DEPTH):
        mod = c_act @ ada_w[l] + ada_b[l]
        sh1, sc1, g1, sh2, sc2, g2, sh3, sc3, g3 = jnp.split(mod, N_MOD, axis=-1)

        h = modulate(rms_norm(x, norm_ffn1[l]), sh1, sc1)
        x = x + 0.5 * g1[:, None, :] * swiglu(h, ffn1_w_gate[l], ffn1_w_up[l], ffn1_w_down[l])

        h = modulate(rms_norm(x, norm_mix[l]), sh2, sc2)
        y = token_mixer(h, mix_w_in[l], mix_b_in[l], sgu_ln_g[l], sgu_ln_b[l],
                        sgu_w_s[l], sgu_b_s[l], conv_w[l], conv_b[l], conv_ln_g[l],
                        conv_ln_b[l], w_branch_a[l], w_branch_b[l], w_out[l])
        x = x + g2[:, None, :] * y

        h = modulate(rms_norm(x, norm_ffn2[l]), sh3, sc3)
        x = x + 0.5 * g3[:, None, :] * swiglu(h, ffn2_w_gate[l], ffn2_w_up[l], ffn2_w_down[l])

    return rms_norm(x, norm_final)


import jax as _jax
import jax.numpy as _jnp

TWIN_FORMAT = 'train_step'
FWD_PARAMS = ['x', 'c', 'ada_w', 'ada_b', 'norm_ffn1', 'ffn1_w_gate', 'ffn1_w_up', 'ffn1_w_down', 'norm_mix', 'mix_w_in', 'mix_b_in', 'sgu_ln_g', 'sgu_ln_b', 'sgu_w_s', 'sgu_b_s', 'conv_w', 'conv_b', 'conv_ln_g', 'conv_ln_b', 'w_branch_a', 'w_branch_b', 'w_out', 'norm_ffn2', 'ffn2_w_gate', 'ffn2_w_up', 'ffn2_w_down', 'norm_final']
TWIN_WEIGHTS = ['ada_w', 'ada_b', 'norm_ffn1', 'ffn1_w_gate', 'ffn1_w_up', 'ffn1_w_down', 'norm_mix', 'mix_w_in', 'mix_b_in', 'sgu_ln_g', 'sgu_ln_b', 'sgu_w_s', 'sgu_b_s', 'conv_w', 'conv_b', 'conv_ln_g', 'conv_ln_b', 'w_branch_a', 'w_branch_b', 'w_out', 'norm_ffn2', 'ffn2_w_gate', 'ffn2_w_up', 'ffn2_w_down', 'norm_final']
TWIN_DIFF_INPUT = 'x'
TWIN_INPUTS = ['x', 'c', 'ada_w', 'ada_b', 'norm_ffn1', 'ffn1_w_gate', 'ffn1_w_up', 'ffn1_w_down', 'norm_mix', 'mix_w_in', 'mix_b_in', 'sgu_ln_g', 'sgu_ln_b', 'sgu_w_s', 'sgu_b_s', 'conv_w', 'conv_b', 'conv_ln_g', 'conv_ln_b', 'w_branch_a', 'w_branch_b', 'w_out', 'norm_ffn2', 'ffn2_w_gate', 'ffn2_w_up', 'ffn2_w_down', 'norm_final', 'loss_target', 'm_ada_w', 'm_ada_b', 'm_norm_ffn1', 'm_ffn1_w_gate', 'm_ffn1_w_up', 'm_ffn1_w_down', 'm_norm_mix', 'm_mix_w_in', 'm_mix_b_in', 'm_sgu_ln_g', 'm_sgu_ln_b', 'm_sgu_w_s', 'm_sgu_b_s', 'm_conv_w', 'm_conv_b', 'm_conv_ln_g', 'm_conv_ln_b', 'm_w_branch_a', 'm_w_branch_b', 'm_w_out', 'm_norm_ffn2', 'm_ffn2_w_gate', 'm_ffn2_w_up', 'm_ffn2_w_down', 'm_norm_final', 'v_ada_w', 'v_ada_b', 'v_norm_ffn1', 'v_ffn1_w_gate', 'v_ffn1_w_up', 'v_ffn1_w_down', 'v_norm_mix', 'v_mix_w_in', 'v_mix_b_in', 'v_sgu_ln_g', 'v_sgu_ln_b', 'v_sgu_w_s', 'v_sgu_b_s', 'v_conv_w', 'v_conv_b', 'v_conv_ln_g', 'v_conv_ln_b', 'v_w_branch_a', 'v_w_branch_b', 'v_w_out', 'v_norm_ffn2', 'v_ffn2_w_gate', 'v_ffn2_w_up', 'v_ffn2_w_down', 'v_norm_final']
TWIN_OUTPUTS = ['loss', 'grad_x', 'grad_ada_w', 'grad_ada_b', 'grad_norm_ffn1', 'grad_ffn1_w_gate', 'grad_ffn1_w_up', 'grad_ffn1_w_down', 'grad_norm_mix', 'grad_mix_w_in', 'grad_mix_b_in', 'grad_sgu_ln_g', 'grad_sgu_ln_b', 'grad_sgu_w_s', 'grad_sgu_b_s', 'grad_conv_w', 'grad_conv_b', 'grad_conv_ln_g', 'grad_conv_ln_b', 'grad_w_branch_a', 'grad_w_branch_b', 'grad_w_out', 'grad_norm_ffn2', 'grad_ffn2_w_gate', 'grad_ffn2_w_up', 'grad_ffn2_w_down', 'grad_norm_final', 'delta_ada_w', 'delta_ada_b', 'delta_norm_ffn1', 'delta_ffn1_w_gate', 'delta_ffn1_w_up', 'delta_ffn1_w_down', 'delta_norm_mix', 'delta_mix_w_in', 'delta_mix_b_in', 'delta_sgu_ln_g', 'delta_sgu_ln_b', 'delta_sgu_w_s', 'delta_sgu_b_s', 'delta_conv_w', 'delta_conv_b', 'delta_conv_ln_g', 'delta_conv_ln_b', 'delta_w_branch_a', 'delta_w_branch_b', 'delta_w_out', 'delta_norm_ffn2', 'delta_ffn2_w_gate', 'delta_ffn2_w_up', 'delta_ffn2_w_down', 'delta_norm_final', 'new_m_ada_w', 'new_m_ada_b', 'new_m_norm_ffn1', 'new_m_ffn1_w_gate', 'new_m_ffn1_w_up', 'new_m_ffn1_w_down', 'new_m_norm_mix', 'new_m_mix_w_in', 'new_m_mix_b_in', 'new_m_sgu_ln_g', 'new_m_sgu_ln_b', 'new_m_sgu_w_s', 'new_m_sgu_b_s', 'new_m_conv_w', 'new_m_conv_b', 'new_m_conv_ln_g', 'new_m_conv_ln_b', 'new_m_w_branch_a', 'new_m_w_branch_b', 'new_m_w_out', 'new_m_norm_ffn2', 'new_m_ffn2_w_gate', 'new_m_ffn2_w_up', 'new_m_ffn2_w_down', 'new_m_norm_final', 'new_v_ada_w', 'new_v_ada_b', 'new_v_norm_ffn1', 'new_v_ffn1_w_gate', 'new_v_ffn1_w_up', 'new_v_ffn1_w_down', 'new_v_norm_mix', 'new_v_mix_w_in', 'new_v_mix_b_in', 'new_v_sgu_ln_g', 'new_v_sgu_ln_b', 'new_v_sgu_w_s', 'new_v_sgu_b_s', 'new_v_conv_w', 'new_v_conv_b', 'new_v_conv_ln_g', 'new_v_conv_ln_b', 'new_v_w_branch_a', 'new_v_w_branch_b', 'new_v_w_out', 'new_v_norm_ffn2', 'new_v_ffn2_w_gate', 'new_v_ffn2_w_up', 'new_v_ffn2_w_down', 'new_v_norm_final']
TWIN_LEAF_KINDS = {'loss': 'loss', 'grad_x': 'grad_x', 'grad_ada_w': 'grad_w', 'grad_ada_b': 'grad_w', 'grad_norm_ffn1': 'grad_w', 'grad_ffn1_w_gate': 'grad_w', 'grad_ffn1_w_up': 'grad_w', 'grad_ffn1_w_down': 'grad_w', 'grad_norm_mix': 'grad_w', 'grad_mix_w_in': 'grad_w', 'grad_mix_b_in': 'grad_w', 'grad_sgu_ln_g': 'grad_w', 'grad_sgu_ln_b': 'grad_w', 'grad_sgu_w_s': 'grad_w', 'grad_sgu_b_s': 'grad_w', 'grad_conv_w': 'grad_w', 'grad_conv_b': 'grad_w', 'grad_conv_ln_g': 'grad_w', 'grad_conv_ln_b': 'grad_w', 'grad_w_branch_a': 'grad_w', 'grad_w_branch_b': 'grad_w', 'grad_w_out': 'grad_w', 'grad_norm_ffn2': 'grad_w', 'grad_ffn2_w_gate': 'grad_w', 'grad_ffn2_w_up': 'grad_w', 'grad_ffn2_w_down': 'grad_w', 'grad_norm_final': 'grad_w', 'delta_ada_w': 'delta_w', 'delta_ada_b': 'delta_w', 'delta_norm_ffn1': 'delta_w', 'delta_ffn1_w_gate': 'delta_w', 'delta_ffn1_w_up': 'delta_w', 'delta_ffn1_w_down': 'delta_w', 'delta_norm_mix': 'delta_w', 'delta_mix_w_in': 'delta_w', 'delta_mix_b_in': 'delta_w', 'delta_sgu_ln_g': 'delta_w', 'delta_sgu_ln_b': 'delta_w', 'delta_sgu_w_s': 'delta_w', 'delta_sgu_b_s': 'delta_w', 'delta_conv_w': 'delta_w', 'delta_conv_b': 'delta_w', 'delta_conv_ln_g': 'delta_w', 'delta_conv_ln_b': 'delta_w', 'delta_w_branch_a': 'delta_w', 'delta_w_branch_b': 'delta_w', 'delta_w_out': 'delta_w', 'delta_norm_ffn2': 'delta_w', 'delta_ffn2_w_gate': 'delta_w', 'delta_ffn2_w_up': 'delta_w', 'delta_ffn2_w_down': 'delta_w', 'delta_norm_final': 'delta_w', 'new_m_ada_w': 'new_m', 'new_m_ada_b': 'new_m', 'new_m_norm_ffn1': 'new_m', 'new_m_ffn1_w_gate': 'new_m', 'new_m_ffn1_w_up': 'new_m', 'new_m_ffn1_w_down': 'new_m', 'new_m_norm_mix': 'new_m', 'new_m_mix_w_in': 'new_m', 'new_m_mix_b_in': 'new_m', 'new_m_sgu_ln_g': 'new_m', 'new_m_sgu_ln_b': 'new_m', 'new_m_sgu_w_s': 'new_m', 'new_m_sgu_b_s': 'new_m', 'new_m_conv_w': 'new_m', 'new_m_conv_b': 'new_m', 'new_m_conv_ln_g': 'new_m', 'new_m_conv_ln_b': 'new_m', 'new_m_w_branch_a': 'new_m', 'new_m_w_branch_b': 'new_m', 'new_m_w_out': 'new_m', 'new_m_norm_ffn2': 'new_m', 'new_m_ffn2_w_gate': 'new_m', 'new_m_ffn2_w_up': 'new_m', 'new_m_ffn2_w_down': 'new_m', 'new_m_norm_final': 'new_m', 'new_v_ada_w': 'new_v', 'new_v_ada_b': 'new_v', 'new_v_norm_ffn1': 'new_v', 'new_v_ffn1_w_gate': 'new_v', 'new_v_ffn1_w_up': 'new_v', 'new_v_ffn1_w_down': 'new_v', 'new_v_norm_mix': 'new_v', 'new_v_mix_w_in': 'new_v', 'new_v_mix_b_in': 'new_v', 'new_v_sgu_ln_g': 'new_v', 'new_v_sgu_ln_b': 'new_v', 'new_v_sgu_w_s': 'new_v', 'new_v_sgu_b_s': 'new_v', 'new_v_conv_w': 'new_v', 'new_v_conv_b': 'new_v', 'new_v_conv_ln_g': 'new_v', 'new_v_conv_ln_b': 'new_v', 'new_v_w_branch_a': 'new_v', 'new_v_w_branch_b': 'new_v', 'new_v_w_out': 'new_v', 'new_v_norm_ffn2': 'new_v', 'new_v_ffn2_w_gate': 'new_v', 'new_v_ffn2_w_up': 'new_v', 'new_v_ffn2_w_down': 'new_v', 'new_v_norm_final': 'new_v'}


def _forward(args):
    return _fwd_reference(*[args[k] for k in FWD_PARAMS])


def _output_shape():
    out = _jax.eval_shape(lambda: _forward(_fwd_setup_inputs(0)))
    return out.shape, out.dtype

N_MICROBATCH = 1
ADAM_LR = 0.001
ADAM_B1 = 0.9
ADAM_B2 = 0.999
ADAM_EPS = 1e-08
ADAM_WD = 0.01
ADAM_STEP = 10
PER_EXAMPLE_BATCH_AXIS = {'x': 0, 'c': 0, 'loss_target': 0}
SHARED_INPUTS = []
_WEIGHT_DTYPES = {'ada_w': _jnp.float32, 'ada_b': _jnp.float32, 'norm_ffn1': _jnp.float32, 'ffn1_w_gate': _jnp.float32, 'ffn1_w_up': _jnp.float32, 'ffn1_w_down': _jnp.float32, 'norm_mix': _jnp.float32, 'mix_w_in': _jnp.float32, 'mix_b_in': _jnp.float32, 'sgu_ln_g': _jnp.float32, 'sgu_ln_b': _jnp.float32, 'sgu_w_s': _jnp.float32, 'sgu_b_s': _jnp.float32, 'conv_w': _jnp.float32, 'conv_b': _jnp.float32, 'conv_ln_g': _jnp.float32, 'conv_ln_b': _jnp.float32, 'w_branch_a': _jnp.float32, 'w_branch_b': _jnp.float32, 'w_out': _jnp.float32, 'norm_ffn2': _jnp.float32, 'ffn2_w_gate': _jnp.float32, 'ffn2_w_up': _jnp.float32, 'ffn2_w_down': _jnp.float32, 'norm_final': _jnp.float32}
MOMENT_SCALE = {'ada_w': 2.692214e-02, 'ada_b': 4.606789e-02, 'norm_ffn1': 2.053471e-02, 'ffn1_w_gate': 7.162620e-03, 'ffn1_w_up': 6.954721e-03, 'ffn1_w_down': 1.389184e-02, 'norm_mix': 2.829526e-02, 'mix_w_in': 1.169199e-02, 'mix_b_in': 1.112184e-02, 'sgu_ln_g': 1.069093e-02, 'sgu_ln_b': 1.043972e-02, 'sgu_w_s': 1.082516e-02, 'sgu_b_s': 1.584839e-02, 'conv_w': 1.324449e-02, 'conv_b': 2.510187e-02, 'conv_ln_g': 1.771970e-02, 'conv_ln_b': 1.549924e-02, 'w_branch_a': 1.946254e-02, 'w_branch_b': 1.305737e-02, 'w_out': 2.368781e-02, 'norm_ffn2': 1.888312e-02, 'ffn2_w_gate': 7.075438e-03, 'ffn2_w_up': 6.888516e-03, 'ffn2_w_down': 1.374471e-02, 'norm_final': 1.601761e+01}


def _to_microbatches(a, axis):
    t = _jnp.moveaxis(a, axis, 0)
    t = t.reshape((N_MICROBATCH, t.shape[0] // N_MICROBATCH) + t.shape[1:])
    return _jnp.moveaxis(t, 1, axis + 1)


def setup_inputs(seed: int = 0) -> dict:
    inp = _fwd_setup_inputs(seed)
    key = _jax.random.fold_in(_jax.random.key(seed), 7919)
    shape, _ = _output_shape()
    out = dict(inp)
    out["loss_target"] = _jax.random.normal(_jax.random.fold_in(key, 0), shape, _jnp.float32)
    for i, name in enumerate(TWIN_WEIGHTS):
        w = inp[name].astype(_jnp.float32)
        if MOMENT_SCALE is None:
            s = _jnp.sqrt(_jnp.mean(_jnp.square(w)) + 1e-30)
        else:
            s = MOMENT_SCALE[name]
        km, kv = _jax.random.split(_jax.random.fold_in(key, i + 1))
        out[name] = w
        out["m_" + name] = s * _jax.random.normal(km, w.shape, _jnp.float32)
        out["v_" + name] = (s * s) * _jax.random.uniform(kv, w.shape, _jnp.float32, 0.5, 1.5)
    if N_MICROBATCH > 1:
        for name, axis in PER_EXAMPLE_BATCH_AXIS.items():
            out[name] = _to_microbatches(out[name], axis)
    return {'x': out['x'], 'c': out['c'], 'ada_w': out['ada_w'], 'ada_b': out['ada_b'], 'norm_ffn1': out['norm_ffn1'], 'ffn1_w_gate': out['ffn1_w_gate'], 'ffn1_w_up': out['ffn1_w_up'], 'ffn1_w_down': out['ffn1_w_down'], 'norm_mix': out['norm_mix'], 'mix_w_in': out['mix_w_in'], 'mix_b_in': out['mix_b_in'], 'sgu_ln_g': out['sgu_ln_g'], 'sgu_ln_b': out['sgu_ln_b'], 'sgu_w_s': out['sgu_w_s'], 'sgu_b_s': out['sgu_b_s'], 'conv_w': out['conv_w'], 'conv_b': out['conv_b'], 'conv_ln_g': out['conv_ln_g'], 'conv_ln_b': out['conv_ln_b'], 'w_branch_a': out['w_branch_a'], 'w_branch_b': out['w_branch_b'], 'w_out': out['w_out'], 'norm_ffn2': out['norm_ffn2'], 'ffn2_w_gate': out['ffn2_w_gate'], 'ffn2_w_up': out['ffn2_w_up'], 'ffn2_w_down': out['ffn2_w_down'], 'norm_final': out['norm_final'], 'loss_target': out['loss_target'], 'm_ada_w': out['m_ada_w'], 'm_ada_b': out['m_ada_b'], 'm_norm_ffn1': out['m_norm_ffn1'], 'm_ffn1_w_gate': out['m_ffn1_w_gate'], 'm_ffn1_w_up': out['m_ffn1_w_up'], 'm_ffn1_w_down': out['m_ffn1_w_down'], 'm_norm_mix': out['m_norm_mix'], 'm_mix_w_in': out['m_mix_w_in'], 'm_mix_b_in': out['m_mix_b_in'], 'm_sgu_ln_g': out['m_sgu_ln_g'], 'm_sgu_ln_b': out['m_sgu_ln_b'], 'm_sgu_w_s': out['m_sgu_w_s'], 'm_sgu_b_s': out['m_sgu_b_s'], 'm_conv_w': out['m_conv_w'], 'm_conv_b': out['m_conv_b'], 'm_conv_ln_g': out['m_conv_ln_g'], 'm_conv_ln_b': out['m_conv_ln_b'], 'm_w_branch_a': out['m_w_branch_a'], 'm_w_branch_b': out['m_w_branch_b'], 'm_w_out': out['m_w_out'], 'm_norm_ffn2': out['m_norm_ffn2'], 'm_ffn2_w_gate': out['m_ffn2_w_gate'], 'm_ffn2_w_up': out['m_ffn2_w_up'], 'm_ffn2_w_down': out['m_ffn2_w_down'], 'm_norm_final': out['m_norm_final'], 'v_ada_w': out['v_ada_w'], 'v_ada_b': out['v_ada_b'], 'v_norm_ffn1': out['v_norm_ffn1'], 'v_ffn1_w_gate': out['v_ffn1_w_gate'], 'v_ffn1_w_up': out['v_ffn1_w_up'], 'v_ffn1_w_down': out['v_ffn1_w_down'], 'v_norm_mix': out['v_norm_mix'], 'v_mix_w_in': out['v_mix_w_in'], 'v_mix_b_in': out['v_mix_b_in'], 'v_sgu_ln_g': out['v_sgu_ln_g'], 'v_sgu_ln_b': out['v_sgu_ln_b'], 'v_sgu_w_s': out['v_sgu_w_s'], 'v_sgu_b_s': out['v_sgu_b_s'], 'v_conv_w': out['v_conv_w'], 'v_conv_b': out['v_conv_b'], 'v_conv_ln_g': out['v_conv_ln_g'], 'v_conv_ln_b': out['v_conv_ln_b'], 'v_w_branch_a': out['v_w_branch_a'], 'v_w_branch_b': out['v_w_branch_b'], 'v_w_out': out['v_w_out'], 'v_norm_ffn2': out['v_norm_ffn2'], 'v_ffn2_w_gate': out['v_ffn2_w_gate'], 'v_ffn2_w_up': out['v_ffn2_w_up'], 'v_ffn2_w_down': out['v_ffn2_w_down'], 'v_norm_final': out['v_norm_final']}


def _loss(weights, diff, rest, loss_target):
    with _jax.named_scope("forward"):
        args = {**rest, TWIN_DIFF_INPUT: diff, **{k: w.astype(_WEIGHT_DTYPES[k]) for k, w in weights.items()}}
        y = _forward(args)
    with _jax.named_scope("loss_head"):
        err = _jnp.square(y.astype(_jnp.float32) - loss_target)
        return 0.5 * _jnp.sum(_jnp.mean(err, axis=-1)) if err.ndim else 0.5 * err


def _adamw(w, g, m, v):
    m = ADAM_B1 * m + (1.0 - ADAM_B1) * g
    v = ADAM_B2 * v + (1.0 - ADAM_B2) * _jnp.square(g)
    m_hat = m / (1.0 - ADAM_B1 ** ADAM_STEP)
    v_hat = v / (1.0 - ADAM_B2 ** ADAM_STEP)
    delta = -ADAM_LR * (m_hat / (_jnp.sqrt(v_hat) + ADAM_EPS) + ADAM_WD * w)
    return delta, m, v


def reference(x, c, ada_w, ada_b, norm_ffn1, ffn1_w_gate, ffn1_w_up, ffn1_w_down, norm_mix, mix_w_in, mix_b_in, sgu_ln_g, sgu_ln_b, sgu_w_s, sgu_b_s, conv_w, conv_b, conv_ln_g, conv_ln_b, w_branch_a, w_branch_b, w_out, norm_ffn2, ffn2_w_gate, ffn2_w_up, ffn2_w_down, norm_final, loss_target, m_ada_w, m_ada_b, m_norm_ffn1, m_ffn1_w_gate, m_ffn1_w_up, m_ffn1_w_down, m_norm_mix, m_mix_w_in, m_mix_b_in, m_sgu_ln_g, m_sgu_ln_b, m_sgu_w_s, m_sgu_b_s, m_conv_w, m_conv_b, m_conv_ln_g, m_conv_ln_b, m_w_branch_a, m_w_branch_b, m_w_out, m_norm_ffn2, m_ffn2_w_gate, m_ffn2_w_up, m_ffn2_w_down, m_norm_final, v_ada_w, v_ada_b, v_norm_ffn1, v_ffn1_w_gate, v_ffn1_w_up, v_ffn1_w_down, v_norm_mix, v_mix_w_in, v_mix_b_in, v_sgu_ln_g, v_sgu_ln_b, v_sgu_w_s, v_sgu_b_s, v_conv_w, v_conv_b, v_conv_ln_g, v_conv_ln_b, v_w_branch_a, v_w_branch_b, v_w_out, v_norm_ffn2, v_ffn2_w_gate, v_ffn2_w_up, v_ffn2_w_down, v_norm_final):
    given = dict(x=x, c=c, ada_w=ada_w, ada_b=ada_b, norm_ffn1=norm_ffn1, ffn1_w_gate=ffn1_w_gate, ffn1_w_up=ffn1_w_up, ffn1_w_down=ffn1_w_down, norm_mix=norm_mix, mix_w_in=mix_w_in, mix_b_in=mix_b_in, sgu_ln_g=sgu_ln_g, sgu_ln_b=sgu_ln_b, sgu_w_s=sgu_w_s, sgu_b_s=sgu_b_s, conv_w=conv_w, conv_b=conv_b, conv_ln_g=conv_ln_g, conv_ln_b=conv_ln_b, w_branch_a=w_branch_a, w_branch_b=w_branch_b, w_out=w_out, norm_ffn2=norm_ffn2, ffn2_w_gate=ffn2_w_gate, ffn2_w_up=ffn2_w_up, ffn2_w_down=ffn2_w_down, norm_final=norm_final, loss_target=loss_target, m_ada_w=m_ada_w, m_ada_b=m_ada_b, m_norm_ffn1=m_norm_ffn1, m_ffn1_w_gate=m_ffn1_w_gate, m_ffn1_w_up=m_ffn1_w_up, m_ffn1_w_down=m_ffn1_w_down, m_norm_mix=m_norm_mix, m_mix_w_in=m_mix_w_in, m_mix_b_in=m_mix_b_in, m_sgu_ln_g=m_sgu_ln_g, m_sgu_ln_b=m_sgu_ln_b, m_sgu_w_s=m_sgu_w_s, m_sgu_b_s=m_sgu_b_s, m_conv_w=m_conv_w, m_conv_b=m_conv_b, m_conv_ln_g=m_conv_ln_g, m_conv_ln_b=m_conv_ln_b, m_w_branch_a=m_w_branch_a, m_w_branch_b=m_w_branch_b, m_w_out=m_w_out, m_norm_ffn2=m_norm_ffn2, m_ffn2_w_gate=m_ffn2_w_gate, m_ffn2_w_up=m_ffn2_w_up, m_ffn2_w_down=m_ffn2_w_down, m_norm_final=m_norm_final, v_ada_w=v_ada_w, v_ada_b=v_ada_b, v_norm_ffn1=v_norm_ffn1, v_ffn1_w_gate=v_ffn1_w_gate, v_ffn1_w_up=v_ffn1_w_up, v_ffn1_w_down=v_ffn1_w_down, v_norm_mix=v_norm_mix, v_mix_w_in=v_mix_w_in, v_mix_b_in=v_mix_b_in, v_sgu_ln_g=v_sgu_ln_g, v_sgu_ln_b=v_sgu_ln_b, v_sgu_w_s=v_sgu_w_s, v_sgu_b_s=v_sgu_b_s, v_conv_w=v_conv_w, v_conv_b=v_conv_b, v_conv_ln_g=v_conv_ln_g, v_conv_ln_b=v_conv_ln_b, v_w_branch_a=v_w_branch_a, v_w_branch_b=v_w_branch_b, v_w_out=v_w_out, v_norm_ffn2=v_norm_ffn2, v_ffn2_w_gate=v_ffn2_w_gate, v_ffn2_w_up=v_ffn2_w_up, v_ffn2_w_down=v_ffn2_w_down, v_norm_final=v_norm_final)
    weights = {n: given[n] for n in TWIN_WEIGHTS}
    shared = {n: given[n] for n in SHARED_INPUTS}
    per_example = {n: given[n] for n in ['x', 'c']}
    grad_fn = _jax.value_and_grad(_loss, argnums=(0, 1))

    def one_microbatch(ex, loss_target):
        ex = dict(ex)
        diff = ex.pop(TWIN_DIFF_INPUT)
        return grad_fn(weights, diff, {**shared, **ex}, loss_target)

    if N_MICROBATCH == 1:
        loss, (grad_w, grad_x) = one_microbatch(per_example, given["loss_target"])
    else:
        def body(carry, xs):
            loss_sum, grad_sum = carry
            l_k, (gw_k, gx_k) = one_microbatch(xs[0], xs[1])
            with _jax.named_scope("update"):
                return (loss_sum + l_k, _jax.tree.map(_jnp.add, grad_sum, gw_k)), gx_k

        init = (_jnp.zeros((), _jnp.float32), _jax.tree.map(_jnp.zeros_like, weights))
        (loss, grad_w), grad_x = _jax.lax.scan(body, init, (per_example, given["loss_target"]))
    with _jax.named_scope("update"):
        delta_w, new_m, new_v = {}, {}, {}
        for n in TWIN_WEIGHTS:
            delta_w[n], new_m[n], new_v[n] = _adamw(weights[n], grad_w[n], given["m_" + n], given["v_" + n])
    return (loss, grad_x, *[grad_w[n] for n in TWIN_WEIGHTS], *[delta_w[n] for n in TWIN_WEIGHTS],
            *[new_m[n] for n in TWIN_WEIGHTS], *[new_v[n] for n in TWIN_WEIGHTS])
```

```python
import functools

import jax
import jax.numpy as jnp
from jax import lax
from jax.experimental import pallas as pl
from jax.experimental.pallas import tpu as pltpu

F32 = jnp.float32
BF16 = jnp.bfloat16
D = 1024
HEADS = 8
CHUNK = 128
KCONV = 31
HALO = 32
N_MOD = 9
EPS = 1e-6
NCHIP = 4
VMEM_LIMIT = 56 * 1024 * 1024
MESH = pl.DeviceIdType.MESH

ADAM_LR = 0.001
ADAM_B1 = 0.9
ADAM_B2 = 0.999
ADAM_EPS = 1e-08
ADAM_WD = 0.01
ADAM_STEP = 10

NN = (((1,), (0,)), ((), ()))
NT = (((1,), (1,)), ((), ()))
TN = (((0,), (0,)), ((), ()))


def _cparams(*sem):
    return pltpu.CompilerParams(dimension_semantics=sem or None, vmem_limit_bytes=VMEM_LIMIT)


def _rms(x, w):
    return x * lax.rsqrt(jnp.mean(x * x, axis=-1, keepdims=True) + EPS) * w


def _ln(x, g, b):
    mu = jnp.mean(x, axis=-1, keepdims=True)
    xc = x - mu
    var = jnp.mean(xc * xc, axis=-1, keepdims=True)
    return xc * lax.rsqrt(var + EPS) * g + b


def _silu(x):
    return x * jax.nn.sigmoid(x)


def _gelu(x):
    return x * (0.5 * (1.0 + jnp.tanh(0.7978845608028654 * (x + 0.044715 * (x * x * x)))))


def _f_pre(x, nw, sc, sh):
    return _rms(x, nw) * (1.0 + sc) + sh


def _f_pre_keep(x, nw, sc, sh):
    return x, _f_pre(x, nw, sc, sh)


def _f_res_pre(scale):
    def fn(x, f, g, nw, sc, sh):
        xn = x + (scale * g) * f
        return xn, _f_pre(xn, nw, sc, sh)
    return fn


def _f_glu(a, b):
    return _silu(a) * b


def _f_merge(gab, ya, yb, bgab):
    g = gab + bgab
    return jax.nn.sigmoid(g[:, :D]) * ya + jax.nn.sigmoid(g[:, D:]) * yb


def _f_lnsilu(z, g, b):
    return _silu(_ln(z, g, b))


def _f_mixa(uv, buv, g, b):
    t = uv + buv
    return _gelu(t[:, :D]), _ln(_gelu(t[:, D:]), g, b)


def _col(t):
    return t if isinstance(t, tuple) else (t, t.shape[-1], 0)


def _rowwise_fwd(name, fn, rows, params, outs, tr):
    rows = [_col(r) for r in rows]
    params = [_col(p) for p in params]
    n_rows = rows[0][0].shape[0]
    nin = len(rows) + len(params)

    def body(*refs):
        res = fn(*[r[...].astype(F32) for r in refs[:nin]])
        res = res if isinstance(res, tuple) else (res,)
        for o, v in zip(refs[nin:], res, strict=True):
            o[...] = v.astype(o.dtype)

    return pl.pallas_call(
        body, name=name, grid=(n_rows // tr,),
        in_specs=[pl.BlockSpec((tr, w), functools.partial(lambda cb, i: (i, cb), cb)) for _, w, cb in rows]
        + [pl.BlockSpec((1, w), functools.partial(lambda cb, i: (0, cb), cb)) for _, w, cb in params],
        out_specs=[pl.BlockSpec((tr, w), lambda i: (i, 0)) for w, _ in outs],
        out_shape=[jax.ShapeDtypeStruct((n_rows, w), dt) for w, dt in outs],
        compiler_params=_cparams("parallel"),
    )(*[r[0] for r in rows], *[p[0] for p in params])


def _rowwise_bwd(name, fn, rows, params, cots, row_grads, tr):
    rows = [_col(r) for r in rows]
    params = [_col(p) for p in params]
    cots = [_col(t) for t in cots]
    n_rows = rows[0][0].shape[0]
    nr, npar, nc = len(rows), len(params), len(cots)
    nin = nr + npar + nc
    n_rg = sum(dt is not None for dt in row_grads)

    def body(*refs):
        i = pl.program_id(0)
        prim = [r[...].astype(F32) for r in refs[:nr + npar]]
        ct = tuple(r[...].astype(F32) for r in refs[nr + npar:nin])
        _, vjp = jax.vjp(fn, *prim)
        g = vjp(ct if nc > 1 else ct[0])
        outs = refs[nin:]
        oi = 0
        for j, dt in enumerate(row_grads):
            if dt is not None:
                outs[oi][...] = g[j].astype(dt)
                oi += 1
        for j in range(npar):
            acc = outs[n_rg + j]

            @pl.when(i == 0)
            def _(acc=acc):
                acc[...] = jnp.zeros(acc.shape, F32)

            acc[...] += g[nr + j]

    res = pl.pallas_call(
        body, name=name, grid=(n_rows // tr,),
        in_specs=[pl.BlockSpec((tr, w), functools.partial(lambda cb, i: (i, cb), cb)) for _, w, cb in rows]
        + [pl.BlockSpec((1, w), functools.partial(lambda cb, i: (0, cb), cb)) for _, w, cb in params]
        + [pl.BlockSpec((tr, w), functools.partial(lambda cb, i: (i, cb), cb)) for _, w, cb in cots],
        out_specs=[pl.BlockSpec((tr, rows[j][1]), lambda i: (i, 0)) for j, dt in enumerate(row_grads) if dt is not None]
        + [pl.BlockSpec((1, w), lambda i: (0, 0)) for _, w, _ in params],
        out_shape=[jax.ShapeDtypeStruct((n_rows, rows[j][1]), dt) for j, dt in enumerate(row_grads) if dt is not None]
        + [jax.ShapeDtypeStruct((1, w), F32) for _, w, _ in params],
        compiler_params=_cparams("arbitrary"),
    )(*[r[0] for r in rows], *[p[0] for p in params], *[t[0] for t in cots])
    return res[:n_rg], res[n_rg:]


def _final(name, x, f, tgt, g, nw, tr):
    n_rows = x.shape[0]

    def body(x_ref, f_ref, t_ref, g_ref, nw_ref, dx_ref, df_ref, dg_ref, dnw_ref, loss_ref):
        i = pl.program_id(0)
        tg = t_ref[...]

        def fn(xv, fv, gv, nwv):
            e = _rms(xv + (0.5 * gv) * fv, nwv) - tg
            return 0.5 * jnp.mean(e * e, axis=-1, keepdims=True)

        per_row, vjp = jax.vjp(fn, x_ref[...], f_ref[...], g_ref[...], nw_ref[...])
        dx, df, dg, dnw = vjp(jnp.ones_like(per_row))
        dx_ref[...] = dx
        df_ref[...] = df.astype(df_ref.dtype)

        @pl.when(i == 0)
        def _():
            dg_ref[...] = jnp.zeros(dg_ref.shape, F32)
            dnw_ref[...] = jnp.zeros(dnw_ref.shape, F32)
            loss_ref[...] = jnp.zeros(loss_ref.shape, F32)

        dg_ref[...] += dg
        dnw_ref[...] += dnw
        loss_ref[...] += jnp.broadcast_to(jnp.sum(per_row, axis=0, keepdims=True), loss_ref.shape)

    row = pl.BlockSpec((tr, D), lambda i: (i, 0))
    par = pl.BlockSpec((1, D), lambda i: (0, 0))
    return pl.pallas_call(
        body, name=name, grid=(n_rows // tr,),
        in_specs=[row, row, row, par, par],
        out_specs=[row, row, par, par, pl.BlockSpec((8, 128), lambda i: (0, 0))],
        out_shape=[jax.ShapeDtypeStruct((n_rows, D), F32), jax.ShapeDtypeStruct((n_rows, D), BF16),
                   jax.ShapeDtypeStruct((1, D), F32), jax.ShapeDtypeStruct((1, D), F32),
                   jax.ShapeDtypeStruct((8, 128), F32)],
        compiler_params=_cparams("arbitrary"),
    )(x, f, tgt, g, nw)


def _tile(dim, pref):
    t = min(dim, pref)
    while dim % t:
        t //= 2
    return t


def _matmul(name, a, b, kind, col, out_dtype, add=None):
    if kind == "wgrad":
        m, kg = a.shape
        ng = b.shape[1]
        r, c = (kg, ng // NCHIP) if col else (kg // NCHIP, ng)
        tm, tn, tk = _tile(r, 1024), _tile(c, 1024), _tile(m, 512)
        grid = (kg // tm, ng // tn, m // tk)
        a_spec = pl.BlockSpec((tk, tm), lambda i, j, k: (k, i))
        b_spec = pl.BlockSpec((tk, tn), lambda i, j, k: (k, j))
        if col:
            o_spec = pl.BlockSpec((None, tm, tn), lambda i, j, k: (j // (c // tn), i, j % (c // tn)))
        else:
            o_spec = pl.BlockSpec((None, tm, tn), lambda i, j, k: (i // (r // tm), i % (r // tm), j))
        out_shape = jax.ShapeDtypeStruct((NCHIP, r, c), F32)
        dims = TN
    else:
        _, r, c = b.shape
        m = a.shape[0]
        kg, ng = (r, NCHIP * c) if col else (NCHIP * r, c)
        tm = _tile(m, 1024)
        if kind == "fwd":
            tn, tk = _tile(c, 1024), _tile(r, 512)
            grid = (m // tm, ng // tn, kg // tk)
            a_spec = pl.BlockSpec((tm, tk), lambda i, j, k: (i, k))
            if col:
                b_spec = pl.BlockSpec((None, tk, tn), lambda i, j, k: (j // (c // tn), k, j % (c // tn)))
            else:
                b_spec = pl.BlockSpec((None, tk, tn), lambda i, j, k: (k // (r // tk), k % (r // tk), j))
            out_shape = jax.ShapeDtypeStruct((m, ng), out_dtype)
            dims = NN
        else:
            tn, tk = _tile(r, 1024), _tile(c, 512)
            grid = (m // tm, kg // tn, ng // tk)
            a_spec = pl.BlockSpec((tm, tk), lambda i, j, k: (i, k))
            if col:
                b_spec = pl.BlockSpec((None, tn, tk), lambda i, j, k: (k // (c // tk), j, k % (c // tk)))
            else:
                b_spec = pl.BlockSpec((None, tn, tk), lambda i, j, k: (j // (r // tn), j % (r // tn), k))
            out_shape = jax.ShapeDtypeStruct((m, kg), out_dtype)
            dims = NT
        o_spec = pl.BlockSpec((tm, tn), lambda i, j, k: (i, j))
    nk = grid[2]
    has_add = add is not None

    def body(*refs):
        a_ref, b_ref = refs[0], refs[1]
        o_ref, acc_ref = refs[-2], refs[-1]
        k = pl.program_id(2)

        @pl.when(k == 0)
        def _():
            acc_ref[...] = jnp.zeros(acc_ref.shape, F32)

        acc_ref[...] += lax.dot_general(a_ref[...].astype(BF16), b_ref[...].astype(BF16), dims,
                                        preferred_element_type=F32)

        @pl.when(k == nk - 1)
        def _():
            r_ = acc_ref[...]
            if has_add:
                r_ = r_ + refs[2][...]
            o_ref[...] = r_.astype(o_ref.dtype)

    in_specs = [a_spec, b_spec] + ([pl.BlockSpec((tm, tn), lambda i, j, k: (i, j))] if has_add else [])
    args = (a, b) + ((add,) if has_add else ())
    return pl.pallas_call(
        body, name=name, grid=grid, in_specs=in_specs, out_specs=o_spec, out_shape=out_shape,
        scratch_shapes=[pltpu.VMEM((tm, tn), F32)],
        compiler_params=_cparams("parallel", "parallel", "arbitrary"),
    )(*args)


def _ffn_up(name, h, wg, wu):
    m, k = h.shape
    _, _, c = wg.shape
    n = NCHIP * c
    tm, tn = _tile(m, 1024), _tile(c, 512)

    def body(h_ref, wg_ref, wu_ref, a_ref, b_ref, s_ref):
        hv = h_ref[...]
        a = jnp.dot(hv, wg_ref[...], preferred_element_type=F32)
        b = jnp.dot(hv, wu_ref[...], preferred_element_type=F32)
        a_ref[...] = a
        b_ref[...] = b
        s_ref[...] = (_silu(a) * b).astype(s_ref.dtype)

    w_spec = pl.BlockSpec((None, k, tn), lambda i, j: (j // (c // tn), 0, j % (c // tn)))
    o_spec = pl.BlockSpec((tm, tn), lambda i, j: (i, j))
    return pl.pallas_call(
        body, name=name, grid=(m // tm, n // tn),
        in_specs=[pl.BlockSpec((tm, k), lambda i, j: (i, 0)), w_spec, w_spec],
        out_specs=[o_spec, o_spec, o_spec],
        out_shape=[jax.ShapeDtypeStruct((m, n), F32), jax.ShapeDtypeStruct((m, n), F32),
                   jax.ShapeDtypeStruct((m, n), BF16)],
        compiler_params=_cparams("parallel", "parallel"),
    )(h, wg, wu)


def _sgu_fwd(name, proj, b_in, lng, lnb, wm, bst, tr):
    n_rows = proj.shape[0]

    def body(uv_ref, buv_ref, g_ref, b_ref, wm_ref, bst_ref, ya_ref):
        u, vln = _f_mixa(uv_ref[...], buv_ref[...], g_ref[...], b_ref[...])
        vb = vln.astype(BF16)
        for ck in range(tr // CHUNK):
            rs = slice(ck * CHUNK, (ck + 1) * CHUNK)
            for h in range(HEADS):
                cs = slice(h * CHUNK, (h + 1) * CHUNK)
                vm = jnp.dot(wm_ref[h], vb[rs, cs], preferred_element_type=F32) + bst_ref[:, cs]
                ya_ref[rs, cs] = (u[rs, cs] * vm).astype(ya_ref.dtype)

    return pl.pallas_call(
        body, name=name, grid=(n_rows // tr,),
        in_specs=[pl.BlockSpec((tr, 2 * D), lambda i: (i, 0)), pl.BlockSpec((1, 2 * D), lambda i: (0, 0)),
                  pl.BlockSpec((1, D), lambda i: (0, 0)), pl.BlockSpec((1, D), lambda i: (0, 0)),
                  pl.BlockSpec((HEADS, CHUNK, CHUNK), lambda i: (0, 0, 0)),
                  pl.BlockSpec((CHUNK, D), lambda i: (0, 0))],
        out_specs=pl.BlockSpec((tr, D), lambda i: (i, 0)),
        out_shape=jax.ShapeDtypeStruct((n_rows, D), BF16),
        compiler_params=_cparams("parallel"),
    )(proj, b_in, lng, lnb, wm, bst)


def _sgu_bwd(name, proj, b_in, lng, lnb, wm, wmt, bst, dya, tr):
    n_rows = proj.shape[0]

    def body(uv_ref, buv_ref, g_ref, b_ref, wm_ref, wmt_ref, bst_ref, dya_ref,
             duv_ref, dbuv_ref, dg_ref, db_ref, dws_ref, dbs_ref, du_s, dvln_s):
        i = pl.program_id(0)

        @pl.when(i == 0)
        def _():
            dbuv_ref[...] = jnp.zeros(dbuv_ref.shape, F32)
            dg_ref[...] = jnp.zeros(dg_ref.shape, F32)
            db_ref[...] = jnp.zeros(db_ref.shape, F32)
            dws_ref[...] = jnp.zeros(dws_ref.shape, F32)
            dbs_ref[...] = jnp.zeros(dbs_ref.shape, F32)

        (u, vln), vjp = jax.vjp(_f_mixa, uv_ref[...], buv_ref[...], g_ref[...], b_ref[...])
        vb = vln.astype(BF16)
        dya_v = dya_ref[...].astype(F32)
        tpos = lax.broadcasted_iota(jnp.int32, (CHUNK, CHUNK), 0)
        spos = lax.broadcasted_iota(jnp.int32, (CHUNK, CHUNK), 1)
        causal = (tpos >= spos).astype(F32)
        for ck in range(tr // CHUNK):
            rs = slice(ck * CHUNK, (ck + 1) * CHUNK)
            for h in range(HEADS):
                cs = slice(h * CHUNK, (h + 1) * CHUNK)
                vbh = vb[rs, cs]
                vm = jnp.dot(wm_ref[h], vbh, preferred_element_type=F32) + bst_ref[:, cs]
                dyb = dya_v[rs, cs]
                du_s[rs, cs] = dyb * vm
                dvm = dyb * u[rs, cs]
                dvmb = dvm.astype(BF16)
                dvln_s[rs, cs] = jnp.dot(wmt_ref[h], dvmb, preferred_element_type=F32)
                dws_ref[h] += causal * lax.dot_general(dvmb, vbh, NT, preferred_element_type=F32)
                dbs_ref[:, cs] += jnp.broadcast_to(jnp.sum(dvm, axis=1, keepdims=True), (CHUNK, CHUNK))
        duv, dbuv, dg, db = vjp((du_s[...], dvln_s[...]))
        duv_ref[...] = duv.astype(duv_ref.dtype)
        dbuv_ref[...] += dbuv
        dg_ref[...] += dg
        db_ref[...] += db

    par = pl.BlockSpec((1, D), lambda i: (0, 0))
    par2 = pl.BlockSpec((1, 2 * D), lambda i: (0, 0))
    w_spec = pl.BlockSpec((HEADS, CHUNK, CHUNK), lambda i: (0, 0, 0))
    b_spec = pl.BlockSpec((CHUNK, D), lambda i: (0, 0))
    return pl.pallas_call(
        body, name=name, grid=(n_rows // tr,),
        in_specs=[pl.BlockSpec((tr, 2 * D), lambda i: (i, 0)), par2, par, par, w_spec, w_spec, b_spec,
                  pl.BlockSpec((tr, D), lambda i: (i, 0))],
        out_specs=[pl.BlockSpec((tr, 2 * D), lambda i: (i, 0)), par2, par, par, w_spec, b_spec],
        out_shape=[jax.ShapeDtypeStruct((n_rows, 2 * D), BF16), jax.ShapeDtypeStruct((1, 2 * D), F32),
                   jax.ShapeDtypeStruct((1, D), F32), jax.ShapeDtypeStruct((1, D), F32),
                   jax.ShapeDtypeStruct((HEADS, CHUNK, CHUNK), F32), jax.ShapeDtypeStruct((CHUNK, D), F32)],
        scratch_shapes=[pltpu.VMEM((tr, D), F32), pltpu.VMEM((tr, D), F32)],
        compiler_params=_cparams("arbitrary"),
    )(proj, b_in, lng, lnb, wm, wmt, bst, dya)


CT = 256
RB = 128
CV0 = 2 * D // CT
CG0 = 3 * D // CT


def _shifted(win, sh):
    if sh == 0:
        return win[0:RB]
    return pltpu.roll(win, RB + HALO - sh, 0)[0:RB]


def _conv_fwd(name, proj, b_in, w, cb):
    n_rows = proj.shape[0]

    def body(cv_ref, cg_ref, bcv_ref, bcg_ref, w_ref, cb_ref, z1_ref, zp_ref):
        zp_ref[0:HALO, :] = jnp.zeros((HALO, CT), F32)
        zp_ref[HALO:, :] = (cv_ref[...] + bcv_ref[...]) * jax.nn.sigmoid(cg_ref[...] + bcg_ref[...])

        def blk(rb, carry):
            base = pl.multiple_of(rb * RB, RB)
            win = zp_ref[pl.ds(base, RB + HALO), :]
            acc = jnp.broadcast_to(cb_ref[...], (RB, CT))
            for k in range(KCONV):
                acc = acc + w_ref[k:k + 1, :] * _shifted(win, k + 2)
            z1_ref[pl.ds(base, RB), :] = acc
            return carry

        lax.fori_loop(0, n_rows // RB, blk, 0)

    return pl.pallas_call(
        body, name=name, grid=(D // CT,),
        in_specs=[pl.BlockSpec((n_rows, CT), lambda j: (0, CV0 + j)), pl.BlockSpec((n_rows, CT), lambda j: (0, CG0 + j)),
                  pl.BlockSpec((1, CT), lambda j: (0, CV0 + j)), pl.BlockSpec((1, CT), lambda j: (0, CG0 + j)),
                  pl.BlockSpec((HALO, CT), lambda j: (0, j)), pl.BlockSpec((1, CT), lambda j: (0, j))],
        out_specs=pl.BlockSpec((n_rows, CT), lambda j: (0, j)),
        out_shape=jax.ShapeDtypeStruct((n_rows, D), F32),
        scratch_shapes=[pltpu.VMEM((n_rows + HALO, CT), F32)],
        compiler_params=_cparams("parallel"),
    )(proj, proj, b_in, b_in, w, cb)


def _conv_bwd(name, dz1, proj, b_in, w):
    n_rows = proj.shape[0]

    def body(dz_ref, cv_ref, cg_ref, bcv_ref, bcg_ref, w_ref, dcv_ref, dcg_ref, sm_ref, zp_ref, dzp_ref):
        cvb = cv_ref[...] + bcv_ref[...]
        sg = jax.nn.sigmoid(cg_ref[...] + bcg_ref[...])
        zp_ref[0:HALO, :] = jnp.zeros((HALO, CT), F32)
        zp_ref[HALO:, :] = cvb * sg
        dz = dz_ref[...]
        dzp_ref[0:n_rows, :] = dz
        dzp_ref[n_rows:, :] = jnp.zeros((HALO, CT), F32)
        sm_ref[...] = jnp.zeros(sm_ref.shape, F32)
        sm_ref[32:33, :] = jnp.sum(dz, axis=0, keepdims=True)

        def blk(rb, carry):
            base = pl.multiple_of(rb * RB, RB)
            dwin = dzp_ref[pl.ds(base, RB + HALO), :]
            zwin = zp_ref[pl.ds(base, RB + HALO), :]
            dzb = dwin[0:RB]
            acc = jnp.zeros((RB, CT), F32)
            for k in range(KCONV):
                acc = acc + w_ref[k:k + 1, :] * _shifted(dwin, KCONV - 1 - k)
                sm_ref[k:k + 1, :] += jnp.sum(dzb * _shifted(zwin, k + 2), axis=0, keepdims=True)
            dzp_ref[pl.ds(base, RB), :] = acc
            return carry

        lax.fori_loop(0, n_rows // RB, blk, 0)
        dz0 = dzp_ref[0:n_rows, :]
        dcv = dz0 * sg
        dcg = dz0 * cvb * (sg * (1.0 - sg))
        dcv_ref[...] = dcv.astype(dcv_ref.dtype)
        dcg_ref[...] = dcg.astype(dcg_ref.dtype)
        sm_ref[33:34, :] = jnp.sum(dcv, axis=0, keepdims=True)
        sm_ref[34:35, :] = jnp.sum(dcg, axis=0, keepdims=True)

    col = pl.BlockSpec((n_rows, CT), lambda j: (0, j))
    return pl.pallas_call(
        body, name=name, grid=(D // CT,),
        in_specs=[col, pl.BlockSpec((n_rows, CT), lambda j: (0, CV0 + j)), pl.BlockSpec((n_rows, CT), lambda j: (0, CG0 + j)),
                  pl.BlockSpec((1, CT), lambda j: (0, CV0 + j)), pl.BlockSpec((1, CT), lambda j: (0, CG0 + j)),
                  pl.BlockSpec((HALO, CT), lambda j: (0, j))],
        out_specs=[col, col, pl.BlockSpec((40, CT), lambda j: (0, j))],
        out_shape=[jax.ShapeDtypeStruct((n_rows, D), BF16), jax.ShapeDtypeStruct((n_rows, D), BF16),
                   jax.ShapeDtypeStruct((40, D), F32)],
        scratch_shapes=[pltpu.VMEM((n_rows + HALO, CT), F32), pltpu.VMEM((n_rows + HALO, CT), F32)],
        compiler_params=_cparams("parallel"),
    )(dz1, proj, proj, b_in, b_in, w)


ADA_TN = 768


def _split_bf16(v):
    hi = v.astype(BF16)
    return hi, (v - hi.astype(F32)).astype(BF16)


def _ada_fwd(name, c_all, w, b):
    n = w.shape[1]

    def body(c_ref, w_ref, b_ref, o_ref):
        ch, cl = _split_bf16(_silu(c_ref[...]))
        wh, wl = _split_bf16(w_ref[...])
        acc = jnp.dot(ch, wl, preferred_element_type=F32) + jnp.dot(cl, wh, preferred_element_type=F32)
        o_ref[...] = acc + jnp.dot(ch, wh, preferred_element_type=F32) + b_ref[...]

    return pl.pallas_call(
        body, name=name, grid=(n // ADA_TN,),
        in_specs=[pl.BlockSpec((8, D), lambda j: (0, 0)), pl.BlockSpec((D, ADA_TN), lambda j: (0, j)),
                  pl.BlockSpec((1, ADA_TN), lambda j: (0, j))],
        out_specs=pl.BlockSpec((8, ADA_TN), lambda j: (0, j)),
        out_shape=jax.ShapeDtypeStruct((8, n), F32),
        compiler_params=_cparams("parallel"),
    )(c_all, w, b)


def _ada_wgrad(name, c_all_t, dmod):
    n = dmod.shape[1]

    def body(ct_ref, dm_ref, o_ref):
        ca = _silu(ct_ref[...])
        acc = ca[:, 0:1] * dm_ref[0:1, :]
        for r in range(1, 8):
            acc = acc + ca[:, r:r + 1] * dm_ref[r:r + 1, :]
        o_ref[...] = acc

    return pl.pallas_call(
        body, name=name, grid=(n // ADA_TN,),
        in_specs=[pl.BlockSpec((D, 8), lambda j: (0, 0)), pl.BlockSpec((8, ADA_TN), lambda j: (0, j))],
        out_specs=pl.BlockSpec((D, ADA_TN), lambda j: (0, j)),
        out_shape=jax.ShapeDtypeStruct((D, n), F32),
        compiler_params=_cparams("parallel"),
    )(c_all_t, dmod)


def _adamw(name, w, g, m, v, tr):
    rows, cols = w.shape

    def body(w_ref, g_ref, m_ref, v_ref, d_ref, nm_ref, nv_ref):
        gv = g_ref[...]
        nm = ADAM_B1 * m_ref[...] + (1.0 - ADAM_B1) * gv
        nv = ADAM_B2 * v_ref[...] + (1.0 - ADAM_B2) * (gv * gv)
        m_hat = nm / (1.0 - ADAM_B1 ** ADAM_STEP)
        v_hat = nv / (1.0 - ADAM_B2 ** ADAM_STEP)
        d_ref[...] = -ADAM_LR * (m_hat / (jnp.sqrt(v_hat) + ADAM_EPS) + ADAM_WD * w_ref[...])
        nm_ref[...] = nm
        nv_ref[...] = nv

    spec = pl.BlockSpec((tr, cols), lambda i: (i, 0))
    sds = jax.ShapeDtypeStruct((rows, cols), F32)
    return pl.pallas_call(
        body, name=name, grid=(rows // tr,), in_specs=[spec] * 4, out_specs=[spec] * 3, out_shape=[sds] * 3,
        compiler_params=_cparams("parallel"),
    )(w, g, m, v)


def _sum8(name, stacked, tr):
    n, rows, cols = stacked.shape

    def body(s_ref, o_ref):
        acc = s_ref[0]
        for r in range(1, n):
            acc = acc + s_ref[r]
        o_ref[...] = acc

    return pl.pallas_call(
        body, name=name, grid=(rows // tr,),
        in_specs=[pl.BlockSpec((n, tr, cols), lambda i: (0, i, 0))],
        out_specs=pl.BlockSpec((tr, cols), lambda i: (i, 0)),
        out_shape=jax.ShapeDtypeStruct((rows, cols), F32),
        compiler_params=_cparams("parallel"),
    )(stacked)


def _pair_add(name, g5, other, c_idx, tr):
    nq, _, rows, cols = g5.shape

    def body(c_ref, g_ref, o_ref, p_ref):
        p_ref[...] = (g_ref[...] + o_ref[...]).astype(p_ref.dtype)

    return pl.pallas_call(
        body, name=name,
        grid_spec=pltpu.PrefetchScalarGridSpec(
            num_scalar_prefetch=1, grid=(nq, rows // tr),
            in_specs=[pl.BlockSpec((None, None, tr, cols), lambda qi, i, cr: (qi, cr[0], i, 0)),
                      pl.BlockSpec((None, tr, cols), lambda qi, i, cr: (qi, i, 0))],
            out_specs=pl.BlockSpec((None, tr, cols), lambda qi, i, cr: (qi, i, 0))),
        out_shape=jax.ShapeDtypeStruct((nq, rows, cols), BF16),
        compiler_params=_cparams("parallel", "parallel"),
    )(c_idx, g5, other)


def _chip_add(name, p, recv, q_idx, tr):
    _, rows, cols = p.shape

    def body(q_ref, p_ref, r_ref, o_ref):
        acc = p_ref[...].astype(F32)
        for k in range(NCHIP - 1):
            acc = acc + r_ref[k].astype(F32)
        o_ref[...] = acc

    return pl.pallas_call(
        body, name=name,
        grid_spec=pltpu.PrefetchScalarGridSpec(
            num_scalar_prefetch=1, grid=(rows // tr,),
            in_specs=[pl.BlockSpec((None, tr, cols), lambda i, qr: (qr[0], i, 0)),
                      pl.BlockSpec((NCHIP - 1, tr, cols), lambda i, qr: (0, i, 0))],
            out_specs=pl.BlockSpec((tr, cols), lambda i, qr: (i, 0))),
        out_shape=jax.ShapeDtypeStruct((rows, cols), F32),
        compiler_params=_cparams("parallel"),
    )(q_idx, p, recv)


def _allgather8(name, blk):
    m_per, n = blk.shape

    def body(x_ref, out_ref, send_sems, recv_sems, local_sem):
        x, y, c = lax.axis_index("x"), lax.axis_index("y"), lax.axis_index("c")
        me, sibling = (x, y, c), (x, y, 1 - c)
        chips = [(1 - x, y), (x, 1 - y), (1 - x, 1 - y)]

        def rows(px, py, pc):
            return out_ref.at[pl.ds((4 * px + 2 * py + pc) * m_per, m_per), :]

        def copy(k, block, to, src=None):
            return pltpu.make_async_remote_copy(
                src_ref=rows(*block) if src is None else src, dst_ref=rows(*block),
                send_sem=send_sems.at[k], recv_sem=recv_sems.at[k], device_id=to, device_id_type=MESH)

        mine = pltpu.make_async_copy(x_ref, rows(*me), local_sem)
        mine.start()
        first = [copy(0, me, sibling, src=x_ref)]
        first += [copy(1 + j, me, (*chip, c), src=x_ref) for j, chip in enumerate(chips)]
        for cp in first:
            cp.start()
        passed = [copy(4 + j, (*chip, c), sibling) for j, chip in enumerate(chips)]
        for j, chip in enumerate(chips):
            copy(1 + j, (*chip, c), me).wait_recv()
            passed[j].start()
        copy(0, sibling, me).wait_recv()
        for j, chip in enumerate(chips):
            copy(4 + j, (*chip, 1 - c), me).wait_recv()
        for cp in first + passed:
            cp.wait_send()
        mine.wait()

    return pl.pallas_call(
        body, name=name,
        out_shape=jax.ShapeDtypeStruct((8 * m_per, n), blk.dtype),
        in_specs=[pl.BlockSpec(memory_space=pltpu.VMEM)],
        out_specs=pl.BlockSpec(memory_space=pltpu.VMEM),
        scratch_shapes=[pltpu.SemaphoreType.DMA((7,)), pltpu.SemaphoreType.DMA((7,)), pltpu.SemaphoreType.DMA],
        compiler_params=pltpu.CompilerParams(vmem_limit_bytes=VMEM_LIMIT),
    )(blk)


def _exchange(name, srcs, out_shapes, n_local, n_remote, plan):
    ni, no = len(srcs), len(out_shapes)

    def body(*refs):
        ins, outs = refs[:ni], refs[ni:ni + no]
        send_sems, recv_sems, local_sems = refs[ni + no:]
        x, y, c = lax.axis_index("x"), lax.axis_index("y"), lax.axis_index("c")
        local, remote = plan(ins, outs, x, y, c)
        assert len(local) == n_local and len(remote) == n_remote
        lcs = [pltpu.make_async_copy(s, d, local_sems.at[i]) for i, (s, d) in enumerate(local)]
        for cp in lcs:
            cp.start()
        rcs = [pltpu.make_async_remote_copy(src_ref=s, dst_ref=d, send_sem=send_sems.at[i], recv_sem=recv_sems.at[i],
                                            device_id=peer, device_id_type=MESH)
               for i, (s, d, peer, _) in enumerate(remote)]
        for cp in rcs:
            cp.start()
        for i, (s, _, peer, lands) in enumerate(remote):
            pltpu.make_async_remote_copy(src_ref=s, dst_ref=lands, send_sem=send_sems.at[i], recv_sem=recv_sems.at[i],
                                         device_id=peer, device_id_type=MESH).wait_recv()
        for cp in rcs:
            cp.wait_send()
        for cp in lcs:
            cp.wait()

    any_spec = pl.BlockSpec(memory_space=pl.ANY)
    return pl.pallas_call(
        body, name=name, out_shape=out_shapes,
        in_specs=[any_spec] * ni, out_specs=[any_spec] * no,
        scratch_shapes=[pltpu.SemaphoreType.DMA((n_remote,)), pltpu.SemaphoreType.DMA((n_remote,)),
                        pltpu.SemaphoreType.DMA((max(n_local, 1),))],
    )(*srcs)


_CHIP_FLIPS = ((0, 1), (1, 0), (1, 1))


def _flip(v, f):
    return 1 - v if f else v


def _gather_chips(name, shards):
    n = len(shards)

    def plan(ins, outs, x, y, c):
        q = 2 * x + y
        local = [(ins[a], outs[a].at[q]) for a in range(n)]
        remote = []
        for a in range(n):
            for fx, fy in _CHIP_FLIPS:
                px, py = _flip(x, fx), _flip(y, fy)
                remote.append((ins[a], outs[a].at[q], (px, py, c), outs[a].at[2 * px + py]))
        return local, remote

    return _exchange(name, shards, [jax.ShapeDtypeStruct((NCHIP,) + s.shape, s.dtype) for s in shards],
                     n, 3 * n, plan)


def _reduce_scatter(grads, c_idx, q_idx):
    n = len(grads)
    g5 = [g.reshape(NCHIP, 2, g.shape[1] // 2, g.shape[2]) for g in grads]
    half = [(g.shape[2], g.shape[3]) for g in g5]

    def plan1(ins, outs, x, y, c):
        sib = (x, y, 1 - c)
        return [], [(ins[a].at[:, 1 - c], outs[a], sib, outs[a]) for a in range(n)]

    got1 = _exchange("rs_pair", g5, [jax.ShapeDtypeStruct((NCHIP,) + h, F32) for h in half], 0, n, plan1)
    part = [_pair_add(f"rs_pair_add{a}", g5[a], got1[a], c_idx, _tile(half[a][0], 256)) for a in range(n)]

    def plan2(ins, outs, x, y, c):
        remote = []
        for a in range(n):
            for k, (fx, fy) in enumerate(_CHIP_FLIPS):
                px, py = _flip(x, fx), _flip(y, fy)
                remote.append((ins[a].at[2 * px + py], outs[a].at[k], (px, py, c), outs[a].at[k]))
        return [], remote

    got2 = _exchange("rs_chips", part, [jax.ShapeDtypeStruct((NCHIP - 1,) + h, BF16) for h in half], 0, 3 * n, plan2)
    red = [_chip_add(f"rs_chip_add{a}", part[a], got2[a], q_idx, _tile(half[a][0], 256)) for a in range(n)]

    def plan3(ins, outs, x, y, c):
        sib = (x, y, 1 - c)
        local = [(ins[a], outs[a].at[c]) for a in range(n)]
        return local, [(ins[a], outs[a].at[c], sib, outs[a].at[1 - c]) for a in range(n)]

    full = _exchange("rs_share", red, [jax.ShapeDtypeStruct((2,) + h, F32) for h in half], n, n, plan3)
    return [f.reshape(2 * h[0], h[1]) for f, h in zip(full, half)]


def _to_bf16(name, w):
    return _rowwise_fwd(name, lambda v: v, [w], [], [(w.shape[1], BF16)], _tile(w.shape[0], 256))[0]


def kernel(x, c, ada_w, ada_b, norm_ffn1, ffn1_w_gate, ffn1_w_up, ffn1_w_down, norm_mix, mix_w_in, mix_b_in, sgu_ln_g, sgu_ln_b, sgu_w_s, sgu_b_s, conv_w, conv_b, conv_ln_g, conv_ln_b, w_branch_a, w_branch_b, w_out, norm_ffn2, ffn2_w_gate, ffn2_w_up, ffn2_w_down, norm_final, loss_target, m_ada_w, m_ada_b, m_norm_ffn1, m_ffn1_w_gate, m_ffn1_w_up, m_ffn1_w_down, m_norm_mix, m_mix_w_in, m_mix_b_in, m_sgu_ln_g, m_sgu_ln_b, m_sgu_w_s, m_sgu_b_s, m_conv_w, m_conv_b, m_conv_ln_g, m_conv_ln_b, m_w_branch_a, m_w_branch_b, m_w_out, m_norm_ffn2, m_ffn2_w_gate, m_ffn2_w_up, m_ffn2_w_down, m_norm_final, v_ada_w, v_ada_b, v_norm_ffn1, v_ffn1_w_gate, v_ffn1_w_up, v_ffn1_w_down, v_norm_mix, v_mix_w_in, v_mix_b_in, v_sgu_ln_g, v_sgu_ln_b, v_sgu_w_s, v_sgu_b_s, v_conv_w, v_conv_b, v_conv_ln_g, v_conv_ln_b, v_w_branch_a, v_w_branch_b, v_w_out, v_norm_ffn2, v_ffn2_w_gate, v_ffn2_w_up, v_ffn2_w_down, v_norm_final):
    xi, yi, ci = lax.axis_index("x"), lax.axis_index("y"), lax.axis_index("c")
    q = 2 * xi + yi
    dev = 4 * xi + 2 * yi + ci
    c_idx = jnp.reshape(ci, (1,)).astype(jnp.int32)
    q_idx = jnp.reshape(q, (1,)).astype(jnp.int32)
    x0 = x[0]
    tgt = loss_target[0]
    seq = x0.shape[0]
    n_ada = ada_w.shape[2]

    c_all = _allgather8("ag_c", jnp.pad(c, ((0, 7), (0, 0))))[::8]
    ada_b_mine = lax.dynamic_slice(ada_b, (0, q * n_ada), (1, n_ada))
    mod_part = _ada_fwd("ada_fwd", c_all, ada_w[0], ada_b_mine)

    big = dict(g1=ffn1_w_gate, u1=ffn1_w_up, d1=ffn1_w_down, win=mix_w_in, wa=w_branch_a, wb=w_branch_b, wo=w_out,
               g2=ffn2_w_gate, u2=ffn2_w_up, d2=ffn2_w_down)
    big_names = list(big)
    shards = [_to_bf16(f"cast_{k}", big[k][0]) for k in big_names]
    gathered = _gather_chips("gather_w", shards + [mod_part, jnp.pad(conv_w[0], ((0, 1), (0, 0)))])
    W = dict(zip(big_names, gathered))
    mod_all = jnp.transpose(gathered[-2], (1, 0, 2)).reshape(8, N_MOD * D)
    mod = lax.dynamic_slice(mod_all, (dev, 0), (1, N_MOD * D))
    sh1, sc1, g1, sh2, sc2, g2, sh3, sc3, g3 = [mod[:, k * D:(k + 1) * D] for k in range(N_MOD)]
    convw = jnp.transpose(gathered[-1], (1, 0, 2)).reshape(HALO, D)

    causal = jnp.tril(jnp.ones((CHUNK, CHUNK), bool))
    wm_f = jnp.where(causal[None], sgu_w_s[0], 0.0)
    wm = wm_f.astype(BF16)
    wmt = jnp.swapaxes(wm_f, 1, 2).astype(BF16)
    bst = jnp.repeat(jnp.transpose(sgu_b_s[0]), CHUNK, axis=1)
    b_in = mix_b_in

    (h1,) = _rowwise_fwd("pre1", _f_pre, [x0], [norm_ffn1, sc1, sh1], [(D, BF16)], 256)
    a1, b1, s1 = _ffn_up("up1", h1, W["g1"], W["u1"])
    f1 = _matmul("down1", s1, W["d1"], "fwd", False, F32)
    x1, h2 = _rowwise_fwd("res1", _f_res_pre(0.5), [x0, f1], [g1, norm_mix, sc2, sh2], [(D, F32), (D, BF16)], 256)
    proj = _matmul("in_proj", h2, W["win"], "fwd", True, F32)
    ya = _sgu_fwd("sgu", proj, b_in, sgu_ln_g, sgu_ln_b, wm, bst, 256)
    z1 = _conv_fwd("conv", proj, b_in, convw, conv_b)
    (z3,) = _rowwise_fwd("conv_ln", _f_lnsilu, [z1], [conv_ln_g, conv_ln_b], [(D, BF16)], 256)
    y_a = _matmul("branch_a", ya, W["wa"], "fwd", False, F32)
    y_b = _matmul("branch_b", z3, W["wb"], "fwd", False, F32)
    (merged,) = _rowwise_fwd("merge", _f_merge, [(proj, 2 * D, 2), y_a, y_b], [(b_in, 2 * D, 2)], [(D, BF16)], 256)
    y = _matmul("out_proj", merged, W["wo"], "fwd", False, F32)
    x2, h3 = _rowwise_fwd("res2", _f_res_pre(1.0), [x1, y], [g2, norm_ffn2, sc3, sh3], [(D, F32), (D, BF16)], 256)
    a3, b3, s3 = _ffn_up("up2", h3, W["g2"], W["u2"])
    f3 = _matmul("down2", s3, W["d2"], "fwd", False, F32)
    dx2, df3, dg3, dnwf, loss_blk = _final("final", x2, f3, tgt, g3, norm_final.reshape(1, D), 256)

    G = {}

    def ffn_bwd(tag, df, s, a, b, h, wg, wu, wd):
        ds = _matmul(f"d_down{tag}", df, wd, "dgrad", False, F32)
        G["d" + tag] = _matmul(f"w_down{tag}", s, df, "wgrad", False, F32)
        (da, db), _ = _rowwise_bwd(f"glu_b{tag}", _f_glu, [a, b], [], [ds], [BF16, BF16], 128)
        dh = _matmul(f"d_gate{tag}", da, wg, "dgrad", True, F32)
        dh = _matmul(f"d_up{tag}", db, wu, "dgrad", True, F32, add=dh)
        G["g" + tag] = _matmul(f"w_gate{tag}", h, da, "wgrad", True, F32)
        G["u" + tag] = _matmul(f"w_up{tag}", h, db, "wgrad", True, F32)
        return dh

    dh3 = ffn_bwd("2", df3, s3, a3, b3, h3, W["g2"], W["u2"], W["d2"])
    (dx1, dy), (dg2, dnw3, dsc3, dsh3) = _rowwise_bwd(
        "res2_b", _f_res_pre(1.0), [x1, y], [g2, norm_ffn2, sc3, sh3], [dx2, dh3], [F32, BF16], 256)
    dmerged = _matmul("d_out", dy, W["wo"], "dgrad", False, F32)
    G["wo"] = _matmul("w_out", merged, dy, "wgrad", False, F32)
    (dgab, dy_a, dy_b), (dbgab,) = _rowwise_bwd(
        "merge_b", _f_merge, [(proj, 2 * D, 2), y_a, y_b], [(b_in, 2 * D, 2)], [dmerged], [BF16, BF16, BF16], 256)
    dya = _matmul("d_a", dy_a, W["wa"], "dgrad", False, F32)
    dz3 = _matmul("d_b", dy_b, W["wb"], "dgrad", False, F32)
    G["wa"] = _matmul("w_a", ya, dy_a, "wgrad", False, F32)
    G["wb"] = _matmul("w_b", z3, dy_b, "wgrad", False, F32)
    (dz1,), (dclg, dclb) = _rowwise_bwd("conv_ln_b", _f_lnsilu, [z1], [conv_ln_g, conv_ln_b], [dz3], [F32], 256)
    dcv, dcg, conv_small = _conv_bwd("conv_b", dz1, proj, b_in, convw)
    duv, dbuv, dlng, dlnb, dws, dbs = _sgu_bwd("sgu_b", proj, b_in, sgu_ln_g, sgu_ln_b, wm, wmt, bst, dya, 256)
    dproj = jnp.concatenate([duv, dcv, dcg, dgab], axis=1)
    dh2 = _matmul("d_in", dproj, W["win"], "dgrad", True, F32)
    G["win"] = _matmul("w_in", h2, dproj, "wgrad", True, F32)
    (dx0, df1), (dg1, dnw2, dsc2, dsh2) = _rowwise_bwd(
        "res1_b", _f_res_pre(0.5), [x0, f1], [g1, norm_mix, sc2, sh2], [dx1, dh2], [F32, BF16], 256)
    dh1 = ffn_bwd("1", df1, s1, a1, b1, h1, W["g1"], W["u1"], W["d1"])
    (grad_x,), (dnw1, dsc1, dsh1) = _rowwise_bwd(
        "pre1_b", _f_pre_keep, [x0], [norm_ffn1, sc1, sh1], [dx0, dh1], [F32], 256)

    db_in = jnp.concatenate([dbuv, conv_small[33:34], conv_small[34:35], dbgab], axis=1)
    dbs_row = jnp.transpose(dbs[:, ::CHUNK]).reshape(1, D)
    loss_row = jnp.pad(loss_blk[0:1, :], ((0, 0), (0, D - 128)))
    small = jnp.concatenate([
        dnw1, dnw2, db_in.reshape(6, D), dlng, dlnb, dws.reshape(CHUNK, D), dbs_row, conv_small[0:KCONV],
        conv_small[32:33], dclg, dclb, dnw3, dnwf,
        dsh1, dsc1, dg1, dsh2, dsc2, dg2, dsh3, dsc3, dg3, loss_row, jnp.zeros((7, D), F32)], axis=0)
    n_small = small.shape[0]
    every = _allgather8("ag_small", small).reshape(8, n_small, D)
    tot = _sum8("sum_small", every, 64)
    r_mod = 1 + 1 + 6 + 1 + 1 + CHUNK + 1 + KCONV + 1 + 1 + 1 + 1 + 1
    loss = tot[r_mod + N_MOD, 0]
    dmod_all = every[:, r_mod:r_mod + N_MOD, :].reshape(8, N_MOD * D)
    dmod_mine = lax.dynamic_slice(dmod_all, (0, q * n_ada), (8, n_ada))
    g_ada_w = _ada_wgrad("ada_wgrad", jnp.transpose(c_all), dmod_mine)

    reduced = dict(zip(big_names, _reduce_scatter([G[k] for k in big_names], c_idx, q_idx)))

    given = dict(ada_w=(ada_w, m_ada_w, v_ada_w), ada_b=(ada_b, m_ada_b, v_ada_b),
                 norm_ffn1=(norm_ffn1, m_norm_ffn1, v_norm_ffn1),
                 ffn1_w_gate=(ffn1_w_gate, m_ffn1_w_gate, v_ffn1_w_gate), ffn1_w_up=(ffn1_w_up, m_ffn1_w_up, v_ffn1_w_up),
                 ffn1_w_down=(ffn1_w_down, m_ffn1_w_down, v_ffn1_w_down), norm_mix=(norm_mix, m_norm_mix, v_norm_mix),
                 mix_w_in=(mix_w_in, m_mix_w_in, v_mix_w_in), mix_b_in=(mix_b_in, m_mix_b_in, v_mix_b_in),
                 sgu_ln_g=(sgu_ln_g, m_sgu_ln_g, v_sgu_ln_g), sgu_ln_b=(sgu_ln_b, m_sgu_ln_b, v_sgu_ln_b),
                 sgu_w_s=(sgu_w_s, m_sgu_w_s, v_sgu_w_s), sgu_b_s=(sgu_b_s, m_sgu_b_s, v_sgu_b_s),
                 conv_w=(conv_w, m_conv_w, v_conv_w), conv_b=(conv_b, m_conv_b, v_conv_b),
                 conv_ln_g=(conv_ln_g, m_conv_ln_g, v_conv_ln_g), conv_ln_b=(conv_ln_b, m_conv_ln_b, v_conv_ln_b),
                 w_branch_a=(w_branch_a, m_w_branch_a, v_w_branch_a), w_branch_b=(w_branch_b, m_w_branch_b, v_w_branch_b),
                 w_out=(w_out, m_w_out, v_w_out), norm_ffn2=(norm_ffn2, m_norm_ffn2, v_norm_ffn2),
                 ffn2_w_gate=(ffn2_w_gate, m_ffn2_w_gate, v_ffn2_w_gate), ffn2_w_up=(ffn2_w_up, m_ffn2_w_up, v_ffn2_w_up),
                 ffn2_w_down=(ffn2_w_down, m_ffn2_w_down, v_ffn2_w_down), norm_final=(norm_final, m_norm_final, v_norm_final))
    out = {}

    def big_update(name, g2d):
        w, m, v = given[name]
        shp = w.shape
        d_, nm, nv = _adamw(f"adamw_{name}", w.reshape(g2d.shape), g2d, m.reshape(g2d.shape), v.reshape(g2d.shape),
                            _tile(g2d.shape[0], 256))
        out[name] = tuple(t.reshape(shp) for t in (g2d, d_, nm, nv))

    for key, name in (("g1", "ffn1_w_gate"), ("u1", "ffn1_w_up"), ("d1", "ffn1_w_down"), ("win", "mix_w_in"),
                      ("wa", "w_branch_a"), ("wb", "w_branch_b"), ("wo", "w_out"),
                      ("g2", "ffn2_w_gate"), ("u2", "ffn2_w_up"), ("d2", "ffn2_w_down")):
        big_update(name, reduced[key])
    big_update("ada_w", g_ada_w)

    small_names = [("norm_ffn1", 1), ("norm_mix", 1), ("mix_b_in", 6), ("sgu_ln_g", 1), ("sgu_ln_b", 1),
                   ("sgu_w_s", CHUNK), ("sgu_b_s", 1), (None, KCONV), ("conv_b", 1), ("conv_ln_g", 1), ("conv_ln_b", 1),
                   ("norm_ffn2", 1), ("norm_final", 1), ("ada_b", N_MOD), (None, 8)]

    def pack(which):
        return jnp.concatenate([jnp.zeros((r, D), F32) if nm is None else given[nm][which].reshape(r, D)
                                for nm, r in small_names], axis=0)

    sd, sm, sv = _adamw("adamw_small", pack(0), tot, pack(1), pack(2), 64)
    row = 0
    for nm, r in small_names:
        if nm is not None:
            shp = given[nm][0].shape
            out[nm] = tuple(t[row:row + r].reshape(shp) for t in (tot, sd, sm, sv))
        row += r
    r_cw = 1 + 1 + 6 + 1 + 1 + CHUNK + 1
    n_cw = conv_w.shape[2]
    g_cw = jnp.pad(lax.dynamic_slice(tot[r_cw:r_cw + KCONV], (0, q * n_cw), (KCONV, n_cw)), ((0, 1), (0, 0)))
    pad_cw = lambda t: jnp.pad(t[0], ((0, 1), (0, 0)))
    cd, cm, cv_ = _adamw("adamw_conv_w", pad_cw(conv_w), g_cw, pad_cw(m_conv_w), pad_cw(v_conv_w), HALO)
    out["conv_w"] = tuple(t[:KCONV][None] for t in (g_cw, cd, cm, cv_))

    order = ["ada_w", "ada_b", "norm_ffn1", "ffn1_w_gate", "ffn1_w_up", "ffn1_w_down", "norm_mix", "mix_w_in", "mix_b_in",
             "sgu_ln_g", "sgu_ln_b", "sgu_w_s", "sgu_b_s", "conv_w", "conv_b", "conv_ln_g", "conv_ln_b", "w_branch_a",
             "w_branch_b", "w_out", "norm_ffn2", "ffn2_w_gate", "ffn2_w_up", "ffn2_w_down", "norm_final"]
    return (loss, grad_x[None], *[out[n][0] for n in order], *[out[n][1] for n in order],
            *[out[n][2] for n in order], *[out[n][3] for n in order])
```

```python
import functools

import jax
import jax.numpy as jnp
from jax import lax
from jax.experimental import pallas as pl
from jax.experimental.pallas import tpu as pltpu

F32 = jnp.float32
BF16 = jnp.bfloat16
D = 1024
HEADS = 8
CHUNK = 128
KCONV = 31
HALO = 32
N_MOD = 9
EPS = 1e-6
NCHIP = 4
VMEM_LIMIT = 56 * 1024 * 1024
MESH = pl.DeviceIdType.MESH

ADAM_LR = 0.001
ADAM_B1 = 0.9
ADAM_B2 = 0.999
ADAM_EPS = 1e-08
ADAM_WD = 0.01
ADAM_STEP = 10

NN = (((1,), (0,)), ((), ()))
NT = (((1,), (1,)), ((), ()))
TN = (((0,), (0,)), ((), ()))


def _cparams(*sem):
    return pltpu.CompilerParams(dimension_semantics=sem or None, vmem_limit_bytes=VMEM_LIMIT)


def _rms(x, w):
    return x * lax.rsqrt(jnp.mean(x * x, axis=-1, keepdims=True) + EPS) * w


def _ln(x, g, b):
    mu = jnp.mean(x, axis=-1, keepdims=True)
    xc = x - mu
    var = jnp.mean(xc * xc, axis=-1, keepdims=True)
    return xc * lax.rsqrt(var + EPS) * g + b


def _silu(x):
    return x * jax.nn.sigmoid(x)


def _gelu(x):
    return x * (0.5 * (1.0 + jnp.tanh(0.7978845608028654 * (x + 0.044715 * (x * x * x)))))


def _f_pre(x, nw, sc, sh):
    return _rms(x, nw) * (1.0 + sc) + sh


def _f_pre_keep(x, nw, sc, sh):
    return x, _f_pre(x, nw, sc, sh)


def _f_res_pre(scale):
    def fn(x, f, g, nw, sc, sh):
        xn = x + (scale * g) * f
        return xn, _f_pre(xn, nw, sc, sh)
    return fn


def _f_glu(a, b):
    return _silu(a) * b


def _f_merge(gab, ya, yb, bgab):
    g = gab + bgab
    return jax.nn.sigmoid(g[:, :D]) * ya + jax.nn.sigmoid(g[:, D:]) * yb


def _f_lnsilu(z, g, b):
    return _silu(_ln(z, g, b))


def _f_mixa(uv, buv, g, b):
    t = uv + buv
    return _gelu(t[:, :D]), _ln(_gelu(t[:, D:]), g, b)


def _col(t):
    return t if isinstance(t, tuple) else (t, t.shape[-1], 0)


def _rowwise_fwd(name, fn, rows, params, outs, tr):
    rows = [_col(r) for r in rows]
    params = [_col(p) for p in params]
    n_rows = rows[0][0].shape[0]
    nin = len(rows) + len(params)

    def body(*refs):
        res = fn(*[r[...].astype(F32) for r in refs[:nin]])
        res = res if isinstance(res, tuple) else (res,)
        for o, v in zip(refs[nin:], res, strict=True):
            o[...] = v.astype(o.dtype)

    return pl.pallas_call(
        body, name=name, grid=(n_rows // tr,),
        in_specs=[pl.BlockSpec((tr, w), functools.partial(lambda cb, i: (i, cb), cb)) for _, w, cb in rows]
        + [pl.BlockSpec((1, w), functools.partial(lambda cb, i: (0, cb), cb)) for _, w, cb in params],
        out_specs=[pl.BlockSpec((tr, w), lambda i: (i, 0)) for w, _ in outs],
        out_shape=[jax.ShapeDtypeStruct((n_rows, w), dt) for w, dt in outs],
        compiler_params=_cparams("parallel"),
    )(*[r[0] for r in rows], *[p[0] for p in params])


def _rowwise_bwd(name, fn, rows, params, cots, row_grads, tr):
    rows = [_col(r) for r in rows]
    params = [_col(p) for p in params]
    cots = [_col(t) for t in cots]
    n_rows = rows[0][0].shape[0]
    nr, npar, nc = len(rows), len(params), len(cots)
    nin = nr + npar + nc
    n_rg = sum(dt is not None for dt in row_grads)

    def body(*refs):
        i = pl.program_id(0)
        prim = [r[...].astype(F32) for r in refs[:nr + npar]]
        ct = tuple(r[...].astype(F32) for r in refs[nr + npar:nin])
        _, vjp = jax.vjp(fn, *prim)
        g = vjp(ct if nc > 1 else ct[0])
        outs = refs[nin:]
        oi = 0
        for j, dt in enumerate(row_grads):
            if dt is not None:
                outs[oi][...] = g[j].astype(dt)
                oi += 1
        for j in range(npar):
            acc = outs[n_rg + j]

            @pl.when(i == 0)
            def _(acc=acc):
                acc[...] = jnp.zeros(acc.shape, F32)

            acc[...] += g[nr + j]

    res = pl.pallas_call(
        body, name=name, grid=(n_rows // tr,),
        in_specs=[pl.BlockSpec((tr, w), functools.partial(lambda cb, i: (i, cb), cb)) for _, w, cb in rows]
        + [pl.BlockSpec((1, w), functools.partial(lambda cb, i: (0, cb), cb)) for _, w, cb in params]
        + [pl.BlockSpec((tr, w), functools.partial(lambda cb, i: (i, cb), cb)) for _, w, cb in cots],
        out_specs=[pl.BlockSpec((tr, rows[j][1]), lambda i: (i, 0)) for j, dt in enumerate(row_grads) if dt is not None]
        + [pl.BlockSpec((1, w), lambda i: (0, 0)) for _, w, _ in params],
        out_shape=[jax.ShapeDtypeStruct((n_rows, rows[j][1]), dt) for j, dt in enumerate(row_grads) if dt is not None]
        + [jax.ShapeDtypeStruct((1, w), F32) for _, w, _ in params],
        compiler_params=_cparams("arbitrary"),
    )(*[r[0] for r in rows], *[p[0] for p in params], *[t[0] for t in cots])
    return res[:n_rg], res[n_rg:]


def _final(name, x, f, tgt, g, nw, tr):
    n_rows = x.shape[0]

    def body(x_ref, f_ref, t_ref, g_ref, nw_ref, dx_ref, df_ref, dg_ref, dnw_ref, loss_ref):
        i = pl.program_id(0)
        tg = t_ref[...]

        def fn(xv, fv, gv, nwv):
            e = _rms(xv + (0.5 * gv) * fv, nwv) - tg
            return 0.5 * jnp.mean(e * e, axis=-1, keepdims=True)

        per_row, vjp = jax.vjp(fn, x_ref[...], f_ref[...], g_ref[...], nw_ref[...])
        dx, df, dg, dnw = vjp(jnp.ones_like(per_row))
        dx_ref[...] = dx
        df_ref[...] = df.astype(df_ref.dtype)

        @pl.when(i == 0)
        def _():
            dg_ref[...] = jnp.zeros(dg_ref.shape, F32)
            dnw_ref[...] = jnp.zeros(dnw_ref.shape, F32)
            loss_ref[...] = jnp.zeros(loss_ref.shape, F32)

        dg_ref[...] += dg
        dnw_ref[...] += dnw
        loss_ref[...] += jnp.broadcast_to(jnp.sum(per_row, axis=0, keepdims=True), loss_ref.shape)

    row = pl.BlockSpec((tr, D), lambda i: (i, 0))
    par = pl.BlockSpec((1, D), lambda i: (0, 0))
    return pl.pallas_call(
        body, name=name, grid=(n_rows // tr,),
        in_specs=[row, row, row, par, par],
        out_specs=[row, row, par, par, pl.BlockSpec((8, 128), lambda i: (0, 0))],
        out_shape=[jax.ShapeDtypeStruct((n_rows, D), F32), jax.ShapeDtypeStruct((n_rows, D), BF16),
                   jax.ShapeDtypeStruct((1, D), F32), jax.ShapeDtypeStruct((1, D), F32),
                   jax.ShapeDtypeStruct((8, 128), F32)],
        compiler_params=_cparams("arbitrary"),
    )(x, f, tgt, g, nw)


def _tile(dim, pref):
    t = min(dim, pref)
    while dim % t:
        t //= 2
    return t


def _matmul(name, a, b, kind, col, out_dtype, add=None):
    if kind == "wgrad":
        m, kg = a.shape
        ng = b.shape[1]
        r, c = (kg, ng // NCHIP) if col else (kg // NCHIP, ng)
        tm, tn, tk = _tile(r, 1024), _tile(c, 1024), _tile(m, 512)
        grid = (kg // tm, ng // tn, m // tk)
        a_spec = pl.BlockSpec((tk, tm), lambda i, j, k: (k, i))
        b_spec = pl.BlockSpec((tk, tn), lambda i, j, k: (k, j))
        if col:
            o_spec = pl.BlockSpec((None, tm, tn), lambda i, j, k: (j // (c // tn), i, j % (c // tn)))
        else:
            o_spec = pl.BlockSpec((None, tm, tn), lambda i, j, k: (i // (r // tm), i % (r // tm), j))
        out_shape = jax.ShapeDtypeStruct((NCHIP, r, c), F32)
        dims = TN
    else:
        _, r, c = b.shape
        m = a.shape[0]
        kg, ng = (r, NCHIP * c) if col else (NCHIP * r, c)
        tm = _tile(m, 1024)
        if kind == "fwd":
            tn, tk = _tile(c, 1024), _tile(r, 512)
            grid = (m // tm, ng // tn, kg // tk)
            a_spec = pl.BlockSpec((tm, tk), lambda i, j, k: (i, k))
            if col:
                b_spec = pl.BlockSpec((None, tk, tn), lambda i, j, k: (j // (c // tn), k, j % (c // tn)))
            else:
                b_spec = pl.BlockSpec((None, tk, tn), lambda i, j, k: (k // (r // tk), k % (r // tk), j))
            out_shape = jax.ShapeDtypeStruct((m, ng), out_dtype)
            dims = NN
        else:
            tn, tk = _tile(r, 1024), _tile(c, 512)
            grid = (m // tm, kg // tn, ng // tk)
            a_spec = pl.BlockSpec((tm, tk), lambda i, j, k: (i, k))
            if col:
                b_spec = pl.BlockSpec((None, tn, tk), lambda i, j, k: (k // (c // tk), j, k % (c // tk)))
            else:
                b_spec = pl.BlockSpec((None, tn, tk), lambda i, j, k: (j // (r // tn), j % (r // tn), k))
            out_shape = jax.ShapeDtypeStruct((m, kg), out_dtype)
            dims = NT
        o_spec = pl.BlockSpec((tm, tn), lambda i, j, k: (i, j))
    nk = grid[2]
    has_add = add is not None

    def body(*refs):
        a_ref, b_ref = refs[0], refs[1]
        o_ref, acc_ref = refs[-2], refs[-1]
        k = pl.program_id(2)

        @pl.when(k == 0)
        def _():
            acc_ref[...] = jnp.zeros(acc_ref.shape, F32)

        acc_ref[...] += lax.dot_general(a_ref[...].astype(BF16), b_ref[...].astype(BF16), dims,
                                        preferred_element_type=F32)

        @pl.when(k == nk - 1)
        def _():
            r_ = acc_ref[...]
            if has_add:
                r_ = r_ + refs[2][...]
            o_ref[...] = r_.astype(o_ref.dtype)

    in_specs = [a_spec, b_spec] + ([pl.BlockSpec((tm, tn), lambda i, j, k: (i, j))] if has_add else [])
    args = (a, b) + ((add,) if has_add else ())
    return pl.pallas_call(
        body, name=name, grid=grid, in_specs=in_specs, out_specs=o_spec, out_shape=out_shape,
        scratch_shapes=[pltpu.VMEM((tm, tn), F32)],
        compiler_params=_cparams("parallel", "parallel", "arbitrary"),
    )(*args)


def _ffn_up(name, h, wg, wu):
    m, k = h.shape
    _, _, c = wg.shape
    n = NCHIP * c
    tm, tn = _tile(m, 1024), _tile(c, 512)

    def body(h_ref, wg_ref, wu_ref, a_ref, b_ref, s_ref):
        hv = h_ref[...]
        a = jnp.dot(hv, wg_ref[...], preferred_element_type=F32)
        b = jnp.dot(hv, wu_ref[...], preferred_element_type=F32)
        a_ref[...] = a
        b_ref[...] = b
        s_ref[...] = (_silu(a) * b).astype(s_ref.dtype)

    w_spec = pl.BlockSpec((None, k, tn), lambda i, j: (j // (c // tn), 0, j % (c // tn)))
    o_spec = pl.BlockSpec((tm, tn), lambda i, j: (i, j))
    return pl.pallas_call(
        body, name=name, grid=(m // tm, n // tn),
        in_specs=[pl.BlockSpec((tm, k), lambda i, j: (i, 0)), w_spec, w_spec],
        out_specs=[o_spec, o_spec, o_spec],
        out_shape=[jax.ShapeDtypeStruct((m, n), F32), jax.ShapeDtypeStruct((m, n), F32),
                   jax.ShapeDtypeStruct((m, n), BF16)],
        compiler_params=_cparams("parallel", "parallel"),
    )(h, wg, wu)


def _sgu_fwd(name, proj, b_in, lng, lnb, wm, bst, tr):
    n_rows = proj.shape[0]

    def body(uv_ref, buv_ref, g_ref, b_ref, wm_ref, bst_ref, ya_ref):
        u, vln = _f_mixa(uv_ref[...], buv_ref[...], g_ref[...], b_ref[...])
        vb = vln.astype(BF16)
        for ck in range(tr // CHUNK):
            rs = slice(ck * CHUNK, (ck + 1) * CHUNK)
            for h in range(HEADS):
                cs = slice(h * CHUNK, (h + 1) * CHUNK)
                vm = jnp.dot(wm_ref[h], vb[rs, cs], preferred_element_type=F32) + bst_ref[:, cs]
                ya_ref[rs, cs] = (u[rs, cs] * vm).astype(ya_ref.dtype)

    return pl.pallas_call(
        body, name=name, grid=(n_rows // tr,),
        in_specs=[pl.BlockSpec((tr, 2 * D), lambda i: (i, 0)), pl.BlockSpec((1, 2 * D), lambda i: (0, 0)),
                  pl.BlockSpec((1, D), lambda i: (0, 0)), pl.BlockSpec((1, D), lambda i: (0, 0)),
                  pl.BlockSpec((HEADS, CHUNK, CHUNK), lambda i: (0, 0, 0)),
                  pl.BlockSpec((CHUNK, D), lambda i: (0, 0))],
        out_specs=pl.BlockSpec((tr, D), lambda i: (i, 0)),
        out_shape=jax.ShapeDtypeStruct((n_rows, D), BF16),
        compiler_params=_cparams("parallel"),
    )(proj, b_in, lng, lnb, wm, bst)


def _sgu_bwd(name, proj, b_in, lng, lnb, wm, wmt, bst, dya, tr):
    n_rows = proj.shape[0]

    def body(uv_ref, buv_ref, g_ref, b_ref, wm_ref, wmt_ref, bst_ref, dya_ref,
             duv_ref, dbuv_ref, dg_ref, db_ref, dws_ref, dbs_ref, du_s, dvln_s):
        i = pl.program_id(0)

        @pl.when(i == 0)
        def _():
            dbuv_ref[...] = jnp.zeros(dbuv_ref.shape, F32)
            dg_ref[...] = jnp.zeros(dg_ref.shape, F32)
            db_ref[...] = jnp.zeros(db_ref.shape, F32)
            dws_ref[...] = jnp.zeros(dws_ref.shape, F32)
            dbs_ref[...] = jnp.zeros(dbs_ref.shape, F32)

        (u, vln), vjp = jax.vjp(_f_mixa, uv_ref[...], buv_ref[...], g_ref[...], b_ref[...])
        vb = vln.astype(BF16)
        dya_v = dya_ref[...].astype(F32)
        tpos = lax.broadcasted_iota(jnp.int32, (CHUNK, CHUNK), 0)
        spos = lax.broadcasted_iota(jnp.int32, (CHUNK, CHUNK), 1)
        causal = (tpos >= spos).astype(F32)
        for ck in range(tr // CHUNK):
            rs = slice(ck * CHUNK, (ck + 1) * CHUNK)
            for h in range(HEADS):
                cs = slice(h * CHUNK, (h + 1) * CHUNK)
                vbh = vb[rs, cs]
                vm = jnp.dot(wm_ref[h], vbh, preferred_element_type=F32) + bst_ref[:, cs]
                dyb = dya_v[rs, cs]
                du_s[rs, cs] = dyb * vm
                dvm = dyb * u[rs, cs]
                dvmb = dvm.astype(BF16)
                dvln_s[rs, cs] = jnp.dot(wmt_ref[h], dvmb, preferred_element_type=F32)
                dws_ref[h] += causal * lax.dot_general(dvmb, vbh, NT, preferred_element_type=F32)
                dbs_ref[:, cs] += jnp.broadcast_to(jnp.sum(dvm, axis=1, keepdims=True), (CHUNK, CHUNK))
        duv, dbuv, dg, db = vjp((du_s[...], dvln_s[...]))
        duv_ref[...] = duv.astype(duv_ref.dtype)
        dbuv_ref[...] += dbuv
        dg_ref[...] += dg
        db_ref[...] += db

    par = pl.BlockSpec((1, D), lambda i: (0, 0))
    par2 = pl.BlockSpec((1, 2 * D), lambda i: (0, 0))
    w_spec = pl.BlockSpec((HEADS, CHUNK, CHUNK), lambda i: (0, 0, 0))
    b_spec = pl.BlockSpec((CHUNK, D), lambda i: (0, 0))
    return pl.pallas_call(
        body, name=name, grid=(n_rows // tr,),
        in_specs=[pl.BlockSpec((tr, 2 * D), lambda i: (i, 0)), par2, par, par, w_spec, w_spec, b_spec,
                  pl.BlockSpec((tr, D), lambda i: (i, 0))],
        out_specs=[pl.BlockSpec((tr, 2 * D), lambda i: (i, 0)), par2, par, par, w_spec, b_spec],
        out_shape=[jax.ShapeDtypeStruct((n_rows, 2 * D), BF16), jax.ShapeDtypeStruct((1, 2 * D), F32),
                   jax.ShapeDtypeStruct((1, D), F32), jax.ShapeDtypeStruct((1, D), F32),
                   jax.ShapeDtypeStruct((HEADS, CHUNK, CHUNK), F32), jax.ShapeDtypeStruct((CHUNK, D), F32)],
        scratch_shapes=[pltpu.VMEM((tr, D), F32), pltpu.VMEM((tr, D), F32)],
        compiler_params=_cparams("arbitrary"),
    )(proj, b_in, lng, lnb, wm, wmt, bst, dya)


CT = 256
RB = 128
CV0 = 2 * D // CT
CG0 = 3 * D // CT


def _shifted(win, sh):
    if sh == 0:
        return win[0:RB]
    return pltpu.roll(win, RB + HALO - sh, 0)[0:RB]


def _conv_fwd(name, proj, b_in, w, cb):
    n_rows = proj.shape[0]

    def body(cv_ref, cg_ref, bcv_ref, bcg_ref, w_ref, cb_ref, z1_ref, zp_ref):
        zp_ref[0:HALO, :] = jnp.zeros((HALO, CT), F32)
        zp_ref[HALO:, :] = (cv_ref[...] + bcv_ref[...]) * jax.nn.sigmoid(cg_ref[...] + bcg_ref[...])

        def blk(rb, carry):
            base = pl.multiple_of(rb * RB, RB)
            win = zp_ref[pl.ds(base, RB + HALO), :]
            acc = jnp.broadcast_to(cb_ref[...], (RB, CT))
            for k in range(KCONV):
                acc = acc + w_ref[k:k + 1, :] * _shifted(win, k + 2)
            z1_ref[pl.ds(base, RB), :] = acc
            return carry

        lax.fori_loop(0, n_rows // RB, blk, 0)

    return pl.pallas_call(
        body, name=name, grid=(D // CT,),
        in_specs=[pl.BlockSpec((n_rows, CT), lambda j: (0, CV0 + j)), pl.BlockSpec((n_rows, CT), lambda j: (0, CG0 + j)),
                  pl.BlockSpec((1, CT), lambda j: (0, CV0 + j)), pl.BlockSpec((1, CT), lambda j: (0, CG0 + j)),
                  pl.BlockSpec((HALO, CT), lambda j: (0, j)), pl.BlockSpec((1, CT), lambda j: (0, j))],
        out_specs=pl.BlockSpec((n_rows, CT), lambda j: (0, j)),
        out_shape=jax.ShapeDtypeStruct((n_rows, D), F32),
        scratch_shapes=[pltpu.VMEM((n_rows + HALO, CT), F32)],
        compiler_params=_cparams("parallel"),
    )(proj, proj, b_in, b_in, w, cb)


def _conv_bwd(name, dz1, proj, b_in, w):
    n_rows = proj.shape[0]

    def body(dz_ref, cv_ref, cg_ref, bcv_ref, bcg_ref, w_ref, dcv_ref, dcg_ref, sm_ref, zp_ref, dzp_ref):
        cvb = cv_ref[...] + bcv_ref[...]
        sg = jax.nn.sigmoid(cg_ref[...] + bcg_ref[...])
        zp_ref[0:HALO, :] = jnp.zeros((HALO, CT), F32)
        zp_ref[HALO:, :] = cvb * sg
        dz = dz_ref[...]
        dzp_ref[0:n_rows, :] = dz
        dzp_ref[n_rows:, :] = jnp.zeros((HALO, CT), F32)
        sm_ref[...] = jnp.zeros(sm_ref.shape, F32)
        sm_ref[32:33, :] = jnp.sum(dz, axis=0, keepdims=True)

        def blk(rb, carry):
            base = pl.multiple_of(rb * RB, RB)
            dwin = dzp_ref[pl.ds(base, RB + HALO), :]
            zwin = zp_ref[pl.ds(base, RB + HALO), :]
            dzb = dwin[0:RB]
            acc = jnp.zeros((RB, CT), F32)
            for k in range(KCONV):
                acc = acc + w_ref[k:k + 1, :] * _shifted(dwin, KCONV - 1 - k)
                sm_ref[k:k + 1, :] += jnp.sum(dzb * _shifted(zwin, k + 2), axis=0, keepdims=True)
            dzp_ref[pl.ds(base, RB), :] = acc
            return carry

        lax.fori_loop(0, n_rows // RB, blk, 0)
        dz0 = dzp_ref[0:n_rows, :]
        dcv = dz0 * sg
        dcg = dz0 * cvb * (sg * (1.0 - sg))
        dcv_ref[...] = dcv.astype(dcv_ref.dtype)
        dcg_ref[...] = dcg.astype(dcg_ref.dtype)
        sm_ref[33:34, :] = jnp.sum(dcv, axis=0, keepdims=True)
        sm_ref[34:35, :] = jnp.sum(dcg, axis=0, keepdims=True)

    col = pl.BlockSpec((n_rows, CT), lambda j: (0, j))
    return pl.pallas_call(
        body, name=name, grid=(D // CT,),
        in_specs=[col, pl.BlockSpec((n_rows, CT), lambda j: (0, CV0 + j)), pl.BlockSpec((n_rows, CT), lambda j: (0, CG0 + j)),
                  pl.BlockSpec((1, CT), lambda j: (0, CV0 + j)), pl.BlockSpec((1, CT), lambda j: (0, CG0 + j)),
                  pl.BlockSpec((HALO, CT), lambda j: (0, j))],
        out_specs=[col, col, pl.BlockSpec((40, CT), lambda j: (0, j))],
        out_shape=[jax.ShapeDtypeStruct((n_rows, D), BF16), jax.ShapeDtypeStruct((n_rows, D), BF16),
                   jax.ShapeDtypeStruct((40, D), F32)],
        scratch_shapes=[pltpu.VMEM((n_rows + HALO, CT), F32), pltpu.VMEM((n_rows + HALO, CT), F32)],
        compiler_params=_cparams("parallel"),
    )(dz1, proj, proj, b_in, b_in, w)


ADA_TN = 768


def _split_bf16(v):
    hi = v.astype(BF16)
    return hi, (v - hi.astype(F32)).astype(BF16)


def _ada_fwd(name, c_all, w, b):
    n = w.shape[1]

    def body(c_ref, w_ref, b_ref, o_ref):
        ch, cl = _split_bf16(_silu(c_ref[...]))
        wh, wl = _split_bf16(w_ref[...])
        acc = jnp.dot(ch, wl, preferred_element_type=F32) + jnp.dot(cl, wh, preferred_element_type=F32)
        o_ref[...] = acc + jnp.dot(ch, wh, preferred_element_type=F32) + b_ref[...]

    return pl.pallas_call(
        body, name=name, grid=(n // ADA_TN,),
        in_specs=[pl.BlockSpec((8, D), lambda j: (0, 0)), pl.BlockSpec((D, ADA_TN), lambda j: (0, j)),
                  pl.BlockSpec((1, ADA_TN), lambda j: (0, j))],
        out_specs=pl.BlockSpec((8, ADA_TN), lambda j: (0, j)),
        out_shape=jax.ShapeDtypeStruct((8, n), F32),
        compiler_params=_cparams("parallel"),
    )(c_all, w, b)


def _ada_wgrad(name, c_all_t, dmod):
    n = dmod.shape[1]

    def body(ct_ref, dm_ref, o_ref):
        ca = _silu(ct_ref[...])
        acc = ca[:, 0:1] * dm_ref[0:1, :]
        for r in range(1, 8):
            acc = acc + ca[:, r:r + 1] * dm_ref[r:r + 1, :]
        o_ref[...] = acc

    return pl.pallas_call(
        body, name=name, grid=(n // ADA_TN,),
        in_specs=[pl.BlockSpec((D, 8), lambda j: (0, 0)), pl.BlockSpec((8, ADA_TN), lambda j: (0, j))],
        out_specs=pl.BlockSpec((D, ADA_TN), lambda j: (0, j)),
        out_shape=jax.ShapeDtypeStruct((D, n), F32),
        compiler_params=_cparams("parallel"),
    )(c_all_t, dmod)


def _adamw(name, w, g, m, v, tr):
    rows, cols = w.shape

    def body(w_ref, g_ref, m_ref, v_ref, d_ref, nm_ref, nv_ref):
        gv = g_ref[...]
        nm = ADAM_B1 * m_ref[...] + (1.0 - ADAM_B1) * gv
        nv = ADAM_B2 * v_ref[...] + (1.0 - ADAM_B2) * (gv * gv)
        m_hat = nm / (1.0 - ADAM_B1 ** ADAM_STEP)
        v_hat = nv / (1.0 - ADAM_B2 ** ADAM_STEP)
        d_ref[...] = -ADAM_LR * (m_hat / (jnp.sqrt(v_hat) + ADAM_EPS) + ADAM_WD * w_ref[...])
        nm_ref[...] = nm
        nv_ref[...] = nv

    spec = pl.BlockSpec((tr, cols), lambda i: (i, 0))
    sds = jax.ShapeDtypeStruct((rows, cols), F32)
    return pl.pallas_call(
        body, name=name, grid=(rows // tr,), in_specs=[spec] * 4, out_specs=[spec] * 3, out_shape=[sds] * 3,
        compiler_params=_cparams("parallel"),
    )(w, g, m, v)


def _sum8(name, stacked, tr):
    n, rows, cols = stacked.shape

    def body(s_ref, o_ref):
        acc = s_ref[0]
        for r in range(1, n):
            acc = acc + s_ref[r]
        o_ref[...] = acc

    return pl.pallas_call(
        body, name=name, grid=(rows // tr,),
        in_specs=[pl.BlockSpec((n, tr, cols), lambda i: (0, i, 0))],
        out_specs=pl.BlockSpec((tr, cols), lambda i: (i, 0)),
        out_shape=jax.ShapeDtypeStruct((rows, cols), F32),
        compiler_params=_cparams("parallel"),
    )(stacked)


def _pair_add(name, g5, other, c_idx, tr):
    nq, _, rows, cols = g5.shape

    def body(c_ref, g_ref, o_ref, p_ref):
        p_ref[...] = (g_ref[...] + o_ref[...]).astype(p_ref.dtype)

    return pl.pallas_call(
        body, name=name,
        grid_spec=pltpu.PrefetchScalarGridSpec(
            num_scalar_prefetch=1, grid=(nq, rows // tr),
            in_specs=[pl.BlockSpec((None, None, tr, cols), lambda qi, i, cr: (qi, cr[0], i, 0)),
                      pl.BlockSpec((None, tr, cols), lambda qi, i, cr: (qi, i, 0))],
            out_specs=pl.BlockSpec((None, tr, cols), lambda qi, i, cr: (qi, i, 0))),
        out_shape=jax.ShapeDtypeStruct((nq, rows, cols), BF16),
        compiler_params=_cparams("parallel", "parallel"),
    )(c_idx, g5, other)


def _chip_add(name, p, recv, qc_idx, tr):
    _, rows, cols = p.shape

    def body(qc_ref, p_ref, r_ref, o_ref):
        acc = p_ref[...].astype(F32)
        for k in range(NCHIP - 1):
            acc = acc + r_ref[k].astype(F32)
        o_ref[...] = acc

    return pl.pallas_call(
        body, name=name,
        grid_spec=pltpu.PrefetchScalarGridSpec(
            num_scalar_prefetch=1, grid=(rows // tr,),
            in_specs=[pl.BlockSpec((None, tr, cols), lambda i, qc: (qc[0], i, 0)),
                      pl.BlockSpec((NCHIP - 1, tr, cols), lambda i, qc: (0, i, 0))],
            out_specs=pl.BlockSpec((None, tr, cols), lambda i, qc: (qc[1], i, 0))),
        out_shape=jax.ShapeDtypeStruct((2, rows, cols), F32),
        compiler_params=_cparams("parallel"),
    )(qc_idx, p, recv)


def _allgather8(name, blk):
    m_per, n = blk.shape

    def body(x_ref, out_ref, send_sems, recv_sems, local_sem):
        x, y, c = lax.axis_index("x"), lax.axis_index("y"), lax.axis_index("c")
        me, sibling = (x, y, c), (x, y, 1 - c)
        chips = [(1 - x, y), (x, 1 - y), (1 - x, 1 - y)]

        def rows(px, py, pc):
            return out_ref.at[pl.ds((4 * px + 2 * py + pc) * m_per, m_per), :]

        def copy(k, block, to, src=None):
            return pltpu.make_async_remote_copy(
                src_ref=rows(*block) if src is None else src, dst_ref=rows(*block),
                send_sem=send_sems.at[k], recv_sem=recv_sems.at[k], device_id=to, device_id_type=MESH)

        mine = pltpu.make_async_copy(x_ref, rows(*me), local_sem)
        mine.start()
        first = [copy(0, me, sibling, src=x_ref)]
        first += [copy(1 + j, me, (*chip, c), src=x_ref) for j, chip in enumerate(chips)]
        for cp in first:
            cp.start()
        passed = [copy(4 + j, (*chip, c), sibling) for j, chip in enumerate(chips)]
        for j, chip in enumerate(chips):
            copy(1 + j, (*chip, c), me).wait_recv()
            passed[j].start()
        copy(0, sibling, me).wait_recv()
        for j, chip in enumerate(chips):
            copy(4 + j, (*chip, 1 - c), me).wait_recv()
        for cp in first + passed:
            cp.wait_send()
        mine.wait()

    return pl.pallas_call(
        body, name=name,
        out_shape=jax.ShapeDtypeStruct((8 * m_per, n), blk.dtype),
        in_specs=[pl.BlockSpec(memory_space=pltpu.VMEM)],
        out_specs=pl.BlockSpec(memory_space=pltpu.VMEM),
        scratch_shapes=[pltpu.SemaphoreType.DMA((7,)), pltpu.SemaphoreType.DMA((7,)), pltpu.SemaphoreType.DMA],
        compiler_params=pltpu.CompilerParams(vmem_limit_bytes=VMEM_LIMIT),
    )(blk)


def _exchange(name, srcs, out_shapes, n_local, n_remote, plan, aliases=None):
    ni, no = len(srcs), len(out_shapes)

    def body(*refs):
        ins, outs = refs[:ni], refs[ni:ni + no]
        send_sems, recv_sems, local_sems = refs[ni + no:]
        x, y, c = lax.axis_index("x"), lax.axis_index("y"), lax.axis_index("c")
        local, remote = plan(ins, outs, x, y, c)
        assert len(local) == n_local and len(remote) == n_remote

        def rcopy(i, dst):
            s, _, peer, _, _ = remote[i]
            return pltpu.make_async_remote_copy(src_ref=s, dst_ref=dst, send_sem=send_sems.at[i],
                                                recv_sem=recv_sems.at[i], device_id=peer, device_id_type=MESH)

        lcs = [pltpu.make_async_copy(s, d, local_sems.at[i]) for i, (s, d) in enumerate(local)]
        for cp in lcs:
            cp.start()
        first = [i for i in range(n_remote) if remote[i][4] is None]
        passed = [i for i in range(n_remote) if remote[i][4] is not None]
        for i in first:
            rcopy(i, remote[i][1]).start()
        arrived = set()
        for i in passed:
            j = remote[i][4]
            rcopy(j, remote[j][3]).wait_recv()
            arrived.add(j)
            rcopy(i, remote[i][1]).start()
        for i in range(n_remote):
            if i not in arrived:
                rcopy(i, remote[i][3]).wait_recv()
        for i in range(n_remote):
            rcopy(i, remote[i][1]).wait_send()
        for cp in lcs:
            cp.wait()

    any_spec = pl.BlockSpec(memory_space=pl.ANY)
    return pl.pallas_call(
        body, name=name, out_shape=out_shapes,
        in_specs=[any_spec] * ni, out_specs=[any_spec] * no,
        input_output_aliases=aliases or {},
        scratch_shapes=[pltpu.SemaphoreType.DMA((n_remote,)), pltpu.SemaphoreType.DMA((n_remote,)),
                        pltpu.SemaphoreType.DMA((max(n_local, 1),))],
    )(*srcs)


_CHIP_FLIPS = ((0, 1), (1, 0), (1, 1))


def _flip(v, f):
    return 1 - v if f else v


def _gather_chips(name, stacks, smalls):
    ns, nsm = len(stacks), len(smalls)

    def plan(ins, outs, x, y, c):
        q = 2 * x + y
        sib = (x, y, 1 - c)
        chips = [(_flip(x, fx), _flip(y, fy)) for fx, fy in _CHIP_FLIPS]
        local = [(ins[ns + b], outs[ns + b].at[q]) for b in range(nsm)]
        remote = []
        for a in range(ns):
            for px, py in chips:
                remote.append((ins[a].at[q, c], outs[a].at[q, c], (px, py, c), outs[a].at[2 * px + py, c], None))
        for b in range(nsm):
            for px, py in chips:
                remote.append((ins[ns + b], outs[ns + b].at[q], (px, py, c), outs[ns + b].at[2 * px + py], None))
        for a in range(ns):
            for k, (px, py) in enumerate(chips):
                p = 2 * px + py
                remote.append((outs[a].at[p, c], outs[a].at[p, c], sib, outs[a].at[p, 1 - c], 3 * a + k))
        return local, remote

    shapes = [jax.ShapeDtypeStruct(s.shape, s.dtype) for s in stacks]
    shapes += [jax.ShapeDtypeStruct((NCHIP,) + s.shape, s.dtype) for s in smalls]
    return _exchange(name, list(stacks) + list(smalls), shapes, nsm, 6 * ns + 3 * nsm, plan,
                     aliases={a: a for a in range(ns)})


def _reduce_scatter(grads, c_idx, q_idx):
    n = len(grads)
    g5 = [g.reshape(NCHIP, 2, g.shape[1] // 2, g.shape[2]) for g in grads]
    half = [(g.shape[2], g.shape[3]) for g in g5]

    def plan1(ins, outs, x, y, c):
        sib = (x, y, 1 - c)
        return [], [(ins[a].at[:, 1 - c], outs[a], sib, outs[a], None) for a in range(n)]

    got1 = _exchange("rs_pair", g5, [jax.ShapeDtypeStruct((NCHIP,) + h, F32) for h in half], 0, n, plan1)
    part = [_pair_add(f"rs_pair_add{a}", g5[a], got1[a], c_idx, _tile(half[a][0], 256)) for a in range(n)]

    def plan2(ins, outs, x, y, c):
        remote = []
        for a in range(n):
            for k, (fx, fy) in enumerate(_CHIP_FLIPS):
                px, py = _flip(x, fx), _flip(y, fy)
                remote.append((ins[a].at[2 * px + py], outs[a].at[k], (px, py, c), outs[a].at[k], None))
        return [], remote

    got2 = _exchange("rs_chips", part, [jax.ShapeDtypeStruct((NCHIP - 1,) + h, BF16) for h in half], 0, 3 * n, plan2)
    qc_idx = jnp.concatenate([q_idx, c_idx])
    red = [_chip_add(f"rs_chip_add{a}", part[a], got2[a], qc_idx, _tile(half[a][0], 256)) for a in range(n)]

    def plan3(ins, outs, x, y, c):
        sib = (x, y, 1 - c)
        return [], [(ins[a].at[c], outs[a].at[c], sib, outs[a].at[1 - c], None) for a in range(n)]

    full = _exchange("rs_share", red, [jax.ShapeDtypeStruct((2,) + h, F32) for h in half], 0, n, plan3,
                     aliases={a: a for a in range(n)})
    return [f.reshape(2 * h[0], h[1]) for f, h in zip(full, half)]


def _to_bf16_stack(name, w, q_idx):
    r, cols = w.shape
    half = r // 2
    tr = _tile(half, 256)
    nb = half // tr

    def body(q_ref, w_ref, o_ref):
        o_ref[...] = w_ref[...].astype(BF16)

    return pl.pallas_call(
        body, name=name,
        grid_spec=pltpu.PrefetchScalarGridSpec(
            num_scalar_prefetch=1, grid=(2 * nb,),
            in_specs=[pl.BlockSpec((tr, cols), lambda i, qr: (i, 0))],
            out_specs=pl.BlockSpec((None, None, tr, cols), lambda i, qr: (qr[0], i // nb, i % nb, 0))),
        out_shape=jax.ShapeDtypeStruct((NCHIP, 2, half, cols), BF16),
        compiler_params=_cparams("parallel"),
    )(q_idx, w)


def kernel(x, c, ada_w, ada_b, norm_ffn1, ffn1_w_gate, ffn1_w_up, ffn1_w_down, norm_mix, mix_w_in, mix_b_in, sgu_ln_g, sgu_ln_b, sgu_w_s, sgu_b_s, conv_w, conv_b, conv_ln_g, conv_ln_b, w_branch_a, w_branch_b, w_out, norm_ffn2, ffn2_w_gate, ffn2_w_up, ffn2_w_down, norm_final, loss_target, m_ada_w, m_ada_b, m_norm_ffn1, m_ffn1_w_gate, m_ffn1_w_up, m_ffn1_w_down, m_norm_mix, m_mix_w_in, m_mix_b_in, m_sgu_ln_g, m_sgu_ln_b, m_sgu_w_s, m_sgu_b_s, m_conv_w, m_conv_b, m_conv_ln_g, m_conv_ln_b, m_w_branch_a, m_w_branch_b, m_w_out, m_norm_ffn2, m_ffn2_w_gate, m_ffn2_w_up, m_ffn2_w_down, m_norm_final, v_ada_w, v_ada_b, v_norm_ffn1, v_ffn1_w_gate, v_ffn1_w_up, v_ffn1_w_down, v_norm_mix, v_mix_w_in, v_mix_b_in, v_sgu_ln_g, v_sgu_ln_b, v_sgu_w_s, v_sgu_b_s, v_conv_w, v_conv_b, v_conv_ln_g, v_conv_ln_b, v_w_branch_a, v_w_branch_b, v_w_out, v_norm_ffn2, v_ffn2_w_gate, v_ffn2_w_up, v_ffn2_w_down, v_norm_final):
    xi, yi, ci = lax.axis_index("x"), lax.axis_index("y"), lax.axis_index("c")
    q = 2 * xi + yi
    dev = 4 * xi + 2 * yi + ci
    c_idx = jnp.reshape(ci, (1,)).astype(jnp.int32)
    q_idx = jnp.reshape(q, (1,)).astype(jnp.int32)
    x0 = x[0]
    tgt = loss_target[0]
    seq = x0.shape[0]
    n_ada = ada_w.shape[2]

    c_all = _allgather8("ag_c", jnp.pad(c, ((0, 7), (0, 0))))[::8]
    ada_b_mine = lax.dynamic_slice(ada_b, (0, q * n_ada), (1, n_ada))
    mod_part = _ada_fwd("ada_fwd", c_all, ada_w[0], ada_b_mine)

    big = dict(g1=ffn1_w_gate, u1=ffn1_w_up, d1=ffn1_w_down, win=mix_w_in, wa=w_branch_a, wb=w_branch_b, wo=w_out,
               g2=ffn2_w_gate, u2=ffn2_w_up, d2=ffn2_w_down)
    big_names = list(big)
    stacks = [_to_bf16_stack(f"cast_{k}", big[k][0], q_idx) for k in big_names]
    gathered = _gather_chips("gather_w", stacks, [mod_part, jnp.pad(conv_w[0], ((0, 1), (0, 0)))])
    W = {k: g.reshape(NCHIP, 2 * g.shape[2], g.shape[3]) for k, g in zip(big_names, gathered)}
    mod_all = jnp.transpose(gathered[-2], (1, 0, 2)).reshape(8, N_MOD * D)
    mod = lax.dynamic_slice(mod_all, (dev, 0), (1, N_MOD * D))
    sh1, sc1, g1, sh2, sc2, g2, sh3, sc3, g3 = [mod[:, k * D:(k + 1) * D] for k in range(N_MOD)]
    convw = jnp.transpose(gathered[-1], (1, 0, 2)).reshape(HALO, D)

    causal = jnp.tril(jnp.ones((CHUNK, CHUNK), bool))
    wm_f = jnp.where(causal[None], sgu_w_s[0], 0.0)
    wm = wm_f.astype(BF16)
    wmt = jnp.swapaxes(wm_f, 1, 2).astype(BF16)
    bst = jnp.repeat(jnp.transpose(sgu_b_s[0]), CHUNK, axis=1)
    b_in = mix_b_in

    (h1,) = _rowwise_fwd("pre1", _f_pre, [x0], [norm_ffn1, sc1, sh1], [(D, BF16)], 256)
    a1, b1, s1 = _ffn_up("up1", h1, W["g1"], W["u1"])
    f1 = _matmul("down1", s1, W["d1"], "fwd", False, F32)
    x1, h2 = _rowwise_fwd("res1", _f_res_pre(0.5), [x0, f1], [g1, norm_mix, sc2, sh2], [(D, F32), (D, BF16)], 256)
    proj = _matmul("in_proj", h2, W["win"], "fwd", True, F32)
    ya = _sgu_fwd("sgu", proj, b_in, sgu_ln_g, sgu_ln_b, wm, bst, 256)
    z1 = _conv_fwd("conv", proj, b_in, convw, conv_b)
    (z3,) = _rowwise_fwd("conv_ln", _f_lnsilu, [z1], [conv_ln_g, conv_ln_b], [(D, BF16)], 256)
    y_a = _matmul("branch_a", ya, W["wa"], "fwd", False, F32)
    y_b = _matmul("branch_b", z3, W["wb"], "fwd", False, F32)
    (merged,) = _rowwise_fwd("merge", _f_merge, [(proj, 2 * D, 2), y_a, y_b], [(b_in, 2 * D, 2)], [(D, BF16)], 256)
    y = _matmul("out_proj", merged, W["wo"], "fwd", False, F32)
    x2, h3 = _rowwise_fwd("res2", _f_res_pre(1.0), [x1, y], [g2, norm_ffn2, sc3, sh3], [(D, F32), (D, BF16)], 256)
    a3, b3, s3 = _ffn_up("up2", h3, W["g2"], W["u2"])
    f3 = _matmul("down2", s3, W["d2"], "fwd", False, F32)
    dx2, df3, dg3, dnwf, loss_blk = _final("final", x2, f3, tgt, g3, norm_final.reshape(1, D), 256)

    G = {}

    def ffn_bwd(tag, df, s, a, b, h, wg, wu, wd):
        ds = _matmul(f"d_down{tag}", df, wd, "dgrad", False, F32)
        G["d" + tag] = _matmul(f"w_down{tag}", s, df, "wgrad", False, F32)
        (da, db), _ = _rowwise_bwd(f"glu_b{tag}", _f_glu, [a, b], [], [ds], [BF16, BF16], 128)
        dh = _matmul(f"d_gate{tag}", da, wg, "dgrad", True, F32)
        dh = _matmul(f"d_up{tag}", db, wu, "dgrad", True, F32, add=dh)
        G["g" + tag] = _matmul(f"w_gate{tag}", h, da, "wgrad", True, F32)
        G["u" + tag] = _matmul(f"w_up{tag}", h, db, "wgrad", True, F32)
        return dh

    dh3 = ffn_bwd("2", df3, s3, a3, b3, h3, W["g2"], W["u2"], W["d2"])
    (dx1, dy), (dg2, dnw3, dsc3, dsh3) = _rowwise_bwd(
        "res2_b", _f_res_pre(1.0), [x1, y], [g2, norm_ffn2, sc3, sh3], [dx2, dh3], [F32, BF16], 256)
    dmerged = _matmul("d_out", dy, W["wo"], "dgrad", False, F32)
    G["wo"] = _matmul("w_out", merged, dy, "wgrad", False, F32)
    (dgab, dy_a, dy_b), (dbgab,) = _rowwise_bwd(
        "merge_b", _f_merge, [(proj, 2 * D, 2), y_a, y_b], [(b_in, 2 * D, 2)], [dmerged], [BF16, BF16, BF16], 256)
    dya = _matmul("d_a", dy_a, W["wa"], "dgrad", False, F32)
    dz3 = _matmul("d_b", dy_b, W["wb"], "dgrad", False, F32)
    G["wa"] = _matmul("w_a", ya, dy_a, "wgrad", False, F32)
    G["wb"] = _matmul("w_b", z3, dy_b, "wgrad", False, F32)
    (dz1,), (dclg, dclb) = _rowwise_bwd("conv_ln_b", _f_lnsilu, [z1], [conv_ln_g, conv_ln_b], [dz3], [F32], 256)
    dcv, dcg, conv_small = _conv_bwd("conv_b", dz1, proj, b_in, convw)
    duv, dbuv, dlng, dlnb, dws, dbs = _sgu_bwd("sgu_b", proj, b_in, sgu_ln_g, sgu_ln_b, wm, wmt, bst, dya, 256)
    dproj = jnp.concatenate([duv, dcv, dcg, dgab], axis=1)
    dh2 = _matmul("d_in", dproj, W["win"], "dgrad", True, F32)
    G["win"] = _matmul("w_in", h2, dproj, "wgrad", True, F32)
    (dx0, df1), (dg1, dnw2, dsc2, dsh2) = _rowwise_bwd(
        "res1_b", _f_res_pre(0.5), [x0, f1], [g1, norm_mix, sc2, sh2], [dx1, dh2], [F32, BF16], 256)
    dh1 = ffn_bwd("1", df1, s1, a1, b1, h1, W["g1"], W["u1"], W["d1"])
    (grad_x,), (dnw1, dsc1, dsh1) = _rowwise_bwd(
        "pre1_b", _f_pre_keep, [x0], [norm_ffn1, sc1, sh1], [dx0, dh1], [F32], 256)

    db_in = jnp.concatenate([dbuv, conv_small[33:34], conv_small[34:35], dbgab], axis=1)
    dbs_row = jnp.transpose(dbs[:, ::CHUNK]).reshape(1, D)
    loss_row = jnp.pad(loss_blk[0:1, :], ((0, 0), (0, D - 128)))
    small = jnp.concatenate([
        dnw1, dnw2, db_in.reshape(6, D), dlng, dlnb, dws.reshape(CHUNK, D), dbs_row, conv_small[0:KCONV],
        conv_small[32:33], dclg, dclb, dnw3, dnwf,
        dsh1, dsc1, dg1, dsh2, dsc2, dg2, dsh3, dsc3, dg3, loss_row, jnp.zeros((7, D), F32)], axis=0)
    n_small = small.shape[0]
    every = _allgather8("ag_small", small).reshape(8, n_small, D)
    tot = _sum8("sum_small", every, 64)
    r_mod = 1 + 1 + 6 + 1 + 1 + CHUNK + 1 + KCONV + 1 + 1 + 1 + 1 + 1
    loss = tot[r_mod + N_MOD, 0]
    dmod_all = every[:, r_mod:r_mod + N_MOD, :].reshape(8, N_MOD * D)
    dmod_mine = lax.dynamic_slice(dmod_all, (0, q * n_ada), (8, n_ada))
    g_ada_w = _ada_wgrad("ada_wgrad", jnp.transpose(c_all), dmod_mine)

    reduced = dict(zip(big_names, _reduce_scatter([G[k] for k in big_names], c_idx, q_idx)))

    given = dict(ada_w=(ada_w, m_ada_w, v_ada_w), ada_b=(ada_b, m_ada_b, v_ada_b),
                 norm_ffn1=(norm_ffn1, m_norm_ffn1, v_norm_ffn1),
                 ffn1_w_gate=(ffn1_w_gate, m_ffn1_w_gate, v_ffn1_w_gate), ffn1_w_up=(ffn1_w_up, m_ffn1_w_up, v_ffn1_w_up),
                 ffn1_w_down=(ffn1_w_down, m_ffn1_w_down, v_ffn1_w_down), norm_mix=(norm_mix, m_norm_mix, v_norm_mix),
                 mix_w_in=(mix_w_in, m_mix_w_in, v_mix_w_in), mix_b_in=(mix_b_in, m_mix_b_in, v_mix_b_in),
                 sgu_ln_g=(sgu_ln_g, m_sgu_ln_g, v_sgu_ln_g), sgu_ln_b=(sgu_ln_b, m_sgu_ln_b, v_sgu_ln_b),
                 sgu_w_s=(sgu_w_s, m_sgu_w_s, v_sgu_w_s), sgu_b_s=(sgu_b_s, m_sgu_b_s, v_sgu_b_s),
                 conv_w=(conv_w, m_conv_w, v_conv_w), conv_b=(conv_b, m_conv_b, v_conv_b),
                 conv_ln_g=(conv_ln_g, m_conv_ln_g, v_conv_ln_g), conv_ln_b=(conv_ln_b, m_conv_ln_b, v_conv_ln_b),
                 w_branch_a=(w_branch_a, m_w_branch_a, v_w_branch_a), w_branch_b=(w_branch_b, m_w_branch_b, v_w_branch_b),
                 w_out=(w_out, m_w_out, v_w_out), norm_ffn2=(norm_ffn2, m_norm_ffn2, v_norm_ffn2),
                 ffn2_w_gate=(ffn2_w_gate, m_ffn2_w_gate, v_ffn2_w_gate), ffn2_w_up=(ffn2_w_up, m_ffn2_w_up, v_ffn2_w_up),
                 ffn2_w_down=(ffn2_w_down, m_ffn2_w_down, v_ffn2_w_down), norm_final=(norm_final, m_norm_final, v_norm_final))
    out = {}

    def big_update(name, g2d):
        w, m, v = given[name]
        shp = w.shape
        d_, nm, nv = _adamw(f"adamw_{name}", w.reshape(g2d.shape), g2d, m.reshape(g2d.shape), v.reshape(g2d.shape),
                            _tile(g2d.shape[0], 256))
        out[name] = tuple(t.reshape(shp) for t in (g2d, d_, nm, nv))

    for key, name in (("g1", "ffn1_w_gate"), ("u1", "ffn1_w_up"), ("d1", "ffn1_w_down"), ("win", "mix_w_in"),
                      ("wa", "w_branch_a"), ("wb", "w_branch_b"), ("wo", "w_out"),
                      ("g2", "ffn2_w_gate"), ("u2", "ffn2_w_up"), ("d2", "ffn2_w_down")):
        big_update(name, reduced[key])
    big_update("ada_w", g_ada_w)

    small_names = [("norm_ffn1", 1), ("norm_mix", 1), ("mix_b_in", 6), ("sgu_ln_g", 1), ("sgu_ln_b", 1),
                   ("sgu_w_s", CHUNK), ("sgu_b_s", 1), (None, KCONV), ("conv_b", 1), ("conv_ln_g", 1), ("conv_ln_b", 1),
                   ("norm_ffn2", 1), ("norm_final", 1), ("ada_b", N_MOD), (None, 8)]

    def pack(which):
        return jnp.concatenate([jnp.zeros((r, D), F32) if nm is None else given[nm][which].reshape(r, D)
                                for nm, r in small_names], axis=0)

    sd, sm, sv = _adamw("adamw_small", pack(0), tot, pack(1), pack(2), 64)
    row = 0
    for nm, r in small_names:
        if nm is not None:
            shp = given[nm][0].shape
            out[nm] = tuple(t[row:row + r].reshape(shp) for t in (tot, sd, sm, sv))
        row += r
    r_cw = 1 + 1 + 6 + 1 + 1 + CHUNK + 1
    n_cw = conv_w.shape[2]
    g_cw = jnp.pad(lax.dynamic_slice(tot[r_cw:r_cw + KCONV], (0, q * n_cw), (KCONV, n_cw)), ((0, 1), (0, 0)))
    pad_cw = lambda t: jnp.pad(t[0], ((0, 1), (0, 0)))
    cd, cm, cv_ = _adamw("adamw_conv_w", pad_cw(conv_w), g_cw, pad_cw(m_conv_w), pad_cw(v_conv_w), HALO)
    out["conv_w"] = tuple(t[:KCONV][None] for t in (g_cw, cd, cm, cv_))

    order = ["ada_w", "ada_b", "norm_ffn1", "ffn1_w_gate", "ffn1_w_up", "ffn1_w_down", "norm_mix", "mix_w_in", "mix_b_in",
             "sgu_ln_g", "sgu_ln_b", "sgu_w_s", "sgu_b_s", "conv_w", "conv_b", "conv_ln_g", "conv_ln_b", "w_branch_a",
             "w_branch_b", "w_out", "norm_ffn2", "ffn2_w_gate", "ffn2_w_up", "ffn2_w_down", "norm_final"]
    return (loss, grad_x[None], *[out[n][0] for n in order], *[out[n][1] for n in order],
            *[out[n][2] for n in order], *[out[n][3] for n in order])
```

```python
import functools

import jax
import jax.numpy as jnp
from jax import lax
from jax.experimental import pallas as pl
from jax.experimental.pallas import tpu as pltpu

F32 = jnp.float32
BF16 = jnp.bfloat16
D = 1024
HEADS = 8
CHUNK = 128
KCONV = 31
HALO = 32
N_MOD = 9
EPS = 1e-6
NCHIP = 4
VMEM_LIMIT = 56 * 1024 * 1024
MESH = pl.DeviceIdType.MESH

ADAM_LR = 0.001
ADAM_B1 = 0.9
ADAM_B2 = 0.999
ADAM_EPS = 1e-08
ADAM_WD = 0.01
ADAM_STEP = 10

NN = (((1,), (0,)), ((), ()))
NT = (((1,), (1,)), ((), ()))
TN = (((0,), (0,)), ((), ()))


def _cparams(*sem):
    return pltpu.CompilerParams(dimension_semantics=sem or None, vmem_limit_bytes=VMEM_LIMIT)


def _rms(x, w):
    return x * lax.rsqrt(jnp.mean(x * x, axis=-1, keepdims=True) + EPS) * w


def _ln(x, g, b):
    mu = jnp.mean(x, axis=-1, keepdims=True)
    xc = x - mu
    var = jnp.mean(xc * xc, axis=-1, keepdims=True)
    return xc * lax.rsqrt(var + EPS) * g + b


def _silu(x):
    return x * jax.nn.sigmoid(x)


def _gelu(x):
    return x * (0.5 * (1.0 + jnp.tanh(0.7978845608028654 * (x + 0.044715 * (x * x * x)))))


def _f_pre(x, nw, sc, sh):
    return _rms(x, nw) * (1.0 + sc) + sh


def _f_pre_keep(x, nw, sc, sh):
    return x, _f_pre(x, nw, sc, sh)


def _f_res_pre(scale):
    def fn(x, f, g, nw, sc, sh):
        xn = x + (scale * g) * f
        return xn, _f_pre(xn, nw, sc, sh)
    return fn


def _f_glu(a, b):
    return _silu(a) * b


def _f_merge(gab, ya, yb, bgab):
    g = gab + bgab
    return jax.nn.sigmoid(g[:, :D]) * ya + jax.nn.sigmoid(g[:, D:]) * yb


def _f_lnsilu(z, g, b):
    return _silu(_ln(z, g, b))


def _f_mixa(uv, buv, g, b):
    t = uv + buv
    return _gelu(t[:, :D]), _ln(_gelu(t[:, D:]), g, b)


def _col(t):
    return t if isinstance(t, tuple) else (t, t.shape[-1], 0)


def _rowwise_fwd(name, fn, rows, params, outs, tr):
    rows = [_col(r) for r in rows]
    params = [_col(p) for p in params]
    n_rows = rows[0][0].shape[0]
    nin = len(rows) + len(params)

    def body(*refs):
        res = fn(*[r[...].astype(F32) for r in refs[:nin]])
        res = res if isinstance(res, tuple) else (res,)
        for o, v in zip(refs[nin:], res, strict=True):
            o[...] = v.astype(o.dtype)

    return pl.pallas_call(
        body, name=name, grid=(n_rows // tr,),
        in_specs=[pl.BlockSpec((tr, w), functools.partial(lambda cb, i: (i, cb), cb)) for _, w, cb in rows]
        + [pl.BlockSpec((1, w), functools.partial(lambda cb, i: (0, cb), cb)) for _, w, cb in params],
        out_specs=[pl.BlockSpec((tr, w), lambda i: (i, 0)) for w, _ in outs],
        out_shape=[jax.ShapeDtypeStruct((n_rows, w), dt) for w, dt in outs],
        compiler_params=_cparams("parallel"),
    )(*[r[0] for r in rows], *[p[0] for p in params])


def _rowwise_bwd(name, fn, rows, params, cots, row_grads, tr):
    rows = [_col(r) for r in rows]
    params = [_col(p) for p in params]
    cots = [_col(t) for t in cots]
    n_rows = rows[0][0].shape[0]
    nr, npar, nc = len(rows), len(params), len(cots)
    nin = nr + npar + nc
    n_rg = sum(dt is not None for dt in row_grads)

    def body(*refs):
        i = pl.program_id(0)
        prim = [r[...].astype(F32) for r in refs[:nr + npar]]
        ct = tuple(r[...].astype(F32) for r in refs[nr + npar:nin])
        _, vjp = jax.vjp(fn, *prim)
        g = vjp(ct if nc > 1 else ct[0])
        outs = refs[nin:]
        oi = 0
        for j, dt in enumerate(row_grads):
            if dt is not None:
                outs[oi][...] = g[j].astype(dt)
                oi += 1
        for j in range(npar):
            acc = outs[n_rg + j]

            @pl.when(i == 0)
            def _(acc=acc):
                acc[...] = jnp.zeros(acc.shape, F32)

            acc[...] += g[nr + j]

    res = pl.pallas_call(
        body, name=name, grid=(n_rows // tr,),
        in_specs=[pl.BlockSpec((tr, w), functools.partial(lambda cb, i: (i, cb), cb)) for _, w, cb in rows]
        + [pl.BlockSpec((1, w), functools.partial(lambda cb, i: (0, cb), cb)) for _, w, cb in params]
        + [pl.BlockSpec((tr, w), functools.partial(lambda cb, i: (i, cb), cb)) for _, w, cb in cots],
        out_specs=[pl.BlockSpec((tr, rows[j][1]), lambda i: (i, 0)) for j, dt in enumerate(row_grads) if dt is not None]
        + [pl.BlockSpec((1, w), lambda i: (0, 0)) for _, w, _ in params],
        out_shape=[jax.ShapeDtypeStruct((n_rows, rows[j][1]), dt) for j, dt in enumerate(row_grads) if dt is not None]
        + [jax.ShapeDtypeStruct((1, w), F32) for _, w, _ in params],
        compiler_params=_cparams("arbitrary"),
    )(*[r[0] for r in rows], *[p[0] for p in params], *[t[0] for t in cots])
    return res[:n_rg], res[n_rg:]


def _final(name, x, f, tgt, g, nw, tr):
    n_rows = x.shape[0]

    def body(x_ref, f_ref, t_ref, g_ref, nw_ref, dx_ref, df_ref, dg_ref, dnw_ref, loss_ref):
        i = pl.program_id(0)
        tg = t_ref[...]

        def fn(xv, fv, gv, nwv):
            e = _rms(xv + (0.5 * gv) * fv, nwv) - tg
            return 0.5 * jnp.mean(e * e, axis=-1, keepdims=True)

        per_row, vjp = jax.vjp(fn, x_ref[...], f_ref[...], g_ref[...], nw_ref[...])
        dx, df, dg, dnw = vjp(jnp.ones_like(per_row))
        dx_ref[...] = dx
        df_ref[...] = df.astype(df_ref.dtype)

        @pl.when(i == 0)
        def _():
            dg_ref[...] = jnp.zeros(dg_ref.shape, F32)
            dnw_ref[...] = jnp.zeros(dnw_ref.shape, F32)
            loss_ref[...] = jnp.zeros(loss_ref.shape, F32)

        dg_ref[...] += dg
        dnw_ref[...] += dnw
        loss_ref[...] += jnp.broadcast_to(jnp.sum(per_row, axis=0, keepdims=True), loss_ref.shape)

    row = pl.BlockSpec((tr, D), lambda i: (i, 0))
    par = pl.BlockSpec((1, D), lambda i: (0, 0))
    return pl.pallas_call(
        body, name=name, grid=(n_rows // tr,),
        in_specs=[row, row, row, par, par],
        out_specs=[row, row, par, par, pl.BlockSpec((8, 128), lambda i: (0, 0))],
        out_shape=[jax.ShapeDtypeStruct((n_rows, D), F32), jax.ShapeDtypeStruct((n_rows, D), BF16),
                   jax.ShapeDtypeStruct((1, D), F32), jax.ShapeDtypeStruct((1, D), F32),
                   jax.ShapeDtypeStruct((8, 128), F32)],
        compiler_params=_cparams("arbitrary"),
    )(x, f, tgt, g, nw)


def _tile(dim, pref):
    t = min(dim, pref)
    while dim % t:
        t //= 2
    return t


def _matmul(name, a, b, kind, col, out_dtype, add=None):
    if kind == "wgrad":
        m, kg = a.shape
        ng = b.shape[1]
        r, c = (kg, ng // NCHIP) if col else (kg // NCHIP, ng)
        tm, tn, tk = _tile(r, 1024), _tile(c, 1536), _tile(m, 2048)
        grid = (kg // tm, ng // tn, m // tk)
        a_spec = pl.BlockSpec((tk, tm), lambda i, j, k: (k, i))
        b_spec = pl.BlockSpec((tk, tn), lambda i, j, k: (k, j))
        if col:
            o_spec = pl.BlockSpec((None, tm, tn), lambda i, j, k: (j // (c // tn), i, j % (c // tn)))
        else:
            o_spec = pl.BlockSpec((None, tm, tn), lambda i, j, k: (i // (r // tm), i % (r // tm), j))
        out_shape = jax.ShapeDtypeStruct((NCHIP, r, c), F32)
        dims = TN
    else:
        _, r, c = b.shape
        m = a.shape[0]
        kg, ng = (r, NCHIP * c) if col else (NCHIP * r, c)
        tm = _tile(m, 1024)
        if kind == "fwd":
            tn, tk = _tile(c, 1536), _tile(r, 2048)
            grid = (m // tm, ng // tn, kg // tk)
            a_spec = pl.BlockSpec((tm, tk), lambda i, j, k: (i, k))
            if col:
                b_spec = pl.BlockSpec((None, tk, tn), lambda i, j, k: (j // (c // tn), k, j % (c // tn)))
            else:
                b_spec = pl.BlockSpec((None, tk, tn), lambda i, j, k: (k // (r // tk), k % (r // tk), j))
            out_shape = jax.ShapeDtypeStruct((m, ng), out_dtype)
            dims = NN
        else:
            tn, tk = _tile(r, 1024), _tile(c, 2048)
            grid = (m // tm, kg // tn, ng // tk)
            a_spec = pl.BlockSpec((tm, tk), lambda i, j, k: (i, k))
            if col:
                b_spec = pl.BlockSpec((None, tn, tk), lambda i, j, k: (k // (c // tk), j, k % (c // tk)))
            else:
                b_spec = pl.BlockSpec((None, tn, tk), lambda i, j, k: (j // (r // tn), j % (r // tn), k))
            out_shape = jax.ShapeDtypeStruct((m, kg), out_dtype)
            dims = NT
        o_spec = pl.BlockSpec((tm, tn), lambda i, j, k: (i, j))
    nk = grid[2]
    has_add = add is not None

    def body(*refs):
        a_ref, b_ref = refs[0], refs[1]
        o_ref = refs[3] if has_add else refs[2]
        k = pl.program_id(2)
        p = lax.dot_general(a_ref[...].astype(BF16), b_ref[...].astype(BF16), dims, preferred_element_type=F32)

        def finish(r_):
            if has_add:
                r_ = r_ + refs[2][...]
            o_ref[...] = r_.astype(o_ref.dtype)

        if nk == 1:
            finish(p)
        else:
            acc_ref = refs[-1]

            @pl.when(k == 0)
            def _():
                acc_ref[...] = p

            @pl.when(jnp.logical_and(k > 0, k < nk - 1))
            def _():
                acc_ref[...] += p

            @pl.when(k == nk - 1)
            def _():
                finish(acc_ref[...] + p)

    in_specs = [a_spec, b_spec] + ([pl.BlockSpec((tm, tn), lambda i, j, k: (i, j))] if has_add else [])
    args = (a, b) + ((add,) if has_add else ())
    return pl.pallas_call(
        body, name=name, grid=grid, in_specs=in_specs, out_specs=o_spec, out_shape=out_shape,
        scratch_shapes=[pltpu.VMEM((tm, tn), F32)] if nk > 1 else [],
        compiler_params=_cparams("parallel", "parallel", "arbitrary"),
    )(*args)


def _ffn_up(name, h, wg, wu):
    m, k = h.shape
    _, _, c = wg.shape
    n = NCHIP * c
    tm, tn = _tile(m, 1024), _tile(c, 1024)

    def body(h_ref, wg_ref, wu_ref, a_ref, b_ref, s_ref):
        hv = h_ref[...]
        a = jnp.dot(hv, wg_ref[...], preferred_element_type=F32)
        b = jnp.dot(hv, wu_ref[...], preferred_element_type=F32)
        a_ref[...] = a
        b_ref[...] = b
        s_ref[...] = (_silu(a) * b).astype(s_ref.dtype)

    w_spec = pl.BlockSpec((None, k, tn), lambda i, j: (j // (c // tn), 0, j % (c // tn)))
    o_spec = pl.BlockSpec((tm, tn), lambda i, j: (i, j))
    return pl.pallas_call(
        body, name=name, grid=(m // tm, n // tn),
        in_specs=[pl.BlockSpec((tm, k), lambda i, j: (i, 0)), w_spec, w_spec],
        out_specs=[o_spec, o_spec, o_spec],
        out_shape=[jax.ShapeDtypeStruct((m, n), F32), jax.ShapeDtypeStruct((m, n), F32),
                   jax.ShapeDtypeStruct((m, n), BF16)],
        compiler_params=_cparams("parallel", "parallel"),
    )(h, wg, wu)


def _dgrad_glu(name, df, wd, a, b):
    m = df.shape[0]
    _, r, c = wd.shape
    tm, tn = _tile(m, 512), _tile(r, 1024)

    def body(df_ref, w_ref, a_ref, b_ref, da_ref, db_ref):
        ds = lax.dot_general(df_ref[...], w_ref[...], NT, preferred_element_type=F32)
        _, vjp = jax.vjp(_f_glu, a_ref[...], b_ref[...])
        da, db = vjp(ds)
        da_ref[...] = da.astype(da_ref.dtype)
        db_ref[...] = db.astype(db_ref.dtype)

    t_spec = pl.BlockSpec((tm, tn), lambda i, j: (i, j))
    sds = jax.ShapeDtypeStruct((m, NCHIP * r), BF16)
    return pl.pallas_call(
        body, name=name, grid=(m // tm, NCHIP * r // tn),
        in_specs=[pl.BlockSpec((tm, c), lambda i, j: (i, 0)),
                  pl.BlockSpec((None, tn, c), lambda i, j: (j // (r // tn), j % (r // tn), 0)), t_spec, t_spec],
        out_specs=[t_spec, t_spec], out_shape=[sds, sds],
        compiler_params=_cparams("parallel", "parallel"),
    )(df, wd, a, b)


def _sgu_fwd(name, proj, b_in, lng, lnb, wm, bst, tr):
    n_rows = proj.shape[0]

    def body(uv_ref, buv_ref, g_ref, b_ref, wm_ref, bst_ref, ya_ref):
        u, vln = _f_mixa(uv_ref[...], buv_ref[...], g_ref[...], b_ref[...])
        vb = vln.astype(BF16)
        for ck in range(tr // CHUNK):
            rs = slice(ck * CHUNK, (ck + 1) * CHUNK)
            for h in range(HEADS):
                cs = slice(h * CHUNK, (h + 1) * CHUNK)
                vm = jnp.dot(wm_ref[h], vb[rs, cs], preferred_element_type=F32) + bst_ref[:, cs]
                ya_ref[rs, cs] = (u[rs, cs] * vm).astype(ya_ref.dtype)

    return pl.pallas_call(
        body, name=name, grid=(n_rows // tr,),
        in_specs=[pl.BlockSpec((tr, 2 * D), lambda i: (i, 0)), pl.BlockSpec((1, 2 * D), lambda i: (0, 0)),
                  pl.BlockSpec((1, D), lambda i: (0, 0)), pl.BlockSpec((1, D), lambda i: (0, 0)),
                  pl.BlockSpec((HEADS, CHUNK, CHUNK), lambda i: (0, 0, 0)),
                  pl.BlockSpec((CHUNK, D), lambda i: (0, 0))],
        out_specs=pl.BlockSpec((tr, D), lambda i: (i, 0)),
        out_shape=jax.ShapeDtypeStruct((n_rows, D), BF16),
        compiler_params=_cparams("parallel"),
    )(proj, b_in, lng, lnb, wm, bst)


def _sgu_bwd(name, proj, b_in, lng, lnb, wm, wmt, bst, dya, tr):
    n_rows = proj.shape[0]

    def body(uv_ref, buv_ref, g_ref, b_ref, wm_ref, wmt_ref, bst_ref, dya_ref,
             duv_ref, dbuv_ref, dg_ref, db_ref, dws_ref, dbs_ref, du_s, dvln_s):
        i = pl.program_id(0)

        @pl.when(i == 0)
        def _():
            dbuv_ref[...] = jnp.zeros(dbuv_ref.shape, F32)
            dg_ref[...] = jnp.zeros(dg_ref.shape, F32)
            db_ref[...] = jnp.zeros(db_ref.shape, F32)
            dws_ref[...] = jnp.zeros(dws_ref.shape, F32)
            dbs_ref[...] = jnp.zeros(dbs_ref.shape, F32)

        (u, vln), vjp = jax.vjp(_f_mixa, uv_ref[...], buv_ref[...], g_ref[...], b_ref[...])
        vb = vln.astype(BF16)
        dya_v = dya_ref[...].astype(F32)
        tpos = lax.broadcasted_iota(jnp.int32, (CHUNK, CHUNK), 0)
        spos = lax.broadcasted_iota(jnp.int32, (CHUNK, CHUNK), 1)
        causal = (tpos >= spos).astype(F32)
        for ck in range(tr // CHUNK):
            rs = slice(ck * CHUNK, (ck + 1) * CHUNK)
            for h in range(HEADS):
                cs = slice(h * CHUNK, (h + 1) * CHUNK)
                vbh = vb[rs, cs]
                vm = jnp.dot(wm_ref[h], vbh, preferred_element_type=F32) + bst_ref[:, cs]
                dyb = dya_v[rs, cs]
                du_s[rs, cs] = dyb * vm
                dvm = dyb * u[rs, cs]
                dvmb = dvm.astype(BF16)
                dvln_s[rs, cs] = jnp.dot(wmt_ref[h], dvmb, preferred_element_type=F32)
                dws_ref[h] += causal * lax.dot_general(dvmb, vbh, NT, preferred_element_type=F32)
                dbs_ref[:, cs] += jnp.broadcast_to(jnp.sum(dvm, axis=1, keepdims=True), (CHUNK, CHUNK))
        duv, dbuv, dg, db = vjp((du_s[...], dvln_s[...]))
        duv_ref[...] = duv.astype(duv_ref.dtype)
        dbuv_ref[...] += dbuv
        dg_ref[...] += dg
        db_ref[...] += db

    par = pl.BlockSpec((1, D), lambda i: (0, 0))
    par2 = pl.BlockSpec((1, 2 * D), lambda i: (0, 0))
    w_spec = pl.BlockSpec((HEADS, CHUNK, CHUNK), lambda i: (0, 0, 0))
    b_spec = pl.BlockSpec((CHUNK, D), lambda i: (0, 0))
    return pl.pallas_call(
        body, name=name, grid=(n_rows // tr,),
        in_specs=[pl.BlockSpec((tr, 2 * D), lambda i: (i, 0)), par2, par, par, w_spec, w_spec, b_spec,
                  pl.BlockSpec((tr, D), lambda i: (i, 0))],
        out_specs=[pl.BlockSpec((tr, 2 * D), lambda i: (i, 0)), par2, par, par, w_spec, b_spec],
        out_shape=[jax.ShapeDtypeStruct((n_rows, 2 * D), BF16), jax.ShapeDtypeStruct((1, 2 * D), F32),
                   jax.ShapeDtypeStruct((1, D), F32), jax.ShapeDtypeStruct((1, D), F32),
                   jax.ShapeDtypeStruct((HEADS, CHUNK, CHUNK), F32), jax.ShapeDtypeStruct((CHUNK, D), F32)],
        scratch_shapes=[pltpu.VMEM((tr, D), F32), pltpu.VMEM((tr, D), F32)],
        compiler_params=_cparams("arbitrary"),
    )(proj, b_in, lng, lnb, wm, wmt, bst, dya)


CT = 256
RB = 128
CV0 = 2 * D // CT
CG0 = 3 * D // CT


def _shifted(win, sh):
    if sh == 0:
        return win[0:RB]
    return pltpu.roll(win, RB + HALO - sh, 0)[0:RB]


def _conv_fwd(name, proj, b_in, w, cb):
    n_rows = proj.shape[0]

    def body(cv_ref, cg_ref, bcv_ref, bcg_ref, w_ref, cb_ref, z1_ref, zp_ref):
        zp_ref[0:HALO, :] = jnp.zeros((HALO, CT), F32)
        zp_ref[HALO:, :] = (cv_ref[...] + bcv_ref[...]) * jax.nn.sigmoid(cg_ref[...] + bcg_ref[...])

        def blk(rb, carry):
            base = pl.multiple_of(rb * RB, RB)
            win = zp_ref[pl.ds(base, RB + HALO), :]
            acc = jnp.broadcast_to(cb_ref[...], (RB, CT))
            for k in range(KCONV):
                acc = acc + w_ref[k:k + 1, :] * _shifted(win, k + 2)
            z1_ref[pl.ds(base, RB), :] = acc
            return carry

        lax.fori_loop(0, n_rows // RB, blk, 0)

    return pl.pallas_call(
        body, name=name, grid=(D // CT,),
        in_specs=[pl.BlockSpec((n_rows, CT), lambda j: (0, CV0 + j)), pl.BlockSpec((n_rows, CT), lambda j: (0, CG0 + j)),
                  pl.BlockSpec((1, CT), lambda j: (0, CV0 + j)), pl.BlockSpec((1, CT), lambda j: (0, CG0 + j)),
                  pl.BlockSpec((HALO, CT), lambda j: (0, j)), pl.BlockSpec((1, CT), lambda j: (0, j))],
        out_specs=pl.BlockSpec((n_rows, CT), lambda j: (0, j)),
        out_shape=jax.ShapeDtypeStruct((n_rows, D), F32),
        scratch_shapes=[pltpu.VMEM((n_rows + HALO, CT), F32)],
        compiler_params=_cparams("parallel"),
    )(proj, proj, b_in, b_in, w, cb)


def _conv_bwd(name, dz1, proj, b_in, w):
    n_rows = proj.shape[0]

    def body(dz_ref, cv_ref, cg_ref, bcv_ref, bcg_ref, w_ref, dcv_ref, dcg_ref, sm_ref, zp_ref, dzp_ref):
        cvb = cv_ref[...] + bcv_ref[...]
        sg = jax.nn.sigmoid(cg_ref[...] + bcg_ref[...])
        zp_ref[0:HALO, :] = jnp.zeros((HALO, CT), F32)
        zp_ref[HALO:, :] = cvb * sg
        dz = dz_ref[...]
        dzp_ref[0:n_rows, :] = dz
        dzp_ref[n_rows:, :] = jnp.zeros((HALO, CT), F32)
        sm_ref[...] = jnp.zeros(sm_ref.shape, F32)
        sm_ref[32:33, :] = jnp.sum(dz, axis=0, keepdims=True)

        def blk(rb, carry):
            base = pl.multiple_of(rb * RB, RB)
            dwin = dzp_ref[pl.ds(base, RB + HALO), :]
            zwin = zp_ref[pl.ds(base, RB + HALO), :]
            dzb = dwin[0:RB]
            acc = jnp.zeros((RB, CT), F32)
            for k in range(KCONV):
                acc = acc + w_ref[k:k + 1, :] * _shifted(dwin, KCONV - 1 - k)
                sm_ref[k:k + 1, :] += jnp.sum(dzb * _shifted(zwin, k + 2), axis=0, keepdims=True)
            dzp_ref[pl.ds(base, RB), :] = acc
            return carry

        lax.fori_loop(0, n_rows // RB, blk, 0)
        dz0 = dzp_ref[0:n_rows, :]
        dcv = dz0 * sg
        dcg = dz0 * cvb * (sg * (1.0 - sg))
        dcv_ref[...] = dcv.astype(dcv_ref.dtype)
        dcg_ref[...] = dcg.astype(dcg_ref.dtype)
        sm_ref[33:34, :] = jnp.sum(dcv, axis=0, keepdims=True)
        sm_ref[34:35, :] = jnp.sum(dcg, axis=0, keepdims=True)

    col = pl.BlockSpec((n_rows, CT), lambda j: (0, j))
    return pl.pallas_call(
        body, name=name, grid=(D // CT,),
        in_specs=[col, pl.BlockSpec((n_rows, CT), lambda j: (0, CV0 + j)), pl.BlockSpec((n_rows, CT), lambda j: (0, CG0 + j)),
                  pl.BlockSpec((1, CT), lambda j: (0, CV0 + j)), pl.BlockSpec((1, CT), lambda j: (0, CG0 + j)),
                  pl.BlockSpec((HALO, CT), lambda j: (0, j))],
        out_specs=[col, col, pl.BlockSpec((40, CT), lambda j: (0, j))],
        out_shape=[jax.ShapeDtypeStruct((n_rows, D), BF16), jax.ShapeDtypeStruct((n_rows, D), BF16),
                   jax.ShapeDtypeStruct((40, D), F32)],
        scratch_shapes=[pltpu.VMEM((n_rows + HALO, CT), F32), pltpu.VMEM((n_rows + HALO, CT), F32)],
        compiler_params=_cparams("parallel"),
    )(dz1, proj, proj, b_in, b_in, w)


ADA_TN = 768


def _split_bf16(v):
    hi = v.astype(BF16)
    return hi, (v - hi.astype(F32)).astype(BF16)


def _ada_fwd(name, c_all, w, b):
    n = w.shape[1]

    def body(c_ref, w_ref, b_ref, o_ref):
        ch, cl = _split_bf16(_silu(c_ref[...]))
        wh, wl = _split_bf16(w_ref[...])
        acc = jnp.dot(ch, wl, preferred_element_type=F32) + jnp.dot(cl, wh, preferred_element_type=F32)
        o_ref[...] = acc + jnp.dot(ch, wh, preferred_element_type=F32) + b_ref[...]

    return pl.pallas_call(
        body, name=name, grid=(n // ADA_TN,),
        in_specs=[pl.BlockSpec((8, D), lambda j: (0, 0)), pl.BlockSpec((D, ADA_TN), lambda j: (0, j)),
                  pl.BlockSpec((1, ADA_TN), lambda j: (0, j))],
        out_specs=pl.BlockSpec((8, ADA_TN), lambda j: (0, j)),
        out_shape=jax.ShapeDtypeStruct((8, n), F32),
        compiler_params=_cparams("parallel"),
    )(c_all, w, b)


def _ada_wgrad(name, c_all_t, dmod):
    n = dmod.shape[1]

    def body(ct_ref, dm_ref, o_ref):
        ca = _silu(ct_ref[...])
        acc = ca[:, 0:1] * dm_ref[0:1, :]
        for r in range(1, 8):
            acc = acc + ca[:, r:r + 1] * dm_ref[r:r + 1, :]
        o_ref[...] = acc

    return pl.pallas_call(
        body, name=name, grid=(n // ADA_TN,),
        in_specs=[pl.BlockSpec((D, 8), lambda j: (0, 0)), pl.BlockSpec((8, ADA_TN), lambda j: (0, j))],
        out_specs=pl.BlockSpec((D, ADA_TN), lambda j: (0, j)),
        out_shape=jax.ShapeDtypeStruct((D, n), F32),
        compiler_params=_cparams("parallel"),
    )(c_all_t, dmod)


def _adamw(name, w, g, m, v, tr):
    rows, cols = w.shape

    def body(w_ref, g_ref, m_ref, v_ref, d_ref, nm_ref, nv_ref):
        gv = g_ref[...]
        nm = ADAM_B1 * m_ref[...] + (1.0 - ADAM_B1) * gv
        nv = ADAM_B2 * v_ref[...] + (1.0 - ADAM_B2) * (gv * gv)
        m_hat = nm / (1.0 - ADAM_B1 ** ADAM_STEP)
        v_hat = nv / (1.0 - ADAM_B2 ** ADAM_STEP)
        d_ref[...] = -ADAM_LR * (m_hat / (jnp.sqrt(v_hat) + ADAM_EPS) + ADAM_WD * w_ref[...])
        nm_ref[...] = nm
        nv_ref[...] = nv

    spec = pl.BlockSpec((tr, cols), lambda i: (i, 0))
    sds = jax.ShapeDtypeStruct((rows, cols), F32)
    return pl.pallas_call(
        body, name=name, grid=(rows // tr,), in_specs=[spec] * 4, out_specs=[spec] * 3, out_shape=[sds] * 3,
        compiler_params=_cparams("parallel"),
    )(w, g, m, v)


def _sum8(name, stacked, tr):
    n, rows, cols = stacked.shape

    def body(s_ref, o_ref):
        acc = s_ref[0]
        for r in range(1, n):
            acc = acc + s_ref[r]
        o_ref[...] = acc

    return pl.pallas_call(
        body, name=name, grid=(rows // tr,),
        in_specs=[pl.BlockSpec((n, tr, cols), lambda i: (0, i, 0))],
        out_specs=pl.BlockSpec((tr, cols), lambda i: (i, 0)),
        out_shape=jax.ShapeDtypeStruct((rows, cols), F32),
        compiler_params=_cparams("parallel"),
    )(stacked)


def _pair_add(name, g5, other, c_idx, tr):
    nq, _, rows, cols = g5.shape

    def body(c_ref, g_ref, o_ref, p_ref):
        p_ref[...] = (g_ref[...] + o_ref[...]).astype(p_ref.dtype)

    return pl.pallas_call(
        body, name=name,
        grid_spec=pltpu.PrefetchScalarGridSpec(
            num_scalar_prefetch=1, grid=(nq, rows // tr),
            in_specs=[pl.BlockSpec((None, None, tr, cols), lambda qi, i, cr: (qi, cr[0], i, 0)),
                      pl.BlockSpec((None, tr, cols), lambda qi, i, cr: (qi, i, 0))],
            out_specs=pl.BlockSpec((None, tr, cols), lambda qi, i, cr: (qi, i, 0))),
        out_shape=jax.ShapeDtypeStruct((nq, rows, cols), BF16),
        compiler_params=_cparams("parallel", "parallel"),
    )(c_idx, g5, other)


def _chip_add(name, p, recv, qc_idx, tr):
    _, rows, cols = p.shape

    def body(qc_ref, p_ref, r_ref, o_ref):
        acc = p_ref[...].astype(F32)
        for k in range(NCHIP - 1):
            acc = acc + r_ref[k].astype(F32)
        o_ref[...] = acc

    return pl.pallas_call(
        body, name=name,
        grid_spec=pltpu.PrefetchScalarGridSpec(
            num_scalar_prefetch=1, grid=(rows // tr,),
            in_specs=[pl.BlockSpec((None, tr, cols), lambda i, qc: (qc[0], i, 0)),
                      pl.BlockSpec((NCHIP - 1, tr, cols), lambda i, qc: (0, i, 0))],
            out_specs=pl.BlockSpec((None, tr, cols), lambda i, qc: (qc[1], i, 0))),
        out_shape=jax.ShapeDtypeStruct((2, rows, cols), F32),
        compiler_params=_cparams("parallel"),
    )(qc_idx, p, recv)


def _allgather8(name, blk):
    m_per, n = blk.shape

    def body(x_ref, out_ref, send_sems, recv_sems, local_sem):
        x, y, c = lax.axis_index("x"), lax.axis_index("y"), lax.axis_index("c")
        me, sibling = (x, y, c), (x, y, 1 - c)
        chips = [(1 - x, y), (x, 1 - y), (1 - x, 1 - y)]

        def rows(px, py, pc):
            return out_ref.at[pl.ds((4 * px + 2 * py + pc) * m_per, m_per), :]

        def copy(k, block, to, src=None):
            return pltpu.make_async_remote_copy(
                src_ref=rows(*block) if src is None else src, dst_ref=rows(*block),
                send_sem=send_sems.at[k], recv_sem=recv_sems.at[k], device_id=to, device_id_type=MESH)

        mine = pltpu.make_async_copy(x_ref, rows(*me), local_sem)
        mine.start()
        first = [copy(0, me, sibling, src=x_ref)]
        first += [copy(1 + j, me, (*chip, c), src=x_ref) for j, chip in enumerate(chips)]
        for cp in first:
            cp.start()
        passed = [copy(4 + j, (*chip, c), sibling) for j, chip in enumerate(chips)]
        for j, chip in enumerate(chips):
            copy(1 + j, (*chip, c), me).wait_recv()
            passed[j].start()
        copy(0, sibling, me).wait_recv()
        for j, chip in enumerate(chips):
            copy(4 + j, (*chip, 1 - c), me).wait_recv()
        for cp in first + passed:
            cp.wait_send()
        mine.wait()

    return pl.pallas_call(
        body, name=name,
        out_shape=jax.ShapeDtypeStruct((8 * m_per, n), blk.dtype),
        in_specs=[pl.BlockSpec(memory_space=pltpu.VMEM)],
        out_specs=pl.BlockSpec(memory_space=pltpu.VMEM),
        scratch_shapes=[pltpu.SemaphoreType.DMA((7,)), pltpu.SemaphoreType.DMA((7,)), pltpu.SemaphoreType.DMA],
        compiler_params=pltpu.CompilerParams(vmem_limit_bytes=VMEM_LIMIT),
    )(blk)


def _exchange(name, srcs, out_shapes, n_local, n_remote, plan, aliases=None):
    ni, no = len(srcs), len(out_shapes)

    def body(*refs):
        ins, outs = refs[:ni], refs[ni:ni + no]
        send_sems, recv_sems, local_sems = refs[ni + no:]
        x, y, c = lax.axis_index("x"), lax.axis_index("y"), lax.axis_index("c")
        local, remote = plan(ins, outs, x, y, c)
        assert len(local) == n_local and len(remote) == n_remote

        def rcopy(i, dst):
            s, _, peer, _, _ = remote[i]
            return pltpu.make_async_remote_copy(src_ref=s, dst_ref=dst, send_sem=send_sems.at[i],
                                                recv_sem=recv_sems.at[i], device_id=peer, device_id_type=MESH)

        lcs = [pltpu.make_async_copy(s, d, local_sems.at[i]) for i, (s, d) in enumerate(local)]
        for cp in lcs:
            cp.start()
        first = [i for i in range(n_remote) if remote[i][4] is None]
        passed = [i for i in range(n_remote) if remote[i][4] is not None]
        for i in first:
            rcopy(i, remote[i][1]).start()
        arrived = set()
        for i in passed:
            j = remote[i][4]
            rcopy(j, remote[j][3]).wait_recv()
            arrived.add(j)
            rcopy(i, remote[i][1]).start()
        for i in range(n_remote):
            if i not in arrived:
                rcopy(i, remote[i][3]).wait_recv()
        for i in range(n_remote):
            rcopy(i, remote[i][1]).wait_send()
        for cp in lcs:
            cp.wait()

    any_spec = pl.BlockSpec(memory_space=pl.ANY)
    return pl.pallas_call(
        body, name=name, out_shape=out_shapes,
        in_specs=[any_spec] * ni, out_specs=[any_spec] * no,
        input_output_aliases=aliases or {},
        scratch_shapes=[pltpu.SemaphoreType.DMA((n_remote,)), pltpu.SemaphoreType.DMA((n_remote,)),
                        pltpu.SemaphoreType.DMA((max(n_local, 1),))],
    )(*srcs)


_CHIP_FLIPS = ((0, 1), (1, 0), (1, 1))


def _flip(v, f):
    return 1 - v if f else v


def _gather_chips(name, stacks, smalls):
    ns, nsm = len(stacks), len(smalls)

    def plan(ins, outs, x, y, c):
        q = 2 * x + y
        sib = (x, y, 1 - c)
        chips = [(_flip(x, fx), _flip(y, fy)) for fx, fy in _CHIP_FLIPS]
        local = [(ins[ns + b], outs[ns + b].at[q]) for b in range(nsm)]
        remote = []
        for a in range(ns):
            for px, py in chips:
                remote.append((ins[a].at[q, c], outs[a].at[q, c], (px, py, c), outs[a].at[2 * px + py, c], None))
        for b in range(nsm):
            for px, py in chips:
                remote.append((ins[ns + b], outs[ns + b].at[q], (px, py, c), outs[ns + b].at[2 * px + py], None))
        for a in range(ns):
            for k, (px, py) in enumerate(chips):
                p = 2 * px + py
                remote.append((outs[a].at[p, c], outs[a].at[p, c], sib, outs[a].at[p, 1 - c], 3 * a + k))
        return local, remote

    shapes = [jax.ShapeDtypeStruct(s.shape, s.dtype) for s in stacks]
    shapes += [jax.ShapeDtypeStruct((NCHIP,) + s.shape, s.dtype) for s in smalls]
    return _exchange(name, list(stacks) + list(smalls), shapes, nsm, 6 * ns + 3 * nsm, plan,
                     aliases={a: a for a in range(ns)})


def _reduce_scatter(grads, c_idx, q_idx):
    n = len(grads)
    g5 = [g.reshape(NCHIP, 2, g.shape[1] // 2, g.shape[2]) for g in grads]
    half = [(g.shape[2], g.shape[3]) for g in g5]

    def plan1(ins, outs, x, y, c):
        sib = (x, y, 1 - c)
        return [], [(ins[a].at[:, 1 - c], outs[a], sib, outs[a], None) for a in range(n)]

    got1 = _exchange("rs_pair", g5, [jax.ShapeDtypeStruct((NCHIP,) + h, F32) for h in half], 0, n, plan1)
    part = [_pair_add(f"rs_pair_add{a}", g5[a], got1[a], c_idx, _tile(half[a][0], 512)) for a in range(n)]

    def plan2(ins, outs, x, y, c):
        remote = []
        for a in range(n):
            for k, (fx, fy) in enumerate(_CHIP_FLIPS):
                px, py = _flip(x, fx), _flip(y, fy)
                remote.append((ins[a].at[2 * px + py], outs[a].at[k], (px, py, c), outs[a].at[k], None))
        return [], remote

    got2 = _exchange("rs_chips", part, [jax.ShapeDtypeStruct((NCHIP - 1,) + h, BF16) for h in half], 0, 3 * n, plan2)
    qc_idx = jnp.concatenate([q_idx, c_idx])
    red = [_chip_add(f"rs_chip_add{a}", part[a], got2[a], qc_idx, _tile(half[a][0], 512)) for a in range(n)]

    def plan3(ins, outs, x, y, c):
        sib = (x, y, 1 - c)
        return [], [(ins[a].at[c], outs[a].at[c], sib, outs[a].at[1 - c], None) for a in range(n)]

    full = _exchange("rs_share", red, [jax.ShapeDtypeStruct((2,) + h, F32) for h in half], 0, n, plan3,
                     aliases={a: a for a in range(n)})
    return [f.reshape(2 * h[0], h[1]) for f, h in zip(full, half)]


def _to_bf16_stack(name, w, q_idx):
    r, cols = w.shape
    half = r // 2
    tr = _tile(half, 256)
    nb = half // tr

    def body(q_ref, w_ref, o_ref):
        o_ref[...] = w_ref[...].astype(BF16)

    return pl.pallas_call(
        body, name=name,
        grid_spec=pltpu.PrefetchScalarGridSpec(
            num_scalar_prefetch=1, grid=(2 * nb,),
            in_specs=[pl.BlockSpec((tr, cols), lambda i, qr: (i, 0))],
            out_specs=pl.BlockSpec((None, None, tr, cols), lambda i, qr: (qr[0], i // nb, i % nb, 0))),
        out_shape=jax.ShapeDtypeStruct((NCHIP, 2, half, cols), BF16),
        compiler_params=_cparams("parallel"),
    )(q_idx, w)


def kernel(x, c, ada_w, ada_b, norm_ffn1, ffn1_w_gate, ffn1_w_up, ffn1_w_down, norm_mix, mix_w_in, mix_b_in, sgu_ln_g, sgu_ln_b, sgu_w_s, sgu_b_s, conv_w, conv_b, conv_ln_g, conv_ln_b, w_branch_a, w_branch_b, w_out, norm_ffn2, ffn2_w_gate, ffn2_w_up, ffn2_w_down, norm_final, loss_target, m_ada_w, m_ada_b, m_norm_ffn1, m_ffn1_w_gate, m_ffn1_w_up, m_ffn1_w_down, m_norm_mix, m_mix_w_in, m_mix_b_in, m_sgu_ln_g, m_sgu_ln_b, m_sgu_w_s, m_sgu_b_s, m_conv_w, m_conv_b, m_conv_ln_g, m_conv_ln_b, m_w_branch_a, m_w_branch_b, m_w_out, m_norm_ffn2, m_ffn2_w_gate, m_ffn2_w_up, m_ffn2_w_down, m_norm_final, v_ada_w, v_ada_b, v_norm_ffn1, v_ffn1_w_gate, v_ffn1_w_up, v_ffn1_w_down, v_norm_mix, v_mix_w_in, v_mix_b_in, v_sgu_ln_g, v_sgu_ln_b, v_sgu_w_s, v_sgu_b_s, v_conv_w, v_conv_b, v_conv_ln_g, v_conv_ln_b, v_w_branch_a, v_w_branch_b, v_w_out, v_norm_ffn2, v_ffn2_w_gate, v_ffn2_w_up, v_ffn2_w_down, v_norm_final):
    xi, yi, ci = lax.axis_index("x"), lax.axis_index("y"), lax.axis_index("c")
    q = 2 * xi + yi
    dev = 4 * xi + 2 * yi + ci
    c_idx = jnp.reshape(ci, (1,)).astype(jnp.int32)
    q_idx = jnp.reshape(q, (1,)).astype(jnp.int32)
    x0 = x[0]
    tgt = loss_target[0]
    seq = x0.shape[0]
    n_ada = ada_w.shape[2]

    c_all = _allgather8("ag_c", jnp.pad(c, ((0, 7), (0, 0))))[::8]
    ada_b_mine = lax.dynamic_slice(ada_b, (0, q * n_ada), (1, n_ada))
    mod_part = _ada_fwd("ada_fwd", c_all, ada_w[0], ada_b_mine)

    big = dict(g1=ffn1_w_gate, u1=ffn1_w_up, d1=ffn1_w_down, win=mix_w_in, wa=w_branch_a, wb=w_branch_b, wo=w_out,
               g2=ffn2_w_gate, u2=ffn2_w_up, d2=ffn2_w_down)
    big_names = list(big)
    stacks = [_to_bf16_stack(f"cast_{k}", big[k][0], q_idx) for k in big_names]
    gathered = _gather_chips("gather_w", stacks, [mod_part, jnp.pad(conv_w[0], ((0, 1), (0, 0)))])
    W = {k: g.reshape(NCHIP, 2 * g.shape[2], g.shape[3]) for k, g in zip(big_names, gathered)}
    mod_all = jnp.transpose(gathered[-2], (1, 0, 2)).reshape(8, N_MOD * D)
    mod = lax.dynamic_slice(mod_all, (dev, 0), (1, N_MOD * D))
    sh1, sc1, g1, sh2, sc2, g2, sh3, sc3, g3 = [mod[:, k * D:(k + 1) * D] for k in range(N_MOD)]
    convw = jnp.transpose(gathered[-1], (1, 0, 2)).reshape(HALO, D)

    causal = jnp.tril(jnp.ones((CHUNK, CHUNK), bool))
    wm_f = jnp.where(causal[None], sgu_w_s[0], 0.0)
    wm = wm_f.astype(BF16)
    wmt = jnp.swapaxes(wm_f, 1, 2).astype(BF16)
    bst = jnp.repeat(jnp.transpose(sgu_b_s[0]), CHUNK, axis=1)
    b_in = mix_b_in

    (h1,) = _rowwise_fwd("pre1", _f_pre, [x0], [norm_ffn1, sc1, sh1], [(D, BF16)], 256)
    a1, b1, s1 = _ffn_up("up1", h1, W["g1"], W["u1"])
    f1 = _matmul("down1", s1, W["d1"], "fwd", False, F32)
    x1, h2 = _rowwise_fwd("res1", _f_res_pre(0.5), [x0, f1], [g1, norm_mix, sc2, sh2], [(D, F32), (D, BF16)], 256)
    proj = _matmul("in_proj", h2, W["win"], "fwd", True, F32)
    ya = _sgu_fwd("sgu", proj, b_in, sgu_ln_g, sgu_ln_b, wm, bst, 256)
    z1 = _conv_fwd("conv", proj, b_in, convw, conv_b)
    (z3,) = _rowwise_fwd("conv_ln", _f_lnsilu, [z1], [conv_ln_g, conv_ln_b], [(D, BF16)], 256)
    y_a = _matmul("branch_a", ya, W["wa"], "fwd", False, F32)
    y_b = _matmul("branch_b", z3, W["wb"], "fwd", False, F32)
    (merged,) = _rowwise_fwd("merge", _f_merge, [(proj, 2 * D, 2), y_a, y_b], [(b_in, 2 * D, 2)], [(D, BF16)], 256)
    y = _matmul("out_proj", merged, W["wo"], "fwd", False, F32)
    x2, h3 = _rowwise_fwd("res2", _f_res_pre(1.0), [x1, y], [g2, norm_ffn2, sc3, sh3], [(D, F32), (D, BF16)], 256)
    a3, b3, s3 = _ffn_up("up2", h3, W["g2"], W["u2"])
    f3 = _matmul("down2", s3, W["d2"], "fwd", False, F32)
    dx2, df3, dg3, dnwf, loss_blk = _final("final", x2, f3, tgt, g3, norm_final.reshape(1, D), 256)

    G = {}

    def ffn_bwd(tag, df, s, a, b, h, wg, wu, wd):
        da, db = _dgrad_glu(f"d_down{tag}", df, wd, a, b)
        G["d" + tag] = _matmul(f"w_down{tag}", s, df, "wgrad", False, F32)
        dh = _matmul(f"d_gate{tag}", da, wg, "dgrad", True, F32)
        dh = _matmul(f"d_up{tag}", db, wu, "dgrad", True, F32, add=dh)
        G["g" + tag] = _matmul(f"w_gate{tag}", h, da, "wgrad", True, F32)
        G["u" + tag] = _matmul(f"w_up{tag}", h, db, "wgrad", True, F32)
        return dh

    dh3 = ffn_bwd("2", df3, s3, a3, b3, h3, W["g2"], W["u2"], W["d2"])
    (dx1, dy), (dg2, dnw3, dsc3, dsh3) = _rowwise_bwd(
        "res2_b", _f_res_pre(1.0), [x1, y], [g2, norm_ffn2, sc3, sh3], [dx2, dh3], [F32, BF16], 256)
    dmerged = _matmul("d_out", dy, W["wo"], "dgrad", False, F32)
    G["wo"] = _matmul("w_out", merged, dy, "wgrad", False, F32)
    (dgab, dy_a, dy_b), (dbgab,) = _rowwise_bwd(
        "merge_b", _f_merge, [(proj, 2 * D, 2), y_a, y_b], [(b_in, 2 * D, 2)], [dmerged], [BF16, BF16, BF16], 256)
    dya = _matmul("d_a", dy_a, W["wa"], "dgrad", False, F32)
    dz3 = _matmul("d_b", dy_b, W["wb"], "dgrad", False, F32)
    G["wa"] = _matmul("w_a", ya, dy_a, "wgrad", False, F32)
    G["wb"] = _matmul("w_b", z3, dy_b, "wgrad", False, F32)
    (dz1,), (dclg, dclb) = _rowwise_bwd("conv_ln_b", _f_lnsilu, [z1], [conv_ln_g, conv_ln_b], [dz3], [F32], 256)
    dcv, dcg, conv_small = _conv_bwd("conv_b", dz1, proj, b_in, convw)
    duv, dbuv, dlng, dlnb, dws, dbs = _sgu_bwd("sgu_b", proj, b_in, sgu_ln_g, sgu_ln_b, wm, wmt, bst, dya, 256)
    dproj = jnp.concatenate([duv, dcv, dcg, dgab], axis=1)
    dh2 = _matmul("d_in", dproj, W["win"], "dgrad", True, F32)
    G["win"] = _matmul("w_in", h2, dproj, "wgrad", True, F32)
    (dx0, df1), (dg1, dnw2, dsc2, dsh2) = _rowwise_bwd(
        "res1_b", _f_res_pre(0.5), [x0, f1], [g1, norm_mix, sc2, sh2], [dx1, dh2], [F32, BF16], 256)
    dh1 = ffn_bwd("1", df1, s1, a1, b1, h1, W["g1"], W["u1"], W["d1"])
    (grad_x,), (dnw1, dsc1, dsh1) = _rowwise_bwd(
        "pre1_b", _f_pre_keep, [x0], [norm_ffn1, sc1, sh1], [dx0, dh1], [F32], 256)

    db_in = jnp.concatenate([dbuv, conv_small[33:34], conv_small[34:35], dbgab], axis=1)
    dbs_row = jnp.transpose(dbs[:, ::CHUNK]).reshape(1, D)
    loss_row = jnp.pad(loss_blk[0:1, :], ((0, 0), (0, D - 128)))
    small = jnp.concatenate([
        dnw1, dnw2, db_in.reshape(6, D), dlng, dlnb, dws.reshape(CHUNK, D), dbs_row, conv_small[0:KCONV],
        conv_small[32:33], dclg, dclb, dnw3, dnwf,
        dsh1, dsc1, dg1, dsh2, dsc2, dg2, dsh3, dsc3, dg3, loss_row, jnp.zeros((7, D), F32)], axis=0)
    n_small = small.shape[0]
    every = _allgather8("ag_small", small).reshape(8, n_small, D)
    tot = _sum8("sum_small", every, 64)
    r_mod = 1 + 1 + 6 + 1 + 1 + CHUNK + 1 + KCONV + 1 + 1 + 1 + 1 + 1
    loss = tot[r_mod + N_MOD, 0]
    dmod_all = every[:, r_mod:r_mod + N_MOD, :].reshape(8, N_MOD * D)
    dmod_mine = lax.dynamic_slice(dmod_all, (0, q * n_ada), (8, n_ada))
    g_ada_w = _ada_wgrad("ada_wgrad", jnp.transpose(c_all), dmod_mine)

    reduced = dict(zip(big_names, _reduce_scatter([G[k] for k in big_names], c_idx, q_idx)))

    given = dict(ada_w=(ada_w, m_ada_w, v_ada_w), ada_b=(ada_b, m_ada_b, v_ada_b),
                 norm_ffn1=(norm_ffn1, m_norm_ffn1, v_norm_ffn1),
                 ffn1_w_gate=(ffn1_w_gate, m_ffn1_w_gate, v_ffn1_w_gate), ffn1_w_up=(ffn1_w_up, m_ffn1_w_up, v_ffn1_w_up),
                 ffn1_w_down=(ffn1_w_down, m_ffn1_w_down, v_ffn1_w_down), norm_mix=(norm_mix, m_norm_mix, v_norm_mix),
                 mix_w_in=(mix_w_in, m_mix_w_in, v_mix_w_in), mix_b_in=(mix_b_in, m_mix_b_in, v_mix_b_in),
                 sgu_ln_g=(sgu_ln_g, m_sgu_ln_g, v_sgu_ln_g), sgu_ln_b=(sgu_ln_b, m_sgu_ln_b, v_sgu_ln_b),
                 sgu_w_s=(sgu_w_s, m_sgu_w_s, v_sgu_w_s), sgu_b_s=(sgu_b_s, m_sgu_b_s, v_sgu_b_s),
                 conv_w=(conv_w, m_conv_w, v_conv_w), conv_b=(conv_b, m_conv_b, v_conv_b),
                 conv_ln_g=(conv_ln_g, m_conv_ln_g, v_conv_ln_g), conv_ln_b=(conv_ln_b, m_conv_ln_b, v_conv_ln_b),
                 w_branch_a=(w_branch_a, m_w_branch_a, v_w_branch_a), w_branch_b=(w_branch_b, m_w_branch_b, v_w_branch_b),
                 w_out=(w_out, m_w_out, v_w_out), norm_ffn2=(norm_ffn2, m_norm_ffn2, v_norm_ffn2),
                 ffn2_w_gate=(ffn2_w_gate, m_ffn2_w_gate, v_ffn2_w_gate), ffn2_w_up=(ffn2_w_up, m_ffn2_w_up, v_ffn2_w_up),
                 ffn2_w_down=(ffn2_w_down, m_ffn2_w_down, v_ffn2_w_down), norm_final=(norm_final, m_norm_final, v_norm_final))
    out = {}

    def big_update(name, g2d):
        w, m, v = given[name]
        shp = w.shape
        d_, nm, nv = _adamw(f"adamw_{name}", w.reshape(g2d.shape), g2d, m.reshape(g2d.shape), v.reshape(g2d.shape),
                            _tile(g2d.shape[0], 512 if g2d.shape[1] <= D else 256))
        out[name] = tuple(t.reshape(shp) for t in (g2d, d_, nm, nv))

    for key, name in (("g1", "ffn1_w_gate"), ("u1", "ffn1_w_up"), ("d1", "ffn1_w_down"), ("win", "mix_w_in"),
                      ("wa", "w_branch_a"), ("wb", "w_branch_b"), ("wo", "w_out"),
                      ("g2", "ffn2_w_gate"), ("u2", "ffn2_w_up"), ("d2", "ffn2_w_down")):
        big_update(name, reduced[key])
    big_update("ada_w", g_ada_w)

    small_names = [("norm_ffn1", 1), ("norm_mix", 1), ("mix_b_in", 6), ("sgu_ln_g", 1), ("sgu_ln_b", 1),
                   ("sgu_w_s", CHUNK), ("sgu_b_s", 1), (None, KCONV), ("conv_b", 1), ("conv_ln_g", 1), ("conv_ln_b", 1),
                   ("norm_ffn2", 1), ("norm_final", 1), ("ada_b", N_MOD), (None, 8)]

    def pack(which):
        return jnp.concatenate([jnp.zeros((r, D), F32) if nm is None else given[nm][which].reshape(r, D)
                                for nm, r in small_names], axis=0)

    sd, sm, sv = _adamw("adamw_small", pack(0), tot, pack(1), pack(2), 64)
    row = 0
    for nm, r in small_names:
        if nm is not None:
            shp = given[nm][0].shape
            out[nm] = tuple(t[row:row + r].reshape(shp) for t in (tot, sd, sm, sv))
        row += r
    r_cw = 1 + 1 + 6 + 1 + 1 + CHUNK + 1
    n_cw = conv_w.shape[2]
    g_cw = jnp.pad(lax.dynamic_slice(tot[r_cw:r_cw + KCONV], (0, q * n_cw), (KCONV, n_cw)), ((0, 1), (0, 0)))
    pad_cw = lambda t: jnp.pad(t[0], ((0, 1), (0, 0)))
    cd, cm, cv_ = _adamw("adamw_conv_w", pad_cw(conv_w), g_cw, pad_cw(m_conv_w), pad_cw(v_conv_w), HALO)
    out["conv_w"] = tuple(t[:KCONV][None] for t in (g_cw, cd, cm, cv_))

    order = ["ada_w", "ada_b", "norm_ffn1", "ffn1_w_gate", "ffn1_w_up", "ffn1_w_down", "norm_mix", "mix_w_in", "mix_b_in",
             "sgu_ln_g", "sgu_ln_b", "sgu_w_s", "sgu_b_s", "conv_w", "conv_b", "conv_ln_g", "conv_ln_b", "w_branch_a",
             "w_branch_b", "w_out", "norm_ffn2", "ffn2_w_gate", "ffn2_w_up", "ffn2_w_down", "norm_final"]
    return (loss, grad_x[None], *[out[n][0] for n in order], *[out[n][1] for n in order],
            *[out[n][2] for n in order], *[out[n][3] for n in order])
```

```python
import functools

import jax
import jax.numpy as jnp
from jax import lax
from jax.experimental import pallas as pl
from jax.experimental.pallas import tpu as pltpu

F32 = jnp.float32
BF16 = jnp.bfloat16
D = 1024
HEADS = 8
CHUNK = 128
KCONV = 31
HALO = 32
N_MOD = 9
EPS = 1e-6
NCHIP = 4
VMEM_LIMIT = 56 * 1024 * 1024
MESH = pl.DeviceIdType.MESH

ADAM_LR = 0.001
ADAM_B1 = 0.9
ADAM_B2 = 0.999
ADAM_EPS = 1e-08
ADAM_WD = 0.01
ADAM_STEP = 10

NN = (((1,), (0,)), ((), ()))
NT = (((1,), (1,)), ((), ()))
TN = (((0,), (0,)), ((), ()))


def _cparams(*sem):
    return pltpu.CompilerParams(dimension_semantics=sem or None, vmem_limit_bytes=VMEM_LIMIT)


def _rms(x, w):
    return x * lax.rsqrt(jnp.mean(x * x, axis=-1, keepdims=True) + EPS) * w


def _ln(x, g, b):
    mu = jnp.mean(x, axis=-1, keepdims=True)
    xc = x - mu
    var = jnp.mean(xc * xc, axis=-1, keepdims=True)
    return xc * lax.rsqrt(var + EPS) * g + b


def _silu(x):
    return x * jax.nn.sigmoid(x)


def _gelu(x):
    return x * (0.5 * (1.0 + jnp.tanh(0.7978845608028654 * (x + 0.044715 * (x * x * x)))))


def _f_pre(x, nw, sc, sh):
    return _rms(x, nw) * (1.0 + sc) + sh


def _f_pre_keep(x, nw, sc, sh):
    return x, _f_pre(x, nw, sc, sh)


def _f_res_pre(scale):
    def fn(x, f, g, nw, sc, sh):
        xn = x + (scale * g) * f
        return xn, _f_pre(xn, nw, sc, sh)
    return fn


def _f_glu(a, b):
    return _silu(a) * b


def _f_merge(gab, ya, yb, bgab):
    g = gab + bgab
    return jax.nn.sigmoid(g[:, :D]) * ya + jax.nn.sigmoid(g[:, D:]) * yb


def _f_lnsilu(z, g, b):
    return _silu(_ln(z, g, b))


def _f_mixa(uv, buv, g, b):
    t = uv + buv
    return _gelu(t[:, :D]), _ln(_gelu(t[:, D:]), g, b)


def _col(t):
    return t if isinstance(t, tuple) else (t, t.shape[-1], 0)


def _rowwise_fwd(name, fn, rows, params, outs, tr):
    rows = [_col(r) for r in rows]
    params = [_col(p) for p in params]
    n_rows = rows[0][0].shape[0]
    nin = len(rows) + len(params)

    def body(*refs):
        res = fn(*[r[...].astype(F32) for r in refs[:nin]])
        res = res if isinstance(res, tuple) else (res,)
        for o, v in zip(refs[nin:], res, strict=True):
            o[...] = v.astype(o.dtype)

    return pl.pallas_call(
        body, name=name, grid=(n_rows // tr,),
        in_specs=[pl.BlockSpec((tr, w), functools.partial(lambda cb, i: (i, cb), cb)) for _, w, cb in rows]
        + [pl.BlockSpec((1, w), functools.partial(lambda cb, i: (0, cb), cb)) for _, w, cb in params],
        out_specs=[pl.BlockSpec((tr, w), lambda i: (i, 0)) for w, _ in outs],
        out_shape=[jax.ShapeDtypeStruct((n_rows, w), dt) for w, dt in outs],
        compiler_params=_cparams("parallel"),
    )(*[r[0] for r in rows], *[p[0] for p in params])


def _rowwise_bwd(name, fn, rows, params, cots, row_grads, tr):
    rows = [_col(r) for r in rows]
    params = [_col(p) for p in params]
    cots = [_col(t) for t in cots]
    n_rows = rows[0][0].shape[0]
    nr, npar, nc = len(rows), len(params), len(cots)
    nin = nr + npar + nc
    n_rg = sum(dt is not None for dt in row_grads)

    def body(*refs):
        i = pl.program_id(0)
        prim = [r[...].astype(F32) for r in refs[:nr + npar]]
        ct = tuple(r[...].astype(F32) for r in refs[nr + npar:nin])
        _, vjp = jax.vjp(fn, *prim)
        g = vjp(ct if nc > 1 else ct[0])
        outs = refs[nin:]
        oi = 0
        for j, dt in enumerate(row_grads):
            if dt is not None:
                outs[oi][...] = g[j].astype(dt)
                oi += 1
        for j in range(npar):
            acc = outs[n_rg + j]

            @pl.when(i == 0)
            def _(acc=acc):
                acc[...] = jnp.zeros(acc.shape, F32)

            acc[...] += g[nr + j]

    res = pl.pallas_call(
        body, name=name, grid=(n_rows // tr,),
        in_specs=[pl.BlockSpec((tr, w), functools.partial(lambda cb, i: (i, cb), cb)) for _, w, cb in rows]
        + [pl.BlockSpec((1, w), functools.partial(lambda cb, i: (0, cb), cb)) for _, w, cb in params]
        + [pl.BlockSpec((tr, w), functools.partial(lambda cb, i: (i, cb), cb)) for _, w, cb in cots],
        out_specs=[pl.BlockSpec((tr, rows[j][1]), lambda i: (i, 0)) for j, dt in enumerate(row_grads) if dt is not None]
        + [pl.BlockSpec((1, w), lambda i: (0, 0)) for _, w, _ in params],
        out_shape=[jax.ShapeDtypeStruct((n_rows, rows[j][1]), dt) for j, dt in enumerate(row_grads) if dt is not None]
        + [jax.ShapeDtypeStruct((1, w), F32) for _, w, _ in params],
        compiler_params=_cparams("arbitrary"),
    )(*[r[0] for r in rows], *[p[0] for p in params], *[t[0] for t in cots])
    return res[:n_rg], res[n_rg:]


def _final(name, x, f, tgt, g, nw, tr):
    n_rows = x.shape[0]

    def body(x_ref, f_ref, t_ref, g_ref, nw_ref, dx_ref, df_ref, dg_ref, dnw_ref, loss_ref):
        i = pl.program_id(0)
        tg = t_ref[...]

        def fn(xv, fv, gv, nwv):
            e = _rms(xv + (0.5 * gv) * fv, nwv) - tg
            return 0.5 * jnp.mean(e * e, axis=-1, keepdims=True)

        per_row, vjp = jax.vjp(fn, x_ref[...], f_ref[...], g_ref[...], nw_ref[...])
        dx, df, dg, dnw = vjp(jnp.ones_like(per_row))
        dx_ref[...] = dx
        df_ref[...] = df.astype(df_ref.dtype)

        @pl.when(i == 0)
        def _():
            dg_ref[...] = jnp.zeros(dg_ref.shape, F32)
            dnw_ref[...] = jnp.zeros(dnw_ref.shape, F32)
            loss_ref[...] = jnp.zeros(loss_ref.shape, F32)

        dg_ref[...] += dg
        dnw_ref[...] += dnw
        loss_ref[...] += jnp.broadcast_to(jnp.sum(per_row, axis=0, keepdims=True), loss_ref.shape)

    row = pl.BlockSpec((tr, D), lambda i: (i, 0))
    par = pl.BlockSpec((1, D), lambda i: (0, 0))
    return pl.pallas_call(
        body, name=name, grid=(n_rows // tr,),
        in_specs=[row, row, row, par, par],
        out_specs=[row, row, par, par, pl.BlockSpec((8, 128), lambda i: (0, 0))],
        out_shape=[jax.ShapeDtypeStruct((n_rows, D), F32), jax.ShapeDtypeStruct((n_rows, D), BF16),
                   jax.ShapeDtypeStruct((1, D), F32), jax.ShapeDtypeStruct((1, D), F32),
                   jax.ShapeDtypeStruct((8, 128), F32)],
        compiler_params=_cparams("arbitrary"),
    )(x, f, tgt, g, nw)


def _tile(dim, pref):
    t = min(dim, pref)
    while dim % t:
        t //= 2
    return t


def _matmul(name, a, b, kind, col, out_dtype, add=None):
    if kind == "wgrad":
        m, kg = a.shape
        ng = b.shape[1]
        r, c = (kg, ng // NCHIP) if col else (kg // NCHIP, ng)
        tm, tn, tk = _tile(r, 1024), _tile(c, 1536), _tile(m, 2048)
        grid = (kg // tm, ng // tn, m // tk)
        a_spec = pl.BlockSpec((tk, tm), lambda i, j, k: (k, i))
        b_spec = pl.BlockSpec((tk, tn), lambda i, j, k: (k, j))
        if col:
            o_spec = pl.BlockSpec((None, tm, tn), lambda i, j, k: (j // (c // tn), i, j % (c // tn)))
        else:
            o_spec = pl.BlockSpec((None, tm, tn), lambda i, j, k: (i // (r // tm), i % (r // tm), j))
        out_shape = jax.ShapeDtypeStruct((NCHIP, r, c), F32)
        dims = TN
    else:
        _, r, c = b.shape
        m = a.shape[0]
        kg, ng = (r, NCHIP * c) if col else (NCHIP * r, c)
        tm = _tile(m, 1024)
        if kind == "fwd":
            tn, tk = _tile(c, 1536), _tile(r, 2048)
            grid = (m // tm, ng // tn, kg // tk)
            a_spec = pl.BlockSpec((tm, tk), lambda i, j, k: (i, k))
            if col:
                b_spec = pl.BlockSpec((None, tk, tn), lambda i, j, k: (j // (c // tn), k, j % (c // tn)))
            else:
                b_spec = pl.BlockSpec((None, tk, tn), lambda i, j, k: (k // (r // tk), k % (r // tk), j))
            out_shape = jax.ShapeDtypeStruct((m, ng), out_dtype)
            dims = NN
        else:
            tn, tk = _tile(r, 1024), _tile(c, 2048)
            grid = (m // tm, kg // tn, ng // tk)
            a_spec = pl.BlockSpec((tm, tk), lambda i, j, k: (i, k))
            if col:
                b_spec = pl.BlockSpec((None, tn, tk), lambda i, j, k: (k // (c // tk), j, k % (c // tk)))
            else:
                b_spec = pl.BlockSpec((None, tn, tk), lambda i, j, k: (j // (r // tn), j % (r // tn), k))
            out_shape = jax.ShapeDtypeStruct((m, kg), out_dtype)
            dims = NT
        o_spec = pl.BlockSpec((tm, tn), lambda i, j, k: (i, j))
    nk = grid[2]
    has_add = add is not None

    def body(*refs):
        a_ref, b_ref = refs[0], refs[1]
        o_ref = refs[3] if has_add else refs[2]
        k = pl.program_id(2)
        p = lax.dot_general(a_ref[...].astype(BF16), b_ref[...].astype(BF16), dims, preferred_element_type=F32)

        def finish(r_):
            if has_add:
                r_ = r_ + refs[2][...]
            o_ref[...] = r_.astype(o_ref.dtype)

        if nk == 1:
            finish(p)
        else:
            acc_ref = refs[-1]

            @pl.when(k == 0)
            def _():
                acc_ref[...] = p

            @pl.when(jnp.logical_and(k > 0, k < nk - 1))
            def _():
                acc_ref[...] += p

            @pl.when(k == nk - 1)
            def _():
                finish(acc_ref[...] + p)

    in_specs = [a_spec, b_spec] + ([pl.BlockSpec((tm, tn), lambda i, j, k: (i, j))] if has_add else [])
    args = (a, b) + ((add,) if has_add else ())
    return pl.pallas_call(
        body, name=name, grid=grid, in_specs=in_specs, out_specs=o_spec, out_shape=out_shape,
        scratch_shapes=[pltpu.VMEM((tm, tn), F32)] if nk > 1 else [],
        compiler_params=_cparams("parallel", "parallel", "arbitrary"),
    )(*args)


def _ffn_up(name, h, wg, wu):
    m, k = h.shape
    _, _, c = wg.shape
    n = NCHIP * c
    tm, tn = _tile(m, 1024), _tile(c, 1024)

    def body(h_ref, wg_ref, wu_ref, a_ref, b_ref, s_ref):
        hv = h_ref[...]
        a = jnp.dot(hv, wg_ref[...], preferred_element_type=F32)
        b = jnp.dot(hv, wu_ref[...], preferred_element_type=F32)
        a_ref[...] = a
        b_ref[...] = b
        s_ref[...] = (_silu(a) * b).astype(s_ref.dtype)

    w_spec = pl.BlockSpec((None, k, tn), lambda i, j: (j // (c // tn), 0, j % (c // tn)))
    o_spec = pl.BlockSpec((tm, tn), lambda i, j: (i, j))
    return pl.pallas_call(
        body, name=name, grid=(m // tm, n // tn),
        in_specs=[pl.BlockSpec((tm, k), lambda i, j: (i, 0)), w_spec, w_spec],
        out_specs=[o_spec, o_spec, o_spec],
        out_shape=[jax.ShapeDtypeStruct((m, n), F32), jax.ShapeDtypeStruct((m, n), F32),
                   jax.ShapeDtypeStruct((m, n), BF16)],
        compiler_params=_cparams("parallel", "parallel"),
    )(h, wg, wu)


def _dgrad_glu(name, df, wd, a, b):
    m = df.shape[0]
    _, r, c = wd.shape
    tm, tn = _tile(m, 512), _tile(r, 1024)

    def body(df_ref, w_ref, a_ref, b_ref, da_ref, db_ref):
        ds = lax.dot_general(df_ref[...], w_ref[...], NT, preferred_element_type=F32)
        _, vjp = jax.vjp(_f_glu, a_ref[...], b_ref[...])
        da, db = vjp(ds)
        da_ref[...] = da.astype(da_ref.dtype)
        db_ref[...] = db.astype(db_ref.dtype)

    t_spec = pl.BlockSpec((tm, tn), lambda i, j: (i, j))
    sds = jax.ShapeDtypeStruct((m, NCHIP * r), BF16)
    return pl.pallas_call(
        body, name=name, grid=(m // tm, NCHIP * r // tn),
        in_specs=[pl.BlockSpec((tm, c), lambda i, j: (i, 0)),
                  pl.BlockSpec((None, tn, c), lambda i, j: (j // (r // tn), j % (r // tn), 0)), t_spec, t_spec],
        out_specs=[t_spec, t_spec], out_shape=[sds, sds],
        compiler_params=_cparams("parallel", "parallel"),
    )(df, wd, a, b)


def _sgu_fwd(name, proj, b_in, lng, lnb, wm, bst, tr):
    n_rows = proj.shape[0]

    def body(uv_ref, buv_ref, g_ref, b_ref, wm_ref, bst_ref, ya_ref):
        u, vln = _f_mixa(uv_ref[...], buv_ref[...], g_ref[...], b_ref[...])
        vb = vln.astype(BF16)
        for ck in range(tr // CHUNK):
            rs = slice(ck * CHUNK, (ck + 1) * CHUNK)
            for h in range(HEADS):
                cs = slice(h * CHUNK, (h + 1) * CHUNK)
                vm = jnp.dot(wm_ref[h], vb[rs, cs], preferred_element_type=F32) + bst_ref[:, cs]
                ya_ref[rs, cs] = (u[rs, cs] * vm).astype(ya_ref.dtype)

    return pl.pallas_call(
        body, name=name, grid=(n_rows // tr,),
        in_specs=[pl.BlockSpec((tr, 2 * D), lambda i: (i, 0)), pl.BlockSpec((1, 2 * D), lambda i: (0, 0)),
                  pl.BlockSpec((1, D), lambda i: (0, 0)), pl.BlockSpec((1, D), lambda i: (0, 0)),
                  pl.BlockSpec((HEADS, CHUNK, CHUNK), lambda i: (0, 0, 0)),
                  pl.BlockSpec((CHUNK, D), lambda i: (0, 0))],
        out_specs=pl.BlockSpec((tr, D), lambda i: (i, 0)),
        out_shape=jax.ShapeDtypeStruct((n_rows, D), BF16),
        compiler_params=_cparams("parallel"),
    )(proj, b_in, lng, lnb, wm, bst)


def _sgu_bwd(name, proj, b_in, lng, lnb, wm, wmt, bst, dya, tr):
    n_rows = proj.shape[0]

    def body(uv_ref, buv_ref, g_ref, b_ref, wm_ref, wmt_ref, bst_ref, dya_ref,
             duv_ref, dbuv_ref, dg_ref, db_ref, dws_ref, dbs_ref, du_s, dvln_s):
        i = pl.program_id(0)

        @pl.when(i == 0)
        def _():
            dbuv_ref[...] = jnp.zeros(dbuv_ref.shape, F32)
            dg_ref[...] = jnp.zeros(dg_ref.shape, F32)
            db_ref[...] = jnp.zeros(db_ref.shape, F32)
            dws_ref[...] = jnp.zeros(dws_ref.shape, F32)
            dbs_ref[...] = jnp.zeros(dbs_ref.shape, F32)

        (u, vln), vjp = jax.vjp(_f_mixa, uv_ref[...], buv_ref[...], g_ref[...], b_ref[...])
        vb = vln.astype(BF16)
        dya_v = dya_ref[...].astype(F32)
        tpos = lax.broadcasted_iota(jnp.int32, (CHUNK, CHUNK), 0)
        spos = lax.broadcasted_iota(jnp.int32, (CHUNK, CHUNK), 1)
        causal = (tpos >= spos).astype(F32)
        for ck in range(tr // CHUNK):
            rs = slice(ck * CHUNK, (ck + 1) * CHUNK)
            for h in range(HEADS):
                cs = slice(h * CHUNK, (h + 1) * CHUNK)
                vbh = vb[rs, cs]
                vm = jnp.dot(wm_ref[h], vbh, preferred_element_type=F32) + bst_ref[:, cs]
                dyb = dya_v[rs, cs]
                du_s[rs, cs] = dyb * vm
                dvm = dyb * u[rs, cs]
                dvmb = dvm.astype(BF16)
                dvln_s[rs, cs] = jnp.dot(wmt_ref[h], dvmb, preferred_element_type=F32)
                dws_ref[h] += causal * lax.dot_general(dvmb, vbh, NT, preferred_element_type=F32)
                dbs_ref[:, cs] += jnp.broadcast_to(jnp.sum(dvm, axis=1, keepdims=True), (CHUNK, CHUNK))
        duv, dbuv, dg, db = vjp((du_s[...], dvln_s[...]))
        duv_ref[...] = duv.astype(duv_ref.dtype)
        dbuv_ref[...] += dbuv
        dg_ref[...] += dg
        db_ref[...] += db

    par = pl.BlockSpec((1, D), lambda i: (0, 0))
    par2 = pl.BlockSpec((1, 2 * D), lambda i: (0, 0))
    w_spec = pl.BlockSpec((HEADS, CHUNK, CHUNK), lambda i: (0, 0, 0))
    b_spec = pl.BlockSpec((CHUNK, D), lambda i: (0, 0))
    return pl.pallas_call(
        body, name=name, grid=(n_rows // tr,),
        in_specs=[pl.BlockSpec((tr, 2 * D), lambda i: (i, 0)), par2, par, par, w_spec, w_spec, b_spec,
                  pl.BlockSpec((tr, D), lambda i: (i, 0))],
        out_specs=[pl.BlockSpec((tr, 2 * D), lambda i: (i, 0)), par2, par, par, w_spec, b_spec],
        out_shape=[jax.ShapeDtypeStruct((n_rows, 2 * D), BF16), jax.ShapeDtypeStruct((1, 2 * D), F32),
                   jax.ShapeDtypeStruct((1, D), F32), jax.ShapeDtypeStruct((1, D), F32),
                   jax.ShapeDtypeStruct((HEADS, CHUNK, CHUNK), F32), jax.ShapeDtypeStruct((CHUNK, D), F32)],
        scratch_shapes=[pltpu.VMEM((tr, D), F32), pltpu.VMEM((tr, D), F32)],
        compiler_params=_cparams("arbitrary"),
    )(proj, b_in, lng, lnb, wm, wmt, bst, dya)


CT = 256
RB = 128
CV0 = 2 * D // CT
CG0 = 3 * D // CT


def _shifted(win, sh):
    if sh == 0:
        return win[0:RB]
    return pltpu.roll(win, RB + HALO - sh, 0)[0:RB]


def _conv_fwd(name, proj, b_in, w, cb):
    n_rows = proj.shape[0]

    def body(cv_ref, cg_ref, bcv_ref, bcg_ref, w_ref, cb_ref, z1_ref, zp_ref):
        zp_ref[0:HALO, :] = jnp.zeros((HALO, CT), F32)
        zp_ref[HALO:, :] = (cv_ref[...] + bcv_ref[...]) * jax.nn.sigmoid(cg_ref[...] + bcg_ref[...])

        def blk(rb, carry):
            base = pl.multiple_of(rb * RB, RB)
            win = zp_ref[pl.ds(base, RB + HALO), :]
            acc = jnp.broadcast_to(cb_ref[...], (RB, CT))
            for k in range(KCONV):
                acc = acc + w_ref[k:k + 1, :] * _shifted(win, k + 2)
            z1_ref[pl.ds(base, RB), :] = acc
            return carry

        lax.fori_loop(0, n_rows // RB, blk, 0)

    return pl.pallas_call(
        body, name=name, grid=(D // CT,),
        in_specs=[pl.BlockSpec((n_rows, CT), lambda j: (0, CV0 + j)), pl.BlockSpec((n_rows, CT), lambda j: (0, CG0 + j)),
                  pl.BlockSpec((1, CT), lambda j: (0, CV0 + j)), pl.BlockSpec((1, CT), lambda j: (0, CG0 + j)),
                  pl.BlockSpec((HALO, CT), lambda j: (0, j)), pl.BlockSpec((1, CT), lambda j: (0, j))],
        out_specs=pl.BlockSpec((n_rows, CT), lambda j: (0, j)),
        out_shape=jax.ShapeDtypeStruct((n_rows, D), F32),
        scratch_shapes=[pltpu.VMEM((n_rows + HALO, CT), F32)],
        compiler_params=_cparams("parallel"),
    )(proj, proj, b_in, b_in, w, cb)


def _conv_bwd(name, dz1, proj, b_in, w):
    n_rows = proj.shape[0]

    def body(dz_ref, cv_ref, cg_ref, bcv_ref, bcg_ref, w_ref, dcv_ref, dcg_ref, sm_ref, zp_ref, dzp_ref):
        cvb = cv_ref[...] + bcv_ref[...]
        sg = jax.nn.sigmoid(cg_ref[...] + bcg_ref[...])
        zp_ref[0:HALO, :] = jnp.zeros((HALO, CT), F32)
        zp_ref[HALO:, :] = cvb * sg
        dz = dz_ref[...]
        dzp_ref[0:n_rows, :] = dz
        dzp_ref[n_rows:, :] = jnp.zeros((HALO, CT), F32)
        sm_ref[...] = jnp.zeros(sm_ref.shape, F32)
        sm_ref[32:33, :] = jnp.sum(dz, axis=0, keepdims=True)

        def blk(rb, carry):
            base = pl.multiple_of(rb * RB, RB)
            dwin = dzp_ref[pl.ds(base, RB + HALO), :]
            zwin = zp_ref[pl.ds(base, RB + HALO), :]
            dzb = dwin[0:RB]
            acc = jnp.zeros((RB, CT), F32)
            for k in range(KCONV):
                acc = acc + w_ref[k:k + 1, :] * _shifted(dwin, KCONV - 1 - k)
                sm_ref[k:k + 1, :] += jnp.sum(dzb * _shifted(zwin, k + 2), axis=0, keepdims=True)
            dzp_ref[pl.ds(base, RB), :] = acc
            return carry

        lax.fori_loop(0, n_rows // RB, blk, 0)
        dz0 = dzp_ref[0:n_rows, :]
        dcv = dz0 * sg
        dcg = dz0 * cvb * (sg * (1.0 - sg))
        dcv_ref[...] = dcv.astype(dcv_ref.dtype)
        dcg_ref[...] = dcg.astype(dcg_ref.dtype)
        sm_ref[33:34, :] = jnp.sum(dcv, axis=0, keepdims=True)
        sm_ref[34:35, :] = jnp.sum(dcg, axis=0, keepdims=True)

    col = pl.BlockSpec((n_rows, CT), lambda j: (0, j))
    return pl.pallas_call(
        body, name=name, grid=(D // CT,),
        in_specs=[col, pl.BlockSpec((n_rows, CT), lambda j: (0, CV0 + j)), pl.BlockSpec((n_rows, CT), lambda j: (0, CG0 + j)),
                  pl.BlockSpec((1, CT), lambda j: (0, CV0 + j)), pl.BlockSpec((1, CT), lambda j: (0, CG0 + j)),
                  pl.BlockSpec((HALO, CT), lambda j: (0, j))],
        out_specs=[col, col, pl.BlockSpec((40, CT), lambda j: (0, j))],
        out_shape=[jax.ShapeDtypeStruct((n_rows, D), BF16), jax.ShapeDtypeStruct((n_rows, D), BF16),
                   jax.ShapeDtypeStruct((40, D), F32)],
        scratch_shapes=[pltpu.VMEM((n_rows + HALO, CT), F32), pltpu.VMEM((n_rows + HALO, CT), F32)],
        compiler_params=_cparams("parallel"),
    )(dz1, proj, proj, b_in, b_in, w)


ADA_TN = 768


def _split_bf16(v):
    hi = v.astype(BF16)
    return hi, (v - hi.astype(F32)).astype(BF16)


def _ada_fwd(name, c_all, w, b):
    n = w.shape[1]

    def body(c_ref, w_ref, b_ref, o_ref):
        ch, cl = _split_bf16(_silu(c_ref[...]))
        wh, wl = _split_bf16(w_ref[...])
        acc = jnp.dot(ch, wl, preferred_element_type=F32) + jnp.dot(cl, wh, preferred_element_type=F32)
        o_ref[...] = acc + jnp.dot(ch, wh, preferred_element_type=F32) + b_ref[...]

    return pl.pallas_call(
        body, name=name, grid=(n // ADA_TN,),
        in_specs=[pl.BlockSpec((8, D), lambda j: (0, 0)), pl.BlockSpec((D, ADA_TN), lambda j: (0, j)),
                  pl.BlockSpec((1, ADA_TN), lambda j: (0, j))],
        out_specs=pl.BlockSpec((8, ADA_TN), lambda j: (0, j)),
        out_shape=jax.ShapeDtypeStruct((8, n), F32),
        compiler_params=_cparams("parallel"),
    )(c_all, w, b)


def _ada_wgrad(name, c_all_t, dmod):
    n = dmod.shape[1]

    def body(ct_ref, dm_ref, o_ref):
        ca = _silu(ct_ref[...])
        acc = ca[:, 0:1] * dm_ref[0:1, :]
        for r in range(1, 8):
            acc = acc + ca[:, r:r + 1] * dm_ref[r:r + 1, :]
        o_ref[...] = acc

    return pl.pallas_call(
        body, name=name, grid=(n // ADA_TN,),
        in_specs=[pl.BlockSpec((D, 8), lambda j: (0, 0)), pl.BlockSpec((8, ADA_TN), lambda j: (0, j))],
        out_specs=pl.BlockSpec((D, ADA_TN), lambda j: (0, j)),
        out_shape=jax.ShapeDtypeStruct((D, n), F32),
        compiler_params=_cparams("parallel"),
    )(c_all_t, dmod)


def _adamw(name, w, g, m, v, tr):
    rows, cols = w.shape

    def body(w_ref, g_ref, m_ref, v_ref, d_ref, nm_ref, nv_ref):
        gv = g_ref[...]
        nm = ADAM_B1 * m_ref[...] + (1.0 - ADAM_B1) * gv
        nv = ADAM_B2 * v_ref[...] + (1.0 - ADAM_B2) * (gv * gv)
        m_hat = nm / (1.0 - ADAM_B1 ** ADAM_STEP)
        v_hat = nv / (1.0 - ADAM_B2 ** ADAM_STEP)
        d_ref[...] = -ADAM_LR * (m_hat / (jnp.sqrt(v_hat) + ADAM_EPS) + ADAM_WD * w_ref[...])
        nm_ref[...] = nm
        nv_ref[...] = nv

    spec = pl.BlockSpec((tr, cols), lambda i: (i, 0))
    sds = jax.ShapeDtypeStruct((rows, cols), F32)
    return pl.pallas_call(
        body, name=name, grid=(rows // tr,), in_specs=[spec] * 4, out_specs=[spec] * 3, out_shape=[sds] * 3,
        compiler_params=_cparams("parallel"),
    )(w, g, m, v)


def _sum8(name, stacked, tr):
    n, rows, cols = stacked.shape

    def body(s_ref, o_ref):
        acc = s_ref[0]
        for r in range(1, n):
            acc = acc + s_ref[r]
        o_ref[...] = acc

    return pl.pallas_call(
        body, name=name, grid=(rows // tr,),
        in_specs=[pl.BlockSpec((n, tr, cols), lambda i: (0, i, 0))],
        out_specs=pl.BlockSpec((tr, cols), lambda i: (i, 0)),
        out_shape=jax.ShapeDtypeStruct((rows, cols), F32),
        compiler_params=_cparams("parallel"),
    )(stacked)


def _pair_add(name, g5, other, c_idx, tr):
    nq, _, rows, cols = g5.shape

    def body(c_ref, g_ref, o_ref, p_ref):
        p_ref[...] = (g_ref[...] + o_ref[...]).astype(p_ref.dtype)

    return pl.pallas_call(
        body, name=name,
        grid_spec=pltpu.PrefetchScalarGridSpec(
            num_scalar_prefetch=1, grid=(nq, rows // tr),
            in_specs=[pl.BlockSpec((None, None, tr, cols), lambda qi, i, cr: (qi, cr[0], i, 0)),
                      pl.BlockSpec((None, tr, cols), lambda qi, i, cr: (qi, i, 0))],
            out_specs=pl.BlockSpec((None, tr, cols), lambda qi, i, cr: (qi, i, 0))),
        out_shape=jax.ShapeDtypeStruct((nq, rows, cols), BF16),
        compiler_params=_cparams("parallel", "parallel"),
    )(c_idx, g5, other)


def _chip_add(name, p, recv, qc_idx, tr):
    _, rows, cols = p.shape

    def body(qc_ref, p_ref, r_ref, o_ref):
        acc = p_ref[...].astype(F32)
        for k in range(NCHIP - 1):
            acc = acc + r_ref[k].astype(F32)
        o_ref[...] = acc

    return pl.pallas_call(
        body, name=name,
        grid_spec=pltpu.PrefetchScalarGridSpec(
            num_scalar_prefetch=1, grid=(rows // tr,),
            in_specs=[pl.BlockSpec((None, tr, cols), lambda i, qc: (qc[0], i, 0)),
                      pl.BlockSpec((NCHIP - 1, tr, cols), lambda i, qc: (0, i, 0))],
            out_specs=pl.BlockSpec((None, tr, cols), lambda i, qc: (qc[1], i, 0))),
        out_shape=jax.ShapeDtypeStruct((2, rows, cols), F32),
        compiler_params=_cparams("parallel"),
    )(qc_idx, p, recv)


def _allgather8(name, blk):
    m_per, n = blk.shape

    def body(x_ref, out_ref, send_sems, recv_sems, local_sem):
        x, y, c = lax.axis_index("x"), lax.axis_index("y"), lax.axis_index("c")
        me, sibling = (x, y, c), (x, y, 1 - c)
        chips = [(1 - x, y), (x, 1 - y), (1 - x, 1 - y)]

        def rows(px, py, pc):
            return out_ref.at[pl.ds((4 * px + 2 * py + pc) * m_per, m_per), :]

        def copy(k, block, to, src=None):
            return pltpu.make_async_remote_copy(
                src_ref=rows(*block) if src is None else src, dst_ref=rows(*block),
                send_sem=send_sems.at[k], recv_sem=recv_sems.at[k], device_id=to, device_id_type=MESH)

        mine = pltpu.make_async_copy(x_ref, rows(*me), local_sem)
        mine.start()
        first = [copy(0, me, sibling, src=x_ref)]
        first += [copy(1 + j, me, (*chip, c), src=x_ref) for j, chip in enumerate(chips)]
        for cp in first:
            cp.start()
        passed = [copy(4 + j, (*chip, c), sibling) for j, chip in enumerate(chips)]
        for j, chip in enumerate(chips):
            copy(1 + j, (*chip, c), me).wait_recv()
            passed[j].start()
        copy(0, sibling, me).wait_recv()
        for j, chip in enumerate(chips):
            copy(4 + j, (*chip, 1 - c), me).wait_recv()
        for cp in first + passed:
            cp.wait_send()
        mine.wait()

    return pl.pallas_call(
        body, name=name,
        out_shape=jax.ShapeDtypeStruct((8 * m_per, n), blk.dtype),
        in_specs=[pl.BlockSpec(memory_space=pltpu.VMEM)],
        out_specs=pl.BlockSpec(memory_space=pltpu.VMEM),
        scratch_shapes=[pltpu.SemaphoreType.DMA((7,)), pltpu.SemaphoreType.DMA((7,)), pltpu.SemaphoreType.DMA],
        compiler_params=pltpu.CompilerParams(vmem_limit_bytes=VMEM_LIMIT),
    )(blk)


def _exchange(name, srcs, out_shapes, n_local, n_remote, plan, aliases=None):
    ni, no = len(srcs), len(out_shapes)

    def body(*refs):
        ins, outs = refs[:ni], refs[ni:ni + no]
        send_sems, recv_sems, local_sems = refs[ni + no:]
        x, y, c = lax.axis_index("x"), lax.axis_index("y"), lax.axis_index("c")
        local, remote = plan(ins, outs, x, y, c)
        assert len(local) == n_local and len(remote) == n_remote

        def rcopy(i, dst):
            s, _, peer, _, _ = remote[i]
            return pltpu.make_async_remote_copy(src_ref=s, dst_ref=dst, send_sem=send_sems.at[i],
                                                recv_sem=recv_sems.at[i], device_id=peer, device_id_type=MESH)

        lcs = [pltpu.make_async_copy(s, d, local_sems.at[i]) for i, (s, d) in enumerate(local)]
        for cp in lcs:
            cp.start()
        first = [i for i in range(n_remote) if remote[i][4] is None]
        passed = [i for i in range(n_remote) if remote[i][4] is not None]
        for i in first:
            rcopy(i, remote[i][1]).start()
        arrived = set()
        for i in passed:
            j = remote[i][4]
            rcopy(j, remote[j][3]).wait_recv()
            arrived.add(j)
            rcopy(i, remote[i][1]).start()
        for i in range(n_remote):
            if i not in arrived:
                rcopy(i, remote[i][3]).wait_recv()
        for i in range(n_remote):
            rcopy(i, remote[i][1]).wait_send()
        for cp in lcs:
            cp.wait()

    any_spec = pl.BlockSpec(memory_space=pl.ANY)
    return pl.pallas_call(
        body, name=name, out_shape=out_shapes,
        in_specs=[any_spec] * ni, out_specs=[any_spec] * no,
        input_output_aliases=aliases or {},
        scratch_shapes=[pltpu.SemaphoreType.DMA((n_remote,)), pltpu.SemaphoreType.DMA((n_remote,)),
                        pltpu.SemaphoreType.DMA((max(n_local, 1),))],
    )(*srcs)


_CHIP_FLIPS = ((0, 1), (1, 0), (1, 1))


def _flip(v, f):
    return 1 - v if f else v


EFFECT = pltpu.SideEffectType.DATAFLOW_SIDE_EFFECTING


def _split_start(name, bufs, n, plan):
    nb = len(bufs)

    def body(*refs):
        ins = refs[:nb]
        send_sems, recv_sems, token = refs[nb], refs[nb + 1], refs[-1]
        x, y, c = lax.axis_index("x"), lax.axis_index("y"), lax.axis_index("c")
        copies = plan(ins, x, y, c)
        assert len(copies) == n
        for i, (s, d, peer, _) in enumerate(copies):
            pltpu.make_async_remote_copy(src_ref=s, dst_ref=d, send_sem=send_sems.at[i], recv_sem=recv_sems.at[i],
                                         device_id=peer, device_id_type=MESH).start()
        token[...] = jnp.zeros_like(token)

    hbm = pl.BlockSpec(memory_space=pltpu.HBM)
    sem = pl.BlockSpec(memory_space=pltpu.SEMAPHORE)
    res = pl.pallas_call(
        body, name=name,
        out_shape=(pltpu.SemaphoreType.DMA((n,)), pltpu.SemaphoreType.DMA((n,)),
                   *[pltpu.HBM(b.shape, b.dtype) for b in bufs], jax.ShapeDtypeStruct((8, 128), F32)),
        in_specs=[hbm] * nb,
        out_specs=(sem, sem, *[hbm] * nb, pl.BlockSpec(memory_space=pltpu.VMEM)),
        input_output_aliases={i: 2 + i for i in range(nb)},
        compiler_params=pltpu.CompilerParams(has_side_effects=EFFECT),
    )(*[pltpu.with_memory_space_constraint(b, pltpu.HBM) for b in bufs])
    return res[0], res[1], list(res[2:2 + nb]), res[-1]


def _split_wait(name, bufs, send_sems, recv_sems, after, n, plan):
    nb = len(bufs)

    def body(*refs):
        ins = refs[:nb]
        ssem, rsem = refs[nb], refs[nb + 1]
        x, y, c = lax.axis_index("x"), lax.axis_index("y"), lax.axis_index("c")
        copies = plan(ins, x, y, c)
        assert len(copies) == n
        for i, (s, _, peer, lands) in enumerate(copies):
            cp = pltpu.make_async_remote_copy(src_ref=s, dst_ref=lands, send_sem=ssem.at[i], recv_sem=rsem.at[i],
                                              device_id=peer, device_id_type=MESH)
            cp.wait_send()
            cp.wait_recv()

    hbm = pl.BlockSpec(memory_space=pltpu.HBM)
    sem = pl.BlockSpec(memory_space=pltpu.SEMAPHORE)
    res = pl.pallas_call(
        body, name=name,
        out_shape=[pltpu.HBM(b.shape, b.dtype) for b in bufs],
        in_specs=[hbm] * nb + [sem, sem, pl.BlockSpec(memory_space=pl.ANY)],
        out_specs=[hbm] * nb,
        input_output_aliases={i: i for i in range(nb)},
        compiler_params=pltpu.CompilerParams(has_side_effects=EFFECT),
    )(*bufs, send_sems, recv_sems, after)
    return list(res)


def _chips_of(x, y):
    return [(_flip(x, fx), _flip(y, fy)) for fx, fy in _CHIP_FLIPS]


def _gather_start(tag, stacks):
    n = len(stacks)

    def plan(refs, x, y, c):
        q = 2 * x + y
        return [(refs[a].at[q, c], refs[a].at[q, c], (px, py, c), refs[a].at[2 * px + py, c])
                for a in range(n) for px, py in _chips_of(x, y)]

    send, recv, thru, token = _split_start(f"gather_start_{tag}", stacks, 3 * n, plan)
    return (send, recv, thru, plan), token


def _gather_finish(tag, state, after):
    send, recv, thru, plan = state
    n = len(thru)
    landed = _split_wait(f"gather_wait_{tag}", thru, send, recv, after, 3 * n, plan)

    def pass_on(ins, outs, x, y, c):
        sib = (x, y, 1 - c)
        remote = []
        for a in range(n):
            for px, py in _chips_of(x, y):
                p = 2 * px + py
                remote.append((ins[a].at[p, c], outs[a].at[p, c], sib, outs[a].at[p, 1 - c], None))
        return [], remote

    full = _exchange(f"gather_pass_{tag}", landed, [jax.ShapeDtypeStruct(s.shape, s.dtype) for s in landed],
                     0, 3 * n, pass_on, aliases={a: a for a in range(n)})
    return [g.reshape(NCHIP, 2 * g.shape[2], g.shape[3]) for g in full]


def _gather_small(name, smalls):
    n = len(smalls)

    def plan(ins, outs, x, y, c):
        q = 2 * x + y
        local = [(ins[b], outs[b].at[q]) for b in range(n)]
        remote = [(ins[b], outs[b].at[q], (px, py, c), outs[b].at[2 * px + py], None)
                  for b in range(n) for px, py in _chips_of(x, y)]
        return local, remote

    return _exchange(name, smalls, [jax.ShapeDtypeStruct((NCHIP,) + s.shape, s.dtype) for s in smalls], n, 3 * n, plan)


def _rs_start(tag, grads, c_idx):
    n = len(grads)
    g5 = [g.reshape(NCHIP, 2, g.shape[1] // 2, g.shape[2]) for g in grads]
    half = [(g.shape[2], g.shape[3]) for g in g5]

    def swap(ins, outs, x, y, c):
        sib = (x, y, 1 - c)
        return [], [(ins[a].at[:, 1 - c], outs[a], sib, outs[a], None) for a in range(n)]

    got = _exchange(f"rs_pair_{tag}", g5, [jax.ShapeDtypeStruct((NCHIP,) + h, F32) for h in half], 0, n, swap)
    part = [_pair_add(f"rs_pair_add_{tag}{a}", g5[a], got[a], c_idx, _tile(half[a][0], 512)) for a in range(n)]
    land = [lax.empty((NCHIP - 1,) + h, BF16) for h in half]

    def plan(refs, x, y, c):
        return [(refs[a].at[2 * px + py], refs[n + a].at[k], (px, py, c), refs[n + a].at[k])
                for a in range(n) for k, (px, py) in enumerate(_chips_of(x, y))]

    send, recv, thru, token = _split_start(f"rs_chips_start_{tag}", part + land, 3 * n, plan)
    return (send, recv, thru, plan, half), token


def _rs_finish(tag, state, after, qc_idx):
    send, recv, thru, plan, half = state
    n = len(half)
    landed = _split_wait(f"rs_chips_wait_{tag}", thru, send, recv, after, 3 * n, plan)
    red = [_chip_add(f"rs_chip_add_{tag}{a}", landed[a], landed[n + a], qc_idx, _tile(half[a][0], 512)) for a in range(n)]

    def share(ins, outs, x, y, c):
        sib = (x, y, 1 - c)
        return [], [(ins[a].at[c], outs[a].at[c], sib, outs[a].at[1 - c], None) for a in range(n)]

    full = _exchange(f"rs_share_{tag}", red, [jax.ShapeDtypeStruct((2,) + h, F32) for h in half], 0, n, share,
                     aliases={a: a for a in range(n)})
    return [f.reshape(2 * h[0], h[1]) for f, h in zip(full, half)]


def _pack(name, n_rows, pieces):
    arrays = [p[1] for p in pieces if p[0] != "zeros"]

    def body(*refs):
        o_ref = refs[-1]
        o_ref[...] = jnp.zeros(o_ref.shape, F32)
        row, ai = 0, 0
        for p in pieces:
            kind = p[0]
            if kind == "zeros":
                row += p[1]
                continue
            ref = refs[ai]
            ai += 1
            if kind == "rows":
                r = ref.shape[0]
                o_ref[row:row + r, :] = ref[...]
            elif kind == "wide":
                r = ref.shape[1] // D
                for j in range(r):
                    o_ref[row + j:row + j + 1, :] = ref[:, j * D:(j + 1) * D]
            elif kind == "slice":
                r = p[3] - p[2]
                o_ref[row:row + r, :] = ref[p[2]:p[3], :]
            elif kind == "heads":
                r = CHUNK
                for h in range(HEADS):
                    o_ref[row:row + r, h * CHUNK:(h + 1) * CHUNK] = ref[h]
            else:
                r = 1
                o_ref[row:row + 1, 0:128] = ref[0:1, :]
            row += r
        assert row == n_rows, (row, n_rows)

    vmem = pl.BlockSpec(memory_space=pltpu.VMEM)
    return pl.pallas_call(
        body, name=name, out_shape=jax.ShapeDtypeStruct((n_rows, D), F32),
        in_specs=[vmem] * len(arrays), out_specs=vmem,
        compiler_params=pltpu.CompilerParams(vmem_limit_bytes=VMEM_LIMIT),
    )(*arrays)


def _to_bf16_stack(name, w, q_idx):
    r, cols = w.shape
    half = r // 2
    tr = _tile(half, 256)
    nb = half // tr

    def body(q_ref, w_ref, o_ref):
        o_ref[...] = w_ref[...].astype(BF16)

    return pl.pallas_call(
        body, name=name,
        grid_spec=pltpu.PrefetchScalarGridSpec(
            num_scalar_prefetch=1, grid=(2 * nb,),
            in_specs=[pl.BlockSpec((tr, cols), lambda i, qr: (i, 0))],
            out_specs=pl.BlockSpec((None, None, tr, cols), lambda i, qr: (qr[0], i // nb, i % nb, 0))),
        out_shape=jax.ShapeDtypeStruct((NCHIP, 2, half, cols), BF16),
        compiler_params=_cparams("parallel"),
    )(q_idx, w)


def kernel(x, c, ada_w, ada_b, norm_ffn1, ffn1_w_gate, ffn1_w_up, ffn1_w_down, norm_mix, mix_w_in, mix_b_in, sgu_ln_g, sgu_ln_b, sgu_w_s, sgu_b_s, conv_w, conv_b, conv_ln_g, conv_ln_b, w_branch_a, w_branch_b, w_out, norm_ffn2, ffn2_w_gate, ffn2_w_up, ffn2_w_down, norm_final, loss_target, m_ada_w, m_ada_b, m_norm_ffn1, m_ffn1_w_gate, m_ffn1_w_up, m_ffn1_w_down, m_norm_mix, m_mix_w_in, m_mix_b_in, m_sgu_ln_g, m_sgu_ln_b, m_sgu_w_s, m_sgu_b_s, m_conv_w, m_conv_b, m_conv_ln_g, m_conv_ln_b, m_w_branch_a, m_w_branch_b, m_w_out, m_norm_ffn2, m_ffn2_w_gate, m_ffn2_w_up, m_ffn2_w_down, m_norm_final, v_ada_w, v_ada_b, v_norm_ffn1, v_ffn1_w_gate, v_ffn1_w_up, v_ffn1_w_down, v_norm_mix, v_mix_w_in, v_mix_b_in, v_sgu_ln_g, v_sgu_ln_b, v_sgu_w_s, v_sgu_b_s, v_conv_w, v_conv_b, v_conv_ln_g, v_conv_ln_b, v_w_branch_a, v_w_branch_b, v_w_out, v_norm_ffn2, v_ffn2_w_gate, v_ffn2_w_up, v_ffn2_w_down, v_norm_final):
    xi, yi, ci = lax.axis_index("x"), lax.axis_index("y"), lax.axis_index("c")
    q = 2 * xi + yi
    dev = 4 * xi + 2 * yi + ci
    c_idx = jnp.reshape(ci, (1,)).astype(jnp.int32)
    q_idx = jnp.reshape(q, (1,)).astype(jnp.int32)
    x0 = x[0]
    tgt = loss_target[0]
    qc_idx = jnp.stack([q, ci]).astype(jnp.int32)
    n_ada = ada_w.shape[2]

    big = dict(g1=ffn1_w_gate, u1=ffn1_w_up, d1=ffn1_w_down, win=mix_w_in, wa=w_branch_a, wb=w_branch_b, wo=w_out,
               g2=ffn2_w_gate, u2=ffn2_w_up, d2=ffn2_w_down)
    groups = dict(f1=("g1", "u1", "d1"), mx=("win", "wa", "wb", "wo"), f2=("g2", "u2", "d2"))
    gather_state = {}
    started = jnp.zeros((1, 1), F32)
    for tag, keys in groups.items():
        gather_state[tag], token = _gather_start(tag, [_to_bf16_stack(f"cast_{k}", big[k][0], q_idx) for k in keys])
        started = started + token[0:1, 0:1]

    c_all = _allgather8("ag_c", jnp.pad(c, ((0, 7), (0, 0))) + started)[::8]
    ada_b_mine = lax.dynamic_slice(ada_b, (0, q * n_ada), (1, n_ada))
    mod_part = _ada_fwd("ada_fwd", c_all, ada_w[0], ada_b_mine)
    small_parts = _gather_small("gather_small", [mod_part, jnp.pad(conv_w[0], ((0, 1), (0, 0)))])
    mod_all = jnp.transpose(small_parts[0], (1, 0, 2)).reshape(8, N_MOD * D)
    mod = lax.dynamic_slice(mod_all, (dev, 0), (1, N_MOD * D))
    sh1, sc1, g1, sh2, sc2, g2, sh3, sc3, g3 = [mod[:, k * D:(k + 1) * D] for k in range(N_MOD)]
    convw = jnp.transpose(small_parts[1], (1, 0, 2)).reshape(HALO, D)
    W = dict(zip(groups["f1"], _gather_finish("f1", gather_state["f1"], mod)))

    causal = jnp.tril(jnp.ones((CHUNK, CHUNK), bool))
    wm_f = jnp.where(causal[None], sgu_w_s[0], 0.0)
    wm = wm_f.astype(BF16)
    wmt = jnp.swapaxes(wm_f, 1, 2).astype(BF16)
    bst = jnp.repeat(jnp.transpose(sgu_b_s[0]), CHUNK, axis=1)
    b_in = mix_b_in

    (h1,) = _rowwise_fwd("pre1", _f_pre, [x0], [norm_ffn1, sc1, sh1], [(D, BF16)], 256)
    a1, b1, s1 = _ffn_up("up1", h1, W["g1"], W["u1"])
    f1 = _matmul("down1", s1, W["d1"], "fwd", False, F32)
    x1, h2 = _rowwise_fwd("res1", _f_res_pre(0.5), [x0, f1], [g1, norm_mix, sc2, sh2], [(D, F32), (D, BF16)], 256)
    W.update(zip(groups["mx"], _gather_finish("mx", gather_state["mx"], x1)))
    proj =_matmul("in_proj", h2, W["win"], "fwd", True, F32)
    ya = _sgu_fwd("sgu", proj, b_in, sgu_ln_g, sgu_ln_b, wm, bst, 256)
    z1 = _conv_fwd("conv", proj, b_in, convw, conv_b)
    (z3,) = _rowwise_fwd("conv_ln", _f_lnsilu, [z1], [conv_ln_g, conv_ln_b], [(D, BF16)], 256)
    y_a = _matmul("branch_a", ya, W["wa"], "fwd", False, F32)
    y_b = _matmul("branch_b", z3, W["wb"], "fwd", False, F32)
    (merged,) = _rowwise_fwd("merge", _f_merge, [(proj, 2 * D, 2), y_a, y_b], [(b_in, 2 * D, 2)], [(D, BF16)], 256)
    y = _matmul("out_proj", merged, W["wo"], "fwd", False, F32)
    x2, h3 = _rowwise_fwd("res2", _f_res_pre(1.0), [x1, y], [g2, norm_ffn2, sc3, sh3], [(D, F32), (D, BF16)], 256)
    W.update(zip(groups["f2"], _gather_finish("f2", gather_state["f2"], x2)))
    a3, b3, s3 = _ffn_up("up2", h3, W["g2"], W["u2"])
    f3 = _matmul("down2", s3, W["d2"], "fwd", False, F32)
    dx2, df3, dg3, dnwf, loss_blk = _final("final", x2, f3, tgt, g3, norm_final.reshape(1, D), 256)

    G = {}

    def ffn_bwd(tag, df, s, a, b, h, wg, wu, wd):
        da, db = _dgrad_glu(f"d_down{tag}", df, wd, a, b)
        G["d" + tag] = _matmul(f"w_down{tag}", s, df, "wgrad", False, F32)
        dh = _matmul(f"d_gate{tag}", da, wg, "dgrad", True, F32)
        dh = _matmul(f"d_up{tag}", db, wu, "dgrad", True, F32, add=dh)
        G["g" + tag] = _matmul(f"w_gate{tag}", h, da, "wgrad", True, F32)
        G["u" + tag] = _matmul(f"w_up{tag}", h, db, "wgrad", True, F32)
        return dh

    reduced = {}
    dh3 = ffn_bwd("2", df3, s3, a3, b3, h3, W["g2"], W["u2"], W["d2"])
    rs_f2, token = _rs_start("f2", [G[k] for k in groups["f2"]], c_idx)
    (dx1, dy), (dg2, dnw3, dsc3, dsh3) = _rowwise_bwd(
        "res2_b", _f_res_pre(1.0), [x1, y], [g2 + token[0:1, 0:1], norm_ffn2, sc3, sh3], [dx2, dh3], [F32, BF16], 256)
    dmerged = _matmul("d_out", dy, W["wo"], "dgrad", False, F32)
    G["wo"] = _matmul("w_out", merged, dy, "wgrad", False, F32)
    (dgab, dy_a, dy_b), (dbgab,) = _rowwise_bwd(
        "merge_b", _f_merge, [(proj, 2 * D, 2), y_a, y_b], [(b_in, 2 * D, 2)], [dmerged], [BF16, BF16, BF16], 256)
    dya = _matmul("d_a", dy_a, W["wa"], "dgrad", False, F32)
    dz3 = _matmul("d_b", dy_b, W["wb"], "dgrad", False, F32)
    G["wa"] = _matmul("w_a", ya, dy_a, "wgrad", False, F32)
    G["wb"] = _matmul("w_b", z3, dy_b, "wgrad", False, F32)
    (dz1,), (dclg, dclb) = _rowwise_bwd("conv_ln_b", _f_lnsilu, [z1], [conv_ln_g, conv_ln_b], [dz3], [F32], 256)
    dcv, dcg, conv_small = _conv_bwd("conv_b", dz1, proj, b_in, convw)
    duv, dbuv, dlng, dlnb, dws, dbs = _sgu_bwd("sgu_b", proj, b_in, sgu_ln_g, sgu_ln_b, wm, wmt, bst, dya, 256)
    dproj = jnp.concatenate([duv, dcv, dcg, dgab], axis=1)
    dh2 = _matmul("d_in", dproj, W["win"], "dgrad", True, F32)
    G["win"] = _matmul("w_in", h2, dproj, "wgrad", True, F32)
    reduced.update(zip(groups["f2"], _rs_finish("f2", rs_f2, dh2, qc_idx)))
    rs_mx, token = _rs_start("mx", [G[k] for k in groups["mx"]], c_idx)
    (dx0, df1), (dg1, dnw2, dsc2, dsh2) = _rowwise_bwd(
        "res1_b", _f_res_pre(0.5), [x0, f1], [g1 + token[0:1, 0:1], norm_mix, sc2, sh2], [dx1, dh2], [F32, BF16], 256)
    dh1 = ffn_bwd("1", df1, s1, a1, b1, h1, W["g1"], W["u1"], W["d1"])
    reduced.update(zip(groups["mx"], _rs_finish("mx", rs_mx, dh1, qc_idx)))
    rs_f1, token = _rs_start("f1", [G[k] for k in groups["f1"]], c_idx)
    (grad_x,), (dnw1, dsc1, dsh1) = _rowwise_bwd(
        "pre1_b", _f_pre_keep, [x0], [norm_ffn1, sc1 + token[0:1, 0:1], sh1], [dx0, dh1], [F32], 256)

    dbs_row = jnp.transpose(dbs[:, ::CHUNK]).reshape(1, D)
    small = _pack("pack_small", 192, [
        ("rows", dnw1), ("rows", dnw2), ("wide", dbuv), ("slice", conv_small, 33, 34), ("slice", conv_small, 34, 35),
        ("wide", dbgab), ("rows", dlng), ("rows", dlnb), ("rows", dbs_row), ("slice", conv_small, 32, 33),
        ("rows", dclg), ("rows", dclb), ("rows", dnw3), ("rows", dnwf),
        ("rows", dsh1), ("rows", dsc1), ("rows", dg1), ("rows", dsh2), ("rows", dsc2), ("rows", dg2),
        ("rows", dsh3), ("rows", dsc3), ("rows", dg3), ("lanes", loss_blk), ("zeros", 6),
        ("slice", conv_small, 0, HALO), ("heads", dws)])
    n_small = small.shape[0]
    every = _allgather8("ag_small", small).reshape(8, n_small, D)
    tot = _sum8("sum_small", every, 64)
    loss = tot[25, 0]
    dmod_all = every[:, 16:16 + N_MOD, :].reshape(8, N_MOD * D)
    dmod_mine = lax.dynamic_slice(dmod_all, (0, q * n_ada), (8, n_ada))
    g_ada_w = _ada_wgrad("ada_wgrad", jnp.transpose(c_all), dmod_mine)

    given = dict(ada_w=(ada_w, m_ada_w, v_ada_w), ada_b=(ada_b, m_ada_b, v_ada_b),
                 norm_ffn1=(norm_ffn1, m_norm_ffn1, v_norm_ffn1),
                 ffn1_w_gate=(ffn1_w_gate, m_ffn1_w_gate, v_ffn1_w_gate), ffn1_w_up=(ffn1_w_up, m_ffn1_w_up, v_ffn1_w_up),
                 ffn1_w_down=(ffn1_w_down, m_ffn1_w_down, v_ffn1_w_down), norm_mix=(norm_mix, m_norm_mix, v_norm_mix),
                 mix_w_in=(mix_w_in, m_mix_w_in, v_mix_w_in), mix_b_in=(mix_b_in, m_mix_b_in, v_mix_b_in),
                 sgu_ln_g=(sgu_ln_g, m_sgu_ln_g, v_sgu_ln_g), sgu_ln_b=(sgu_ln_b, m_sgu_ln_b, v_sgu_ln_b),
                 sgu_w_s=(sgu_w_s, m_sgu_w_s, v_sgu_w_s), sgu_b_s=(sgu_b_s, m_sgu_b_s, v_sgu_b_s),
                 conv_w=(conv_w, m_conv_w, v_conv_w), conv_b=(conv_b, m_conv_b, v_conv_b),
                 conv_ln_g=(conv_ln_g, m_conv_ln_g, v_conv_ln_g), conv_ln_b=(conv_ln_b, m_conv_ln_b, v_conv_ln_b),
                 w_branch_a=(w_branch_a, m_w_branch_a, v_w_branch_a), w_branch_b=(w_branch_b, m_w_branch_b, v_w_branch_b),
                 w_out=(w_out, m_w_out, v_w_out), norm_ffn2=(norm_ffn2, m_norm_ffn2, v_norm_ffn2),
                 ffn2_w_gate=(ffn2_w_gate, m_ffn2_w_gate, v_ffn2_w_gate), ffn2_w_up=(ffn2_w_up, m_ffn2_w_up, v_ffn2_w_up),
                 ffn2_w_down=(ffn2_w_down, m_ffn2_w_down, v_ffn2_w_down), norm_final=(norm_final, m_norm_final, v_norm_final))
    out = {}

    def big_update(name, g2d):
        w, m, v = given[name]
        shp = w.shape
        d_, nm, nv = _adamw(f"adamw_{name}", w.reshape(g2d.shape), g2d, m.reshape(g2d.shape), v.reshape(g2d.shape),
                            _tile(g2d.shape[0], 512 if g2d.shape[1] <= D else 256))
        out[name] = tuple(t.reshape(shp) for t in (g2d, d_, nm, nv))

    big_names = dict(g1="ffn1_w_gate", u1="ffn1_w_up", d1="ffn1_w_down", win="mix_w_in", wa="w_branch_a",
                     wb="w_branch_b", wo="w_out", g2="ffn2_w_gate", u2="ffn2_w_up", d2="ffn2_w_down")
    for key in groups["f2"] + groups["mx"]:
        big_update(big_names[key], reduced[key])
    big_update("ada_w", g_ada_w)

    small_rows = [("norm_ffn1", 0, 1), ("norm_mix", 1, 1), ("mix_b_in", 2, 6), ("sgu_ln_g", 8, 1), ("sgu_ln_b", 9, 1),
                  ("sgu_b_s", 10, 1), ("conv_b", 11, 1), ("conv_ln_g", 12, 1), ("conv_ln_b", 13, 1),
                  ("norm_ffn2", 14, 1), ("norm_final", 15, 1), ("ada_b", 16, N_MOD)]

    def pack_rows(which, label):
        pieces = []
        for nm, _, r in small_rows:
            t = given[nm][which]
            pieces.append(("wide", t) if r > 1 else ("rows", t.reshape(1, D)))
        return _pack(f"pack_{label}", HALO, pieces + [("zeros", HALO - 16 - N_MOD)])

    sd, sm, sv = _adamw("adamw_small", pack_rows(0, "w"), tot, pack_rows(1, "m"), pack_rows(2, "v"), HALO)
    for nm, r0, r in small_rows:
        shp = given[nm][0].shape
        out[nm] = tuple(t[r0:r0 + r].reshape(shp) for t in (tot, sd, sm, sv))
    n_cw = conv_w.shape[2]
    g_cw = lax.dynamic_slice(tot, (HALO, q * n_cw), (HALO, n_cw))
    pad_cw = lambda t: jnp.pad(t[0], ((0, 1), (0, 0)))
    cd, cm, cv_ = _adamw("adamw_conv_w", pad_cw(conv_w), g_cw, pad_cw(m_conv_w), pad_cw(v_conv_w), HALO)
    out["conv_w"] = tuple(t[:KCONV][None] for t in (g_cw, cd, cm, cv_))
    g_ws = jnp.transpose(tot[2 * HALO:].reshape(CHUNK, HEADS, CHUNK), (1, 0, 2)).reshape(HEADS * CHUNK, CHUNK)
    flat_ws = lambda t: t.reshape(HEADS * CHUNK, CHUNK)
    wd_, wm_, wv_ = _adamw("adamw_sgu_w_s", flat_ws(sgu_w_s), g_ws, flat_ws(m_sgu_w_s), flat_ws(v_sgu_w_s), 512)
    out["sgu_w_s"] = tuple(t.reshape(sgu_w_s.shape) for t in (g_ws, wd_, wm_, wv_))

    reduced.update(zip(groups["f1"], _rs_finish("f1", rs_f1, wd_, qc_idx)))
    for key in groups["f1"]:
        big_update(big_names[key], reduced[key])

    order = ["ada_w", "ada_b", "norm_ffn1", "ffn1_w_gate", "ffn1_w_up", "ffn1_w_down", "norm_mix", "mix_w_in", "mix_b_in",
             "sgu_ln_g", "sgu_ln_b", "sgu_w_s", "sgu_b_s", "conv_w", "conv_b", "conv_ln_g", "conv_ln_b", "w_branch_a",
             "w_branch_b", "w_out", "norm_ffn2", "ffn2_w_gate", "ffn2_w_up", "ffn2_w_down", "norm_final"]
    return (loss, grad_x[None], *[out[n][0] for n in order], *[out[n][1] for n in order],
            *[out[n][2] for n in order], *[out[n][3] for n in order])
```

```python
import functools

import jax
import jax.numpy as jnp
from jax import lax
from jax.experimental import pallas as pl
from jax.experimental.pallas import tpu as pltpu

F32 = jnp.float32
BF16 = jnp.bfloat16
D = 1024
HEADS = 8
CHUNK = 128
KCONV = 31
HALO = 32
N_MOD = 9
EPS = 1e-6
NCHIP = 4
VMEM_LIMIT = 56 * 1024 * 1024
MESH = pl.DeviceIdType.MESH

ADAM_LR = 0.001
ADAM_B1 = 0.9
ADAM_B2 = 0.999
ADAM_EPS = 1e-08
ADAM_WD = 0.01
ADAM_STEP = 10

NN = (((1,), (0,)), ((), ()))
NT = (((1,), (1,)), ((), ()))
TN = (((0,), (0,)), ((), ()))


def _cparams(*sem):
    return pltpu.CompilerParams(dimension_semantics=sem or None, vmem_limit_bytes=VMEM_LIMIT)


def _rms(x, w):
    return x * lax.rsqrt(jnp.mean(x * x, axis=-1, keepdims=True) + EPS) * w


def _ln(x, g, b):
    mu = jnp.mean(x, axis=-1, keepdims=True)
    xc = x - mu
    var = jnp.mean(xc * xc, axis=-1, keepdims=True)
    return xc * lax.rsqrt(var + EPS) * g + b


def _silu(x):
    return x * jax.nn.sigmoid(x)


def _gelu(x):
    return x * (0.5 * (1.0 + jnp.tanh(0.7978845608028654 * (x + 0.044715 * (x * x * x)))))


def _f_pre(x, nw, sc, sh):
    return _rms(x, nw) * (1.0 + sc) + sh


def _f_pre_keep(x, nw, sc, sh):
    return x, _f_pre(x, nw, sc, sh)


def _f_res_pre(scale):
    def fn(x, f, g, nw, sc, sh):
        xn = x + (scale * g) * f
        return xn, _f_pre(xn, nw, sc, sh)
    return fn


def _f_glu(a, b):
    return _silu(a) * b


def _f_merge(gab, ya, yb, bgab):
    g = gab + bgab
    return jax.nn.sigmoid(g[:, :D]) * ya + jax.nn.sigmoid(g[:, D:]) * yb


def _f_lnsilu(z, g, b):
    return _silu(_ln(z, g, b))


def _f_mixa(uv, buv, g, b):
    t = uv + buv
    return _gelu(t[:, :D]), _ln(_gelu(t[:, D:]), g, b)


def _col(t):
    return t if isinstance(t, tuple) else (t, t.shape[-1], 0)


def _rowwise_fwd(name, fn, rows, params, outs, tr):
    rows = [_col(r) for r in rows]
    params = [_col(p) for p in params]
    n_rows = rows[0][0].shape[0]
    nin = len(rows) + len(params)

    def body(*refs):
        res = fn(*[r[...].astype(F32) for r in refs[:nin]])
        res = res if isinstance(res, tuple) else (res,)
        for o, v in zip(refs[nin:], res, strict=True):
            o[...] = v.astype(o.dtype)

    return pl.pallas_call(
        body, name=name, grid=(n_rows // tr,),
        in_specs=[pl.BlockSpec((tr, w), functools.partial(lambda cb, i: (i, cb), cb)) for _, w, cb in rows]
        + [pl.BlockSpec((1, w), functools.partial(lambda cb, i: (0, cb), cb)) for _, w, cb in params],
        out_specs=[pl.BlockSpec((tr, w), lambda i: (i, 0)) for w, _ in outs],
        out_shape=[jax.ShapeDtypeStruct((n_rows, w), dt) for w, dt in outs],
        compiler_params=_cparams("parallel"),
    )(*[r[0] for r in rows], *[p[0] for p in params])


def _rowwise_bwd(name, fn, rows, params, cots, row_grads, tr):
    rows = [_col(r) for r in rows]
    params = [_col(p) for p in params]
    cots = [_col(t) for t in cots]
    n_rows = rows[0][0].shape[0]
    nr, npar, nc = len(rows), len(params), len(cots)
    nin = nr + npar + nc
    n_rg = sum(dt is not None for dt in row_grads)

    def body(*refs):
        i = pl.program_id(0)
        prim = [r[...].astype(F32) for r in refs[:nr + npar]]
        ct = tuple(r[...].astype(F32) for r in refs[nr + npar:nin])
        _, vjp = jax.vjp(fn, *prim)
        g = vjp(ct if nc > 1 else ct[0])
        outs = refs[nin:]
        oi = 0
        for j, dt in enumerate(row_grads):
            if dt is not None:
                outs[oi][...] = g[j].astype(dt)
                oi += 1
        for j in range(npar):
            acc = outs[n_rg + j]

            @pl.when(i == 0)
            def _(acc=acc):
                acc[...] = jnp.zeros(acc.shape, F32)

            acc[...] += g[nr + j]

    res = pl.pallas_call(
        body, name=name, grid=(n_rows // tr,),
        in_specs=[pl.BlockSpec((tr, w), functools.partial(lambda cb, i: (i, cb), cb)) for _, w, cb in rows]
        + [pl.BlockSpec((1, w), functools.partial(lambda cb, i: (0, cb), cb)) for _, w, cb in params]
        + [pl.BlockSpec((tr, w), functools.partial(lambda cb, i: (i, cb), cb)) for _, w, cb in cots],
        out_specs=[pl.BlockSpec((tr, rows[j][1]), lambda i: (i, 0)) for j, dt in enumerate(row_grads) if dt is not None]
        + [pl.BlockSpec((1, w), lambda i: (0, 0)) for _, w, _ in params],
        out_shape=[jax.ShapeDtypeStruct((n_rows, rows[j][1]), dt) for j, dt in enumerate(row_grads) if dt is not None]
        + [jax.ShapeDtypeStruct((1, w), F32) for _, w, _ in params],
        compiler_params=_cparams("arbitrary"),
    )(*[r[0] for r in rows], *[p[0] for p in params], *[t[0] for t in cots])
    return res[:n_rg], res[n_rg:]


def _final(name, x, f, tgt, g, nw, tr):
    n_rows = x.shape[0]

    def body(x_ref, f_ref, t_ref, g_ref, nw_ref, dx_ref, df_ref, dg_ref, dnw_ref, loss_ref):
        i = pl.program_id(0)
        tg = t_ref[...]

        def fn(xv, fv, gv, nwv):
            e = _rms(xv + (0.5 * gv) * fv, nwv) - tg
            return 0.5 * jnp.mean(e * e, axis=-1, keepdims=True)

        per_row, vjp = jax.vjp(fn, x_ref[...], f_ref[...], g_ref[...], nw_ref[...])
        dx, df, dg, dnw = vjp(jnp.ones_like(per_row))
        dx_ref[...] = dx
        df_ref[...] = df.astype(df_ref.dtype)

        @pl.when(i == 0)
        def _():
            dg_ref[...] = jnp.zeros(dg_ref.shape, F32)
            dnw_ref[...] = jnp.zeros(dnw_ref.shape, F32)
            loss_ref[...] = jnp.zeros(loss_ref.shape, F32)

        dg_ref[...] += dg
        dnw_ref[...] += dnw
        loss_ref[...] += jnp.broadcast_to(jnp.sum(per_row, axis=0, keepdims=True), loss_ref.shape)

    row = pl.BlockSpec((tr, D), lambda i: (i, 0))
    par = pl.BlockSpec((1, D), lambda i: (0, 0))
    return pl.pallas_call(
        body, name=name, grid=(n_rows // tr,),
        in_specs=[row, row, row, par, par],
        out_specs=[row, row, par, par, pl.BlockSpec((8, 128), lambda i: (0, 0))],
        out_shape=[jax.ShapeDtypeStruct((n_rows, D), F32), jax.ShapeDtypeStruct((n_rows, D), BF16),
                   jax.ShapeDtypeStruct((1, D), F32), jax.ShapeDtypeStruct((1, D), F32),
                   jax.ShapeDtypeStruct((8, 128), F32)],
        compiler_params=_cparams("arbitrary"),
    )(x, f, tgt, g, nw)


def _tile(dim, pref):
    t = min(dim, pref)
    while dim % t:
        t //= 2
    return t


def _matmul(name, a, b, kind, col, out_dtype, add=None):
    if kind == "wgrad":
        m, kg = a.shape
        ng = b.shape[1]
        r, c = (kg, ng // NCHIP) if col else (kg // NCHIP, ng)
        tm, tn, tk = _tile(r, 1024), _tile(c, 1536), _tile(m, 2048)
        grid = (kg // tm, ng // tn, m // tk)
        a_spec = pl.BlockSpec((tk, tm), lambda i, j, k: (k, i))
        b_spec = pl.BlockSpec((tk, tn), lambda i, j, k: (k, j))
        if col:
            o_spec = pl.BlockSpec((None, tm, tn), lambda i, j, k: (j // (c // tn), i, j % (c // tn)))
        else:
            o_spec = pl.BlockSpec((None, tm, tn), lambda i, j, k: (i // (r // tm), i % (r // tm), j))
        out_shape = jax.ShapeDtypeStruct((NCHIP, r, c), F32)
        dims = TN
    else:
        _, r, c = b.shape
        m = a.shape[0]
        kg, ng = (r, NCHIP * c) if col else (NCHIP * r, c)
        tm = _tile(m, 1024)
        if kind == "fwd":
            tn, tk = _tile(c, 1536), _tile(r, 2048)
            grid = (m // tm, ng // tn, kg // tk)
            a_spec = pl.BlockSpec((tm, tk), lambda i, j, k: (i, k))
            if col:
                b_spec = pl.BlockSpec((None, tk, tn), lambda i, j, k: (j // (c // tn), k, j % (c // tn)))
            else:
                b_spec = pl.BlockSpec((None, tk, tn), lambda i, j, k: (k // (r // tk), k % (r // tk), j))
            out_shape = jax.ShapeDtypeStruct((m, ng), out_dtype)
            dims = NN
        else:
            tn, tk = _tile(r, 1024), _tile(c, 2048)
            grid = (m // tm, kg // tn, ng // tk)
            a_spec = pl.BlockSpec((tm, tk), lambda i, j, k: (i, k))
            if col:
                b_spec = pl.BlockSpec((None, tn, tk), lambda i, j, k: (k // (c // tk), j, k % (c // tk)))
            else:
                b_spec = pl.BlockSpec((None, tn, tk), lambda i, j, k: (j // (r // tn), j % (r // tn), k))
            out_shape = jax.ShapeDtypeStruct((m, kg), out_dtype)
            dims = NT
        o_spec = pl.BlockSpec((tm, tn), lambda i, j, k: (i, j))
    nk = grid[2]
    has_add = add is not None

    def body(*refs):
        a_ref, b_ref = refs[0], refs[1]
        o_ref = refs[3] if has_add else refs[2]
        k = pl.program_id(2)
        p = lax.dot_general(a_ref[...].astype(BF16), b_ref[...].astype(BF16), dims, preferred_element_type=F32)

        def finish(r_):
            if has_add:
                r_ = r_ + refs[2][...]
            o_ref[...] = r_.astype(o_ref.dtype)

        if nk == 1:
            finish(p)
        else:
            acc_ref = refs[-1]

            @pl.when(k == 0)
            def _():
                acc_ref[...] = p

            @pl.when(jnp.logical_and(k > 0, k < nk - 1))
            def _():
                acc_ref[...] += p

            @pl.when(k == nk - 1)
            def _():
                finish(acc_ref[...] + p)

    in_specs = [a_spec, b_spec] + ([pl.BlockSpec((tm, tn), lambda i, j, k: (i, j))] if has_add else [])
    args = (a, b) + ((add,) if has_add else ())
    return pl.pallas_call(
        body, name=name, grid=grid, in_specs=in_specs, out_specs=o_spec, out_shape=out_shape,
        scratch_shapes=[pltpu.VMEM((tm, tn), F32)] if nk > 1 else [],
        compiler_params=_cparams("parallel", "parallel", "arbitrary"),
    )(*args)


def _ffn_up(name, h, wg, wu):
    m, k = h.shape
    _, _, c = wg.shape
    n = NCHIP * c
    tm, tn = _tile(m, 1024), _tile(c, 1024)

    def body(h_ref, wg_ref, wu_ref, a_ref, b_ref, s_ref):
        hv = h_ref[...]
        a = jnp.dot(hv, wg_ref[...], preferred_element_type=F32)
        b = jnp.dot(hv, wu_ref[...], preferred_element_type=F32)
        a_ref[...] = a
        b_ref[...] = b
        s_ref[...] = (_silu(a) * b).astype(s_ref.dtype)

    w_spec = pl.BlockSpec((None, k, tn), lambda i, j: (j // (c // tn), 0, j % (c // tn)))
    o_spec = pl.BlockSpec((tm, tn), lambda i, j: (i, j))
    return pl.pallas_call(
        body, name=name, grid=(m // tm, n // tn),
        in_specs=[pl.BlockSpec((tm, k), lambda i, j: (i, 0)), w_spec, w_spec],
        out_specs=[o_spec, o_spec, o_spec],
        out_shape=[jax.ShapeDtypeStruct((m, n), F32), jax.ShapeDtypeStruct((m, n), F32),
                   jax.ShapeDtypeStruct((m, n), BF16)],
        compiler_params=_cparams("parallel", "parallel"),
    )(h, wg, wu)


def _dgrad_glu(name, df, wd, a, b):
    m = df.shape[0]
    _, r, c = wd.shape
    tm, tn = _tile(m, 512), _tile(r, 1024)

    def body(df_ref, w_ref, a_ref, b_ref, da_ref, db_ref):
        ds = lax.dot_general(df_ref[...], w_ref[...], NT, preferred_element_type=F32)
        _, vjp = jax.vjp(_f_glu, a_ref[...], b_ref[...])
        da, db = vjp(ds)
        da_ref[...] = da.astype(da_ref.dtype)
        db_ref[...] = db.astype(db_ref.dtype)

    t_spec = pl.BlockSpec((tm, tn), lambda i, j: (i, j))
    sds = jax.ShapeDtypeStruct((m, NCHIP * r), BF16)
    return pl.pallas_call(
        body, name=name, grid=(m // tm, NCHIP * r // tn),
        in_specs=[pl.BlockSpec((tm, c), lambda i, j: (i, 0)),
                  pl.BlockSpec((None, tn, c), lambda i, j: (j // (r // tn), j % (r // tn), 0)), t_spec, t_spec],
        out_specs=[t_spec, t_spec], out_shape=[sds, sds],
        compiler_params=_cparams("parallel", "parallel"),
    )(df, wd, a, b)


def _sgu_fwd(name, proj, b_in, lng, lnb, wm, bst, tr):
    n_rows = proj.shape[0]

    def body(uv_ref, buv_ref, g_ref, b_ref, wm_ref, bst_ref, ya_ref):
        u, vln = _f_mixa(uv_ref[...], buv_ref[...], g_ref[...], b_ref[...])
        vb = vln.astype(BF16)
        for ck in range(tr // CHUNK):
            rs = slice(ck * CHUNK, (ck + 1) * CHUNK)
            for h in range(HEADS):
                cs = slice(h * CHUNK, (h + 1) * CHUNK)
                vm = jnp.dot(wm_ref[h], vb[rs, cs], preferred_element_type=F32) + bst_ref[:, cs]
                ya_ref[rs, cs] = (u[rs, cs] * vm).astype(ya_ref.dtype)

    return pl.pallas_call(
        body, name=name, grid=(n_rows // tr,),
        in_specs=[pl.BlockSpec((tr, 2 * D), lambda i: (i, 0)), pl.BlockSpec((1, 2 * D), lambda i: (0, 0)),
                  pl.BlockSpec((1, D), lambda i: (0, 0)), pl.BlockSpec((1, D), lambda i: (0, 0)),
                  pl.BlockSpec((HEADS, CHUNK, CHUNK), lambda i: (0, 0, 0)),
                  pl.BlockSpec((CHUNK, D), lambda i: (0, 0))],
        out_specs=pl.BlockSpec((tr, D), lambda i: (i, 0)),
        out_shape=jax.ShapeDtypeStruct((n_rows, D), BF16),
        compiler_params=_cparams("parallel"),
    )(proj, b_in, lng, lnb, wm, bst)


def _sgu_bwd(name, proj, b_in, lng, lnb, wm, wmt, bst, dya, tr):
    n_rows = proj.shape[0]

    def body(uv_ref, buv_ref, g_ref, b_ref, wm_ref, wmt_ref, bst_ref, dya_ref,
             duv_ref, dbuv_ref, dg_ref, db_ref, dws_ref, dbs_ref, du_s, dvln_s):
        i = pl.program_id(0)

        @pl.when(i == 0)
        def _():
            dbuv_ref[...] = jnp.zeros(dbuv_ref.shape, F32)
            dg_ref[...] = jnp.zeros(dg_ref.shape, F32)
            db_ref[...] = jnp.zeros(db_ref.shape, F32)
            dws_ref[...] = jnp.zeros(dws_ref.shape, F32)
            dbs_ref[...] = jnp.zeros(dbs_ref.shape, F32)

        (u, vln), vjp = jax.vjp(_f_mixa, uv_ref[...], buv_ref[...], g_ref[...], b_ref[...])
        vb = vln.astype(BF16)
        dya_v = dya_ref[...].astype(F32)
        tpos = lax.broadcasted_iota(jnp.int32, (CHUNK, CHUNK), 0)
        spos = lax.broadcasted_iota(jnp.int32, (CHUNK, CHUNK), 1)
        causal = (tpos >= spos).astype(F32)
        for ck in range(tr // CHUNK):
            rs = slice(ck * CHUNK, (ck + 1) * CHUNK)
            for h in range(HEADS):
                cs = slice(h * CHUNK, (h + 1) * CHUNK)
                vbh = vb[rs, cs]
                vm = jnp.dot(wm_ref[h], vbh, preferred_element_type=F32) + bst_ref[:, cs]
                dyb = dya_v[rs, cs]
                du_s[rs, cs] = dyb * vm
                dvm = dyb * u[rs, cs]
                dvmb = dvm.astype(BF16)
                dvln_s[rs, cs] = jnp.dot(wmt_ref[h], dvmb, preferred_element_type=F32)
                dws_ref[h] += causal * lax.dot_general(dvmb, vbh, NT, preferred_element_type=F32)
                dbs_ref[:, cs] += jnp.broadcast_to(jnp.sum(dvm, axis=1, keepdims=True), (CHUNK, CHUNK))
        duv, dbuv, dg, db = vjp((du_s[...], dvln_s[...]))
        duv_ref[...] = duv.astype(duv_ref.dtype)
        dbuv_ref[...] += dbuv
        dg_ref[...] += dg
        db_ref[...] += db

    par = pl.BlockSpec((1, D), lambda i: (0, 0))
    par2 = pl.BlockSpec((1, 2 * D), lambda i: (0, 0))
    w_spec = pl.BlockSpec((HEADS, CHUNK, CHUNK), lambda i: (0, 0, 0))
    b_spec = pl.BlockSpec((CHUNK, D), lambda i: (0, 0))
    return pl.pallas_call(
        body, name=name, grid=(n_rows // tr,),
        in_specs=[pl.BlockSpec((tr, 2 * D), lambda i: (i, 0)), par2, par, par, w_spec, w_spec, b_spec,
                  pl.BlockSpec((tr, D), lambda i: (i, 0))],
        out_specs=[pl.BlockSpec((tr, 2 * D), lambda i: (i, 0)), par2, par, par, w_spec, b_spec],
        out_shape=[jax.ShapeDtypeStruct((n_rows, 2 * D), BF16), jax.ShapeDtypeStruct((1, 2 * D), F32),
                   jax.ShapeDtypeStruct((1, D), F32), jax.ShapeDtypeStruct((1, D), F32),
                   jax.ShapeDtypeStruct((HEADS, CHUNK, CHUNK), F32), jax.ShapeDtypeStruct((CHUNK, D), F32)],
        scratch_shapes=[pltpu.VMEM((tr, D), F32), pltpu.VMEM((tr, D), F32)],
        compiler_params=_cparams("arbitrary"),
    )(proj, b_in, lng, lnb, wm, wmt, bst, dya)


CT = 256
RB = 128
CV0 = 2 * D // CT
CG0 = 3 * D // CT


def _shifted(win, sh):
    if sh == 0:
        return win[0:RB]
    return pltpu.roll(win, RB + HALO - sh, 0)[0:RB]


def _conv_fwd(name, proj, b_in, w, cb):
    n_rows = proj.shape[0]

    def body(cv_ref, cg_ref, bcv_ref, bcg_ref, w_ref, cb_ref, z1_ref, zp_ref):
        zp_ref[0:HALO, :] = jnp.zeros((HALO, CT), F32)
        zp_ref[HALO:, :] = (cv_ref[...] + bcv_ref[...]) * jax.nn.sigmoid(cg_ref[...] + bcg_ref[...])

        def blk(rb, carry):
            base = pl.multiple_of(rb * RB, RB)
            win = zp_ref[pl.ds(base, RB + HALO), :]
            acc = jnp.broadcast_to(cb_ref[...], (RB, CT))
            for k in range(KCONV):
                acc = acc + w_ref[k:k + 1, :] * _shifted(win, k + 2)
            z1_ref[pl.ds(base, RB), :] = acc
            return carry

        lax.fori_loop(0, n_rows // RB, blk, 0)

    return pl.pallas_call(
        body, name=name, grid=(D // CT,),
        in_specs=[pl.BlockSpec((n_rows, CT), lambda j: (0, CV0 + j)), pl.BlockSpec((n_rows, CT), lambda j: (0, CG0 + j)),
                  pl.BlockSpec((1, CT), lambda j: (0, CV0 + j)), pl.BlockSpec((1, CT), lambda j: (0, CG0 + j)),
                  pl.BlockSpec((HALO, CT), lambda j: (0, j)), pl.BlockSpec((1, CT), lambda j: (0, j))],
        out_specs=pl.BlockSpec((n_rows, CT), lambda j: (0, j)),
        out_shape=jax.ShapeDtypeStruct((n_rows, D), F32),
        scratch_shapes=[pltpu.VMEM((n_rows + HALO, CT), F32)],
        compiler_params=_cparams("parallel"),
    )(proj, proj, b_in, b_in, w, cb)


def _conv_bwd(name, dz1, proj, b_in, w):
    n_rows = proj.shape[0]

    def body(dz_ref, cv_ref, cg_ref, bcv_ref, bcg_ref, w_ref, dcv_ref, dcg_ref, sm_ref, zp_ref, dzp_ref):
        cvb = cv_ref[...] + bcv_ref[...]
        sg = jax.nn.sigmoid(cg_ref[...] + bcg_ref[...])
        zp_ref[0:HALO, :] = jnp.zeros((HALO, CT), F32)
        zp_ref[HALO:, :] = cvb * sg
        dz = dz_ref[...]
        dzp_ref[0:n_rows, :] = dz
        dzp_ref[n_rows:, :] = jnp.zeros((HALO, CT), F32)
        sm_ref[...] = jnp.zeros(sm_ref.shape, F32)
        sm_ref[32:33, :] = jnp.sum(dz, axis=0, keepdims=True)

        def blk(rb, carry):
            base = pl.multiple_of(rb * RB, RB)
            dwin = dzp_ref[pl.ds(base, RB + HALO), :]
            zwin = zp_ref[pl.ds(base, RB + HALO), :]
            dzb = dwin[0:RB]
            acc = jnp.zeros((RB, CT), F32)
            for k in range(KCONV):
                acc = acc + w_ref[k:k + 1, :] * _shifted(dwin, KCONV - 1 - k)
                sm_ref[k:k + 1, :] += jnp.sum(dzb * _shifted(zwin, k + 2), axis=0, keepdims=True)
            dzp_ref[pl.ds(base, RB), :] = acc
            return carry

        lax.fori_loop(0, n_rows // RB, blk, 0)
        dz0 = dzp_ref[0:n_rows, :]
        dcv = dz0 * sg
        dcg = dz0 * cvb * (sg * (1.0 - sg))
        dcv_ref[...] = dcv.astype(dcv_ref.dtype)
        dcg_ref[...] = dcg.astype(dcg_ref.dtype)
        sm_ref[33:34, :] = jnp.sum(dcv, axis=0, keepdims=True)
        sm_ref[34:35, :] = jnp.sum(dcg, axis=0, keepdims=True)

    col = pl.BlockSpec((n_rows, CT), lambda j: (0, j))
    return pl.pallas_call(
        body, name=name, grid=(D // CT,),
        in_specs=[col, pl.BlockSpec((n_rows, CT), lambda j: (0, CV0 + j)), pl.BlockSpec((n_rows, CT), lambda j: (0, CG0 + j)),
                  pl.BlockSpec((1, CT), lambda j: (0, CV0 + j)), pl.BlockSpec((1, CT), lambda j: (0, CG0 + j)),
                  pl.BlockSpec((HALO, CT), lambda j: (0, j))],
        out_specs=[col, col, pl.BlockSpec((40, CT), lambda j: (0, j))],
        out_shape=[jax.ShapeDtypeStruct((n_rows, D), BF16), jax.ShapeDtypeStruct((n_rows, D), BF16),
                   jax.ShapeDtypeStruct((40, D), F32)],
        scratch_shapes=[pltpu.VMEM((n_rows + HALO, CT), F32), pltpu.VMEM((n_rows + HALO, CT), F32)],
        compiler_params=_cparams("parallel"),
    )(dz1, proj, proj, b_in, b_in, w)


ADA_TN = 768


def _split_bf16(v):
    hi = v.astype(BF16)
    return hi, (v - hi.astype(F32)).astype(BF16)


def _ada_fwd(name, c_all, w, b):
    n = w.shape[1]

    def body(c_ref, w_ref, b_ref, o_ref):
        ch, cl = _split_bf16(_silu(c_ref[...]))
        wh, wl = _split_bf16(w_ref[...])
        acc = jnp.dot(ch, wl, preferred_element_type=F32) + jnp.dot(cl, wh, preferred_element_type=F32)
        o_ref[...] = acc + jnp.dot(ch, wh, preferred_element_type=F32) + b_ref[...]

    return pl.pallas_call(
        body, name=name, grid=(n // ADA_TN,),
        in_specs=[pl.BlockSpec((8, D), lambda j: (0, 0)), pl.BlockSpec((D, ADA_TN), lambda j: (0, j)),
                  pl.BlockSpec((1, ADA_TN), lambda j: (0, j))],
        out_specs=pl.BlockSpec((8, ADA_TN), lambda j: (0, j)),
        out_shape=jax.ShapeDtypeStruct((8, n), F32),
        compiler_params=_cparams("parallel"),
    )(c_all, w, b)


def _ada_wgrad(name, c_all_t, dmod):
    n = dmod.shape[1]

    def body(ct_ref, dm_ref, o_ref):
        ca = _silu(ct_ref[...])
        acc = ca[:, 0:1] * dm_ref[0:1, :]
        for r in range(1, 8):
            acc = acc + ca[:, r:r + 1] * dm_ref[r:r + 1, :]
        o_ref[...] = acc

    return pl.pallas_call(
        body, name=name, grid=(n // ADA_TN,),
        in_specs=[pl.BlockSpec((D, 8), lambda j: (0, 0)), pl.BlockSpec((8, ADA_TN), lambda j: (0, j))],
        out_specs=pl.BlockSpec((D, ADA_TN), lambda j: (0, j)),
        out_shape=jax.ShapeDtypeStruct((D, n), F32),
        compiler_params=_cparams("parallel"),
    )(c_all_t, dmod)


def _adamw(name, w, g, m, v, tr):
    rows, cols = w.shape

    def body(w_ref, g_ref, m_ref, v_ref, d_ref, nm_ref, nv_ref):
        gv = g_ref[...]
        nm = ADAM_B1 * m_ref[...] + (1.0 - ADAM_B1) * gv
        nv = ADAM_B2 * v_ref[...] + (1.0 - ADAM_B2) * (gv * gv)
        m_hat = nm / (1.0 - ADAM_B1 ** ADAM_STEP)
        v_hat = nv / (1.0 - ADAM_B2 ** ADAM_STEP)
        d_ref[...] = -ADAM_LR * (m_hat / (jnp.sqrt(v_hat) + ADAM_EPS) + ADAM_WD * w_ref[...])
        nm_ref[...] = nm
        nv_ref[...] = nv

    spec = pl.BlockSpec((tr, cols), lambda i: (i, 0))
    sds = jax.ShapeDtypeStruct((rows, cols), F32)
    return pl.pallas_call(
        body, name=name, grid=(rows // tr,), in_specs=[spec] * 4, out_specs=[spec] * 3, out_shape=[sds] * 3,
        compiler_params=_cparams("parallel"),
    )(w, g, m, v)


def _sum8(name, stacked, tr):
    n, rows, cols = stacked.shape

    def body(s_ref, o_ref):
        acc = s_ref[0]
        for r in range(1, n):
            acc = acc + s_ref[r]
        o_ref[...] = acc

    return pl.pallas_call(
        body, name=name, grid=(rows // tr,),
        in_specs=[pl.BlockSpec((n, tr, cols), lambda i: (0, i, 0))],
        out_specs=pl.BlockSpec((tr, cols), lambda i: (i, 0)),
        out_shape=jax.ShapeDtypeStruct((rows, cols), F32),
        compiler_params=_cparams("parallel"),
    )(stacked)


def _pair_add(name, g5, other, c_idx, tr):
    nq, _, rows, cols = g5.shape

    def body(c_ref, g_ref, o_ref, p_ref):
        p_ref[...] = (g_ref[...] + o_ref[...]).astype(p_ref.dtype)

    return pl.pallas_call(
        body, name=name,
        grid_spec=pltpu.PrefetchScalarGridSpec(
            num_scalar_prefetch=1, grid=(nq, rows // tr),
            in_specs=[pl.BlockSpec((None, None, tr, cols), lambda qi, i, cr: (qi, cr[0], i, 0)),
                      pl.BlockSpec((None, tr, cols), lambda qi, i, cr: (qi, i, 0))],
            out_specs=pl.BlockSpec((None, tr, cols), lambda qi, i, cr: (qi, i, 0))),
        out_shape=jax.ShapeDtypeStruct((nq, rows, cols), BF16),
        compiler_params=_cparams("parallel", "parallel"),
    )(c_idx, g5, other)


def _chip_add(name, p, recv, qc_idx, tr):
    _, rows, cols = p.shape

    def body(qc_ref, p_ref, r_ref, o_ref):
        acc = p_ref[...].astype(F32)
        for k in range(NCHIP - 1):
            acc = acc + r_ref[k].astype(F32)
        o_ref[...] = acc

    return pl.pallas_call(
        body, name=name,
        grid_spec=pltpu.PrefetchScalarGridSpec(
            num_scalar_prefetch=1, grid=(rows // tr,),
            in_specs=[pl.BlockSpec((None, tr, cols), lambda i, qc: (qc[0], i, 0)),
                      pl.BlockSpec((NCHIP - 1, tr, cols), lambda i, qc: (0, i, 0))],
            out_specs=pl.BlockSpec((None, tr, cols), lambda i, qc: (qc[1], i, 0))),
        out_shape=jax.ShapeDtypeStruct((2, rows, cols), F32),
        compiler_params=_cparams("parallel"),
    )(qc_idx, p, recv)


def _allgather8(name, blk):
    m_per, n = blk.shape

    def body(x_ref, out_ref, send_sems, recv_sems, local_sem):
        x, y, c = lax.axis_index("x"), lax.axis_index("y"), lax.axis_index("c")
        me, sibling = (x, y, c), (x, y, 1 - c)
        chips = [(1 - x, y), (x, 1 - y), (1 - x, 1 - y)]

        def rows(px, py, pc):
            return out_ref.at[pl.ds((4 * px + 2 * py + pc) * m_per, m_per), :]

        def copy(k, block, to, src=None):
            return pltpu.make_async_remote_copy(
                src_ref=rows(*block) if src is None else src, dst_ref=rows(*block),
                send_sem=send_sems.at[k], recv_sem=recv_sems.at[k], device_id=to, device_id_type=MESH)

        mine = pltpu.make_async_copy(x_ref, rows(*me), local_sem)
        mine.start()
        first = [copy(0, me, sibling, src=x_ref)]
        first += [copy(1 + j, me, (*chip, c), src=x_ref) for j, chip in enumerate(chips)]
        for cp in first:
            cp.start()
        passed = [copy(4 + j, (*chip, c), sibling) for j, chip in enumerate(chips)]
        for j, chip in enumerate(chips):
            copy(1 + j, (*chip, c), me).wait_recv()
            passed[j].start()
        copy(0, sibling, me).wait_recv()
        for j, chip in enumerate(chips):
            copy(4 + j, (*chip, 1 - c), me).wait_recv()
        for cp in first + passed:
            cp.wait_send()
        mine.wait()

    return pl.pallas_call(
        body, name=name,
        out_shape=jax.ShapeDtypeStruct((8 * m_per, n), blk.dtype),
        in_specs=[pl.BlockSpec(memory_space=pltpu.VMEM)],
        out_specs=pl.BlockSpec(memory_space=pltpu.VMEM),
        scratch_shapes=[pltpu.SemaphoreType.DMA((7,)), pltpu.SemaphoreType.DMA((7,)), pltpu.SemaphoreType.DMA],
        compiler_params=pltpu.CompilerParams(vmem_limit_bytes=VMEM_LIMIT),
    )(blk)


def _exchange(name, srcs, out_shapes, n_local, n_remote, plan, aliases=None):
    ni, no = len(srcs), len(out_shapes)

    def body(*refs):
        ins, outs = refs[:ni], refs[ni:ni + no]
        send_sems, recv_sems, local_sems = refs[ni + no:]
        x, y, c = lax.axis_index("x"), lax.axis_index("y"), lax.axis_index("c")
        local, remote = plan(ins, outs, x, y, c)
        assert len(local) == n_local and len(remote) == n_remote

        def rcopy(i, dst):
            s, _, peer, _, _ = remote[i]
            return pltpu.make_async_remote_copy(src_ref=s, dst_ref=dst, send_sem=send_sems.at[i],
                                                recv_sem=recv_sems.at[i], device_id=peer, device_id_type=MESH)

        lcs = [pltpu.make_async_copy(s, d, local_sems.at[i]) for i, (s, d) in enumerate(local)]
        for cp in lcs:
            cp.start()
        first = [i for i in range(n_remote) if remote[i][4] is None]
        passed = [i for i in range(n_remote) if remote[i][4] is not None]
        for i in first:
            rcopy(i, remote[i][1]).start()
        arrived = set()
        for i in passed:
            j = remote[i][4]
            rcopy(j, remote[j][3]).wait_recv()
            arrived.add(j)
            rcopy(i, remote[i][1]).start()
        for i in range(n_remote):
            if i not in arrived:
                rcopy(i, remote[i][3]).wait_recv()
        for i in range(n_remote):
            rcopy(i, remote[i][1]).wait_send()
        for cp in lcs:
            cp.wait()

    any_spec = pl.BlockSpec(memory_space=pl.ANY)
    return pl.pallas_call(
        body, name=name, out_shape=out_shapes,
        in_specs=[any_spec] * ni, out_specs=[any_spec] * no,
        input_output_aliases=aliases or {},
        scratch_shapes=[pltpu.SemaphoreType.DMA((n_remote,)), pltpu.SemaphoreType.DMA((n_remote,)),
                        pltpu.SemaphoreType.DMA((max(n_local, 1),))],
    )(*srcs)


_CHIP_FLIPS = ((0, 1), (1, 0), (1, 1))


def _flip(v, f):
    return 1 - v if f else v


EFFECT = pltpu.SideEffectType.DATAFLOW_SIDE_EFFECTING


def _split_start(name, bufs, n, plan, after=None):
    nb = len(bufs)
    extra = [] if after is None else [after]

    def body(*refs):
        ins = refs[:nb]
        send_sems, recv_sems = refs[nb + len(extra)], refs[nb + len(extra) + 1]
        token = refs[-1]
        x, y, c = lax.axis_index("x"), lax.axis_index("y"), lax.axis_index("c")
        copies = plan(ins, x, y, c)
        assert len(copies) == n
        for i, (s, d, peer, _) in enumerate(copies):
            pltpu.make_async_remote_copy(src_ref=s, dst_ref=d, send_sem=send_sems.at[i], recv_sem=recv_sems.at[i],
                                         device_id=peer, device_id_type=MESH).start()
        token[...] = jnp.zeros_like(token)

    hbm = pl.BlockSpec(memory_space=pltpu.HBM)
    sem = pl.BlockSpec(memory_space=pltpu.SEMAPHORE)
    res = pl.pallas_call(
        body, name=name,
        out_shape=(pltpu.SemaphoreType.DMA((n,)), pltpu.SemaphoreType.DMA((n,)),
                   *[pltpu.HBM(b.shape, b.dtype) for b in bufs], jax.ShapeDtypeStruct((8, 128), F32)),
        in_specs=[hbm] * nb + [pl.BlockSpec(memory_space=pl.ANY)] * len(extra),
        out_specs=(sem, sem, *[hbm] * nb, pl.BlockSpec(memory_space=pltpu.VMEM)),
        input_output_aliases={i: 2 + i for i in range(nb)},
        compiler_params=pltpu.CompilerParams(has_side_effects=EFFECT),
    )(*[pltpu.with_memory_space_constraint(b, pltpu.HBM) for b in bufs], *extra)
    return res[0], res[1], list(res[2:2 + nb]), res[-1]


def _split_wait(name, bufs, send_sems, recv_sems, after, n, plan):
    nb = len(bufs)

    def body(*refs):
        ins = refs[:nb]
        ssem, rsem = refs[nb], refs[nb + 1]
        x, y, c = lax.axis_index("x"), lax.axis_index("y"), lax.axis_index("c")
        copies = plan(ins, x, y, c)
        assert len(copies) == n
        for i, (s, _, peer, lands) in enumerate(copies):
            cp = pltpu.make_async_remote_copy(src_ref=s, dst_ref=lands, send_sem=ssem.at[i], recv_sem=rsem.at[i],
                                              device_id=peer, device_id_type=MESH)
            cp.wait_send()
            cp.wait_recv()

    hbm = pl.BlockSpec(memory_space=pltpu.HBM)
    sem = pl.BlockSpec(memory_space=pltpu.SEMAPHORE)
    res = pl.pallas_call(
        body, name=name,
        out_shape=[pltpu.HBM(b.shape, b.dtype) for b in bufs],
        in_specs=[hbm] * nb + [sem, sem, pl.BlockSpec(memory_space=pl.ANY)],
        out_specs=[hbm] * nb,
        input_output_aliases={i: i for i in range(nb)},
        compiler_params=pltpu.CompilerParams(has_side_effects=EFFECT),
    )(*bufs, send_sems, recv_sems, after)
    return list(res)


def _chips_of(x, y):
    return [(_flip(x, fx), _flip(y, fy)) for fx, fy in _CHIP_FLIPS]


def _gather_start(tag, stacks, after):
    n = len(stacks)

    def plan(refs, x, y, c):
        q = 2 * x + y
        return [(refs[a].at[q, c], refs[a].at[q, c], (px, py, c), refs[a].at[2 * px + py, c])
                for a in range(n) for px, py in _chips_of(x, y)]

    send, recv, thru, token = _split_start(f"gather_start_{tag}", stacks, 3 * n, plan, after)
    return (send, recv, thru, plan), token


def _gather_finish(tag, state, after):
    send, recv, thru, plan = state
    n = len(thru)
    landed = _split_wait(f"gather_wait_{tag}", thru, send, recv, after, 3 * n, plan)

    def pass_on(ins, outs, x, y, c):
        sib = (x, y, 1 - c)
        remote = []
        for a in range(n):
            for px, py in _chips_of(x, y):
                p = 2 * px + py
                remote.append((ins[a].at[p, c], outs[a].at[p, c], sib, outs[a].at[p, 1 - c], None))
        return [], remote

    full = _exchange(f"gather_pass_{tag}", landed, [jax.ShapeDtypeStruct(s.shape, s.dtype) for s in landed],
                     0, 3 * n, pass_on, aliases={a: a for a in range(n)})
    return [g.reshape(NCHIP, 2 * g.shape[2], g.shape[3]) for g in full]


def _gather_small(name, smalls):
    n = len(smalls)

    def plan(ins, outs, x, y, c):
        q = 2 * x + y
        local = [(ins[b], outs[b].at[q]) for b in range(n)]
        remote = [(ins[b], outs[b].at[q], (px, py, c), outs[b].at[2 * px + py], None)
                  for b in range(n) for px, py in _chips_of(x, y)]
        return local, remote

    return _exchange(name, smalls, [jax.ShapeDtypeStruct((NCHIP,) + s.shape, s.dtype) for s in smalls], n, 3 * n, plan)


def _rs_start(tag, grads, c_idx):
    n = len(grads)
    g5 = [g.reshape(NCHIP, 2, g.shape[1] // 2, g.shape[2]) for g in grads]
    half = [(g.shape[2], g.shape[3]) for g in g5]

    def swap(ins, outs, x, y, c):
        sib = (x, y, 1 - c)
        return [], [(ins[a].at[:, 1 - c], outs[a], sib, outs[a], None) for a in range(n)]

    got = _exchange(f"rs_pair_{tag}", g5, [jax.ShapeDtypeStruct((NCHIP,) + h, F32) for h in half], 0, n, swap)
    part = [_pair_add(f"rs_pair_add_{tag}{a}", g5[a], got[a], c_idx, _tile(half[a][0], 512)) for a in range(n)]
    land = [lax.empty((NCHIP - 1,) + h, BF16) for h in half]

    def plan(refs, x, y, c):
        return [(refs[a].at[2 * px + py], refs[n + a].at[k], (px, py, c), refs[n + a].at[k])
                for a in range(n) for k, (px, py) in enumerate(_chips_of(x, y))]

    send, recv, thru, token = _split_start(f"rs_chips_start_{tag}", part + land, 3 * n, plan)
    return (send, recv, thru, plan, half), token


def _rs_finish(tag, state, after, qc_idx):
    send, recv, thru, plan, half = state
    n = len(half)
    landed = _split_wait(f"rs_chips_wait_{tag}", thru, send, recv, after, 3 * n, plan)
    red = [_chip_add(f"rs_chip_add_{tag}{a}", landed[a], landed[n + a], qc_idx, _tile(half[a][0], 512)) for a in range(n)]

    def share(ins, outs, x, y, c):
        sib = (x, y, 1 - c)
        return [], [(ins[a].at[c], outs[a].at[c], sib, outs[a].at[1 - c], None) for a in range(n)]

    full = _exchange(f"rs_share_{tag}", red, [jax.ShapeDtypeStruct((2,) + h, F32) for h in half], 0, n, share,
                     aliases={a: a for a in range(n)})
    return [f.reshape(2 * h[0], h[1]) for f, h in zip(full, half)]


def _pack(name, n_rows, pieces):
    arrays = [p[1] for p in pieces if p[0] != "zeros"]

    def body(*refs):
        o_ref = refs[-1]
        o_ref[...] = jnp.zeros(o_ref.shape, F32)
        row, ai = 0, 0
        for p in pieces:
            kind = p[0]
            if kind == "zeros":
                row += p[1]
                continue
            ref = refs[ai]
            ai += 1
            if kind == "rows":
                r = ref.shape[0]
                o_ref[row:row + r, :] = ref[...]
            elif kind == "wide":
                r = ref.shape[1] // D
                for j in range(r):
                    o_ref[row + j:row + j + 1, :] = ref[:, j * D:(j + 1) * D]
            elif kind == "slice":
                r = p[3] - p[2]
                o_ref[row:row + r, :] = ref[p[2]:p[3], :]
            elif kind == "heads":
                r = CHUNK
                for h in range(HEADS):
                    o_ref[row:row + r, h * CHUNK:(h + 1) * CHUNK] = ref[h]
            else:
                r = 1
                o_ref[row:row + 1, 0:128] = ref[0:1, :]
            row += r
        assert row == n_rows, (row, n_rows)

    vmem = pl.BlockSpec(memory_space=pltpu.VMEM)
    return pl.pallas_call(
        body, name=name, out_shape=jax.ShapeDtypeStruct((n_rows, D), F32),
        in_specs=[vmem] * len(arrays), out_specs=vmem,
        compiler_params=pltpu.CompilerParams(vmem_limit_bytes=VMEM_LIMIT),
    )(*arrays)


def _to_bf16_stack(name, w, q_idx):
    r, cols = w.shape
    half = r // 2
    tr = _tile(half, 256)
    nb = half // tr

    def body(q_ref, w_ref, o_ref):
        o_ref[...] = w_ref[...].astype(BF16)

    return pl.pallas_call(
        body, name=name,
        grid_spec=pltpu.PrefetchScalarGridSpec(
            num_scalar_prefetch=1, grid=(2 * nb,),
            in_specs=[pl.BlockSpec((tr, cols), lambda i, qr: (i, 0))],
            out_specs=pl.BlockSpec((None, None, tr, cols), lambda i, qr: (qr[0], i // nb, i % nb, 0))),
        out_shape=jax.ShapeDtypeStruct((NCHIP, 2, half, cols), BF16),
        compiler_params=_cparams("parallel"),
    )(q_idx, w)


def kernel(x, c, ada_w, ada_b, norm_ffn1, ffn1_w_gate, ffn1_w_up, ffn1_w_down, norm_mix, mix_w_in, mix_b_in, sgu_ln_g, sgu_ln_b, sgu_w_s, sgu_b_s, conv_w, conv_b, conv_ln_g, conv_ln_b, w_branch_a, w_branch_b, w_out, norm_ffn2, ffn2_w_gate, ffn2_w_up, ffn2_w_down, norm_final, loss_target, m_ada_w, m_ada_b, m_norm_ffn1, m_ffn1_w_gate, m_ffn1_w_up, m_ffn1_w_down, m_norm_mix, m_mix_w_in, m_mix_b_in, m_sgu_ln_g, m_sgu_ln_b, m_sgu_w_s, m_sgu_b_s, m_conv_w, m_conv_b, m_conv_ln_g, m_conv_ln_b, m_w_branch_a, m_w_branch_b, m_w_out, m_norm_ffn2, m_ffn2_w_gate, m_ffn2_w_up, m_ffn2_w_down, m_norm_final, v_ada_w, v_ada_b, v_norm_ffn1, v_ffn1_w_gate, v_ffn1_w_up, v_ffn1_w_down, v_norm_mix, v_mix_w_in, v_mix_b_in, v_sgu_ln_g, v_sgu_ln_b, v_sgu_w_s, v_sgu_b_s, v_conv_w, v_conv_b, v_conv_ln_g, v_conv_ln_b, v_w_branch_a, v_w_branch_b, v_w_out, v_norm_ffn2, v_ffn2_w_gate, v_ffn2_w_up, v_ffn2_w_down, v_norm_final):
    xi, yi, ci = lax.axis_index("x"), lax.axis_index("y"), lax.axis_index("c")
    q = 2 * xi + yi
    dev = 4 * xi + 2 * yi + ci
    c_idx = jnp.reshape(ci, (1,)).astype(jnp.int32)
    q_idx = jnp.reshape(q, (1,)).astype(jnp.int32)
    x0 = x[0]
    tgt = loss_target[0]
    qc_idx = jnp.stack([q, ci]).astype(jnp.int32)
    n_ada = ada_w.shape[2]

    c_all = _allgather8("ag_c", jnp.pad(c, ((0, 7), (0, 0))))[::8]
    ada_b_mine = lax.dynamic_slice(ada_b, (0, q * n_ada), (1, n_ada))
    mod_part = _ada_fwd("ada_fwd", c_all, ada_w[0], ada_b_mine)
    small_parts = _gather_small("gather_small", [mod_part, jnp.pad(conv_w[0], ((0, 1), (0, 0)))])

    big = dict(g1=ffn1_w_gate, u1=ffn1_w_up, d1=ffn1_w_down, win=mix_w_in, wa=w_branch_a, wb=w_branch_b, wo=w_out,
               g2=ffn2_w_gate, u2=ffn2_w_up, d2=ffn2_w_down)
    groups = dict(f1=("d1", "g1", "u1"), mx=("wo", "wa", "wb", "win"), f2=("d2", "g2", "u2"))
    gather_groups = dict(up1=("g1", "u1"), dn1=("d1",), mx=("win", "wa", "wb", "wo"), f2=("g2", "u2", "d2"))
    gather_state = {}
    started = jnp.zeros((1, 1), F32)
    for tag, keys in gather_groups.items():
        gather_state[tag], token = _gather_start(
            tag, [_to_bf16_stack(f"cast_{k}", big[k][0], q_idx) for k in keys], small_parts[0])
        started = started + token[0:1, 0:1]

    mod_all = jnp.transpose(small_parts[0], (1, 0, 2)).reshape(8, N_MOD * D)
    mod = lax.dynamic_slice(mod_all, (dev, 0), (1, N_MOD * D)) + started
    sh1, sc1, g1, sh2, sc2, g2, sh3, sc3, g3 = [mod[:, k * D:(k + 1) * D] for k in range(N_MOD)]
    convw = jnp.transpose(small_parts[1], (1, 0, 2)).reshape(HALO, D)
    W = dict(zip(gather_groups["up1"], _gather_finish("up1", gather_state["up1"], mod)))

    causal = jnp.tril(jnp.ones((CHUNK, CHUNK), bool))
    wm_f = jnp.where(causal[None], sgu_w_s[0], 0.0)
    wm = wm_f.astype(BF16)
    wmt = jnp.swapaxes(wm_f, 1, 2).astype(BF16)
    bst = jnp.repeat(jnp.transpose(sgu_b_s[0]), CHUNK, axis=1)
    b_in = mix_b_in

    (h1,) = _rowwise_fwd("pre1", _f_pre, [x0], [norm_ffn1, sc1, sh1], [(D, BF16)], 256)
    a1, b1, s1 = _ffn_up("up1", h1, W["g1"], W["u1"])
    W.update(zip(gather_groups["dn1"], _gather_finish("dn1", gather_state["dn1"], s1)))
    f1 =_matmul("down1", s1, W["d1"], "fwd", False, F32)
    x1, h2 = _rowwise_fwd("res1", _f_res_pre(0.5), [x0, f1], [g1, norm_mix, sc2, sh2], [(D, F32), (D, BF16)], 256)
    W.update(zip(gather_groups["mx"], _gather_finish("mx", gather_state["mx"], x1)))
    proj =_matmul("in_proj", h2, W["win"], "fwd", True, F32)
    ya = _sgu_fwd("sgu", proj, b_in, sgu_ln_g, sgu_ln_b, wm, bst, 256)
    z1 = _conv_fwd("conv", proj, b_in, convw, conv_b)
    (z3,) = _rowwise_fwd("conv_ln", _f_lnsilu, [z1], [conv_ln_g, conv_ln_b], [(D, BF16)], 256)
    y_a = _matmul("branch_a", ya, W["wa"], "fwd", False, F32)
    y_b = _matmul("branch_b", z3, W["wb"], "fwd", False, F32)
    (merged,) = _rowwise_fwd("merge", _f_merge, [(proj, 2 * D, 2), y_a, y_b], [(b_in, 2 * D, 2)], [(D, BF16)], 256)
    y = _matmul("out_proj", merged, W["wo"], "fwd", False, F32)
    x2, h3 = _rowwise_fwd("res2", _f_res_pre(1.0), [x1, y], [g2, norm_ffn2, sc3, sh3], [(D, F32), (D, BF16)], 256)
    W.update(zip(gather_groups["f2"], _gather_finish("f2", gather_state["f2"], x2)))
    a3, b3, s3 = _ffn_up("up2", h3, W["g2"], W["u2"])
    f3 = _matmul("down2", s3, W["d2"], "fwd", False, F32)
    dx2, df3, dg3, dnwf, loss_blk = _final("final", x2, f3, tgt, g3, norm_final.reshape(1, D), 256)

    G = {}

    def ffn_bwd(tag, df, s, a, b, h, wg, wu, wd):
        da, db = _dgrad_glu(f"d_down{tag}", df, wd, a, b)
        G["d" + tag] = _matmul(f"w_down{tag}", s, df, "wgrad", False, F32)
        dh = _matmul(f"d_gate{tag}", da, wg, "dgrad", True, F32)
        dh = _matmul(f"d_up{tag}", db, wu, "dgrad", True, F32, add=dh)
        G["g" + tag] = _matmul(f"w_gate{tag}", h, da, "wgrad", True, F32)
        G["u" + tag] = _matmul(f"w_up{tag}", h, db, "wgrad", True, F32)
        return dh

    reduced = {}
    dh3 = ffn_bwd("2", df3, s3, a3, b3, h3, W["g2"], W["u2"], W["d2"])
    rs_f2, token = _rs_start("f2", [G[k] for k in groups["f2"]], c_idx)
    (dx1, dy), (dg2, dnw3, dsc3, dsh3) = _rowwise_bwd(
        "res2_b", _f_res_pre(1.0), [x1, y], [g2 + token[0:1, 0:1], norm_ffn2, sc3, sh3], [dx2, dh3], [F32, BF16], 256)
    dmerged = _matmul("d_out", dy, W["wo"], "dgrad", False, F32)
    G["wo"] = _matmul("w_out", merged, dy, "wgrad", False, F32)
    (dgab, dy_a, dy_b), (dbgab,) = _rowwise_bwd(
        "merge_b", _f_merge, [(proj, 2 * D, 2), y_a, y_b], [(b_in, 2 * D, 2)], [dmerged], [BF16, BF16, BF16], 256)
    dya = _matmul("d_a", dy_a, W["wa"], "dgrad", False, F32)
    dz3 = _matmul("d_b", dy_b, W["wb"], "dgrad", False, F32)
    G["wa"] = _matmul("w_a", ya, dy_a, "wgrad", False, F32)
    G["wb"] = _matmul("w_b", z3, dy_b, "wgrad", False, F32)
    (dz1,), (dclg, dclb) = _rowwise_bwd("conv_ln_b", _f_lnsilu, [z1], [conv_ln_g, conv_ln_b], [dz3], [F32], 256)
    dcv, dcg, conv_small = _conv_bwd("conv_b", dz1, proj, b_in, convw)
    duv, dbuv, dlng, dlnb, dws, dbs = _sgu_bwd("sgu_b", proj, b_in, sgu_ln_g, sgu_ln_b, wm, wmt, bst, dya, 256)
    dproj = jnp.concatenate([duv, dcv, dcg, dgab], axis=1)
    dh2 = _matmul("d_in", dproj, W["win"], "dgrad", True, F32)
    G["win"] = _matmul("w_in", h2, dproj, "wgrad", True, F32)
    reduced.update(zip(groups["f2"], _rs_finish("f2", rs_f2, dh2, qc_idx)))
    rs_mx, token = _rs_start("mx", [G[k] for k in groups["mx"]], c_idx)
    (dx0, df1), (dg1, dnw2, dsc2, dsh2) = _rowwise_bwd(
        "res1_b", _f_res_pre(0.5), [x0, f1], [g1 + token[0:1, 0:1], norm_mix, sc2, sh2], [dx1, dh2], [F32, BF16], 256)
    dh1 = ffn_bwd("1", df1, s1, a1, b1, h1, W["g1"], W["u1"], W["d1"])
    reduced.update(zip(groups["mx"], _rs_finish("mx", rs_mx, dh1, qc_idx)))
    rs_f1, token = _rs_start("f1", [G[k] for k in groups["f1"]], c_idx)
    (grad_x,), (dnw1, dsc1, dsh1) = _rowwise_bwd(
        "pre1_b", _f_pre_keep, [x0], [norm_ffn1, sc1 + token[0:1, 0:1], sh1], [dx0, dh1], [F32], 256)

    dbs_row = jnp.transpose(dbs[:, ::CHUNK]).reshape(1, D)
    small = _pack("pack_small", 192, [
        ("rows", dnw1), ("rows", dnw2), ("wide", dbuv), ("slice", conv_small, 33, 34), ("slice", conv_small, 34, 35),
        ("wide", dbgab), ("rows", dlng), ("rows", dlnb), ("rows", dbs_row), ("slice", conv_small, 32, 33),
        ("rows", dclg), ("rows", dclb), ("rows", dnw3), ("rows", dnwf),
        ("rows", dsh1), ("rows", dsc1), ("rows", dg1), ("rows", dsh2), ("rows", dsc2), ("rows", dg2),
        ("rows", dsh3), ("rows", dsc3), ("rows", dg3), ("lanes", loss_blk), ("zeros", 6),
        ("slice", conv_small, 0, HALO), ("heads", dws)])
    n_small = small.shape[0]
    every = _allgather8("ag_small", small).reshape(8, n_small, D)
    tot = _sum8("sum_small", every, 64)
    loss = tot[25, 0]
    dmod_all = every[:, 16:16 + N_MOD, :].reshape(8, N_MOD * D)
    dmod_mine = lax.dynamic_slice(dmod_all, (0, q * n_ada), (8, n_ada))
    g_ada_w = _ada_wgrad("ada_wgrad", jnp.transpose(c_all), dmod_mine)

    given = dict(ada_w=(ada_w, m_ada_w, v_ada_w), ada_b=(ada_b, m_ada_b, v_ada_b),
                 norm_ffn1=(norm_ffn1, m_norm_ffn1, v_norm_ffn1),
                 ffn1_w_gate=(ffn1_w_gate, m_ffn1_w_gate, v_ffn1_w_gate), ffn1_w_up=(ffn1_w_up, m_ffn1_w_up, v_ffn1_w_up),
                 ffn1_w_down=(ffn1_w_down, m_ffn1_w_down, v_ffn1_w_down), norm_mix=(norm_mix, m_norm_mix, v_norm_mix),
                 mix_w_in=(mix_w_in, m_mix_w_in, v_mix_w_in), mix_b_in=(mix_b_in, m_mix_b_in, v_mix_b_in),
                 sgu_ln_g=(sgu_ln_g, m_sgu_ln_g, v_sgu_ln_g), sgu_ln_b=(sgu_ln_b, m_sgu_ln_b, v_sgu_ln_b),
                 sgu_w_s=(sgu_w_s, m_sgu_w_s, v_sgu_w_s), sgu_b_s=(sgu_b_s, m_sgu_b_s, v_sgu_b_s),
                 conv_w=(conv_w, m_conv_w, v_conv_w), conv_b=(conv_b, m_conv_b, v_conv_b),
                 conv_ln_g=(conv_ln_g, m_conv_ln_g, v_conv_ln_g), conv_ln_b=(conv_ln_b, m_conv_ln_b, v_conv_ln_b),
                 w_branch_a=(w_branch_a, m_w_branch_a, v_w_branch_a), w_branch_b=(w_branch_b, m_w_branch_b, v_w_branch_b),
                 w_out=(w_out, m_w_out, v_w_out), norm_ffn2=(norm_ffn2, m_norm_ffn2, v_norm_ffn2),
                 ffn2_w_gate=(ffn2_w_gate, m_ffn2_w_gate, v_ffn2_w_gate), ffn2_w_up=(ffn2_w_up, m_ffn2_w_up, v_ffn2_w_up),
                 ffn2_w_down=(ffn2_w_down, m_ffn2_w_down, v_ffn2_w_down), norm_final=(norm_final, m_norm_final, v_norm_final))
    out = {}

    def big_update(name, g2d):
        w, m, v = given[name]
        shp = w.shape
        d_, nm, nv = _adamw(f"adamw_{name}", w.reshape(g2d.shape), g2d, m.reshape(g2d.shape), v.reshape(g2d.shape),
                            _tile(g2d.shape[0], 512 if g2d.shape[1] <= D else 256))
        out[name] = tuple(t.reshape(shp) for t in (g2d, d_, nm, nv))

    big_names = dict(g1="ffn1_w_gate", u1="ffn1_w_up", d1="ffn1_w_down", win="mix_w_in", wa="w_branch_a",
                     wb="w_branch_b", wo="w_out", g2="ffn2_w_gate", u2="ffn2_w_up", d2="ffn2_w_down")
    for key in groups["f2"] + groups["mx"]:
        big_update(big_names[key], reduced[key])
    big_update("ada_w", g_ada_w)

    small_rows = [("norm_ffn1", 0, 1), ("norm_mix", 1, 1), ("mix_b_in", 2, 6), ("sgu_ln_g", 8, 1), ("sgu_ln_b", 9, 1),
                  ("sgu_b_s", 10, 1), ("conv_b", 11, 1), ("conv_ln_g", 12, 1), ("conv_ln_b", 13, 1),
                  ("norm_ffn2", 14, 1), ("norm_final", 15, 1), ("ada_b", 16, N_MOD)]

    def pack_rows(which, label):
        pieces = []
        for nm, _, r in small_rows:
            t = given[nm][which]
            pieces.append(("wide", t) if r > 1 else ("rows", t.reshape(1, D)))
        return _pack(f"pack_{label}", HALO, pieces + [("zeros", HALO - 16 - N_MOD)])

    sd, sm, sv = _adamw("adamw_small", pack_rows(0, "w"), tot, pack_rows(1, "m"), pack_rows(2, "v"), HALO)
    for nm, r0, r in small_rows:
        shp = given[nm][0].shape
        out[nm] = tuple(t[r0:r0 + r].reshape(shp) for t in (tot, sd, sm, sv))
    n_cw = conv_w.shape[2]
    g_cw = lax.dynamic_slice(tot, (HALO, q * n_cw), (HALO, n_cw))
    pad_cw = lambda t: jnp.pad(t[0], ((0, 1), (0, 0)))
    cd, cm, cv_ = _adamw("adamw_conv_w", pad_cw(conv_w), g_cw, pad_cw(m_conv_w), pad_cw(v_conv_w), HALO)
    out["conv_w"] = tuple(t[:KCONV][None] for t in (g_cw, cd, cm, cv_))
    g_ws = jnp.transpose(tot[2 * HALO:].reshape(CHUNK, HEADS, CHUNK), (1, 0, 2)).reshape(HEADS * CHUNK, CHUNK)
    flat_ws = lambda t: t.reshape(HEADS * CHUNK, CHUNK)
    wd_, wm_, wv_ = _adamw("adamw_sgu_w_s", flat_ws(sgu_w_s), g_ws, flat_ws(m_sgu_w_s), flat_ws(v_sgu_w_s), 512)
    out["sgu_w_s"] = tuple(t.reshape(sgu_w_s.shape) for t in (g_ws, wd_, wm_, wv_))

    reduced.update(zip(groups["f1"], _rs_finish("f1", rs_f1, wd_, qc_idx)))
    for key in groups["f1"]:
        big_update(big_names[key], reduced[key])

    order = ["ada_w", "ada_b", "norm_ffn1", "ffn1_w_gate", "ffn1_w_up", "ffn1_w_down", "norm_mix", "mix_w_in", "mix_b_in",
             "sgu_ln_g", "sgu_ln_b", "sgu_w_s", "sgu_b_s", "conv_w", "conv_b", "conv_ln_g", "conv_ln_b", "w_branch_a",
             "w_branch_b", "w_out", "norm_ffn2", "ffn2_w_gate", "ffn2_w_up", "ffn2_w_down", "norm_final"]
    return (loss, grad_x[None], *[out[n][0] for n in order], *[out[n][1] for n in order],
            *[out[n][2] for n in order], *[out[n][3] for n in order])
```

```python
import functools

import jax
import jax.numpy as jnp
from jax import lax
from jax.experimental import pallas as pl
from jax.experimental.pallas import tpu as pltpu

F32 = jnp.float32
BF16 = jnp.bfloat16
D = 1024
HEADS = 8
CHUNK = 128
KCONV = 31
HALO = 32
N_MOD = 9
EPS = 1e-6
NCHIP = 4
VMEM_LIMIT = 56 * 1024 * 1024
MESH = pl.DeviceIdType.MESH

ADAM_LR = 0.001
ADAM_B1 = 0.9
ADAM_B2 = 0.999
ADAM_EPS = 1e-08
ADAM_WD = 0.01
ADAM_STEP = 10

NN = (((1,), (0,)), ((), ()))
NT = (((1,), (1,)), ((), ()))
TN = (((0,), (0,)), ((), ()))


def _cparams(*sem):
    return pltpu.CompilerParams(dimension_semantics=sem or None, vmem_limit_bytes=VMEM_LIMIT)


def _rms(x, w):
    return x * lax.rsqrt(jnp.mean(x * x, axis=-1, keepdims=True) + EPS) * w


def _ln(x, g, b):
    mu = jnp.mean(x, axis=-1, keepdims=True)
    xc = x - mu
    var = jnp.mean(xc * xc, axis=-1, keepdims=True)
    return xc * lax.rsqrt(var + EPS) * g + b


def _silu(x):
    return x * jax.nn.sigmoid(x)


def _gelu(x):
    return x * (0.5 * (1.0 + jnp.tanh(0.7978845608028654 * (x + 0.044715 * (x * x * x)))))


def _f_pre(x, nw, sc, sh):
    return _rms(x, nw) * (1.0 + sc) + sh


def _f_pre_keep(x, nw, sc, sh):
    return x, _f_pre(x, nw, sc, sh)


def _f_res_pre(scale):
    def fn(x, f, g, nw, sc, sh):
        xn = x + (scale * g) * f
        return xn, _f_pre(xn, nw, sc, sh)
    return fn


def _f_glu(a, b):
    return _silu(a) * b


def _f_merge(gab, ya, yb, bgab):
    g = gab + bgab
    return jax.nn.sigmoid(g[:, :D]) * ya + jax.nn.sigmoid(g[:, D:]) * yb


def _f_lnsilu(z, g, b):
    return _silu(_ln(z, g, b))


def _f_mixa(uv, buv, g, b):
    t = uv + buv
    return _gelu(t[:, :D]), _ln(_gelu(t[:, D:]), g, b)


def _col(t):
    return t if isinstance(t, tuple) else (t, t.shape[-1], 0)


def _rowwise_fwd(name, fn, rows, params, outs, tr):
    rows = [_col(r) for r in rows]
    params = [_col(p) for p in params]
    n_rows = rows[0][0].shape[0]
    nin = len(rows) + len(params)

    def body(*refs):
        res = fn(*[r[...].astype(F32) for r in refs[:nin]])
        res = res if isinstance(res, tuple) else (res,)
        for o, v in zip(refs[nin:], res, strict=True):
            o[...] = v.astype(o.dtype)

    return pl.pallas_call(
        body, name=name, grid=(n_rows // tr,),
        in_specs=[pl.BlockSpec((tr, w), functools.partial(lambda cb, i: (i, cb), cb)) for _, w, cb in rows]
        + [pl.BlockSpec((1, w), functools.partial(lambda cb, i: (0, cb), cb)) for _, w, cb in params],
        out_specs=[pl.BlockSpec((tr, w), lambda i: (i, 0)) for w, _ in outs],
        out_shape=[jax.ShapeDtypeStruct((n_rows, w), dt) for w, dt in outs],
        compiler_params=_cparams("parallel"),
    )(*[r[0] for r in rows], *[p[0] for p in params])


def _rowwise_bwd(name, fn, rows, params, cots, row_grads, tr):
    rows = [_col(r) for r in rows]
    params = [_col(p) for p in params]
    cots = [_col(t) for t in cots]
    n_rows = rows[0][0].shape[0]
    nr, npar, nc = len(rows), len(params), len(cots)
    nin = nr + npar + nc
    n_rg = sum(dt is not None for dt in row_grads)

    def body(*refs):
        i = pl.program_id(0)
        prim = [r[...].astype(F32) for r in refs[:nr + npar]]
        ct = tuple(r[...].astype(F32) for r in refs[nr + npar:nin])
        _, vjp = jax.vjp(fn, *prim)
        g = vjp(ct if nc > 1 else ct[0])
        outs = refs[nin:]
        oi = 0
        for j, dt in enumerate(row_grads):
            if dt is not None:
                outs[oi][...] = g[j].astype(dt)
                oi += 1
        for j in range(npar):
            acc = outs[n_rg + j]

            @pl.when(i == 0)
            def _(acc=acc):
                acc[...] = jnp.zeros(acc.shape, F32)

            acc[...] += g[nr + j]

    res = pl.pallas_call(
        body, name=name, grid=(n_rows // tr,),
        in_specs=[pl.BlockSpec((tr, w), functools.partial(lambda cb, i: (i, cb), cb)) for _, w, cb in rows]
        + [pl.BlockSpec((1, w), functools.partial(lambda cb, i: (0, cb), cb)) for _, w, cb in params]
        + [pl.BlockSpec((tr, w), functools.partial(lambda cb, i: (i, cb), cb)) for _, w, cb in cots],
        out_specs=[pl.BlockSpec((tr, rows[j][1]), lambda i: (i, 0)) for j, dt in enumerate(row_grads) if dt is not None]
        + [pl.BlockSpec((1, w), lambda i: (0, 0)) for _, w, _ in params],
        out_shape=[jax.ShapeDtypeStruct((n_rows, rows[j][1]), dt) for j, dt in enumerate(row_grads) if dt is not None]
        + [jax.ShapeDtypeStruct((1, w), F32) for _, w, _ in params],
        compiler_params=_cparams("arbitrary"),
    )(*[r[0] for r in rows], *[p[0] for p in params], *[t[0] for t in cots])
    return res[:n_rg], res[n_rg:]


def _final(name, x, f, tgt, g, nw, tr):
    n_rows = x.shape[0]

    def body(x_ref, f_ref, t_ref, g_ref, nw_ref, dx_ref, df_ref, dg_ref, dnw_ref, loss_ref):
        i = pl.program_id(0)
        tg = t_ref[...]

        def fn(xv, fv, gv, nwv):
            e = _rms(xv + (0.5 * gv) * fv, nwv) - tg
            return 0.5 * jnp.mean(e * e, axis=-1, keepdims=True)

        per_row, vjp = jax.vjp(fn, x_ref[...], f_ref[...], g_ref[...], nw_ref[...])
        dx, df, dg, dnw = vjp(jnp.ones_like(per_row))
        dx_ref[...] = dx
        df_ref[...] = df.astype(df_ref.dtype)

        @pl.when(i == 0)
        def _():
            dg_ref[...] = jnp.zeros(dg_ref.shape, F32)
            dnw_ref[...] = jnp.zeros(dnw_ref.shape, F32)
            loss_ref[...] = jnp.zeros(loss_ref.shape, F32)

        dg_ref[...] += dg
        dnw_ref[...] += dnw
        loss_ref[...] += jnp.broadcast_to(jnp.sum(per_row, axis=0, keepdims=True), loss_ref.shape)

    row = pl.BlockSpec((tr, D), lambda i: (i, 0))
    par = pl.BlockSpec((1, D), lambda i: (0, 0))
    return pl.pallas_call(
        body, name=name, grid=(n_rows // tr,),
        in_specs=[row, row, row, par, par],
        out_specs=[row, row, par, par, pl.BlockSpec((8, 128), lambda i: (0, 0))],
        out_shape=[jax.ShapeDtypeStruct((n_rows, D), F32), jax.ShapeDtypeStruct((n_rows, D), BF16),
                   jax.ShapeDtypeStruct((1, D), F32), jax.ShapeDtypeStruct((1, D), F32),
                   jax.ShapeDtypeStruct((8, 128), F32)],
        compiler_params=_cparams("arbitrary"),
    )(x, f, tgt, g, nw)


def _tile(dim, pref):
    t = min(dim, pref)
    while dim % t:
        t //= 2
    return t


def _matmul(name, a, b, kind, col, out_dtype, add=None):
    if kind == "wgrad":
        m, kg = a.shape
        ng = b.shape[1]
        r, c = (kg, ng // NCHIP) if col else (kg // NCHIP, ng)
        tm, tn, tk = _tile(r, 1024), _tile(c, 1536), _tile(m, 2048)
        grid = (kg // tm, ng // tn, m // tk)
        a_spec = pl.BlockSpec((tk, tm), lambda i, j, k: (k, i))
        b_spec = pl.BlockSpec((tk, tn), lambda i, j, k: (k, j))
        if col:
            o_spec = pl.BlockSpec((None, tm, tn), lambda i, j, k: (j // (c // tn), i, j % (c // tn)))
        else:
            o_spec = pl.BlockSpec((None, tm, tn), lambda i, j, k: (i // (r // tm), i % (r // tm), j))
        out_shape = jax.ShapeDtypeStruct((NCHIP, r, c), F32)
        dims = TN
    else:
        _, r, c = b.shape
        m = a.shape[0]
        kg, ng = (r, NCHIP * c) if col else (NCHIP * r, c)
        tm = _tile(m, 1024)
        if kind == "fwd":
            tn, tk = _tile(c, 1536), _tile(r, 2048)
            grid = (m // tm, ng // tn, kg // tk)
            a_spec = pl.BlockSpec((tm, tk), lambda i, j, k: (i, k))
            if col:
                b_spec = pl.BlockSpec((None, tk, tn), lambda i, j, k: (j // (c // tn), k, j % (c // tn)))
            else:
                b_spec = pl.BlockSpec((None, tk, tn), lambda i, j, k: (k // (r // tk), k % (r // tk), j))
            out_shape = jax.ShapeDtypeStruct((m, ng), out_dtype)
            dims = NN
        else:
            tn, tk = _tile(r, 1024), _tile(c, 2048)
            grid = (m // tm, kg // tn, ng // tk)
            a_spec = pl.BlockSpec((tm, tk), lambda i, j, k: (i, k))
            if col:
                b_spec = pl.BlockSpec((None, tn, tk), lambda i, j, k: (k // (c // tk), j, k % (c // tk)))
            else:
                b_spec = pl.BlockSpec((None, tn, tk), lambda i, j, k: (j // (r // tn), j % (r // tn), k))
            out_shape = jax.ShapeDtypeStruct((m, kg), out_dtype)
            dims = NT
        o_spec = pl.BlockSpec((tm, tn), lambda i, j, k: (i, j))
    nk = grid[2]
    has_add = add is not None

    def body(*refs):
        a_ref, b_ref = refs[0], refs[1]
        o_ref = refs[3] if has_add else refs[2]
        k = pl.program_id(2)
        p = lax.dot_general(a_ref[...].astype(BF16), b_ref[...].astype(BF16), dims, preferred_element_type=F32)

        def finish(r_):
            if has_add:
                r_ = r_ + refs[2][...]
            o_ref[...] = r_.astype(o_ref.dtype)

        if nk == 1:
            finish(p)
        else:
            acc_ref = refs[-1]

            @pl.when(k == 0)
            def _():
                acc_ref[...] = p

            @pl.when(jnp.logical_and(k > 0, k < nk - 1))
            def _():
                acc_ref[...] += p

            @pl.when(k == nk - 1)
            def _():
                finish(acc_ref[...] + p)

    in_specs = [a_spec, b_spec] + ([pl.BlockSpec((tm, tn), lambda i, j, k: (i, j))] if has_add else [])
    args = (a, b) + ((add,) if has_add else ())
    return pl.pallas_call(
        body, name=name, grid=grid, in_specs=in_specs, out_specs=o_spec, out_shape=out_shape,
        scratch_shapes=[pltpu.VMEM((tm, tn), F32)] if nk > 1 else [],
        compiler_params=_cparams("parallel", "parallel", "arbitrary"),
    )(*args)


def _ffn_up(name, h, wg, wu):
    m, k = h.shape
    _, _, c = wg.shape
    n = NCHIP * c
    tm, tn = _tile(m, 1024), _tile(c, 1024)

    def body(h_ref, wg_ref, wu_ref, a_ref, b_ref, s_ref):
        hv = h_ref[...]
        a = jnp.dot(hv, wg_ref[...], preferred_element_type=F32)
        b = jnp.dot(hv, wu_ref[...], preferred_element_type=F32)
        a_ref[...] = a
        b_ref[...] = b
        s_ref[...] = (_silu(a) * b).astype(s_ref.dtype)

    w_spec = pl.BlockSpec((None, k, tn), lambda i, j: (j // (c // tn), 0, j % (c // tn)))
    o_spec = pl.BlockSpec((tm, tn), lambda i, j: (i, j))
    return pl.pallas_call(
        body, name=name, grid=(m // tm, n // tn),
        in_specs=[pl.BlockSpec((tm, k), lambda i, j: (i, 0)), w_spec, w_spec],
        out_specs=[o_spec, o_spec, o_spec],
        out_shape=[jax.ShapeDtypeStruct((m, n), F32), jax.ShapeDtypeStruct((m, n), F32),
                   jax.ShapeDtypeStruct((m, n), BF16)],
        compiler_params=_cparams("parallel", "parallel"),
    )(h, wg, wu)


def _dgrad_glu(name, df, wd, a, b):
    m = df.shape[0]
    _, r, c = wd.shape
    tm, tn = _tile(m, 512), _tile(r, 1024)

    def body(df_ref, w_ref, a_ref, b_ref, da_ref, db_ref):
        ds = lax.dot_general(df_ref[...], w_ref[...], NT, preferred_element_type=F32)
        _, vjp = jax.vjp(_f_glu, a_ref[...], b_ref[...])
        da, db = vjp(ds)
        da_ref[...] = da.astype(da_ref.dtype)
        db_ref[...] = db.astype(db_ref.dtype)

    t_spec = pl.BlockSpec((tm, tn), lambda i, j: (i, j))
    sds = jax.ShapeDtypeStruct((m, NCHIP * r), BF16)
    return pl.pallas_call(
        body, name=name, grid=(m // tm, NCHIP * r // tn),
        in_specs=[pl.BlockSpec((tm, c), lambda i, j: (i, 0)),
                  pl.BlockSpec((None, tn, c), lambda i, j: (j // (r // tn), j % (r // tn), 0)), t_spec, t_spec],
        out_specs=[t_spec, t_spec], out_shape=[sds, sds],
        compiler_params=_cparams("parallel", "parallel"),
    )(df, wd, a, b)


def _sgu_fwd(name, proj, b_in, lng, lnb, wm, bst, tr):
    n_rows = proj.shape[0]

    def body(uv_ref, buv_ref, g_ref, b_ref, wm_ref, bst_ref, ya_ref):
        u, vln = _f_mixa(uv_ref[...], buv_ref[...], g_ref[...], b_ref[...])
        vb = vln.astype(BF16)
        for ck in range(tr // CHUNK):
            rs = slice(ck * CHUNK, (ck + 1) * CHUNK)
            for h in range(HEADS):
                cs = slice(h * CHUNK, (h + 1) * CHUNK)
                vm = jnp.dot(wm_ref[h], vb[rs, cs], preferred_element_type=F32) + bst_ref[:, cs]
                ya_ref[rs, cs] = (u[rs, cs] * vm).astype(ya_ref.dtype)

    return pl.pallas_call(
        body, name=name, grid=(n_rows // tr,),
        in_specs=[pl.BlockSpec((tr, 2 * D), lambda i: (i, 0)), pl.BlockSpec((1, 2 * D), lambda i: (0, 0)),
                  pl.BlockSpec((1, D), lambda i: (0, 0)), pl.BlockSpec((1, D), lambda i: (0, 0)),
                  pl.BlockSpec((HEADS, CHUNK, CHUNK), lambda i: (0, 0, 0)),
                  pl.BlockSpec((CHUNK, D), lambda i: (0, 0))],
        out_specs=pl.BlockSpec((tr, D), lambda i: (i, 0)),
        out_shape=jax.ShapeDtypeStruct((n_rows, D), BF16),
        compiler_params=_cparams("parallel"),
    )(proj, b_in, lng, lnb, wm, bst)


def _sgu_bwd(name, proj, b_in, lng, lnb, wm, wmt, bst, dya, tr):
    n_rows = proj.shape[0]

    def body(uv_ref, buv_ref, g_ref, b_ref, wm_ref, wmt_ref, bst_ref, dya_ref,
             duv_ref, dbuv_ref, dg_ref, db_ref, dws_ref, dbs_ref, du_s, dvln_s):
        i = pl.program_id(0)

        @pl.when(i == 0)
        def _():
            dbuv_ref[...] = jnp.zeros(dbuv_ref.shape, F32)
            dg_ref[...] = jnp.zeros(dg_ref.shape, F32)
            db_ref[...] = jnp.zeros(db_ref.shape, F32)
            dws_ref[...] = jnp.zeros(dws_ref.shape, F32)
            dbs_ref[...] = jnp.zeros(dbs_ref.shape, F32)

        (u, vln), vjp = jax.vjp(_f_mixa, uv_ref[...], buv_ref[...], g_ref[...], b_ref[...])
        vb = vln.astype(BF16)
        dya_v = dya_ref[...].astype(F32)
        tpos = lax.broadcasted_iota(jnp.int32, (CHUNK, CHUNK), 0)
        spos = lax.broadcasted_iota(jnp.int32, (CHUNK, CHUNK), 1)
        causal = (tpos >= spos).astype(F32)
        for ck in range(tr // CHUNK):
            rs = slice(ck * CHUNK, (ck + 1) * CHUNK)
            for h in range(HEADS):
                cs = slice(h * CHUNK, (h + 1) * CHUNK)
                vbh = vb[rs, cs]
                vm = jnp.dot(wm_ref[h], vbh, preferred_element_type=F32) + bst_ref[:, cs]
                dyb = dya_v[rs, cs]
                du_s[rs, cs] = dyb * vm
                dvm = dyb * u[rs, cs]
                dvmb = dvm.astype(BF16)
                dvln_s[rs, cs] = jnp.dot(wmt_ref[h], dvmb, preferred_element_type=F32)
                dws_ref[h] += causal * lax.dot_general(dvmb, vbh, NT, preferred_element_type=F32)
                dbs_ref[:, cs] += jnp.broadcast_to(jnp.sum(dvm, axis=1, keepdims=True), (CHUNK, CHUNK))
        duv, dbuv, dg, db = vjp((du_s[...], dvln_s[...]))
        duv_ref[...] = duv.astype(duv_ref.dtype)
        dbuv_ref[...] += dbuv
        dg_ref[...] += dg
        db_ref[...] += db

    par = pl.BlockSpec((1, D), lambda i: (0, 0))
    par2 = pl.BlockSpec((1, 2 * D), lambda i: (0, 0))
    w_spec = pl.BlockSpec((HEADS, CHUNK, CHUNK), lambda i: (0, 0, 0))
    b_spec = pl.BlockSpec((CHUNK, D), lambda i: (0, 0))
    return pl.pallas_call(
        body, name=name, grid=(n_rows // tr,),
        in_specs=[pl.BlockSpec((tr, 2 * D), lambda i: (i, 0)), par2, par, par, w_spec, w_spec, b_spec,
                  pl.BlockSpec((tr, D), lambda i: (i, 0))],
        out_specs=[pl.BlockSpec((tr, 2 * D), lambda i: (i, 0)), par2, par, par, w_spec, b_spec],
        out_shape=[jax.ShapeDtypeStruct((n_rows, 2 * D), BF16), jax.ShapeDtypeStruct((1, 2 * D), F32),
                   jax.ShapeDtypeStruct((1, D), F32), jax.ShapeDtypeStruct((1, D), F32),
                   jax.ShapeDtypeStruct((HEADS, CHUNK, CHUNK), F32), jax.ShapeDtypeStruct((CHUNK, D), F32)],
        scratch_shapes=[pltpu.VMEM((tr, D), F32), pltpu.VMEM((tr, D), F32)],
        compiler_params=_cparams("arbitrary"),
    )(proj, b_in, lng, lnb, wm, wmt, bst, dya)


CT = 256
RB = 128
CV0 = 2 * D // CT
CG0 = 3 * D // CT


def _shifted(win, sh):
    if sh == 0:
        return win[0:RB]
    return pltpu.roll(win, RB + HALO - sh, 0)[0:RB]


def _conv_fwd(name, proj, b_in, w, cb):
    n_rows = proj.shape[0]

    def body(cv_ref, cg_ref, bcv_ref, bcg_ref, w_ref, cb_ref, z1_ref, zp_ref):
        zp_ref[0:HALO, :] = jnp.zeros((HALO, CT), F32)
        zp_ref[HALO:, :] = (cv_ref[...] + bcv_ref[...]) * jax.nn.sigmoid(cg_ref[...] + bcg_ref[...])

        def blk(rb, carry):
            base = pl.multiple_of(rb * RB, RB)
            win = zp_ref[pl.ds(base, RB + HALO), :]
            acc = jnp.broadcast_to(cb_ref[...], (RB, CT))
            for k in range(KCONV):
                acc = acc + w_ref[k:k + 1, :] * _shifted(win, k + 2)
            z1_ref[pl.ds(base, RB), :] = acc
            return carry

        lax.fori_loop(0, n_rows // RB, blk, 0)

    return pl.pallas_call(
        body, name=name, grid=(D // CT,),
        in_specs=[pl.BlockSpec((n_rows, CT), lambda j: (0, CV0 + j)), pl.BlockSpec((n_rows, CT), lambda j: (0, CG0 + j)),
                  pl.BlockSpec((1, CT), lambda j: (0, CV0 + j)), pl.BlockSpec((1, CT), lambda j: (0, CG0 + j)),
                  pl.BlockSpec((HALO, CT), lambda j: (0, j)), pl.BlockSpec((1, CT), lambda j: (0, j))],
        out_specs=pl.BlockSpec((n_rows, CT), lambda j: (0, j)),
        out_shape=jax.ShapeDtypeStruct((n_rows, D), F32),
        scratch_shapes=[pltpu.VMEM((n_rows + HALO, CT), F32)],
        compiler_params=_cparams("parallel"),
    )(proj, proj, b_in, b_in, w, cb)


def _conv_bwd(name, dz1, proj, b_in, w):
    n_rows = proj.shape[0]

    def body(dz_ref, cv_ref, cg_ref, bcv_ref, bcg_ref, w_ref, dcv_ref, dcg_ref, sm_ref, zp_ref, dzp_ref):
        cvb = cv_ref[...] + bcv_ref[...]
        sg = jax.nn.sigmoid(cg_ref[...] + bcg_ref[...])
        zp_ref[0:HALO, :] = jnp.zeros((HALO, CT), F32)
        zp_ref[HALO:, :] = cvb * sg
        dz = dz_ref[...]
        dzp_ref[0:n_rows, :] = dz
        dzp_ref[n_rows:, :] = jnp.zeros((HALO, CT), F32)
        sm_ref[...] = jnp.zeros(sm_ref.shape, F32)
        sm_ref[32:33, :] = jnp.sum(dz, axis=0, keepdims=True)

        def blk(rb, carry):
            base = pl.multiple_of(rb * RB, RB)
            dwin = dzp_ref[pl.ds(base, RB + HALO), :]
            zwin = zp_ref[pl.ds(base, RB + HALO), :]
            dzb = dwin[0:RB]
            acc = jnp.zeros((RB, CT), F32)
            for k in range(KCONV):
                acc = acc + w_ref[k:k + 1, :] * _shifted(dwin, KCONV - 1 - k)
                sm_ref[k:k + 1, :] += jnp.sum(dzb * _shifted(zwin, k + 2), axis=0, keepdims=True)
            dzp_ref[pl.ds(base, RB), :] = acc
            return carry

        lax.fori_loop(0, n_rows // RB, blk, 0)
        dz0 = dzp_ref[0:n_rows, :]
        dcv = dz0 * sg
        dcg = dz0 * cvb * (sg * (1.0 - sg))
        dcv_ref[...] = dcv.astype(dcv_ref.dtype)
        dcg_ref[...] = dcg.astype(dcg_ref.dtype)
        sm_ref[33:34, :] = jnp.sum(dcv, axis=0, keepdims=True)
        sm_ref[34:35, :] = jnp.sum(dcg, axis=0, keepdims=True)

    col = pl.BlockSpec((n_rows, CT), lambda j: (0, j))
    return pl.pallas_call(
        body, name=name, grid=(D // CT,),
        in_specs=[col, pl.BlockSpec((n_rows, CT), lambda j: (0, CV0 + j)), pl.BlockSpec((n_rows, CT), lambda j: (0, CG0 + j)),
                  pl.BlockSpec((1, CT), lambda j: (0, CV0 + j)), pl.BlockSpec((1, CT), lambda j: (0, CG0 + j)),
                  pl.BlockSpec((HALO, CT), lambda j: (0, j))],
        out_specs=[col, col, pl.BlockSpec((40, CT), lambda j: (0, j))],
        out_shape=[jax.ShapeDtypeStruct((n_rows, D), BF16), jax.ShapeDtypeStruct((n_rows, D), BF16),
                   jax.ShapeDtypeStruct((40, D), F32)],
        scratch_shapes=[pltpu.VMEM((n_rows + HALO, CT), F32), pltpu.VMEM((n_rows + HALO, CT), F32)],
        compiler_params=_cparams("parallel"),
    )(dz1, proj, proj, b_in, b_in, w)


ADA_TN = 768


def _split_bf16(v):
    hi = v.astype(BF16)
    return hi, (v - hi.astype(F32)).astype(BF16)


def _ada_fwd(name, c_all, w, b):
    n = w.shape[1]

    def body(c_ref, w_ref, b_ref, o_ref):
        ch, cl = _split_bf16(_silu(c_ref[...]))
        wh, wl = _split_bf16(w_ref[...])
        acc = jnp.dot(ch, wl, preferred_element_type=F32) + jnp.dot(cl, wh, preferred_element_type=F32)
        o_ref[...] = acc + jnp.dot(ch, wh, preferred_element_type=F32) + b_ref[...]

    return pl.pallas_call(
        body, name=name, grid=(n // ADA_TN,),
        in_specs=[pl.BlockSpec((8, D), lambda j: (0, 0)), pl.BlockSpec((D, ADA_TN), lambda j: (0, j)),
                  pl.BlockSpec((1, ADA_TN), lambda j: (0, j))],
        out_specs=pl.BlockSpec((8, ADA_TN), lambda j: (0, j)),
        out_shape=jax.ShapeDtypeStruct((8, n), F32),
        compiler_params=_cparams("parallel"),
    )(c_all, w, b)


def _ada_wgrad(name, c_all_t, dmod):
    n = dmod.shape[1]

    def body(ct_ref, dm_ref, o_ref):
        ca = _silu(ct_ref[...])
        acc = ca[:, 0:1] * dm_ref[0:1, :]
        for r in range(1, 8):
            acc = acc + ca[:, r:r + 1] * dm_ref[r:r + 1, :]
        o_ref[...] = acc

    return pl.pallas_call(
        body, name=name, grid=(n // ADA_TN,),
        in_specs=[pl.BlockSpec((D, 8), lambda j: (0, 0)), pl.BlockSpec((8, ADA_TN), lambda j: (0, j))],
        out_specs=pl.BlockSpec((D, ADA_TN), lambda j: (0, j)),
        out_shape=jax.ShapeDtypeStruct((D, n), F32),
        compiler_params=_cparams("parallel"),
    )(c_all_t, dmod)


def _adamw(name, w, g, m, v, tr):
    rows, cols = w.shape

    def body(w_ref, g_ref, m_ref, v_ref, d_ref, nm_ref, nv_ref):
        gv = g_ref[...]
        nm = ADAM_B1 * m_ref[...] + (1.0 - ADAM_B1) * gv
        nv = ADAM_B2 * v_ref[...] + (1.0 - ADAM_B2) * (gv * gv)
        m_hat = nm / (1.0 - ADAM_B1 ** ADAM_STEP)
        v_hat = nv / (1.0 - ADAM_B2 ** ADAM_STEP)
        d_ref[...] = -ADAM_LR * (m_hat / (jnp.sqrt(v_hat) + ADAM_EPS) + ADAM_WD * w_ref[...])
        nm_ref[...] = nm
        nv_ref[...] = nv

    spec = pl.BlockSpec((tr, cols), lambda i: (i, 0))
    sds = jax.ShapeDtypeStruct((rows, cols), F32)
    return pl.pallas_call(
        body, name=name, grid=(rows // tr,), in_specs=[spec] * 4, out_specs=[spec] * 3, out_shape=[sds] * 3,
        compiler_params=_cparams("parallel"),
    )(w, g, m, v)


def _sum8(name, stacked, tr):
    n, rows, cols = stacked.shape

    def body(s_ref, o_ref):
        acc = s_ref[0]
        for r in range(1, n):
            acc = acc + s_ref[r]
        o_ref[...] = acc

    return pl.pallas_call(
        body, name=name, grid=(rows // tr,),
        in_specs=[pl.BlockSpec((n, tr, cols), lambda i: (0, i, 0))],
        out_specs=pl.BlockSpec((tr, cols), lambda i: (i, 0)),
        out_shape=jax.ShapeDtypeStruct((rows, cols), F32),
        compiler_params=_cparams("parallel"),
    )(stacked)


def _pair_add(name, g5, other, c_idx, tr):
    nq, _, rows, cols = g5.shape

    def body(c_ref, g_ref, o_ref, p_ref):
        p_ref[...] = (g_ref[...] + o_ref[...]).astype(p_ref.dtype)

    return pl.pallas_call(
        body, name=name,
        grid_spec=pltpu.PrefetchScalarGridSpec(
            num_scalar_prefetch=1, grid=(nq, rows // tr),
            in_specs=[pl.BlockSpec((None, None, tr, cols), lambda qi, i, cr: (qi, cr[0], i, 0)),
                      pl.BlockSpec((None, tr, cols), lambda qi, i, cr: (qi, i, 0))],
            out_specs=pl.BlockSpec((None, tr, cols), lambda qi, i, cr: (qi, i, 0))),
        out_shape=jax.ShapeDtypeStruct((nq, rows, cols), BF16),
        compiler_params=_cparams("parallel", "parallel"),
    )(c_idx, g5, other)


def _chip_add(name, p, recv, qc_idx, tr):
    _, rows, cols = p.shape

    def body(qc_ref, p_ref, r_ref, o_ref):
        acc = p_ref[...].astype(F32)
        for k in range(NCHIP - 1):
            acc = acc + r_ref[k].astype(F32)
        o_ref[...] = acc

    return pl.pallas_call(
        body, name=name,
        grid_spec=pltpu.PrefetchScalarGridSpec(
            num_scalar_prefetch=1, grid=(rows // tr,),
            in_specs=[pl.BlockSpec((None, tr, cols), lambda i, qc: (qc[0], i, 0)),
                      pl.BlockSpec((NCHIP - 1, tr, cols), lambda i, qc: (0, i, 0))],
            out_specs=pl.BlockSpec((None, tr, cols), lambda i, qc: (qc[1], i, 0))),
        out_shape=jax.ShapeDtypeStruct((2, rows, cols), F32),
        compiler_params=_cparams("parallel"),
    )(qc_idx, p, recv)


def _allgather8(name, blk):
    m_per, n = blk.shape

    def body(x_ref, out_ref, send_sems, recv_sems, local_sem):
        x, y, c = lax.axis_index("x"), lax.axis_index("y"), lax.axis_index("c")
        me, sibling = (x, y, c), (x, y, 1 - c)
        chips = [(1 - x, y), (x, 1 - y), (1 - x, 1 - y)]

        def rows(px, py, pc):
            return out_ref.at[pl.ds((4 * px + 2 * py + pc) * m_per, m_per), :]

        def copy(k, block, to, src=None):
            return pltpu.make_async_remote_copy(
                src_ref=rows(*block) if src is None else src, dst_ref=rows(*block),
                send_sem=send_sems.at[k], recv_sem=recv_sems.at[k], device_id=to, device_id_type=MESH)

        mine = pltpu.make_async_copy(x_ref, rows(*me), local_sem)
        mine.start()
        first = [copy(0, me, sibling, src=x_ref)]
        first += [copy(1 + j, me, (*chip, c), src=x_ref) for j, chip in enumerate(chips)]
        for cp in first:
            cp.start()
        passed = [copy(4 + j, (*chip, c), sibling) for j, chip in enumerate(chips)]
        for j, chip in enumerate(chips):
            copy(1 + j, (*chip, c), me).wait_recv()
            passed[j].start()
        copy(0, sibling, me).wait_recv()
        for j, chip in enumerate(chips):
            copy(4 + j, (*chip, 1 - c), me).wait_recv()
        for cp in first + passed:
            cp.wait_send()
        mine.wait()

    return pl.pallas_call(
        body, name=name,
        out_shape=jax.ShapeDtypeStruct((8 * m_per, n), blk.dtype),
        in_specs=[pl.BlockSpec(memory_space=pltpu.VMEM)],
        out_specs=pl.BlockSpec(memory_space=pltpu.VMEM),
        scratch_shapes=[pltpu.SemaphoreType.DMA((7,)), pltpu.SemaphoreType.DMA((7,)), pltpu.SemaphoreType.DMA],
        compiler_params=pltpu.CompilerParams(vmem_limit_bytes=VMEM_LIMIT),
    )(blk)


def _exchange(name, srcs, out_shapes, n_local, n_remote, plan, aliases=None, after=None):
    ni, no = len(srcs), len(out_shapes)
    extra = [] if after is None else [after]

    def body(*refs):
        ins, outs = refs[:ni], refs[ni + len(extra):ni + len(extra) + no]
        send_sems, recv_sems, local_sems = refs[ni + len(extra) + no:]
        x, y, c = lax.axis_index("x"), lax.axis_index("y"), lax.axis_index("c")
        local, remote = plan(ins, outs, x, y, c)
        assert len(local) == n_local and len(remote) == n_remote

        def rcopy(i, dst):
            s, _, peer, _, _ = remote[i]
            return pltpu.make_async_remote_copy(src_ref=s, dst_ref=dst, send_sem=send_sems.at[i],
                                                recv_sem=recv_sems.at[i], device_id=peer, device_id_type=MESH)

        lcs = [pltpu.make_async_copy(s, d, local_sems.at[i]) for i, (s, d) in enumerate(local)]
        for cp in lcs:
            cp.start()
        first = [i for i in range(n_remote) if remote[i][4] is None]
        passed = [i for i in range(n_remote) if remote[i][4] is not None]
        for i in first:
            rcopy(i, remote[i][1]).start()
        arrived = set()
        for i in passed:
            j = remote[i][4]
            rcopy(j, remote[j][3]).wait_recv()
            arrived.add(j)
            rcopy(i, remote[i][1]).start()
        for i in range(n_remote):
            if i not in arrived:
                rcopy(i, remote[i][3]).wait_recv()
        for i in range(n_remote):
            rcopy(i, remote[i][1]).wait_send()
        for cp in lcs:
            cp.wait()

    any_spec = pl.BlockSpec(memory_space=pl.ANY)
    return pl.pallas_call(
        body, name=name, out_shape=out_shapes,
        in_specs=[any_spec] * (ni + len(extra)), out_specs=[any_spec] * no,
        input_output_aliases=aliases or {},
        scratch_shapes=[pltpu.SemaphoreType.DMA((n_remote,)), pltpu.SemaphoreType.DMA((n_remote,)),
                        pltpu.SemaphoreType.DMA((max(n_local, 1),))],
    )(*srcs, *extra)


_CHIP_FLIPS = ((0, 1), (1, 0), (1, 1))


def _flip(v, f):
    return 1 - v if f else v


EFFECT = pltpu.SideEffectType.DATAFLOW_SIDE_EFFECTING


def _split_start(name, bufs, n, plan, after=None):
    nb = len(bufs)
    extra = [] if after is None else [after]

    def body(*refs):
        ins = refs[:nb]
        send_sems, recv_sems = refs[nb + len(extra)], refs[nb + len(extra) + 1]
        token = refs[-1]
        x, y, c = lax.axis_index("x"), lax.axis_index("y"), lax.axis_index("c")
        copies = plan(ins, x, y, c)
        assert len(copies) == n
        for i, (s, d, peer, _) in enumerate(copies):
            pltpu.make_async_remote_copy(src_ref=s, dst_ref=d, send_sem=send_sems.at[i], recv_sem=recv_sems.at[i],
                                         device_id=peer, device_id_type=MESH).start()
        token[...] = jnp.zeros_like(token)

    hbm = pl.BlockSpec(memory_space=pltpu.HBM)
    sem = pl.BlockSpec(memory_space=pltpu.SEMAPHORE)
    res = pl.pallas_call(
        body, name=name,
        out_shape=(pltpu.SemaphoreType.DMA((n,)), pltpu.SemaphoreType.DMA((n,)),
                   *[pltpu.HBM(b.shape, b.dtype) for b in bufs], jax.ShapeDtypeStruct((8, 128), F32)),
        in_specs=[hbm] * nb + [pl.BlockSpec(memory_space=pl.ANY)] * len(extra),
        out_specs=(sem, sem, *[hbm] * nb, pl.BlockSpec(memory_space=pltpu.VMEM)),
        input_output_aliases={i: 2 + i for i in range(nb)},
        compiler_params=pltpu.CompilerParams(has_side_effects=EFFECT),
    )(*[pltpu.with_memory_space_constraint(b, pltpu.HBM) for b in bufs], *extra)
    return res[0], res[1], list(res[2:2 + nb]), res[-1]


def _split_wait(name, bufs, send_sems, recv_sems, after, n, plan):
    nb = len(bufs)

    def body(*refs):
        ins = refs[:nb]
        ssem, rsem = refs[nb], refs[nb + 1]
        x, y, c = lax.axis_index("x"), lax.axis_index("y"), lax.axis_index("c")
        copies = plan(ins, x, y, c)
        assert len(copies) == n
        for i, (s, _, peer, lands) in enumerate(copies):
            cp = pltpu.make_async_remote_copy(src_ref=s, dst_ref=lands, send_sem=ssem.at[i], recv_sem=rsem.at[i],
                                              device_id=peer, device_id_type=MESH)
            cp.wait_send()
            cp.wait_recv()

    hbm = pl.BlockSpec(memory_space=pltpu.HBM)
    sem = pl.BlockSpec(memory_space=pltpu.SEMAPHORE)
    res = pl.pallas_call(
        body, name=name,
        out_shape=[pltpu.HBM(b.shape, b.dtype) for b in bufs],
        in_specs=[hbm] * nb + [sem, sem, pl.BlockSpec(memory_space=pl.ANY)],
        out_specs=[hbm] * nb,
        input_output_aliases={i: i for i in range(nb)},
        compiler_params=pltpu.CompilerParams(has_side_effects=EFFECT),
    )(*bufs, send_sems, recv_sems, after)
    return list(res)


def _chips_of(x, y):
    return [(_flip(x, fx), _flip(y, fy)) for fx, fy in _CHIP_FLIPS]


def _gather_start(tag, stacks, after):
    n = len(stacks)

    def plan(refs, x, y, c):
        q = 2 * x + y
        return [(refs[a].at[q, c], refs[a].at[q, c], (px, py, c), refs[a].at[2 * px + py, c])
                for a in range(n) for px, py in _chips_of(x, y)]

    send, recv, thru, token = _split_start(f"gather_start_{tag}", stacks, 3 * n, plan, after)
    return (send, recv, thru, plan), token


def _gather_finish(tag, state, after):
    send, recv, thru, plan = state
    n = len(thru)
    landed = _split_wait(f"gather_wait_{tag}", thru, send, recv, after, 3 * n, plan)

    def pass_on(ins, outs, x, y, c):
        sib = (x, y, 1 - c)
        remote = []
        for a in range(n):
            for px, py in _chips_of(x, y):
                p = 2 * px + py
                remote.append((ins[a].at[p, c], outs[a].at[p, c], sib, outs[a].at[p, 1 - c], None))
        return [], remote

    full = _exchange(f"gather_pass_{tag}", landed, [jax.ShapeDtypeStruct(s.shape, s.dtype) for s in landed],
                     0, 3 * n, pass_on, aliases={a: a for a in range(n)})
    return [g.reshape(NCHIP, 2 * g.shape[2], g.shape[3]) for g in full]


def _gather_small(name, smalls):
    n = len(smalls)

    def plan(ins, outs, x, y, c):
        q = 2 * x + y
        local = [(ins[b], outs[b].at[q]) for b in range(n)]
        remote = [(ins[b], outs[b].at[q], (px, py, c), outs[b].at[2 * px + py], None)
                  for b in range(n) for px, py in _chips_of(x, y)]
        return local, remote

    return _exchange(name, smalls, [jax.ShapeDtypeStruct((NCHIP,) + s.shape, s.dtype) for s in smalls], n, 3 * n, plan)


def _rs_start(tag, grads, c_idx, after=None):
    n = len(grads)
    g5 = [g.reshape(NCHIP, 2, g.shape[1] // 2, g.shape[2]) for g in grads]
    half = [(g.shape[2], g.shape[3]) for g in g5]

    def swap(ins, outs, x, y, c):
        sib = (x, y, 1 - c)
        return [], [(ins[a].at[:, 1 - c], outs[a], sib, outs[a], None) for a in range(n)]

    got = _exchange(f"rs_pair_{tag}", g5, [jax.ShapeDtypeStruct((NCHIP,) + h, F32) for h in half], 0, n, swap,
                    after=after)
    part = [_pair_add(f"rs_pair_add_{tag}{a}", g5[a], got[a], c_idx, _tile(half[a][0], 512)) for a in range(n)]
    land = [lax.empty((NCHIP - 1,) + h, BF16) for h in half]

    def plan(refs, x, y, c):
        return [(refs[a].at[2 * px + py], refs[n + a].at[k], (px, py, c), refs[n + a].at[k])
                for a in range(n) for k, (px, py) in enumerate(_chips_of(x, y))]

    send, recv, thru, token = _split_start(f"rs_chips_start_{tag}", part + land, 3 * n, plan)
    return (send, recv, thru, plan, half), token


def _rs_finish(tag, state, after, qc_idx):
    send, recv, thru, plan, half = state
    n = len(half)
    landed = _split_wait(f"rs_chips_wait_{tag}", thru, send, recv, after, 3 * n, plan)
    red = [_chip_add(f"rs_chip_add_{tag}{a}", landed[a], landed[n + a], qc_idx, _tile(half[a][0], 512)) for a in range(n)]

    def share(ins, outs, x, y, c):
        sib = (x, y, 1 - c)
        return [], [(ins[a].at[c], outs[a].at[c], sib, outs[a].at[1 - c], None) for a in range(n)]

    full = _exchange(f"rs_share_{tag}", red, [jax.ShapeDtypeStruct((2,) + h, F32) for h in half], 0, n, share,
                     aliases={a: a for a in range(n)})
    return [f.reshape(2 * h[0], h[1]) for f, h in zip(full, half)]


def _pack(name, n_rows, pieces, after=None):
    arrays = [p[1] for p in pieces if p[0] != "zeros"]
    extra = [] if after is None else [after]

    def body(*refs):
        o_ref = refs[-1]
        o_ref[...] = jnp.zeros(o_ref.shape, F32)
        row, ai = 0, 0
        for p in pieces:
            kind = p[0]
            if kind == "zeros":
                row += p[1]
                continue
            ref = refs[len(extra) + ai]
            ai += 1
            if kind == "rows":
                r = ref.shape[0]
                o_ref[row:row + r, :] = ref[...]
            elif kind == "wide":
                r = ref.shape[1] // D
                for j in range(r):
                    o_ref[row + j:row + j + 1, :] = ref[:, j * D:(j + 1) * D]
            elif kind == "slice":
                r = p[3] - p[2]
                o_ref[row:row + r, :] = ref[p[2]:p[3], :]
            elif kind == "heads":
                r = CHUNK
                for h in range(HEADS):
                    o_ref[row:row + r, h * CHUNK:(h + 1) * CHUNK] = ref[h]
            else:
                r = 1
                o_ref[row:row + 1, 0:128] = ref[0:1, :]
            row += r
        assert row == n_rows, (row, n_rows)

    vmem = pl.BlockSpec(memory_space=pltpu.VMEM)
    return pl.pallas_call(
        body, name=name, out_shape=jax.ShapeDtypeStruct((n_rows, D), F32),
        in_specs=[pl.BlockSpec(memory_space=pl.ANY)] * len(extra) + [vmem] * len(arrays), out_specs=vmem,
        compiler_params=pltpu.CompilerParams(vmem_limit_bytes=VMEM_LIMIT),
    )(*extra, *arrays)


def _to_bf16_stack(name, w, q_idx):
    r, cols = w.shape
    half = r // 2
    tr = _tile(half, 256)
    nb = half // tr

    def body(q_ref, w_ref, o_ref):
        o_ref[...] = w_ref[...].astype(BF16)

    return pl.pallas_call(
        body, name=name,
        grid_spec=pltpu.PrefetchScalarGridSpec(
            num_scalar_prefetch=1, grid=(2 * nb,),
            in_specs=[pl.BlockSpec((tr, cols), lambda i, qr: (i, 0))],
            out_specs=pl.BlockSpec((None, None, tr, cols), lambda i, qr: (qr[0], i // nb, i % nb, 0))),
        out_shape=jax.ShapeDtypeStruct((NCHIP, 2, half, cols), BF16),
        compiler_params=_cparams("parallel"),
    )(q_idx, w)


def kernel(x, c, ada_w, ada_b, norm_ffn1, ffn1_w_gate, ffn1_w_up, ffn1_w_down, norm_mix, mix_w_in, mix_b_in, sgu_ln_g, sgu_ln_b, sgu_w_s, sgu_b_s, conv_w, conv_b, conv_ln_g, conv_ln_b, w_branch_a, w_branch_b, w_out, norm_ffn2, ffn2_w_gate, ffn2_w_up, ffn2_w_down, norm_final, loss_target, m_ada_w, m_ada_b, m_norm_ffn1, m_ffn1_w_gate, m_ffn1_w_up, m_ffn1_w_down, m_norm_mix, m_mix_w_in, m_mix_b_in, m_sgu_ln_g, m_sgu_ln_b, m_sgu_w_s, m_sgu_b_s, m_conv_w, m_conv_b, m_conv_ln_g, m_conv_ln_b, m_w_branch_a, m_w_branch_b, m_w_out, m_norm_ffn2, m_ffn2_w_gate, m_ffn2_w_up, m_ffn2_w_down, m_norm_final, v_ada_w, v_ada_b, v_norm_ffn1, v_ffn1_w_gate, v_ffn1_w_up, v_ffn1_w_down, v_norm_mix, v_mix_w_in, v_mix_b_in, v_sgu_ln_g, v_sgu_ln_b, v_sgu_w_s, v_sgu_b_s, v_conv_w, v_conv_b, v_conv_ln_g, v_conv_ln_b, v_w_branch_a, v_w_branch_b, v_w_out, v_norm_ffn2, v_ffn2_w_gate, v_ffn2_w_up, v_ffn2_w_down, v_norm_final):
    xi, yi, ci = lax.axis_index("x"), lax.axis_index("y"), lax.axis_index("c")
    q = 2 * xi + yi
    dev = 4 * xi + 2 * yi + ci
    c_idx = jnp.reshape(ci, (1,)).astype(jnp.int32)
    q_idx = jnp.reshape(q, (1,)).astype(jnp.int32)
    x0 = x[0]
    tgt = loss_target[0]
    qc_idx = jnp.stack([q, ci]).astype(jnp.int32)
    n_ada = ada_w.shape[2]

    c_all = _allgather8("ag_c", jnp.pad(c, ((0, 7), (0, 0))))[::8]
    ada_b_mine = lax.dynamic_slice(ada_b, (0, q * n_ada), (1, n_ada))
    mod_part = _ada_fwd("ada_fwd", c_all, ada_w[0], ada_b_mine)
    small_parts = _gather_small("gather_small", [mod_part, jnp.pad(conv_w[0], ((0, 1), (0, 0)))])

    big = dict(g1=ffn1_w_gate, u1=ffn1_w_up, d1=ffn1_w_down, win=mix_w_in, wa=w_branch_a, wb=w_branch_b, wo=w_out,
               g2=ffn2_w_gate, u2=ffn2_w_up, d2=ffn2_w_down)
    groups = dict(f1=("d1", "g1", "u1"), mx=("wo", "wa", "wb", "win"), f2=("d2", "g2", "u2"))
    gather_groups = dict(up1=("g1", "u1"), dn1=("d1",), win=("win",), mx=("wa", "wb", "wo"), f2=("g2", "u2", "d2"))
    gather_state = {}
    started = jnp.zeros((1, 1), F32)
    for tag, keys in gather_groups.items():
        gather_state[tag], token = _gather_start(
            tag, [_to_bf16_stack(f"cast_{k}", big[k][0], q_idx) for k in keys], small_parts[0])
        started = started + token[0:1, 0:1]

    mod_all = jnp.transpose(small_parts[0], (1, 0, 2)).reshape(8, N_MOD * D)
    mod = lax.dynamic_slice(mod_all, (dev, 0), (1, N_MOD * D)) + started
    sh1, sc1, g1, sh2, sc2, g2, sh3, sc3, g3 = [mod[:, k * D:(k + 1) * D] for k in range(N_MOD)]
    convw = jnp.transpose(small_parts[1], (1, 0, 2)).reshape(HALO, D)
    W = dict(zip(gather_groups["up1"], _gather_finish("up1", gather_state["up1"], mod)))

    causal = jnp.tril(jnp.ones((CHUNK, CHUNK), bool))
    wm_f = jnp.where(causal[None], sgu_w_s[0], 0.0)
    wm = wm_f.astype(BF16)
    wmt = jnp.swapaxes(wm_f, 1, 2).astype(BF16)
    bst = jnp.repeat(jnp.transpose(sgu_b_s[0]), CHUNK, axis=1)
    b_in = mix_b_in

    (h1,) = _rowwise_fwd("pre1", _f_pre, [x0], [norm_ffn1, sc1, sh1], [(D, BF16)], 256)
    a1, b1, s1 = _ffn_up("up1", h1, W["g1"], W["u1"])
    W.update(zip(gather_groups["dn1"], _gather_finish("dn1", gather_state["dn1"], s1)))
    f1 = _matmul("down1", s1, W["d1"], "fwd", False, F32)
    x1, h2 = _rowwise_fwd("res1", _f_res_pre(0.5), [x0, f1], [g1, norm_mix, sc2, sh2], [(D, F32), (D, BF16)], 256)
    W.update(zip(gather_groups["win"], _gather_finish("win", gather_state["win"], x1)))
    proj = _matmul("in_proj", h2, W["win"], "fwd", True, F32)
    W.update(zip(gather_groups["mx"], _gather_finish("mx", gather_state["mx"], proj)))
    ya = _sgu_fwd("sgu", proj, b_in, sgu_ln_g, sgu_ln_b, wm, bst, 256)
    z1 = _conv_fwd("conv", proj, b_in, convw, conv_b)
    (z3,) = _rowwise_fwd("conv_ln", _f_lnsilu, [z1], [conv_ln_g, conv_ln_b], [(D, BF16)], 256)
    y_a = _matmul("branch_a", ya, W["wa"], "fwd", False, F32)
    y_b = _matmul("branch_b", z3, W["wb"], "fwd", False, F32)
    (merged,) = _rowwise_fwd("merge", _f_merge, [(proj, 2 * D, 2), y_a, y_b], [(b_in, 2 * D, 2)], [(D, BF16)], 256)
    y = _matmul("out_proj", merged, W["wo"], "fwd", False, F32)
    x2, h3 = _rowwise_fwd("res2", _f_res_pre(1.0), [x1, y], [g2, norm_ffn2, sc3, sh3], [(D, F32), (D, BF16)], 256)
    W.update(zip(gather_groups["f2"], _gather_finish("f2", gather_state["f2"], x2)))
    a3, b3, s3 = _ffn_up("up2", h3, W["g2"], W["u2"])
    f3 = _matmul("down2", s3, W["d2"], "fwd", False, F32)
    dx2, df3, dg3, dnwf, loss_blk = _final("final", x2, f3, tgt, g3, norm_final.reshape(1, D), 256)

    G = {}

    def ffn_bwd(tag, df, s, a, b, h, wg, wu, wd):
        da, db = _dgrad_glu(f"d_down{tag}", df, wd, a, b)
        G["d" + tag] = _matmul(f"w_down{tag}", s, df, "wgrad", False, F32)
        dh = _matmul(f"d_gate{tag}", da, wg, "dgrad", True, F32)
        dh = _matmul(f"d_up{tag}", db, wu, "dgrad", True, F32, add=dh)
        G["g" + tag] = _matmul(f"w_gate{tag}", h, da, "wgrad", True, F32)
        G["u" + tag] = _matmul(f"w_up{tag}", h, db, "wgrad", True, F32)
        return dh

    reduced = {}
    dh3 = ffn_bwd("2", df3, s3, a3, b3, h3, W["g2"], W["u2"], W["d2"])
    rs_f2, token = _rs_start("f2", [G[k] for k in groups["f2"]], c_idx)
    (dx1, dy), (dg2, dnw3, dsc3, dsh3) = _rowwise_bwd(
        "res2_b", _f_res_pre(1.0), [x1, y], [g2 + token[0:1, 0:1], norm_ffn2, sc3, sh3], [dx2, dh3], [F32, BF16], 256)
    dmerged = _matmul("d_out", dy, W["wo"], "dgrad", False, F32)
    G["wo"] = _matmul("w_out", merged, dy, "wgrad", False, F32)
    (dgab, dy_a, dy_b), (dbgab,) = _rowwise_bwd(
        "merge_b", _f_merge, [(proj, 2 * D, 2), y_a, y_b], [(b_in, 2 * D, 2)], [dmerged], [BF16, BF16, BF16], 256)
    dya = _matmul("d_a", dy_a, W["wa"], "dgrad", False, F32)
    dz3 = _matmul("d_b", dy_b, W["wb"], "dgrad", False, F32)
    G["wa"] = _matmul("w_a", ya, dy_a, "wgrad", False, F32)
    G["wb"] = _matmul("w_b", z3, dy_b, "wgrad", False, F32)
    (dz1,), (dclg, dclb) = _rowwise_bwd("conv_ln_b", _f_lnsilu, [z1], [conv_ln_g, conv_ln_b], [dz3], [F32], 256)
    dcv, dcg, conv_small = _conv_bwd("conv_b", dz1, proj, b_in, convw)
    duv, dbuv, dlng, dlnb, dws, dbs = _sgu_bwd("sgu_b", proj, b_in, sgu_ln_g, sgu_ln_b, wm, wmt, bst, dya, 256)
    dproj = jnp.concatenate([duv, dcv, dcg, dgab], axis=1)
    dh2 = _matmul("d_in", dproj, W["win"], "dgrad", True, F32)
    G["win"] = _matmul("w_in", h2, dproj, "wgrad", True, F32)
    reduced.update(zip(groups["f2"], _rs_finish("f2", rs_f2, dh2, qc_idx)))

    early = _pack("pack_early", HALO + CHUNK, [("slice", conv_small, 0, HALO), ("heads", dws)], after=reduced["d2"])
    every_early = _allgather8("ag_early", early)
    tot_early = _sum8("sum_early", every_early.reshape(8, HALO + CHUNK, D), HALO)

    rs_mx, token = _rs_start("mx", [G[k] for k in groups["mx"]], c_idx, every_early)
    (dx0, df1), (dg1, dnw2, dsc2, dsh2) = _rowwise_bwd(
        "res1_b", _f_res_pre(0.5), [x0, f1], [g1 + token[0:1, 0:1], norm_mix, sc2, sh2], [dx1, dh2], [F32, BF16], 256)
    dh1 = ffn_bwd("1", df1, s1, a1, b1, h1, W["g1"], W["u1"], W["d1"])
    reduced.update(zip(groups["mx"], _rs_finish("mx", rs_mx, dh1, qc_idx)))
    (grad_x,), (dnw1, dsc1, dsh1) = _rowwise_bwd(
        "pre1_b", _f_pre_keep, [x0], [norm_ffn1, sc1, sh1], [dx0, dh1], [F32], 256)

    dbs_row = jnp.transpose(dbs[:, ::CHUNK]).reshape(1, D)
    small = _pack("pack_small", HALO, [
        ("rows", dnw1), ("rows", dnw2), ("wide", dbuv), ("slice", conv_small, 33, 34), ("slice", conv_small, 34, 35),
        ("wide", dbgab), ("rows", dlng), ("rows", dlnb), ("rows", dbs_row), ("slice", conv_small, 32, 33),
        ("rows", dclg), ("rows", dclb), ("rows", dnw3), ("rows", dnwf),
        ("rows", dsh1), ("rows", dsc1), ("rows", dg1), ("rows", dsh2), ("rows", dsc2), ("rows", dg2),
        ("rows", dsh3), ("rows", dsc3), ("rows", dg3), ("lanes", loss_blk), ("zeros", 6)])
    every = _allgather8("ag_small", small)
    rs_f1, token = _rs_start("f1", [G[k] for k in groups["f1"]], c_idx, every)
    every = (every + token[0:1, 0:1]).reshape(8, HALO, D)
    tot = _sum8("sum_small", every, HALO)
    loss = tot[25, 0]
    dmod_all = every[:, 16:16 + N_MOD, :].reshape(8, N_MOD * D)
    dmod_mine = lax.dynamic_slice(dmod_all, (0, q * n_ada), (8, n_ada))
    g_ada_w = _ada_wgrad("ada_wgrad", jnp.transpose(c_all), dmod_mine)

    given = dict(ada_w=(ada_w, m_ada_w, v_ada_w), ada_b=(ada_b, m_ada_b, v_ada_b),
                 norm_ffn1=(norm_ffn1, m_norm_ffn1, v_norm_ffn1),
                 ffn1_w_gate=(ffn1_w_gate, m_ffn1_w_gate, v_ffn1_w_gate), ffn1_w_up=(ffn1_w_up, m_ffn1_w_up, v_ffn1_w_up),
                 ffn1_w_down=(ffn1_w_down, m_ffn1_w_down, v_ffn1_w_down), norm_mix=(norm_mix, m_norm_mix, v_norm_mix),
                 mix_w_in=(mix_w_in, m_mix_w_in, v_mix_w_in), mix_b_in=(mix_b_in, m_mix_b_in, v_mix_b_in),
                 sgu_ln_g=(sgu_ln_g, m_sgu_ln_g, v_sgu_ln_g), sgu_ln_b=(sgu_ln_b, m_sgu_ln_b, v_sgu_ln_b),
                 sgu_w_s=(sgu_w_s, m_sgu_w_s, v_sgu_w_s), sgu_b_s=(sgu_b_s, m_sgu_b_s, v_sgu_b_s),
                 conv_w=(conv_w, m_conv_w, v_conv_w), conv_b=(conv_b, m_conv_b, v_conv_b),
                 conv_ln_g=(conv_ln_g, m_conv_ln_g, v_conv_ln_g), conv_ln_b=(conv_ln_b, m_conv_ln_b, v_conv_ln_b),
                 w_branch_a=(w_branch_a, m_w_branch_a, v_w_branch_a), w_branch_b=(w_branch_b, m_w_branch_b, v_w_branch_b),
                 w_out=(w_out, m_w_out, v_w_out), norm_ffn2=(norm_ffn2, m_norm_ffn2, v_norm_ffn2),
                 ffn2_w_gate=(ffn2_w_gate, m_ffn2_w_gate, v_ffn2_w_gate), ffn2_w_up=(ffn2_w_up, m_ffn2_w_up, v_ffn2_w_up),
                 ffn2_w_down=(ffn2_w_down, m_ffn2_w_down, v_ffn2_w_down), norm_final=(norm_final, m_norm_final, v_norm_final))
    out = {}

    def big_update(name, g2d):
        w, m, v = given[name]
        shp = w.shape
        d_, nm, nv = _adamw(f"adamw_{name}", w.reshape(g2d.shape), g2d, m.reshape(g2d.shape), v.reshape(g2d.shape),
                            _tile(g2d.shape[0], 512 if g2d.shape[1] <= D else 256))
        out[name] = tuple(t.reshape(shp) for t in (g2d, d_, nm, nv))

    big_names = dict(g1="ffn1_w_gate", u1="ffn1_w_up", d1="ffn1_w_down", win="mix_w_in", wa="w_branch_a",
                     wb="w_branch_b", wo="w_out", g2="ffn2_w_gate", u2="ffn2_w_up", d2="ffn2_w_down")
    for key in groups["f2"] + groups["mx"]:
        big_update(big_names[key], reduced[key])
    big_update("ada_w", g_ada_w)

    small_rows = [("norm_ffn1", 0, 1), ("norm_mix", 1, 1), ("mix_b_in", 2, 6), ("sgu_ln_g", 8, 1), ("sgu_ln_b", 9, 1),
                  ("sgu_b_s", 10, 1), ("conv_b", 11, 1), ("conv_ln_g", 12, 1), ("conv_ln_b", 13, 1),
                  ("norm_ffn2", 14, 1), ("norm_final", 15, 1), ("ada_b", 16, N_MOD)]

    def pack_rows(which, label):
        pieces = []
        for nm, _, r in small_rows:
            t = given[nm][which]
            pieces.append(("wide", t) if r > 1 else ("rows", t.reshape(1, D)))
        return _pack(f"pack_{label}", HALO, pieces + [("zeros", HALO - 16 - N_MOD)])

    sd, sm, sv = _adamw("adamw_small", pack_rows(0, "w"), tot, pack_rows(1, "m"), pack_rows(2, "v"), HALO)
    for nm, r0, r in small_rows:
        shp = given[nm][0].shape
        out[nm] = tuple(t[r0:r0 + r].reshape(shp) for t in (tot, sd, sm, sv))
    n_cw = conv_w.shape[2]
    g_cw = lax.dynamic_slice(tot_early, (0, q * n_cw), (HALO, n_cw))
    pad_cw = lambda t: jnp.pad(t[0], ((0, 1), (0, 0)))
    cd, cm, cv_ = _adamw("adamw_conv_w", pad_cw(conv_w), g_cw, pad_cw(m_conv_w), pad_cw(v_conv_w), HALO)
    out["conv_w"] = tuple(t[:KCONV][None] for t in (g_cw, cd, cm, cv_))
    g_ws = jnp.transpose(tot_early[HALO:].reshape(CHUNK, HEADS, CHUNK), (1, 0, 2)).reshape(HEADS * CHUNK, CHUNK)
    flat_ws = lambda t: t.reshape(HEADS * CHUNK, CHUNK)
    wd_, wm_, wv_ = _adamw("adamw_sgu_w_s", flat_ws(sgu_w_s), g_ws, flat_ws(m_sgu_w_s), flat_ws(v_sgu_w_s), 512)
    out["sgu_w_s"] = tuple(t.reshape(sgu_w_s.shape) for t in (g_ws, wd_, wm_, wv_))

    reduced.update(zip(groups["f1"], _rs_finish("f1", rs_f1, wd_, qc_idx)))
    for key in groups["f1"]:
        big_update(big_names[key], reduced[key])

    order = ["ada_w", "ada_b", "norm_ffn1", "ffn1_w_gate", "ffn1_w_up", "ffn1_w_down", "norm_mix", "mix_w_in", "mix_b_in",
             "sgu_ln_g", "sgu_ln_b", "sgu_w_s", "sgu_b_s", "conv_w", "conv_b", "conv_ln_g", "conv_ln_b", "w_branch_a",
             "w_branch_b", "w_out", "norm_ffn2", "ffn2_w_gate", "ffn2_w_up", "ffn2_w_down", "norm_final"]
    return (loss, grad_x[None], *[out[n][0] for n in order], *[out[n][1] for n in order],
            *[out[n][2] for n in order], *[out[n][3] for n in order])
```

```python
import functools

import jax
import jax.numpy as jnp
from jax import lax
from jax.experimental import pallas as pl
from jax.experimental.pallas import tpu as pltpu

F32 = jnp.float32
BF16 = jnp.bfloat16
D = 1024
HEADS = 8
CHUNK = 128
KCONV = 31
HALO = 32
N_MOD = 9
EPS = 1e-6
NCHIP = 4
VMEM_LIMIT = 56 * 1024 * 1024
MESH = pl.DeviceIdType.MESH

ADAM_LR = 0.001
ADAM_B1 = 0.9
ADAM_B2 = 0.999
ADAM_EPS = 1e-08
ADAM_WD = 0.01
ADAM_STEP = 10

NN = (((1,), (0,)), ((), ()))
NT = (((1,), (1,)), ((), ()))
TN = (((0,), (0,)), ((), ()))


def _cparams(*sem):
    return pltpu.CompilerParams(dimension_semantics=sem or None, vmem_limit_bytes=VMEM_LIMIT)


def _rms(x, w):
    return x * lax.rsqrt(jnp.mean(x * x, axis=-1, keepdims=True) + EPS) * w


def _ln(x, g, b):
    mu = jnp.mean(x, axis=-1, keepdims=True)
    xc = x - mu
    var = jnp.mean(xc * xc, axis=-1, keepdims=True)
    return xc * lax.rsqrt(var + EPS) * g + b


def _silu(x):
    return x * jax.nn.sigmoid(x)


def _gelu(x):
    return x * (0.5 * (1.0 + jnp.tanh(0.7978845608028654 * (x + 0.044715 * (x * x * x)))))


def _f_pre(x, nw, sc, sh):
    return _rms(x, nw) * (1.0 + sc) + sh


def _f_pre_keep(x, nw, sc, sh):
    return x, _f_pre(x, nw, sc, sh)


def _f_res_pre(scale):
    def fn(x, f, g, nw, sc, sh):
        xn = x + (scale * g) * f
        return xn, _f_pre(xn, nw, sc, sh)
    return fn


def _f_glu(a, b):
    return _silu(a) * b


def _f_merge(gab, ya, yb, bgab):
    g = gab + bgab
    return jax.nn.sigmoid(g[:, :D]) * ya + jax.nn.sigmoid(g[:, D:]) * yb


def _f_lnsilu(z, g, b):
    return _silu(_ln(z, g, b))


def _f_mixa(uv, buv, g, b):
    t = uv + buv
    return _gelu(t[:, :D]), _ln(_gelu(t[:, D:]), g, b)


def _col(t):
    return t if isinstance(t, tuple) else (t, t.shape[-1], 0)


def _rowwise_fwd(name, fn, rows, params, outs, tr):
    rows = [_col(r) for r in rows]
    params = [_col(p) for p in params]
    n_rows = rows[0][0].shape[0]
    nin = len(rows) + len(params)

    def body(*refs):
        res = fn(*[r[...].astype(F32) for r in refs[:nin]])
        res = res if isinstance(res, tuple) else (res,)
        for o, v in zip(refs[nin:], res, strict=True):
            o[...] = v.astype(o.dtype)

    return pl.pallas_call(
        body, name=name, grid=(n_rows // tr,),
        in_specs=[pl.BlockSpec((tr, w), functools.partial(lambda cb, i: (i, cb), cb)) for _, w, cb in rows]
        + [pl.BlockSpec((1, w), functools.partial(lambda cb, i: (0, cb), cb)) for _, w, cb in params],
        out_specs=[pl.BlockSpec((tr, w), lambda i: (i, 0)) for w, _ in outs],
        out_shape=[jax.ShapeDtypeStruct((n_rows, w), dt) for w, dt in outs],
        compiler_params=_cparams("parallel"),
    )(*[r[0] for r in rows], *[p[0] for p in params])


def _rowwise_bwd(name, fn, rows, params, cots, row_grads, tr):
    rows = [_col(r) for r in rows]
    params = [_col(p) for p in params]
    cots = [_col(t) for t in cots]
    n_rows = rows[0][0].shape[0]
    nr, npar, nc = len(rows), len(params), len(cots)
    nin = nr + npar + nc
    n_rg = sum(dt is not None for dt in row_grads)

    def body(*refs):
        i = pl.program_id(0)
        prim = [r[...].astype(F32) for r in refs[:nr + npar]]
        ct = tuple(r[...].astype(F32) for r in refs[nr + npar:nin])
        _, vjp = jax.vjp(fn, *prim)
        g = vjp(ct if nc > 1 else ct[0])
        outs = refs[nin:]
        oi = 0
        for j, dt in enumerate(row_grads):
            if dt is not None:
                outs[oi][...] = g[j].astype(dt)
                oi += 1
        for j in range(npar):
            acc = outs[n_rg + j]

            @pl.when(i == 0)
            def _(acc=acc):
                acc[...] = jnp.zeros(acc.shape, F32)

            acc[...] += g[nr + j]

    res = pl.pallas_call(
        body, name=name, grid=(n_rows // tr,),
        in_specs=[pl.BlockSpec((tr, w), functools.partial(lambda cb, i: (i, cb), cb)) for _, w, cb in rows]
        + [pl.BlockSpec((1, w), functools.partial(lambda cb, i: (0, cb), cb)) for _, w, cb in params]
        + [pl.BlockSpec((tr, w), functools.partial(lambda cb, i: (i, cb), cb)) for _, w, cb in cots],
        out_specs=[pl.BlockSpec((tr, rows[j][1]), lambda i: (i, 0)) for j, dt in enumerate(row_grads) if dt is not None]
        + [pl.BlockSpec((1, w), lambda i: (0, 0)) for _, w, _ in params],
        out_shape=[jax.ShapeDtypeStruct((n_rows, rows[j][1]), dt) for j, dt in enumerate(row_grads) if dt is not None]
        + [jax.ShapeDtypeStruct((1, w), F32) for _, w, _ in params],
        compiler_params=_cparams("arbitrary"),
    )(*[r[0] for r in rows], *[p[0] for p in params], *[t[0] for t in cots])
    return res[:n_rg], res[n_rg:]


def _final(name, x, f, tgt, g, nw, tr):
    n_rows = x.shape[0]

    def body(x_ref, f_ref, t_ref, g_ref, nw_ref, dx_ref, df_ref, dg_ref, dnw_ref, loss_ref):
        i = pl.program_id(0)
        tg = t_ref[...]

        def fn(xv, fv, gv, nwv):
            e = _rms(xv + (0.5 * gv) * fv, nwv) - tg
            return 0.5 * jnp.mean(e * e, axis=-1, keepdims=True)

        per_row, vjp = jax.vjp(fn, x_ref[...], f_ref[...], g_ref[...], nw_ref[...])
        dx, df, dg, dnw = vjp(jnp.ones_like(per_row))
        dx_ref[...] = dx
        df_ref[...] = df.astype(df_ref.dtype)

        @pl.when(i == 0)
        def _():
            dg_ref[...] = jnp.zeros(dg_ref.shape, F32)
            dnw_ref[...] = jnp.zeros(dnw_ref.shape, F32)
            loss_ref[...] = jnp.zeros(loss_ref.shape, F32)

        dg_ref[...] += dg
        dnw_ref[...] += dnw
        loss_ref[...] += jnp.broadcast_to(jnp.sum(per_row, axis=0, keepdims=True), loss_ref.shape)

    row = pl.BlockSpec((tr, D), lambda i: (i, 0))
    par = pl.BlockSpec((1, D), lambda i: (0, 0))
    return pl.pallas_call(
        body, name=name, grid=(n_rows // tr,),
        in_specs=[row, row, row, par, par],
        out_specs=[row, row, par, par, pl.BlockSpec((8, 128), lambda i: (0, 0))],
        out_shape=[jax.ShapeDtypeStruct((n_rows, D), F32), jax.ShapeDtypeStruct((n_rows, D), BF16),
                   jax.ShapeDtypeStruct((1, D), F32), jax.ShapeDtypeStruct((1, D), F32),
                   jax.ShapeDtypeStruct((8, 128), F32)],
        compiler_params=_cparams("arbitrary"),
    )(x, f, tgt, g, nw)


def _tile(dim, pref):
    t = min(dim, pref)
    while dim % t:
        t //= 2
    return t


def _matmul(name, a, b, kind, col, out_dtype, add=None):
    if kind == "wgrad":
        m, kg = a.shape
        ng = b.shape[1]
        r, c = (kg, ng // NCHIP) if col else (kg // NCHIP, ng)
        tm, tn, tk = _tile(r, 1024), _tile(c, 1536), _tile(m, 2048)
        grid = (kg // tm, ng // tn, m // tk)
        a_spec = pl.BlockSpec((tk, tm), lambda i, j, k: (k, i))
        b_spec = pl.BlockSpec((tk, tn), lambda i, j, k: (k, j))
        if col:
            o_spec = pl.BlockSpec((None, tm, tn), lambda i, j, k: (j // (c // tn), i, j % (c // tn)))
        else:
            o_spec = pl.BlockSpec((None, tm, tn), lambda i, j, k: (i // (r // tm), i % (r // tm), j))
        out_shape = jax.ShapeDtypeStruct((NCHIP, r, c), F32)
        dims = TN
    else:
        _, r, c = b.shape
        m = a.shape[0]
        kg, ng = (r, NCHIP * c) if col else (NCHIP * r, c)
        tm = _tile(m, 1024)
        if kind == "fwd":
            tn, tk = _tile(c, 1536), _tile(r, 2048)
            grid = (m // tm, ng // tn, kg // tk)
            a_spec = pl.BlockSpec((tm, tk), lambda i, j, k: (i, k))
            if col:
                b_spec = pl.BlockSpec((None, tk, tn), lambda i, j, k: (j // (c // tn), k, j % (c // tn)))
            else:
                b_spec = pl.BlockSpec((None, tk, tn), lambda i, j, k: (k // (r // tk), k % (r // tk), j))
            out_shape = jax.ShapeDtypeStruct((m, ng), out_dtype)
            dims = NN
        else:
            tn, tk = _tile(r, 1024), _tile(c, 2048)
            grid = (m // tm, kg // tn, ng // tk)
            a_spec = pl.BlockSpec((tm, tk), lambda i, j, k: (i, k))
            if col:
                b_spec = pl.BlockSpec((None, tn, tk), lambda i, j, k: (k // (c // tk), j, k % (c // tk)))
            else:
                b_spec = pl.BlockSpec((None, tn, tk), lambda i, j, k: (j // (r // tn), j % (r // tn), k))
            out_shape = jax.ShapeDtypeStruct((m, kg), out_dtype)
            dims = NT
        o_spec = pl.BlockSpec((tm, tn), lambda i, j, k: (i, j))
    nk = grid[2]
    has_add = add is not None

    def body(*refs):
        a_ref, b_ref = refs[0], refs[1]
        o_ref = refs[3] if has_add else refs[2]
        k = pl.program_id(2)
        p = lax.dot_general(a_ref[...].astype(BF16), b_ref[...].astype(BF16), dims, preferred_element_type=F32)

        def finish(r_):
            if has_add:
                r_ = r_ + refs[2][...]
            o_ref[...] = r_.astype(o_ref.dtype)

        if nk == 1:
            finish(p)
        else:
            acc_ref = refs[-1]

            @pl.when(k == 0)
            def _():
                acc_ref[...] = p

            @pl.when(jnp.logical_and(k > 0, k < nk - 1))
            def _():
                acc_ref[...] += p

            @pl.when(k == nk - 1)
            def _():
                finish(acc_ref[...] + p)

    in_specs = [a_spec, b_spec] + ([pl.BlockSpec((tm, tn), lambda i, j, k: (i, j))] if has_add else [])
    args = (a, b) + ((add,) if has_add else ())
    return pl.pallas_call(
        body, name=name, grid=grid, in_specs=in_specs, out_specs=o_spec, out_shape=out_shape,
        scratch_shapes=[pltpu.VMEM((tm, tn), F32)] if nk > 1 else [],
        compiler_params=_cparams("parallel", "parallel", "arbitrary"),
    )(*args)


def _ffn_up(name, h, wg, wu):
    m, k = h.shape
    _, _, c = wg.shape
    n = NCHIP * c
    tm, tn = _tile(m, 1024), _tile(c, 1024)

    def body(h_ref, wg_ref, wu_ref, a_ref, b_ref, s_ref):
        hv = h_ref[...]
        a = jnp.dot(hv, wg_ref[...], preferred_element_type=F32)
        b = jnp.dot(hv, wu_ref[...], preferred_element_type=F32)
        a_ref[...] = a
        b_ref[...] = b
        s_ref[...] = (_silu(a) * b).astype(s_ref.dtype)

    w_spec = pl.BlockSpec((None, k, tn), lambda i, j: (j // (c // tn), 0, j % (c // tn)))
    o_spec = pl.BlockSpec((tm, tn), lambda i, j: (i, j))
    return pl.pallas_call(
        body, name=name, grid=(m // tm, n // tn),
        in_specs=[pl.BlockSpec((tm, k), lambda i, j: (i, 0)), w_spec, w_spec],
        out_specs=[o_spec, o_spec, o_spec],
        out_shape=[jax.ShapeDtypeStruct((m, n), F32), jax.ShapeDtypeStruct((m, n), F32),
                   jax.ShapeDtypeStruct((m, n), BF16)],
        compiler_params=_cparams("parallel", "parallel"),
    )(h, wg, wu)


def _dgrad_glu(name, df, wd, a, b, after=None):
    m = df.shape[0]
    _, r, c = wd.shape
    tm, tn = _tile(m, 512), _tile(r, 1024)
    extra = [] if after is None else [after]

    def body(df_ref, w_ref, a_ref, b_ref, *rest):
        da_ref, db_ref = rest[-2], rest[-1]
        ds = lax.dot_general(df_ref[...], w_ref[...], NT, preferred_element_type=F32)
        _, vjp = jax.vjp(_f_glu, a_ref[...], b_ref[...])
        da, db = vjp(ds)
        da_ref[...] = da.astype(da_ref.dtype)
        db_ref[...] = db.astype(db_ref.dtype)

    t_spec = pl.BlockSpec((tm, tn), lambda i, j: (i, j))
    sds = jax.ShapeDtypeStruct((m, NCHIP * r), BF16)
    return pl.pallas_call(
        body, name=name, grid=(m // tm, NCHIP * r // tn),
        in_specs=[pl.BlockSpec((tm, c), lambda i, j: (i, 0)),
                  pl.BlockSpec((None, tn, c), lambda i, j: (j // (r // tn), j % (r // tn), 0)), t_spec, t_spec]
        + [pl.BlockSpec(memory_space=pl.ANY)] * len(extra),
        out_specs=[t_spec, t_spec], out_shape=[sds, sds],
        compiler_params=_cparams("parallel", "parallel"),
    )(df, wd, a, b, *extra)


def _sgu_fwd(name, proj, b_in, lng, lnb, wm, bst, tr):
    n_rows = proj.shape[0]

    def body(uv_ref, buv_ref, g_ref, b_ref, wm_ref, bst_ref, ya_ref):
        u, vln = _f_mixa(uv_ref[...], buv_ref[...], g_ref[...], b_ref[...])
        vb = vln.astype(BF16)
        for ck in range(tr // CHUNK):
            rs = slice(ck * CHUNK, (ck + 1) * CHUNK)
            for h in range(HEADS):
                cs = slice(h * CHUNK, (h + 1) * CHUNK)
                vm = jnp.dot(wm_ref[h], vb[rs, cs], preferred_element_type=F32) + bst_ref[:, cs]
                ya_ref[rs, cs] = (u[rs, cs] * vm).astype(ya_ref.dtype)

    return pl.pallas_call(
        body, name=name, grid=(n_rows // tr,),
        in_specs=[pl.BlockSpec((tr, 2 * D), lambda i: (i, 0)), pl.BlockSpec((1, 2 * D), lambda i: (0, 0)),
                  pl.BlockSpec((1, D), lambda i: (0, 0)), pl.BlockSpec((1, D), lambda i: (0, 0)),
                  pl.BlockSpec((HEADS, CHUNK, CHUNK), lambda i: (0, 0, 0)),
                  pl.BlockSpec((CHUNK, D), lambda i: (0, 0))],
        out_specs=pl.BlockSpec((tr, D), lambda i: (i, 0)),
        out_shape=jax.ShapeDtypeStruct((n_rows, D), BF16),
        compiler_params=_cparams("parallel"),
    )(proj, b_in, lng, lnb, wm, bst)


def _sgu_bwd(name, proj, b_in, lng, lnb, wm, wmt, bst, dya, tr):
    n_rows = proj.shape[0]

    def body(uv_ref, buv_ref, g_ref, b_ref, wm_ref, wmt_ref, bst_ref, dya_ref,
             duv_ref, dbuv_ref, dg_ref, db_ref, dws_ref, dbs_ref, du_s, dvln_s):
        i = pl.program_id(0)

        @pl.when(i == 0)
        def _():
            dbuv_ref[...] = jnp.zeros(dbuv_ref.shape, F32)
            dg_ref[...] = jnp.zeros(dg_ref.shape, F32)
            db_ref[...] = jnp.zeros(db_ref.shape, F32)
            dws_ref[...] = jnp.zeros(dws_ref.shape, F32)
            dbs_ref[...] = jnp.zeros(dbs_ref.shape, F32)

        (u, vln), vjp = jax.vjp(_f_mixa, uv_ref[...], buv_ref[...], g_ref[...], b_ref[...])
        vb = vln.astype(BF16)
        dya_v = dya_ref[...].astype(F32)
        tpos = lax.broadcasted_iota(jnp.int32, (CHUNK, CHUNK), 0)
        spos = lax.broadcasted_iota(jnp.int32, (CHUNK, CHUNK), 1)
        causal = (tpos >= spos).astype(F32)
        for ck in range(tr // CHUNK):
            rs = slice(ck * CHUNK, (ck + 1) * CHUNK)
            for h in range(HEADS):
                cs = slice(h * CHUNK, (h + 1) * CHUNK)
                vbh = vb[rs, cs]
                vm = jnp.dot(wm_ref[h], vbh, preferred_element_type=F32) + bst_ref[:, cs]
                dyb = dya_v[rs, cs]
                du_s[rs, cs] = dyb * vm
                dvm = dyb * u[rs, cs]
                dvmb = dvm.astype(BF16)
                dvln_s[rs, cs] = jnp.dot(wmt_ref[h], dvmb, preferred_element_type=F32)
                dws_ref[h] += causal * lax.dot_general(dvmb, vbh, NT, preferred_element_type=F32)
                dbs_ref[:, cs] += jnp.broadcast_to(jnp.sum(dvm, axis=1, keepdims=True), (CHUNK, CHUNK))
        duv, dbuv, dg, db = vjp((du_s[...], dvln_s[...]))
        duv_ref[...] = duv.astype(duv_ref.dtype)
        dbuv_ref[...] += dbuv
        dg_ref[...] += dg
        db_ref[...] += db

    par = pl.BlockSpec((1, D), lambda i: (0, 0))
    par2 = pl.BlockSpec((1, 2 * D), lambda i: (0, 0))
    w_spec = pl.BlockSpec((HEADS, CHUNK, CHUNK), lambda i: (0, 0, 0))
    b_spec = pl.BlockSpec((CHUNK, D), lambda i: (0, 0))
    return pl.pallas_call(
        body, name=name, grid=(n_rows // tr,),
        in_specs=[pl.BlockSpec((tr, 2 * D), lambda i: (i, 0)), par2, par, par, w_spec, w_spec, b_spec,
                  pl.BlockSpec((tr, D), lambda i: (i, 0))],
        out_specs=[pl.BlockSpec((tr, 2 * D), lambda i: (i, 0)), par2, par, par, w_spec, b_spec],
        out_shape=[jax.ShapeDtypeStruct((n_rows, 2 * D), BF16), jax.ShapeDtypeStruct((1, 2 * D), F32),
                   jax.ShapeDtypeStruct((1, D), F32), jax.ShapeDtypeStruct((1, D), F32),
                   jax.ShapeDtypeStruct((HEADS, CHUNK, CHUNK), F32), jax.ShapeDtypeStruct((CHUNK, D), F32)],
        scratch_shapes=[pltpu.VMEM((tr, D), F32), pltpu.VMEM((tr, D), F32)],
        compiler_params=_cparams("arbitrary"),
    )(proj, b_in, lng, lnb, wm, wmt, bst, dya)


CT = 256
RB = 128
CV0 = 2 * D // CT
CG0 = 3 * D // CT


def _shifted(win, sh):
    if sh == 0:
        return win[0:RB]
    return pltpu.roll(win, RB + HALO - sh, 0)[0:RB]


def _conv_fwd(name, proj, b_in, w, cb):
    n_rows = proj.shape[0]

    def body(cv_ref, cg_ref, bcv_ref, bcg_ref, w_ref, cb_ref, z1_ref, zp_ref):
        zp_ref[0:HALO, :] = jnp.zeros((HALO, CT), F32)
        zp_ref[HALO:, :] = (cv_ref[...] + bcv_ref[...]) * jax.nn.sigmoid(cg_ref[...] + bcg_ref[...])

        def blk(rb, carry):
            base = pl.multiple_of(rb * RB, RB)
            win = zp_ref[pl.ds(base, RB + HALO), :]
            acc = jnp.broadcast_to(cb_ref[...], (RB, CT))
            for k in range(KCONV):
                acc = acc + w_ref[k:k + 1, :] * _shifted(win, k + 2)
            z1_ref[pl.ds(base, RB), :] = acc
            return carry

        lax.fori_loop(0, n_rows // RB, blk, 0)

    return pl.pallas_call(
        body, name=name, grid=(D // CT,),
        in_specs=[pl.BlockSpec((n_rows, CT), lambda j: (0, CV0 + j)), pl.BlockSpec((n_rows, CT), lambda j: (0, CG0 + j)),
                  pl.BlockSpec((1, CT), lambda j: (0, CV0 + j)), pl.BlockSpec((1, CT), lambda j: (0, CG0 + j)),
                  pl.BlockSpec((HALO, CT), lambda j: (0, j)), pl.BlockSpec((1, CT), lambda j: (0, j))],
        out_specs=pl.BlockSpec((n_rows, CT), lambda j: (0, j)),
        out_shape=jax.ShapeDtypeStruct((n_rows, D), F32),
        scratch_shapes=[pltpu.VMEM((n_rows + HALO, CT), F32)],
        compiler_params=_cparams("parallel"),
    )(proj, proj, b_in, b_in, w, cb)


def _conv_bwd(name, dz1, proj, b_in, w):
    n_rows = proj.shape[0]

    def body(dz_ref, cv_ref, cg_ref, bcv_ref, bcg_ref, w_ref, dcv_ref, dcg_ref, sm_ref, zp_ref, dzp_ref):
        cvb = cv_ref[...] + bcv_ref[...]
        sg = jax.nn.sigmoid(cg_ref[...] + bcg_ref[...])
        zp_ref[0:HALO, :] = jnp.zeros((HALO, CT), F32)
        zp_ref[HALO:, :] = cvb * sg
        dz = dz_ref[...]
        dzp_ref[0:n_rows, :] = dz
        dzp_ref[n_rows:, :] = jnp.zeros((HALO, CT), F32)
        sm_ref[...] = jnp.zeros(sm_ref.shape, F32)
        sm_ref[32:33, :] = jnp.sum(dz, axis=0, keepdims=True)

        def blk(rb, carry):
            base = pl.multiple_of(rb * RB, RB)
            dwin = dzp_ref[pl.ds(base, RB + HALO), :]
            zwin = zp_ref[pl.ds(base, RB + HALO), :]
            dzb = dwin[0:RB]
            acc = jnp.zeros((RB, CT), F32)
            for k in range(KCONV):
                acc = acc + w_ref[k:k + 1, :] * _shifted(dwin, KCONV - 1 - k)
                sm_ref[k:k + 1, :] += jnp.sum(dzb * _shifted(zwin, k + 2), axis=0, keepdims=True)
            dzp_ref[pl.ds(base, RB), :] = acc
            return carry

        lax.fori_loop(0, n_rows // RB, blk, 0)
        dz0 = dzp_ref[0:n_rows, :]
        dcv = dz0 * sg
        dcg = dz0 * cvb * (sg * (1.0 - sg))
        dcv_ref[...] = dcv.astype(dcv_ref.dtype)
        dcg_ref[...] = dcg.astype(dcg_ref.dtype)
        sm_ref[33:34, :] = jnp.sum(dcv, axis=0, keepdims=True)
        sm_ref[34:35, :] = jnp.sum(dcg, axis=0, keepdims=True)

    col = pl.BlockSpec((n_rows, CT), lambda j: (0, j))
    return pl.pallas_call(
        body, name=name, grid=(D // CT,),
        in_specs=[col, pl.BlockSpec((n_rows, CT), lambda j: (0, CV0 + j)), pl.BlockSpec((n_rows, CT), lambda j: (0, CG0 + j)),
                  pl.BlockSpec((1, CT), lambda j: (0, CV0 + j)), pl.BlockSpec((1, CT), lambda j: (0, CG0 + j)),
                  pl.BlockSpec((HALO, CT), lambda j: (0, j))],
        out_specs=[col, col, pl.BlockSpec((40, CT), lambda j: (0, j))],
        out_shape=[jax.ShapeDtypeStruct((n_rows, D), BF16), jax.ShapeDtypeStruct((n_rows, D), BF16),
                   jax.ShapeDtypeStruct((40, D), F32)],
        scratch_shapes=[pltpu.VMEM((n_rows + HALO, CT), F32), pltpu.VMEM((n_rows + HALO, CT), F32)],
        compiler_params=_cparams("parallel"),
    )(dz1, proj, proj, b_in, b_in, w)


ADA_TN = 768


def _split_bf16(v):
    hi = v.astype(BF16)
    return hi, (v - hi.astype(F32)).astype(BF16)


def _ada_fwd(name, c_all, w, b):
    n = w.shape[1]

    def body(c_ref, w_ref, b_ref, o_ref):
        ch, cl = _split_bf16(_silu(c_ref[...]))
        wh, wl = _split_bf16(w_ref[...])
        acc = jnp.dot(ch, wl, preferred_element_type=F32) + jnp.dot(cl, wh, preferred_element_type=F32)
        o_ref[...] = acc + jnp.dot(ch, wh, preferred_element_type=F32) + b_ref[...]

    return pl.pallas_call(
        body, name=name, grid=(n // ADA_TN,),
        in_specs=[pl.BlockSpec((8, D), lambda j: (0, 0)), pl.BlockSpec((D, ADA_TN), lambda j: (0, j)),
                  pl.BlockSpec((1, ADA_TN), lambda j: (0, j))],
        out_specs=pl.BlockSpec((8, ADA_TN), lambda j: (0, j)),
        out_shape=jax.ShapeDtypeStruct((8, n), F32),
        compiler_params=_cparams("parallel"),
    )(c_all, w, b)


def _ada_wgrad(name, c_all_t, dmod):
    n = dmod.shape[1]

    def body(ct_ref, dm_ref, o_ref):
        ca = _silu(ct_ref[...])
        acc = ca[:, 0:1] * dm_ref[0:1, :]
        for r in range(1, 8):
            acc = acc + ca[:, r:r + 1] * dm_ref[r:r + 1, :]
        o_ref[...] = acc

    return pl.pallas_call(
        body, name=name, grid=(n // ADA_TN,),
        in_specs=[pl.BlockSpec((D, 8), lambda j: (0, 0)), pl.BlockSpec((8, ADA_TN), lambda j: (0, j))],
        out_specs=pl.BlockSpec((D, ADA_TN), lambda j: (0, j)),
        out_shape=jax.ShapeDtypeStruct((D, n), F32),
        compiler_params=_cparams("parallel"),
    )(c_all_t, dmod)


def _adamw(name, w, g, m, v, tr):
    rows, cols = w.shape

    def body(w_ref, g_ref, m_ref, v_ref, g_out, d_ref, nm_ref, nv_ref):
        gv = g_ref[...]
        nm = ADAM_B1 * m_ref[...] + (1.0 - ADAM_B1) * gv
        nv = ADAM_B2 * v_ref[...] + (1.0 - ADAM_B2) * (gv * gv)
        m_hat = nm / (1.0 - ADAM_B1 ** ADAM_STEP)
        v_hat = nv / (1.0 - ADAM_B2 ** ADAM_STEP)
        d_ref[...] = -ADAM_LR * (m_hat / (jnp.sqrt(v_hat) + ADAM_EPS) + ADAM_WD * w_ref[...])
        nm_ref[...] = nm
        nv_ref[...] = nv
        g_out[...] = gv

    spec = pl.BlockSpec((tr, cols), lambda i: (i, 0))
    sds = jax.ShapeDtypeStruct((rows, cols), F32)
    return pl.pallas_call(
        body, name=name, grid=(rows // tr,), in_specs=[spec] * 4, out_specs=[spec] * 4, out_shape=[sds] * 4,
        compiler_params=_cparams("parallel"),
    )(w, g, m, v)


def _sum8(name, stacked, tr):
    n, rows, cols = stacked.shape

    def body(s_ref, o_ref):
        acc = s_ref[0]
        for r in range(1, n):
            acc = acc + s_ref[r]
        o_ref[...] = acc

    return pl.pallas_call(
        body, name=name, grid=(rows // tr,),
        in_specs=[pl.BlockSpec((n, tr, cols), lambda i: (0, i, 0))],
        out_specs=pl.BlockSpec((tr, cols), lambda i: (i, 0)),
        out_shape=jax.ShapeDtypeStruct((rows, cols), F32),
        compiler_params=_cparams("parallel"),
    )(stacked)


def _pair_add(name, g5, other, c_idx, tr):
    nq, _, rows, cols = g5.shape

    def body(c_ref, g_ref, o_ref, p_ref):
        p_ref[...] = (g_ref[...] + o_ref[...]).astype(p_ref.dtype)

    return pl.pallas_call(
        body, name=name,
        grid_spec=pltpu.PrefetchScalarGridSpec(
            num_scalar_prefetch=1, grid=(nq, rows // tr),
            in_specs=[pl.BlockSpec((None, None, tr, cols), lambda qi, i, cr: (qi, cr[0], i, 0)),
                      pl.BlockSpec((None, tr, cols), lambda qi, i, cr: (qi, i, 0))],
            out_specs=pl.BlockSpec((None, tr, cols), lambda qi, i, cr: (qi, i, 0))),
        out_shape=jax.ShapeDtypeStruct((nq, rows, cols), BF16),
        compiler_params=_cparams("parallel", "parallel"),
    )(c_idx, g5, other)


def _chip_add(name, p, recv, qc_idx, tr):
    _, rows, cols = p.shape

    def body(qc_ref, p_ref, r_ref, o_ref):
        acc = p_ref[...].astype(F32)
        for k in range(NCHIP - 1):
            acc = acc + r_ref[k].astype(F32)
        o_ref[...] = acc

    return pl.pallas_call(
        body, name=name,
        grid_spec=pltpu.PrefetchScalarGridSpec(
            num_scalar_prefetch=1, grid=(rows // tr,),
            in_specs=[pl.BlockSpec((None, tr, cols), lambda i, qc: (qc[0], i, 0)),
                      pl.BlockSpec((NCHIP - 1, tr, cols), lambda i, qc: (0, i, 0))],
            out_specs=pl.BlockSpec((None, tr, cols), lambda i, qc: (qc[1], i, 0))),
        out_shape=jax.ShapeDtypeStruct((2, rows, cols), F32),
        compiler_params=_cparams("parallel"),
    )(qc_idx, p, recv)


def _allgather8(name, blk):
    m_per, n = blk.shape

    def body(x_ref, out_ref, send_sems, recv_sems, local_sem):
        x, y, c = lax.axis_index("x"), lax.axis_index("y"), lax.axis_index("c")
        me, sibling = (x, y, c), (x, y, 1 - c)
        chips = [(1 - x, y), (x, 1 - y), (1 - x, 1 - y)]

        def rows(px, py, pc):
            return out_ref.at[pl.ds((4 * px + 2 * py + pc) * m_per, m_per), :]

        def copy(k, block, to, src=None):
            return pltpu.make_async_remote_copy(
                src_ref=rows(*block) if src is None else src, dst_ref=rows(*block),
                send_sem=send_sems.at[k], recv_sem=recv_sems.at[k], device_id=to, device_id_type=MESH)

        mine = pltpu.make_async_copy(x_ref, rows(*me), local_sem)
        mine.start()
        first = [copy(0, me, sibling, src=x_ref)]
        first += [copy(1 + j, me, (*chip, c), src=x_ref) for j, chip in enumerate(chips)]
        for cp in first:
            cp.start()
        passed = [copy(4 + j, (*chip, c), sibling) for j, chip in enumerate(chips)]
        for j, chip in enumerate(chips):
            copy(1 + j, (*chip, c), me).wait_recv()
            passed[j].start()
        copy(0, sibling, me).wait_recv()
        for j, chip in enumerate(chips):
            copy(4 + j, (*chip, 1 - c), me).wait_recv()
        for cp in first + passed:
            cp.wait_send()
        mine.wait()

    return pl.pallas_call(
        body, name=name,
        out_shape=jax.ShapeDtypeStruct((8 * m_per, n), blk.dtype),
        in_specs=[pl.BlockSpec(memory_space=pltpu.VMEM)],
        out_specs=pl.BlockSpec(memory_space=pltpu.VMEM),
        scratch_shapes=[pltpu.SemaphoreType.DMA((7,)), pltpu.SemaphoreType.DMA((7,)), pltpu.SemaphoreType.DMA],
        compiler_params=pltpu.CompilerParams(vmem_limit_bytes=VMEM_LIMIT),
    )(blk)


def _exchange(name, srcs, out_shapes, n_local, n_remote, plan, aliases=None, after=None):
    ni, no = len(srcs), len(out_shapes)
    extra = [] if after is None else [after]

    def body(*refs):
        ins, outs = refs[:ni], refs[ni + len(extra):ni + len(extra) + no]
        send_sems, recv_sems, local_sems = refs[ni + len(extra) + no:]
        x, y, c = lax.axis_index("x"), lax.axis_index("y"), lax.axis_index("c")
        local, remote = plan(ins, outs, x, y, c)
        assert len(local) == n_local and len(remote) == n_remote

        def rcopy(i, dst):
            s, _, peer, _, _ = remote[i]
            return pltpu.make_async_remote_copy(src_ref=s, dst_ref=dst, send_sem=send_sems.at[i],
                                                recv_sem=recv_sems.at[i], device_id=peer, device_id_type=MESH)

        lcs = [pltpu.make_async_copy(s, d, local_sems.at[i]) for i, (s, d) in enumerate(local)]
        for cp in lcs:
            cp.start()
        first = [i for i in range(n_remote) if remote[i][4] is None]
        passed = [i for i in range(n_remote) if remote[i][4] is not None]
        for i in first:
            rcopy(i, remote[i][1]).start()
        arrived = set()
        for i in passed:
            j = remote[i][4]
            rcopy(j, remote[j][3]).wait_recv()
            arrived.add(j)
            rcopy(i, remote[i][1]).start()
        for i in range(n_remote):
            if i not in arrived:
                rcopy(i, remote[i][3]).wait_recv()
        for i in range(n_remote):
            rcopy(i, remote[i][1]).wait_send()
        for cp in lcs:
            cp.wait()

    any_spec = pl.BlockSpec(memory_space=pl.ANY)
    return pl.pallas_call(
        body, name=name, out_shape=out_shapes,
        in_specs=[any_spec] * (ni + len(extra)), out_specs=[any_spec] * no,
        input_output_aliases=aliases or {},
        scratch_shapes=[pltpu.SemaphoreType.DMA((n_remote,)), pltpu.SemaphoreType.DMA((n_remote,)),
                        pltpu.SemaphoreType.DMA((max(n_local, 1),))],
    )(*srcs, *extra)


_CHIP_FLIPS = ((0, 1), (1, 0), (1, 1))


def _flip(v, f):
    return 1 - v if f else v


EFFECT = pltpu.SideEffectType.DATAFLOW_SIDE_EFFECTING


def _split_start(name, bufs, n, plan, after=None):
    nb = len(bufs)
    extra = [] if after is None else [after]

    def body(*refs):
        ins = refs[:nb]
        send_sems, recv_sems = refs[nb + len(extra)], refs[nb + len(extra) + 1]
        token = refs[-1]
        x, y, c = lax.axis_index("x"), lax.axis_index("y"), lax.axis_index("c")
        copies = plan(ins, x, y, c)
        assert len(copies) == n
        for i, (s, d, peer, _) in enumerate(copies):
            pltpu.make_async_remote_copy(src_ref=s, dst_ref=d, send_sem=send_sems.at[i], recv_sem=recv_sems.at[i],
                                         device_id=peer, device_id_type=MESH).start()
        token[...] = jnp.zeros_like(token)

    hbm = pl.BlockSpec(memory_space=pltpu.HBM)
    sem = pl.BlockSpec(memory_space=pltpu.SEMAPHORE)
    res = pl.pallas_call(
        body, name=name,
        out_shape=(pltpu.SemaphoreType.DMA((n,)), pltpu.SemaphoreType.DMA((n,)),
                   *[pltpu.HBM(b.shape, b.dtype) for b in bufs], jax.ShapeDtypeStruct((8, 128), F32)),
        in_specs=[hbm] * nb + [pl.BlockSpec(memory_space=pl.ANY)] * len(extra),
        out_specs=(sem, sem, *[hbm] * nb, pl.BlockSpec(memory_space=pltpu.VMEM)),
        input_output_aliases={i: 2 + i for i in range(nb)},
        compiler_params=pltpu.CompilerParams(has_side_effects=EFFECT),
    )(*[pltpu.with_memory_space_constraint(b, pltpu.HBM) for b in bufs], *extra)
    return res[0], res[1], list(res[2:2 + nb]), res[-1]


def _split_wait(name, bufs, send_sems, recv_sems, after, n, plan):
    nb = len(bufs)

    def body(*refs):
        ins = refs[:nb]
        ssem, rsem = refs[nb], refs[nb + 1]
        x, y, c = lax.axis_index("x"), lax.axis_index("y"), lax.axis_index("c")
        copies = plan(ins, x, y, c)
        assert len(copies) == n
        for i, (s, _, peer, lands) in enumerate(copies):
            cp = pltpu.make_async_remote_copy(src_ref=s, dst_ref=lands, send_sem=ssem.at[i], recv_sem=rsem.at[i],
                                              device_id=peer, device_id_type=MESH)
            cp.wait_send()
            cp.wait_recv()

    hbm = pl.BlockSpec(memory_space=pltpu.HBM)
    sem = pl.BlockSpec(memory_space=pltpu.SEMAPHORE)
    afters = list(after) if isinstance(after, (list, tuple)) else [after]
    res = pl.pallas_call(
        body, name=name,
        out_shape=[pltpu.HBM(b.shape, b.dtype) for b in bufs],
        in_specs=[hbm] * nb + [sem, sem] + [pl.BlockSpec(memory_space=pl.ANY)] * len(afters),
        out_specs=[hbm] * nb,
        input_output_aliases={i: i for i in range(nb)},
        compiler_params=pltpu.CompilerParams(has_side_effects=EFFECT),
    )(*bufs, send_sems, recv_sems, *afters)
    return list(res)


def _chips_of(x, y):
    return [(_flip(x, fx), _flip(y, fy)) for fx, fy in _CHIP_FLIPS]


def _gather_start(tag, stacks, after):
    n = len(stacks)

    def plan(refs, x, y, c):
        q = 2 * x + y
        return [(refs[a].at[q, c], refs[a].at[q, c], (px, py, c), refs[a].at[2 * px + py, c])
                for a in range(n) for px, py in _chips_of(x, y)]

    send, recv, thru, token = _split_start(f"gather_start_{tag}", stacks, 3 * n, plan, after)
    return (send, recv, thru, plan), token


def _gather_finish(tag, state, after):
    send, recv, thru, plan = state
    n = len(thru)
    landed = _split_wait(f"gather_wait_{tag}", thru, send, recv, after, 3 * n, plan)

    def pass_on(ins, outs, x, y, c):
        sib = (x, y, 1 - c)
        remote = []
        for a in range(n):
            for px, py in _chips_of(x, y):
                p = 2 * px + py
                remote.append((ins[a].at[p, c], outs[a].at[p, c], sib, outs[a].at[p, 1 - c], None))
        return [], remote

    full = _exchange(f"gather_pass_{tag}", landed, [jax.ShapeDtypeStruct(s.shape, s.dtype) for s in landed],
                     0, 3 * n, pass_on, aliases={a: a for a in range(n)})
    return [g.reshape(NCHIP, 2 * g.shape[2], g.shape[3]) for g in full]


def _gather_small(name, smalls):
    n = len(smalls)

    def plan(ins, outs, x, y, c):
        q = 2 * x + y
        local = [(ins[b], outs[b].at[q]) for b in range(n)]
        remote = [(ins[b], outs[b].at[q], (px, py, c), outs[b].at[2 * px + py], None)
                  for b in range(n) for px, py in _chips_of(x, y)]
        return local, remote

    return _exchange(name, smalls, [jax.ShapeDtypeStruct((NCHIP,) + s.shape, s.dtype) for s in smalls], n, 3 * n, plan)


def _rs_pair_start(tag, grads, after=None):
    n = len(grads)
    g5 = [g.reshape(NCHIP, 2, g.shape[1] // 2, g.shape[2]) for g in grads]
    half = [(g.shape[2], g.shape[3]) for g in g5]
    land = [lax.empty((NCHIP,) + h, F32) for h in half]

    def plan(refs, x, y, c):
        sib = (x, y, 1 - c)
        return [(refs[a].at[:, 1 - c], refs[n + a], sib, refs[n + a]) for a in range(n)]

    send, recv, thru, token = _split_start(f"rs_pair_start_{tag}", g5 + land, n, plan, after)
    return (send, recv, thru, plan, half), token


def _rs_chips_start(tag, state, after, c_idx):
    send, recv, thru, pair_plan, half = state
    n = len(half)
    landed = _split_wait(f"rs_pair_wait_{tag}", thru, send, recv, after, n, pair_plan)
    g5, got = landed[:n], landed[n:]
    part = [_pair_add(f"rs_pair_add_{tag}{a}", g5[a], got[a], c_idx, _tile(half[a][0], 512)) for a in range(n)]
    land = [lax.empty((NCHIP - 1,) + h, BF16) for h in half]

    def plan(refs, x, y, c):
        return [(refs[a].at[2 * px + py], refs[n + a].at[k], (px, py, c), refs[n + a].at[k])
                for a in range(n) for k, (px, py) in enumerate(_chips_of(x, y))]

    send, recv, thru, token = _split_start(f"rs_chips_start_{tag}", part + land, 3 * n, plan)
    return (send, recv, thru, plan, half), token


def _rs_finish(tag, state, after, qc_idx):
    send, recv, thru, plan, half = state
    n = len(half)
    landed = _split_wait(f"rs_chips_wait_{tag}", thru, send, recv, after, 3 * n, plan)
    red = [_chip_add(f"rs_chip_add_{tag}{a}", landed[a], landed[n + a], qc_idx, _tile(half[a][0], 512)) for a in range(n)]

    def share(ins, outs, x, y, c):
        sib = (x, y, 1 - c)
        return [], [(ins[a].at[c], outs[a].at[c], sib, outs[a].at[1 - c], None) for a in range(n)]

    full = _exchange(f"rs_share_{tag}", red, [jax.ShapeDtypeStruct((2,) + h, F32) for h in half], 0, n, share,
                     aliases={a: a for a in range(n)})
    return [f.reshape(2 * h[0], h[1]) for f, h in zip(full, half)]


def _pack(name, n_rows, pieces, after=None):
    arrays = [p[1] for p in pieces if p[0] != "zeros"]
    extra = [] if after is None else [after]

    def body(*refs):
        o_ref = refs[-1]
        o_ref[...] = jnp.zeros(o_ref.shape, F32)
        row, ai = 0, 0
        for p in pieces:
            kind = p[0]
            if kind == "zeros":
                row += p[1]
                continue
            ref = refs[len(extra) + ai]
            ai += 1
            if kind == "rows":
                r = ref.shape[0]
                o_ref[row:row + r, :] = ref[...]
            elif kind == "wide":
                r = ref.shape[1] // D
                for j in range(r):
                    o_ref[row + j:row + j + 1, :] = ref[:, j * D:(j + 1) * D]
            elif kind == "slice":
                r = p[3] - p[2]
                o_ref[row:row + r, :] = ref[p[2]:p[3], :]
            elif kind == "heads":
                r = CHUNK
                for h in range(HEADS):
                    o_ref[row:row + r, h * CHUNK:(h + 1) * CHUNK] = ref[h]
            else:
                r = 1
                o_ref[row:row + 1, 0:128] = ref[0:1, :]
            row += r
        assert row == n_rows, (row, n_rows)

    vmem = pl.BlockSpec(memory_space=pltpu.VMEM)
    return pl.pallas_call(
        body, name=name, out_shape=jax.ShapeDtypeStruct((n_rows, D), F32),
        in_specs=[pl.BlockSpec(memory_space=pl.ANY)] * len(extra) + [vmem] * len(arrays), out_specs=vmem,
        compiler_params=pltpu.CompilerParams(vmem_limit_bytes=VMEM_LIMIT),
    )(*extra, *arrays)


def _to_bf16_stack(name, w, q_idx):
    r, cols = w.shape
    half = r // 2
    tr = _tile(half, 256)
    nb = half // tr

    def body(q_ref, w_ref, o_ref):
        o_ref[...] = w_ref[...].astype(BF16)

    return pl.pallas_call(
        body, name=name,
        grid_spec=pltpu.PrefetchScalarGridSpec(
            num_scalar_prefetch=1, grid=(2 * nb,),
            in_specs=[pl.BlockSpec((tr, cols), lambda i, qr: (i, 0))],
            out_specs=pl.BlockSpec((None, None, tr, cols), lambda i, qr: (qr[0], i // nb, i % nb, 0))),
        out_shape=jax.ShapeDtypeStruct((NCHIP, 2, half, cols), BF16),
        compiler_params=_cparams("parallel"),
    )(q_idx, w)


def kernel(x, c, ada_w, ada_b, norm_ffn1, ffn1_w_gate, ffn1_w_up, ffn1_w_down, norm_mix, mix_w_in, mix_b_in, sgu_ln_g, sgu_ln_b, sgu_w_s, sgu_b_s, conv_w, conv_b, conv_ln_g, conv_ln_b, w_branch_a, w_branch_b, w_out, norm_ffn2, ffn2_w_gate, ffn2_w_up, ffn2_w_down, norm_final, loss_target, m_ada_w, m_ada_b, m_norm_ffn1, m_ffn1_w_gate, m_ffn1_w_up, m_ffn1_w_down, m_norm_mix, m_mix_w_in, m_mix_b_in, m_sgu_ln_g, m_sgu_ln_b, m_sgu_w_s, m_sgu_b_s, m_conv_w, m_conv_b, m_conv_ln_g, m_conv_ln_b, m_w_branch_a, m_w_branch_b, m_w_out, m_norm_ffn2, m_ffn2_w_gate, m_ffn2_w_up, m_ffn2_w_down, m_norm_final, v_ada_w, v_ada_b, v_norm_ffn1, v_ffn1_w_gate, v_ffn1_w_up, v_ffn1_w_down, v_norm_mix, v_mix_w_in, v_mix_b_in, v_sgu_ln_g, v_sgu_ln_b, v_sgu_w_s, v_sgu_b_s, v_conv_w, v_conv_b, v_conv_ln_g, v_conv_ln_b, v_w_branch_a, v_w_branch_b, v_w_out, v_norm_ffn2, v_ffn2_w_gate, v_ffn2_w_up, v_ffn2_w_down, v_norm_final):
    xi, yi, ci = lax.axis_index("x"), lax.axis_index("y"), lax.axis_index("c")
    q = 2 * xi + yi
    dev = 4 * xi + 2 * yi + ci
    c_idx = jnp.reshape(ci, (1,)).astype(jnp.int32)
    q_idx = jnp.reshape(q, (1,)).astype(jnp.int32)
    x0 = x[0]
    tgt = loss_target[0]
    qc_idx = jnp.stack([q, ci]).astype(jnp.int32)
    n_ada = ada_w.shape[2]

    c_all = _allgather8("ag_c", jnp.pad(c, ((0, 7), (0, 0))))[::8]
    ada_b_mine = lax.dynamic_slice(ada_b, (0, q * n_ada), (1, n_ada))
    mod_part = _ada_fwd("ada_fwd", c_all, ada_w[0], ada_b_mine)
    small_parts = _gather_small("gather_small", [mod_part, jnp.pad(conv_w[0], ((0, 1), (0, 0)))])

    big = dict(g1=ffn1_w_gate, u1=ffn1_w_up, d1=ffn1_w_down, win=mix_w_in, wa=w_branch_a, wb=w_branch_b, wo=w_out,
               g2=ffn2_w_gate, u2=ffn2_w_up, d2=ffn2_w_down)
    groups = dict(f1=("d1", "g1", "u1"), mx=("wo", "wa", "wb", "win"), f2=("d2", "g2", "u2"))
    gather_groups = dict(up1=("g1", "u1"), dn1=("d1",), win=("win",), mx=("wa", "wb", "wo"), f2=("g2", "u2", "d2"))
    gather_state = {}
    started = jnp.zeros((1, 1), F32)
    for tag, keys in gather_groups.items():
        gather_state[tag], token = _gather_start(
            tag, [_to_bf16_stack(f"cast_{k}", big[k][0], q_idx) for k in keys], small_parts[0])
        started = started + token[0:1, 0:1]

    mod_all = jnp.transpose(small_parts[0], (1, 0, 2)).reshape(8, N_MOD * D)
    mod = lax.dynamic_slice(mod_all, (dev, 0), (1, N_MOD * D)) + started
    sh1, sc1, g1, sh2, sc2, g2, sh3, sc3, g3 = [mod[:, k * D:(k + 1) * D] for k in range(N_MOD)]
    convw = jnp.transpose(small_parts[1], (1, 0, 2)).reshape(HALO, D)
    W = dict(zip(gather_groups["up1"], _gather_finish("up1", gather_state["up1"], mod)))

    causal = jnp.tril(jnp.ones((CHUNK, CHUNK), bool))
    wm_f = jnp.where(causal[None], sgu_w_s[0], 0.0)
    wm = wm_f.astype(BF16)
    wmt = jnp.swapaxes(wm_f, 1, 2).astype(BF16)
    bst = jnp.repeat(jnp.transpose(sgu_b_s[0]), CHUNK, axis=1)
    b_in = mix_b_in

    (h1,) = _rowwise_fwd("pre1", _f_pre, [x0], [norm_ffn1, sc1, sh1], [(D, BF16)], 256)
    a1, b1, s1 = _ffn_up("up1", h1, W["g1"], W["u1"])
    W.update(zip(gather_groups["dn1"], _gather_finish("dn1", gather_state["dn1"], s1)))
    f1 = _matmul("down1", s1, W["d1"], "fwd", False, F32)
    x1, h2 = _rowwise_fwd("res1", _f_res_pre(0.5), [x0, f1], [g1, norm_mix, sc2, sh2], [(D, F32), (D, BF16)], 256)
    W.update(zip(gather_groups["win"], _gather_finish("win", gather_state["win"], x1)))
    proj = _matmul("in_proj", h2, W["win"], "fwd", True, F32)
    W.update(zip(gather_groups["mx"], _gather_finish("mx", gather_state["mx"], proj)))
    ya = _sgu_fwd("sgu", proj, b_in, sgu_ln_g, sgu_ln_b, wm, bst, 256)
    z1 = _conv_fwd("conv", proj, b_in, convw, conv_b)
    (z3,) = _rowwise_fwd("conv_ln", _f_lnsilu, [z1], [conv_ln_g, conv_ln_b], [(D, BF16)], 256)
    y_a = _matmul("branch_a", ya, W["wa"], "fwd", False, F32)
    y_b = _matmul("branch_b", z3, W["wb"], "fwd", False, F32)
    (merged,) = _rowwise_fwd("merge", _f_merge, [(proj, 2 * D, 2), y_a, y_b], [(b_in, 2 * D, 2)], [(D, BF16)], 256)
    y = _matmul("out_proj", merged, W["wo"], "fwd", False, F32)
    x2, h3 = _rowwise_fwd("res2", _f_res_pre(1.0), [x1, y], [g2, norm_ffn2, sc3, sh3], [(D, F32), (D, BF16)], 256)
    W.update(zip(gather_groups["f2"], _gather_finish("f2", gather_state["f2"], x2)))
    a3, b3, s3 = _ffn_up("up2", h3, W["g2"], W["u2"])
    f3 = _matmul("down2", s3, W["d2"], "fwd", False, F32)
    dx2, df3, dg3, dnwf, loss_blk = _final("final", x2, f3, tgt, g3, norm_final.reshape(1, D), 256)

    G = {}

    def ffn_bwd(tag, df, s, a, b, h, wg, wu, wd, after=None):
        da, db = _dgrad_glu(f"d_down{tag}", df, wd, a, b, after)
        G["d" + tag] = _matmul(f"w_down{tag}", s, df, "wgrad", False, F32)
        dh = _matmul(f"d_gate{tag}", da, wg, "dgrad", True, F32)
        dh = _matmul(f"d_up{tag}", db, wu, "dgrad", True, F32, add=dh)
        G["g" + tag] = _matmul(f"w_gate{tag}", h, da, "wgrad", True, F32)
        G["u" + tag] = _matmul(f"w_up{tag}", h, db, "wgrad", True, F32)
        return dh

    reduced = {}
    dh3 = ffn_bwd("2", df3, s3, a3, b3, h3, W["g2"], W["u2"], W["d2"])
    pair_f2, token = _rs_pair_start("f2", [G[k] for k in groups["f2"]])
    (dx1, dy), (dg2, dnw3, dsc3, dsh3) = _rowwise_bwd(
        "res2_b", _f_res_pre(1.0), [x1, y], [g2 + token[0:1, 0:1], norm_ffn2, sc3, sh3], [dx2, dh3], [F32, BF16], 256)
    dmerged = _matmul("d_out", dy, W["wo"], "dgrad", False, F32)
    G["wo"] = _matmul("w_out", merged, dy, "wgrad", False, F32)
    rs_f2, token = _rs_chips_start("f2", pair_f2, [dmerged, G["wo"]], c_idx)
    (dgab, dy_a, dy_b), (dbgab,) = _rowwise_bwd(
        "merge_b", _f_merge, [(proj, 2 * D, 2), y_a, y_b], [(b_in + token[0:1, 0:1], 2 * D, 2)], [dmerged],
        [BF16, BF16, BF16], 256)
    dya = _matmul("d_a", dy_a, W["wa"], "dgrad", False, F32)
    dz3 = _matmul("d_b", dy_b, W["wb"], "dgrad", False, F32)
    G["wa"] = _matmul("w_a", ya, dy_a, "wgrad", False, F32)
    G["wb"] = _matmul("w_b", z3, dy_b, "wgrad", False, F32)
    (dz1,), (dclg, dclb) = _rowwise_bwd("conv_ln_b", _f_lnsilu, [z1], [conv_ln_g, conv_ln_b], [dz3], [F32], 256)
    dcv, dcg, conv_small = _conv_bwd("conv_b", dz1, proj, b_in, convw)
    duv, dbuv, dlng, dlnb, dws, dbs = _sgu_bwd("sgu_b", proj, b_in, sgu_ln_g, sgu_ln_b, wm, wmt, bst, dya, 256)
    dproj = jnp.concatenate([duv, dcv, dcg, dgab], axis=1)
    dh2 = _matmul("d_in", dproj, W["win"], "dgrad", True, F32)
    G["win"] = _matmul("w_in", h2, dproj, "wgrad", True, F32)
    pair_mx, token = _rs_pair_start("mx", [G[k] for k in groups["mx"]])
    reduced.update(zip(groups["f2"], _rs_finish("f2", rs_f2, [dh2, token], qc_idx)))

    early = _pack("pack_early", HALO + CHUNK, [("slice", conv_small, 0, HALO), ("heads", dws)], after=reduced["d2"])
    every_early = _allgather8("ag_early", early)
    tot_early = _sum8("sum_early", every_early.reshape(8, HALO + CHUNK, D), HALO)

    (dx0, df1), (dg1, dnw2, dsc2, dsh2) = _rowwise_bwd(
        "res1_b", _f_res_pre(0.5), [x0, f1], [g1 + token[0:1, 0:1], norm_mix, sc2, sh2], [dx1, dh2], [F32, BF16], 256)
    rs_mx, token = _rs_chips_start("mx", pair_mx, [df1, every_early], c_idx)
    dh1 = ffn_bwd("1", df1, s1, a1, b1, h1, W["g1"], W["u1"], W["d1"], after=token)
    pair_f1, token = _rs_pair_start("f1", [G[k] for k in groups["f1"]])
    reduced.update(zip(groups["mx"], _rs_finish("mx", rs_mx, [dh1, token], qc_idx)))
    (grad_x,), (dnw1, dsc1, dsh1) = _rowwise_bwd(
        "pre1_b", _f_pre_keep, [x0], [norm_ffn1, sc1 + token[0:1, 0:1], sh1], [dx0, dh1], [F32], 256)

    dbs_row = jnp.transpose(dbs[:, ::CHUNK]).reshape(1, D)
    small = _pack("pack_small", HALO, [
        ("rows", dnw1), ("rows", dnw2), ("wide", dbuv), ("slice", conv_small, 33, 34), ("slice", conv_small, 34, 35),
        ("wide", dbgab), ("rows", dlng), ("rows", dlnb), ("rows", dbs_row), ("slice", conv_small, 32, 33),
        ("rows", dclg), ("rows", dclb), ("rows", dnw3), ("rows", dnwf),
        ("rows", dsh1), ("rows", dsc1), ("rows", dg1), ("rows", dsh2), ("rows", dsc2), ("rows", dg2),
        ("rows", dsh3), ("rows", dsc3), ("rows", dg3), ("lanes", loss_blk), ("zeros", 6)])
    every = _allgather8("ag_small", small)
    rs_f1, token = _rs_chips_start("f1", pair_f1, every, c_idx)
    every = (every + token[0:1, 0:1]).reshape(8, HALO, D)
    tot = _sum8("sum_small", every, HALO)
    loss = tot[25, 0]
    dmod_all = every[:, 16:16 + N_MOD, :].reshape(8, N_MOD * D)
    dmod_mine = lax.dynamic_slice(dmod_all, (0, q * n_ada), (8, n_ada))
    g_ada_w = _ada_wgrad("ada_wgrad", jnp.transpose(c_all), dmod_mine)

    given = dict(ada_w=(ada_w, m_ada_w, v_ada_w), ada_b=(ada_b, m_ada_b, v_ada_b),
                 norm_ffn1=(norm_ffn1, m_norm_ffn1, v_norm_ffn1),
                 ffn1_w_gate=(ffn1_w_gate, m_ffn1_w_gate, v_ffn1_w_gate), ffn1_w_up=(ffn1_w_up, m_ffn1_w_up, v_ffn1_w_up),
                 ffn1_w_down=(ffn1_w_down, m_ffn1_w_down, v_ffn1_w_down), norm_mix=(norm_mix, m_norm_mix, v_norm_mix),
                 mix_w_in=(mix_w_in, m_mix_w_in, v_mix_w_in), mix_b_in=(mix_b_in, m_mix_b_in, v_mix_b_in),
                 sgu_ln_g=(sgu_ln_g, m_sgu_ln_g, v_sgu_ln_g), sgu_ln_b=(sgu_ln_b, m_sgu_ln_b, v_sgu_ln_b),
                 sgu_w_s=(sgu_w_s, m_sgu_w_s, v_sgu_w_s), sgu_b_s=(sgu_b_s, m_sgu_b_s, v_sgu_b_s),
                 conv_w=(conv_w, m_conv_w, v_conv_w), conv_b=(conv_b, m_conv_b, v_conv_b),
                 conv_ln_g=(conv_ln_g, m_conv_ln_g, v_conv_ln_g), conv_ln_b=(conv_ln_b, m_conv_ln_b, v_conv_ln_b),
                 w_branch_a=(w_branch_a, m_w_branch_a, v_w_branch_a), w_branch_b=(w_branch_b, m_w_branch_b, v_w_branch_b),
                 w_out=(w_out, m_w_out, v_w_out), norm_ffn2=(norm_ffn2, m_norm_ffn2, v_norm_ffn2),
                 ffn2_w_gate=(ffn2_w_gate, m_ffn2_w_gate, v_ffn2_w_gate), ffn2_w_up=(ffn2_w_up, m_ffn2_w_up, v_ffn2_w_up),
                 ffn2_w_down=(ffn2_w_down, m_ffn2_w_down, v_ffn2_w_down), norm_final=(norm_final, m_norm_final, v_norm_final))
    out = {}

    def big_update(name, g2d):
        w, m, v = given[name]
        shp = w.shape
        res = _adamw(f"adamw_{name}", w.reshape(g2d.shape), g2d, m.reshape(g2d.shape), v.reshape(g2d.shape),
                     _tile(g2d.shape[0], 512 if g2d.shape[1] <= D else 256))
        out[name] = tuple(t.reshape(shp) for t in res)

    big_names = dict(g1="ffn1_w_gate", u1="ffn1_w_up", d1="ffn1_w_down", win="mix_w_in", wa="w_branch_a",
                     wb="w_branch_b", wo="w_out", g2="ffn2_w_gate", u2="ffn2_w_up", d2="ffn2_w_down")
    for key in groups["f2"] + groups["mx"]:
        big_update(big_names[key], reduced[key])
    big_update("ada_w", g_ada_w)

    small_rows = [("norm_ffn1", 0, 1), ("norm_mix", 1, 1), ("mix_b_in", 2, 6), ("sgu_ln_g", 8, 1), ("sgu_ln_b", 9, 1),
                  ("sgu_b_s", 10, 1), ("conv_b", 11, 1), ("conv_ln_g", 12, 1), ("conv_ln_b", 13, 1),
                  ("norm_ffn2", 14, 1), ("norm_final", 15, 1), ("ada_b", 16, N_MOD)]

    def pack_rows(which, label):
        pieces = []
        for nm, _, r in small_rows:
            t = given[nm][which]
            pieces.append(("wide", t) if r > 1 else ("rows", t.reshape(1, D)))
        return _pack(f"pack_{label}", HALO, pieces + [("zeros", HALO - 16 - N_MOD)])

    small_res = _adamw("adamw_small", pack_rows(0, "w"), tot, pack_rows(1, "m"), pack_rows(2, "v"), HALO)
    for nm, r0, r in small_rows:
        shp = given[nm][0].shape
        out[nm] = tuple(t[r0:r0 + r].reshape(shp) for t in small_res)
    n_cw = conv_w.shape[2]
    g_cw = lax.dynamic_slice(tot_early, (0, q * n_cw), (HALO, n_cw))
    pad_cw = lambda t: jnp.pad(t[0], ((0, 1), (0, 0)))
    cw_res = _adamw("adamw_conv_w", pad_cw(conv_w), g_cw, pad_cw(m_conv_w), pad_cw(v_conv_w), HALO)
    out["conv_w"] = tuple(t[:KCONV][None] for t in cw_res)
    g_ws = jnp.transpose(tot_early[HALO:].reshape(CHUNK, HEADS, CHUNK), (1, 0, 2)).reshape(HEADS * CHUNK, CHUNK)
    flat_ws = lambda t: t.reshape(HEADS * CHUNK, CHUNK)
    ws_res = _adamw("adamw_sgu_w_s", flat_ws(sgu_w_s), g_ws, flat_ws(m_sgu_w_s), flat_ws(v_sgu_w_s), 512)
    out["sgu_w_s"] = tuple(t.reshape(sgu_w_s.shape) for t in ws_res)

    done = [out[n][1] for n in ("ada_w", "sgu_w_s", "conv_w", "ada_b")] + [out[big_names[k]][1] for k in groups["f2"] + groups["mx"]]
    reduced.update(zip(groups["f1"], _rs_finish("f1", rs_f1, done, qc_idx)))
    for key in groups["f1"]:
        big_update(big_names[key], reduced[key])

    order = ["ada_w", "ada_b", "norm_ffn1", "ffn1_w_gate", "ffn1_w_up", "ffn1_w_down", "norm_mix", "mix_w_in", "mix_b_in",
             "sgu_ln_g", "sgu_ln_b", "sgu_w_s", "sgu_b_s", "conv_w", "conv_b", "conv_ln_g", "conv_ln_b", "w_branch_a",
             "w_branch_b", "w_out", "norm_ffn2", "ffn2_w_gate", "ffn2_w_up", "ffn2_w_down", "norm_final"]
    return (loss, grad_x[None], *[out[n][0] for n in order], *[out[n][1] for n in order],
            *[out[n][2] for n in order], *[out[n][3] for n in order])
```

```python
import functools

import jax
import jax.numpy as jnp
from jax import lax
from jax.experimental import pallas as pl
from jax.experimental.pallas import tpu as pltpu

F32 = jnp.float32
BF16 = jnp.bfloat16
D = 1024
HEADS = 8
CHUNK = 128
KCONV = 31
HALO = 32
N_MOD = 9
EPS = 1e-6
NCHIP = 4
VMEM_LIMIT = 56 * 1024 * 1024
MESH = pl.DeviceIdType.MESH

ADAM_LR = 0.001
ADAM_B1 = 0.9
ADAM_B2 = 0.999
ADAM_EPS = 1e-08
ADAM_WD = 0.01
ADAM_STEP = 10

NN = (((1,), (0,)), ((), ()))
NT = (((1,), (1,)), ((), ()))
TN = (((0,), (0,)), ((), ()))


def _cparams(*sem):
    return pltpu.CompilerParams(dimension_semantics=sem or None, vmem_limit_bytes=VMEM_LIMIT)


def _rms(x, w):
    return x * lax.rsqrt(jnp.mean(x * x, axis=-1, keepdims=True) + EPS) * w


def _ln(x, g, b):
    mu = jnp.mean(x, axis=-1, keepdims=True)
    xc = x - mu
    var = jnp.mean(xc * xc, axis=-1, keepdims=True)
    return xc * lax.rsqrt(var + EPS) * g + b


def _silu(x):
    return x * jax.nn.sigmoid(x)


def _gelu(x):
    return x * (0.5 * (1.0 + jnp.tanh(0.7978845608028654 * (x + 0.044715 * (x * x * x)))))


def _f_pre(x, nw, sc, sh):
    return _rms(x, nw) * (1.0 + sc) + sh


def _f_pre_keep(x, nw, sc, sh):
    return x, _f_pre(x, nw, sc, sh)


def _f_res_pre(scale):
    def fn(x, f, g, nw, sc, sh):
        xn = x + (scale * g) * f
        return xn, _f_pre(xn, nw, sc, sh)
    return fn


def _f_glu(a, b):
    return _silu(a) * b


def _f_merge(gab, ya, yb, bgab):
    g = gab + bgab
    return jax.nn.sigmoid(g[:, :D]) * ya + jax.nn.sigmoid(g[:, D:]) * yb


def _f_lnsilu(z, g, b):
    return _silu(_ln(z, g, b))


def _f_mixa(uv, buv, g, b):
    t = uv + buv
    return _gelu(t[:, :D]), _ln(_gelu(t[:, D:]), g, b)


def _col(t):
    return t if isinstance(t, tuple) else (t, t.shape[-1], 0)


def _rowwise_fwd(name, fn, rows, params, outs, tr):
    rows = [_col(r) for r in rows]
    params = [_col(p) for p in params]
    n_rows = rows[0][0].shape[0]
    nin = len(rows) + len(params)

    def body(*refs):
        res = fn(*[r[...].astype(F32) for r in refs[:nin]])
        res = res if isinstance(res, tuple) else (res,)
        for o, v in zip(refs[nin:], res, strict=True):
            o[...] = v.astype(o.dtype)

    return pl.pallas_call(
        body, name=name, grid=(n_rows // tr,),
        in_specs=[pl.BlockSpec((tr, w), functools.partial(lambda cb, i: (i, cb), cb)) for _, w, cb in rows]
        + [pl.BlockSpec((1, w), functools.partial(lambda cb, i: (0, cb), cb)) for _, w, cb in params],
        out_specs=[pl.BlockSpec((tr, w), lambda i: (i, 0)) for w, _ in outs],
        out_shape=[jax.ShapeDtypeStruct((n_rows, w), dt) for w, dt in outs],
        compiler_params=_cparams("parallel"),
    )(*[r[0] for r in rows], *[p[0] for p in params])


def _rowwise_bwd(name, fn, rows, params, cots, row_grads, tr):
    rows = [_col(r) for r in rows]
    params = [_col(p) for p in params]
    cots = [_col(t) for t in cots]
    n_rows = rows[0][0].shape[0]
    nr, npar, nc = len(rows), len(params), len(cots)
    nin = nr + npar + nc
    n_rg = sum(dt is not None for dt in row_grads)

    def body(*refs):
        i = pl.program_id(0)
        prim = [r[...].astype(F32) for r in refs[:nr + npar]]
        ct = tuple(r[...].astype(F32) for r in refs[nr + npar:nin])
        _, vjp = jax.vjp(fn, *prim)
        g = vjp(ct if nc > 1 else ct[0])
        outs = refs[nin:]
        oi = 0
        for j, dt in enumerate(row_grads):
            if dt is not None:
                outs[oi][...] = g[j].astype(dt)
                oi += 1
        for j in range(npar):
            acc = outs[n_rg + j]

            @pl.when(i == 0)
            def _(acc=acc):
                acc[...] = jnp.zeros(acc.shape, F32)

            acc[...] += g[nr + j]

    res = pl.pallas_call(
        body, name=name, grid=(n_rows // tr,),
        in_specs=[pl.BlockSpec((tr, w), functools.partial(lambda cb, i: (i, cb), cb)) for _, w, cb in rows]
        + [pl.BlockSpec((1, w), functools.partial(lambda cb, i: (0, cb), cb)) for _, w, cb in params]
        + [pl.BlockSpec((tr, w), functools.partial(lambda cb, i: (i, cb), cb)) for _, w, cb in cots],
        out_specs=[pl.BlockSpec((tr, rows[j][1]), lambda i: (i, 0)) for j, dt in enumerate(row_grads) if dt is not None]
        + [pl.BlockSpec((1, w), lambda i: (0, 0)) for _, w, _ in params],
        out_shape=[jax.ShapeDtypeStruct((n_rows, rows[j][1]), dt) for j, dt in enumerate(row_grads) if dt is not None]
        + [jax.ShapeDtypeStruct((1, w), F32) for _, w, _ in params],
        compiler_params=_cparams("arbitrary"),
    )(*[r[0] for r in rows], *[p[0] for p in params], *[t[0] for t in cots])
    return res[:n_rg], res[n_rg:]


def _final(name, x, f, tgt, g, nw, tr):
    n_rows = x.shape[0]

    def body(x_ref, f_ref, t_ref, g_ref, nw_ref, dx_ref, df_ref, dg_ref, dnw_ref, loss_ref):
        i = pl.program_id(0)
        tg = t_ref[...]

        def fn(xv, fv, gv, nwv):
            e = _rms(xv + (0.5 * gv) * fv, nwv) - tg
            return 0.5 * jnp.mean(e * e, axis=-1, keepdims=True)

        per_row, vjp = jax.vjp(fn, x_ref[...], f_ref[...], g_ref[...], nw_ref[...])
        dx, df, dg, dnw = vjp(jnp.ones_like(per_row))
        dx_ref[...] = dx
        df_ref[...] = df.astype(df_ref.dtype)

        @pl.when(i == 0)
        def _():
            dg_ref[...] = jnp.zeros(dg_ref.shape, F32)
            dnw_ref[...] = jnp.zeros(dnw_ref.shape, F32)
            loss_ref[...] = jnp.zeros(loss_ref.shape, F32)

        dg_ref[...] += dg
        dnw_ref[...] += dnw
        loss_ref[...] += jnp.broadcast_to(jnp.sum(per_row, axis=0, keepdims=True), loss_ref.shape)

    row = pl.BlockSpec((tr, D), lambda i: (i, 0))
    par = pl.BlockSpec((1, D), lambda i: (0, 0))
    return pl.pallas_call(
        body, name=name, grid=(n_rows // tr,),
        in_specs=[row, row, row, par, par],
        out_specs=[row, row, par, par, pl.BlockSpec((8, 128), lambda i: (0, 0))],
        out_shape=[jax.ShapeDtypeStruct((n_rows, D), F32), jax.ShapeDtypeStruct((n_rows, D), BF16),
                   jax.ShapeDtypeStruct((1, D), F32), jax.ShapeDtypeStruct((1, D), F32),
                   jax.ShapeDtypeStruct((8, 128), F32)],
        compiler_params=_cparams("arbitrary"),
    )(x, f, tgt, g, nw)


def _tile(dim, pref):
    t = min(dim, pref)
    while dim % t:
        t //= 2
    return t


def _matmul(name, a, b, kind, col, out_dtype, add=None):
    if kind == "wgrad":
        m, kg = a.shape
        ng = b.shape[1]
        r, c = (kg, ng // NCHIP) if col else (kg // NCHIP, ng)
        tm, tn, tk = _tile(r, 1024), _tile(c, 1536), _tile(m, 2048)
        grid = (kg // tm, ng // tn, m // tk)
        a_spec = pl.BlockSpec((tk, tm), lambda i, j, k: (k, i))
        b_spec = pl.BlockSpec((tk, tn), lambda i, j, k: (k, j))
        if col:
            o_spec = pl.BlockSpec((None, tm, tn), lambda i, j, k: (j // (c // tn), i, j % (c // tn)))
        else:
            o_spec = pl.BlockSpec((None, tm, tn), lambda i, j, k: (i // (r // tm), i % (r // tm), j))
        out_shape = jax.ShapeDtypeStruct((NCHIP, r, c), out_dtype)
        dims = TN
    else:
        _, r, c = b.shape
        m = a.shape[0]
        kg, ng = (r, NCHIP * c) if col else (NCHIP * r, c)
        tm = _tile(m, 1024)
        if kind == "fwd":
            tn, tk = _tile(c, 1536), _tile(r, 2048)
            grid = (m // tm, ng // tn, kg // tk)
            a_spec = pl.BlockSpec((tm, tk), lambda i, j, k: (i, k))
            if col:
                b_spec = pl.BlockSpec((None, tk, tn), lambda i, j, k: (j // (c // tn), k, j % (c // tn)))
            else:
                b_spec = pl.BlockSpec((None, tk, tn), lambda i, j, k: (k // (r // tk), k % (r // tk), j))
            out_shape = jax.ShapeDtypeStruct((m, ng), out_dtype)
            dims = NN
        else:
            tn, tk = _tile(r, 1024), _tile(c, 2048)
            grid = (m // tm, kg // tn, ng // tk)
            a_spec = pl.BlockSpec((tm, tk), lambda i, j, k: (i, k))
            if col:
                b_spec = pl.BlockSpec((None, tn, tk), lambda i, j, k: (k // (c // tk), j, k % (c // tk)))
            else:
                b_spec = pl.BlockSpec((None, tn, tk), lambda i, j, k: (j // (r // tn), j % (r // tn), k))
            out_shape = jax.ShapeDtypeStruct((m, kg), out_dtype)
            dims = NT
        o_spec = pl.BlockSpec((tm, tn), lambda i, j, k: (i, j))
    nk = grid[2]
    has_add = add is not None

    def body(*refs):
        a_ref, b_ref = refs[0], refs[1]
        o_ref = refs[3] if has_add else refs[2]
        k = pl.program_id(2)
        p = lax.dot_general(a_ref[...].astype(BF16), b_ref[...].astype(BF16), dims, preferred_element_type=F32)

        def finish(r_):
            if has_add:
                r_ = r_ + refs[2][...]
            o_ref[...] = r_.astype(o_ref.dtype)

        if nk == 1:
            finish(p)
        else:
            acc_ref = refs[-1]

            @pl.when(k == 0)
            def _():
                acc_ref[...] = p

            @pl.when(jnp.logical_and(k > 0, k < nk - 1))
            def _():
                acc_ref[...] += p

            @pl.when(k == nk - 1)
            def _():
                finish(acc_ref[...] + p)

    in_specs = [a_spec, b_spec] + ([pl.BlockSpec((tm, tn), lambda i, j, k: (i, j))] if has_add else [])
    args = (a, b) + ((add,) if has_add else ())
    return pl.pallas_call(
        body, name=name, grid=grid, in_specs=in_specs, out_specs=o_spec, out_shape=out_shape,
        scratch_shapes=[pltpu.VMEM((tm, tn), F32)] if nk > 1 else [],
        compiler_params=_cparams("parallel", "parallel", "arbitrary"),
    )(*args)


def _ffn_up(name, h, wg, wu):
    m, k = h.shape
    _, _, c = wg.shape
    n = NCHIP * c
    tm, tn = _tile(m, 1024), _tile(c, 1024)

    def body(h_ref, wg_ref, wu_ref, a_ref, b_ref, s_ref):
        hv = h_ref[...]
        a = jnp.dot(hv, wg_ref[...], preferred_element_type=F32)
        b = jnp.dot(hv, wu_ref[...], preferred_element_type=F32)
        a_ref[...] = a
        b_ref[...] = b
        s_ref[...] = (_silu(a) * b).astype(s_ref.dtype)

    w_spec = pl.BlockSpec((None, k, tn), lambda i, j: (j // (c // tn), 0, j % (c // tn)))
    o_spec = pl.BlockSpec((tm, tn), lambda i, j: (i, j))
    return pl.pallas_call(
        body, name=name, grid=(m // tm, n // tn),
        in_specs=[pl.BlockSpec((tm, k), lambda i, j: (i, 0)), w_spec, w_spec],
        out_specs=[o_spec, o_spec, o_spec],
        out_shape=[jax.ShapeDtypeStruct((m, n), F32), jax.ShapeDtypeStruct((m, n), F32),
                   jax.ShapeDtypeStruct((m, n), BF16)],
        compiler_params=_cparams("parallel", "parallel"),
    )(h, wg, wu)


def _dgrad_glu(name, df, wd, a, b, after=None):
    m = df.shape[0]
    _, r, c = wd.shape
    tm, tn = _tile(m, 512), _tile(r, 1024)
    extra = [] if after is None else [after]

    def body(df_ref, w_ref, a_ref, b_ref, *rest):
        da_ref, db_ref = rest[-2], rest[-1]
        ds = lax.dot_general(df_ref[...], w_ref[...], NT, preferred_element_type=F32)
        _, vjp = jax.vjp(_f_glu, a_ref[...], b_ref[...])
        da, db = vjp(ds)
        da_ref[...] = da.astype(da_ref.dtype)
        db_ref[...] = db.astype(db_ref.dtype)

    t_spec = pl.BlockSpec((tm, tn), lambda i, j: (i, j))
    sds = jax.ShapeDtypeStruct((m, NCHIP * r), BF16)
    return pl.pallas_call(
        body, name=name, grid=(m // tm, NCHIP * r // tn),
        in_specs=[pl.BlockSpec((tm, c), lambda i, j: (i, 0)),
                  pl.BlockSpec((None, tn, c), lambda i, j: (j // (r // tn), j % (r // tn), 0)), t_spec, t_spec]
        + [pl.BlockSpec(memory_space=pl.ANY)] * len(extra),
        out_specs=[t_spec, t_spec], out_shape=[sds, sds],
        compiler_params=_cparams("parallel", "parallel"),
    )(df, wd, a, b, *extra)


def _sgu_fwd(name, proj, b_in, lng, lnb, wm, bst, tr):
    n_rows = proj.shape[0]

    def body(uv_ref, buv_ref, g_ref, b_ref, wm_ref, bst_ref, ya_ref):
        u, vln = _f_mixa(uv_ref[...], buv_ref[...], g_ref[...], b_ref[...])
        vb = vln.astype(BF16)
        for ck in range(tr // CHUNK):
            rs = slice(ck * CHUNK, (ck + 1) * CHUNK)
            for h in range(HEADS):
                cs = slice(h * CHUNK, (h + 1) * CHUNK)
                vm = jnp.dot(wm_ref[h], vb[rs, cs], preferred_element_type=F32) + bst_ref[:, cs]
                ya_ref[rs, cs] = (u[rs, cs] * vm).astype(ya_ref.dtype)

    return pl.pallas_call(
        body, name=name, grid=(n_rows // tr,),
        in_specs=[pl.BlockSpec((tr, 2 * D), lambda i: (i, 0)), pl.BlockSpec((1, 2 * D), lambda i: (0, 0)),
                  pl.BlockSpec((1, D), lambda i: (0, 0)), pl.BlockSpec((1, D), lambda i: (0, 0)),
                  pl.BlockSpec((HEADS, CHUNK, CHUNK), lambda i: (0, 0, 0)),
                  pl.BlockSpec((CHUNK, D), lambda i: (0, 0))],
        out_specs=pl.BlockSpec((tr, D), lambda i: (i, 0)),
        out_shape=jax.ShapeDtypeStruct((n_rows, D), BF16),
        compiler_params=_cparams("parallel"),
    )(proj, b_in, lng, lnb, wm, bst)


def _sgu_bwd(name, proj, b_in, lng, lnb, wm, wmt, bst, dya, tr):
    n_rows = proj.shape[0]

    def body(uv_ref, buv_ref, g_ref, b_ref, wm_ref, wmt_ref, bst_ref, dya_ref,
             duv_ref, dbuv_ref, dg_ref, db_ref, dws_ref, dbs_ref, du_s, dvln_s):
        i = pl.program_id(0)

        @pl.when(i == 0)
        def _():
            dbuv_ref[...] = jnp.zeros(dbuv_ref.shape, F32)
            dg_ref[...] = jnp.zeros(dg_ref.shape, F32)
            db_ref[...] = jnp.zeros(db_ref.shape, F32)
            dws_ref[...] = jnp.zeros(dws_ref.shape, F32)
            dbs_ref[...] = jnp.zeros(dbs_ref.shape, F32)

        (u, vln), vjp = jax.vjp(_f_mixa, uv_ref[...], buv_ref[...], g_ref[...], b_ref[...])
        vb = vln.astype(BF16)
        dya_v = dya_ref[...].astype(F32)
        tpos = lax.broadcasted_iota(jnp.int32, (CHUNK, CHUNK), 0)
        spos = lax.broadcasted_iota(jnp.int32, (CHUNK, CHUNK), 1)
        causal = (tpos >= spos).astype(F32)
        for ck in range(tr // CHUNK):
            rs = slice(ck * CHUNK, (ck + 1) * CHUNK)
            for h in range(HEADS):
                cs = slice(h * CHUNK, (h + 1) * CHUNK)
                vbh = vb[rs, cs]
                vm = jnp.dot(wm_ref[h], vbh, preferred_element_type=F32) + bst_ref[:, cs]
                dyb = dya_v[rs, cs]
                du_s[rs, cs] = dyb * vm
                dvm = dyb * u[rs, cs]
                dvmb = dvm.astype(BF16)
                dvln_s[rs, cs] = jnp.dot(wmt_ref[h], dvmb, preferred_element_type=F32)
                dws_ref[h] += causal * lax.dot_general(dvmb, vbh, NT, preferred_element_type=F32)
                dbs_ref[:, cs] += jnp.broadcast_to(jnp.sum(dvm, axis=1, keepdims=True), (CHUNK, CHUNK))
        duv, dbuv, dg, db = vjp((du_s[...], dvln_s[...]))
        duv_ref[...] = duv.astype(duv_ref.dtype)
        dbuv_ref[...] += dbuv
        dg_ref[...] += dg
        db_ref[...] += db

    par = pl.BlockSpec((1, D), lambda i: (0, 0))
    par2 = pl.BlockSpec((1, 2 * D), lambda i: (0, 0))
    w_spec = pl.BlockSpec((HEADS, CHUNK, CHUNK), lambda i: (0, 0, 0))
    b_spec = pl.BlockSpec((CHUNK, D), lambda i: (0, 0))
    return pl.pallas_call(
        body, name=name, grid=(n_rows // tr,),
        in_specs=[pl.BlockSpec((tr, 2 * D), lambda i: (i, 0)), par2, par, par, w_spec, w_spec, b_spec,
                  pl.BlockSpec((tr, D), lambda i: (i, 0))],
        out_specs=[pl.BlockSpec((tr, 2 * D), lambda i: (i, 0)), par2, par, par, w_spec, b_spec],
        out_shape=[jax.ShapeDtypeStruct((n_rows, 2 * D), BF16), jax.ShapeDtypeStruct((1, 2 * D), F32),
                   jax.ShapeDtypeStruct((1, D), F32), jax.ShapeDtypeStruct((1, D), F32),
                   jax.ShapeDtypeStruct((HEADS, CHUNK, CHUNK), F32), jax.ShapeDtypeStruct((CHUNK, D), F32)],
        scratch_shapes=[pltpu.VMEM((tr, D), F32), pltpu.VMEM((tr, D), F32)],
        compiler_params=_cparams("arbitrary"),
    )(proj, b_in, lng, lnb, wm, wmt, bst, dya)


CT = 256
RB = 128
CV0 = 2 * D // CT
CG0 = 3 * D // CT


def _shift_bank(win):
    return [win] + [pltpu.roll(win, RB + HALO - r, 0) for r in range(1, 8)]


def _shifted(bank, sh):
    lo = 8 * (sh // 8)
    return bank[sh % 8][lo:lo + RB]


def _conv_fwd(name, proj, b_in, w, cb):
    n_rows = proj.shape[0]

    def body(cv_ref, cg_ref, bcv_ref, bcg_ref, w_ref, cb_ref, z1_ref, zp_ref):
        zp_ref[0:HALO, :] = jnp.zeros((HALO, CT), F32)
        zp_ref[HALO:, :] = (cv_ref[...] + bcv_ref[...]) * jax.nn.sigmoid(cg_ref[...] + bcg_ref[...])

        def blk(rb, carry):
            base = pl.multiple_of(rb * RB, RB)
            bank = _shift_bank(zp_ref[pl.ds(base, RB + HALO), :])
            acc = jnp.broadcast_to(cb_ref[...], (RB, CT))
            for k in range(KCONV):
                acc = acc + w_ref[k:k + 1, :] * _shifted(bank, k + 2)
            z1_ref[pl.ds(base, RB), :] = acc
            return carry

        lax.fori_loop(0, n_rows // RB, blk, 0)

    return pl.pallas_call(
        body, name=name, grid=(D // CT,),
        in_specs=[pl.BlockSpec((n_rows, CT), lambda j: (0, CV0 + j)), pl.BlockSpec((n_rows, CT), lambda j: (0, CG0 + j)),
                  pl.BlockSpec((1, CT), lambda j: (0, CV0 + j)), pl.BlockSpec((1, CT), lambda j: (0, CG0 + j)),
                  pl.BlockSpec((HALO, CT), lambda j: (0, j)), pl.BlockSpec((1, CT), lambda j: (0, j))],
        out_specs=pl.BlockSpec((n_rows, CT), lambda j: (0, j)),
        out_shape=jax.ShapeDtypeStruct((n_rows, D), F32),
        scratch_shapes=[pltpu.VMEM((n_rows + HALO, CT), F32)],
        compiler_params=_cparams("parallel"),
    )(proj, proj, b_in, b_in, w, cb)


def _conv_bwd(name, dz1, proj, b_in, w):
    n_rows = proj.shape[0]

    def body(dz_ref, cv_ref, cg_ref, bcv_ref, bcg_ref, w_ref, dcv_ref, dcg_ref, sm_ref, zp_ref, dzp_ref):
        cvb = cv_ref[...] + bcv_ref[...]
        sg = jax.nn.sigmoid(cg_ref[...] + bcg_ref[...])
        zp_ref[0:HALO, :] = jnp.zeros((HALO, CT), F32)
        zp_ref[HALO:, :] = cvb * sg
        dz = dz_ref[...]
        dzp_ref[0:n_rows, :] = dz
        dzp_ref[n_rows:, :] = jnp.zeros((HALO, CT), F32)
        sm_ref[...] = jnp.zeros(sm_ref.shape, F32)
        sm_ref[32:33, :] = jnp.sum(dz, axis=0, keepdims=True)

        def blk(rb, carry):
            base = pl.multiple_of(rb * RB, RB)
            dbank = _shift_bank(dzp_ref[pl.ds(base, RB + HALO), :])
            zbank = _shift_bank(zp_ref[pl.ds(base, RB + HALO), :])
            dzb = dbank[0][0:RB]
            acc = jnp.zeros((RB, CT), F32)
            for k in range(KCONV):
                acc = acc + w_ref[k:k + 1, :] * _shifted(dbank, KCONV - 1 - k)
                sm_ref[k:k + 1, :] += jnp.sum(dzb * _shifted(zbank, k + 2), axis=0, keepdims=True)
            dzp_ref[pl.ds(base, RB), :] = acc
            return carry

        lax.fori_loop(0, n_rows // RB, blk, 0)
        dz0 = dzp_ref[0:n_rows, :]
        dcv = dz0 * sg
        dcg = dz0 * cvb * (sg * (1.0 - sg))
        dcv_ref[...] = dcv.astype(dcv_ref.dtype)
        dcg_ref[...] = dcg.astype(dcg_ref.dtype)
        sm_ref[33:34, :] = jnp.sum(dcv, axis=0, keepdims=True)
        sm_ref[34:35, :] = jnp.sum(dcg, axis=0, keepdims=True)

    col = pl.BlockSpec((n_rows, CT), lambda j: (0, j))
    return pl.pallas_call(
        body, name=name, grid=(D // CT,),
        in_specs=[col, pl.BlockSpec((n_rows, CT), lambda j: (0, CV0 + j)), pl.BlockSpec((n_rows, CT), lambda j: (0, CG0 + j)),
                  pl.BlockSpec((1, CT), lambda j: (0, CV0 + j)), pl.BlockSpec((1, CT), lambda j: (0, CG0 + j)),
                  pl.BlockSpec((HALO, CT), lambda j: (0, j))],
        out_specs=[col, col, pl.BlockSpec((40, CT), lambda j: (0, j))],
        out_shape=[jax.ShapeDtypeStruct((n_rows, D), BF16), jax.ShapeDtypeStruct((n_rows, D), BF16),
                   jax.ShapeDtypeStruct((40, D), F32)],
        scratch_shapes=[pltpu.VMEM((n_rows + HALO, CT), F32), pltpu.VMEM((n_rows + HALO, CT), F32)],
        compiler_params=_cparams("parallel"),
    )(dz1, proj, proj, b_in, b_in, w)


ADA_TN = 768


def _split_bf16(v):
    hi = v.astype(BF16)
    return hi, (v - hi.astype(F32)).astype(BF16)


def _ada_fwd(name, c_all, w, b):
    n = w.shape[1]

    def body(c_ref, w_ref, b_ref, o_ref):
        ch, cl = _split_bf16(_silu(c_ref[...]))
        wh, wl = _split_bf16(w_ref[...])
        acc = jnp.dot(ch, wl, preferred_element_type=F32) + jnp.dot(cl, wh, preferred_element_type=F32)
        o_ref[...] = acc + jnp.dot(ch, wh, preferred_element_type=F32) + b_ref[...]

    return pl.pallas_call(
        body, name=name, grid=(n // ADA_TN,),
        in_specs=[pl.BlockSpec((8, D), lambda j: (0, 0)), pl.BlockSpec((D, ADA_TN), lambda j: (0, j)),
                  pl.BlockSpec((1, ADA_TN), lambda j: (0, j))],
        out_specs=pl.BlockSpec((8, ADA_TN), lambda j: (0, j)),
        out_shape=jax.ShapeDtypeStruct((8, n), F32),
        compiler_params=_cparams("parallel"),
    )(c_all, w, b)


def _ada_wgrad(name, c_all_t, dmod):
    n = dmod.shape[1]

    def body(ct_ref, dm_ref, o_ref):
        ca = _silu(ct_ref[...])
        acc = ca[:, 0:1] * dm_ref[0:1, :]
        for r in range(1, 8):
            acc = acc + ca[:, r:r + 1] * dm_ref[r:r + 1, :]
        o_ref[...] = acc

    return pl.pallas_call(
        body, name=name, grid=(n // ADA_TN,),
        in_specs=[pl.BlockSpec((D, 8), lambda j: (0, 0)), pl.BlockSpec((8, ADA_TN), lambda j: (0, j))],
        out_specs=pl.BlockSpec((D, ADA_TN), lambda j: (0, j)),
        out_shape=jax.ShapeDtypeStruct((D, n), F32),
        compiler_params=_cparams("parallel"),
    )(c_all_t, dmod)


def _adamw(name, w, g, m, v, tr):
    rows, cols = w.shape

    def body(w_ref, g_ref, m_ref, v_ref, g_out, d_ref, nm_ref, nv_ref):
        gv = g_ref[...]
        nm = ADAM_B1 * m_ref[...] + (1.0 - ADAM_B1) * gv
        nv = ADAM_B2 * v_ref[...] + (1.0 - ADAM_B2) * (gv * gv)
        m_hat = nm / (1.0 - ADAM_B1 ** ADAM_STEP)
        v_hat = nv / (1.0 - ADAM_B2 ** ADAM_STEP)
        d_ref[...] = -ADAM_LR * (m_hat / (jnp.sqrt(v_hat) + ADAM_EPS) + ADAM_WD * w_ref[...])
        nm_ref[...] = nm
        nv_ref[...] = nv
        g_out[...] = gv

    spec = pl.BlockSpec((tr, cols), lambda i: (i, 0))
    sds = jax.ShapeDtypeStruct((rows, cols), F32)
    return pl.pallas_call(
        body, name=name, grid=(rows // tr,), in_specs=[spec] * 4, out_specs=[spec] * 4, out_shape=[sds] * 4,
        compiler_params=_cparams("parallel"),
    )(w, g, m, v)


def _sum8(name, stacked, tr):
    n, rows, cols = stacked.shape

    def body(s_ref, o_ref):
        acc = s_ref[0]
        for r in range(1, n):
            acc = acc + s_ref[r]
        o_ref[...] = acc

    return pl.pallas_call(
        body, name=name, grid=(rows // tr,),
        in_specs=[pl.BlockSpec((n, tr, cols), lambda i: (0, i, 0))],
        out_specs=pl.BlockSpec((tr, cols), lambda i: (i, 0)),
        out_shape=jax.ShapeDtypeStruct((rows, cols), F32),
        compiler_params=_cparams("parallel"),
    )(stacked)


def _pair_add(name, g5, other, c_idx, tr):
    nq, _, rows, cols = g5.shape

    def body(c_ref, g_ref, o_ref, p_ref):
        p_ref[...] = (g_ref[...].astype(F32) + o_ref[...].astype(F32)).astype(p_ref.dtype)

    return pl.pallas_call(
        body, name=name,
        grid_spec=pltpu.PrefetchScalarGridSpec(
            num_scalar_prefetch=1, grid=(nq, rows // tr),
            in_specs=[pl.BlockSpec((None, None, tr, cols), lambda qi, i, cr: (qi, cr[0], i, 0)),
                      pl.BlockSpec((None, tr, cols), lambda qi, i, cr: (qi, i, 0))],
            out_specs=pl.BlockSpec((None, tr, cols), lambda qi, i, cr: (qi, i, 0))),
        out_shape=jax.ShapeDtypeStruct((nq, rows, cols), BF16),
        compiler_params=_cparams("parallel", "parallel"),
    )(c_idx, g5, other)


def _chip_add(name, p, recv, qc_idx, tr):
    _, rows, cols = p.shape

    def body(qc_ref, p_ref, r_ref, o_ref):
        acc = p_ref[...].astype(F32)
        for k in range(NCHIP - 1):
            acc = acc + r_ref[k].astype(F32)
        o_ref[...] = acc

    return pl.pallas_call(
        body, name=name,
        grid_spec=pltpu.PrefetchScalarGridSpec(
            num_scalar_prefetch=1, grid=(rows // tr,),
            in_specs=[pl.BlockSpec((None, tr, cols), lambda i, qc: (qc[0], i, 0)),
                      pl.BlockSpec((NCHIP - 1, tr, cols), lambda i, qc: (0, i, 0))],
            out_specs=pl.BlockSpec((None, tr, cols), lambda i, qc: (qc[1], i, 0))),
        out_shape=jax.ShapeDtypeStruct((2, rows, cols), F32),
        compiler_params=_cparams("parallel"),
    )(qc_idx, p, recv)


def _allgather8(name, blk):
    m_per, n = blk.shape

    def body(x_ref, out_ref, send_sems, recv_sems, local_sem):
        x, y, c = lax.axis_index("x"), lax.axis_index("y"), lax.axis_index("c")
        me, sibling = (x, y, c), (x, y, 1 - c)
        chips = [(1 - x, y), (x, 1 - y), (1 - x, 1 - y)]

        def rows(px, py, pc):
            return out_ref.at[pl.ds((4 * px + 2 * py + pc) * m_per, m_per), :]

        def copy(k, block, to, src=None):
            return pltpu.make_async_remote_copy(
                src_ref=rows(*block) if src is None else src, dst_ref=rows(*block),
                send_sem=send_sems.at[k], recv_sem=recv_sems.at[k], device_id=to, device_id_type=MESH)

        mine = pltpu.make_async_copy(x_ref, rows(*me), local_sem)
        mine.start()
        first = [copy(0, me, sibling, src=x_ref)]
        first += [copy(1 + j, me, (*chip, c), src=x_ref) for j, chip in enumerate(chips)]
        for cp in first:
            cp.start()
        passed = [copy(4 + j, (*chip, c), sibling) for j, chip in enumerate(chips)]
        for j, chip in enumerate(chips):
            copy(1 + j, (*chip, c), me).wait_recv()
            passed[j].start()
        copy(0, sibling, me).wait_recv()
        for j, chip in enumerate(chips):
            copy(4 + j, (*chip, 1 - c), me).wait_recv()
        for cp in first + passed:
            cp.wait_send()
        mine.wait()

    return pl.pallas_call(
        body, name=name,
        out_shape=jax.ShapeDtypeStruct((8 * m_per, n), blk.dtype),
        in_specs=[pl.BlockSpec(memory_space=pltpu.VMEM)],
        out_specs=pl.BlockSpec(memory_space=pltpu.VMEM),
        scratch_shapes=[pltpu.SemaphoreType.DMA((7,)), pltpu.SemaphoreType.DMA((7,)), pltpu.SemaphoreType.DMA],
        compiler_params=pltpu.CompilerParams(vmem_limit_bytes=VMEM_LIMIT),
    )(blk)


def _exchange(name, srcs, out_shapes, n_local, n_remote, plan, aliases=None, after=None):
    ni, no = len(srcs), len(out_shapes)
    extra = [] if after is None else [after]

    def body(*refs):
        ins, outs = refs[:ni], refs[ni + len(extra):ni + len(extra) + no]
        send_sems, recv_sems, local_sems = refs[ni + len(extra) + no:]
        x, y, c = lax.axis_index("x"), lax.axis_index("y"), lax.axis_index("c")
        local, remote = plan(ins, outs, x, y, c)
        assert len(local) == n_local and len(remote) == n_remote

        def rcopy(i, dst):
            s, _, peer, _, _ = remote[i]
            return pltpu.make_async_remote_copy(src_ref=s, dst_ref=dst, send_sem=send_sems.at[i],
                                                recv_sem=recv_sems.at[i], device_id=peer, device_id_type=MESH)

        lcs = [pltpu.make_async_copy(s, d, local_sems.at[i]) for i, (s, d) in enumerate(local)]
        for cp in lcs:
            cp.start()
        first = [i for i in range(n_remote) if remote[i][4] is None]
        passed = [i for i in range(n_remote) if remote[i][4] is not None]
        for i in first:
            rcopy(i, remote[i][1]).start()
        arrived = set()
        for i in passed:
            j = remote[i][4]
            rcopy(j, remote[j][3]).wait_recv()
            arrived.add(j)
            rcopy(i, remote[i][1]).start()
        for i in range(n_remote):
            if i not in arrived:
                rcopy(i, remote[i][3]).wait_recv()
        for i in range(n_remote):
            rcopy(i, remote[i][1]).wait_send()
        for cp in lcs:
            cp.wait()

    any_spec = pl.BlockSpec(memory_space=pl.ANY)
    return pl.pallas_call(
        body, name=name, out_shape=out_shapes,
        in_specs=[any_spec] * (ni + len(extra)), out_specs=[any_spec] * no,
        input_output_aliases=aliases or {},
        scratch_shapes=[pltpu.SemaphoreType.DMA((n_remote,)), pltpu.SemaphoreType.DMA((n_remote,)),
                        pltpu.SemaphoreType.DMA((max(n_local, 1),))],
    )(*srcs, *extra)


_CHIP_FLIPS = ((0, 1), (1, 0), (1, 1))


def _flip(v, f):
    return 1 - v if f else v


EFFECT = pltpu.SideEffectType.DATAFLOW_SIDE_EFFECTING


def _split_start(name, bufs, n, plan, after=None):
    nb = len(bufs)
    extra = [] if after is None else [after]

    def body(*refs):
        ins = refs[:nb]
        send_sems, recv_sems = refs[nb + len(extra)], refs[nb + len(extra) + 1]
        token = refs[-1]
        x, y, c = lax.axis_index("x"), lax.axis_index("y"), lax.axis_index("c")
        copies = plan(ins, x, y, c)
        assert len(copies) == n
        for i, (s, d, peer, _) in enumerate(copies):
            pltpu.make_async_remote_copy(src_ref=s, dst_ref=d, send_sem=send_sems.at[i], recv_sem=recv_sems.at[i],
                                         device_id=peer, device_id_type=MESH).start()
        token[...] = jnp.zeros_like(token)

    hbm = pl.BlockSpec(memory_space=pltpu.HBM)
    sem = pl.BlockSpec(memory_space=pltpu.SEMAPHORE)
    res = pl.pallas_call(
        body, name=name,
        out_shape=(pltpu.SemaphoreType.DMA((n,)), pltpu.SemaphoreType.DMA((n,)),
                   *[pltpu.HBM(b.shape, b.dtype) for b in bufs], jax.ShapeDtypeStruct((8, 128), F32)),
        in_specs=[hbm] * nb + [pl.BlockSpec(memory_space=pl.ANY)] * len(extra),
        out_specs=(sem, sem, *[hbm] * nb, pl.BlockSpec(memory_space=pltpu.VMEM)),
        input_output_aliases={i: 2 + i for i in range(nb)},
        compiler_params=pltpu.CompilerParams(has_side_effects=EFFECT),
    )(*[pltpu.with_memory_space_constraint(b, pltpu.HBM) for b in bufs], *extra)
    return res[0], res[1], list(res[2:2 + nb]), res[-1]


def _split_wait(name, bufs, send_sems, recv_sems, after, n, plan):
    nb = len(bufs)

    def body(*refs):
        ins = refs[:nb]
        ssem, rsem = refs[nb], refs[nb + 1]
        x, y, c = lax.axis_index("x"), lax.axis_index("y"), lax.axis_index("c")
        copies = plan(ins, x, y, c)
        assert len(copies) == n
        for i, (s, _, peer, lands) in enumerate(copies):
            cp = pltpu.make_async_remote_copy(src_ref=s, dst_ref=lands, send_sem=ssem.at[i], recv_sem=rsem.at[i],
                                              device_id=peer, device_id_type=MESH)
            cp.wait_send()
            cp.wait_recv()

    hbm = pl.BlockSpec(memory_space=pltpu.HBM)
    sem = pl.BlockSpec(memory_space=pltpu.SEMAPHORE)
    afters = list(after) if isinstance(after, (list, tuple)) else [after]
    res = pl.pallas_call(
        body, name=name,
        out_shape=[pltpu.HBM(b.shape, b.dtype) for b in bufs],
        in_specs=[hbm] * nb + [sem, sem] + [pl.BlockSpec(memory_space=pl.ANY)] * len(afters),
        out_specs=[hbm] * nb,
        input_output_aliases={i: i for i in range(nb)},
        compiler_params=pltpu.CompilerParams(has_side_effects=EFFECT),
    )(*bufs, send_sems, recv_sems, *afters)
    return list(res)


def _chips_of(x, y):
    return [(_flip(x, fx), _flip(y, fy)) for fx, fy in _CHIP_FLIPS]


def _gather_start(tag, stacks, after):
    n = len(stacks)

    def plan(refs, x, y, c):
        q = 2 * x + y
        return [(refs[a].at[q, c], refs[a].at[q, c], (px, py, c), refs[a].at[2 * px + py, c])
                for a in range(n) for px, py in _chips_of(x, y)]

    send, recv, thru, token = _split_start(f"gather_start_{tag}", stacks, 3 * n, plan, after)
    return (send, recv, thru, plan), token


def _gather_finish(tag, state, after):
    send, recv, thru, plan = state
    n = len(thru)
    landed = _split_wait(f"gather_wait_{tag}", thru, send, recv, after, 3 * n, plan)

    def pass_on(ins, outs, x, y, c):
        sib = (x, y, 1 - c)
        remote = []
        for a in range(n):
            for px, py in _chips_of(x, y):
                p = 2 * px + py
                remote.append((ins[a].at[p, c], outs[a].at[p, c], sib, outs[a].at[p, 1 - c], None))
        return [], remote

    full = _exchange(f"gather_pass_{tag}", landed, [jax.ShapeDtypeStruct(s.shape, s.dtype) for s in landed],
                     0, 3 * n, pass_on, aliases={a: a for a in range(n)})
    return [g.reshape(NCHIP, 2 * g.shape[2], g.shape[3]) for g in full]


def _gather_small(name, smalls):
    n = len(smalls)

    def plan(ins, outs, x, y, c):
        q = 2 * x + y
        local = [(ins[b], outs[b].at[q]) for b in range(n)]
        remote = [(ins[b], outs[b].at[q], (px, py, c), outs[b].at[2 * px + py], None)
                  for b in range(n) for px, py in _chips_of(x, y)]
        return local, remote

    return _exchange(name, smalls, [jax.ShapeDtypeStruct((NCHIP,) + s.shape, s.dtype) for s in smalls], n, 3 * n, plan)


def _rs_pair_start(tag, grads, after=None):
    n = len(grads)
    g5 = [g.reshape(NCHIP, 2, g.shape[1] // 2, g.shape[2]) for g in grads]
    half = [(g.shape[2], g.shape[3]) for g in g5]
    land = [lax.empty((NCHIP,) + h, g.dtype) for h, g in zip(half, grads)]

    def plan(refs, x, y, c):
        sib = (x, y, 1 - c)
        return [(refs[a].at[:, 1 - c], refs[n + a], sib, refs[n + a]) for a in range(n)]

    send, recv, thru, token = _split_start(f"rs_pair_start_{tag}", g5 + land, n, plan, after)
    return (send, recv, thru, plan, half), token


def _rs_chips_start(tag, state, after, c_idx):
    send, recv, thru, pair_plan, half = state
    n = len(half)
    landed = _split_wait(f"rs_pair_wait_{tag}", thru, send, recv, after, n, pair_plan)
    g5, got = landed[:n], landed[n:]
    part = [_pair_add(f"rs_pair_add_{tag}{a}", g5[a], got[a], c_idx, _tile(half[a][0], 512)) for a in range(n)]
    land = [lax.empty((NCHIP - 1,) + h, BF16) for h in half]

    def plan(refs, x, y, c):
        return [(refs[a].at[2 * px + py], refs[n + a].at[k], (px, py, c), refs[n + a].at[k])
                for a in range(n) for k, (px, py) in enumerate(_chips_of(x, y))]

    send, recv, thru, token = _split_start(f"rs_chips_start_{tag}", part + land, 3 * n, plan)
    return (send, recv, thru, plan, half), token


def _rs_finish(tag, state, after, qc_idx):
    send, recv, thru, plan, half = state
    n = len(half)
    landed = _split_wait(f"rs_chips_wait_{tag}", thru, send, recv, after, 3 * n, plan)
    red = [_chip_add(f"rs_chip_add_{tag}{a}", landed[a], landed[n + a], qc_idx, _tile(half[a][0], 512)) for a in range(n)]

    def share(ins, outs, x, y, c):
        sib = (x, y, 1 - c)
        return [], [(ins[a].at[c], outs[a].at[c], sib, outs[a].at[1 - c], None) for a in range(n)]

    full = _exchange(f"rs_share_{tag}", red, [jax.ShapeDtypeStruct((2,) + h, F32) for h in half], 0, n, share,
                     aliases={a: a for a in range(n)})
    return [f.reshape(2 * h[0], h[1]) for f, h in zip(full, half)]


def _pack(name, n_rows, pieces, after=None):
    arrays = [p[1] for p in pieces if p[0] != "zeros"]
    extra = [] if after is None else [after]

    def body(*refs):
        o_ref = refs[-1]
        o_ref[...] = jnp.zeros(o_ref.shape, F32)
        row, ai = 0, 0
        for p in pieces:
            kind = p[0]
            if kind == "zeros":
                row += p[1]
                continue
            ref = refs[len(extra) + ai]
            ai += 1
            if kind == "rows":
                r = ref.shape[0]
                o_ref[row:row + r, :] = ref[...]
            elif kind == "wide":
                r = ref.shape[1] // D
                for j in range(r):
                    o_ref[row + j:row + j + 1, :] = ref[:, j * D:(j + 1) * D]
            elif kind == "slice":
                r = p[3] - p[2]
                o_ref[row:row + r, :] = ref[p[2]:p[3], :]
            elif kind == "heads":
                r = CHUNK
                for h in range(HEADS):
                    o_ref[row:row + r, h * CHUNK:(h + 1) * CHUNK] = ref[h]
            else:
                r = 1
                o_ref[row:row + 1, 0:128] = ref[0:1, :]
            row += r
        assert row == n_rows, (row, n_rows)

    vmem = pl.BlockSpec(memory_space=pltpu.VMEM)
    return pl.pallas_call(
        body, name=name, out_shape=jax.ShapeDtypeStruct((n_rows, D), F32),
        in_specs=[pl.BlockSpec(memory_space=pl.ANY)] * len(extra) + [vmem] * len(arrays), out_specs=vmem,
        compiler_params=pltpu.CompilerParams(vmem_limit_bytes=VMEM_LIMIT),
    )(*extra, *arrays)


def _to_bf16_stack(name, w, q_idx):
    r, cols = w.shape
    half = r // 2
    tr = _tile(half, 256)
    nb = half // tr

    def body(q_ref, w_ref, o_ref):
        o_ref[...] = w_ref[...].astype(BF16)

    return pl.pallas_call(
        body, name=name,
        grid_spec=pltpu.PrefetchScalarGridSpec(
            num_scalar_prefetch=1, grid=(2 * nb,),
            in_specs=[pl.BlockSpec((tr, cols), lambda i, qr: (i, 0))],
            out_specs=pl.BlockSpec((None, None, tr, cols), lambda i, qr: (qr[0], i // nb, i % nb, 0))),
        out_shape=jax.ShapeDtypeStruct((NCHIP, 2, half, cols), BF16),
        compiler_params=_cparams("parallel"),
    )(q_idx, w)


def kernel(x, c, ada_w, ada_b, norm_ffn1, ffn1_w_gate, ffn1_w_up, ffn1_w_down, norm_mix, mix_w_in, mix_b_in, sgu_ln_g, sgu_ln_b, sgu_w_s, sgu_b_s, conv_w, conv_b, conv_ln_g, conv_ln_b, w_branch_a, w_branch_b, w_out, norm_ffn2, ffn2_w_gate, ffn2_w_up, ffn2_w_down, norm_final, loss_target, m_ada_w, m_ada_b, m_norm_ffn1, m_ffn1_w_gate, m_ffn1_w_up, m_ffn1_w_down, m_norm_mix, m_mix_w_in, m_mix_b_in, m_sgu_ln_g, m_sgu_ln_b, m_sgu_w_s, m_sgu_b_s, m_conv_w, m_conv_b, m_conv_ln_g, m_conv_ln_b, m_w_branch_a, m_w_branch_b, m_w_out, m_norm_ffn2, m_ffn2_w_gate, m_ffn2_w_up, m_ffn2_w_down, m_norm_final, v_ada_w, v_ada_b, v_norm_ffn1, v_ffn1_w_gate, v_ffn1_w_up, v_ffn1_w_down, v_norm_mix, v_mix_w_in, v_mix_b_in, v_sgu_ln_g, v_sgu_ln_b, v_sgu_w_s, v_sgu_b_s, v_conv_w, v_conv_b, v_conv_ln_g, v_conv_ln_b, v_w_branch_a, v_w_branch_b, v_w_out, v_norm_ffn2, v_ffn2_w_gate, v_ffn2_w_up, v_ffn2_w_down, v_norm_final):
    xi, yi, ci = lax.axis_index("x"), lax.axis_index("y"), lax.axis_index("c")
    q = 2 * xi + yi
    dev = 4 * xi + 2 * yi + ci
    c_idx = jnp.reshape(ci, (1,)).astype(jnp.int32)
    q_idx = jnp.reshape(q, (1,)).astype(jnp.int32)
    x0 = x[0]
    tgt = loss_target[0]
    qc_idx = jnp.stack([q, ci]).astype(jnp.int32)
    n_ada = ada_w.shape[2]

    c_all = _allgather8("ag_c", jnp.pad(c, ((0, 7), (0, 0))))[::8]
    ada_b_mine = lax.dynamic_slice(ada_b, (0, q * n_ada), (1, n_ada))
    mod_part = _ada_fwd("ada_fwd", c_all, ada_w[0], ada_b_mine)
    small_parts = _gather_small("gather_small", [mod_part, jnp.pad(conv_w[0], ((0, 1), (0, 0)))])

    big = dict(g1=ffn1_w_gate, u1=ffn1_w_up, d1=ffn1_w_down, win=mix_w_in, wa=w_branch_a, wb=w_branch_b, wo=w_out,
               g2=ffn2_w_gate, u2=ffn2_w_up, d2=ffn2_w_down)
    groups = dict(f1=("d1", "g1", "u1"), mx=("wo", "wa", "wb", "win"), f2=("d2", "g2", "u2"))
    gather_groups = dict(up1=("g1", "u1"), dn1=("d1",), win=("win",), mx=("wa", "wb", "wo"), f2=("g2", "u2", "d2"))
    gather_state = {}
    started = jnp.zeros((1, 1), F32)
    for tag, keys in gather_groups.items():
        gather_state[tag], token = _gather_start(
            tag, [_to_bf16_stack(f"cast_{k}", big[k][0], q_idx) for k in keys], small_parts[0])
        started = started + token[0:1, 0:1]

    mod_all = jnp.transpose(small_parts[0], (1, 0, 2)).reshape(8, N_MOD * D)
    mod = lax.dynamic_slice(mod_all, (dev, 0), (1, N_MOD * D)) + started
    sh1, sc1, g1, sh2, sc2, g2, sh3, sc3, g3 = [mod[:, k * D:(k + 1) * D] for k in range(N_MOD)]
    convw = jnp.transpose(small_parts[1], (1, 0, 2)).reshape(HALO, D)
    W = dict(zip(gather_groups["up1"], _gather_finish("up1", gather_state["up1"], mod)))

    causal = jnp.tril(jnp.ones((CHUNK, CHUNK), bool))
    wm_f = jnp.where(causal[None], sgu_w_s[0], 0.0)
    wm = wm_f.astype(BF16)
    wmt = jnp.swapaxes(wm_f, 1, 2).astype(BF16)
    bst = jnp.repeat(jnp.transpose(sgu_b_s[0]), CHUNK, axis=1)
    b_in = mix_b_in

    (h1,) = _rowwise_fwd("pre1", _f_pre, [x0], [norm_ffn1, sc1, sh1], [(D, BF16)], 256)
    a1, b1, s1 = _ffn_up("up1", h1, W["g1"], W["u1"])
    W.update(zip(gather_groups["dn1"], _gather_finish("dn1", gather_state["dn1"], s1)))
    f1 = _matmul("down1", s1, W["d1"], "fwd", False, F32)
    x1, h2 = _rowwise_fwd("res1", _f_res_pre(0.5), [x0, f1], [g1, norm_mix, sc2, sh2], [(D, F32), (D, BF16)], 256)
    W.update(zip(gather_groups["win"], _gather_finish("win", gather_state["win"], x1)))
    proj = _matmul("in_proj", h2, W["win"], "fwd", True, F32)
    W.update(zip(gather_groups["mx"], _gather_finish("mx", gather_state["mx"], proj)))
    ya = _sgu_fwd("sgu", proj, b_in, sgu_ln_g, sgu_ln_b, wm, bst, 256)
    z1 = _conv_fwd("conv", proj, b_in, convw, conv_b)
    (z3,) = _rowwise_fwd("conv_ln", _f_lnsilu, [z1], [conv_ln_g, conv_ln_b], [(D, BF16)], 256)
    y_a = _matmul("branch_a", ya, W["wa"], "fwd", False, F32)
    y_b = _matmul("branch_b", z3, W["wb"], "fwd", False, F32)
    (merged,) = _rowwise_fwd("merge", _f_merge, [(proj, 2 * D, 2), y_a, y_b], [(b_in, 2 * D, 2)], [(D, BF16)], 256)
    y = _matmul("out_proj", merged, W["wo"], "fwd", False, F32)
    x2, h3 = _rowwise_fwd("res2", _f_res_pre(1.0), [x1, y], [g2, norm_ffn2, sc3, sh3], [(D, F32), (D, BF16)], 256)
    W.update(zip(gather_groups["f2"], _gather_finish("f2", gather_state["f2"], x2)))
    a3, b3, s3 = _ffn_up("up2", h3, W["g2"], W["u2"])
    f3 = _matmul("down2", s3, W["d2"], "fwd", False, F32)
    dx2, df3, dg3, dnwf, loss_blk = _final("final", x2, f3, tgt, g3, norm_final.reshape(1, D), 256)

    G = {}

    def ffn_bwd(tag, df, s, a, b, h, wg, wu, wd, after=None):
        da, db = _dgrad_glu(f"d_down{tag}", df, wd, a, b, after)
        G["d" + tag] = _matmul(f"w_down{tag}", s, df, "wgrad", False, BF16)
        dh = _matmul(f"d_gate{tag}", da, wg, "dgrad", True, F32)
        dh = _matmul(f"d_up{tag}", db, wu, "dgrad", True, F32, add=dh)
        G["g" + tag] = _matmul(f"w_gate{tag}", h, da, "wgrad", True, BF16)
        G["u" + tag] = _matmul(f"w_up{tag}", h, db, "wgrad", True, BF16)
        return dh

    reduced = {}
    dh3 = ffn_bwd("2", df3, s3, a3, b3, h3, W["g2"], W["u2"], W["d2"])
    pair_f2, token = _rs_pair_start("f2", [G[k] for k in groups["f2"]])
    (dx1, dy), (dg2, dnw3, dsc3, dsh3) = _rowwise_bwd(
        "res2_b", _f_res_pre(1.0), [x1, y], [g2 + token[0:1, 0:1], norm_ffn2, sc3, sh3], [dx2, dh3], [F32, BF16], 256)
    dmerged = _matmul("d_out", dy, W["wo"], "dgrad", False, F32)
    G["wo"] = _matmul("w_out", merged, dy, "wgrad", False, BF16)
    rs_f2, token = _rs_chips_start("f2", pair_f2, [dmerged, G["wo"]], c_idx)
    (dgab, dy_a, dy_b), (dbgab,) = _rowwise_bwd(
        "merge_b", _f_merge, [(proj, 2 * D, 2), y_a, y_b], [(b_in + token[0:1, 0:1], 2 * D, 2)], [dmerged],
        [BF16, BF16, BF16], 256)
    dya = _matmul("d_a", dy_a, W["wa"], "dgrad", False, F32)
    dz3 = _matmul("d_b", dy_b, W["wb"], "dgrad", False, F32)
    G["wa"] = _matmul("w_a", ya, dy_a, "wgrad", False, BF16)
    G["wb"] = _matmul("w_b", z3, dy_b, "wgrad", False, BF16)
    (dz1,), (dclg, dclb) = _rowwise_bwd("conv_ln_b", _f_lnsilu, [z1], [conv_ln_g, conv_ln_b], [dz3], [F32], 256)
    dcv, dcg, conv_small = _conv_bwd("conv_b", dz1, proj, b_in, convw)
    duv, dbuv, dlng, dlnb, dws, dbs = _sgu_bwd("sgu_b", proj, b_in, sgu_ln_g, sgu_ln_b, wm, wmt, bst, dya, 256)
    dproj = jnp.concatenate([duv, dcv, dcg, dgab], axis=1)
    dh2 = _matmul("d_in", dproj, W["win"], "dgrad", True, F32)
    G["win"] = _matmul("w_in", h2, dproj, "wgrad", True, BF16)
    pair_mx, token = _rs_pair_start("mx", [G[k] for k in groups["mx"]])
    reduced.update(zip(groups["f2"], _rs_finish("f2", rs_f2, [dh2, token], qc_idx)))

    early = _pack("pack_early", HALO + CHUNK, [("slice", conv_small, 0, HALO), ("heads", dws)], after=reduced["d2"])
    every_early = _allgather8("ag_early", early)
    tot_early = _sum8("sum_early", every_early.reshape(8, HALO + CHUNK, D), HALO)

    (dx0, df1), (dg1, dnw2, dsc2, dsh2) = _rowwise_bwd(
        "res1_b", _f_res_pre(0.5), [x0, f1], [g1 + token[0:1, 0:1], norm_mix, sc2, sh2], [dx1, dh2], [F32, BF16], 256)
    rs_mx, token = _rs_chips_start("mx", pair_mx, [df1, every_early], c_idx)
    dh1 = ffn_bwd("1", df1, s1, a1, b1, h1, W["g1"], W["u1"], W["d1"], after=token)
    pair_f1, token = _rs_pair_start("f1", [G[k] for k in groups["f1"]])
    reduced.update(zip(groups["mx"], _rs_finish("mx", rs_mx, [dh1, token], qc_idx)))
    (grad_x,), (dnw1, dsc1, dsh1) = _rowwise_bwd(
        "pre1_b", _f_pre_keep, [x0], [norm_ffn1, sc1 + token[0:1, 0:1], sh1], [dx0, dh1], [F32], 256)

    dbs_row = jnp.transpose(dbs[:, ::CHUNK]).reshape(1, D)
    small = _pack("pack_small", HALO, [
        ("rows", dnw1), ("rows", dnw2), ("wide", dbuv), ("slice", conv_small, 33, 34), ("slice", conv_small, 34, 35),
        ("wide", dbgab), ("rows", dlng), ("rows", dlnb), ("rows", dbs_row), ("slice", conv_small, 32, 33),
        ("rows", dclg), ("rows", dclb), ("rows", dnw3), ("rows", dnwf),
        ("rows", dsh1), ("rows", dsc1), ("rows", dg1), ("rows", dsh2), ("rows", dsc2), ("rows", dg2),
        ("rows", dsh3), ("rows", dsc3), ("rows", dg3), ("lanes", loss_blk), ("zeros", 6)])
    every = _allgather8("ag_small", small)
    rs_f1, token = _rs_chips_start("f1", pair_f1, every, c_idx)
    every = (every + token[0:1, 0:1]).reshape(8, HALO, D)
    tot = _sum8("sum_small", every, HALO)
    loss = tot[25, 0]
    dmod_all = every[:, 16:16 + N_MOD, :].reshape(8, N_MOD * D)
    dmod_mine = lax.dynamic_slice(dmod_all, (0, q * n_ada), (8, n_ada))
    g_ada_w = _ada_wgrad("ada_wgrad", jnp.transpose(c_all), dmod_mine)

    given = dict(ada_w=(ada_w, m_ada_w, v_ada_w), ada_b=(ada_b, m_ada_b, v_ada_b),
                 norm_ffn1=(norm_ffn1, m_norm_ffn1, v_norm_ffn1),
                 ffn1_w_gate=(ffn1_w_gate, m_ffn1_w_gate, v_ffn1_w_gate), ffn1_w_up=(ffn1_w_up, m_ffn1_w_up, v_ffn1_w_up),
                 ffn1_w_down=(ffn1_w_down, m_ffn1_w_down, v_ffn1_w_down), norm_mix=(norm_mix, m_norm_mix, v_norm_mix),
                 mix_w_in=(mix_w_in, m_mix_w_in, v_mix_w_in), mix_b_in=(mix_b_in, m_mix_b_in, v_mix_b_in),
                 sgu_ln_g=(sgu_ln_g, m_sgu_ln_g, v_sgu_ln_g), sgu_ln_b=(sgu_ln_b, m_sgu_ln_b, v_sgu_ln_b),
                 sgu_w_s=(sgu_w_s, m_sgu_w_s, v_sgu_w_s), sgu_b_s=(sgu_b_s, m_sgu_b_s, v_sgu_b_s),
                 conv_w=(conv_w, m_conv_w, v_conv_w), conv_b=(conv_b, m_conv_b, v_conv_b),
                 conv_ln_g=(conv_ln_g, m_conv_ln_g, v_conv_ln_g), conv_ln_b=(conv_ln_b, m_conv_ln_b, v_conv_ln_b),
                 w_branch_a=(w_branch_a, m_w_branch_a, v_w_branch_a), w_branch_b=(w_branch_b, m_w_branch_b, v_w_branch_b),
                 w_out=(w_out, m_w_out, v_w_out), norm_ffn2=(norm_ffn2, m_norm_ffn2, v_norm_ffn2),
                 ffn2_w_gate=(ffn2_w_gate, m_ffn2_w_gate, v_ffn2_w_gate), ffn2_w_up=(ffn2_w_up, m_ffn2_w_up, v_ffn2_w_up),
                 ffn2_w_down=(ffn2_w_down, m_ffn2_w_down, v_ffn2_w_down), norm_final=(norm_final, m_norm_final, v_norm_final))
    out = {}

    def big_update(name, g2d):
        w, m, v = given[name]
        shp = w.shape
        res = _adamw(f"adamw_{name}", w.reshape(g2d.shape), g2d, m.reshape(g2d.shape), v.reshape(g2d.shape),
                     _tile(g2d.shape[0], 512 if g2d.shape[1] <= D else 256))
        out[name] = tuple(t.reshape(shp) for t in res)

    big_names = dict(g1="ffn1_w_gate", u1="ffn1_w_up", d1="ffn1_w_down", win="mix_w_in", wa="w_branch_a",
                     wb="w_branch_b", wo="w_out", g2="ffn2_w_gate", u2="ffn2_w_up", d2="ffn2_w_down")
    for key in groups["f2"] + groups["mx"]:
        big_update(big_names[key], reduced[key])
    big_update("ada_w", g_ada_w)

    small_rows = [("norm_ffn1", 0, 1), ("norm_mix", 1, 1), ("mix_b_in", 2, 6), ("sgu_ln_g", 8, 1), ("sgu_ln_b", 9, 1),
                  ("sgu_b_s", 10, 1), ("conv_b", 11, 1), ("conv_ln_g", 12, 1), ("conv_ln_b", 13, 1),
                  ("norm_ffn2", 14, 1), ("norm_final", 15, 1), ("ada_b", 16, N_MOD)]

    def pack_rows(which, label):
        pieces = []
        for nm, _, r in small_rows:
            t = given[nm][which]
            pieces.append(("wide", t) if r > 1 else ("rows", t.reshape(1, D)))
        return _pack(f"pack_{label}", HALO, pieces + [("zeros", HALO - 16 - N_MOD)])

    small_res = _adamw("adamw_small", pack_rows(0, "w"), tot, pack_rows(1, "m"), pack_rows(2, "v"), HALO)
    for nm, r0, r in small_rows:
        shp = given[nm][0].shape
        out[nm] = tuple(t[r0:r0 + r].reshape(shp) for t in small_res)
    n_cw = conv_w.shape[2]
    g_cw = lax.dynamic_slice(tot_early, (0, q * n_cw), (HALO, n_cw))
    pad_cw = lambda t: jnp.pad(t[0], ((0, 1), (0, 0)))
    cw_res = _adamw("adamw_conv_w", pad_cw(conv_w), g_cw, pad_cw(m_conv_w), pad_cw(v_conv_w), HALO)
    out["conv_w"] = tuple(t[:KCONV][None] for t in cw_res)
    g_ws = jnp.transpose(tot_early[HALO:].reshape(CHUNK, HEADS, CHUNK), (1, 0, 2)).reshape(HEADS * CHUNK, CHUNK)
    flat_ws = lambda t: t.reshape(HEADS * CHUNK, CHUNK)
    ws_res = _adamw("adamw_sgu_w_s", flat_ws(sgu_w_s), g_ws, flat_ws(m_sgu_w_s), flat_ws(v_sgu_w_s), 512)
    out["sgu_w_s"] = tuple(t.reshape(sgu_w_s.shape) for t in ws_res)

    done = [out[n][1] for n in ("ada_w", "sgu_w_s", "conv_w", "ada_b")] + [out[big_names[k]][1] for k in groups["f2"] + groups["mx"]]
    reduced.update(zip(groups["f1"], _rs_finish("f1", rs_f1, done, qc_idx)))
    for key in groups["f1"]:
        big_update(big_names[key], reduced[key])

    order = ["ada_w", "ada_b", "norm_ffn1", "ffn1_w_gate", "ffn1_w_up", "ffn1_w_down", "norm_mix", "mix_w_in", "mix_b_in",
             "sgu_ln_g", "sgu_ln_b", "sgu_w_s", "sgu_b_s", "conv_w", "conv_b", "conv_ln_g", "conv_ln_b", "w_branch_a",
             "w_branch_b", "w_out", "norm_ffn2", "ffn2_w_gate", "ffn2_w_up", "ffn2_w_down", "norm_final"]
    return (loss, grad_x[None], *[out[n][0] for n in order], *[out[n][1] for n in order],
            *[out[n][2] for n in order], *[out[n][3] for n in order])
```

```python
import functools

import jax
import jax.numpy as jnp
from jax import lax
from jax.experimental import pallas as pl
from jax.experimental.pallas import tpu as pltpu

F32 = jnp.float32
BF16 = jnp.bfloat16
D = 1024
HEADS = 8
CHUNK = 128
KCONV = 31
HALO = 32
N_MOD = 9
EPS = 1e-6
NCHIP = 4
VMEM_LIMIT = 56 * 1024 * 1024
MESH = pl.DeviceIdType.MESH

ADAM_LR = 0.001
ADAM_B1 = 0.9
ADAM_B2 = 0.999
ADAM_EPS = 1e-08
ADAM_WD = 0.01
ADAM_STEP = 10

NN = (((1,), (0,)), ((), ()))
NT = (((1,), (1,)), ((), ()))
TN = (((0,), (0,)), ((), ()))


def _cparams(*sem):
    return pltpu.CompilerParams(dimension_semantics=sem or None, vmem_limit_bytes=VMEM_LIMIT)


def _rms(x, w):
    return x * lax.rsqrt(jnp.mean(x * x, axis=-1, keepdims=True) + EPS) * w


def _ln(x, g, b):
    mu = jnp.mean(x, axis=-1, keepdims=True)
    xc = x - mu
    var = jnp.mean(xc * xc, axis=-1, keepdims=True)
    return xc * lax.rsqrt(var + EPS) * g + b


def _silu(x):
    return x * jax.nn.sigmoid(x)


def _gelu(x):
    return x * (0.5 * (1.0 + jnp.tanh(0.7978845608028654 * (x + 0.044715 * (x * x * x)))))


def _f_pre(x, nw, sc, sh):
    return _rms(x, nw) * (1.0 + sc) + sh


def _f_pre_keep(x, nw, sc, sh):
    return x, _f_pre(x, nw, sc, sh)


def _f_res_pre(scale):
    def fn(x, f, g, nw, sc, sh):
        xn = x + (scale * g) * f
        return xn, _f_pre(xn, nw, sc, sh)
    return fn


def _f_glu(a, b):
    return _silu(a) * b


def _f_merge(gab, ya, yb, bgab):
    g = gab + bgab
    return jax.nn.sigmoid(g[:, :D]) * ya + jax.nn.sigmoid(g[:, D:]) * yb


def _f_lnsilu(z, g, b):
    return _silu(_ln(z, g, b))


def _f_mixa(uv, buv, g, b):
    t = uv + buv
    return _gelu(t[:, :D]), _ln(_gelu(t[:, D:]), g, b)


def _col(t):
    return t if isinstance(t, tuple) else (t, t.shape[-1], 0)


def _rowwise_fwd(name, fn, rows, params, outs, tr):
    rows = [_col(r) for r in rows]
    params = [_col(p) for p in params]
    n_rows = rows[0][0].shape[0]
    nin = len(rows) + len(params)

    def body(*refs):
        res = fn(*[r[...].astype(F32) for r in refs[:nin]])
        res = res if isinstance(res, tuple) else (res,)
        for o, v in zip(refs[nin:], res, strict=True):
            o[...] = v.astype(o.dtype)

    return pl.pallas_call(
        body, name=name, grid=(n_rows // tr,),
        in_specs=[pl.BlockSpec((tr, w), functools.partial(lambda cb, i: (i, cb), cb)) for _, w, cb in rows]
        + [pl.BlockSpec((1, w), functools.partial(lambda cb, i: (0, cb), cb)) for _, w, cb in params],
        out_specs=[pl.BlockSpec((tr, w), lambda i: (i, 0)) for w, _ in outs],
        out_shape=[jax.ShapeDtypeStruct((n_rows, w), dt) for w, dt in outs],
        compiler_params=_cparams("parallel"),
    )(*[r[0] for r in rows], *[p[0] for p in params])


def _rowwise_bwd(name, fn, rows, params, cots, row_grads, tr):
    rows = [_col(r) for r in rows]
    params = [_col(p) for p in params]
    cots = [_col(t) for t in cots]
    n_rows = rows[0][0].shape[0]
    nr, npar, nc = len(rows), len(params), len(cots)
    nin = nr + npar + nc
    n_rg = sum(dt is not None for dt in row_grads)

    def body(*refs):
        i = pl.program_id(0)
        prim = [r[...].astype(F32) for r in refs[:nr + npar]]
        ct = tuple(r[...].astype(F32) for r in refs[nr + npar:nin])
        _, vjp = jax.vjp(fn, *prim)
        g = vjp(ct if nc > 1 else ct[0])
        outs = refs[nin:]
        oi = 0
        for j, dt in enumerate(row_grads):
            if dt is not None:
                outs[oi][...] = g[j].astype(dt)
                oi += 1
        for j in range(npar):
            acc = outs[n_rg + j]

            @pl.when(i == 0)
            def _(acc=acc):
                acc[...] = jnp.zeros(acc.shape, F32)

            acc[...] += g[nr + j]

    res = pl.pallas_call(
        body, name=name, grid=(n_rows // tr,),
        in_specs=[pl.BlockSpec((tr, w), functools.partial(lambda cb, i: (i, cb), cb)) for _, w, cb in rows]
        + [pl.BlockSpec((1, w), functools.partial(lambda cb, i: (0, cb), cb)) for _, w, cb in params]
        + [pl.BlockSpec((tr, w), functools.partial(lambda cb, i: (i, cb), cb)) for _, w, cb in cots],
        out_specs=[pl.BlockSpec((tr, rows[j][1]), lambda i: (i, 0)) for j, dt in enumerate(row_grads) if dt is not None]
        + [pl.BlockSpec((1, w), lambda i: (0, 0)) for _, w, _ in params],
        out_shape=[jax.ShapeDtypeStruct((n_rows, rows[j][1]), dt) for j, dt in enumerate(row_grads) if dt is not None]
        + [jax.ShapeDtypeStruct((1, w), F32) for _, w, _ in params],
        compiler_params=_cparams("arbitrary"),
    )(*[r[0] for r in rows], *[p[0] for p in params], *[t[0] for t in cots])
    return res[:n_rg], res[n_rg:]


def _final(name, x, f, tgt, g, nw, tr):
    n_rows = x.shape[0]

    def body(x_ref, f_ref, t_ref, g_ref, nw_ref, dx_ref, df_ref, dg_ref, dnw_ref, loss_ref):
        i = pl.program_id(0)
        tg = t_ref[...]

        def fn(xv, fv, gv, nwv):
            e = _rms(xv + (0.5 * gv) * fv, nwv) - tg
            return 0.5 * jnp.mean(e * e, axis=-1, keepdims=True)

        per_row, vjp = jax.vjp(fn, x_ref[...], f_ref[...], g_ref[...], nw_ref[...])
        dx, df, dg, dnw = vjp(jnp.ones_like(per_row))
        dx_ref[...] = dx
        df_ref[...] = df.astype(df_ref.dtype)

        @pl.when(i == 0)
        def _():
            dg_ref[...] = jnp.zeros(dg_ref.shape, F32)
            dnw_ref[...] = jnp.zeros(dnw_ref.shape, F32)
            loss_ref[...] = jnp.zeros(loss_ref.shape, F32)

        dg_ref[...] += dg
        dnw_ref[...] += dnw
        loss_ref[...] += jnp.broadcast_to(jnp.sum(per_row, axis=0, keepdims=True), loss_ref.shape)

    row = pl.BlockSpec((tr, D), lambda i: (i, 0))
    par = pl.BlockSpec((1, D), lambda i: (0, 0))
    return pl.pallas_call(
        body, name=name, grid=(n_rows // tr,),
        in_specs=[row, row, row, par, par],
        out_specs=[row, row, par, par, pl.BlockSpec((8, 128), lambda i: (0, 0))],
        out_shape=[jax.ShapeDtypeStruct((n_rows, D), F32), jax.ShapeDtypeStruct((n_rows, D), BF16),
                   jax.ShapeDtypeStruct((1, D), F32), jax.ShapeDtypeStruct((1, D), F32),
                   jax.ShapeDtypeStruct((8, 128), F32)],
        compiler_params=_cparams("arbitrary"),
    )(x, f, tgt, g, nw)


def _tile(dim, pref):
    t = min(dim, pref)
    while dim % t:
        t //= 2
    return t


def _matmul(name, a, b, kind, col, out_dtype, add=None, after=None):
    if kind == "wgrad":
        m, kg = a.shape
        ng = b.shape[1]
        r, c = (kg, ng // NCHIP) if col else (kg // NCHIP, ng)
        tm, tn, tk = _tile(r, 1024), _tile(c, 1536), _tile(m, 2048)
        grid = (kg // tm, ng // tn, m // tk)
        a_spec = pl.BlockSpec((tk, tm), lambda i, j, k: (k, i))
        b_spec = pl.BlockSpec((tk, tn), lambda i, j, k: (k, j))
        if col:
            o_spec = pl.BlockSpec((None, tm, tn), lambda i, j, k: (j // (c // tn), i, j % (c // tn)))
        else:
            o_spec = pl.BlockSpec((None, tm, tn), lambda i, j, k: (i // (r // tm), i % (r // tm), j))
        out_shape = jax.ShapeDtypeStruct((NCHIP, r, c), out_dtype)
        dims = TN
    else:
        _, r, c = b.shape
        m = a.shape[0]
        kg, ng = (r, NCHIP * c) if col else (NCHIP * r, c)
        tm = _tile(m, 1024)
        if kind == "fwd":
            tn, tk = _tile(c, 1536), _tile(r, 2048)
            grid = (m // tm, ng // tn, kg // tk)
            a_spec = pl.BlockSpec((tm, tk), lambda i, j, k: (i, k))
            if col:
                b_spec = pl.BlockSpec((None, tk, tn), lambda i, j, k: (j // (c // tn), k, j % (c // tn)))
            else:
                b_spec = pl.BlockSpec((None, tk, tn), lambda i, j, k: (k // (r // tk), k % (r // tk), j))
            out_shape = jax.ShapeDtypeStruct((m, ng), out_dtype)
            dims = NN
        else:
            tn, tk = _tile(r, 1024), _tile(c, 2048)
            grid = (m // tm, kg // tn, ng // tk)
            a_spec = pl.BlockSpec((tm, tk), lambda i, j, k: (i, k))
            if col:
                b_spec = pl.BlockSpec((None, tn, tk), lambda i, j, k: (k // (c // tk), j, k % (c // tk)))
            else:
                b_spec = pl.BlockSpec((None, tn, tk), lambda i, j, k: (j // (r // tn), j % (r // tn), k))
            out_shape = jax.ShapeDtypeStruct((m, kg), out_dtype)
            dims = NT
        o_spec = pl.BlockSpec((tm, tn), lambda i, j, k: (i, j))
    nk = grid[2]
    has_add = add is not None
    extra = [] if after is None else [after]

    def body(*refs):
        a_ref, b_ref = refs[0], refs[1]
        o_ref = refs[2 + has_add + len(extra)]
        k = pl.program_id(2)
        p = lax.dot_general(a_ref[...].astype(BF16), b_ref[...].astype(BF16), dims, preferred_element_type=F32)

        def finish(r_):
            if has_add:
                r_ = r_ + refs[2][...]
            o_ref[...] = r_.astype(o_ref.dtype)

        if nk == 1:
            finish(p)
        else:
            acc_ref = refs[-1]

            @pl.when(k == 0)
            def _():
                acc_ref[...] = p

            @pl.when(jnp.logical_and(k > 0, k < nk - 1))
            def _():
                acc_ref[...] += p

            @pl.when(k == nk - 1)
            def _():
                finish(acc_ref[...] + p)

    in_specs = [a_spec, b_spec] + ([pl.BlockSpec((tm, tn), lambda i, j, k: (i, j))] if has_add else [])
    in_specs += [pl.BlockSpec(memory_space=pl.ANY)] * len(extra)
    args = (a, b) + ((add,) if has_add else ()) + tuple(extra)
    return pl.pallas_call(
        body, name=name, grid=grid, in_specs=in_specs, out_specs=o_spec, out_shape=out_shape,
        scratch_shapes=[pltpu.VMEM((tm, tn), F32)] if nk > 1 else [],
        compiler_params=_cparams("parallel", "parallel", "arbitrary"),
    )(*args)


def _ffn_up(name, h, wg, wu):
    m, k = h.shape
    _, _, c = wg.shape
    n = NCHIP * c
    tm, tn = _tile(m, 1024), _tile(c, 1024)

    def body(h_ref, wg_ref, wu_ref, a_ref, b_ref, s_ref):
        hv = h_ref[...]
        a = jnp.dot(hv, wg_ref[...], preferred_element_type=F32)
        b = jnp.dot(hv, wu_ref[...], preferred_element_type=F32)
        a_ref[...] = a
        b_ref[...] = b
        s_ref[...] = (_silu(a) * b).astype(s_ref.dtype)

    w_spec = pl.BlockSpec((None, k, tn), lambda i, j: (j // (c // tn), 0, j % (c // tn)))
    o_spec = pl.BlockSpec((tm, tn), lambda i, j: (i, j))
    return pl.pallas_call(
        body, name=name, grid=(m // tm, n // tn),
        in_specs=[pl.BlockSpec((tm, k), lambda i, j: (i, 0)), w_spec, w_spec],
        out_specs=[o_spec, o_spec, o_spec],
        out_shape=[jax.ShapeDtypeStruct((m, n), F32), jax.ShapeDtypeStruct((m, n), F32),
                   jax.ShapeDtypeStruct((m, n), BF16)],
        compiler_params=_cparams("parallel", "parallel"),
    )(h, wg, wu)


def _dgrad_glu(name, df, wd, a, b, after=None):
    m = df.shape[0]
    _, r, c = wd.shape
    tm, tn = _tile(m, 512), _tile(r, 1024)
    extra = [] if after is None else [after]

    def body(df_ref, w_ref, a_ref, b_ref, *rest):
        da_ref, db_ref = rest[-2], rest[-1]
        ds = lax.dot_general(df_ref[...], w_ref[...], NT, preferred_element_type=F32)
        _, vjp = jax.vjp(_f_glu, a_ref[...], b_ref[...])
        da, db = vjp(ds)
        da_ref[...] = da.astype(da_ref.dtype)
        db_ref[...] = db.astype(db_ref.dtype)

    t_spec = pl.BlockSpec((tm, tn), lambda i, j: (i, j))
    sds = jax.ShapeDtypeStruct((m, NCHIP * r), BF16)
    return pl.pallas_call(
        body, name=name, grid=(m // tm, NCHIP * r // tn),
        in_specs=[pl.BlockSpec((tm, c), lambda i, j: (i, 0)),
                  pl.BlockSpec((None, tn, c), lambda i, j: (j // (r // tn), j % (r // tn), 0)), t_spec, t_spec]
        + [pl.BlockSpec(memory_space=pl.ANY)] * len(extra),
        out_specs=[t_spec, t_spec], out_shape=[sds, sds],
        compiler_params=_cparams("parallel", "parallel"),
    )(df, wd, a, b, *extra)


def _sgu_fwd(name, proj, b_in, lng, lnb, wm, bst, tr):
    n_rows = proj.shape[0]

    def body(uv_ref, buv_ref, g_ref, b_ref, wm_ref, bst_ref, ya_ref):
        u, vln = _f_mixa(uv_ref[...], buv_ref[...], g_ref[...], b_ref[...])
        vb = vln.astype(BF16)
        for ck in range(tr // CHUNK):
            rs = slice(ck * CHUNK, (ck + 1) * CHUNK)
            for h in range(HEADS):
                cs = slice(h * CHUNK, (h + 1) * CHUNK)
                vm = jnp.dot(wm_ref[h], vb[rs, cs], preferred_element_type=F32) + bst_ref[:, cs]
                ya_ref[rs, cs] = (u[rs, cs] * vm).astype(ya_ref.dtype)

    return pl.pallas_call(
        body, name=name, grid=(n_rows // tr,),
        in_specs=[pl.BlockSpec((tr, 2 * D), lambda i: (i, 0)), pl.BlockSpec((1, 2 * D), lambda i: (0, 0)),
                  pl.BlockSpec((1, D), lambda i: (0, 0)), pl.BlockSpec((1, D), lambda i: (0, 0)),
                  pl.BlockSpec((HEADS, CHUNK, CHUNK), lambda i: (0, 0, 0)),
                  pl.BlockSpec((CHUNK, D), lambda i: (0, 0))],
        out_specs=pl.BlockSpec((tr, D), lambda i: (i, 0)),
        out_shape=jax.ShapeDtypeStruct((n_rows, D), BF16),
        compiler_params=_cparams("parallel"),
    )(proj, b_in, lng, lnb, wm, bst)


def _sgu_bwd(name, proj, b_in, lng, lnb, wm, wmt, bst, dya, tr):
    n_rows = proj.shape[0]

    def body(uv_ref, buv_ref, g_ref, b_ref, wm_ref, wmt_ref, bst_ref, dya_ref,
             duv_ref, dbuv_ref, dg_ref, db_ref, dws_ref, dbs_ref, du_s, dvln_s):
        i = pl.program_id(0)

        @pl.when(i == 0)
        def _():
            dbuv_ref[...] = jnp.zeros(dbuv_ref.shape, F32)
            dg_ref[...] = jnp.zeros(dg_ref.shape, F32)
            db_ref[...] = jnp.zeros(db_ref.shape, F32)
            dws_ref[...] = jnp.zeros(dws_ref.shape, F32)
            dbs_ref[...] = jnp.zeros(dbs_ref.shape, F32)

        (u, vln), vjp = jax.vjp(_f_mixa, uv_ref[...], buv_ref[...], g_ref[...], b_ref[...])
        vb = vln.astype(BF16)
        dya_v = dya_ref[...].astype(F32)
        tpos = lax.broadcasted_iota(jnp.int32, (CHUNK, CHUNK), 0)
        spos = lax.broadcasted_iota(jnp.int32, (CHUNK, CHUNK), 1)
        causal = (tpos >= spos).astype(F32)
        for ck in range(tr // CHUNK):
            rs = slice(ck * CHUNK, (ck + 1) * CHUNK)
            for h in range(HEADS):
                cs = slice(h * CHUNK, (h + 1) * CHUNK)
                vbh = vb[rs, cs]
                vm = jnp.dot(wm_ref[h], vbh, preferred_element_type=F32) + bst_ref[:, cs]
                dyb = dya_v[rs, cs]
                du_s[rs, cs] = dyb * vm
                dvm = dyb * u[rs, cs]
                dvmb = dvm.astype(BF16)
                dvln_s[rs, cs] = jnp.dot(wmt_ref[h], dvmb, preferred_element_type=F32)
                dws_ref[h] += causal * lax.dot_general(dvmb, vbh, NT, preferred_element_type=F32)
                dbs_ref[:, cs] += jnp.broadcast_to(jnp.sum(dvm, axis=1, keepdims=True), (CHUNK, CHUNK))
        duv, dbuv, dg, db = vjp((du_s[...], dvln_s[...]))
        duv_ref[...] = duv.astype(duv_ref.dtype)
        dbuv_ref[...] += dbuv
        dg_ref[...] += dg
        db_ref[...] += db

    par = pl.BlockSpec((1, D), lambda i: (0, 0))
    par2 = pl.BlockSpec((1, 2 * D), lambda i: (0, 0))
    w_spec = pl.BlockSpec((HEADS, CHUNK, CHUNK), lambda i: (0, 0, 0))
    b_spec = pl.BlockSpec((CHUNK, D), lambda i: (0, 0))
    return pl.pallas_call(
        body, name=name, grid=(n_rows // tr,),
        in_specs=[pl.BlockSpec((tr, 2 * D), lambda i: (i, 0)), par2, par, par, w_spec, w_spec, b_spec,
                  pl.BlockSpec((tr, D), lambda i: (i, 0))],
        out_specs=[pl.BlockSpec((tr, 2 * D), lambda i: (i, 0)), par2, par, par, w_spec, b_spec],
        out_shape=[jax.ShapeDtypeStruct((n_rows, 2 * D), BF16), jax.ShapeDtypeStruct((1, 2 * D), F32),
                   jax.ShapeDtypeStruct((1, D), F32), jax.ShapeDtypeStruct((1, D), F32),
                   jax.ShapeDtypeStruct((HEADS, CHUNK, CHUNK), F32), jax.ShapeDtypeStruct((CHUNK, D), F32)],
        scratch_shapes=[pltpu.VMEM((tr, D), F32), pltpu.VMEM((tr, D), F32)],
        compiler_params=_cparams("arbitrary"),
    )(proj, b_in, lng, lnb, wm, wmt, bst, dya)


CT = 256
RB = 128
CV0 = 2 * D // CT
CG0 = 3 * D // CT


def _shift_bank(win):
    return [win] + [pltpu.roll(win, RB + HALO - r, 0) for r in range(1, 8)]


def _shifted(bank, sh):
    lo = 8 * (sh // 8)
    return bank[sh % 8][lo:lo + RB]


def _conv_fwd(name, proj, b_in, w, cb):
    n_rows = proj.shape[0]

    def body(cv_ref, cg_ref, bcv_ref, bcg_ref, w_ref, cb_ref, z1_ref, zp_ref):
        zp_ref[0:HALO, :] = jnp.zeros((HALO, CT), F32)
        zp_ref[HALO:, :] = (cv_ref[...] + bcv_ref[...]) * jax.nn.sigmoid(cg_ref[...] + bcg_ref[...])

        def blk(rb, carry):
            base = pl.multiple_of(rb * RB, RB)
            bank = _shift_bank(zp_ref[pl.ds(base, RB + HALO), :])
            acc = jnp.broadcast_to(cb_ref[...], (RB, CT))
            for k in range(KCONV):
                acc = acc + w_ref[k:k + 1, :] * _shifted(bank, k + 2)
            z1_ref[pl.ds(base, RB), :] = acc
            return carry

        lax.fori_loop(0, n_rows // RB, blk, 0)

    return pl.pallas_call(
        body, name=name, grid=(D // CT,),
        in_specs=[pl.BlockSpec((n_rows, CT), lambda j: (0, CV0 + j)), pl.BlockSpec((n_rows, CT), lambda j: (0, CG0 + j)),
                  pl.BlockSpec((1, CT), lambda j: (0, CV0 + j)), pl.BlockSpec((1, CT), lambda j: (0, CG0 + j)),
                  pl.BlockSpec((HALO, CT), lambda j: (0, j)), pl.BlockSpec((1, CT), lambda j: (0, j))],
        out_specs=pl.BlockSpec((n_rows, CT), lambda j: (0, j)),
        out_shape=jax.ShapeDtypeStruct((n_rows, D), F32),
        scratch_shapes=[pltpu.VMEM((n_rows + HALO, CT), F32)],
        compiler_params=_cparams("parallel"),
    )(proj, proj, b_in, b_in, w, cb)


def _conv_bwd(name, dz1, proj, b_in, w):
    n_rows = proj.shape[0]

    def body(dz_ref, cv_ref, cg_ref, bcv_ref, bcg_ref, w_ref, dcv_ref, dcg_ref, sm_ref, zp_ref, dzp_ref):
        cvb = cv_ref[...] + bcv_ref[...]
        sg = jax.nn.sigmoid(cg_ref[...] + bcg_ref[...])
        zp_ref[0:HALO, :] = jnp.zeros((HALO, CT), F32)
        zp_ref[HALO:, :] = cvb * sg
        dz = dz_ref[...]
        dzp_ref[0:n_rows, :] = dz
        dzp_ref[n_rows:, :] = jnp.zeros((HALO, CT), F32)
        sm_ref[...] = jnp.zeros(sm_ref.shape, F32)
        sm_ref[32:33, :] = jnp.sum(dz, axis=0, keepdims=True)

        def blk(rb, carry):
            base = pl.multiple_of(rb * RB, RB)
            dbank = _shift_bank(dzp_ref[pl.ds(base, RB + HALO), :])
            zbank = _shift_bank(zp_ref[pl.ds(base, RB + HALO), :])
            dzb = dbank[0][0:RB]
            acc = jnp.zeros((RB, CT), F32)
            for k in range(KCONV):
                acc = acc + w_ref[k:k + 1, :] * _shifted(dbank, KCONV - 1 - k)
                sm_ref[k:k + 1, :] += jnp.sum(dzb * _shifted(zbank, k + 2), axis=0, keepdims=True)
            dzp_ref[pl.ds(base, RB), :] = acc
            return carry

        lax.fori_loop(0, n_rows // RB, blk, 0)
        dz0 = dzp_ref[0:n_rows, :]
        dcv = dz0 * sg
        dcg = dz0 * cvb * (sg * (1.0 - sg))
        dcv_ref[...] = dcv.astype(dcv_ref.dtype)
        dcg_ref[...] = dcg.astype(dcg_ref.dtype)
        sm_ref[33:34, :] = jnp.sum(dcv, axis=0, keepdims=True)
        sm_ref[34:35, :] = jnp.sum(dcg, axis=0, keepdims=True)

    col = pl.BlockSpec((n_rows, CT), lambda j: (0, j))
    return pl.pallas_call(
        body, name=name, grid=(D // CT,),
        in_specs=[col, pl.BlockSpec((n_rows, CT), lambda j: (0, CV0 + j)), pl.BlockSpec((n_rows, CT), lambda j: (0, CG0 + j)),
                  pl.BlockSpec((1, CT), lambda j: (0, CV0 + j)), pl.BlockSpec((1, CT), lambda j: (0, CG0 + j)),
                  pl.BlockSpec((HALO, CT), lambda j: (0, j))],
        out_specs=[col, col, pl.BlockSpec((40, CT), lambda j: (0, j))],
        out_shape=[jax.ShapeDtypeStruct((n_rows, D), BF16), jax.ShapeDtypeStruct((n_rows, D), BF16),
                   jax.ShapeDtypeStruct((40, D), F32)],
        scratch_shapes=[pltpu.VMEM((n_rows + HALO, CT), F32), pltpu.VMEM((n_rows + HALO, CT), F32)],
        compiler_params=_cparams("parallel"),
    )(dz1, proj, proj, b_in, b_in, w)


ADA_TN = 768


def _split_bf16(v):
    hi = v.astype(BF16)
    return hi, (v - hi.astype(F32)).astype(BF16)


def _ada_fwd(name, c_all, w, b):
    n = w.shape[1]

    def body(c_ref, w_ref, b_ref, o_ref):
        ch, cl = _split_bf16(_silu(c_ref[...]))
        wh, wl = _split_bf16(w_ref[...])
        acc = jnp.dot(ch, wl, preferred_element_type=F32) + jnp.dot(cl, wh, preferred_element_type=F32)
        o_ref[...] = acc + jnp.dot(ch, wh, preferred_element_type=F32) + b_ref[...]

    return pl.pallas_call(
        body, name=name, grid=(n // ADA_TN,),
        in_specs=[pl.BlockSpec((8, D), lambda j: (0, 0)), pl.BlockSpec((D, ADA_TN), lambda j: (0, j)),
                  pl.BlockSpec((1, ADA_TN), lambda j: (0, j))],
        out_specs=pl.BlockSpec((8, ADA_TN), lambda j: (0, j)),
        out_shape=jax.ShapeDtypeStruct((8, n), F32),
        compiler_params=_cparams("parallel"),
    )(c_all, w, b)


def _ada_wgrad(name, c_all_t, dmod):
    n = dmod.shape[1]

    def body(ct_ref, dm_ref, o_ref):
        ca = _silu(ct_ref[...])
        acc = ca[:, 0:1] * dm_ref[0:1, :]
        for r in range(1, 8):
            acc = acc + ca[:, r:r + 1] * dm_ref[r:r + 1, :]
        o_ref[...] = acc

    return pl.pallas_call(
        body, name=name, grid=(n // ADA_TN,),
        in_specs=[pl.BlockSpec((D, 8), lambda j: (0, 0)), pl.BlockSpec((8, ADA_TN), lambda j: (0, j))],
        out_specs=pl.BlockSpec((D, ADA_TN), lambda j: (0, j)),
        out_shape=jax.ShapeDtypeStruct((D, n), F32),
        compiler_params=_cparams("parallel"),
    )(c_all_t, dmod)


def _adamw(name, w, g, m, v, tr):
    rows, cols = w.shape

    def body(w_ref, g_ref, m_ref, v_ref, g_out, d_ref, nm_ref, nv_ref):
        gv = g_ref[...]
        nm = ADAM_B1 * m_ref[...] + (1.0 - ADAM_B1) * gv
        nv = ADAM_B2 * v_ref[...] + (1.0 - ADAM_B2) * (gv * gv)
        m_hat = nm / (1.0 - ADAM_B1 ** ADAM_STEP)
        v_hat = nv / (1.0 - ADAM_B2 ** ADAM_STEP)
        d_ref[...] = -ADAM_LR * (m_hat / (jnp.sqrt(v_hat) + ADAM_EPS) + ADAM_WD * w_ref[...])
        nm_ref[...] = nm
        nv_ref[...] = nv
        g_out[...] = gv

    spec = pl.BlockSpec((tr, cols), lambda i: (i, 0))
    sds = jax.ShapeDtypeStruct((rows, cols), F32)
    return pl.pallas_call(
        body, name=name, grid=(rows // tr,), in_specs=[spec] * 4, out_specs=[spec] * 4, out_shape=[sds] * 4,
        compiler_params=_cparams("parallel"),
    )(w, g, m, v)


def _sum8(name, stacked, tr):
    n, rows, cols = stacked.shape

    def body(s_ref, o_ref):
        acc = s_ref[0]
        for r in range(1, n):
            acc = acc + s_ref[r]
        o_ref[...] = acc

    return pl.pallas_call(
        body, name=name, grid=(rows // tr,),
        in_specs=[pl.BlockSpec((n, tr, cols), lambda i: (0, i, 0))],
        out_specs=pl.BlockSpec((tr, cols), lambda i: (i, 0)),
        out_shape=jax.ShapeDtypeStruct((rows, cols), F32),
        compiler_params=_cparams("parallel"),
    )(stacked)


def _pair_add(name, g5, other, c_idx, tr):
    nq, _, rows, cols = g5.shape

    def body(c_ref, g_ref, o_ref, p_ref):
        p_ref[...] = (g_ref[...].astype(F32) + o_ref[...].astype(F32)).astype(p_ref.dtype)

    return pl.pallas_call(
        body, name=name,
        grid_spec=pltpu.PrefetchScalarGridSpec(
            num_scalar_prefetch=1, grid=(nq, rows // tr),
            in_specs=[pl.BlockSpec((None, None, tr, cols), lambda qi, i, cr: (qi, cr[0], i, 0)),
                      pl.BlockSpec((None, tr, cols), lambda qi, i, cr: (qi, i, 0))],
            out_specs=pl.BlockSpec((None, tr, cols), lambda qi, i, cr: (qi, i, 0))),
        out_shape=jax.ShapeDtypeStruct((nq, rows, cols), BF16),
        compiler_params=_cparams("parallel", "parallel"),
    )(c_idx, g5, other)


def _chip_add(name, p, recv, qc_idx, tr):
    _, rows, cols = p.shape

    def body(qc_ref, p_ref, r_ref, o_ref):
        acc = p_ref[...].astype(F32)
        for k in range(NCHIP - 1):
            acc = acc + r_ref[k].astype(F32)
        o_ref[...] = acc

    return pl.pallas_call(
        body, name=name,
        grid_spec=pltpu.PrefetchScalarGridSpec(
            num_scalar_prefetch=1, grid=(rows // tr,),
            in_specs=[pl.BlockSpec((None, tr, cols), lambda i, qc: (qc[0], i, 0)),
                      pl.BlockSpec((NCHIP - 1, tr, cols), lambda i, qc: (0, i, 0))],
            out_specs=pl.BlockSpec((None, tr, cols), lambda i, qc: (qc[1], i, 0))),
        out_shape=jax.ShapeDtypeStruct((2, rows, cols), F32),
        compiler_params=_cparams("parallel"),
    )(qc_idx, p, recv)


def _allgather8(name, blk):
    m_per, n = blk.shape

    def body(x_ref, out_ref, send_sems, recv_sems, local_sem):
        x, y, c = lax.axis_index("x"), lax.axis_index("y"), lax.axis_index("c")
        me, sibling = (x, y, c), (x, y, 1 - c)
        chips = [(1 - x, y), (x, 1 - y), (1 - x, 1 - y)]

        def rows(px, py, pc):
            return out_ref.at[pl.ds((4 * px + 2 * py + pc) * m_per, m_per), :]

        def copy(k, block, to, src=None):
            return pltpu.make_async_remote_copy(
                src_ref=rows(*block) if src is None else src, dst_ref=rows(*block),
                send_sem=send_sems.at[k], recv_sem=recv_sems.at[k], device_id=to, device_id_type=MESH)

        mine = pltpu.make_async_copy(x_ref, rows(*me), local_sem)
        mine.start()
        first = [copy(0, me, sibling, src=x_ref)]
        first += [copy(1 + j, me, (*chip, c), src=x_ref) for j, chip in enumerate(chips)]
        for cp in first:
            cp.start()
        passed = [copy(4 + j, (*chip, c), sibling) for j, chip in enumerate(chips)]
        for j, chip in enumerate(chips):
            copy(1 + j, (*chip, c), me).wait_recv()
            passed[j].start()
        copy(0, sibling, me).wait_recv()
        for j, chip in enumerate(chips):
            copy(4 + j, (*chip, 1 - c), me).wait_recv()
        for cp in first + passed:
            cp.wait_send()
        mine.wait()

    return pl.pallas_call(
        body, name=name,
        out_shape=jax.ShapeDtypeStruct((8 * m_per, n), blk.dtype),
        in_specs=[pl.BlockSpec(memory_space=pltpu.VMEM)],
        out_specs=pl.BlockSpec(memory_space=pltpu.VMEM),
        scratch_shapes=[pltpu.SemaphoreType.DMA((7,)), pltpu.SemaphoreType.DMA((7,)), pltpu.SemaphoreType.DMA],
        compiler_params=pltpu.CompilerParams(vmem_limit_bytes=VMEM_LIMIT),
    )(blk)


def _exchange(name, srcs, out_shapes, n_local, n_remote, plan, aliases=None, after=None):
    ni, no = len(srcs), len(out_shapes)
    extra = [] if after is None else [after]

    def body(*refs):
        ins, outs = refs[:ni], refs[ni + len(extra):ni + len(extra) + no]
        send_sems, recv_sems, local_sems = refs[ni + len(extra) + no:]
        x, y, c = lax.axis_index("x"), lax.axis_index("y"), lax.axis_index("c")
        local, remote = plan(ins, outs, x, y, c)
        assert len(local) == n_local and len(remote) == n_remote

        def rcopy(i, dst):
            s, _, peer, _, _ = remote[i]
            return pltpu.make_async_remote_copy(src_ref=s, dst_ref=dst, send_sem=send_sems.at[i],
                                                recv_sem=recv_sems.at[i], device_id=peer, device_id_type=MESH)

        lcs = [pltpu.make_async_copy(s, d, local_sems.at[i]) for i, (s, d) in enumerate(local)]
        for cp in lcs:
            cp.start()
        first = [i for i in range(n_remote) if remote[i][4] is None]
        passed = [i for i in range(n_remote) if remote[i][4] is not None]
        for i in first:
            rcopy(i, remote[i][1]).start()
        arrived = set()
        for i in passed:
            j = remote[i][4]
            rcopy(j, remote[j][3]).wait_recv()
            arrived.add(j)
            rcopy(i, remote[i][1]).start()
        for i in range(n_remote):
            if i not in arrived:
                rcopy(i, remote[i][3]).wait_recv()
        for i in range(n_remote):
            rcopy(i, remote[i][1]).wait_send()
        for cp in lcs:
            cp.wait()

    any_spec = pl.BlockSpec(memory_space=pl.ANY)
    return pl.pallas_call(
        body, name=name, out_shape=out_shapes,
        in_specs=[any_spec] * (ni + len(extra)), out_specs=[any_spec] * no,
        input_output_aliases=aliases or {},
        scratch_shapes=[pltpu.SemaphoreType.DMA((n_remote,)), pltpu.SemaphoreType.DMA((n_remote,)),
                        pltpu.SemaphoreType.DMA((max(n_local, 1),))],
    )(*srcs, *extra)


_CHIP_FLIPS = ((0, 1), (1, 0), (1, 1))


def _flip(v, f):
    return 1 - v if f else v


EFFECT = pltpu.SideEffectType.DATAFLOW_SIDE_EFFECTING


def _split_start(name, bufs, n, plan, after=None):
    nb = len(bufs)
    extra = [] if after is None else [after]

    def body(*refs):
        ins = refs[:nb]
        send_sems, recv_sems = refs[nb + len(extra)], refs[nb + len(extra) + 1]
        token = refs[-1]
        x, y, c = lax.axis_index("x"), lax.axis_index("y"), lax.axis_index("c")
        copies = plan(ins, x, y, c)
        assert len(copies) == n
        for i, (s, d, peer, _) in enumerate(copies):
            pltpu.make_async_remote_copy(src_ref=s, dst_ref=d, send_sem=send_sems.at[i], recv_sem=recv_sems.at[i],
                                         device_id=peer, device_id_type=MESH).start()
        token[...] = jnp.zeros_like(token)

    hbm = pl.BlockSpec(memory_space=pltpu.HBM)
    sem = pl.BlockSpec(memory_space=pltpu.SEMAPHORE)
    res = pl.pallas_call(
        body, name=name,
        out_shape=(pltpu.SemaphoreType.DMA((n,)), pltpu.SemaphoreType.DMA((n,)),
                   *[pltpu.HBM(b.shape, b.dtype) for b in bufs], jax.ShapeDtypeStruct((8, 128), F32)),
        in_specs=[hbm] * nb + [pl.BlockSpec(memory_space=pl.ANY)] * len(extra),
        out_specs=(sem, sem, *[hbm] * nb, pl.BlockSpec(memory_space=pltpu.VMEM)),
        input_output_aliases={i: 2 + i for i in range(nb)},
        compiler_params=pltpu.CompilerParams(has_side_effects=EFFECT),
    )(*[pltpu.with_memory_space_constraint(b, pltpu.HBM) for b in bufs], *extra)
    return res[0], res[1], list(res[2:2 + nb]), res[-1]


def _split_wait(name, bufs, send_sems, recv_sems, after, n, plan):
    nb = len(bufs)

    def body(*refs):
        ins = refs[:nb]
        ssem, rsem = refs[nb], refs[nb + 1]
        x, y, c = lax.axis_index("x"), lax.axis_index("y"), lax.axis_index("c")
        copies = plan(ins, x, y, c)
        assert len(copies) == n
        for i, (s, _, peer, lands) in enumerate(copies):
            cp = pltpu.make_async_remote_copy(src_ref=s, dst_ref=lands, send_sem=ssem.at[i], recv_sem=rsem.at[i],
                                              device_id=peer, device_id_type=MESH)
            cp.wait_send()
            cp.wait_recv()

    hbm = pl.BlockSpec(memory_space=pltpu.HBM)
    sem = pl.BlockSpec(memory_space=pltpu.SEMAPHORE)
    afters = list(after) if isinstance(after, (list, tuple)) else [after]
    res = pl.pallas_call(
        body, name=name,
        out_shape=[pltpu.HBM(b.shape, b.dtype) for b in bufs],
        in_specs=[hbm] * nb + [sem, sem] + [pl.BlockSpec(memory_space=pl.ANY)] * len(afters),
        out_specs=[hbm] * nb,
        input_output_aliases={i: i for i in range(nb)},
        compiler_params=pltpu.CompilerParams(has_side_effects=EFFECT),
    )(*bufs, send_sems, recv_sems, *afters)
    return list(res)


def _chips_of(x, y):
    return [(_flip(x, fx), _flip(y, fy)) for fx, fy in _CHIP_FLIPS]


def _gather_start(tag, stacks, after):
    n = len(stacks)

    def plan(refs, x, y, c):
        q = 2 * x + y
        return [(refs[a].at[q, c], refs[a].at[q, c], (px, py, c), refs[a].at[2 * px + py, c])
                for a in range(n) for px, py in _chips_of(x, y)]

    send, recv, thru, token = _split_start(f"gather_start_{tag}", stacks, 3 * n, plan, after)
    return (send, recv, thru, plan), token


def _gather_finish(tag, state, after):
    send, recv, thru, plan = state
    n = len(thru)
    landed = _split_wait(f"gather_wait_{tag}", thru, send, recv, after, 3 * n, plan)

    def pass_on(ins, outs, x, y, c):
        sib = (x, y, 1 - c)
        remote = []
        for a in range(n):
            for px, py in _chips_of(x, y):
                p = 2 * px + py
                remote.append((ins[a].at[p, c], outs[a].at[p, c], sib, outs[a].at[p, 1 - c], None))
        return [], remote

    full = _exchange(f"gather_pass_{tag}", landed, [jax.ShapeDtypeStruct(s.shape, s.dtype) for s in landed],
                     0, 3 * n, pass_on, aliases={a: a for a in range(n)})
    return [g.reshape(NCHIP, 2 * g.shape[2], g.shape[3]) for g in full]


def _gather_land(tag, state, after):
    send, recv, thru, plan = state
    n = len(thru)
    landed = _split_wait(f"gather_wait_{tag}", thru, send, recv, after, 3 * n, plan)

    def pass_plan(refs, x, y, c):
        sib = (x, y, 1 - c)
        return [(refs[a].at[2 * px + py, c], refs[a].at[2 * px + py, c], sib, refs[a].at[2 * px + py, 1 - c])
                for a in range(n) for px, py in _chips_of(x, y)]

    send, recv, thru, token = _split_start(f"gather_pass_start_{tag}", landed, 3 * n, pass_plan)
    return (send, recv, thru, pass_plan), token


def _gather_done(tag, state, after):
    send, recv, thru, plan = state
    full = _split_wait(f"gather_pass_wait_{tag}", thru, send, recv, after, 3 * len(thru), plan)
    return [g.reshape(NCHIP, 2 * g.shape[2], g.shape[3]) for g in full]


_OTHERS = tuple((fx, fy, fc) for fx in (0, 1) for fy in (0, 1) for fc in (0, 1) if fx or fy or fc)


def _allgather_start(tag, buf, after=None):
    def plan(refs, x, y, c):
        me = 4 * x + 2 * y + c
        copies = []
        for fx, fy, fc in _OTHERS:
            px, py, pc = _flip(x, fx), _flip(y, fy), _flip(c, fc)
            copies.append((refs[0].at[me], refs[0].at[me], (px, py, pc), refs[0].at[4 * px + 2 * py + pc]))
        return copies

    send, recv, thru, token = _split_start(f"{tag}_start", [buf], len(_OTHERS), plan, after)
    return (send, recv, thru, plan), token


def _allgather_done(tag, state, after):
    send, recv, thru, plan = state
    return _split_wait(f"{tag}_wait", thru, send, recv, after, len(_OTHERS), plan)[0]


def _gather_small(name, smalls):
    n = len(smalls)

    def plan(ins, outs, x, y, c):
        q = 2 * x + y
        local = [(ins[b], outs[b].at[q]) for b in range(n)]
        remote = [(ins[b], outs[b].at[q], (px, py, c), outs[b].at[2 * px + py], None)
                  for b in range(n) for px, py in _chips_of(x, y)]
        return local, remote

    return _exchange(name, smalls, [jax.ShapeDtypeStruct((NCHIP,) + s.shape, s.dtype) for s in smalls], n, 3 * n, plan)


def _rs_pair_start(tag, grads, after=None):
    n = len(grads)
    g5 = [g.reshape(NCHIP, 2, g.shape[1] // 2, g.shape[2]) for g in grads]
    half = [(g.shape[2], g.shape[3]) for g in g5]
    land = [lax.empty((NCHIP,) + h, g.dtype) for h, g in zip(half, grads)]

    def plan(refs, x, y, c):
        sib = (x, y, 1 - c)
        return [(refs[a].at[:, 1 - c], refs[n + a], sib, refs[n + a]) for a in range(n)]

    send, recv, thru, token = _split_start(f"rs_pair_start_{tag}", g5 + land, n, plan, after)
    return (send, recv, thru, plan, half), token


def _rs_chips_start(tag, state, after, c_idx):
    send, recv, thru, pair_plan, half = state
    n = len(half)
    landed = _split_wait(f"rs_pair_wait_{tag}", thru, send, recv, after, n, pair_plan)
    g5, got = landed[:n], landed[n:]
    part = [_pair_add(f"rs_pair_add_{tag}{a}", g5[a], got[a], c_idx, _tile(half[a][0], 512)) for a in range(n)]
    land = [lax.empty((NCHIP - 1,) + h, BF16) for h in half]

    def plan(refs, x, y, c):
        return [(refs[a].at[2 * px + py], refs[n + a].at[k], (px, py, c), refs[n + a].at[k])
                for a in range(n) for k, (px, py) in enumerate(_chips_of(x, y))]

    send, recv, thru, token = _split_start(f"rs_chips_start_{tag}", part + land, 3 * n, plan)
    return (send, recv, thru, plan, half), token


def _rs_finish(tag, state, after, qc_idx):
    send, recv, thru, plan, half = state
    n = len(half)
    landed = _split_wait(f"rs_chips_wait_{tag}", thru, send, recv, after, 3 * n, plan)
    red = [_chip_add(f"rs_chip_add_{tag}{a}", landed[a], landed[n + a], qc_idx, _tile(half[a][0], 512)) for a in range(n)]

    def share(ins, outs, x, y, c):
        sib = (x, y, 1 - c)
        return [], [(ins[a].at[c], outs[a].at[c], sib, outs[a].at[1 - c], None) for a in range(n)]

    full = _exchange(f"rs_share_{tag}", red, [jax.ShapeDtypeStruct((2,) + h, F32) for h in half], 0, n, share,
                     aliases={a: a for a in range(n)})
    return [f.reshape(2 * h[0], h[1]) for f, h in zip(full, half)]


def _rs_land(tag, state, after, qc_idx):
    send, recv, thru, plan, half = state
    n = len(half)
    landed = _split_wait(f"rs_chips_wait_{tag}", thru, send, recv, after, 3 * n, plan)
    red = [_chip_add(f"rs_chip_add_{tag}{a}", landed[a], landed[n + a], qc_idx, _tile(half[a][0], 512)) for a in range(n)]

    def share(refs, x, y, c):
        sib = (x, y, 1 - c)
        return [(refs[a].at[c], refs[a].at[c], sib, refs[a].at[1 - c]) for a in range(n)]

    send, recv, thru, token = _split_start(f"rs_share_start_{tag}", red, n, share)
    return (send, recv, thru, share, half), token


def _rs_done(tag, state, after):
    send, recv, thru, plan, half = state
    full = _split_wait(f"rs_share_wait_{tag}", thru, send, recv, after, len(half), plan)
    return [f.reshape(2 * h[0], h[1]) for f, h in zip(full, half)]


def _place(name, blk, idx):
    rows, cols = blk.shape

    def body(i_ref, b_ref, o_ref):
        o_ref[...] = b_ref[...]

    return pl.pallas_call(
        body, name=name,
        grid_spec=pltpu.PrefetchScalarGridSpec(
            num_scalar_prefetch=1, grid=(1,),
            in_specs=[pl.BlockSpec((rows, cols), lambda i, d: (0, 0))],
            out_specs=pl.BlockSpec((None, rows, cols), lambda i, d: (d[0], 0, 0))),
        out_shape=jax.ShapeDtypeStruct((8, rows, cols), blk.dtype),
        compiler_params=_cparams("arbitrary"),
    )(idx, blk)


def _pack(name, n_rows, pieces, after=None):
    arrays = [p[1] for p in pieces if p[0] != "zeros"]
    extra = [] if after is None else [after]

    def body(*refs):
        o_ref = refs[-1]
        o_ref[...] = jnp.zeros(o_ref.shape, F32)
        row, ai = 0, 0
        for p in pieces:
            kind = p[0]
            if kind == "zeros":
                row += p[1]
                continue
            ref = refs[len(extra) + ai]
            ai += 1
            if kind == "rows":
                r = ref.shape[0]
                o_ref[row:row + r, :] = ref[...]
            elif kind == "wide":
                r = ref.shape[1] // D
                for j in range(r):
                    o_ref[row + j:row + j + 1, :] = ref[:, j * D:(j + 1) * D]
            elif kind == "slice":
                r = p[3] - p[2]
                o_ref[row:row + r, :] = ref[p[2]:p[3], :]
            elif kind == "heads":
                r = CHUNK
                for h in range(HEADS):
                    o_ref[row:row + r, h * CHUNK:(h + 1) * CHUNK] = ref[h]
            else:
                r = 1
                o_ref[row:row + 1, 0:128] = ref[0:1, :]
            row += r
        assert row == n_rows, (row, n_rows)

    vmem = pl.BlockSpec(memory_space=pltpu.VMEM)
    return pl.pallas_call(
        body, name=name, out_shape=jax.ShapeDtypeStruct((n_rows, D), F32),
        in_specs=[pl.BlockSpec(memory_space=pl.ANY)] * len(extra) + [vmem] * len(arrays), out_specs=vmem,
        compiler_params=pltpu.CompilerParams(vmem_limit_bytes=VMEM_LIMIT),
    )(*extra, *arrays)


def _to_bf16_stack(name, w, q_idx):
    r, cols = w.shape
    half = r // 2
    tr = _tile(half, 256)
    nb = half // tr

    def body(q_ref, w_ref, o_ref):
        o_ref[...] = w_ref[...].astype(BF16)

    return pl.pallas_call(
        body, name=name,
        grid_spec=pltpu.PrefetchScalarGridSpec(
            num_scalar_prefetch=1, grid=(2 * nb,),
            in_specs=[pl.BlockSpec((tr, cols), lambda i, qr: (i, 0))],
            out_specs=pl.BlockSpec((None, None, tr, cols), lambda i, qr: (qr[0], i // nb, i % nb, 0))),
        out_shape=jax.ShapeDtypeStruct((NCHIP, 2, half, cols), BF16),
        compiler_params=_cparams("parallel"),
    )(q_idx, w)


def kernel(x, c, ada_w, ada_b, norm_ffn1, ffn1_w_gate, ffn1_w_up, ffn1_w_down, norm_mix, mix_w_in, mix_b_in, sgu_ln_g, sgu_ln_b, sgu_w_s, sgu_b_s, conv_w, conv_b, conv_ln_g, conv_ln_b, w_branch_a, w_branch_b, w_out, norm_ffn2, ffn2_w_gate, ffn2_w_up, ffn2_w_down, norm_final, loss_target, m_ada_w, m_ada_b, m_norm_ffn1, m_ffn1_w_gate, m_ffn1_w_up, m_ffn1_w_down, m_norm_mix, m_mix_w_in, m_mix_b_in, m_sgu_ln_g, m_sgu_ln_b, m_sgu_w_s, m_sgu_b_s, m_conv_w, m_conv_b, m_conv_ln_g, m_conv_ln_b, m_w_branch_a, m_w_branch_b, m_w_out, m_norm_ffn2, m_ffn2_w_gate, m_ffn2_w_up, m_ffn2_w_down, m_norm_final, v_ada_w, v_ada_b, v_norm_ffn1, v_ffn1_w_gate, v_ffn1_w_up, v_ffn1_w_down, v_norm_mix, v_mix_w_in, v_mix_b_in, v_sgu_ln_g, v_sgu_ln_b, v_sgu_w_s, v_sgu_b_s, v_conv_w, v_conv_b, v_conv_ln_g, v_conv_ln_b, v_w_branch_a, v_w_branch_b, v_w_out, v_norm_ffn2, v_ffn2_w_gate, v_ffn2_w_up, v_ffn2_w_down, v_norm_final):
    xi, yi, ci = lax.axis_index("x"), lax.axis_index("y"), lax.axis_index("c")
    q = 2 * xi + yi
    dev = 4 * xi + 2 * yi + ci
    c_idx = jnp.reshape(ci, (1,)).astype(jnp.int32)
    q_idx = jnp.reshape(q, (1,)).astype(jnp.int32)
    x0 = x[0]
    tgt = loss_target[0]
    qc_idx = jnp.stack([q, ci]).astype(jnp.int32)
    n_ada = ada_w.shape[2]

    c_all = _allgather8("ag_c", jnp.pad(c, ((0, 7), (0, 0))))[::8]
    ada_b_mine = lax.dynamic_slice(ada_b, (0, q * n_ada), (1, n_ada))
    mod_part = _ada_fwd("ada_fwd", c_all, ada_w[0], ada_b_mine)
    small_parts = _gather_small("gather_small", [mod_part, jnp.pad(conv_w[0], ((0, 1), (0, 0)))])

    big = dict(g1=ffn1_w_gate, u1=ffn1_w_up, d1=ffn1_w_down, win=mix_w_in, wa=w_branch_a, wb=w_branch_b, wo=w_out,
               g2=ffn2_w_gate, u2=ffn2_w_up, d2=ffn2_w_down)
    groups = dict(f1=("d1", "g1", "u1"), mx=("wo", "wa", "wb", "win"), f2=("d2", "g2", "u2"))
    gather_groups = dict(up1=("g1", "u1"), dn1=("d1",), win=("win",), mx=("wa", "wb", "wo"), f2=("g2", "u2", "d2"))
    gather_state = {}
    started = jnp.zeros((1, 1), F32)
    for tag, keys in gather_groups.items():
        gather_state[tag], token = _gather_start(
            tag, [_to_bf16_stack(f"cast_{k}", big[k][0], q_idx) for k in keys], small_parts[0])
        started = started + token[0:1, 0:1]

    mod_all = jnp.transpose(small_parts[0], (1, 0, 2)).reshape(8, N_MOD * D)
    mod = lax.dynamic_slice(mod_all, (dev, 0), (1, N_MOD * D)) + started
    sh1, sc1, g1, sh2, sc2, g2, sh3, sc3, g3 = [mod[:, k * D:(k + 1) * D] for k in range(N_MOD)]
    convw = jnp.transpose(small_parts[1], (1, 0, 2)).reshape(HALO, D)
    W = dict(zip(gather_groups["up1"], _gather_finish("up1", gather_state["up1"], mod)))

    causal = jnp.tril(jnp.ones((CHUNK, CHUNK), bool))
    wm_f = jnp.where(causal[None], sgu_w_s[0], 0.0)
    wm = wm_f.astype(BF16)
    wmt = jnp.swapaxes(wm_f, 1, 2).astype(BF16)
    bst = jnp.repeat(jnp.transpose(sgu_b_s[0]), CHUNK, axis=1)
    b_in = mix_b_in

    (h1,) = _rowwise_fwd("pre1", _f_pre, [x0], [norm_ffn1, sc1, sh1], [(D, BF16)], 256)
    a1, b1, s1 = _ffn_up("up1", h1, W["g1"], W["u1"])
    W.update(zip(gather_groups["dn1"], _gather_finish("dn1", gather_state["dn1"], s1)))
    f1 = _matmul("down1", s1, W["d1"], "fwd", False, F32)
    pass_win, token = _gather_land("win", gather_state["win"], f1)
    x1, h2 = _rowwise_fwd("res1", _f_res_pre(0.5), [x0, f1], [g1, norm_mix + token[0:1, 0:1], sc2, sh2],
                          [(D, F32), (D, BF16)], 256)
    W.update(zip(gather_groups["win"], _gather_done("win", pass_win, x1)))
    pass_mx, token = _gather_land("mx", gather_state["mx"], x1)
    proj = _matmul("in_proj", h2, W["win"], "fwd", True, F32, after=token)
    W.update(zip(gather_groups["mx"], _gather_done("mx", pass_mx, proj)))
    ya = _sgu_fwd("sgu", proj, b_in, sgu_ln_g, sgu_ln_b, wm, bst, 256)
    z1 = _conv_fwd("conv", proj, b_in, convw, conv_b)
    (z3,) = _rowwise_fwd("conv_ln", _f_lnsilu, [z1], [conv_ln_g, conv_ln_b], [(D, BF16)], 256)
    y_a = _matmul("branch_a", ya, W["wa"], "fwd", False, F32)
    y_b = _matmul("branch_b", z3, W["wb"], "fwd", False, F32)
    pass_f2, token = _gather_land("f2", gather_state["f2"], y_b)
    (merged,) = _rowwise_fwd("merge", _f_merge, [(proj, 2 * D, 2), y_a, y_b], [(b_in + token[0:1, 0:1], 2 * D, 2)],
                             [(D, BF16)], 256)
    y = _matmul("out_proj", merged, W["wo"], "fwd", False, F32)
    x2, h3 = _rowwise_fwd("res2", _f_res_pre(1.0), [x1, y], [g2, norm_ffn2, sc3, sh3], [(D, F32), (D, BF16)], 256)
    W.update(zip(gather_groups["f2"], _gather_done("f2", pass_f2, x2)))
    a3, b3, s3 = _ffn_up("up2", h3, W["g2"], W["u2"])
    f3 = _matmul("down2", s3, W["d2"], "fwd", False, F32)
    dx2, df3, dg3, dnwf, loss_blk = _final("final", x2, f3, tgt, g3, norm_final.reshape(1, D), 256)

    G = {}

    def ffn_bwd(tag, df, s, a, b, h, wg, wu, wd, after=None):
        da, db = _dgrad_glu(f"d_down{tag}", df, wd, a, b, after)
        G["d" + tag] = _matmul(f"w_down{tag}", s, df, "wgrad", False, BF16)
        dh = _matmul(f"d_gate{tag}", da, wg, "dgrad", True, F32)
        dh = _matmul(f"d_up{tag}", db, wu, "dgrad", True, F32, add=dh)
        G["g" + tag] = _matmul(f"w_gate{tag}", h, da, "wgrad", True, BF16)
        G["u" + tag] = _matmul(f"w_up{tag}", h, db, "wgrad", True, BF16)
        return dh

    reduced = {}
    dh3 = ffn_bwd("2", df3, s3, a3, b3, h3, W["g2"], W["u2"], W["d2"])
    pair_f2, token = _rs_pair_start("f2", [G[k] for k in groups["f2"]])
    (dx1, dy), (dg2, dnw3, dsc3, dsh3) = _rowwise_bwd(
        "res2_b", _f_res_pre(1.0), [x1, y], [g2 + token[0:1, 0:1], norm_ffn2, sc3, sh3], [dx2, dh3], [F32, BF16], 256)
    dmerged = _matmul("d_out", dy, W["wo"], "dgrad", False, F32)
    G["wo"] = _matmul("w_out", merged, dy, "wgrad", False, BF16)
    rs_f2, token = _rs_chips_start("f2", pair_f2, [dmerged, G["wo"]], c_idx)
    (dgab, dy_a, dy_b), (dbgab,) = _rowwise_bwd(
        "merge_b", _f_merge, [(proj, 2 * D, 2), y_a, y_b], [(b_in + token[0:1, 0:1], 2 * D, 2)], [dmerged],
        [BF16, BF16, BF16], 256)
    dya = _matmul("d_a", dy_a, W["wa"], "dgrad", False, F32)
    dz3 = _matmul("d_b", dy_b, W["wb"], "dgrad", False, F32)
    G["wa"] = _matmul("w_a", ya, dy_a, "wgrad", False, BF16)
    G["wb"] = _matmul("w_b", z3, dy_b, "wgrad", False, BF16)
    (dz1,), (dclg, dclb) = _rowwise_bwd("conv_ln_b", _f_lnsilu, [z1], [conv_ln_g, conv_ln_b], [dz3], [F32], 256)
    dcv, dcg, conv_small = _conv_bwd("conv_b", dz1, proj, b_in, convw)
    duv, dbuv, dlng, dlnb, dws, dbs = _sgu_bwd("sgu_b", proj, b_in, sgu_ln_g, sgu_ln_b, wm, wmt, bst, dya, 256)
    dproj = jnp.concatenate([duv, dcv, dcg, dgab], axis=1)
    dh2 = _matmul("d_in", dproj, W["win"], "dgrad", True, F32)
    G["win"] = _matmul("w_in", h2, dproj, "wgrad", True, BF16)
    pair_mx, token = _rs_pair_start("mx", [G[k] for k in groups["mx"]])
    share_f2, token2 = _rs_land("f2", rs_f2, [dh2, token], qc_idx)

    dev_idx = jnp.reshape(dev, (1,)).astype(jnp.int32)
    early = _pack("pack_early", HALO + CHUNK, [("slice", conv_small, 0, HALO), ("heads", dws)])
    ag_early, token3 = _allgather_start("ag_early", _place("place_early", early, dev_idx), token2)

    (dx0, df1), (dg1, dnw2, dsc2, dsh2) = _rowwise_bwd(
        "res1_b", _f_res_pre(0.5), [x0, f1], [g1 + (token[0:1, 0:1] + token3[0:1, 0:1]), norm_mix, sc2, sh2],
        [dx1, dh2], [F32, BF16], 256)
    rs_mx, token = _rs_chips_start("mx", pair_mx, df1, c_idx)
    dh1 = ffn_bwd("1", df1, s1, a1, b1, h1, W["g1"], W["u1"], W["d1"], after=token)
    pair_f1, token = _rs_pair_start("f1", [G[k] for k in groups["f1"]])
    reduced.update(zip(groups["f2"], _rs_done("f2", share_f2, dh1)))
    share_mx, token2 = _rs_land("mx", rs_mx, [dh1, token], qc_idx)
    (grad_x,), (dnw1, dsc1, dsh1) = _rowwise_bwd(
        "pre1_b", _f_pre_keep, [x0], [norm_ffn1, sc1 + (token[0:1, 0:1] + token2[0:1, 0:1]), sh1], [dx0, dh1], [F32], 256)
    reduced.update(zip(groups["mx"], _rs_done("mx", share_mx, grad_x)))
    tot_early = _sum8("sum_early", _allgather_done("ag_early", ag_early, grad_x), HALO)

    dbs_row = jnp.transpose(dbs[:, ::CHUNK]).reshape(1, D)
    small = _pack("pack_small", HALO, [
        ("rows", dnw1), ("rows", dnw2), ("wide", dbuv), ("slice", conv_small, 33, 34), ("slice", conv_small, 34, 35),
        ("wide", dbgab), ("rows", dlng), ("rows", dlnb), ("rows", dbs_row), ("slice", conv_small, 32, 33),
        ("rows", dclg), ("rows", dclb), ("rows", dnw3), ("rows", dnwf),
        ("rows", dsh1), ("rows", dsc1), ("rows", dg1), ("rows", dsh2), ("rows", dsc2), ("rows", dg2),
        ("rows", dsh3), ("rows", dsc3), ("rows", dg3), ("lanes", loss_blk), ("zeros", 6)])
    every = _allgather8("ag_small", small)
    rs_f1, token = _rs_chips_start("f1", pair_f1, every, c_idx)
    every = (every + token[0:1, 0:1]).reshape(8, HALO, D)
    tot = _sum8("sum_small", every, HALO)
    loss = tot[25, 0]
    dmod_all = every[:, 16:16 + N_MOD, :].reshape(8, N_MOD * D)
    dmod_mine = lax.dynamic_slice(dmod_all, (0, q * n_ada), (8, n_ada))
    g_ada_w = _ada_wgrad("ada_wgrad", jnp.transpose(c_all), dmod_mine)

    given = dict(ada_w=(ada_w, m_ada_w, v_ada_w), ada_b=(ada_b, m_ada_b, v_ada_b),
                 norm_ffn1=(norm_ffn1, m_norm_ffn1, v_norm_ffn1),
                 ffn1_w_gate=(ffn1_w_gate, m_ffn1_w_gate, v_ffn1_w_gate), ffn1_w_up=(ffn1_w_up, m_ffn1_w_up, v_ffn1_w_up),
                 ffn1_w_down=(ffn1_w_down, m_ffn1_w_down, v_ffn1_w_down), norm_mix=(norm_mix, m_norm_mix, v_norm_mix),
                 mix_w_in=(mix_w_in, m_mix_w_in, v_mix_w_in), mix_b_in=(mix_b_in, m_mix_b_in, v_mix_b_in),
                 sgu_ln_g=(sgu_ln_g, m_sgu_ln_g, v_sgu_ln_g), sgu_ln_b=(sgu_ln_b, m_sgu_ln_b, v_sgu_ln_b),
                 sgu_w_s=(sgu_w_s, m_sgu_w_s, v_sgu_w_s), sgu_b_s=(sgu_b_s, m_sgu_b_s, v_sgu_b_s),
                 conv_w=(conv_w, m_conv_w, v_conv_w), conv_b=(conv_b, m_conv_b, v_conv_b),
                 conv_ln_g=(conv_ln_g, m_conv_ln_g, v_conv_ln_g), conv_ln_b=(conv_ln_b, m_conv_ln_b, v_conv_ln_b),
                 w_branch_a=(w_branch_a, m_w_branch_a, v_w_branch_a), w_branch_b=(w_branch_b, m_w_branch_b, v_w_branch_b),
                 w_out=(w_out, m_w_out, v_w_out), norm_ffn2=(norm_ffn2, m_norm_ffn2, v_norm_ffn2),
                 ffn2_w_gate=(ffn2_w_gate, m_ffn2_w_gate, v_ffn2_w_gate), ffn2_w_up=(ffn2_w_up, m_ffn2_w_up, v_ffn2_w_up),
                 ffn2_w_down=(ffn2_w_down, m_ffn2_w_down, v_ffn2_w_down), norm_final=(norm_final, m_norm_final, v_norm_final))
    out = {}

    def big_update(name, g2d):
        w, m, v = given[name]
        shp = w.shape
        res = _adamw(f"adamw_{name}", w.reshape(g2d.shape), g2d, m.reshape(g2d.shape), v.reshape(g2d.shape),
                     _tile(g2d.shape[0], 512 if g2d.shape[1] <= D else 256))
        out[name] = tuple(t.reshape(shp) for t in res)

    big_names = dict(g1="ffn1_w_gate", u1="ffn1_w_up", d1="ffn1_w_down", win="mix_w_in", wa="w_branch_a",
                     wb="w_branch_b", wo="w_out", g2="ffn2_w_gate", u2="ffn2_w_up", d2="ffn2_w_down")
    for key in groups["f2"] + groups["mx"]:
        big_update(big_names[key], reduced[key])
    big_update("ada_w", g_ada_w)

    small_rows = [("norm_ffn1", 0, 1), ("norm_mix", 1, 1), ("mix_b_in", 2, 6), ("sgu_ln_g", 8, 1), ("sgu_ln_b", 9, 1),
                  ("sgu_b_s", 10, 1), ("conv_b", 11, 1), ("conv_ln_g", 12, 1), ("conv_ln_b", 13, 1),
                  ("norm_ffn2", 14, 1), ("norm_final", 15, 1), ("ada_b", 16, N_MOD)]

    def pack_rows(which, label):
        pieces = []
        for nm, _, r in small_rows:
            t = given[nm][which]
            pieces.append(("wide", t) if r > 1 else ("rows", t.reshape(1, D)))
        return _pack(f"pack_{label}", HALO, pieces + [("zeros", HALO - 16 - N_MOD)])

    small_res = _adamw("adamw_small", pack_rows(0, "w"), tot, pack_rows(1, "m"), pack_rows(2, "v"), HALO)
    for nm, r0, r in small_rows:
        shp = given[nm][0].shape
        out[nm] = tuple(t[r0:r0 + r].reshape(shp) for t in small_res)
    n_cw = conv_w.shape[2]
    g_cw = lax.dynamic_slice(tot_early, (0, q * n_cw), (HALO, n_cw))
    pad_cw = lambda t: jnp.pad(t[0], ((0, 1), (0, 0)))
    cw_res = _adamw("adamw_conv_w", pad_cw(conv_w), g_cw, pad_cw(m_conv_w), pad_cw(v_conv_w), HALO)
    out["conv_w"] = tuple(t[:KCONV][None] for t in cw_res)
    g_ws = jnp.transpose(tot_early[HALO:].reshape(CHUNK, HEADS, CHUNK), (1, 0, 2)).reshape(HEADS * CHUNK, CHUNK)
    flat_ws = lambda t: t.reshape(HEADS * CHUNK, CHUNK)
    ws_res = _adamw("adamw_sgu_w_s", flat_ws(sgu_w_s), g_ws, flat_ws(m_sgu_w_s), flat_ws(v_sgu_w_s), 512)
    out["sgu_w_s"] = tuple(t.reshape(sgu_w_s.shape) for t in ws_res)

    done = [out[n][1] for n in ("ada_w", "sgu_w_s", "conv_w", "ada_b")] + [out[big_names[k]][1] for k in groups["f2"] + groups["mx"]]
    reduced.update(zip(groups["f1"], _rs_finish("f1", rs_f1, done, qc_idx)))
    for key in groups["f1"]:
        big_update(big_names[key], reduced[key])

    order = ["ada_w", "ada_b", "norm_ffn1", "ffn1_w_gate", "ffn1_w_up", "ffn1_w_down", "norm_mix", "mix_w_in", "mix_b_in",
             "sgu_ln_g", "sgu_ln_b", "sgu_w_s", "sgu_b_s", "conv_w", "conv_b", "conv_ln_g", "conv_ln_b", "w_branch_a",
             "w_branch_b", "w_out", "norm_ffn2", "ffn2_w_gate", "ffn2_w_up", "ffn2_w_down", "norm_final"]
    return (loss, grad_x[None], *[out[n][0] for n in order], *[out[n][1] for n in order],
            *[out[n][2] for n in order], *[out[n][3] for n in order])
```

```python
import functools

import jax
import jax.numpy as jnp
from jax import lax
from jax.experimental import pallas as pl
from jax.experimental.pallas import tpu as pltpu

F32 = jnp.float32
BF16 = jnp.bfloat16
D = 1024
HEADS = 8
CHUNK = 128
KCONV = 31
HALO = 32
N_MOD = 9
EPS = 1e-6
NCHIP = 4
VMEM_LIMIT = 56 * 1024 * 1024
MESH = pl.DeviceIdType.MESH

ADAM_LR = 0.001
ADAM_B1 = 0.9
ADAM_B2 = 0.999
ADAM_EPS = 1e-08
ADAM_WD = 0.01
ADAM_STEP = 10

NN = (((1,), (0,)), ((), ()))
NT = (((1,), (1,)), ((), ()))
TN = (((0,), (0,)), ((), ()))


def _cparams(*sem):
    return pltpu.CompilerParams(dimension_semantics=sem or None, vmem_limit_bytes=VMEM_LIMIT)


def _rms(x, w):
    return x * lax.rsqrt(jnp.mean(x * x, axis=-1, keepdims=True) + EPS) * w


def _ln(x, g, b):
    mu = jnp.mean(x, axis=-1, keepdims=True)
    xc = x - mu
    var = jnp.mean(xc * xc, axis=-1, keepdims=True)
    return xc * lax.rsqrt(var + EPS) * g + b


def _silu(x):
    return x * jax.nn.sigmoid(x)


def _gelu(x):
    return x * (0.5 * (1.0 + jnp.tanh(0.7978845608028654 * (x + 0.044715 * (x * x * x)))))


def _f_pre(x, nw, sc, sh):
    return _rms(x, nw) * (1.0 + sc) + sh


def _f_pre_keep(x, nw, sc, sh):
    return x, _f_pre(x, nw, sc, sh)


def _f_res_pre(scale):
    def fn(x, f, g, nw, sc, sh):
        xn = x + (scale * g) * f
        return xn, _f_pre(xn, nw, sc, sh)
    return fn


def _f_glu(a, b):
    return _silu(a) * b


def _f_merge(gab, ya, yb, bgab):
    g = gab + bgab
    return jax.nn.sigmoid(g[:, :D]) * ya + jax.nn.sigmoid(g[:, D:]) * yb


def _f_lnsilu(z, g, b):
    return _silu(_ln(z, g, b))


def _f_mixa(uv, buv, g, b):
    t = uv + buv
    return _gelu(t[:, :D]), _ln(_gelu(t[:, D:]), g, b)


def _col(t):
    return t if isinstance(t, tuple) else (t, t.shape[-1], 0)


def _afters(after):
    if after is None:
        return []
    return list(after) if isinstance(after, (list, tuple)) else [after]


def _rowwise_fwd(name, fn, rows, params, outs, tr, after=None):
    rows = [_col(r) for r in rows]
    params = [_col(p) for p in params]
    extra = _afters(after)
    n_rows = rows[0][0].shape[0]
    nin = len(rows) + len(params)

    def body(*refs):
        res = fn(*[r[...].astype(F32) for r in refs[:nin]])
        res = res if isinstance(res, tuple) else (res,)
        for o, v in zip(refs[nin + len(extra):], res, strict=True):
            o[...] = v.astype(o.dtype)

    return pl.pallas_call(
        body, name=name, grid=(n_rows // tr,),
        in_specs=[pl.BlockSpec((tr, w), functools.partial(lambda cb, i: (i, cb), cb)) for _, w, cb in rows]
        + [pl.BlockSpec((1, w), functools.partial(lambda cb, i: (0, cb), cb)) for _, w, cb in params]
        + [pl.BlockSpec(memory_space=pl.ANY)] * len(extra),
        out_specs=[pl.BlockSpec((tr, w), lambda i: (i, 0)) for w, _ in outs],
        out_shape=[jax.ShapeDtypeStruct((n_rows, w), dt) for w, dt in outs],
        compiler_params=_cparams("parallel"),
    )(*[r[0] for r in rows], *[p[0] for p in params], *extra)


def _rowwise_bwd(name, fn, rows, params, cots, row_grads, tr, after=None):
    rows = [_col(r) for r in rows]
    params = [_col(p) for p in params]
    cots = [_col(t) for t in cots]
    extra = _afters(after)
    n_rows = rows[0][0].shape[0]
    nr, npar, nc = len(rows), len(params), len(cots)
    nin = nr + npar + nc
    n_rg = sum(dt is not None for dt in row_grads)

    def body(*refs):
        i = pl.program_id(0)
        prim = [r[...].astype(F32) for r in refs[:nr + npar]]
        ct = tuple(r[...].astype(F32) for r in refs[nr + npar:nin])
        _, vjp = jax.vjp(fn, *prim)
        g = vjp(ct if nc > 1 else ct[0])
        outs = refs[nin + len(extra):]
        oi = 0
        for j, dt in enumerate(row_grads):
            if dt is not None:
                outs[oi][...] = g[j].astype(dt)
                oi += 1
        for j in range(npar):
            acc = outs[n_rg + j]

            @pl.when(i == 0)
            def _(acc=acc):
                acc[...] = jnp.zeros(acc.shape, F32)

            acc[...] += g[nr + j]

    res = pl.pallas_call(
        body, name=name, grid=(n_rows // tr,),
        in_specs=[pl.BlockSpec((tr, w), functools.partial(lambda cb, i: (i, cb), cb)) for _, w, cb in rows]
        + [pl.BlockSpec((1, w), functools.partial(lambda cb, i: (0, cb), cb)) for _, w, cb in params]
        + [pl.BlockSpec((tr, w), functools.partial(lambda cb, i: (i, cb), cb)) for _, w, cb in cots]
        + [pl.BlockSpec(memory_space=pl.ANY)] * len(extra),
        out_specs=[pl.BlockSpec((tr, rows[j][1]), lambda i: (i, 0)) for j, dt in enumerate(row_grads) if dt is not None]
        + [pl.BlockSpec((1, w), lambda i: (0, 0)) for _, w, _ in params],
        out_shape=[jax.ShapeDtypeStruct((n_rows, rows[j][1]), dt) for j, dt in enumerate(row_grads) if dt is not None]
        + [jax.ShapeDtypeStruct((1, w), F32) for _, w, _ in params],
        compiler_params=_cparams("arbitrary"),
    )(*[r[0] for r in rows], *[p[0] for p in params], *[t[0] for t in cots], *extra)
    return res[:n_rg], res[n_rg:]


def _final(name, x, f, tgt, g, nw, tr):
    n_rows = x.shape[0]

    def body(x_ref, f_ref, t_ref, g_ref, nw_ref, dx_ref, df_ref, dg_ref, dnw_ref, loss_ref):
        i = pl.program_id(0)
        tg = t_ref[...]

        def fn(xv, fv, gv, nwv):
            e = _rms(xv + (0.5 * gv) * fv, nwv) - tg
            return 0.5 * jnp.mean(e * e, axis=-1, keepdims=True)

        per_row, vjp = jax.vjp(fn, x_ref[...], f_ref[...], g_ref[...], nw_ref[...])
        dx, df, dg, dnw = vjp(jnp.ones_like(per_row))
        dx_ref[...] = dx
        df_ref[...] = df.astype(df_ref.dtype)

        @pl.when(i == 0)
        def _():
            dg_ref[...] = jnp.zeros(dg_ref.shape, F32)
            dnw_ref[...] = jnp.zeros(dnw_ref.shape, F32)
            loss_ref[...] = jnp.zeros(loss_ref.shape, F32)

        dg_ref[...] += dg
        dnw_ref[...] += dnw
        loss_ref[...] += jnp.broadcast_to(jnp.sum(per_row, axis=0, keepdims=True), loss_ref.shape)

    row = pl.BlockSpec((tr, D), lambda i: (i, 0))
    par = pl.BlockSpec((1, D), lambda i: (0, 0))
    return pl.pallas_call(
        body, name=name, grid=(n_rows // tr,),
        in_specs=[row, row, row, par, par],
        out_specs=[row, row, par, par, pl.BlockSpec((8, 128), lambda i: (0, 0))],
        out_shape=[jax.ShapeDtypeStruct((n_rows, D), F32), jax.ShapeDtypeStruct((n_rows, D), BF16),
                   jax.ShapeDtypeStruct((1, D), F32), jax.ShapeDtypeStruct((1, D), F32),
                   jax.ShapeDtypeStruct((8, 128), F32)],
        compiler_params=_cparams("arbitrary"),
    )(x, f, tgt, g, nw)


def _tile(dim, pref):
    t = min(dim, pref)
    while dim % t:
        t //= 2
    return t


def _matmul(name, a, b, kind, col, out_dtype, add=None, after=None):
    if kind == "wgrad":
        m, kg = a.shape
        ng = b.shape[1]
        r, c = (kg, ng // NCHIP) if col else (kg // NCHIP, ng)
        tm, tn, tk = _tile(r, 1024), _tile(c, 1536), _tile(m, 2048)
        grid = (kg // tm, ng // tn, m // tk)
        a_spec = pl.BlockSpec((tk, tm), lambda i, j, k: (k, i))
        b_spec = pl.BlockSpec((tk, tn), lambda i, j, k: (k, j))
        if col:
            o_spec = pl.BlockSpec((None, tm, tn), lambda i, j, k: (j // (c // tn), i, j % (c // tn)))
        else:
            o_spec = pl.BlockSpec((None, tm, tn), lambda i, j, k: (i // (r // tm), i % (r // tm), j))
        out_shape = jax.ShapeDtypeStruct((NCHIP, r, c), out_dtype)
        dims = TN
    else:
        _, r, c = b.shape
        m = a.shape[0]
        kg, ng = (r, NCHIP * c) if col else (NCHIP * r, c)
        tm = _tile(m, 1024)
        if kind == "fwd":
            tn, tk = _tile(c, 1536), _tile(r, 2048)
            grid = (m // tm, ng // tn, kg // tk)
            a_spec = pl.BlockSpec((tm, tk), lambda i, j, k: (i, k))
            if col:
                b_spec = pl.BlockSpec((None, tk, tn), lambda i, j, k: (j // (c // tn), k, j % (c // tn)))
            else:
                b_spec = pl.BlockSpec((None, tk, tn), lambda i, j, k: (k // (r // tk), k % (r // tk), j))
            out_shape = jax.ShapeDtypeStruct((m, ng), out_dtype)
            dims = NN
        else:
            tn, tk = _tile(r, 1024), _tile(c, 2048)
            grid = (m // tm, kg // tn, ng // tk)
            a_spec = pl.BlockSpec((tm, tk), lambda i, j, k: (i, k))
            if col:
                b_spec = pl.BlockSpec((None, tn, tk), lambda i, j, k: (k // (c // tk), j, k % (c // tk)))
            else:
                b_spec = pl.BlockSpec((None, tn, tk), lambda i, j, k: (j // (r // tn), j % (r // tn), k))
            out_shape = jax.ShapeDtypeStruct((m, kg), out_dtype)
            dims = NT
        o_spec = pl.BlockSpec((tm, tn), lambda i, j, k: (i, j))
    nk = grid[2]
    has_add = add is not None
    extra = [] if after is None else [after]

    def body(*refs):
        a_ref, b_ref = refs[0], refs[1]
        o_ref = refs[2 + has_add + len(extra)]
        k = pl.program_id(2)
        p = lax.dot_general(a_ref[...].astype(BF16), b_ref[...].astype(BF16), dims, preferred_element_type=F32)

        def finish(r_):
            if has_add:
                r_ = r_ + refs[2][...]
            o_ref[...] = r_.astype(o_ref.dtype)

        if nk == 1:
            finish(p)
        else:
            acc_ref = refs[-1]

            @pl.when(k == 0)
            def _():
                acc_ref[...] = p

            @pl.when(jnp.logical_and(k > 0, k < nk - 1))
            def _():
                acc_ref[...] += p

            @pl.when(k == nk - 1)
            def _():
                finish(acc_ref[...] + p)

    in_specs = [a_spec, b_spec] + ([pl.BlockSpec((tm, tn), lambda i, j, k: (i, j))] if has_add else [])
    in_specs += [pl.BlockSpec(memory_space=pl.ANY)] * len(extra)
    args = (a, b) + ((add,) if has_add else ()) + tuple(extra)
    return pl.pallas_call(
        body, name=name, grid=grid, in_specs=in_specs, out_specs=o_spec, out_shape=out_shape,
        scratch_shapes=[pltpu.VMEM((tm, tn), F32)] if nk > 1 else [],
        compiler_params=_cparams("parallel", "parallel", "arbitrary"),
    )(*args)


def _ffn_up(name, h, wg, wu):
    m, k = h.shape
    _, _, c = wg.shape
    n = NCHIP * c
    tm, tn = _tile(m, 1024), _tile(c, 1024)

    def body(h_ref, wg_ref, wu_ref, a_ref, b_ref, s_ref):
        hv = h_ref[...]
        a = jnp.dot(hv, wg_ref[...], preferred_element_type=F32)
        b = jnp.dot(hv, wu_ref[...], preferred_element_type=F32)
        a_ref[...] = a
        b_ref[...] = b
        s_ref[...] = (_silu(a) * b).astype(s_ref.dtype)

    w_spec = pl.BlockSpec((None, k, tn), lambda i, j: (j // (c // tn), 0, j % (c // tn)))
    o_spec = pl.BlockSpec((tm, tn), lambda i, j: (i, j))
    return pl.pallas_call(
        body, name=name, grid=(m // tm, n // tn),
        in_specs=[pl.BlockSpec((tm, k), lambda i, j: (i, 0)), w_spec, w_spec],
        out_specs=[o_spec, o_spec, o_spec],
        out_shape=[jax.ShapeDtypeStruct((m, n), F32), jax.ShapeDtypeStruct((m, n), F32),
                   jax.ShapeDtypeStruct((m, n), BF16)],
        compiler_params=_cparams("parallel", "parallel"),
    )(h, wg, wu)


def _dgrad_glu(name, df, wd, a, b, after=None):
    m = df.shape[0]
    _, r, c = wd.shape
    tm, tn = _tile(m, 512), _tile(r, 1024)
    extra = [] if after is None else [after]

    def body(df_ref, w_ref, a_ref, b_ref, *rest):
        da_ref, db_ref = rest[-2], rest[-1]
        ds = lax.dot_general(df_ref[...], w_ref[...], NT, preferred_element_type=F32)
        _, vjp = jax.vjp(_f_glu, a_ref[...], b_ref[...])
        da, db = vjp(ds)
        da_ref[...] = da.astype(da_ref.dtype)
        db_ref[...] = db.astype(db_ref.dtype)

    t_spec = pl.BlockSpec((tm, tn), lambda i, j: (i, j))
    sds = jax.ShapeDtypeStruct((m, NCHIP * r), BF16)
    return pl.pallas_call(
        body, name=name, grid=(m // tm, NCHIP * r // tn),
        in_specs=[pl.BlockSpec((tm, c), lambda i, j: (i, 0)),
                  pl.BlockSpec((None, tn, c), lambda i, j: (j // (r // tn), j % (r // tn), 0)), t_spec, t_spec]
        + [pl.BlockSpec(memory_space=pl.ANY)] * len(extra),
        out_specs=[t_spec, t_spec], out_shape=[sds, sds],
        compiler_params=_cparams("parallel", "parallel"),
    )(df, wd, a, b, *extra)


def _sgu_fwd(name, proj, b_in, lng, lnb, wm, bst, tr):
    n_rows = proj.shape[0]

    def body(uv_ref, buv_ref, g_ref, b_ref, wm_ref, bst_ref, ya_ref):
        u, vln = _f_mixa(uv_ref[...], buv_ref[...], g_ref[...], b_ref[...])
        vb = vln.astype(BF16)
        for ck in range(tr // CHUNK):
            rs = slice(ck * CHUNK, (ck + 1) * CHUNK)
            for h in range(HEADS):
                cs = slice(h * CHUNK, (h + 1) * CHUNK)
                vm = jnp.dot(wm_ref[h], vb[rs, cs], preferred_element_type=F32) + bst_ref[:, cs]
                ya_ref[rs, cs] = (u[rs, cs] * vm).astype(ya_ref.dtype)

    return pl.pallas_call(
        body, name=name, grid=(n_rows // tr,),
        in_specs=[pl.BlockSpec((tr, 2 * D), lambda i: (i, 0)), pl.BlockSpec((1, 2 * D), lambda i: (0, 0)),
                  pl.BlockSpec((1, D), lambda i: (0, 0)), pl.BlockSpec((1, D), lambda i: (0, 0)),
                  pl.BlockSpec((HEADS, CHUNK, CHUNK), lambda i: (0, 0, 0)),
                  pl.BlockSpec((CHUNK, D), lambda i: (0, 0))],
        out_specs=pl.BlockSpec((tr, D), lambda i: (i, 0)),
        out_shape=jax.ShapeDtypeStruct((n_rows, D), BF16),
        compiler_params=_cparams("parallel"),
    )(proj, b_in, lng, lnb, wm, bst)


def _sgu_bwd(name, proj, b_in, lng, lnb, wm, wmt, bst, dya, tr):
    n_rows = proj.shape[0]

    def body(uv_ref, buv_ref, g_ref, b_ref, wm_ref, wmt_ref, bst_ref, dya_ref,
             duv_ref, dbuv_ref, dg_ref, db_ref, dws_ref, dbs_ref, du_s, dvln_s):
        i = pl.program_id(0)

        @pl.when(i == 0)
        def _():
            dbuv_ref[...] = jnp.zeros(dbuv_ref.shape, F32)
            dg_ref[...] = jnp.zeros(dg_ref.shape, F32)
            db_ref[...] = jnp.zeros(db_ref.shape, F32)
            dws_ref[...] = jnp.zeros(dws_ref.shape, F32)
            dbs_ref[...] = jnp.zeros(dbs_ref.shape, F32)

        (u, vln), vjp = jax.vjp(_f_mixa, uv_ref[...], buv_ref[...], g_ref[...], b_ref[...])
        vb = vln.astype(BF16)
        dya_v = dya_ref[...].astype(F32)
        tpos = lax.broadcasted_iota(jnp.int32, (CHUNK, CHUNK), 0)
        spos = lax.broadcasted_iota(jnp.int32, (CHUNK, CHUNK), 1)
        causal = (tpos >= spos).astype(F32)
        for ck in range(tr // CHUNK):
            rs = slice(ck * CHUNK, (ck + 1) * CHUNK)
            for h in range(HEADS):
                cs = slice(h * CHUNK, (h + 1) * CHUNK)
                vbh = vb[rs, cs]
                vm = jnp.dot(wm_ref[h], vbh, preferred_element_type=F32) + bst_ref[:, cs]
                dyb = dya_v[rs, cs]
                du_s[rs, cs] = dyb * vm
                dvm = dyb * u[rs, cs]
                dvmb = dvm.astype(BF16)
                dvln_s[rs, cs] = jnp.dot(wmt_ref[h], dvmb, preferred_element_type=F32)
                dws_ref[h] += causal * lax.dot_general(dvmb, vbh, NT, preferred_element_type=F32)
                dbs_ref[:, cs] += jnp.broadcast_to(jnp.sum(dvm, axis=1, keepdims=True), (CHUNK, CHUNK))
        duv, dbuv, dg, db = vjp((du_s[...], dvln_s[...]))
        duv_ref[...] = duv.astype(duv_ref.dtype)
        dbuv_ref[...] += dbuv
        dg_ref[...] += dg
        db_ref[...] += db

    par = pl.BlockSpec((1, D), lambda i: (0, 0))
    par2 = pl.BlockSpec((1, 2 * D), lambda i: (0, 0))
    w_spec = pl.BlockSpec((HEADS, CHUNK, CHUNK), lambda i: (0, 0, 0))
    b_spec = pl.BlockSpec((CHUNK, D), lambda i: (0, 0))
    return pl.pallas_call(
        body, name=name, grid=(n_rows // tr,),
        in_specs=[pl.BlockSpec((tr, 2 * D), lambda i: (i, 0)), par2, par, par, w_spec, w_spec, b_spec,
                  pl.BlockSpec((tr, D), lambda i: (i, 0))],
        out_specs=[pl.BlockSpec((tr, 2 * D), lambda i: (i, 0)), par2, par, par, w_spec, b_spec],
        out_shape=[jax.ShapeDtypeStruct((n_rows, 2 * D), BF16), jax.ShapeDtypeStruct((1, 2 * D), F32),
                   jax.ShapeDtypeStruct((1, D), F32), jax.ShapeDtypeStruct((1, D), F32),
                   jax.ShapeDtypeStruct((HEADS, CHUNK, CHUNK), F32), jax.ShapeDtypeStruct((CHUNK, D), F32)],
        scratch_shapes=[pltpu.VMEM((tr, D), F32), pltpu.VMEM((tr, D), F32)],
        compiler_params=_cparams("arbitrary"),
    )(proj, b_in, lng, lnb, wm, wmt, bst, dya)


CT = 256
RB = 128
CV0 = 2 * D // CT
CG0 = 3 * D // CT


def _shift_bank(win):
    return [win] + [pltpu.roll(win, RB + HALO - r, 0) for r in range(1, 8)]


def _shifted(bank, sh):
    lo = 8 * (sh // 8)
    return bank[sh % 8][lo:lo + RB]


def _conv_fwd(name, proj, b_in, w, cb):
    n_rows = proj.shape[0]

    def body(cv_ref, cg_ref, bcv_ref, bcg_ref, w_ref, cb_ref, z1_ref, zp_ref):
        zp_ref[0:HALO, :] = jnp.zeros((HALO, CT), F32)
        zp_ref[HALO:, :] = (cv_ref[...] + bcv_ref[...]) * jax.nn.sigmoid(cg_ref[...] + bcg_ref[...])

        def blk(rb, carry):
            base = pl.multiple_of(rb * RB, RB)
            bank = _shift_bank(zp_ref[pl.ds(base, RB + HALO), :])
            acc = jnp.broadcast_to(cb_ref[...], (RB, CT))
            for k in range(KCONV):
                acc = acc + w_ref[k:k + 1, :] * _shifted(bank, k + 2)
            z1_ref[pl.ds(base, RB), :] = acc
            return carry

        lax.fori_loop(0, n_rows // RB, blk, 0)

    return pl.pallas_call(
        body, name=name, grid=(D // CT,),
        in_specs=[pl.BlockSpec((n_rows, CT), lambda j: (0, CV0 + j)), pl.BlockSpec((n_rows, CT), lambda j: (0, CG0 + j)),
                  pl.BlockSpec((1, CT), lambda j: (0, CV0 + j)), pl.BlockSpec((1, CT), lambda j: (0, CG0 + j)),
                  pl.BlockSpec((HALO, CT), lambda j: (0, j)), pl.BlockSpec((1, CT), lambda j: (0, j))],
        out_specs=pl.BlockSpec((n_rows, CT), lambda j: (0, j)),
        out_shape=jax.ShapeDtypeStruct((n_rows, D), F32),
        scratch_shapes=[pltpu.VMEM((n_rows + HALO, CT), F32)],
        compiler_params=_cparams("parallel"),
    )(proj, proj, b_in, b_in, w, cb)


def _conv_bwd(name, dz1, proj, b_in, w):
    n_rows = proj.shape[0]

    def body(dz_ref, cv_ref, cg_ref, bcv_ref, bcg_ref, w_ref, dcv_ref, dcg_ref, sm_ref, zp_ref, dzp_ref):
        cvb = cv_ref[...] + bcv_ref[...]
        sg = jax.nn.sigmoid(cg_ref[...] + bcg_ref[...])
        zp_ref[0:HALO, :] = jnp.zeros((HALO, CT), F32)
        zp_ref[HALO:, :] = cvb * sg
        dz = dz_ref[...]
        dzp_ref[0:n_rows, :] = dz
        dzp_ref[n_rows:, :] = jnp.zeros((HALO, CT), F32)
        sm_ref[...] = jnp.zeros(sm_ref.shape, F32)
        sm_ref[32:33, :] = jnp.sum(dz, axis=0, keepdims=True)

        def blk(rb, carry):
            base = pl.multiple_of(rb * RB, RB)
            dbank = _shift_bank(dzp_ref[pl.ds(base, RB + HALO), :])
            zbank = _shift_bank(zp_ref[pl.ds(base, RB + HALO), :])
            dzb = dbank[0][0:RB]
            acc = jnp.zeros((RB, CT), F32)
            for k in range(KCONV):
                acc = acc + w_ref[k:k + 1, :] * _shifted(dbank, KCONV - 1 - k)
                sm_ref[k:k + 1, :] += jnp.sum(dzb * _shifted(zbank, k + 2), axis=0, keepdims=True)
            dzp_ref[pl.ds(base, RB), :] = acc
            return carry

        lax.fori_loop(0, n_rows // RB, blk, 0)
        dz0 = dzp_ref[0:n_rows, :]
        dcv = dz0 * sg
        dcg = dz0 * cvb * (sg * (1.0 - sg))
        dcv_ref[...] = dcv.astype(dcv_ref.dtype)
        dcg_ref[...] = dcg.astype(dcg_ref.dtype)
        sm_ref[33:34, :] = jnp.sum(dcv, axis=0, keepdims=True)
        sm_ref[34:35, :] = jnp.sum(dcg, axis=0, keepdims=True)

    col = pl.BlockSpec((n_rows, CT), lambda j: (0, j))
    return pl.pallas_call(
        body, name=name, grid=(D // CT,),
        in_specs=[col, pl.BlockSpec((n_rows, CT), lambda j: (0, CV0 + j)), pl.BlockSpec((n_rows, CT), lambda j: (0, CG0 + j)),
                  pl.BlockSpec((1, CT), lambda j: (0, CV0 + j)), pl.BlockSpec((1, CT), lambda j: (0, CG0 + j)),
                  pl.BlockSpec((HALO, CT), lambda j: (0, j))],
        out_specs=[col, col, pl.BlockSpec((40, CT), lambda j: (0, j))],
        out_shape=[jax.ShapeDtypeStruct((n_rows, D), BF16), jax.ShapeDtypeStruct((n_rows, D), BF16),
                   jax.ShapeDtypeStruct((40, D), F32)],
        scratch_shapes=[pltpu.VMEM((n_rows + HALO, CT), F32), pltpu.VMEM((n_rows + HALO, CT), F32)],
        compiler_params=_cparams("parallel"),
    )(dz1, proj, proj, b_in, b_in, w)


ADA_TN = 768


def _split_bf16(v):
    hi = v.astype(BF16)
    return hi, (v - hi.astype(F32)).astype(BF16)


def _ada_fwd(name, c_all, w, b):
    n = w.shape[1]

    def body(c_ref, w_ref, b_ref, o_ref):
        ch, cl = _split_bf16(_silu(c_ref[...]))
        wh, wl = _split_bf16(w_ref[...])
        acc = jnp.dot(ch, wl, preferred_element_type=F32) + jnp.dot(cl, wh, preferred_element_type=F32)
        o_ref[...] = acc + jnp.dot(ch, wh, preferred_element_type=F32) + b_ref[...]

    return pl.pallas_call(
        body, name=name, grid=(n // ADA_TN,),
        in_specs=[pl.BlockSpec((8, D), lambda j: (0, 0)), pl.BlockSpec((D, ADA_TN), lambda j: (0, j)),
                  pl.BlockSpec((1, ADA_TN), lambda j: (0, j))],
        out_specs=pl.BlockSpec((8, ADA_TN), lambda j: (0, j)),
        out_shape=jax.ShapeDtypeStruct((8, n), F32),
        compiler_params=_cparams("parallel"),
    )(c_all, w, b)


def _ada_wgrad(name, c_all_t, dmod):
    n = dmod.shape[1]

    def body(ct_ref, dm_ref, o_ref):
        ca = _silu(ct_ref[...])
        acc = ca[:, 0:1] * dm_ref[0:1, :]
        for r in range(1, 8):
            acc = acc + ca[:, r:r + 1] * dm_ref[r:r + 1, :]
        o_ref[...] = acc

    return pl.pallas_call(
        body, name=name, grid=(n // ADA_TN,),
        in_specs=[pl.BlockSpec((D, 8), lambda j: (0, 0)), pl.BlockSpec((8, ADA_TN), lambda j: (0, j))],
        out_specs=pl.BlockSpec((D, ADA_TN), lambda j: (0, j)),
        out_shape=jax.ShapeDtypeStruct((D, n), F32),
        compiler_params=_cparams("parallel"),
    )(c_all_t, dmod)


def _adamw(name, w, g, m, v, tr):
    rows, cols = w.shape

    def body(w_ref, g_ref, m_ref, v_ref, g_out, d_ref, nm_ref, nv_ref):
        gv = g_ref[...]
        nm = ADAM_B1 * m_ref[...] + (1.0 - ADAM_B1) * gv
        nv = ADAM_B2 * v_ref[...] + (1.0 - ADAM_B2) * (gv * gv)
        m_hat = nm / (1.0 - ADAM_B1 ** ADAM_STEP)
        v_hat = nv / (1.0 - ADAM_B2 ** ADAM_STEP)
        d_ref[...] = -ADAM_LR * (m_hat / (jnp.sqrt(v_hat) + ADAM_EPS) + ADAM_WD * w_ref[...])
        nm_ref[...] = nm
        nv_ref[...] = nv
        g_out[...] = gv

    spec = pl.BlockSpec((tr, cols), lambda i: (i, 0))
    sds = jax.ShapeDtypeStruct((rows, cols), F32)
    return pl.pallas_call(
        body, name=name, grid=(rows // tr,), in_specs=[spec] * 4, out_specs=[spec] * 4, out_shape=[sds] * 4,
        compiler_params=_cparams("parallel"),
    )(w, g, m, v)


def _sum8(name, stacked, tr, after=None):
    n, rows, cols = stacked.shape
    extra = _afters(after)

    def body(s_ref, *rest):
        o_ref = rest[-1]
        acc = s_ref[0]
        for r in range(1, n):
            acc = acc + s_ref[r]
        o_ref[...] = acc

    return pl.pallas_call(
        body, name=name, grid=(rows // tr,),
        in_specs=[pl.BlockSpec((n, tr, cols), lambda i: (0, i, 0))] + [pl.BlockSpec(memory_space=pl.ANY)] * len(extra),
        out_specs=pl.BlockSpec((tr, cols), lambda i: (i, 0)),
        out_shape=jax.ShapeDtypeStruct((rows, cols), F32),
        compiler_params=_cparams("parallel"),
    )(stacked, *extra)


def _pair_add(name, g5, other, c_idx, tr):
    nq, _, rows, cols = g5.shape

    def body(c_ref, g_ref, o_ref, p_ref):
        p_ref[...] = (g_ref[...].astype(F32) + o_ref[...].astype(F32)).astype(p_ref.dtype)

    return pl.pallas_call(
        body, name=name,
        grid_spec=pltpu.PrefetchScalarGridSpec(
            num_scalar_prefetch=1, grid=(nq, rows // tr),
            in_specs=[pl.BlockSpec((None, None, tr, cols), lambda qi, i, cr: (qi, cr[0], i, 0)),
                      pl.BlockSpec((None, tr, cols), lambda qi, i, cr: (qi, i, 0))],
            out_specs=pl.BlockSpec((None, tr, cols), lambda qi, i, cr: (qi, i, 0))),
        out_shape=jax.ShapeDtypeStruct((nq, rows, cols), BF16),
        compiler_params=_cparams("parallel", "parallel"),
    )(c_idx, g5, other)


def _chip_add(name, p, recv, qc_idx, tr):
    _, rows, cols = p.shape

    def body(qc_ref, p_ref, r_ref, o_ref):
        acc = p_ref[...].astype(F32)
        for k in range(NCHIP - 1):
            acc = acc + r_ref[k].astype(F32)
        o_ref[...] = acc

    return pl.pallas_call(
        body, name=name,
        grid_spec=pltpu.PrefetchScalarGridSpec(
            num_scalar_prefetch=1, grid=(rows // tr,),
            in_specs=[pl.BlockSpec((None, tr, cols), lambda i, qc: (qc[0], i, 0)),
                      pl.BlockSpec((NCHIP - 1, tr, cols), lambda i, qc: (0, i, 0))],
            out_specs=pl.BlockSpec((None, tr, cols), lambda i, qc: (qc[1], i, 0))),
        out_shape=jax.ShapeDtypeStruct((2, rows, cols), F32),
        compiler_params=_cparams("parallel"),
    )(qc_idx, p, recv)


def _allgather8(name, blk):
    m_per, n = blk.shape

    def body(x_ref, out_ref, send_sems, recv_sems, local_sem):
        x, y, c = lax.axis_index("x"), lax.axis_index("y"), lax.axis_index("c")
        me, sibling = (x, y, c), (x, y, 1 - c)
        chips = [(1 - x, y), (x, 1 - y), (1 - x, 1 - y)]

        def rows(px, py, pc):
            return out_ref.at[pl.ds((4 * px + 2 * py + pc) * m_per, m_per), :]

        def copy(k, block, to, src=None):
            return pltpu.make_async_remote_copy(
                src_ref=rows(*block) if src is None else src, dst_ref=rows(*block),
                send_sem=send_sems.at[k], recv_sem=recv_sems.at[k], device_id=to, device_id_type=MESH)

        mine = pltpu.make_async_copy(x_ref, rows(*me), local_sem)
        mine.start()
        first = [copy(0, me, sibling, src=x_ref)]
        first += [copy(1 + j, me, (*chip, c), src=x_ref) for j, chip in enumerate(chips)]
        for cp in first:
            cp.start()
        passed = [copy(4 + j, (*chip, c), sibling) for j, chip in enumerate(chips)]
        for j, chip in enumerate(chips):
            copy(1 + j, (*chip, c), me).wait_recv()
            passed[j].start()
        copy(0, sibling, me).wait_recv()
        for j, chip in enumerate(chips):
            copy(4 + j, (*chip, 1 - c), me).wait_recv()
        for cp in first + passed:
            cp.wait_send()
        mine.wait()

    return pl.pallas_call(
        body, name=name,
        out_shape=jax.ShapeDtypeStruct((8 * m_per, n), blk.dtype),
        in_specs=[pl.BlockSpec(memory_space=pltpu.VMEM)],
        out_specs=pl.BlockSpec(memory_space=pltpu.VMEM),
        scratch_shapes=[pltpu.SemaphoreType.DMA((7,)), pltpu.SemaphoreType.DMA((7,)), pltpu.SemaphoreType.DMA],
        compiler_params=pltpu.CompilerParams(vmem_limit_bytes=VMEM_LIMIT),
    )(blk)


def _exchange(name, srcs, out_shapes, n_local, n_remote, plan, aliases=None, after=None):
    ni, no = len(srcs), len(out_shapes)
    extra = [] if after is None else [after]

    def body(*refs):
        ins, outs = refs[:ni], refs[ni + len(extra):ni + len(extra) + no]
        send_sems, recv_sems, local_sems = refs[ni + len(extra) + no:]
        x, y, c = lax.axis_index("x"), lax.axis_index("y"), lax.axis_index("c")
        local, remote = plan(ins, outs, x, y, c)
        assert len(local) == n_local and len(remote) == n_remote

        def rcopy(i, dst):
            s, _, peer, _, _ = remote[i]
            return pltpu.make_async_remote_copy(src_ref=s, dst_ref=dst, send_sem=send_sems.at[i],
                                                recv_sem=recv_sems.at[i], device_id=peer, device_id_type=MESH)

        lcs = [pltpu.make_async_copy(s, d, local_sems.at[i]) for i, (s, d) in enumerate(local)]
        for cp in lcs:
            cp.start()
        first = [i for i in range(n_remote) if remote[i][4] is None]
        passed = [i for i in range(n_remote) if remote[i][4] is not None]
        for i in first:
            rcopy(i, remote[i][1]).start()
        arrived = set()
        for i in passed:
            j = remote[i][4]
            rcopy(j, remote[j][3]).wait_recv()
            arrived.add(j)
            rcopy(i, remote[i][1]).start()
        for i in range(n_remote):
            if i not in arrived:
                rcopy(i, remote[i][3]).wait_recv()
        for i in range(n_remote):
            rcopy(i, remote[i][1]).wait_send()
        for cp in lcs:
            cp.wait()

    any_spec = pl.BlockSpec(memory_space=pl.ANY)
    return pl.pallas_call(
        body, name=name, out_shape=out_shapes,
        in_specs=[any_spec] * (ni + len(extra)), out_specs=[any_spec] * no,
        input_output_aliases=aliases or {},
        scratch_shapes=[pltpu.SemaphoreType.DMA((n_remote,)), pltpu.SemaphoreType.DMA((n_remote,)),
                        pltpu.SemaphoreType.DMA((max(n_local, 1),))],
    )(*srcs, *extra)


_CHIP_FLIPS = ((0, 1), (1, 0), (1, 1))


def _flip(v, f):
    return 1 - v if f else v


EFFECT = pltpu.SideEffectType.DATAFLOW_SIDE_EFFECTING


def _split_start(name, bufs, n, plan, after=None):
    nb = len(bufs)
    extra = [] if after is None else [after]

    def body(*refs):
        ins = refs[:nb]
        send_sems, recv_sems = refs[nb + len(extra)], refs[nb + len(extra) + 1]
        token = refs[-1]
        x, y, c = lax.axis_index("x"), lax.axis_index("y"), lax.axis_index("c")
        copies = plan(ins, x, y, c)
        assert len(copies) == n
        for i, (s, d, peer, _) in enumerate(copies):
            pltpu.make_async_remote_copy(src_ref=s, dst_ref=d, send_sem=send_sems.at[i], recv_sem=recv_sems.at[i],
                                         device_id=peer, device_id_type=MESH).start()
        token[...] = jnp.zeros_like(token)

    hbm = pl.BlockSpec(memory_space=pltpu.HBM)
    sem = pl.BlockSpec(memory_space=pltpu.SEMAPHORE)
    res = pl.pallas_call(
        body, name=name,
        out_shape=(pltpu.SemaphoreType.DMA((n,)), pltpu.SemaphoreType.DMA((n,)),
                   *[pltpu.HBM(b.shape, b.dtype) for b in bufs], jax.ShapeDtypeStruct((8, 128), F32)),
        in_specs=[hbm] * nb + [pl.BlockSpec(memory_space=pl.ANY)] * len(extra),
        out_specs=(sem, sem, *[hbm] * nb, pl.BlockSpec(memory_space=pltpu.VMEM)),
        input_output_aliases={i: 2 + i for i in range(nb)},
        compiler_params=pltpu.CompilerParams(has_side_effects=EFFECT),
    )(*[pltpu.with_memory_space_constraint(b, pltpu.HBM) for b in bufs], *extra)
    return res[0], res[1], list(res[2:2 + nb]), res[-1]


def _split_wait(name, bufs, send_sems, recv_sems, after, n, plan):
    nb = len(bufs)

    def body(*refs):
        ins = refs[:nb]
        ssem, rsem = refs[nb], refs[nb + 1]
        x, y, c = lax.axis_index("x"), lax.axis_index("y"), lax.axis_index("c")
        copies = plan(ins, x, y, c)
        assert len(copies) == n
        for i, (s, _, peer, lands) in enumerate(copies):
            cp = pltpu.make_async_remote_copy(src_ref=s, dst_ref=lands, send_sem=ssem.at[i], recv_sem=rsem.at[i],
                                              device_id=peer, device_id_type=MESH)
            cp.wait_send()
            cp.wait_recv()

    hbm = pl.BlockSpec(memory_space=pltpu.HBM)
    sem = pl.BlockSpec(memory_space=pltpu.SEMAPHORE)
    afters = list(after) if isinstance(after, (list, tuple)) else [after]
    res = pl.pallas_call(
        body, name=name,
        out_shape=[pltpu.HBM(b.shape, b.dtype) for b in bufs],
        in_specs=[hbm] * nb + [sem, sem] + [pl.BlockSpec(memory_space=pl.ANY)] * len(afters),
        out_specs=[hbm] * nb,
        input_output_aliases={i: i for i in range(nb)},
        compiler_params=pltpu.CompilerParams(has_side_effects=EFFECT),
    )(*bufs, send_sems, recv_sems, *afters)
    return list(res)


def _chips_of(x, y):
    return [(_flip(x, fx), _flip(y, fy)) for fx, fy in _CHIP_FLIPS]


def _gather_start(tag, stacks, after):
    n = len(stacks)

    def plan(refs, x, y, c):
        q = 2 * x + y
        return [(refs[a].at[q, c], refs[a].at[q, c], (px, py, c), refs[a].at[2 * px + py, c])
                for a in range(n) for px, py in _chips_of(x, y)]

    send, recv, thru, token = _split_start(f"gather_start_{tag}", stacks, 3 * n, plan, after)
    return (send, recv, thru, plan), token


def _gather_finish(tag, state, after):
    send, recv, thru, plan = state
    n = len(thru)
    landed = _split_wait(f"gather_wait_{tag}", thru, send, recv, after, 3 * n, plan)

    def pass_on(ins, outs, x, y, c):
        sib = (x, y, 1 - c)
        remote = []
        for a in range(n):
            for px, py in _chips_of(x, y):
                p = 2 * px + py
                remote.append((ins[a].at[p, c], outs[a].at[p, c], sib, outs[a].at[p, 1 - c], None))
        return [], remote

    full = _exchange(f"gather_pass_{tag}", landed, [jax.ShapeDtypeStruct(s.shape, s.dtype) for s in landed],
                     0, 3 * n, pass_on, aliases={a: a for a in range(n)})
    return [g.reshape(NCHIP, 2 * g.shape[2], g.shape[3]) for g in full]


def _gather_land(tag, state, after):
    send, recv, thru, plan = state
    n = len(thru)
    landed = _split_wait(f"gather_wait_{tag}", thru, send, recv, after, 3 * n, plan)

    def pass_plan(refs, x, y, c):
        sib = (x, y, 1 - c)
        return [(refs[a].at[2 * px + py, c], refs[a].at[2 * px + py, c], sib, refs[a].at[2 * px + py, 1 - c])
                for a in range(n) for px, py in _chips_of(x, y)]

    send, recv, thru, token = _split_start(f"gather_pass_start_{tag}", landed, 3 * n, pass_plan)
    return (send, recv, thru, pass_plan), token


def _gather_done(tag, state, after):
    send, recv, thru, plan = state
    full = _split_wait(f"gather_pass_wait_{tag}", thru, send, recv, after, 3 * len(thru), plan)
    return [g.reshape(NCHIP, 2 * g.shape[2], g.shape[3]) for g in full]


_OTHERS = tuple((fx, fy, fc) for fx in (0, 1) for fy in (0, 1) for fc in (0, 1) if fx or fy or fc)


def _allgather_start(tag, buf, after=None):
    def plan(refs, x, y, c):
        me = 4 * x + 2 * y + c
        copies = []
        for fx, fy, fc in _OTHERS:
            px, py, pc = _flip(x, fx), _flip(y, fy), _flip(c, fc)
            copies.append((refs[0].at[me], refs[0].at[me], (px, py, pc), refs[0].at[4 * px + 2 * py + pc]))
        return copies

    send, recv, thru, token = _split_start(f"{tag}_start", [buf], len(_OTHERS), plan, after)
    return (send, recv, thru, plan), token


def _allgather_done(tag, state, after):
    send, recv, thru, plan = state
    return _split_wait(f"{tag}_wait", thru, send, recv, after, len(_OTHERS), plan)[0]


def _gather_small(name, smalls):
    n = len(smalls)

    def plan(ins, outs, x, y, c):
        q = 2 * x + y
        local = [(ins[b], outs[b].at[q]) for b in range(n)]
        remote = [(ins[b], outs[b].at[q], (px, py, c), outs[b].at[2 * px + py], None)
                  for b in range(n) for px, py in _chips_of(x, y)]
        return local, remote

    return _exchange(name, smalls, [jax.ShapeDtypeStruct((NCHIP,) + s.shape, s.dtype) for s in smalls], n, 3 * n, plan)


def _rs_pair_start(tag, grads, after=None):
    n = len(grads)
    g5 = [g.reshape(NCHIP, 2, g.shape[1] // 2, g.shape[2]) for g in grads]
    half = [(g.shape[2], g.shape[3]) for g in g5]
    land = [lax.empty((NCHIP,) + h, g.dtype) for h, g in zip(half, grads)]

    def plan(refs, x, y, c):
        sib = (x, y, 1 - c)
        return [(refs[a].at[:, 1 - c], refs[n + a], sib, refs[n + a]) for a in range(n)]

    send, recv, thru, token = _split_start(f"rs_pair_start_{tag}", g5 + land, n, plan, after)
    return (send, recv, thru, plan, half), token


def _rs_chips_start(tag, state, after, c_idx):
    send, recv, thru, pair_plan, half = state
    n = len(half)
    landed = _split_wait(f"rs_pair_wait_{tag}", thru, send, recv, after, n, pair_plan)
    g5, got = landed[:n], landed[n:]
    part = [_pair_add(f"rs_pair_add_{tag}{a}", g5[a], got[a], c_idx, _tile(half[a][0], 512)) for a in range(n)]
    land = [lax.empty((NCHIP - 1,) + h, BF16) for h in half]

    def plan(refs, x, y, c):
        return [(refs[a].at[2 * px + py], refs[n + a].at[k], (px, py, c), refs[n + a].at[k])
                for a in range(n) for k, (px, py) in enumerate(_chips_of(x, y))]

    send, recv, thru, token = _split_start(f"rs_chips_start_{tag}", part + land, 3 * n, plan)
    return (send, recv, thru, plan, half), token


def _rs_finish(tag, state, after, qc_idx):
    send, recv, thru, plan, half = state
    n = len(half)
    landed = _split_wait(f"rs_chips_wait_{tag}", thru, send, recv, after, 3 * n, plan)
    red = [_chip_add(f"rs_chip_add_{tag}{a}", landed[a], landed[n + a], qc_idx, _tile(half[a][0], 512)) for a in range(n)]

    def share(ins, outs, x, y, c):
        sib = (x, y, 1 - c)
        return [], [(ins[a].at[c], outs[a].at[c], sib, outs[a].at[1 - c], None) for a in range(n)]

    full = _exchange(f"rs_share_{tag}", red, [jax.ShapeDtypeStruct((2,) + h, F32) for h in half], 0, n, share,
                     aliases={a: a for a in range(n)})
    return [f.reshape(2 * h[0], h[1]) for f, h in zip(full, half)]


def _rs_land(tag, state, after, qc_idx):
    send, recv, thru, plan, half = state
    n = len(half)
    landed = _split_wait(f"rs_chips_wait_{tag}", thru, send, recv, after, 3 * n, plan)
    red = [_chip_add(f"rs_chip_add_{tag}{a}", landed[a], landed[n + a], qc_idx, _tile(half[a][0], 512)) for a in range(n)]

    def share(refs, x, y, c):
        sib = (x, y, 1 - c)
        return [(refs[a].at[c], refs[a].at[c], sib, refs[a].at[1 - c]) for a in range(n)]

    send, recv, thru, token = _split_start(f"rs_share_start_{tag}", red, n, share)
    return (send, recv, thru, share, half), token


def _rs_done(tag, state, after):
    send, recv, thru, plan, half = state
    full = _split_wait(f"rs_share_wait_{tag}", thru, send, recv, after, len(half), plan)
    return [f.reshape(2 * h[0], h[1]) for f, h in zip(full, half)]


def _place(name, blk, idx):
    rows, cols = blk.shape

    def body(i_ref, b_ref, o_ref):
        o_ref[...] = b_ref[...]

    return pl.pallas_call(
        body, name=name,
        grid_spec=pltpu.PrefetchScalarGridSpec(
            num_scalar_prefetch=1, grid=(1,),
            in_specs=[pl.BlockSpec((rows, cols), lambda i, d: (0, 0))],
            out_specs=pl.BlockSpec((None, rows, cols), lambda i, d: (d[0], 0, 0))),
        out_shape=jax.ShapeDtypeStruct((8, rows, cols), blk.dtype),
        compiler_params=_cparams("arbitrary"),
    )(idx, blk)


def _pack(name, n_rows, pieces, after=None):
    arrays = [p[1] for p in pieces if p[0] != "zeros"]
    extra = [] if after is None else [after]

    def body(*refs):
        o_ref = refs[-1]
        o_ref[...] = jnp.zeros(o_ref.shape, F32)
        row, ai = 0, 0
        for p in pieces:
            kind = p[0]
            if kind == "zeros":
                row += p[1]
                continue
            ref = refs[len(extra) + ai]
            ai += 1
            if kind == "rows":
                r = ref.shape[0]
                o_ref[row:row + r, :] = ref[...]
            elif kind == "wide":
                r = ref.shape[1] // D
                for j in range(r):
                    o_ref[row + j:row + j + 1, :] = ref[:, j * D:(j + 1) * D]
            elif kind == "slice":
                r = p[3] - p[2]
                o_ref[row:row + r, :] = ref[p[2]:p[3], :]
            elif kind == "heads":
                r = CHUNK
                for h in range(HEADS):
                    o_ref[row:row + r, h * CHUNK:(h + 1) * CHUNK] = ref[h]
            else:
                r = 1
                o_ref[row:row + 1, 0:128] = ref[0:1, :]
            row += r
        assert row == n_rows, (row, n_rows)

    vmem = pl.BlockSpec(memory_space=pltpu.VMEM)
    return pl.pallas_call(
        body, name=name, out_shape=jax.ShapeDtypeStruct((n_rows, D), F32),
        in_specs=[pl.BlockSpec(memory_space=pl.ANY)] * len(extra) + [vmem] * len(arrays), out_specs=vmem,
        compiler_params=pltpu.CompilerParams(vmem_limit_bytes=VMEM_LIMIT),
    )(*extra, *arrays)


def _to_bf16_stack(name, w, q_idx):
    r, cols = w.shape
    half = r // 2
    tr = _tile(half, 256)
    nb = half // tr

    def body(q_ref, w_ref, o_ref):
        o_ref[...] = w_ref[...].astype(BF16)

    return pl.pallas_call(
        body, name=name,
        grid_spec=pltpu.PrefetchScalarGridSpec(
            num_scalar_prefetch=1, grid=(2 * nb,),
            in_specs=[pl.BlockSpec((tr, cols), lambda i, qr: (i, 0))],
            out_specs=pl.BlockSpec((None, None, tr, cols), lambda i, qr: (qr[0], i // nb, i % nb, 0))),
        out_shape=jax.ShapeDtypeStruct((NCHIP, 2, half, cols), BF16),
        compiler_params=_cparams("parallel"),
    )(q_idx, w)


def kernel(x, c, ada_w, ada_b, norm_ffn1, ffn1_w_gate, ffn1_w_up, ffn1_w_down, norm_mix, mix_w_in, mix_b_in, sgu_ln_g, sgu_ln_b, sgu_w_s, sgu_b_s, conv_w, conv_b, conv_ln_g, conv_ln_b, w_branch_a, w_branch_b, w_out, norm_ffn2, ffn2_w_gate, ffn2_w_up, ffn2_w_down, norm_final, loss_target, m_ada_w, m_ada_b, m_norm_ffn1, m_ffn1_w_gate, m_ffn1_w_up, m_ffn1_w_down, m_norm_mix, m_mix_w_in, m_mix_b_in, m_sgu_ln_g, m_sgu_ln_b, m_sgu_w_s, m_sgu_b_s, m_conv_w, m_conv_b, m_conv_ln_g, m_conv_ln_b, m_w_branch_a, m_w_branch_b, m_w_out, m_norm_ffn2, m_ffn2_w_gate, m_ffn2_w_up, m_ffn2_w_down, m_norm_final, v_ada_w, v_ada_b, v_norm_ffn1, v_ffn1_w_gate, v_ffn1_w_up, v_ffn1_w_down, v_norm_mix, v_mix_w_in, v_mix_b_in, v_sgu_ln_g, v_sgu_ln_b, v_sgu_w_s, v_sgu_b_s, v_conv_w, v_conv_b, v_conv_ln_g, v_conv_ln_b, v_w_branch_a, v_w_branch_b, v_w_out, v_norm_ffn2, v_ffn2_w_gate, v_ffn2_w_up, v_ffn2_w_down, v_norm_final):
    xi, yi, ci = lax.axis_index("x"), lax.axis_index("y"), lax.axis_index("c")
    q = 2 * xi + yi
    dev = 4 * xi + 2 * yi + ci
    c_idx = jnp.reshape(ci, (1,)).astype(jnp.int32)
    q_idx = jnp.reshape(q, (1,)).astype(jnp.int32)
    x0 = x[0]
    tgt = loss_target[0]
    qc_idx = jnp.stack([q, ci]).astype(jnp.int32)
    n_ada = ada_w.shape[2]

    c_all = _allgather8("ag_c", jnp.pad(c, ((0, 7), (0, 0))))[::8]
    ada_b_mine = lax.dynamic_slice(ada_b, (0, q * n_ada), (1, n_ada))
    mod_part = _ada_fwd("ada_fwd", c_all, ada_w[0], ada_b_mine)
    small_parts = _gather_small("gather_small", [mod_part, jnp.pad(conv_w[0], ((0, 1), (0, 0)))])

    big = dict(g1=ffn1_w_gate, u1=ffn1_w_up, d1=ffn1_w_down, win=mix_w_in, wa=w_branch_a, wb=w_branch_b, wo=w_out,
               g2=ffn2_w_gate, u2=ffn2_w_up, d2=ffn2_w_down)
    groups = dict(f1=("d1", "g1", "u1"), mx=("wo", "wa", "wb", "win"), f2=("d2", "g2", "u2"))
    gather_groups = dict(up1=("g1", "u1"), dn1=("d1",), win=("win",), mx=("wa", "wb", "wo"), f2=("g2", "u2", "d2"))
    gather_state = {}
    started = []
    for tag, keys in gather_groups.items():
        gather_state[tag], token = _gather_start(
            tag, [_to_bf16_stack(f"cast_{k}", big[k][0], q_idx) for k in keys], small_parts[0])
        started.append(token)

    mod_all = jnp.transpose(small_parts[0], (1, 0, 2)).reshape(8, N_MOD * D)
    mod = lax.dynamic_slice(mod_all, (dev, 0), (1, N_MOD * D))
    sh1, sc1, g1, sh2, sc2, g2, sh3, sc3 = [(mod, D, k) for k in range(N_MOD - 1)]
    g3 = mod[:, (N_MOD - 1) * D:]
    convw = jnp.transpose(small_parts[1], (1, 0, 2)).reshape(HALO, D)
    W = dict(zip(gather_groups["up1"], _gather_finish("up1", gather_state["up1"], [mod] + started)))

    causal = jnp.tril(jnp.ones((CHUNK, CHUNK), bool))
    wm_f = jnp.where(causal[None], sgu_w_s[0], 0.0)
    wm = wm_f.astype(BF16)
    wmt = jnp.swapaxes(wm_f, 1, 2).astype(BF16)
    bst = jnp.repeat(jnp.transpose(sgu_b_s[0]), CHUNK, axis=1)
    b_in = mix_b_in

    (h1,) = _rowwise_fwd("pre1", _f_pre, [x0], [norm_ffn1, sc1, sh1], [(D, BF16)], 256)
    a1, b1, s1 = _ffn_up("up1", h1, W["g1"], W["u1"])
    W.update(zip(gather_groups["dn1"], _gather_finish("dn1", gather_state["dn1"], s1)))
    f1 = _matmul("down1", s1, W["d1"], "fwd", False, F32)
    pass_win, token = _gather_land("win", gather_state["win"], f1)
    x1, h2 = _rowwise_fwd("res1", _f_res_pre(0.5), [x0, f1], [g1, norm_mix, sc2, sh2],
                          [(D, F32), (D, BF16)], 256, after=token)
    W.update(zip(gather_groups["win"], _gather_done("win", pass_win, x1)))
    pass_mx, token = _gather_land("mx", gather_state["mx"], x1)
    proj = _matmul("in_proj", h2, W["win"], "fwd", True, F32, after=token)
    W.update(zip(gather_groups["mx"], _gather_done("mx", pass_mx, proj)))
    ya = _sgu_fwd("sgu", proj, b_in, sgu_ln_g, sgu_ln_b, wm, bst, 256)
    z1 = _conv_fwd("conv", proj, b_in, convw, conv_b)
    (z3,) = _rowwise_fwd("conv_ln", _f_lnsilu, [z1], [conv_ln_g, conv_ln_b], [(D, BF16)], 256)
    y_a = _matmul("branch_a", ya, W["wa"], "fwd", False, F32)
    y_b = _matmul("branch_b", z3, W["wb"], "fwd", False, F32)
    pass_f2, token = _gather_land("f2", gather_state["f2"], y_b)
    (merged,) = _rowwise_fwd("merge", _f_merge, [(proj, 2 * D, 2), y_a, y_b], [(b_in, 2 * D, 2)],
                             [(D, BF16)], 256, after=token)
    y = _matmul("out_proj", merged, W["wo"], "fwd", False, F32)
    x2, h3 = _rowwise_fwd("res2", _f_res_pre(1.0), [x1, y], [g2, norm_ffn2, sc3, sh3], [(D, F32), (D, BF16)], 256)
    W.update(zip(gather_groups["f2"], _gather_done("f2", pass_f2, x2)))
    a3, b3, s3 = _ffn_up("up2", h3, W["g2"], W["u2"])
    f3 = _matmul("down2", s3, W["d2"], "fwd", False, F32)
    dx2, df3, dg3, dnwf, loss_blk = _final("final", x2, f3, tgt, g3, norm_final.reshape(1, D), 256)

    G = {}

    def ffn_bwd(tag, df, s, a, b, h, wg, wu, wd, after=None):
        da, db = _dgrad_glu(f"d_down{tag}", df, wd, a, b, after)
        G["d" + tag] = _matmul(f"w_down{tag}", s, df, "wgrad", False, BF16)
        dh = _matmul(f"d_gate{tag}", da, wg, "dgrad", True, F32)
        dh = _matmul(f"d_up{tag}", db, wu, "dgrad", True, F32, add=dh)
        G["g" + tag] = _matmul(f"w_gate{tag}", h, da, "wgrad", True, BF16)
        G["u" + tag] = _matmul(f"w_up{tag}", h, db, "wgrad", True, BF16)
        return dh

    reduced = {}
    dh3 = ffn_bwd("2", df3, s3, a3, b3, h3, W["g2"], W["u2"], W["d2"])
    pair_f2, token = _rs_pair_start("f2", [G[k] for k in groups["f2"]])
    (dx1, dy), (dg2, dnw3, dsc3, dsh3) = _rowwise_bwd(
        "res2_b", _f_res_pre(1.0), [x1, y], [g2, norm_ffn2, sc3, sh3], [dx2, dh3], [F32, BF16], 256, after=token)
    dmerged = _matmul("d_out", dy, W["wo"], "dgrad", False, F32)
    G["wo"] = _matmul("w_out", merged, dy, "wgrad", False, BF16)
    rs_f2, token = _rs_chips_start("f2", pair_f2, [dmerged, G["wo"]], c_idx)
    (dgab, dy_a, dy_b), (dbgab,) = _rowwise_bwd(
        "merge_b", _f_merge, [(proj, 2 * D, 2), y_a, y_b], [(b_in, 2 * D, 2)], [dmerged],
        [BF16, BF16, BF16], 256, after=token)
    dya = _matmul("d_a", dy_a, W["wa"], "dgrad", False, F32)
    dz3 = _matmul("d_b", dy_b, W["wb"], "dgrad", False, F32)
    G["wa"] = _matmul("w_a", ya, dy_a, "wgrad", False, BF16)
    G["wb"] = _matmul("w_b", z3, dy_b, "wgrad", False, BF16)
    (dz1,), (dclg, dclb) = _rowwise_bwd("conv_ln_b", _f_lnsilu, [z1], [conv_ln_g, conv_ln_b], [dz3], [F32], 256)
    dcv, dcg, conv_small = _conv_bwd("conv_b", dz1, proj, b_in, convw)
    duv, dbuv, dlng, dlnb, dws, dbs = _sgu_bwd("sgu_b", proj, b_in, sgu_ln_g, sgu_ln_b, wm, wmt, bst, dya, 256)
    dproj = jnp.concatenate([duv, dcv, dcg, dgab], axis=1)
    dh2 = _matmul("d_in", dproj, W["win"], "dgrad", True, F32)
    G["win"] = _matmul("w_in", h2, dproj, "wgrad", True, BF16)
    pair_mx, token = _rs_pair_start("mx", [G[k] for k in groups["mx"]])
    share_f2, token2 = _rs_land("f2", rs_f2, [dh2, token], qc_idx)

    dev_idx = jnp.reshape(dev, (1,)).astype(jnp.int32)
    early = _pack("pack_early", HALO + CHUNK, [("slice", conv_small, 0, HALO), ("heads", dws)])
    ag_early, token3 = _allgather_start("ag_early", _place("place_early", early, dev_idx), token2)

    (dx0, df1), (dg1, dnw2, dsc2, dsh2) = _rowwise_bwd(
        "res1_b", _f_res_pre(0.5), [x0, f1], [g1, norm_mix, sc2, sh2],
        [dx1, dh2], [F32, BF16], 256, after=[token, token3])
    rs_mx, token = _rs_chips_start("mx", pair_mx, df1, c_idx)
    dh1 = ffn_bwd("1", df1, s1, a1, b1, h1, W["g1"], W["u1"], W["d1"], after=token)
    pair_f1, token = _rs_pair_start("f1", [G[k] for k in groups["f1"]])
    reduced.update(zip(groups["f2"], _rs_done("f2", share_f2, dh1)))
    share_mx, token2 = _rs_land("mx", rs_mx, [dh1, token], qc_idx)
    (grad_x,), (dnw1, dsc1, dsh1) = _rowwise_bwd(
        "pre1_b", _f_pre_keep, [x0], [norm_ffn1, sc1, sh1], [dx0, dh1], [F32], 256, after=[token, token2])
    reduced.update(zip(groups["mx"], _rs_done("mx", share_mx, grad_x)))
    tot_early = _sum8("sum_early", _allgather_done("ag_early", ag_early, grad_x), HALO)

    dbs_row = jnp.transpose(dbs[:, ::CHUNK]).reshape(1, D)
    small = _pack("pack_small", HALO, [
        ("rows", dnw1), ("rows", dnw2), ("wide", dbuv), ("slice", conv_small, 33, 34), ("slice", conv_small, 34, 35),
        ("wide", dbgab), ("rows", dlng), ("rows", dlnb), ("rows", dbs_row), ("slice", conv_small, 32, 33),
        ("rows", dclg), ("rows", dclb), ("rows", dnw3), ("rows", dnwf),
        ("rows", dsh1), ("rows", dsc1), ("rows", dg1), ("rows", dsh2), ("rows", dsc2), ("rows", dg2),
        ("rows", dsh3), ("rows", dsc3), ("rows", dg3), ("lanes", loss_blk), ("zeros", 6)])
    every = _allgather8("ag_small", small)
    rs_f1, token = _rs_chips_start("f1", pair_f1, every, c_idx)
    every = every.reshape(8, HALO, D)
    tot = _sum8("sum_small", every, HALO, after=token)
    loss = tot[25, 0]
    dmod_all = every[:, 16:16 + N_MOD, :].reshape(8, N_MOD * D)
    dmod_mine = lax.dynamic_slice(dmod_all, (0, q * n_ada), (8, n_ada))
    g_ada_w = _ada_wgrad("ada_wgrad", jnp.transpose(c_all), dmod_mine)

    given = dict(ada_w=(ada_w, m_ada_w, v_ada_w), ada_b=(ada_b, m_ada_b, v_ada_b),
                 norm_ffn1=(norm_ffn1, m_norm_ffn1, v_norm_ffn1),
                 ffn1_w_gate=(ffn1_w_gate, m_ffn1_w_gate, v_ffn1_w_gate), ffn1_w_up=(ffn1_w_up, m_ffn1_w_up, v_ffn1_w_up),
                 ffn1_w_down=(ffn1_w_down, m_ffn1_w_down, v_ffn1_w_down), norm_mix=(norm_mix, m_norm_mix, v_norm_mix),
                 mix_w_in=(mix_w_in, m_mix_w_in, v_mix_w_in), mix_b_in=(mix_b_in, m_mix_b_in, v_mix_b_in),
                 sgu_ln_g=(sgu_ln_g, m_sgu_ln_g, v_sgu_ln_g), sgu_ln_b=(sgu_ln_b, m_sgu_ln_b, v_sgu_ln_b),
                 sgu_w_s=(sgu_w_s, m_sgu_w_s, v_sgu_w_s), sgu_b_s=(sgu_b_s, m_sgu_b_s, v_sgu_b_s),
                 conv_w=(conv_w, m_conv_w, v_conv_w), conv_b=(conv_b, m_conv_b, v_conv_b),
                 conv_ln_g=(conv_ln_g, m_conv_ln_g, v_conv_ln_g), conv_ln_b=(conv_ln_b, m_conv_ln_b, v_conv_ln_b),
                 w_branch_a=(w_branch_a, m_w_branch_a, v_w_branch_a), w_branch_b=(w_branch_b, m_w_branch_b, v_w_branch_b),
                 w_out=(w_out, m_w_out, v_w_out), norm_ffn2=(norm_ffn2, m_norm_ffn2, v_norm_ffn2),
                 ffn2_w_gate=(ffn2_w_gate, m_ffn2_w_gate, v_ffn2_w_gate), ffn2_w_up=(ffn2_w_up, m_ffn2_w_up, v_ffn2_w_up),
                 ffn2_w_down=(ffn2_w_down, m_ffn2_w_down, v_ffn2_w_down), norm_final=(norm_final, m_norm_final, v_norm_final))
    out = {}

    def big_update(name, g2d):
        w, m, v = given[name]
        shp = w.shape
        res = _adamw(f"adamw_{name}", w.reshape(g2d.shape), g2d, m.reshape(g2d.shape), v.reshape(g2d.shape),
                     _tile(g2d.shape[0], 512 if g2d.shape[1] <= D else 256))
        out[name] = tuple(t.reshape(shp) for t in res)

    big_names = dict(g1="ffn1_w_gate", u1="ffn1_w_up", d1="ffn1_w_down", win="mix_w_in", wa="w_branch_a",
                     wb="w_branch_b", wo="w_out", g2="ffn2_w_gate", u2="ffn2_w_up", d2="ffn2_w_down")
    for key in groups["f2"] + groups["mx"]:
        big_update(big_names[key], reduced[key])
    big_update("ada_w", g_ada_w)

    small_rows = [("norm_ffn1", 0, 1), ("norm_mix", 1, 1), ("mix_b_in", 2, 6), ("sgu_ln_g", 8, 1), ("sgu_ln_b", 9, 1),
                  ("sgu_b_s", 10, 1), ("conv_b", 11, 1), ("conv_ln_g", 12, 1), ("conv_ln_b", 13, 1),
                  ("norm_ffn2", 14, 1), ("norm_final", 15, 1), ("ada_b", 16, N_MOD)]

    def pack_rows(which, label):
        pieces = []
        for nm, _, r in small_rows:
            t = given[nm][which]
            pieces.append(("wide", t) if r > 1 else ("rows", t.reshape(1, D)))
        return _pack(f"pack_{label}", HALO, pieces + [("zeros", HALO - 16 - N_MOD)])

    small_res = _adamw("adamw_small", pack_rows(0, "w"), tot, pack_rows(1, "m"), pack_rows(2, "v"), HALO)
    for nm, r0, r in small_rows:
        shp = given[nm][0].shape
        out[nm] = tuple(t[r0:r0 + r].reshape(shp) for t in small_res)
    n_cw = conv_w.shape[2]
    g_cw = lax.dynamic_slice(tot_early, (0, q * n_cw), (HALO, n_cw))
    pad_cw = lambda t: jnp.pad(t[0], ((0, 1), (0, 0)))
    cw_res = _adamw("adamw_conv_w", pad_cw(conv_w), g_cw, pad_cw(m_conv_w), pad_cw(v_conv_w), HALO)
    out["conv_w"] = tuple(t[:KCONV][None] for t in cw_res)
    g_ws = jnp.transpose(tot_early[HALO:].reshape(CHUNK, HEADS, CHUNK), (1, 0, 2)).reshape(HEADS * CHUNK, CHUNK)
    flat_ws = lambda t: t.reshape(HEADS * CHUNK, CHUNK)
    ws_res = _adamw("adamw_sgu_w_s", flat_ws(sgu_w_s), g_ws, flat_ws(m_sgu_w_s), flat_ws(v_sgu_w_s), 512)
    out["sgu_w_s"] = tuple(t.reshape(sgu_w_s.shape) for t in ws_res)

    done = [out[n][1] for n in ("ada_w", "sgu_w_s", "conv_w", "ada_b")] + [out[big_names[k]][1] for k in groups["f2"] + groups["mx"]]
    reduced.update(zip(groups["f1"], _rs_finish("f1", rs_f1, done, qc_idx)))
    for key in groups["f1"]:
        big_update(big_names[key], reduced[key])

    order = ["ada_w", "ada_b", "norm_ffn1", "ffn1_w_gate", "ffn1_w_up", "ffn1_w_down", "norm_mix", "mix_w_in", "mix_b_in",
             "sgu_ln_g", "sgu_ln_b", "sgu_w_s", "sgu_b_s", "conv_w", "conv_b", "conv_ln_g", "conv_ln_b", "w_branch_a",
             "w_branch_b", "w_out", "norm_ffn2", "ffn2_w_gate", "ffn2_w_up", "ffn2_w_down", "norm_final"]
    return (loss, grad_x[None], *[out[n][0] for n in order], *[out[n][1] for n in order],
            *[out[n][2] for n in order], *[out[n][3] for n in order])
```

```python
import functools

import jax
import jax.numpy as jnp
from jax import lax
from jax.experimental import pallas as pl
from jax.experimental.pallas import tpu as pltpu

F32 = jnp.float32
BF16 = jnp.bfloat16
D = 1024
HEADS = 8
CHUNK = 128
KCONV = 31
HALO = 32
N_MOD = 9
EPS = 1e-6
NCHIP = 4
VMEM_LIMIT = 56 * 1024 * 1024
MESH = pl.DeviceIdType.MESH

ADAM_LR = 0.001
ADAM_B1 = 0.9
ADAM_B2 = 0.999
ADAM_EPS = 1e-08
ADAM_WD = 0.01
ADAM_STEP = 10

NN = (((1,), (0,)), ((), ()))
NT = (((1,), (1,)), ((), ()))
TN = (((0,), (0,)), ((), ()))


def _cparams(*sem):
    return pltpu.CompilerParams(dimension_semantics=sem or None, vmem_limit_bytes=VMEM_LIMIT)


def _rms(x, w):
    return x * lax.rsqrt(jnp.mean(x * x, axis=-1, keepdims=True) + EPS) * w


def _ln(x, g, b):
    mu = jnp.mean(x, axis=-1, keepdims=True)
    xc = x - mu
    var = jnp.mean(xc * xc, axis=-1, keepdims=True)
    return xc * lax.rsqrt(var + EPS) * g + b


def _silu(x):
    return x * jax.nn.sigmoid(x)


def _gelu(x):
    return x * (0.5 * (1.0 + jnp.tanh(0.7978845608028654 * (x + 0.044715 * (x * x * x)))))


def _f_pre(x, nw, sc, sh):
    return _rms(x, nw) * (1.0 + sc) + sh


def _f_pre_keep(x, nw, sc, sh):
    return x, _f_pre(x, nw, sc, sh)


def _f_res_pre(scale):
    def fn(x, f, g, nw, sc, sh):
        xn = x + (scale * g) * f
        return xn, _f_pre(xn, nw, sc, sh)
    return fn


def _f_glu(a, b):
    return _silu(a) * b


def _f_merge(gab, ya, yb, bgab):
    g = gab + bgab
    return jax.nn.sigmoid(g[:, :D]) * ya + jax.nn.sigmoid(g[:, D:]) * yb


def _f_lnsilu(z, g, b):
    return _silu(_ln(z, g, b))


def _f_mixa(uv, buv, g, b):
    t = uv + buv
    return _gelu(t[:, :D]), _ln(_gelu(t[:, D:]), g, b)


def _col(t):
    return t if isinstance(t, tuple) else (t, t.shape[-1], 0)


def _afters(after):
    if after is None:
        return []
    return list(after) if isinstance(after, (list, tuple)) else [after]


def _rowwise_fwd(name, fn, rows, params, outs, tr, after=None):
    rows = [_col(r) for r in rows]
    params = [_col(p) for p in params]
    extra = _afters(after)
    n_rows = rows[0][0].shape[0]
    nin = len(rows) + len(params)

    def body(*refs):
        res = fn(*[r[...].astype(F32) for r in refs[:nin]])
        res = res if isinstance(res, tuple) else (res,)
        for o, v in zip(refs[nin + len(extra):], res, strict=True):
            o[...] = v.astype(o.dtype)

    return pl.pallas_call(
        body, name=name, grid=(n_rows // tr,),
        in_specs=[pl.BlockSpec((tr, w), functools.partial(lambda cb, i: (i, cb), cb)) for _, w, cb in rows]
        + [pl.BlockSpec((1, w), functools.partial(lambda cb, i: (0, cb), cb)) for _, w, cb in params]
        + [pl.BlockSpec(memory_space=pl.ANY)] * len(extra),
        out_specs=[pl.BlockSpec((tr, w), lambda i: (i, 0)) for w, _ in outs],
        out_shape=[jax.ShapeDtypeStruct((n_rows, w), dt) for w, dt in outs],
        compiler_params=_cparams("parallel"),
    )(*[r[0] for r in rows], *[p[0] for p in params], *extra)


def _rowwise_bwd(name, fn, rows, params, cots, row_grads, tr, after=None):
    rows = [_col(r) for r in rows]
    params = [_col(p) for p in params]
    cots = [_col(t) for t in cots]
    extra = _afters(after)
    n_rows = rows[0][0].shape[0]
    nr, npar, nc = len(rows), len(params), len(cots)
    nin = nr + npar + nc
    n_rg = sum(dt is not None for dt in row_grads)

    def body(*refs):
        i = pl.program_id(0)
        prim = [r[...].astype(F32) for r in refs[:nr + npar]]
        ct = tuple(r[...].astype(F32) for r in refs[nr + npar:nin])
        _, vjp = jax.vjp(fn, *prim)
        g = vjp(ct if nc > 1 else ct[0])
        outs = refs[nin + len(extra):]
        oi = 0
        for j, dt in enumerate(row_grads):
            if dt is not None:
                outs[oi][...] = g[j].astype(dt)
                oi += 1
        for j in range(npar):
            acc = outs[n_rg + j]

            @pl.when(i == 0)
            def _(acc=acc):
                acc[...] = jnp.zeros(acc.shape, F32)

            acc[...] += g[nr + j]

    res = pl.pallas_call(
        body, name=name, grid=(n_rows // tr,),
        in_specs=[pl.BlockSpec((tr, w), functools.partial(lambda cb, i: (i, cb), cb)) for _, w, cb in rows]
        + [pl.BlockSpec((1, w), functools.partial(lambda cb, i: (0, cb), cb)) for _, w, cb in params]
        + [pl.BlockSpec((tr, w), functools.partial(lambda cb, i: (i, cb), cb)) for _, w, cb in cots]
        + [pl.BlockSpec(memory_space=pl.ANY)] * len(extra),
        out_specs=[pl.BlockSpec((tr, rows[j][1]), lambda i: (i, 0)) for j, dt in enumerate(row_grads) if dt is not None]
        + [pl.BlockSpec((1, w), lambda i: (0, 0)) for _, w, _ in params],
        out_shape=[jax.ShapeDtypeStruct((n_rows, rows[j][1]), dt) for j, dt in enumerate(row_grads) if dt is not None]
        + [jax.ShapeDtypeStruct((1, w), F32) for _, w, _ in params],
        compiler_params=_cparams("arbitrary"),
    )(*[r[0] for r in rows], *[p[0] for p in params], *[t[0] for t in cots], *extra)
    return res[:n_rg], res[n_rg:]


def _final(name, x, f, tgt, g, nw, tr):
    n_rows = x.shape[0]

    def body(x_ref, f_ref, t_ref, g_ref, nw_ref, dx_ref, df_ref, dg_ref, dnw_ref, loss_ref):
        i = pl.program_id(0)
        tg = t_ref[...]

        def fn(xv, fv, gv, nwv):
            e = _rms(xv + (0.5 * gv) * fv, nwv) - tg
            return 0.5 * jnp.mean(e * e, axis=-1, keepdims=True)

        per_row, vjp = jax.vjp(fn, x_ref[...], f_ref[...], g_ref[...], nw_ref[...])
        dx, df, dg, dnw = vjp(jnp.ones_like(per_row))
        dx_ref[...] = dx
        df_ref[...] = df.astype(df_ref.dtype)

        @pl.when(i == 0)
        def _():
            dg_ref[...] = jnp.zeros(dg_ref.shape, F32)
            dnw_ref[...] = jnp.zeros(dnw_ref.shape, F32)
            loss_ref[...] = jnp.zeros(loss_ref.shape, F32)

        dg_ref[...] += dg
        dnw_ref[...] += dnw
        loss_ref[...] += jnp.broadcast_to(jnp.sum(per_row, axis=0, keepdims=True), loss_ref.shape)

    row = pl.BlockSpec((tr, D), lambda i: (i, 0))
    par = pl.BlockSpec((1, D), lambda i: (0, 0))
    return pl.pallas_call(
        body, name=name, grid=(n_rows // tr,),
        in_specs=[row, row, row, par, par],
        out_specs=[row, row, par, par, pl.BlockSpec((8, 128), lambda i: (0, 0))],
        out_shape=[jax.ShapeDtypeStruct((n_rows, D), F32), jax.ShapeDtypeStruct((n_rows, D), BF16),
                   jax.ShapeDtypeStruct((1, D), F32), jax.ShapeDtypeStruct((1, D), F32),
                   jax.ShapeDtypeStruct((8, 128), F32)],
        compiler_params=_cparams("arbitrary"),
    )(x, f, tgt, g, nw)


def _tile(dim, pref):
    t = min(dim, pref)
    while dim % t:
        t //= 2
    return t


def _matmul(name, a, b, kind, col, out_dtype, add=None, after=None):
    if kind == "wgrad":
        m, kg = a.shape
        ng = b.shape[1]
        r, c = (kg, ng // NCHIP) if col else (kg // NCHIP, ng)
        tm, tn, tk = _tile(r, 1024), _tile(c, 1536), _tile(m, 2048)
        grid = (kg // tm, ng // tn, m // tk)
        a_spec = pl.BlockSpec((tk, tm), lambda i, j, k: (k, i))
        b_spec = pl.BlockSpec((tk, tn), lambda i, j, k: (k, j))
        if col:
            o_spec = pl.BlockSpec((None, tm, tn), lambda i, j, k: (j // (c // tn), i, j % (c // tn)))
        else:
            o_spec = pl.BlockSpec((None, tm, tn), lambda i, j, k: (i // (r // tm), i % (r // tm), j))
        out_shape = jax.ShapeDtypeStruct((NCHIP, r, c), out_dtype)
        dims = TN
    else:
        _, r, c = b.shape
        m = a.shape[0]
        kg, ng = (r, NCHIP * c) if col else (NCHIP * r, c)
        tm = _tile(m, 1024)
        if kind == "fwd":
            tn, tk = _tile(c, 1536), _tile(r, 2048)
            grid = (m // tm, ng // tn, kg // tk)
            a_spec = pl.BlockSpec((tm, tk), lambda i, j, k: (i, k))
            if col:
                b_spec = pl.BlockSpec((None, tk, tn), lambda i, j, k: (j // (c // tn), k, j % (c // tn)))
            else:
                b_spec = pl.BlockSpec((None, tk, tn), lambda i, j, k: (k // (r // tk), k % (r // tk), j))
            out_shape = jax.ShapeDtypeStruct((m, ng), out_dtype)
            dims = NN
        else:
            tn, tk = _tile(r, 1024), _tile(c, 2048)
            grid = (m // tm, kg // tn, ng // tk)
            a_spec = pl.BlockSpec((tm, tk), lambda i, j, k: (i, k))
            if col:
                b_spec = pl.BlockSpec((None, tn, tk), lambda i, j, k: (k // (c // tk), j, k % (c // tk)))
            else:
                b_spec = pl.BlockSpec((None, tn, tk), lambda i, j, k: (j // (r // tn), j % (r // tn), k))
            out_shape = jax.ShapeDtypeStruct((m, kg), out_dtype)
            dims = NT
        o_spec = pl.BlockSpec((tm, tn), lambda i, j, k: (i, j))
    nk = grid[2]
    has_add = add is not None
    extra = [] if after is None else [after]

    def body(*refs):
        a_ref, b_ref = refs[0], refs[1]
        o_ref = refs[2 + has_add + len(extra)]
        k = pl.program_id(2)
        p = lax.dot_general(a_ref[...].astype(BF16), b_ref[...].astype(BF16), dims, preferred_element_type=F32)

        def finish(r_):
            if has_add:
                r_ = r_ + refs[2][...]
            o_ref[...] = r_.astype(o_ref.dtype)

        if nk == 1:
            finish(p)
        else:
            acc_ref = refs[-1]

            @pl.when(k == 0)
            def _():
                acc_ref[...] = p

            @pl.when(jnp.logical_and(k > 0, k < nk - 1))
            def _():
                acc_ref[...] += p

            @pl.when(k == nk - 1)
            def _():
                finish(acc_ref[...] + p)

    in_specs = [a_spec, b_spec] + ([pl.BlockSpec((tm, tn), lambda i, j, k: (i, j))] if has_add else [])
    in_specs += [pl.BlockSpec(memory_space=pl.ANY)] * len(extra)
    args = (a, b) + ((add,) if has_add else ()) + tuple(extra)
    return pl.pallas_call(
        body, name=name, grid=grid, in_specs=in_specs, out_specs=o_spec, out_shape=out_shape,
        scratch_shapes=[pltpu.VMEM((tm, tn), F32)] if nk > 1 else [],
        compiler_params=_cparams("parallel", "parallel", "arbitrary"),
    )(*args)


def _ffn_up(name, h, wg, wu):
    m, k = h.shape
    _, _, c = wg.shape
    n = NCHIP * c
    tm, tn = _tile(m, 1024), _tile(c, 1024)

    def body(h_ref, wg_ref, wu_ref, a_ref, b_ref, s_ref):
        hv = h_ref[...]
        a = jnp.dot(hv, wg_ref[...], preferred_element_type=F32)
        b = jnp.dot(hv, wu_ref[...], preferred_element_type=F32)
        a_ref[...] = a
        b_ref[...] = b
        s_ref[...] = (_silu(a) * b).astype(s_ref.dtype)

    w_spec = pl.BlockSpec((None, k, tn), lambda i, j: (j // (c // tn), 0, j % (c // tn)))
    o_spec = pl.BlockSpec((tm, tn), lambda i, j: (i, j))
    return pl.pallas_call(
        body, name=name, grid=(m // tm, n // tn),
        in_specs=[pl.BlockSpec((tm, k), lambda i, j: (i, 0)), w_spec, w_spec],
        out_specs=[o_spec, o_spec, o_spec],
        out_shape=[jax.ShapeDtypeStruct((m, n), F32), jax.ShapeDtypeStruct((m, n), F32),
                   jax.ShapeDtypeStruct((m, n), BF16)],
        compiler_params=_cparams("parallel", "parallel"),
    )(h, wg, wu)


def _dgrad_glu(name, df, wd, a, b, after=None):
    m = df.shape[0]
    _, r, c = wd.shape
    tm, tn = _tile(m, 512), _tile(r, 1024)
    extra = [] if after is None else [after]

    def body(df_ref, w_ref, a_ref, b_ref, *rest):
        da_ref, db_ref = rest[-2], rest[-1]
        ds = lax.dot_general(df_ref[...], w_ref[...], NT, preferred_element_type=F32)
        _, vjp = jax.vjp(_f_glu, a_ref[...], b_ref[...])
        da, db = vjp(ds)
        da_ref[...] = da.astype(da_ref.dtype)
        db_ref[...] = db.astype(db_ref.dtype)

    t_spec = pl.BlockSpec((tm, tn), lambda i, j: (i, j))
    sds = jax.ShapeDtypeStruct((m, NCHIP * r), BF16)
    return pl.pallas_call(
        body, name=name, grid=(m // tm, NCHIP * r // tn),
        in_specs=[pl.BlockSpec((tm, c), lambda i, j: (i, 0)),
                  pl.BlockSpec((None, tn, c), lambda i, j: (j // (r // tn), j % (r // tn), 0)), t_spec, t_spec]
        + [pl.BlockSpec(memory_space=pl.ANY)] * len(extra),
        out_specs=[t_spec, t_spec], out_shape=[sds, sds],
        compiler_params=_cparams("parallel", "parallel"),
    )(df, wd, a, b, *extra)


def _sgu_fwd(name, proj, b_in, lng, lnb, wm, bst, tr):
    n_rows = proj.shape[0]

    def body(uv_ref, buv_ref, g_ref, b_ref, wm_ref, bst_ref, ya_ref):
        u, vln = _f_mixa(uv_ref[...], buv_ref[...], g_ref[...], b_ref[...])
        vb = vln.astype(BF16)
        for ck in range(tr // CHUNK):
            rs = slice(ck * CHUNK, (ck + 1) * CHUNK)
            for h in range(HEADS):
                cs = slice(h * CHUNK, (h + 1) * CHUNK)
                vm = jnp.dot(wm_ref[h], vb[rs, cs], preferred_element_type=F32) + bst_ref[:, cs]
                ya_ref[rs, cs] = (u[rs, cs] * vm).astype(ya_ref.dtype)

    return pl.pallas_call(
        body, name=name, grid=(n_rows // tr,),
        in_specs=[pl.BlockSpec((tr, 2 * D), lambda i: (i, 0)), pl.BlockSpec((1, 2 * D), lambda i: (0, 0)),
                  pl.BlockSpec((1, D), lambda i: (0, 0)), pl.BlockSpec((1, D), lambda i: (0, 0)),
                  pl.BlockSpec((HEADS, CHUNK, CHUNK), lambda i: (0, 0, 0)),
                  pl.BlockSpec((CHUNK, D), lambda i: (0, 0))],
        out_specs=pl.BlockSpec((tr, D), lambda i: (i, 0)),
        out_shape=jax.ShapeDtypeStruct((n_rows, D), BF16),
        compiler_params=_cparams("parallel"),
    )(proj, b_in, lng, lnb, wm, bst)


def _sgu_bwd(name, proj, b_in, lng, lnb, wm, wmt, bst, dya, tr):
    n_rows = proj.shape[0]

    def body(uv_ref, buv_ref, g_ref, b_ref, wm_ref, wmt_ref, bst_ref, dya_ref,
             duv_ref, dbuv_ref, dg_ref, db_ref, dws_ref, dbs_ref, du_s, dvln_s):
        i = pl.program_id(0)

        @pl.when(i == 0)
        def _():
            dbuv_ref[...] = jnp.zeros(dbuv_ref.shape, F32)
            dg_ref[...] = jnp.zeros(dg_ref.shape, F32)
            db_ref[...] = jnp.zeros(db_ref.shape, F32)
            dws_ref[...] = jnp.zeros(dws_ref.shape, F32)
            dbs_ref[...] = jnp.zeros(dbs_ref.shape, F32)

        (u, vln), vjp = jax.vjp(_f_mixa, uv_ref[...], buv_ref[...], g_ref[...], b_ref[...])
        vb = vln.astype(BF16)
        dya_v = dya_ref[...].astype(F32)
        tpos = lax.broadcasted_iota(jnp.int32, (CHUNK, CHUNK), 0)
        spos = lax.broadcasted_iota(jnp.int32, (CHUNK, CHUNK), 1)
        causal = (tpos >= spos).astype(F32)
        for ck in range(tr // CHUNK):
            rs = slice(ck * CHUNK, (ck + 1) * CHUNK)
            for h in range(HEADS):
                cs = slice(h * CHUNK, (h + 1) * CHUNK)
                vbh = vb[rs, cs]
                vm = jnp.dot(wm_ref[h], vbh, preferred_element_type=F32) + bst_ref[:, cs]
                dyb = dya_v[rs, cs]
                du_s[rs, cs] = dyb * vm
                dvm = dyb * u[rs, cs]
                dvmb = dvm.astype(BF16)
                dvln_s[rs, cs] = jnp.dot(wmt_ref[h], dvmb, preferred_element_type=F32)
                dws_ref[h] += causal * lax.dot_general(dvmb, vbh, NT, preferred_element_type=F32)
                dbs_ref[:, cs] += jnp.broadcast_to(jnp.sum(dvm, axis=1, keepdims=True), (CHUNK, CHUNK))
        duv, dbuv, dg, db = vjp((du_s[...], dvln_s[...]))
        duv_ref[...] = duv.astype(duv_ref.dtype)
        dbuv_ref[...] += dbuv
        dg_ref[...] += dg
        db_ref[...] += db

    par = pl.BlockSpec((1, D), lambda i: (0, 0))
    par2 = pl.BlockSpec((1, 2 * D), lambda i: (0, 0))
    w_spec = pl.BlockSpec((HEADS, CHUNK, CHUNK), lambda i: (0, 0, 0))
    b_spec = pl.BlockSpec((CHUNK, D), lambda i: (0, 0))
    return pl.pallas_call(
        body, name=name, grid=(n_rows // tr,),
        in_specs=[pl.BlockSpec((tr, 2 * D), lambda i: (i, 0)), par2, par, par, w_spec, w_spec, b_spec,
                  pl.BlockSpec((tr, D), lambda i: (i, 0))],
        out_specs=[pl.BlockSpec((tr, 2 * D), lambda i: (i, 0)), par2, par, par, w_spec, b_spec],
        out_shape=[jax.ShapeDtypeStruct((n_rows, 2 * D), BF16), jax.ShapeDtypeStruct((1, 2 * D), F32),
                   jax.ShapeDtypeStruct((1, D), F32), jax.ShapeDtypeStruct((1, D), F32),
                   jax.ShapeDtypeStruct((HEADS, CHUNK, CHUNK), F32), jax.ShapeDtypeStruct((CHUNK, D), F32)],
        scratch_shapes=[pltpu.VMEM((tr, D), F32), pltpu.VMEM((tr, D), F32)],
        compiler_params=_cparams("arbitrary"),
    )(proj, b_in, lng, lnb, wm, wmt, bst, dya)


CT = 256
RB = 128
CV0 = 2 * D // CT
CG0 = 3 * D // CT


def _shift_bank(win):
    return [win] + [pltpu.roll(win, RB + HALO - r, 0) for r in range(1, 8)]


def _shifted(bank, sh):
    lo = 8 * (sh // 8)
    return bank[sh % 8][lo:lo + RB]


def _conv_fwd(name, proj, b_in, w, cb):
    n_rows = proj.shape[0]

    def body(cv_ref, cg_ref, bcv_ref, bcg_ref, w_ref, cb_ref, z1_ref, zp_ref):
        zp_ref[0:HALO, :] = jnp.zeros((HALO, CT), F32)
        zp_ref[HALO:, :] = (cv_ref[...] + bcv_ref[...]) * jax.nn.sigmoid(cg_ref[...] + bcg_ref[...])

        def blk(rb, carry):
            base = pl.multiple_of(rb * RB, RB)
            bank = _shift_bank(zp_ref[pl.ds(base, RB + HALO), :])
            acc = jnp.broadcast_to(cb_ref[...], (RB, CT))
            for k in range(KCONV):
                acc = acc + w_ref[k:k + 1, :] * _shifted(bank, k + 2)
            z1_ref[pl.ds(base, RB), :] = acc
            return carry

        lax.fori_loop(0, n_rows // RB, blk, 0)

    return pl.pallas_call(
        body, name=name, grid=(D // CT,),
        in_specs=[pl.BlockSpec((n_rows, CT), lambda j: (0, CV0 + j)), pl.BlockSpec((n_rows, CT), lambda j: (0, CG0 + j)),
                  pl.BlockSpec((1, CT), lambda j: (0, CV0 + j)), pl.BlockSpec((1, CT), lambda j: (0, CG0 + j)),
                  pl.BlockSpec((HALO, CT), lambda j: (0, j)), pl.BlockSpec((1, CT), lambda j: (0, j))],
        out_specs=pl.BlockSpec((n_rows, CT), lambda j: (0, j)),
        out_shape=jax.ShapeDtypeStruct((n_rows, D), F32),
        scratch_shapes=[pltpu.VMEM((n_rows + HALO, CT), F32)],
        compiler_params=_cparams("parallel"),
    )(proj, proj, b_in, b_in, w, cb)


def _conv_bwd(name, dz1, proj, b_in, w):
    n_rows = proj.shape[0]

    def body(dz_ref, cv_ref, cg_ref, bcv_ref, bcg_ref, w_ref, dcv_ref, dcg_ref, sm_ref, zp_ref, dzp_ref):
        cvb = cv_ref[...] + bcv_ref[...]
        sg = jax.nn.sigmoid(cg_ref[...] + bcg_ref[...])
        zp_ref[0:HALO, :] = jnp.zeros((HALO, CT), F32)
        zp_ref[HALO:, :] = cvb * sg
        dz = dz_ref[...]
        dzp_ref[0:n_rows, :] = dz
        dzp_ref[n_rows:, :] = jnp.zeros((HALO, CT), F32)
        sm_ref[...] = jnp.zeros(sm_ref.shape, F32)
        sm_ref[32:33, :] = jnp.sum(dz, axis=0, keepdims=True)

        def blk(rb, carry):
            base = pl.multiple_of(rb * RB, RB)
            dbank = _shift_bank(dzp_ref[pl.ds(base, RB + HALO), :])
            zbank = _shift_bank(zp_ref[pl.ds(base, RB + HALO), :])
            dzb = dbank[0][0:RB]
            acc = jnp.zeros((RB, CT), F32)
            for k in range(KCONV):
                acc = acc + w_ref[k:k + 1, :] * _shifted(dbank, KCONV - 1 - k)
                sm_ref[k:k + 1, :] += jnp.sum(dzb * _shifted(zbank, k + 2), axis=0, keepdims=True)
            dzp_ref[pl.ds(base, RB), :] = acc
            return carry

        lax.fori_loop(0, n_rows // RB, blk, 0)
        dz0 = dzp_ref[0:n_rows, :]
        dcv = dz0 * sg
        dcg = dz0 * cvb * (sg * (1.0 - sg))
        dcv_ref[...] = dcv.astype(dcv_ref.dtype)
        dcg_ref[...] = dcg.astype(dcg_ref.dtype)
        sm_ref[33:34, :] = jnp.sum(dcv, axis=0, keepdims=True)
        sm_ref[34:35, :] = jnp.sum(dcg, axis=0, keepdims=True)

    col = pl.BlockSpec((n_rows, CT), lambda j: (0, j))
    return pl.pallas_call(
        body, name=name, grid=(D // CT,),
        in_specs=[col, pl.BlockSpec((n_rows, CT), lambda j: (0, CV0 + j)), pl.BlockSpec((n_rows, CT), lambda j: (0, CG0 + j)),
                  pl.BlockSpec((1, CT), lambda j: (0, CV0 + j)), pl.BlockSpec((1, CT), lambda j: (0, CG0 + j)),
                  pl.BlockSpec((HALO, CT), lambda j: (0, j))],
        out_specs=[col, col, pl.BlockSpec((40, CT), lambda j: (0, j))],
        out_shape=[jax.ShapeDtypeStruct((n_rows, D), BF16), jax.ShapeDtypeStruct((n_rows, D), BF16),
                   jax.ShapeDtypeStruct((40, D), F32)],
        scratch_shapes=[pltpu.VMEM((n_rows + HALO, CT), F32), pltpu.VMEM((n_rows + HALO, CT), F32)],
        compiler_params=_cparams("parallel"),
    )(dz1, proj, proj, b_in, b_in, w)


ADA_TN = 768


def _split_bf16(v):
    hi = v.astype(BF16)
    return hi, (v - hi.astype(F32)).astype(BF16)


def _ada_fwd(name, c_all, w, b):
    n = w.shape[1]

    def body(c_ref, w_ref, b_ref, o_ref):
        ch, cl = _split_bf16(_silu(c_ref[...]))
        wh, wl = _split_bf16(w_ref[...])
        acc = jnp.dot(ch, wl, preferred_element_type=F32) + jnp.dot(cl, wh, preferred_element_type=F32)
        o_ref[...] = acc + jnp.dot(ch, wh, preferred_element_type=F32) + b_ref[...]

    return pl.pallas_call(
        body, name=name, grid=(n // ADA_TN,),
        in_specs=[pl.BlockSpec((8, D), lambda j: (0, 0)), pl.BlockSpec((D, ADA_TN), lambda j: (0, j)),
                  pl.BlockSpec((1, ADA_TN), lambda j: (0, j))],
        out_specs=pl.BlockSpec((8, ADA_TN), lambda j: (0, j)),
        out_shape=jax.ShapeDtypeStruct((8, n), F32),
        compiler_params=_cparams("parallel"),
    )(c_all, w, b)


def _ada_wgrad(name, c_all_t, dmod):
    n = dmod.shape[1]

    def body(ct_ref, dm_ref, o_ref):
        ca = _silu(ct_ref[...])
        acc = ca[:, 0:1] * dm_ref[0:1, :]
        for r in range(1, 8):
            acc = acc + ca[:, r:r + 1] * dm_ref[r:r + 1, :]
        o_ref[...] = acc

    return pl.pallas_call(
        body, name=name, grid=(n // ADA_TN,),
        in_specs=[pl.BlockSpec((D, 8), lambda j: (0, 0)), pl.BlockSpec((8, ADA_TN), lambda j: (0, j))],
        out_specs=pl.BlockSpec((D, ADA_TN), lambda j: (0, j)),
        out_shape=jax.ShapeDtypeStruct((D, n), F32),
        compiler_params=_cparams("parallel"),
    )(c_all_t, dmod)


def _adamw(name, w, g, m, v, tr):
    rows, cols = w.shape

    def body(w_ref, g_ref, m_ref, v_ref, g_out, d_ref, nm_ref, nv_ref):
        gv = g_ref[...]
        nm = ADAM_B1 * m_ref[...] + (1.0 - ADAM_B1) * gv
        nv = ADAM_B2 * v_ref[...] + (1.0 - ADAM_B2) * (gv * gv)
        m_hat = nm / (1.0 - ADAM_B1 ** ADAM_STEP)
        v_hat = nv / (1.0 - ADAM_B2 ** ADAM_STEP)
        d_ref[...] = -ADAM_LR * (m_hat / (jnp.sqrt(v_hat) + ADAM_EPS) + ADAM_WD * w_ref[...])
        nm_ref[...] = nm
        nv_ref[...] = nv
        g_out[...] = gv

    spec = pl.BlockSpec((tr, cols), lambda i: (i, 0))
    sds = jax.ShapeDtypeStruct((rows, cols), F32)
    return pl.pallas_call(
        body, name=name, grid=(rows // tr,), in_specs=[spec] * 4, out_specs=[spec] * 4, out_shape=[sds] * 4,
        compiler_params=_cparams("parallel"),
    )(w, g, m, v)


def _sum8(name, stacked, tr, after=None):
    n, rows, cols = stacked.shape
    extra = _afters(after)

    def body(s_ref, *rest):
        o_ref = rest[-1]
        acc = s_ref[0]
        for r in range(1, n):
            acc = acc + s_ref[r]
        o_ref[...] = acc

    return pl.pallas_call(
        body, name=name, grid=(rows // tr,),
        in_specs=[pl.BlockSpec((n, tr, cols), lambda i: (0, i, 0))] + [pl.BlockSpec(memory_space=pl.ANY)] * len(extra),
        out_specs=pl.BlockSpec((tr, cols), lambda i: (i, 0)),
        out_shape=jax.ShapeDtypeStruct((rows, cols), F32),
        compiler_params=_cparams("parallel"),
    )(stacked, *extra)


def _pair_add(name, g5, other, c_idx, tr):
    nq, _, rows, cols = g5.shape

    def body(c_ref, g_ref, o_ref, p_ref):
        p_ref[...] = (g_ref[...].astype(F32) + o_ref[...].astype(F32)).astype(p_ref.dtype)

    return pl.pallas_call(
        body, name=name,
        grid_spec=pltpu.PrefetchScalarGridSpec(
            num_scalar_prefetch=1, grid=(nq, rows // tr),
            in_specs=[pl.BlockSpec((None, None, tr, cols), lambda qi, i, cr: (qi, cr[0], i, 0)),
                      pl.BlockSpec((None, tr, cols), lambda qi, i, cr: (qi, i, 0))],
            out_specs=pl.BlockSpec((None, tr, cols), lambda qi, i, cr: (qi, i, 0))),
        out_shape=jax.ShapeDtypeStruct((nq, rows, cols), BF16),
        compiler_params=_cparams("parallel", "parallel"),
    )(c_idx, g5, other)


def _chip_add(name, p, recv, qc_idx, tr):
    _, rows, cols = p.shape

    def body(qc_ref, p_ref, r_ref, o_ref):
        acc = p_ref[...].astype(F32)
        for k in range(NCHIP - 1):
            acc = acc + r_ref[k].astype(F32)
        o_ref[...] = acc

    return pl.pallas_call(
        body, name=name,
        grid_spec=pltpu.PrefetchScalarGridSpec(
            num_scalar_prefetch=1, grid=(rows // tr,),
            in_specs=[pl.BlockSpec((None, tr, cols), lambda i, qc: (qc[0], i, 0)),
                      pl.BlockSpec((NCHIP - 1, tr, cols), lambda i, qc: (0, i, 0))],
            out_specs=pl.BlockSpec((None, tr, cols), lambda i, qc: (qc[1], i, 0))),
        out_shape=jax.ShapeDtypeStruct((2, rows, cols), F32),
        compiler_params=_cparams("parallel"),
    )(qc_idx, p, recv)


def _allgather8(name, blk):
    m_per, n = blk.shape

    def body(x_ref, out_ref, send_sems, recv_sems, local_sem):
        x, y, c = lax.axis_index("x"), lax.axis_index("y"), lax.axis_index("c")
        me, sibling = (x, y, c), (x, y, 1 - c)
        chips = [(1 - x, y), (x, 1 - y), (1 - x, 1 - y)]

        def rows(px, py, pc):
            return out_ref.at[pl.ds((4 * px + 2 * py + pc) * m_per, m_per), :]

        def copy(k, block, to, src=None):
            return pltpu.make_async_remote_copy(
                src_ref=rows(*block) if src is None else src, dst_ref=rows(*block),
                send_sem=send_sems.at[k], recv_sem=recv_sems.at[k], device_id=to, device_id_type=MESH)

        mine = pltpu.make_async_copy(x_ref, rows(*me), local_sem)
        mine.start()
        first = [copy(0, me, sibling, src=x_ref)]
        first += [copy(1 + j, me, (*chip, c), src=x_ref) for j, chip in enumerate(chips)]
        for cp in first:
            cp.start()
        passed = [copy(4 + j, (*chip, c), sibling) for j, chip in enumerate(chips)]
        for j, chip in enumerate(chips):
            copy(1 + j, (*chip, c), me).wait_recv()
            passed[j].start()
        copy(0, sibling, me).wait_recv()
        for j, chip in enumerate(chips):
            copy(4 + j, (*chip, 1 - c), me).wait_recv()
        for cp in first + passed:
            cp.wait_send()
        mine.wait()

    return pl.pallas_call(
        body, name=name,
        out_shape=jax.ShapeDtypeStruct((8 * m_per, n), blk.dtype),
        in_specs=[pl.BlockSpec(memory_space=pltpu.VMEM)],
        out_specs=pl.BlockSpec(memory_space=pltpu.VMEM),
        scratch_shapes=[pltpu.SemaphoreType.DMA((7,)), pltpu.SemaphoreType.DMA((7,)), pltpu.SemaphoreType.DMA],
        compiler_params=pltpu.CompilerParams(vmem_limit_bytes=VMEM_LIMIT),
    )(blk)


def _exchange(name, srcs, out_shapes, n_local, n_remote, plan, aliases=None, after=None):
    ni, no = len(srcs), len(out_shapes)
    extra = [] if after is None else [after]

    def body(*refs):
        ins, outs = refs[:ni], refs[ni + len(extra):ni + len(extra) + no]
        send_sems, recv_sems, local_sems = refs[ni + len(extra) + no:]
        x, y, c = lax.axis_index("x"), lax.axis_index("y"), lax.axis_index("c")
        local, remote = plan(ins, outs, x, y, c)
        assert len(local) == n_local and len(remote) == n_remote

        def rcopy(i, dst):
            s, _, peer, _, _ = remote[i]
            return pltpu.make_async_remote_copy(src_ref=s, dst_ref=dst, send_sem=send_sems.at[i],
                                                recv_sem=recv_sems.at[i], device_id=peer, device_id_type=MESH)

        lcs = [pltpu.make_async_copy(s, d, local_sems.at[i]) for i, (s, d) in enumerate(local)]
        for cp in lcs:
            cp.start()
        first = [i for i in range(n_remote) if remote[i][4] is None]
        passed = [i for i in range(n_remote) if remote[i][4] is not None]
        for i in first:
            rcopy(i, remote[i][1]).start()
        arrived = set()
        for i in passed:
            j = remote[i][4]
            rcopy(j, remote[j][3]).wait_recv()
            arrived.add(j)
            rcopy(i, remote[i][1]).start()
        for i in range(n_remote):
            if i not in arrived:
                rcopy(i, remote[i][3]).wait_recv()
        for i in range(n_remote):
            rcopy(i, remote[i][1]).wait_send()
        for cp in lcs:
            cp.wait()

    any_spec = pl.BlockSpec(memory_space=pl.ANY)
    return pl.pallas_call(
        body, name=name, out_shape=out_shapes,
        in_specs=[any_spec] * (ni + len(extra)), out_specs=[any_spec] * no,
        input_output_aliases=aliases or {},
        scratch_shapes=[pltpu.SemaphoreType.DMA((n_remote,)), pltpu.SemaphoreType.DMA((n_remote,)),
                        pltpu.SemaphoreType.DMA((max(n_local, 1),))],
    )(*srcs, *extra)


_CHIP_FLIPS = ((0, 1), (1, 0), (1, 1))


def _flip(v, f):
    return 1 - v if f else v


EFFECT = pltpu.SideEffectType.DATAFLOW_SIDE_EFFECTING


def _split_start(name, bufs, n, plan, after=None):
    nb = len(bufs)
    extra = [] if after is None else [after]

    def body(*refs):
        ins = refs[:nb]
        send_sems, recv_sems = refs[nb + len(extra)], refs[nb + len(extra) + 1]
        token = refs[-1]
        x, y, c = lax.axis_index("x"), lax.axis_index("y"), lax.axis_index("c")
        copies = plan(ins, x, y, c)
        assert len(copies) == n
        for i, (s, d, peer, _) in enumerate(copies):
            pltpu.make_async_remote_copy(src_ref=s, dst_ref=d, send_sem=send_sems.at[i], recv_sem=recv_sems.at[i],
                                         device_id=peer, device_id_type=MESH).start()
        token[...] = jnp.zeros_like(token)

    hbm = pl.BlockSpec(memory_space=pltpu.HBM)
    sem = pl.BlockSpec(memory_space=pltpu.SEMAPHORE)
    res = pl.pallas_call(
        body, name=name,
        out_shape=(pltpu.SemaphoreType.DMA((n,)), pltpu.SemaphoreType.DMA((n,)),
                   *[pltpu.HBM(b.shape, b.dtype) for b in bufs], jax.ShapeDtypeStruct((8, 128), F32)),
        in_specs=[hbm] * nb + [pl.BlockSpec(memory_space=pl.ANY)] * len(extra),
        out_specs=(sem, sem, *[hbm] * nb, pl.BlockSpec(memory_space=pltpu.VMEM)),
        input_output_aliases={i: 2 + i for i in range(nb)},
        compiler_params=pltpu.CompilerParams(has_side_effects=EFFECT),
    )(*[pltpu.with_memory_space_constraint(b, pltpu.HBM) for b in bufs], *extra)
    return res[0], res[1], list(res[2:2 + nb]), res[-1]


def _split_wait(name, bufs, send_sems, recv_sems, after, n, plan):
    nb = len(bufs)

    def body(*refs):
        ins = refs[:nb]
        ssem, rsem = refs[nb], refs[nb + 1]
        x, y, c = lax.axis_index("x"), lax.axis_index("y"), lax.axis_index("c")
        copies = plan(ins, x, y, c)
        assert len(copies) == n
        for i, (s, _, peer, lands) in enumerate(copies):
            cp = pltpu.make_async_remote_copy(src_ref=s, dst_ref=lands, send_sem=ssem.at[i], recv_sem=rsem.at[i],
                                              device_id=peer, device_id_type=MESH)
            cp.wait_send()
            cp.wait_recv()

    hbm = pl.BlockSpec(memory_space=pltpu.HBM)
    sem = pl.BlockSpec(memory_space=pltpu.SEMAPHORE)
    afters = list(after) if isinstance(after, (list, tuple)) else [after]
    res = pl.pallas_call(
        body, name=name,
        out_shape=[pltpu.HBM(b.shape, b.dtype) for b in bufs],
        in_specs=[hbm] * nb + [sem, sem] + [pl.BlockSpec(memory_space=pl.ANY)] * len(afters),
        out_specs=[hbm] * nb,
        input_output_aliases={i: i for i in range(nb)},
        compiler_params=pltpu.CompilerParams(has_side_effects=EFFECT),
    )(*bufs, send_sems, recv_sems, *afters)
    return list(res)


def _chips_of(x, y):
    return [(_flip(x, fx), _flip(y, fy)) for fx, fy in _CHIP_FLIPS]


def _gather_start(tag, stacks, after):
    n = len(stacks)

    def plan(refs, x, y, c):
        q = 2 * x + y
        return [(refs[a].at[q, c], refs[a].at[q, c], (px, py, c), refs[a].at[2 * px + py, c])
                for a in range(n) for px, py in _chips_of(x, y)]

    send, recv, thru, token = _split_start(f"gather_start_{tag}", stacks, 3 * n, plan, after)
    return (send, recv, thru, plan), token


def _gather_finish(tag, state, after):
    send, recv, thru, plan = state
    n = len(thru)
    landed = _split_wait(f"gather_wait_{tag}", thru, send, recv, after, 3 * n, plan)

    def pass_on(ins, outs, x, y, c):
        sib = (x, y, 1 - c)
        remote = []
        for a in range(n):
            for px, py in _chips_of(x, y):
                p = 2 * px + py
                remote.append((ins[a].at[p, c], outs[a].at[p, c], sib, outs[a].at[p, 1 - c], None))
        return [], remote

    full = _exchange(f"gather_pass_{tag}", landed, [jax.ShapeDtypeStruct(s.shape, s.dtype) for s in landed],
                     0, 3 * n, pass_on, aliases={a: a for a in range(n)})
    return [g.reshape(NCHIP, 2 * g.shape[2], g.shape[3]) for g in full]


def _gather_land(tag, state, after):
    send, recv, thru, plan = state
    n = len(thru)
    landed = _split_wait(f"gather_wait_{tag}", thru, send, recv, after, 3 * n, plan)

    def pass_plan(refs, x, y, c):
        sib = (x, y, 1 - c)
        return [(refs[a].at[2 * px + py, c], refs[a].at[2 * px + py, c], sib, refs[a].at[2 * px + py, 1 - c])
                for a in range(n) for px, py in _chips_of(x, y)]

    send, recv, thru, token = _split_start(f"gather_pass_start_{tag}", landed, 3 * n, pass_plan)
    return (send, recv, thru, pass_plan), token


def _gather_done(tag, state, after):
    send, recv, thru, plan = state
    full = _split_wait(f"gather_pass_wait_{tag}", thru, send, recv, after, 3 * len(thru), plan)
    return [g.reshape(NCHIP, 2 * g.shape[2], g.shape[3]) for g in full]


_OTHERS = tuple((fx, fy, fc) for fx in (0, 1) for fy in (0, 1) for fc in (0, 1) if fx or fy or fc)


def _allgather_start(tag, buf, after=None):
    def plan(refs, x, y, c):
        me = 4 * x + 2 * y + c
        copies = []
        for fx, fy, fc in _OTHERS:
            px, py, pc = _flip(x, fx), _flip(y, fy), _flip(c, fc)
            copies.append((refs[0].at[me], refs[0].at[me], (px, py, pc), refs[0].at[4 * px + 2 * py + pc]))
        return copies

    send, recv, thru, token = _split_start(f"{tag}_start", [buf], len(_OTHERS), plan, after)
    return (send, recv, thru, plan), token


def _allgather_done(tag, state, after):
    send, recv, thru, plan = state
    return _split_wait(f"{tag}_wait", thru, send, recv, after, len(_OTHERS), plan)[0]


def _gather_small(name, smalls):
    n = len(smalls)

    def plan(ins, outs, x, y, c):
        q = 2 * x + y
        local = [(ins[b], outs[b].at[q]) for b in range(n)]
        remote = [(ins[b], outs[b].at[q], (px, py, c), outs[b].at[2 * px + py], None)
                  for b in range(n) for px, py in _chips_of(x, y)]
        return local, remote

    return _exchange(name, smalls, [jax.ShapeDtypeStruct((NCHIP,) + s.shape, s.dtype) for s in smalls], n, 3 * n, plan)


def _rs_pair_start(tag, grads, after=None):
    n = len(grads)
    g5 = [g.reshape(NCHIP, 2, g.shape[1] // 2, g.shape[2]) for g in grads]
    half = [(g.shape[2], g.shape[3]) for g in g5]
    land = [lax.empty((NCHIP,) + h, g.dtype) for h, g in zip(half, grads)]

    def plan(refs, x, y, c):
        sib = (x, y, 1 - c)
        return [(refs[a].at[:, 1 - c], refs[n + a], sib, refs[n + a]) for a in range(n)]

    send, recv, thru, token = _split_start(f"rs_pair_start_{tag}", g5 + land, n, plan, after)
    return (send, recv, thru, plan, half), token


def _rs_chips_start(tag, state, after, c_idx):
    send, recv, thru, pair_plan, half = state
    n = len(half)
    landed = _split_wait(f"rs_pair_wait_{tag}", thru, send, recv, after, n, pair_plan)
    g5, got = landed[:n], landed[n:]
    part = [_pair_add(f"rs_pair_add_{tag}{a}", g5[a], got[a], c_idx, _tile(half[a][0], 512)) for a in range(n)]
    land = [lax.empty((NCHIP - 1,) + h, BF16) for h in half]

    def plan(refs, x, y, c):
        return [(refs[a].at[2 * px + py], refs[n + a].at[k], (px, py, c), refs[n + a].at[k])
                for a in range(n) for k, (px, py) in enumerate(_chips_of(x, y))]

    send, recv, thru, token = _split_start(f"rs_chips_start_{tag}", part + land, 3 * n, plan)
    return (send, recv, thru, plan, half), token


def _rs_finish(tag, state, after, qc_idx):
    send, recv, thru, plan, half = state
    n = len(half)
    landed = _split_wait(f"rs_chips_wait_{tag}", thru, send, recv, after, 3 * n, plan)
    red = [_chip_add(f"rs_chip_add_{tag}{a}", landed[a], landed[n + a], qc_idx, _tile(half[a][0], 512)) for a in range(n)]

    def share(ins, outs, x, y, c):
        sib = (x, y, 1 - c)
        return [], [(ins[a].at[c], outs[a].at[c], sib, outs[a].at[1 - c], None) for a in range(n)]

    full = _exchange(f"rs_share_{tag}", red, [jax.ShapeDtypeStruct((2,) + h, F32) for h in half], 0, n, share,
                     aliases={a: a for a in range(n)})
    return [f.reshape(2 * h[0], h[1]) for f, h in zip(full, half)]


def _rs_land(tag, state, after, qc_idx):
    send, recv, thru, plan, half = state
    n = len(half)
    landed = _split_wait(f"rs_chips_wait_{tag}", thru, send, recv, after, 3 * n, plan)
    red = [_chip_add(f"rs_chip_add_{tag}{a}", landed[a], landed[n + a], qc_idx, _tile(half[a][0], 512)) for a in range(n)]

    def share(refs, x, y, c):
        sib = (x, y, 1 - c)
        return [(refs[a].at[c], refs[a].at[c], sib, refs[a].at[1 - c]) for a in range(n)]

    send, recv, thru, token = _split_start(f"rs_share_start_{tag}", red, n, share)
    return (send, recv, thru, share, half), token


def _rs_done(tag, state, after):
    send, recv, thru, plan, half = state
    full = _split_wait(f"rs_share_wait_{tag}", thru, send, recv, after, len(half), plan)
    return [f.reshape(2 * h[0], h[1]) for f, h in zip(full, half)]


def _place(name, blk, idx):
    rows, cols = blk.shape

    def body(i_ref, b_ref, o_ref):
        o_ref[...] = b_ref[...]

    return pl.pallas_call(
        body, name=name,
        grid_spec=pltpu.PrefetchScalarGridSpec(
            num_scalar_prefetch=1, grid=(1,),
            in_specs=[pl.BlockSpec((rows, cols), lambda i, d: (0, 0))],
            out_specs=pl.BlockSpec((None, rows, cols), lambda i, d: (d[0], 0, 0))),
        out_shape=jax.ShapeDtypeStruct((8, rows, cols), blk.dtype),
        compiler_params=_cparams("arbitrary"),
    )(idx, blk)


def _pack(name, n_rows, pieces, after=None):
    arrays = [p[1] for p in pieces if p[0] != "zeros"]
    extra = [] if after is None else [after]

    def body(*refs):
        o_ref = refs[-1]
        o_ref[...] = jnp.zeros(o_ref.shape, F32)
        row, ai = 0, 0
        for p in pieces:
            kind = p[0]
            if kind == "zeros":
                row += p[1]
                continue
            ref = refs[len(extra) + ai]
            ai += 1
            if kind == "rows":
                r = ref.shape[0]
                o_ref[row:row + r, :] = ref[...]
            elif kind == "wide":
                r = ref.shape[1] // D
                for j in range(r):
                    o_ref[row + j:row + j + 1, :] = ref[:, j * D:(j + 1) * D]
            elif kind == "slice":
                r = p[3] - p[2]
                o_ref[row:row + r, :] = ref[p[2]:p[3], :]
            elif kind == "heads":
                r = CHUNK
                for h in range(HEADS):
                    o_ref[row:row + r, h * CHUNK:(h + 1) * CHUNK] = ref[h]
            else:
                r = 1
                o_ref[row:row + 1, 0:128] = ref[0:1, :]
            row += r
        assert row == n_rows, (row, n_rows)

    vmem = pl.BlockSpec(memory_space=pltpu.VMEM)
    return pl.pallas_call(
        body, name=name, out_shape=jax.ShapeDtypeStruct((n_rows, D), F32),
        in_specs=[pl.BlockSpec(memory_space=pl.ANY)] * len(extra) + [vmem] * len(arrays), out_specs=vmem,
        compiler_params=pltpu.CompilerParams(vmem_limit_bytes=VMEM_LIMIT),
    )(*extra, *arrays)


def _to_bf16_stack(name, w, q_idx):
    r, cols = w.shape
    half = r // 2
    tr = _tile(half, 256)
    nb = half // tr

    def body(q_ref, w_ref, o_ref):
        o_ref[...] = w_ref[...].astype(BF16)

    return pl.pallas_call(
        body, name=name,
        grid_spec=pltpu.PrefetchScalarGridSpec(
            num_scalar_prefetch=1, grid=(2 * nb,),
            in_specs=[pl.BlockSpec((tr, cols), lambda i, qr: (i, 0))],
            out_specs=pl.BlockSpec((None, None, tr, cols), lambda i, qr: (qr[0], i // nb, i % nb, 0))),
        out_shape=jax.ShapeDtypeStruct((NCHIP, 2, half, cols), BF16),
        compiler_params=_cparams("parallel"),
    )(q_idx, w)


def kernel(x, c, ada_w, ada_b, norm_ffn1, ffn1_w_gate, ffn1_w_up, ffn1_w_down, norm_mix, mix_w_in, mix_b_in, sgu_ln_g, sgu_ln_b, sgu_w_s, sgu_b_s, conv_w, conv_b, conv_ln_g, conv_ln_b, w_branch_a, w_branch_b, w_out, norm_ffn2, ffn2_w_gate, ffn2_w_up, ffn2_w_down, norm_final, loss_target, m_ada_w, m_ada_b, m_norm_ffn1, m_ffn1_w_gate, m_ffn1_w_up, m_ffn1_w_down, m_norm_mix, m_mix_w_in, m_mix_b_in, m_sgu_ln_g, m_sgu_ln_b, m_sgu_w_s, m_sgu_b_s, m_conv_w, m_conv_b, m_conv_ln_g, m_conv_ln_b, m_w_branch_a, m_w_branch_b, m_w_out, m_norm_ffn2, m_ffn2_w_gate, m_ffn2_w_up, m_ffn2_w_down, m_norm_final, v_ada_w, v_ada_b, v_norm_ffn1, v_ffn1_w_gate, v_ffn1_w_up, v_ffn1_w_down, v_norm_mix, v_mix_w_in, v_mix_b_in, v_sgu_ln_g, v_sgu_ln_b, v_sgu_w_s, v_sgu_b_s, v_conv_w, v_conv_b, v_conv_ln_g, v_conv_ln_b, v_w_branch_a, v_w_branch_b, v_w_out, v_norm_ffn2, v_ffn2_w_gate, v_ffn2_w_up, v_ffn2_w_down, v_norm_final):
    xi, yi, ci = lax.axis_index("x"), lax.axis_index("y"), lax.axis_index("c")
    q = 2 * xi + yi
    dev = 4 * xi + 2 * yi + ci
    c_idx = jnp.reshape(ci, (1,)).astype(jnp.int32)
    q_idx = jnp.reshape(q, (1,)).astype(jnp.int32)
    x0 = x[0]
    tgt = loss_target[0]
    qc_idx = jnp.stack([q, ci]).astype(jnp.int32)
    n_ada = ada_w.shape[2]

    c_all = _allgather8("ag_c", jnp.pad(c, ((0, 7), (0, 0))))[::8]
    ada_b_mine = lax.dynamic_slice(ada_b, (0, q * n_ada), (1, n_ada))
    mod_part = _ada_fwd("ada_fwd", c_all, ada_w[0], ada_b_mine)
    small_parts = _gather_small("gather_small", [mod_part, jnp.pad(conv_w[0], ((0, 1), (0, 0)))])

    big = dict(g1=ffn1_w_gate, u1=ffn1_w_up, d1=ffn1_w_down, win=mix_w_in, wa=w_branch_a, wb=w_branch_b, wo=w_out,
               g2=ffn2_w_gate, u2=ffn2_w_up, d2=ffn2_w_down)
    groups = dict(f1=("d1", "g1", "u1"), mx=("wo", "wa", "wb", "win"), f2=("d2", "g2", "u2"))
    gather_groups = dict(up1=("g1", "u1"), dn1=("d1",), win=("win",), mx=("wa", "wb", "wo"), f2=("g2", "u2", "d2"))
    gather_state = {}
    started = [small_parts[0]]
    for tag, keys in gather_groups.items():
        gather_state[tag], token = _gather_start(
            tag, [_to_bf16_stack(f"cast_{k}", big[k][0], q_idx) for k in keys], started[-1])
        started.append(token)

    mod_all = jnp.transpose(small_parts[0], (1, 0, 2)).reshape(8, N_MOD * D)
    mod = lax.dynamic_slice(mod_all, (dev, 0), (1, N_MOD * D))
    sh1, sc1, g1, sh2, sc2, g2, sh3, sc3 = [(mod, D, k) for k in range(N_MOD - 1)]
    g3 = mod[:, (N_MOD - 1) * D:]
    convw = jnp.transpose(small_parts[1], (1, 0, 2)).reshape(HALO, D)
    W = dict(zip(gather_groups["up1"], _gather_finish("up1", gather_state["up1"], [mod] + started)))

    causal = jnp.tril(jnp.ones((CHUNK, CHUNK), bool))
    wm_f = jnp.where(causal[None], sgu_w_s[0], 0.0)
    wm = wm_f.astype(BF16)
    wmt = jnp.swapaxes(wm_f, 1, 2).astype(BF16)
    bst = jnp.repeat(jnp.transpose(sgu_b_s[0]), CHUNK, axis=1)
    b_in = mix_b_in

    (h1,) = _rowwise_fwd("pre1", _f_pre, [x0], [norm_ffn1, sc1, sh1], [(D, BF16)], 256)
    a1, b1, s1 = _ffn_up("up1", h1, W["g1"], W["u1"])
    W.update(zip(gather_groups["dn1"], _gather_finish("dn1", gather_state["dn1"], s1)))
    f1 = _matmul("down1", s1, W["d1"], "fwd", False, F32)
    pass_win, token = _gather_land("win", gather_state["win"], f1)
    x1, h2 = _rowwise_fwd("res1", _f_res_pre(0.5), [x0, f1], [g1, norm_mix, sc2, sh2],
                          [(D, F32), (D, BF16)], 256, after=token)
    W.update(zip(gather_groups["win"], _gather_done("win", pass_win, x1)))
    pass_mx, token = _gather_land("mx", gather_state["mx"], x1)
    proj = _matmul("in_proj", h2, W["win"], "fwd", True, F32, after=token)
    W.update(zip(gather_groups["mx"], _gather_done("mx", pass_mx, proj)))
    ya = _sgu_fwd("sgu", proj, b_in, sgu_ln_g, sgu_ln_b, wm, bst, 256)
    z1 = _conv_fwd("conv", proj, b_in, convw, conv_b)
    (z3,) = _rowwise_fwd("conv_ln", _f_lnsilu, [z1], [conv_ln_g, conv_ln_b], [(D, BF16)], 256)
    y_a = _matmul("branch_a", ya, W["wa"], "fwd", False, F32)
    y_b = _matmul("branch_b", z3, W["wb"], "fwd", False, F32)
    pass_f2, token = _gather_land("f2", gather_state["f2"], y_b)
    (merged,) = _rowwise_fwd("merge", _f_merge, [(proj, 2 * D, 2), y_a, y_b], [(b_in, 2 * D, 2)],
                             [(D, BF16)], 256, after=token)
    y = _matmul("out_proj", merged, W["wo"], "fwd", False, F32)
    x2, h3 = _rowwise_fwd("res2", _f_res_pre(1.0), [x1, y], [g2, norm_ffn2, sc3, sh3], [(D, F32), (D, BF16)], 256)
    W.update(zip(gather_groups["f2"], _gather_done("f2", pass_f2, x2)))
    a3, b3, s3 = _ffn_up("up2", h3, W["g2"], W["u2"])
    f3 = _matmul("down2", s3, W["d2"], "fwd", False, F32)
    dx2, df3, dg3, dnwf, loss_blk = _final("final", x2, f3, tgt, g3, norm_final.reshape(1, D), 256)

    G = {}

    def ffn_bwd(tag, df, s, a, b, h, wg, wu, wd, after=None):
        da, db = _dgrad_glu(f"d_down{tag}", df, wd, a, b, after)
        G["d" + tag] = _matmul(f"w_down{tag}", s, df, "wgrad", False, BF16)
        dh = _matmul(f"d_gate{tag}", da, wg, "dgrad", True, F32)
        dh = _matmul(f"d_up{tag}", db, wu, "dgrad", True, F32, add=dh)
        G["g" + tag] = _matmul(f"w_gate{tag}", h, da, "wgrad", True, BF16)
        G["u" + tag] = _matmul(f"w_up{tag}", h, db, "wgrad", True, BF16)
        return dh

    reduced = {}
    dh3 = ffn_bwd("2", df3, s3, a3, b3, h3, W["g2"], W["u2"], W["d2"])
    pair_f2, token = _rs_pair_start("f2", [G[k] for k in groups["f2"]])
    (dx1, dy), (dg2, dnw3, dsc3, dsh3) = _rowwise_bwd(
        "res2_b", _f_res_pre(1.0), [x1, y], [g2, norm_ffn2, sc3, sh3], [dx2, dh3], [F32, BF16], 256, after=token)
    dmerged = _matmul("d_out", dy, W["wo"], "dgrad", False, F32)
    G["wo"] = _matmul("w_out", merged, dy, "wgrad", False, BF16)
    rs_f2, token = _rs_chips_start("f2", pair_f2, [dmerged, G["wo"]], c_idx)
    (dgab, dy_a, dy_b), (dbgab,) = _rowwise_bwd(
        "merge_b", _f_merge, [(proj, 2 * D, 2), y_a, y_b], [(b_in, 2 * D, 2)], [dmerged],
        [BF16, BF16, BF16], 256, after=token)
    dya = _matmul("d_a", dy_a, W["wa"], "dgrad", False, F32)
    dz3 = _matmul("d_b", dy_b, W["wb"], "dgrad", False, F32)
    G["wa"] = _matmul("w_a", ya, dy_a, "wgrad", False, BF16)
    G["wb"] = _matmul("w_b", z3, dy_b, "wgrad", False, BF16)
    (dz1,), (dclg, dclb) = _rowwise_bwd("conv_ln_b", _f_lnsilu, [z1], [conv_ln_g, conv_ln_b], [dz3], [F32], 256)
    dcv, dcg, conv_small = _conv_bwd("conv_b", dz1, proj, b_in, convw)
    duv, dbuv, dlng, dlnb, dws, dbs = _sgu_bwd("sgu_b", proj, b_in, sgu_ln_g, sgu_ln_b, wm, wmt, bst, dya, 256)
    dproj = jnp.concatenate([duv, dcv, dcg, dgab], axis=1)
    dh2 = _matmul("d_in", dproj, W["win"], "dgrad", True, F32)
    G["win"] = _matmul("w_in", h2, dproj, "wgrad", True, BF16)
    pair_mx, token = _rs_pair_start("mx", [G[k] for k in groups["mx"]])
    share_f2, token2 = _rs_land("f2", rs_f2, [dh2, token], qc_idx)

    dev_idx = jnp.reshape(dev, (1,)).astype(jnp.int32)
    early = _pack("pack_early", HALO + CHUNK, [("slice", conv_small, 0, HALO), ("heads", dws)])
    ag_early, token3 = _allgather_start("ag_early", _place("place_early", early, dev_idx), token2)

    (dx0, df1), (dg1, dnw2, dsc2, dsh2) = _rowwise_bwd(
        "res1_b", _f_res_pre(0.5), [x0, f1], [g1, norm_mix, sc2, sh2],
        [dx1, dh2], [F32, BF16], 256, after=[token, token3])
    rs_mx, token = _rs_chips_start("mx", pair_mx, df1, c_idx)
    dh1 = ffn_bwd("1", df1, s1, a1, b1, h1, W["g1"], W["u1"], W["d1"], after=token)
    pair_f1, token = _rs_pair_start("f1", [G[k] for k in groups["f1"]])
    reduced.update(zip(groups["f2"], _rs_done("f2", share_f2, dh1)))
    share_mx, token2 = _rs_land("mx", rs_mx, [dh1, token], qc_idx)
    (grad_x,), (dnw1, dsc1, dsh1) = _rowwise_bwd(
        "pre1_b", _f_pre_keep, [x0], [norm_ffn1, sc1, sh1], [dx0, dh1], [F32], 256, after=[token, token2])
    reduced.update(zip(groups["mx"], _rs_done("mx", share_mx, grad_x)))
    tot_early = _sum8("sum_early", _allgather_done("ag_early", ag_early, grad_x), HALO)

    dbs_row = jnp.transpose(dbs[:, ::CHUNK]).reshape(1, D)
    small = _pack("pack_small", HALO, [
        ("rows", dnw1), ("rows", dnw2), ("wide", dbuv), ("slice", conv_small, 33, 34), ("slice", conv_small, 34, 35),
        ("wide", dbgab), ("rows", dlng), ("rows", dlnb), ("rows", dbs_row), ("slice", conv_small, 32, 33),
        ("rows", dclg), ("rows", dclb), ("rows", dnw3), ("rows", dnwf),
        ("rows", dsh1), ("rows", dsc1), ("rows", dg1), ("rows", dsh2), ("rows", dsc2), ("rows", dg2),
        ("rows", dsh3), ("rows", dsc3), ("rows", dg3), ("lanes", loss_blk), ("zeros", 6)])
    every = _allgather8("ag_small", small)
    rs_f1, token = _rs_chips_start("f1", pair_f1, every, c_idx)
    every = every.reshape(8, HALO, D)
    tot = _sum8("sum_small", every, HALO, after=token)
    loss = tot[25, 0]
    dmod_all = every[:, 16:16 + N_MOD, :].reshape(8, N_MOD * D)
    dmod_mine = lax.dynamic_slice(dmod_all, (0, q * n_ada), (8, n_ada))
    g_ada_w = _ada_wgrad("ada_wgrad", jnp.transpose(c_all), dmod_mine)

    given = dict(ada_w=(ada_w, m_ada_w, v_ada_w), ada_b=(ada_b, m_ada_b, v_ada_b),
                 norm_ffn1=(norm_ffn1, m_norm_ffn1, v_norm_ffn1),
                 ffn1_w_gate=(ffn1_w_gate, m_ffn1_w_gate, v_ffn1_w_gate), ffn1_w_up=(ffn1_w_up, m_ffn1_w_up, v_ffn1_w_up),
                 ffn1_w_down=(ffn1_w_down, m_ffn1_w_down, v_ffn1_w_down), norm_mix=(norm_mix, m_norm_mix, v_norm_mix),
                 mix_w_in=(mix_w_in, m_mix_w_in, v_mix_w_in), mix_b_in=(mix_b_in, m_mix_b_in, v_mix_b_in),
                 sgu_ln_g=(sgu_ln_g, m_sgu_ln_g, v_sgu_ln_g), sgu_ln_b=(sgu_ln_b, m_sgu_ln_b, v_sgu_ln_b),
                 sgu_w_s=(sgu_w_s, m_sgu_w_s, v_sgu_w_s), sgu_b_s=(sgu_b_s, m_sgu_b_s, v_sgu_b_s),
                 conv_w=(conv_w, m_conv_w, v_conv_w), conv_b=(conv_b, m_conv_b, v_conv_b),
                 conv_ln_g=(conv_ln_g, m_conv_ln_g, v_conv_ln_g), conv_ln_b=(conv_ln_b, m_conv_ln_b, v_conv_ln_b),
                 w_branch_a=(w_branch_a, m_w_branch_a, v_w_branch_a), w_branch_b=(w_branch_b, m_w_branch_b, v_w_branch_b),
                 w_out=(w_out, m_w_out, v_w_out), norm_ffn2=(norm_ffn2, m_norm_ffn2, v_norm_ffn2),
                 ffn2_w_gate=(ffn2_w_gate, m_ffn2_w_gate, v_ffn2_w_gate), ffn2_w_up=(ffn2_w_up, m_ffn2_w_up, v_ffn2_w_up),
                 ffn2_w_down=(ffn2_w_down, m_ffn2_w_down, v_ffn2_w_down), norm_final=(norm_final, m_norm_final, v_norm_final))
    out = {}

    def big_update(name, g2d):
        w, m, v = given[name]
        shp = w.shape
        res = _adamw(f"adamw_{name}", w.reshape(g2d.shape), g2d, m.reshape(g2d.shape), v.reshape(g2d.shape),
                     _tile(g2d.shape[0], 512 if g2d.shape[1] <= D else 256))
        out[name] = tuple(t.reshape(shp) for t in res)

    big_names = dict(g1="ffn1_w_gate", u1="ffn1_w_up", d1="ffn1_w_down", win="mix_w_in", wa="w_branch_a",
                     wb="w_branch_b", wo="w_out", g2="ffn2_w_gate", u2="ffn2_w_up", d2="ffn2_w_down")
    for key in groups["f2"] + groups["mx"]:
        big_update(big_names[key], reduced[key])
    big_update("ada_w", g_ada_w)

    small_rows = [("norm_ffn1", 0, 1), ("norm_mix", 1, 1), ("mix_b_in", 2, 6), ("sgu_ln_g", 8, 1), ("sgu_ln_b", 9, 1),
                  ("sgu_b_s", 10, 1), ("conv_b", 11, 1), ("conv_ln_g", 12, 1), ("conv_ln_b", 13, 1),
                  ("norm_ffn2", 14, 1), ("norm_final", 15, 1), ("ada_b", 16, N_MOD)]

    def pack_rows(which, label):
        pieces = []
        for nm, _, r in small_rows:
            t = given[nm][which]
            pieces.append(("wide", t) if r > 1 else ("rows", t.reshape(1, D)))
        return _pack(f"pack_{label}", HALO, pieces + [("zeros", HALO - 16 - N_MOD)])

    small_res = _adamw("adamw_small", pack_rows(0, "w"), tot, pack_rows(1, "m"), pack_rows(2, "v"), HALO)
    for nm, r0, r in small_rows:
        shp = given[nm][0].shape
        out[nm] = tuple(t[r0:r0 + r].reshape(shp) for t in small_res)
    n_cw = conv_w.shape[2]
    g_cw = lax.dynamic_slice(tot_early, (0, q * n_cw), (HALO, n_cw))
    pad_cw = lambda t: jnp.pad(t[0], ((0, 1), (0, 0)))
    cw_res = _adamw("adamw_conv_w", pad_cw(conv_w), g_cw, pad_cw(m_conv_w), pad_cw(v_conv_w), HALO)
    out["conv_w"] = tuple(t[:KCONV][None] for t in cw_res)
    g_ws = jnp.transpose(tot_early[HALO:].reshape(CHUNK, HEADS, CHUNK), (1, 0, 2)).reshape(HEADS * CHUNK, CHUNK)
    flat_ws = lambda t: t.reshape(HEADS * CHUNK, CHUNK)
    ws_res = _adamw("adamw_sgu_w_s", flat_ws(sgu_w_s), g_ws, flat_ws(m_sgu_w_s), flat_ws(v_sgu_w_s), 512)
    out["sgu_w_s"] = tuple(t.reshape(sgu_w_s.shape) for t in ws_res)

    done = [out[n][1] for n in ("ada_w", "sgu_w_s", "conv_w", "ada_b")] + [out[big_names[k]][1] for k in groups["f2"] + groups["mx"]]
    reduced.update(zip(groups["f1"], _rs_finish("f1", rs_f1, done, qc_idx)))
    for key in groups["f1"]:
        big_update(big_names[key], reduced[key])

    order = ["ada_w", "ada_b", "norm_ffn1", "ffn1_w_gate", "ffn1_w_up", "ffn1_w_down", "norm_mix", "mix_w_in", "mix_b_in",
             "sgu_ln_g", "sgu_ln_b", "sgu_w_s", "sgu_b_s", "conv_w", "conv_b", "conv_ln_g", "conv_ln_b", "w_branch_a",
             "w_branch_b", "w_out", "norm_ffn2", "ffn2_w_gate", "ffn2_w_up", "ffn2_w_down", "norm_final"]
    return (loss, grad_x[None], *[out[n][0] for n in order], *[out[n][1] for n in order],
            *[out[n][2] for n in order], *[out[n][3] for n in order])
```

```python
import functools

import jax
import jax.numpy as jnp
from jax import lax
from jax.experimental import pallas as pl
from jax.experimental.pallas import tpu as pltpu

F32 = jnp.float32
BF16 = jnp.bfloat16
D = 1024
HEADS = 8
CHUNK = 128
KCONV = 31
HALO = 32
N_MOD = 9
EPS = 1e-6
NCHIP = 4
VMEM_LIMIT = 56 * 1024 * 1024
MESH = pl.DeviceIdType.MESH

ADAM_LR = 0.001
ADAM_B1 = 0.9
ADAM_B2 = 0.999
ADAM_EPS = 1e-08
ADAM_WD = 0.01
ADAM_STEP = 10

NN = (((1,), (0,)), ((), ()))
NT = (((1,), (1,)), ((), ()))
TN = (((0,), (0,)), ((), ()))


def _cparams(*sem):
    return pltpu.CompilerParams(dimension_semantics=sem or None, vmem_limit_bytes=VMEM_LIMIT)


def _rms(x, w):
    return x * lax.rsqrt(jnp.mean(x * x, axis=-1, keepdims=True) + EPS) * w


def _ln(x, g, b):
    mu = jnp.mean(x, axis=-1, keepdims=True)
    xc = x - mu
    var = jnp.mean(xc * xc, axis=-1, keepdims=True)
    return xc * lax.rsqrt(var + EPS) * g + b


def _silu(x):
    return x * jax.nn.sigmoid(x)


def _gelu(x):
    return x * (0.5 * (1.0 + jnp.tanh(0.7978845608028654 * (x + 0.044715 * (x * x * x)))))


def _f_pre(x, nw, sc, sh):
    return _rms(x, nw) * (1.0 + sc) + sh


def _f_pre_keep(x, nw, sc, sh):
    return x, _f_pre(x, nw, sc, sh)


def _f_res_pre(scale):
    def fn(x, f, g, nw, sc, sh):
        xn = x + (scale * g) * f
        return xn, _f_pre(xn, nw, sc, sh)
    return fn


def _f_glu(a, b):
    return _silu(a) * b


def _f_merge(gab, ya, yb, bgab):
    g = gab + bgab
    return jax.nn.sigmoid(g[:, :D]) * ya + jax.nn.sigmoid(g[:, D:]) * yb


def _f_lnsilu(z, g, b):
    return _silu(_ln(z, g, b))


def _f_mixa(uv, buv, g, b):
    t = uv + buv
    return _gelu(t[:, :D]), _ln(_gelu(t[:, D:]), g, b)


def _col(t):
    return t if isinstance(t, tuple) else (t, t.shape[-1], 0)


def _afters(after):
    if after is None:
        return []
    return list(after) if isinstance(after, (list, tuple)) else [after]


def _rowwise_fwd(name, fn, rows, params, outs, tr, after=None):
    rows = [_col(r) for r in rows]
    params = [_col(p) for p in params]
    extra = _afters(after)
    n_rows = rows[0][0].shape[0]
    nin = len(rows) + len(params)

    def body(*refs):
        res = fn(*[r[...].astype(F32) for r in refs[:nin]])
        res = res if isinstance(res, tuple) else (res,)
        for o, v in zip(refs[nin + len(extra):], res, strict=True):
            o[...] = v.astype(o.dtype)

    return pl.pallas_call(
        body, name=name, grid=(n_rows // tr,),
        in_specs=[pl.BlockSpec((tr, w), functools.partial(lambda cb, i: (i, cb), cb)) for _, w, cb in rows]
        + [pl.BlockSpec((1, w), functools.partial(lambda cb, i: (0, cb), cb)) for _, w, cb in params]
        + [pl.BlockSpec(memory_space=pl.ANY)] * len(extra),
        out_specs=[pl.BlockSpec((tr, w), lambda i: (i, 0)) for w, _ in outs],
        out_shape=[jax.ShapeDtypeStruct((n_rows, w), dt) for w, dt in outs],
        compiler_params=_cparams("parallel"),
    )(*[r[0] for r in rows], *[p[0] for p in params], *extra)


def _rowwise_bwd(name, fn, rows, params, cots, row_grads, tr, after=None):
    rows = [_col(r) for r in rows]
    params = [_col(p) for p in params]
    cots = [_col(t) for t in cots]
    extra = _afters(after)
    n_rows = rows[0][0].shape[0]
    nr, npar, nc = len(rows), len(params), len(cots)
    nin = nr + npar + nc
    n_rg = sum(dt is not None for dt in row_grads)

    def body(*refs):
        i = pl.program_id(0)
        prim = [r[...].astype(F32) for r in refs[:nr + npar]]
        ct = tuple(r[...].astype(F32) for r in refs[nr + npar:nin])
        _, vjp = jax.vjp(fn, *prim)
        g = vjp(ct if nc > 1 else ct[0])
        outs = refs[nin + len(extra):]
        oi = 0
        for j, dt in enumerate(row_grads):
            if dt is not None:
                outs[oi][...] = g[j].astype(dt)
                oi += 1
        for j in range(npar):
            acc = outs[n_rg + j]

            @pl.when(i == 0)
            def _(acc=acc):
                acc[...] = jnp.zeros(acc.shape, F32)

            acc[...] += g[nr + j]

    res = pl.pallas_call(
        body, name=name, grid=(n_rows // tr,),
        in_specs=[pl.BlockSpec((tr, w), functools.partial(lambda cb, i: (i, cb), cb)) for _, w, cb in rows]
        + [pl.BlockSpec((1, w), functools.partial(lambda cb, i: (0, cb), cb)) for _, w, cb in params]
        + [pl.BlockSpec((tr, w), functools.partial(lambda cb, i: (i, cb), cb)) for _, w, cb in cots]
        + [pl.BlockSpec(memory_space=pl.ANY)] * len(extra),
        out_specs=[pl.BlockSpec((tr, rows[j][1]), lambda i: (i, 0)) for j, dt in enumerate(row_grads) if dt is not None]
        + [pl.BlockSpec((1, w), lambda i: (0, 0)) for _, w, _ in params],
        out_shape=[jax.ShapeDtypeStruct((n_rows, rows[j][1]), dt) for j, dt in enumerate(row_grads) if dt is not None]
        + [jax.ShapeDtypeStruct((1, w), F32) for _, w, _ in params],
        compiler_params=_cparams("arbitrary"),
    )(*[r[0] for r in rows], *[p[0] for p in params], *[t[0] for t in cots], *extra)
    return res[:n_rg], res[n_rg:]


def _final(name, x, f, tgt, g, nw, tr):
    n_rows = x.shape[0]

    def body(x_ref, f_ref, t_ref, g_ref, nw_ref, dx_ref, df_ref, dg_ref, dnw_ref, loss_ref):
        i = pl.program_id(0)
        tg = t_ref[...]

        def fn(xv, fv, gv, nwv):
            e = _rms(xv + (0.5 * gv) * fv, nwv) - tg
            return 0.5 * jnp.mean(e * e, axis=-1, keepdims=True)

        per_row, vjp = jax.vjp(fn, x_ref[...], f_ref[...], g_ref[...], nw_ref[...])
        dx, df, dg, dnw = vjp(jnp.ones_like(per_row))
        dx_ref[...] = dx
        df_ref[...] = df.astype(df_ref.dtype)

        @pl.when(i == 0)
        def _():
            dg_ref[...] = jnp.zeros(dg_ref.shape, F32)
            dnw_ref[...] = jnp.zeros(dnw_ref.shape, F32)
            loss_ref[...] = jnp.zeros(loss_ref.shape, F32)

        dg_ref[...] += dg
        dnw_ref[...] += dnw
        loss_ref[...] += jnp.broadcast_to(jnp.sum(per_row, axis=0, keepdims=True), loss_ref.shape)

    row = pl.BlockSpec((tr, D), lambda i: (i, 0))
    par = pl.BlockSpec((1, D), lambda i: (0, 0))
    return pl.pallas_call(
        body, name=name, grid=(n_rows // tr,),
        in_specs=[row, row, row, par, par],
        out_specs=[row, row, par, par, pl.BlockSpec((8, 128), lambda i: (0, 0))],
        out_shape=[jax.ShapeDtypeStruct((n_rows, D), F32), jax.ShapeDtypeStruct((n_rows, D), BF16),
                   jax.ShapeDtypeStruct((1, D), F32), jax.ShapeDtypeStruct((1, D), F32),
                   jax.ShapeDtypeStruct((8, 128), F32)],
        compiler_params=_cparams("arbitrary"),
    )(x, f, tgt, g, nw)


def _tile(dim, pref):
    t = min(dim, pref)
    while dim % t:
        t //= 2
    return t


def _matmul(name, a, b, kind, col, out_dtype, add=None, after=None):
    if kind == "wgrad":
        m, kg = a.shape
        ng = b.shape[1]
        r, c = (kg, ng // NCHIP) if col else (kg // NCHIP, ng)
        tm, tn, tk = _tile(r, 1024), _tile(c, 1536), _tile(m, 2048)
        grid = (kg // tm, ng // tn, m // tk)
        a_spec = pl.BlockSpec((tk, tm), lambda i, j, k: (k, i))
        b_spec = pl.BlockSpec((tk, tn), lambda i, j, k: (k, j))
        if col:
            o_spec = pl.BlockSpec((None, tm, tn), lambda i, j, k: (j // (c // tn), i, j % (c // tn)))
        else:
            o_spec = pl.BlockSpec((None, tm, tn), lambda i, j, k: (i // (r // tm), i % (r // tm), j))
        out_shape = jax.ShapeDtypeStruct((NCHIP, r, c), out_dtype)
        dims = TN
    else:
        _, r, c = b.shape
        m = a.shape[0]
        kg, ng = (r, NCHIP * c) if col else (NCHIP * r, c)
        tm = _tile(m, 1024)
        if kind == "fwd":
            tn, tk = _tile(c, 1536), _tile(r, 2048)
            grid = (m // tm, ng // tn, kg // tk)
            a_spec = pl.BlockSpec((tm, tk), lambda i, j, k: (i, k))
            if col:
                b_spec = pl.BlockSpec((None, tk, tn), lambda i, j, k: (j // (c // tn), k, j % (c // tn)))
            else:
                b_spec = pl.BlockSpec((None, tk, tn), lambda i, j, k: (k // (r // tk), k % (r // tk), j))
            out_shape = jax.ShapeDtypeStruct((m, ng), out_dtype)
            dims = NN
        else:
            tn, tk = _tile(r, 1024), _tile(c, 2048)
            grid = (m // tm, kg // tn, ng // tk)
            a_spec = pl.BlockSpec((tm, tk), lambda i, j, k: (i, k))
            if col:
                b_spec = pl.BlockSpec((None, tn, tk), lambda i, j, k: (k // (c // tk), j, k % (c // tk)))
            else:
                b_spec = pl.BlockSpec((None, tn, tk), lambda i, j, k: (j // (r // tn), j % (r // tn), k))
            out_shape = jax.ShapeDtypeStruct((m, kg), out_dtype)
            dims = NT
        o_spec = pl.BlockSpec((tm, tn), lambda i, j, k: (i, j))
    nk = grid[2]
    has_add = add is not None
    extra = [] if after is None else [after]

    def body(*refs):
        a_ref, b_ref = refs[0], refs[1]
        o_ref = refs[2 + has_add + len(extra)]
        k = pl.program_id(2)
        p = lax.dot_general(a_ref[...].astype(BF16), b_ref[...].astype(BF16), dims, preferred_element_type=F32)

        def finish(r_):
            if has_add:
                r_ = r_ + refs[2][...]
            o_ref[...] = r_.astype(o_ref.dtype)

        if nk == 1:
            finish(p)
        else:
            acc_ref = refs[-1]

            @pl.when(k == 0)
            def _():
                acc_ref[...] = p

            @pl.when(jnp.logical_and(k > 0, k < nk - 1))
            def _():
                acc_ref[...] += p

            @pl.when(k == nk - 1)
            def _():
                finish(acc_ref[...] + p)

    in_specs = [a_spec, b_spec] + ([pl.BlockSpec((tm, tn), lambda i, j, k: (i, j))] if has_add else [])
    in_specs += [pl.BlockSpec(memory_space=pl.ANY)] * len(extra)
    args = (a, b) + ((add,) if has_add else ()) + tuple(extra)
    return pl.pallas_call(
        body, name=name, grid=grid, in_specs=in_specs, out_specs=o_spec, out_shape=out_shape,
        scratch_shapes=[pltpu.VMEM((tm, tn), F32)] if nk > 1 else [],
        compiler_params=_cparams("parallel", "parallel", "arbitrary"),
    )(*args)


def _ffn_up(name, h, wg, wu):
    m, k = h.shape
    _, _, c = wg.shape
    n = NCHIP * c
    tm, tn = _tile(m, 1024), _tile(c, 1024)

    def body(h_ref, wg_ref, wu_ref, a_ref, b_ref, s_ref):
        hv = h_ref[...]
        a = jnp.dot(hv, wg_ref[...], preferred_element_type=F32)
        b = jnp.dot(hv, wu_ref[...], preferred_element_type=F32)
        a_ref[...] = a
        b_ref[...] = b
        s_ref[...] = (_silu(a) * b).astype(s_ref.dtype)

    w_spec = pl.BlockSpec((None, k, tn), lambda i, j: (j // (c // tn), 0, j % (c // tn)))
    o_spec = pl.BlockSpec((tm, tn), lambda i, j: (i, j))
    return pl.pallas_call(
        body, name=name, grid=(m // tm, n // tn),
        in_specs=[pl.BlockSpec((tm, k), lambda i, j: (i, 0)), w_spec, w_spec],
        out_specs=[o_spec, o_spec, o_spec],
        out_shape=[jax.ShapeDtypeStruct((m, n), F32), jax.ShapeDtypeStruct((m, n), F32),
                   jax.ShapeDtypeStruct((m, n), BF16)],
        compiler_params=_cparams("parallel", "parallel"),
    )(h, wg, wu)


def _dgrad_glu(name, df, wd, a, b, after=None):
    m = df.shape[0]
    _, r, c = wd.shape
    tm, tn = _tile(m, 512), _tile(r, 1024)
    extra = [] if after is None else [after]

    def body(df_ref, w_ref, a_ref, b_ref, *rest):
        da_ref, db_ref = rest[-2], rest[-1]
        ds = lax.dot_general(df_ref[...], w_ref[...], NT, preferred_element_type=F32)
        _, vjp = jax.vjp(_f_glu, a_ref[...], b_ref[...])
        da, db = vjp(ds)
        da_ref[...] = da.astype(da_ref.dtype)
        db_ref[...] = db.astype(db_ref.dtype)

    t_spec = pl.BlockSpec((tm, tn), lambda i, j: (i, j))
    sds = jax.ShapeDtypeStruct((m, NCHIP * r), BF16)
    return pl.pallas_call(
        body, name=name, grid=(m // tm, NCHIP * r // tn),
        in_specs=[pl.BlockSpec((tm, c), lambda i, j: (i, 0)),
                  pl.BlockSpec((None, tn, c), lambda i, j: (j // (r // tn), j % (r // tn), 0)), t_spec, t_spec]
        + [pl.BlockSpec(memory_space=pl.ANY)] * len(extra),
        out_specs=[t_spec, t_spec], out_shape=[sds, sds],
        compiler_params=_cparams("parallel", "parallel"),
    )(df, wd, a, b, *extra)


def _sgu_fwd(name, proj, b_in, lng, lnb, wm, bst, tr):
    n_rows = proj.shape[0]

    def body(uv_ref, buv_ref, g_ref, b_ref, wm_ref, bst_ref, ya_ref):
        u, vln = _f_mixa(uv_ref[...], buv_ref[...], g_ref[...], b_ref[...])
        vb = vln.astype(BF16)
        for ck in range(tr // CHUNK):
            rs = slice(ck * CHUNK, (ck + 1) * CHUNK)
            for h in range(HEADS):
                cs = slice(h * CHUNK, (h + 1) * CHUNK)
                vm = jnp.dot(wm_ref[h], vb[rs, cs], preferred_element_type=F32) + bst_ref[:, cs]
                ya_ref[rs, cs] = (u[rs, cs] * vm).astype(ya_ref.dtype)

    return pl.pallas_call(
        body, name=name, grid=(n_rows // tr,),
        in_specs=[pl.BlockSpec((tr, 2 * D), lambda i: (i, 0)), pl.BlockSpec((1, 2 * D), lambda i: (0, 0)),
                  pl.BlockSpec((1, D), lambda i: (0, 0)), pl.BlockSpec((1, D), lambda i: (0, 0)),
                  pl.BlockSpec((HEADS, CHUNK, CHUNK), lambda i: (0, 0, 0)),
                  pl.BlockSpec((CHUNK, D), lambda i: (0, 0))],
        out_specs=pl.BlockSpec((tr, D), lambda i: (i, 0)),
        out_shape=jax.ShapeDtypeStruct((n_rows, D), BF16),
        compiler_params=_cparams("parallel"),
    )(proj, b_in, lng, lnb, wm, bst)


def _sgu_bwd(name, proj, b_in, lng, lnb, wm, wmt, bst, dya, tr):
    n_rows = proj.shape[0]

    def body(uv_ref, buv_ref, g_ref, b_ref, wm_ref, wmt_ref, bst_ref, dya_ref,
             duv_ref, dbuv_ref, dg_ref, db_ref, dws_ref, dbs_ref, du_s, dvln_s):
        i = pl.program_id(0)

        @pl.when(i == 0)
        def _():
            dbuv_ref[...] = jnp.zeros(dbuv_ref.shape, F32)
            dg_ref[...] = jnp.zeros(dg_ref.shape, F32)
            db_ref[...] = jnp.zeros(db_ref.shape, F32)
            dws_ref[...] = jnp.zeros(dws_ref.shape, F32)
            dbs_ref[...] = jnp.zeros(dbs_ref.shape, F32)

        (u, vln), vjp = jax.vjp(_f_mixa, uv_ref[...], buv_ref[...], g_ref[...], b_ref[...])
        vb = vln.astype(BF16)
        dya_v = dya_ref[...].astype(F32)
        tpos = lax.broadcasted_iota(jnp.int32, (CHUNK, CHUNK), 0)
        spos = lax.broadcasted_iota(jnp.int32, (CHUNK, CHUNK), 1)
        causal = (tpos >= spos).astype(F32)
        for ck in range(tr // CHUNK):
            rs = slice(ck * CHUNK, (ck + 1) * CHUNK)
            for h in range(HEADS):
                cs = slice(h * CHUNK, (h + 1) * CHUNK)
                vbh = vb[rs, cs]
                vm = jnp.dot(wm_ref[h], vbh, preferred_element_type=F32) + bst_ref[:, cs]
                dyb = dya_v[rs, cs]
                du_s[rs, cs] = dyb * vm
                dvm = dyb * u[rs, cs]
                dvmb = dvm.astype(BF16)
                dvln_s[rs, cs] = jnp.dot(wmt_ref[h], dvmb, preferred_element_type=F32)
                dws_ref[h] += causal * lax.dot_general(dvmb, vbh, NT, preferred_element_type=F32)
                dbs_ref[:, cs] += jnp.broadcast_to(jnp.sum(dvm, axis=1, keepdims=True), (CHUNK, CHUNK))
        duv, dbuv, dg, db = vjp((du_s[...], dvln_s[...]))
        duv_ref[...] = duv.astype(duv_ref.dtype)
        dbuv_ref[...] += dbuv
        dg_ref[...] += dg
        db_ref[...] += db

    par = pl.BlockSpec((1, D), lambda i: (0, 0))
    par2 = pl.BlockSpec((1, 2 * D), lambda i: (0, 0))
    w_spec = pl.BlockSpec((HEADS, CHUNK, CHUNK), lambda i: (0, 0, 0))
    b_spec = pl.BlockSpec((CHUNK, D), lambda i: (0, 0))
    return pl.pallas_call(
        body, name=name, grid=(n_rows // tr,),
        in_specs=[pl.BlockSpec((tr, 2 * D), lambda i: (i, 0)), par2, par, par, w_spec, w_spec, b_spec,
                  pl.BlockSpec((tr, D), lambda i: (i, 0))],
        out_specs=[pl.BlockSpec((tr, 2 * D), lambda i: (i, 0)), par2, par, par, w_spec, b_spec],
        out_shape=[jax.ShapeDtypeStruct((n_rows, 2 * D), BF16), jax.ShapeDtypeStruct((1, 2 * D), F32),
                   jax.ShapeDtypeStruct((1, D), F32), jax.ShapeDtypeStruct((1, D), F32),
                   jax.ShapeDtypeStruct((HEADS, CHUNK, CHUNK), F32), jax.ShapeDtypeStruct((CHUNK, D), F32)],
        scratch_shapes=[pltpu.VMEM((tr, D), F32), pltpu.VMEM((tr, D), F32)],
        compiler_params=_cparams("arbitrary"),
    )(proj, b_in, lng, lnb, wm, wmt, bst, dya)


CT = 256
RB = 128
CV0 = 2 * D // CT
CG0 = 3 * D // CT


def _shift_bank(win):
    return [win] + [pltpu.roll(win, RB + HALO - r, 0) for r in range(1, 8)]


def _shifted(bank, sh):
    lo = 8 * (sh // 8)
    return bank[sh % 8][lo:lo + RB]


def _conv_fwd(name, proj, b_in, w, cb):
    n_rows = proj.shape[0]

    def body(cv_ref, cg_ref, bcv_ref, bcg_ref, w_ref, cb_ref, z1_ref, zp_ref):
        zp_ref[0:HALO, :] = jnp.zeros((HALO, CT), F32)
        zp_ref[HALO:, :] = (cv_ref[...] + bcv_ref[...]) * jax.nn.sigmoid(cg_ref[...] + bcg_ref[...])

        def blk(rb, carry):
            base = pl.multiple_of(rb * RB, RB)
            bank = _shift_bank(zp_ref[pl.ds(base, RB + HALO), :])
            acc = jnp.broadcast_to(cb_ref[...], (RB, CT))
            for k in range(KCONV):
                acc = acc + w_ref[k:k + 1, :] * _shifted(bank, k + 2)
            z1_ref[pl.ds(base, RB), :] = acc
            return carry

        lax.fori_loop(0, n_rows // RB, blk, 0)

    return pl.pallas_call(
        body, name=name, grid=(D // CT,),
        in_specs=[pl.BlockSpec((n_rows, CT), lambda j: (0, CV0 + j)), pl.BlockSpec((n_rows, CT), lambda j: (0, CG0 + j)),
                  pl.BlockSpec((1, CT), lambda j: (0, CV0 + j)), pl.BlockSpec((1, CT), lambda j: (0, CG0 + j)),
                  pl.BlockSpec((HALO, CT), lambda j: (0, j)), pl.BlockSpec((1, CT), lambda j: (0, j))],
        out_specs=pl.BlockSpec((n_rows, CT), lambda j: (0, j)),
        out_shape=jax.ShapeDtypeStruct((n_rows, D), F32),
        scratch_shapes=[pltpu.VMEM((n_rows + HALO, CT), F32)],
        compiler_params=_cparams("parallel"),
    )(proj, proj, b_in, b_in, w, cb)


def _conv_bwd(name, dz1, proj, b_in, w):
    n_rows = proj.shape[0]

    def body(dz_ref, cv_ref, cg_ref, bcv_ref, bcg_ref, w_ref, dcv_ref, dcg_ref, sm_ref, zp_ref, dzp_ref):
        cvb = cv_ref[...] + bcv_ref[...]
        sg = jax.nn.sigmoid(cg_ref[...] + bcg_ref[...])
        zp_ref[0:HALO, :] = jnp.zeros((HALO, CT), F32)
        zp_ref[HALO:, :] = cvb * sg
        dz = dz_ref[...]
        dzp_ref[0:n_rows, :] = dz
        dzp_ref[n_rows:, :] = jnp.zeros((HALO, CT), F32)
        sm_ref[...] = jnp.zeros(sm_ref.shape, F32)
        sm_ref[32:33, :] = jnp.sum(dz, axis=0, keepdims=True)

        def blk(rb, carry):
            base = pl.multiple_of(rb * RB, RB)
            dbank = _shift_bank(dzp_ref[pl.ds(base, RB + HALO), :])
            zbank = _shift_bank(zp_ref[pl.ds(base, RB + HALO), :])
            dzb = dbank[0][0:RB]
            acc = jnp.zeros((RB, CT), F32)
            for k in range(KCONV):
                acc = acc + w_ref[k:k + 1, :] * _shifted(dbank, KCONV - 1 - k)
                sm_ref[k:k + 1, :] += jnp.sum(dzb * _shifted(zbank, k + 2), axis=0, keepdims=True)
            dzp_ref[pl.ds(base, RB), :] = acc
            return carry

        lax.fori_loop(0, n_rows // RB, blk, 0)
        dz0 = dzp_ref[0:n_rows, :]
        dcv = dz0 * sg
        dcg = dz0 * cvb * (sg * (1.0 - sg))
        dcv_ref[...] = dcv.astype(dcv_ref.dtype)
        dcg_ref[...] = dcg.astype(dcg_ref.dtype)
        sm_ref[33:34, :] = jnp.sum(dcv, axis=0, keepdims=True)
        sm_ref[34:35, :] = jnp.sum(dcg, axis=0, keepdims=True)

    col = pl.BlockSpec((n_rows, CT), lambda j: (0, j))
    return pl.pallas_call(
        body, name=name, grid=(D // CT,),
        in_specs=[col, pl.BlockSpec((n_rows, CT), lambda j: (0, CV0 + j)), pl.BlockSpec((n_rows, CT), lambda j: (0, CG0 + j)),
                  pl.BlockSpec((1, CT), lambda j: (0, CV0 + j)), pl.BlockSpec((1, CT), lambda j: (0, CG0 + j)),
                  pl.BlockSpec((HALO, CT), lambda j: (0, j))],
        out_specs=[col, col, pl.BlockSpec((40, CT), lambda j: (0, j))],
        out_shape=[jax.ShapeDtypeStruct((n_rows, D), BF16), jax.ShapeDtypeStruct((n_rows, D), BF16),
                   jax.ShapeDtypeStruct((40, D), F32)],
        scratch_shapes=[pltpu.VMEM((n_rows + HALO, CT), F32), pltpu.VMEM((n_rows + HALO, CT), F32)],
        compiler_params=_cparams("parallel"),
    )(dz1, proj, proj, b_in, b_in, w)


ADA_TN = 768


def _split_bf16(v):
    hi = v.astype(BF16)
    return hi, (v - hi.astype(F32)).astype(BF16)


def _ada_fwd(name, c_all, w, b):
    n = w.shape[1]

    def body(c_ref, w_ref, b_ref, o_ref):
        ch, cl = _split_bf16(_silu(c_ref[...]))
        wh, wl = _split_bf16(w_ref[...])
        acc = jnp.dot(ch, wl, preferred_element_type=F32) + jnp.dot(cl, wh, preferred_element_type=F32)
        o_ref[...] = acc + jnp.dot(ch, wh, preferred_element_type=F32) + b_ref[...]

    return pl.pallas_call(
        body, name=name, grid=(n // ADA_TN,),
        in_specs=[pl.BlockSpec((8, D), lambda j: (0, 0)), pl.BlockSpec((D, ADA_TN), lambda j: (0, j)),
                  pl.BlockSpec((1, ADA_TN), lambda j: (0, j))],
        out_specs=pl.BlockSpec((8, ADA_TN), lambda j: (0, j)),
        out_shape=jax.ShapeDtypeStruct((8, n), F32),
        compiler_params=_cparams("parallel"),
    )(c_all, w, b)


def _ada_wgrad(name, c_all_t, dmod):
    n = dmod.shape[1]

    def body(ct_ref, dm_ref, o_ref):
        ca = _silu(ct_ref[...])
        acc = ca[:, 0:1] * dm_ref[0:1, :]
        for r in range(1, 8):
            acc = acc + ca[:, r:r + 1] * dm_ref[r:r + 1, :]
        o_ref[...] = acc

    return pl.pallas_call(
        body, name=name, grid=(n // ADA_TN,),
        in_specs=[pl.BlockSpec((D, 8), lambda j: (0, 0)), pl.BlockSpec((8, ADA_TN), lambda j: (0, j))],
        out_specs=pl.BlockSpec((D, ADA_TN), lambda j: (0, j)),
        out_shape=jax.ShapeDtypeStruct((D, n), F32),
        compiler_params=_cparams("parallel"),
    )(c_all_t, dmod)


def _adamw(name, w, g, m, v, tr):
    rows, cols = w.shape

    def body(w_ref, g_ref, m_ref, v_ref, g_out, d_ref, nm_ref, nv_ref):
        gv = g_ref[...]
        nm = ADAM_B1 * m_ref[...] + (1.0 - ADAM_B1) * gv
        nv = ADAM_B2 * v_ref[...] + (1.0 - ADAM_B2) * (gv * gv)
        m_hat = nm / (1.0 - ADAM_B1 ** ADAM_STEP)
        v_hat = nv / (1.0 - ADAM_B2 ** ADAM_STEP)
        d_ref[...] = -ADAM_LR * (m_hat / (jnp.sqrt(v_hat) + ADAM_EPS) + ADAM_WD * w_ref[...])
        nm_ref[...] = nm
        nv_ref[...] = nv
        g_out[...] = gv

    spec = pl.BlockSpec((tr, cols), lambda i: (i, 0))
    sds = jax.ShapeDtypeStruct((rows, cols), F32)
    return pl.pallas_call(
        body, name=name, grid=(rows // tr,), in_specs=[spec] * 4, out_specs=[spec] * 4, out_shape=[sds] * 4,
        compiler_params=_cparams("parallel"),
    )(w, g, m, v)


def _sum8(name, stacked, tr, after=None):
    n, rows, cols = stacked.shape
    extra = _afters(after)

    def body(s_ref, *rest):
        o_ref = rest[-1]
        acc = s_ref[0]
        for r in range(1, n):
            acc = acc + s_ref[r]
        o_ref[...] = acc

    return pl.pallas_call(
        body, name=name, grid=(rows // tr,),
        in_specs=[pl.BlockSpec((n, tr, cols), lambda i: (0, i, 0))] + [pl.BlockSpec(memory_space=pl.ANY)] * len(extra),
        out_specs=pl.BlockSpec((tr, cols), lambda i: (i, 0)),
        out_shape=jax.ShapeDtypeStruct((rows, cols), F32),
        compiler_params=_cparams("parallel"),
    )(stacked, *extra)


def _pair_add(name, g5, other, c_idx, tr):
    nq, _, rows, cols = g5.shape

    def body(c_ref, g_ref, o_ref, p_ref):
        p_ref[...] = (g_ref[...].astype(F32) + o_ref[...].astype(F32)).astype(p_ref.dtype)

    return pl.pallas_call(
        body, name=name,
        grid_spec=pltpu.PrefetchScalarGridSpec(
            num_scalar_prefetch=1, grid=(nq, rows // tr),
            in_specs=[pl.BlockSpec((None, None, tr, cols), lambda qi, i, cr: (qi, cr[0], i, 0)),
                      pl.BlockSpec((None, tr, cols), lambda qi, i, cr: (qi, i, 0))],
            out_specs=pl.BlockSpec((None, tr, cols), lambda qi, i, cr: (qi, i, 0))),
        out_shape=jax.ShapeDtypeStruct((nq, rows, cols), BF16),
        compiler_params=_cparams("parallel", "parallel"),
    )(c_idx, g5, other)


def _chip_add(name, p, recv, qc_idx, tr):
    _, rows, cols = p.shape

    def body(qc_ref, p_ref, r_ref, o_ref):
        acc = p_ref[...].astype(F32)
        for k in range(NCHIP - 1):
            acc = acc + r_ref[k].astype(F32)
        o_ref[...] = acc

    return pl.pallas_call(
        body, name=name,
        grid_spec=pltpu.PrefetchScalarGridSpec(
            num_scalar_prefetch=1, grid=(rows // tr,),
            in_specs=[pl.BlockSpec((None, tr, cols), lambda i, qc: (qc[0], i, 0)),
                      pl.BlockSpec((NCHIP - 1, tr, cols), lambda i, qc: (0, i, 0))],
            out_specs=pl.BlockSpec((None, tr, cols), lambda i, qc: (qc[1], i, 0))),
        out_shape=jax.ShapeDtypeStruct((2, rows, cols), F32),
        compiler_params=_cparams("parallel"),
    )(qc_idx, p, recv)


def _allgather8(name, blk):
    m_per, n = blk.shape

    def body(x_ref, out_ref, send_sems, recv_sems, local_sem):
        x, y, c = lax.axis_index("x"), lax.axis_index("y"), lax.axis_index("c")
        me, sibling = (x, y, c), (x, y, 1 - c)
        chips = [(1 - x, y), (x, 1 - y), (1 - x, 1 - y)]

        def rows(px, py, pc):
            return out_ref.at[pl.ds((4 * px + 2 * py + pc) * m_per, m_per), :]

        def copy(k, block, to, src=None):
            return pltpu.make_async_remote_copy(
                src_ref=rows(*block) if src is None else src, dst_ref=rows(*block),
                send_sem=send_sems.at[k], recv_sem=recv_sems.at[k], device_id=to, device_id_type=MESH)

        mine = pltpu.make_async_copy(x_ref, rows(*me), local_sem)
        mine.start()
        first = [copy(0, me, sibling, src=x_ref)]
        first += [copy(1 + j, me, (*chip, c), src=x_ref) for j, chip in enumerate(chips)]
        for cp in first:
            cp.start()
        passed = [copy(4 + j, (*chip, c), sibling) for j, chip in enumerate(chips)]
        for j, chip in enumerate(chips):
            copy(1 + j, (*chip, c), me).wait_recv()
            passed[j].start()
        copy(0, sibling, me).wait_recv()
        for j, chip in enumerate(chips):
            copy(4 + j, (*chip, 1 - c), me).wait_recv()
        for cp in first + passed:
            cp.wait_send()
        mine.wait()

    return pl.pallas_call(
        body, name=name,
        out_shape=jax.ShapeDtypeStruct((8 * m_per, n), blk.dtype),
        in_specs=[pl.BlockSpec(memory_space=pltpu.VMEM)],
        out_specs=pl.BlockSpec(memory_space=pltpu.VMEM),
        scratch_shapes=[pltpu.SemaphoreType.DMA((7,)), pltpu.SemaphoreType.DMA((7,)), pltpu.SemaphoreType.DMA],
        compiler_params=pltpu.CompilerParams(vmem_limit_bytes=VMEM_LIMIT),
    )(blk)


def _exchange(name, srcs, out_shapes, n_local, n_remote, plan, aliases=None, after=None):
    ni, no = len(srcs), len(out_shapes)
    extra = [] if after is None else [after]

    def body(*refs):
        ins, outs = refs[:ni], refs[ni + len(extra):ni + len(extra) + no]
        send_sems, recv_sems, local_sems = refs[ni + len(extra) + no:]
        x, y, c = lax.axis_index("x"), lax.axis_index("y"), lax.axis_index("c")
        local, remote = plan(ins, outs, x, y, c)
        assert len(local) == n_local and len(remote) == n_remote

        def rcopy(i, dst):
            s, _, peer, _, _ = remote[i]
            return pltpu.make_async_remote_copy(src_ref=s, dst_ref=dst, send_sem=send_sems.at[i],
                                                recv_sem=recv_sems.at[i], device_id=peer, device_id_type=MESH)

        lcs = [pltpu.make_async_copy(s, d, local_sems.at[i]) for i, (s, d) in enumerate(local)]
        for cp in lcs:
            cp.start()
        first = [i for i in range(n_remote) if remote[i][4] is None]
        passed = [i for i in range(n_remote) if remote[i][4] is not None]
        for i in first:
            rcopy(i, remote[i][1]).start()
        arrived = set()
        for i in passed:
            j = remote[i][4]
            rcopy(j, remote[j][3]).wait_recv()
            arrived.add(j)
            rcopy(i, remote[i][1]).start()
        for i in range(n_remote):
            if i not in arrived:
                rcopy(i, remote[i][3]).wait_recv()
        for i in range(n_remote):
            rcopy(i, remote[i][1]).wait_send()
        for cp in lcs:
            cp.wait()

    any_spec = pl.BlockSpec(memory_space=pl.ANY)
    return pl.pallas_call(
        body, name=name, out_shape=out_shapes,
        in_specs=[any_spec] * (ni + len(extra)), out_specs=[any_spec] * no,
        input_output_aliases=aliases or {},
        scratch_shapes=[pltpu.SemaphoreType.DMA((n_remote,)), pltpu.SemaphoreType.DMA((n_remote,)),
                        pltpu.SemaphoreType.DMA((max(n_local, 1),))],
    )(*srcs, *extra)


_CHIP_FLIPS = ((0, 1), (1, 0), (1, 1))


def _flip(v, f):
    return 1 - v if f else v


EFFECT = pltpu.SideEffectType.DATAFLOW_SIDE_EFFECTING


def _split_start(name, bufs, n, plan, after=None):
    nb = len(bufs)
    extra = [] if after is None else [after]

    def body(*refs):
        ins = refs[:nb]
        send_sems, recv_sems = refs[nb + len(extra)], refs[nb + len(extra) + 1]
        token = refs[-1]
        x, y, c = lax.axis_index("x"), lax.axis_index("y"), lax.axis_index("c")
        copies = plan(ins, x, y, c)
        assert len(copies) == n
        for i, (s, d, peer, _) in enumerate(copies):
            pltpu.make_async_remote_copy(src_ref=s, dst_ref=d, send_sem=send_sems.at[i], recv_sem=recv_sems.at[i],
                                         device_id=peer, device_id_type=MESH).start()
        token[...] = jnp.zeros_like(token)

    hbm = pl.BlockSpec(memory_space=pltpu.HBM)
    sem = pl.BlockSpec(memory_space=pltpu.SEMAPHORE)
    res = pl.pallas_call(
        body, name=name,
        out_shape=(pltpu.SemaphoreType.DMA((n,)), pltpu.SemaphoreType.DMA((n,)),
                   *[pltpu.HBM(b.shape, b.dtype) for b in bufs], jax.ShapeDtypeStruct((8, 128), F32)),
        in_specs=[hbm] * nb + [pl.BlockSpec(memory_space=pl.ANY)] * len(extra),
        out_specs=(sem, sem, *[hbm] * nb, pl.BlockSpec(memory_space=pltpu.VMEM)),
        input_output_aliases={i: 2 + i for i in range(nb)},
        compiler_params=pltpu.CompilerParams(has_side_effects=EFFECT),
    )(*[pltpu.with_memory_space_constraint(b, pltpu.HBM) for b in bufs], *extra)
    return res[0], res[1], list(res[2:2 + nb]), res[-1]


def _split_wait(name, bufs, send_sems, recv_sems, after, n, plan):
    nb = len(bufs)

    def body(*refs):
        ins = refs[:nb]
        ssem, rsem = refs[nb], refs[nb + 1]
        x, y, c = lax.axis_index("x"), lax.axis_index("y"), lax.axis_index("c")
        copies = plan(ins, x, y, c)
        assert len(copies) == n
        for i, (s, _, peer, lands) in enumerate(copies):
            cp = pltpu.make_async_remote_copy(src_ref=s, dst_ref=lands, send_sem=ssem.at[i], recv_sem=rsem.at[i],
                                              device_id=peer, device_id_type=MESH)
            cp.wait_send()
            cp.wait_recv()

    hbm = pl.BlockSpec(memory_space=pltpu.HBM)
    sem = pl.BlockSpec(memory_space=pltpu.SEMAPHORE)
    afters = list(after) if isinstance(after, (list, tuple)) else [after]
    res = pl.pallas_call(
        body, name=name,
        out_shape=[pltpu.HBM(b.shape, b.dtype) for b in bufs],
        in_specs=[hbm] * nb + [sem, sem] + [pl.BlockSpec(memory_space=pl.ANY)] * len(afters),
        out_specs=[hbm] * nb,
        input_output_aliases={i: i for i in range(nb)},
        compiler_params=pltpu.CompilerParams(has_side_effects=EFFECT),
    )(*bufs, send_sems, recv_sems, *afters)
    return list(res)


def _chips_of(x, y):
    return [(_flip(x, fx), _flip(y, fy)) for fx, fy in _CHIP_FLIPS]


def _gather_start(tag, stacks, after):
    n = len(stacks)

    def plan(refs, x, y, c):
        q = 2 * x + y
        return [(refs[a].at[q, c], refs[a].at[q, c], (px, py, c), refs[a].at[2 * px + py, c])
                for a in range(n) for px, py in _chips_of(x, y)]

    send, recv, thru, token = _split_start(f"gather_start_{tag}", stacks, 3 * n, plan, after)
    return (send, recv, thru, plan), token


def _gather_finish(tag, state, after):
    send, recv, thru, plan = state
    n = len(thru)
    landed = _split_wait(f"gather_wait_{tag}", thru, send, recv, after, 3 * n, plan)

    def pass_on(ins, outs, x, y, c):
        sib = (x, y, 1 - c)
        remote = []
        for a in range(n):
            for px, py in _chips_of(x, y):
                p = 2 * px + py
                remote.append((ins[a].at[p, c], outs[a].at[p, c], sib, outs[a].at[p, 1 - c], None))
        return [], remote

    full = _exchange(f"gather_pass_{tag}", landed, [jax.ShapeDtypeStruct(s.shape, s.dtype) for s in landed],
                     0, 3 * n, pass_on, aliases={a: a for a in range(n)})
    return [g.reshape(NCHIP, 2 * g.shape[2], g.shape[3]) for g in full]


def _gather_land(tag, state, after):
    send, recv, thru, plan = state
    n = len(thru)
    landed = _split_wait(f"gather_wait_{tag}", thru, send, recv, after, 3 * n, plan)

    def pass_plan(refs, x, y, c):
        sib = (x, y, 1 - c)
        return [(refs[a].at[2 * px + py, c], refs[a].at[2 * px + py, c], sib, refs[a].at[2 * px + py, 1 - c])
                for a in range(n) for px, py in _chips_of(x, y)]

    send, recv, thru, token = _split_start(f"gather_pass_start_{tag}", landed, 3 * n, pass_plan)
    return (send, recv, thru, pass_plan), token


def _gather_done(tag, state, after):
    send, recv, thru, plan = state
    full = _split_wait(f"gather_pass_wait_{tag}", thru, send, recv, after, 3 * len(thru), plan)
    return [g.reshape(NCHIP, 2 * g.shape[2], g.shape[3]) for g in full]


_OTHERS = tuple((fx, fy, fc) for fx in (0, 1) for fy in (0, 1) for fc in (0, 1) if fx or fy or fc)


def _allgather_start(tag, buf, after=None):
    def plan(refs, x, y, c):
        me = 4 * x + 2 * y + c
        copies = []
        for fx, fy, fc in _OTHERS:
            px, py, pc = _flip(x, fx), _flip(y, fy), _flip(c, fc)
            copies.append((refs[0].at[me], refs[0].at[me], (px, py, pc), refs[0].at[4 * px + 2 * py + pc]))
        return copies

    send, recv, thru, token = _split_start(f"{tag}_start", [buf], len(_OTHERS), plan, after)
    return (send, recv, thru, plan), token


def _allgather_done(tag, state, after):
    send, recv, thru, plan = state
    return _split_wait(f"{tag}_wait", thru, send, recv, after, len(_OTHERS), plan)[0]


def _gather_small(name, smalls):
    n = len(smalls)

    def plan(ins, outs, x, y, c):
        q = 2 * x + y
        local = [(ins[b], outs[b].at[q]) for b in range(n)]
        remote = [(ins[b], outs[b].at[q], (px, py, c), outs[b].at[2 * px + py], None)
                  for b in range(n) for px, py in _chips_of(x, y)]
        return local, remote

    return _exchange(name, smalls, [jax.ShapeDtypeStruct((NCHIP,) + s.shape, s.dtype) for s in smalls], n, 3 * n, plan)


def _rs_pair_start(tag, grads, after=None):
    n = len(grads)
    g5 = [g.reshape(NCHIP, 2, g.shape[1] // 2, g.shape[2]) for g in grads]
    half = [(g.shape[2], g.shape[3]) for g in g5]
    land = [lax.empty((NCHIP,) + h, g.dtype) for h, g in zip(half, grads)]

    def plan(refs, x, y, c):
        sib = (x, y, 1 - c)
        return [(refs[a].at[:, 1 - c], refs[n + a], sib, refs[n + a]) for a in range(n)]

    send, recv, thru, token = _split_start(f"rs_pair_start_{tag}", g5 + land, n, plan, after)
    return (send, recv, thru, plan, half), token


def _rs_chips_start(tag, state, after, c_idx):
    send, recv, thru, pair_plan, half = state
    n = len(half)
    landed = _split_wait(f"rs_pair_wait_{tag}", thru, send, recv, after, n, pair_plan)
    g5, got = landed[:n], landed[n:]
    part = [_pair_add(f"rs_pair_add_{tag}{a}", g5[a], got[a], c_idx, _tile(half[a][0], 512)) for a in range(n)]
    land = [lax.empty((NCHIP - 1,) + h, BF16) for h in half]

    def plan(refs, x, y, c):
        return [(refs[a].at[2 * px + py], refs[n + a].at[k], (px, py, c), refs[n + a].at[k])
                for a in range(n) for k, (px, py) in enumerate(_chips_of(x, y))]

    send, recv, thru, token = _split_start(f"rs_chips_start_{tag}", part + land, 3 * n, plan)
    return (send, recv, thru, plan, half), token


def _rs_finish(tag, state, after, qc_idx):
    send, recv, thru, plan, half = state
    n = len(half)
    landed = _split_wait(f"rs_chips_wait_{tag}", thru, send, recv, after, 3 * n, plan)
    red = [_chip_add(f"rs_chip_add_{tag}{a}", landed[a], landed[n + a], qc_idx, _tile(half[a][0], 512)) for a in range(n)]

    def share(ins, outs, x, y, c):
        sib = (x, y, 1 - c)
        return [], [(ins[a].at[c], outs[a].at[c], sib, outs[a].at[1 - c], None) for a in range(n)]

    full = _exchange(f"rs_share_{tag}", red, [jax.ShapeDtypeStruct((2,) + h, F32) for h in half], 0, n, share,
                     aliases={a: a for a in range(n)})
    return [f.reshape(2 * h[0], h[1]) for f, h in zip(full, half)]


def _rs_land(tag, state, after, qc_idx):
    send, recv, thru, plan, half = state
    n = len(half)
    landed = _split_wait(f"rs_chips_wait_{tag}", thru, send, recv, after, 3 * n, plan)
    red = [_chip_add(f"rs_chip_add_{tag}{a}", landed[a], landed[n + a], qc_idx, _tile(half[a][0], 512)) for a in range(n)]

    def share(refs, x, y, c):
        sib = (x, y, 1 - c)
        return [(refs[a].at[c], refs[a].at[c], sib, refs[a].at[1 - c]) for a in range(n)]

    send, recv, thru, token = _split_start(f"rs_share_start_{tag}", red, n, share)
    return (send, recv, thru, share, half), token


def _rs_done(tag, state, after):
    send, recv, thru, plan, half = state
    full = _split_wait(f"rs_share_wait_{tag}", thru, send, recv, after, len(half), plan)
    return [f.reshape(2 * h[0], h[1]) for f, h in zip(full, half)]


def _place(name, blk, idx):
    rows, cols = blk.shape

    def body(i_ref, b_ref, o_ref):
        o_ref[...] = b_ref[...]

    return pl.pallas_call(
        body, name=name,
        grid_spec=pltpu.PrefetchScalarGridSpec(
            num_scalar_prefetch=1, grid=(1,),
            in_specs=[pl.BlockSpec((rows, cols), lambda i, d: (0, 0))],
            out_specs=pl.BlockSpec((None, rows, cols), lambda i, d: (d[0], 0, 0))),
        out_shape=jax.ShapeDtypeStruct((8, rows, cols), blk.dtype),
        compiler_params=_cparams("arbitrary"),
    )(idx, blk)


def _pack(name, n_rows, pieces, after=None):
    arrays = [p[1] for p in pieces if p[0] != "zeros"]
    extra = [] if after is None else [after]

    def body(*refs):
        o_ref = refs[-1]
        o_ref[...] = jnp.zeros(o_ref.shape, F32)
        row, ai = 0, 0
        for p in pieces:
            kind = p[0]
            if kind == "zeros":
                row += p[1]
                continue
            ref = refs[len(extra) + ai]
            ai += 1
            if kind == "rows":
                r = ref.shape[0]
                o_ref[row:row + r, :] = ref[...]
            elif kind == "wide":
                r = ref.shape[1] // D
                for j in range(r):
                    o_ref[row + j:row + j + 1, :] = ref[:, j * D:(j + 1) * D]
            elif kind == "slice":
                r = p[3] - p[2]
                o_ref[row:row + r, :] = ref[p[2]:p[3], :]
            elif kind == "heads":
                r = CHUNK
                for h in range(HEADS):
                    o_ref[row:row + r, h * CHUNK:(h + 1) * CHUNK] = ref[h]
            else:
                r = 1
                o_ref[row:row + 1, 0:128] = ref[0:1, :]
            row += r
        assert row == n_rows, (row, n_rows)

    vmem = pl.BlockSpec(memory_space=pltpu.VMEM)
    return pl.pallas_call(
        body, name=name, out_shape=jax.ShapeDtypeStruct((n_rows, D), F32),
        in_specs=[pl.BlockSpec(memory_space=pl.ANY)] * len(extra) + [vmem] * len(arrays), out_specs=vmem,
        compiler_params=pltpu.CompilerParams(vmem_limit_bytes=VMEM_LIMIT),
    )(*extra, *arrays)


def _to_bf16_stack(name, w, q_idx):
    r, cols = w.shape
    half = r // 2
    tr = _tile(half, 256)
    nb = half // tr

    def body(q_ref, w_ref, o_ref):
        o_ref[...] = w_ref[...].astype(BF16)

    return pl.pallas_call(
        body, name=name,
        grid_spec=pltpu.PrefetchScalarGridSpec(
            num_scalar_prefetch=1, grid=(2 * nb,),
            in_specs=[pl.BlockSpec((tr, cols), lambda i, qr: (i, 0))],
            out_specs=pl.BlockSpec((None, None, tr, cols), lambda i, qr: (qr[0], i // nb, i % nb, 0))),
        out_shape=jax.ShapeDtypeStruct((NCHIP, 2, half, cols), BF16),
        compiler_params=_cparams("parallel"),
    )(q_idx, w)


def kernel(x, c, ada_w, ada_b, norm_ffn1, ffn1_w_gate, ffn1_w_up, ffn1_w_down, norm_mix, mix_w_in, mix_b_in, sgu_ln_g, sgu_ln_b, sgu_w_s, sgu_b_s, conv_w, conv_b, conv_ln_g, conv_ln_b, w_branch_a, w_branch_b, w_out, norm_ffn2, ffn2_w_gate, ffn2_w_up, ffn2_w_down, norm_final, loss_target, m_ada_w, m_ada_b, m_norm_ffn1, m_ffn1_w_gate, m_ffn1_w_up, m_ffn1_w_down, m_norm_mix, m_mix_w_in, m_mix_b_in, m_sgu_ln_g, m_sgu_ln_b, m_sgu_w_s, m_sgu_b_s, m_conv_w, m_conv_b, m_conv_ln_g, m_conv_ln_b, m_w_branch_a, m_w_branch_b, m_w_out, m_norm_ffn2, m_ffn2_w_gate, m_ffn2_w_up, m_ffn2_w_down, m_norm_final, v_ada_w, v_ada_b, v_norm_ffn1, v_ffn1_w_gate, v_ffn1_w_up, v_ffn1_w_down, v_norm_mix, v_mix_w_in, v_mix_b_in, v_sgu_ln_g, v_sgu_ln_b, v_sgu_w_s, v_sgu_b_s, v_conv_w, v_conv_b, v_conv_ln_g, v_conv_ln_b, v_w_branch_a, v_w_branch_b, v_w_out, v_norm_ffn2, v_ffn2_w_gate, v_ffn2_w_up, v_ffn2_w_down, v_norm_final):
    xi, yi, ci = lax.axis_index("x"), lax.axis_index("y"), lax.axis_index("c")
    q = 2 * xi + yi
    dev = 4 * xi + 2 * yi + ci
    c_idx = jnp.reshape(ci, (1,)).astype(jnp.int32)
    q_idx = jnp.reshape(q, (1,)).astype(jnp.int32)
    x0 = x[0]
    tgt = loss_target[0]
    qc_idx = jnp.stack([q, ci]).astype(jnp.int32)
    n_ada = ada_w.shape[2]

    c_all = _allgather8("ag_c", jnp.pad(c, ((0, 7), (0, 0))))[::8]
    ada_b_mine = lax.dynamic_slice(ada_b, (0, q * n_ada), (1, n_ada))
    mod_part = _ada_fwd("ada_fwd", c_all, ada_w[0], ada_b_mine)
    small_parts = _gather_small("gather_small", [mod_part, jnp.pad(conv_w[0], ((0, 1), (0, 0)))])

    big = dict(g1=ffn1_w_gate, u1=ffn1_w_up, d1=ffn1_w_down, win=mix_w_in, wa=w_branch_a, wb=w_branch_b, wo=w_out,
               g2=ffn2_w_gate, u2=ffn2_w_up, d2=ffn2_w_down)
    groups = dict(f1=("d1", "g1", "u1"), mx=("wo", "wa", "wb", "win"), f2=("d2", "g2", "u2"))
    gather_groups = dict(up1=("g1", "u1"), dn1=("d1",), win=("win",), mx=("wa", "wb", "wo"), up2=("g2", "u2"),
                         dn2=("d2",))
    gather_state = {}
    started = [small_parts[0]]
    for tag, keys in gather_groups.items():
        gather_state[tag], token = _gather_start(
            tag, [_to_bf16_stack(f"cast_{k}", big[k][0], q_idx) for k in keys], started[-1])
        started.append(token)

    mod_all = jnp.transpose(small_parts[0], (1, 0, 2)).reshape(8, N_MOD * D)
    mod = lax.dynamic_slice(mod_all, (dev, 0), (1, N_MOD * D))
    sh1, sc1, g1, sh2, sc2, g2, sh3, sc3 = [(mod, D, k) for k in range(N_MOD - 1)]
    g3 = mod[:, (N_MOD - 1) * D:]
    convw = jnp.transpose(small_parts[1], (1, 0, 2)).reshape(HALO, D)
    W = dict(zip(gather_groups["up1"], _gather_finish("up1", gather_state["up1"], [mod] + started)))

    causal = jnp.tril(jnp.ones((CHUNK, CHUNK), bool))
    wm_f = jnp.where(causal[None], sgu_w_s[0], 0.0)
    wm = wm_f.astype(BF16)
    wmt = jnp.swapaxes(wm_f, 1, 2).astype(BF16)
    bst = jnp.repeat(jnp.transpose(sgu_b_s[0]), CHUNK, axis=1)
    b_in = mix_b_in

    (h1,) = _rowwise_fwd("pre1", _f_pre, [x0], [norm_ffn1, sc1, sh1], [(D, BF16)], 256)
    a1, b1, s1 = _ffn_up("up1", h1, W["g1"], W["u1"])
    W.update(zip(gather_groups["dn1"], _gather_finish("dn1", gather_state["dn1"], s1)))
    f1 = _matmul("down1", s1, W["d1"], "fwd", False, F32)
    pass_win, token = _gather_land("win", gather_state["win"], f1)
    x1, h2 = _rowwise_fwd("res1", _f_res_pre(0.5), [x0, f1], [g1, norm_mix, sc2, sh2],
                          [(D, F32), (D, BF16)], 256, after=token)
    W.update(zip(gather_groups["win"], _gather_done("win", pass_win, x1)))
    proj = _matmul("in_proj", h2, W["win"], "fwd", True, F32)
    pass_mx, token = _gather_land("mx", gather_state["mx"], proj)
    ya = _sgu_fwd("sgu", proj, b_in, sgu_ln_g, sgu_ln_b, wm, bst, 256)
    z1 = _conv_fwd("conv", proj, b_in, convw, conv_b)
    (z3,) = _rowwise_fwd("conv_ln", _f_lnsilu, [z1], [conv_ln_g, conv_ln_b], [(D, BF16)], 256, after=token)
    W.update(zip(gather_groups["mx"], _gather_done("mx", pass_mx, z3)))
    y_a = _matmul("branch_a", ya, W["wa"], "fwd", False, F32)
    y_b = _matmul("branch_b", z3, W["wb"], "fwd", False, F32)
    pass_up2, token = _gather_land("up2", gather_state["up2"], y_b)
    (merged,) = _rowwise_fwd("merge", _f_merge, [(proj, 2 * D, 2), y_a, y_b], [(b_in, 2 * D, 2)],
                             [(D, BF16)], 256, after=token)
    y = _matmul("out_proj", merged, W["wo"], "fwd", False, F32)
    x2, h3 = _rowwise_fwd("res2", _f_res_pre(1.0), [x1, y], [g2, norm_ffn2, sc3, sh3], [(D, F32), (D, BF16)], 256)
    W.update(zip(gather_groups["up2"], _gather_done("up2", pass_up2, x2)))
    a3, b3, s3 = _ffn_up("up2", h3, W["g2"], W["u2"])
    W.update(zip(gather_groups["dn2"], _gather_finish("dn2", gather_state["dn2"], s3)))
    f3 = _matmul("down2", s3, W["d2"], "fwd", False, F32)
    dx2, df3, dg3, dnwf, loss_blk = _final("final", x2, f3, tgt, g3, norm_final.reshape(1, D), 256)

    G = {}

    def ffn_bwd(tag, df, s, a, b, h, wg, wu, wd, after=None):
        da, db = _dgrad_glu(f"d_down{tag}", df, wd, a, b, after)
        G["d" + tag] = _matmul(f"w_down{tag}", s, df, "wgrad", False, BF16)
        dh = _matmul(f"d_gate{tag}", da, wg, "dgrad", True, F32)
        dh = _matmul(f"d_up{tag}", db, wu, "dgrad", True, F32, add=dh)
        G["g" + tag] = _matmul(f"w_gate{tag}", h, da, "wgrad", True, BF16)
        G["u" + tag] = _matmul(f"w_up{tag}", h, db, "wgrad", True, BF16)
        return dh

    reduced = {}
    dh3 = ffn_bwd("2", df3, s3, a3, b3, h3, W["g2"], W["u2"], W["d2"])
    pair_f2, token = _rs_pair_start("f2", [G[k] for k in groups["f2"]])
    (dx1, dy), (dg2, dnw3, dsc3, dsh3) = _rowwise_bwd(
        "res2_b", _f_res_pre(1.0), [x1, y], [g2, norm_ffn2, sc3, sh3], [dx2, dh3], [F32, BF16], 256, after=token)
    dmerged = _matmul("d_out", dy, W["wo"], "dgrad", False, F32)
    G["wo"] = _matmul("w_out", merged, dy, "wgrad", False, BF16)
    rs_f2, token = _rs_chips_start("f2", pair_f2, [dmerged, G["wo"]], c_idx)
    (dgab, dy_a, dy_b), (dbgab,) = _rowwise_bwd(
        "merge_b", _f_merge, [(proj, 2 * D, 2), y_a, y_b], [(b_in, 2 * D, 2)], [dmerged],
        [BF16, BF16, BF16], 256, after=token)
    dya = _matmul("d_a", dy_a, W["wa"], "dgrad", False, F32)
    dz3 = _matmul("d_b", dy_b, W["wb"], "dgrad", False, F32)
    G["wa"] = _matmul("w_a", ya, dy_a, "wgrad", False, BF16)
    G["wb"] = _matmul("w_b", z3, dy_b, "wgrad", False, BF16)
    (dz1,), (dclg, dclb) = _rowwise_bwd("conv_ln_b", _f_lnsilu, [z1], [conv_ln_g, conv_ln_b], [dz3], [F32], 256)
    dcv, dcg, conv_small = _conv_bwd("conv_b", dz1, proj, b_in, convw)
    duv, dbuv, dlng, dlnb, dws, dbs = _sgu_bwd("sgu_b", proj, b_in, sgu_ln_g, sgu_ln_b, wm, wmt, bst, dya, 256)
    dproj = jnp.concatenate([duv, dcv, dcg, dgab], axis=1)
    dh2 = _matmul("d_in", dproj, W["win"], "dgrad", True, F32)
    G["win"] = _matmul("w_in", h2, dproj, "wgrad", True, BF16)
    pair_mx, token = _rs_pair_start("mx", [G[k] for k in groups["mx"]])
    share_f2, token2 = _rs_land("f2", rs_f2, [dh2, token], qc_idx)

    dev_idx = jnp.reshape(dev, (1,)).astype(jnp.int32)
    early = _pack("pack_early", HALO + CHUNK, [("slice", conv_small, 0, HALO), ("heads", dws)])
    ag_early, token3 = _allgather_start("ag_early", _place("place_early", early, dev_idx), token2)

    (dx0, df1), (dg1, dnw2, dsc2, dsh2) = _rowwise_bwd(
        "res1_b", _f_res_pre(0.5), [x0, f1], [g1, norm_mix, sc2, sh2],
        [dx1, dh2], [F32, BF16], 256, after=[token, token3])
    rs_mx, token = _rs_chips_start("mx", pair_mx, df1, c_idx)
    dh1 = ffn_bwd("1", df1, s1, a1, b1, h1, W["g1"], W["u1"], W["d1"], after=token)
    pair_f1, token = _rs_pair_start("f1", [G[k] for k in groups["f1"]])
    reduced.update(zip(groups["f2"], _rs_done("f2", share_f2, dh1)))
    share_mx, token2 = _rs_land("mx", rs_mx, [dh1, token], qc_idx)
    (grad_x,), (dnw1, dsc1, dsh1) = _rowwise_bwd(
        "pre1_b", _f_pre_keep, [x0], [norm_ffn1, sc1, sh1], [dx0, dh1], [F32], 256, after=[token, token2])
    reduced.update(zip(groups["mx"], _rs_done("mx", share_mx, grad_x)))
    tot_early = _sum8("sum_early", _allgather_done("ag_early", ag_early, grad_x), HALO)

    dbs_row = jnp.transpose(dbs[:, ::CHUNK]).reshape(1, D)
    small = _pack("pack_small", HALO, [
        ("rows", dnw1), ("rows", dnw2), ("wide", dbuv), ("slice", conv_small, 33, 34), ("slice", conv_small, 34, 35),
        ("wide", dbgab), ("rows", dlng), ("rows", dlnb), ("rows", dbs_row), ("slice", conv_small, 32, 33),
        ("rows", dclg), ("rows", dclb), ("rows", dnw3), ("rows", dnwf),
        ("rows", dsh1), ("rows", dsc1), ("rows", dg1), ("rows", dsh2), ("rows", dsc2), ("rows", dg2),
        ("rows", dsh3), ("rows", dsc3), ("rows", dg3), ("lanes", loss_blk), ("zeros", 6)])
    every = _allgather8("ag_small", small)
    rs_f1, token = _rs_chips_start("f1", pair_f1, every, c_idx)
    every = every.reshape(8, HALO, D)
    tot = _sum8("sum_small", every, HALO, after=token)
    loss = tot[25, 0]
    dmod_all = every[:, 16:16 + N_MOD, :].reshape(8, N_MOD * D)
    dmod_mine = lax.dynamic_slice(dmod_all, (0, q * n_ada), (8, n_ada))
    g_ada_w = _ada_wgrad("ada_wgrad", jnp.transpose(c_all), dmod_mine)

    given = dict(ada_w=(ada_w, m_ada_w, v_ada_w), ada_b=(ada_b, m_ada_b, v_ada_b),
                 norm_ffn1=(norm_ffn1, m_norm_ffn1, v_norm_ffn1),
                 ffn1_w_gate=(ffn1_w_gate, m_ffn1_w_gate, v_ffn1_w_gate), ffn1_w_up=(ffn1_w_up, m_ffn1_w_up, v_ffn1_w_up),
                 ffn1_w_down=(ffn1_w_down, m_ffn1_w_down, v_ffn1_w_down), norm_mix=(norm_mix, m_norm_mix, v_norm_mix),
                 mix_w_in=(mix_w_in, m_mix_w_in, v_mix_w_in), mix_b_in=(mix_b_in, m_mix_b_in, v_mix_b_in),
                 sgu_ln_g=(sgu_ln_g, m_sgu_ln_g, v_sgu_ln_g), sgu_ln_b=(sgu_ln_b, m_sgu_ln_b, v_sgu_ln_b),
                 sgu_w_s=(sgu_w_s, m_sgu_w_s, v_sgu_w_s), sgu_b_s=(sgu_b_s, m_sgu_b_s, v_sgu_b_s),
                 conv_w=(conv_w, m_conv_w, v_conv_w), conv_b=(conv_b, m_conv_b, v_conv_b),
                 conv_ln_g=(conv_ln_g, m_conv_ln_g, v_conv_ln_g), conv_ln_b=(conv_ln_b, m_conv_ln_b, v_conv_ln_b),
                 w_branch_a=(w_branch_a, m_w_branch_a, v_w_branch_a), w_branch_b=(w_branch_b, m_w_branch_b, v_w_branch_b),
                 w_out=(w_out, m_w_out, v_w_out), norm_ffn2=(norm_ffn2, m_norm_ffn2, v_norm_ffn2),
                 ffn2_w_gate=(ffn2_w_gate, m_ffn2_w_gate, v_ffn2_w_gate), ffn2_w_up=(ffn2_w_up, m_ffn2_w_up, v_ffn2_w_up),
                 ffn2_w_down=(ffn2_w_down, m_ffn2_w_down, v_ffn2_w_down), norm_final=(norm_final, m_norm_final, v_norm_final))
    out = {}

    def big_update(name, g2d):
        w, m, v = given[name]
        shp = w.shape
        res = _adamw(f"adamw_{name}", w.reshape(g2d.shape), g2d, m.reshape(g2d.shape), v.reshape(g2d.shape),
                     _tile(g2d.shape[0], 512 if g2d.shape[1] <= D else 256))
        out[name] = tuple(t.reshape(shp) for t in res)

    big_names = dict(g1="ffn1_w_gate", u1="ffn1_w_up", d1="ffn1_w_down", win="mix_w_in", wa="w_branch_a",
                     wb="w_branch_b", wo="w_out", g2="ffn2_w_gate", u2="ffn2_w_up", d2="ffn2_w_down")
    for key in groups["f2"] + groups["mx"]:
        big_update(big_names[key], reduced[key])
    big_update("ada_w", g_ada_w)

    small_rows = [("norm_ffn1", 0, 1), ("norm_mix", 1, 1), ("mix_b_in", 2, 6), ("sgu_ln_g", 8, 1), ("sgu_ln_b", 9, 1),
                  ("sgu_b_s", 10, 1), ("conv_b", 11, 1), ("conv_ln_g", 12, 1), ("conv_ln_b", 13, 1),
                  ("norm_ffn2", 14, 1), ("norm_final", 15, 1), ("ada_b", 16, N_MOD)]

    def pack_rows(which, label):
        pieces = []
        for nm, _, r in small_rows:
            t = given[nm][which]
            pieces.append(("wide", t) if r > 1 else ("rows", t.reshape(1, D)))
        return _pack(f"pack_{label}", HALO, pieces + [("zeros", HALO - 16 - N_MOD)])

    small_res = _adamw("adamw_small", pack_rows(0, "w"), tot, pack_rows(1, "m"), pack_rows(2, "v"), HALO)
    for nm, r0, r in small_rows:
        shp = given[nm][0].shape
        out[nm] = tuple(t[r0:r0 + r].reshape(shp) for t in small_res)
    n_cw = conv_w.shape[2]
    g_cw = lax.dynamic_slice(tot_early, (0, q * n_cw), (HALO, n_cw))
    pad_cw = lambda t: jnp.pad(t[0], ((0, 1), (0, 0)))
    cw_res = _adamw("adamw_conv_w", pad_cw(conv_w), g_cw, pad_cw(m_conv_w), pad_cw(v_conv_w), HALO)
    out["conv_w"] = tuple(t[:KCONV][None] for t in cw_res)
    g_ws = jnp.transpose(tot_early[HALO:].reshape(CHUNK, HEADS, CHUNK), (1, 0, 2)).reshape(HEADS * CHUNK, CHUNK)
    flat_ws = lambda t: t.reshape(HEADS * CHUNK, CHUNK)
    ws_res = _adamw("adamw_sgu_w_s", flat_ws(sgu_w_s), g_ws, flat_ws(m_sgu_w_s), flat_ws(v_sgu_w_s), 512)
    out["sgu_w_s"] = tuple(t.reshape(sgu_w_s.shape) for t in ws_res)

    done = [out[n][1] for n in ("ada_w", "sgu_w_s", "conv_w", "ada_b")] + [out[big_names[k]][1] for k in groups["f2"] + groups["mx"]]
    reduced.update(zip(groups["f1"], _rs_finish("f1", rs_f1, done, qc_idx)))
    for key in groups["f1"]:
        big_update(big_names[key], reduced[key])

    order = ["ada_w", "ada_b", "norm_ffn1", "ffn1_w_gate", "ffn1_w_up", "ffn1_w_down", "norm_mix", "mix_w_in", "mix_b_in",
             "sgu_ln_g", "sgu_ln_b", "sgu_w_s", "sgu_b_s", "conv_w", "conv_b", "conv_ln_g", "conv_ln_b", "w_branch_a",
             "w_branch_b", "w_out", "norm_ffn2", "ffn2_w_gate", "ffn2_w_up", "ffn2_w_down", "norm_final"]
    return (loss, grad_x[None], *[out[n][0] for n in order], *[out[n][1] for n in order],
            *[out[n][2] for n in order], *[out[n][3] for n in order])
```

```python
import functools

import jax
import jax.numpy as jnp
from jax import lax
from jax.experimental import pallas as pl
from jax.experimental.pallas import tpu as pltpu

F32 = jnp.float32
BF16 = jnp.bfloat16
D = 1024
HEADS = 8
CHUNK = 128
KCONV = 31
HALO = 32
N_MOD = 9
EPS = 1e-6
NCHIP = 4
VMEM_LIMIT = 56 * 1024 * 1024
MESH = pl.DeviceIdType.MESH

ADAM_LR = 0.001
ADAM_B1 = 0.9
ADAM_B2 = 0.999
ADAM_EPS = 1e-08
ADAM_WD = 0.01
ADAM_STEP = 10

NN = (((1,), (0,)), ((), ()))
NT = (((1,), (1,)), ((), ()))
TN = (((0,), (0,)), ((), ()))


def _cparams(*sem):
    return pltpu.CompilerParams(dimension_semantics=sem or None, vmem_limit_bytes=VMEM_LIMIT)


def _rms(x, w):
    return x * lax.rsqrt(jnp.mean(x * x, axis=-1, keepdims=True) + EPS) * w


def _ln(x, g, b):
    mu = jnp.mean(x, axis=-1, keepdims=True)
    xc = x - mu
    var = jnp.mean(xc * xc, axis=-1, keepdims=True)
    return xc * lax.rsqrt(var + EPS) * g + b


def _silu(x):
    return x * jax.nn.sigmoid(x)


def _gelu(x):
    return x * (0.5 * (1.0 + jnp.tanh(0.7978845608028654 * (x + 0.044715 * (x * x * x)))))


def _f_pre(x, nw, sc, sh):
    return _rms(x, nw) * (1.0 + sc) + sh


def _f_pre_keep(x, nw, sc, sh):
    return x, _f_pre(x, nw, sc, sh)


def _f_res_pre(scale):
    def fn(x, f, g, nw, sc, sh):
        xn = x + (scale * g) * f
        return xn, _f_pre(xn, nw, sc, sh)
    return fn


def _f_glu(a, b):
    return _silu(a) * b


def _f_merge(gab, ya, yb, bgab):
    g = gab + bgab
    return jax.nn.sigmoid(g[:, :D]) * ya + jax.nn.sigmoid(g[:, D:]) * yb


def _f_lnsilu(z, g, b):
    return _silu(_ln(z, g, b))


def _f_mixa(uv, buv, g, b):
    t = uv + buv
    return _gelu(t[:, :D]), _ln(_gelu(t[:, D:]), g, b)


def _col(t):
    return t if isinstance(t, tuple) else (t, t.shape[-1], 0)


def _afters(after):
    if after is None:
        return []
    return list(after) if isinstance(after, (list, tuple)) else [after]


def _rowwise_fwd(name, fn, rows, params, outs, tr, after=None):
    rows = [_col(r) for r in rows]
    params = [_col(p) for p in params]
    extra = _afters(after)
    n_rows = rows[0][0].shape[0]
    nin = len(rows) + len(params)

    def body(*refs):
        res = fn(*[r[...].astype(F32) for r in refs[:nin]])
        res = res if isinstance(res, tuple) else (res,)
        for o, v in zip(refs[nin + len(extra):], res, strict=True):
            o[...] = v.astype(o.dtype)

    return pl.pallas_call(
        body, name=name, grid=(n_rows // tr,),
        in_specs=[pl.BlockSpec((tr, w), functools.partial(lambda cb, i: (i, cb), cb)) for _, w, cb in rows]
        + [pl.BlockSpec((1, w), functools.partial(lambda cb, i: (0, cb), cb)) for _, w, cb in params]
        + [pl.BlockSpec(memory_space=pl.ANY)] * len(extra),
        out_specs=[pl.BlockSpec((tr, w), lambda i: (i, 0)) for w, _ in outs],
        out_shape=[jax.ShapeDtypeStruct((n_rows, w), dt) for w, dt in outs],
        compiler_params=_cparams("parallel"),
    )(*[r[0] for r in rows], *[p[0] for p in params], *extra)


def _rowwise_bwd(name, fn, rows, params, cots, row_grads, tr, after=None):
    rows = [_col(r) for r in rows]
    params = [_col(p) for p in params]
    cots = [_col(t) for t in cots]
    extra = _afters(after)
    n_rows = rows[0][0].shape[0]
    nr, npar, nc = len(rows), len(params), len(cots)
    nin = nr + npar + nc
    n_rg = sum(dt is not None for dt in row_grads)

    def body(*refs):
        i = pl.program_id(0)
        prim = [r[...].astype(F32) for r in refs[:nr + npar]]
        ct = tuple(r[...].astype(F32) for r in refs[nr + npar:nin])
        _, vjp = jax.vjp(fn, *prim)
        g = vjp(ct if nc > 1 else ct[0])
        outs = refs[nin + len(extra):]
        oi = 0
        for j, dt in enumerate(row_grads):
            if dt is not None:
                outs[oi][...] = g[j].astype(dt)
                oi += 1
        for j in range(npar):
            acc = outs[n_rg + j]

            @pl.when(i == 0)
            def _(acc=acc):
                acc[...] = jnp.zeros(acc.shape, F32)

            acc[...] += g[nr + j]

    res = pl.pallas_call(
        body, name=name, grid=(n_rows // tr,),
        in_specs=[pl.BlockSpec((tr, w), functools.partial(lambda cb, i: (i, cb), cb)) for _, w, cb in rows]
        + [pl.BlockSpec((1, w), functools.partial(lambda cb, i: (0, cb), cb)) for _, w, cb in params]
        + [pl.BlockSpec((tr, w), functools.partial(lambda cb, i: (i, cb), cb)) for _, w, cb in cots]
        + [pl.BlockSpec(memory_space=pl.ANY)] * len(extra),
        out_specs=[pl.BlockSpec((tr, rows[j][1]), lambda i: (i, 0)) for j, dt in enumerate(row_grads) if dt is not None]
        + [pl.BlockSpec((1, w), lambda i: (0, 0)) for _, w, _ in params],
        out_shape=[jax.ShapeDtypeStruct((n_rows, rows[j][1]), dt) for j, dt in enumerate(row_grads) if dt is not None]
        + [jax.ShapeDtypeStruct((1, w), F32) for _, w, _ in params],
        compiler_params=_cparams("arbitrary"),
    )(*[r[0] for r in rows], *[p[0] for p in params], *[t[0] for t in cots], *extra)
    return res[:n_rg], res[n_rg:]


def _final(name, x, f, tgt, g, nw, tr):
    n_rows = x.shape[0]

    def body(x_ref, f_ref, t_ref, g_ref, nw_ref, dx_ref, df_ref, dg_ref, dnw_ref, loss_ref):
        i = pl.program_id(0)
        tg = t_ref[...]

        def fn(xv, fv, gv, nwv):
            e = _rms(xv + (0.5 * gv) * fv, nwv) - tg
            return 0.5 * jnp.mean(e * e, axis=-1, keepdims=True)

        per_row, vjp = jax.vjp(fn, x_ref[...], f_ref[...], g_ref[...], nw_ref[...])
        dx, df, dg, dnw = vjp(jnp.ones_like(per_row))
        dx_ref[...] = dx
        df_ref[...] = df.astype(df_ref.dtype)

        @pl.when(i == 0)
        def _():
            dg_ref[...] = jnp.zeros(dg_ref.shape, F32)
            dnw_ref[...] = jnp.zeros(dnw_ref.shape, F32)
            loss_ref[...] = jnp.zeros(loss_ref.shape, F32)

        dg_ref[...] += dg
        dnw_ref[...] += dnw
        loss_ref[...] += jnp.broadcast_to(jnp.sum(per_row, axis=0, keepdims=True), loss_ref.shape)

    row = pl.BlockSpec((tr, D), lambda i: (i, 0))
    par = pl.BlockSpec((1, D), lambda i: (0, 0))
    return pl.pallas_call(
        body, name=name, grid=(n_rows // tr,),
        in_specs=[row, row, row, par, par],
        out_specs=[row, row, par, par, pl.BlockSpec((8, 128), lambda i: (0, 0))],
        out_shape=[jax.ShapeDtypeStruct((n_rows, D), F32), jax.ShapeDtypeStruct((n_rows, D), BF16),
                   jax.ShapeDtypeStruct((1, D), F32), jax.ShapeDtypeStruct((1, D), F32),
                   jax.ShapeDtypeStruct((8, 128), F32)],
        compiler_params=_cparams("arbitrary"),
    )(x, f, tgt, g, nw)


def _tile(dim, pref):
    t = min(dim, pref)
    while dim % t:
        t //= 2
    return t


def _matmul(name, a, b, kind, col, out_dtype, add=None, after=None):
    if kind == "wgrad":
        m, kg = a.shape
        ng = b.shape[1]
        r, c = (kg, ng // NCHIP) if col else (kg // NCHIP, ng)
        tm, tn, tk = _tile(r, 1024), _tile(c, 1536), _tile(m, 2048)
        grid = (kg // tm, ng // tn, m // tk)
        a_spec = pl.BlockSpec((tk, tm), lambda i, j, k: (k, i))
        b_spec = pl.BlockSpec((tk, tn), lambda i, j, k: (k, j))
        if col:
            o_spec = pl.BlockSpec((None, tm, tn), lambda i, j, k: (j // (c // tn), i, j % (c // tn)))
        else:
            o_spec = pl.BlockSpec((None, tm, tn), lambda i, j, k: (i // (r // tm), i % (r // tm), j))
        out_shape = jax.ShapeDtypeStruct((NCHIP, r, c), out_dtype)
        dims = TN
    else:
        _, r, c = b.shape
        m = a.shape[0]
        kg, ng = (r, NCHIP * c) if col else (NCHIP * r, c)
        tm = _tile(m, 1024)
        if kind == "fwd":
            tn, tk = _tile(c, 1536), _tile(r, 2048)
            grid = (m // tm, ng // tn, kg // tk)
            a_spec = pl.BlockSpec((tm, tk), lambda i, j, k: (i, k))
            if col:
                b_spec = pl.BlockSpec((None, tk, tn), lambda i, j, k: (j // (c // tn), k, j % (c // tn)))
            else:
                b_spec = pl.BlockSpec((None, tk, tn), lambda i, j, k: (k // (r // tk), k % (r // tk), j))
            out_shape = jax.ShapeDtypeStruct((m, ng), out_dtype)
            dims = NN
        else:
            tn, tk = _tile(r, 1024), _tile(c, 2048)
            grid = (m // tm, kg // tn, ng // tk)
            a_spec = pl.BlockSpec((tm, tk), lambda i, j, k: (i, k))
            if col:
                b_spec = pl.BlockSpec((None, tn, tk), lambda i, j, k: (k // (c // tk), j, k % (c // tk)))
            else:
                b_spec = pl.BlockSpec((None, tn, tk), lambda i, j, k: (j // (r // tn), j % (r // tn), k))
            out_shape = jax.ShapeDtypeStruct((m, kg), out_dtype)
            dims = NT
        o_spec = pl.BlockSpec((tm, tn), lambda i, j, k: (i, j))
    nk = grid[2]
    has_add = add is not None
    extra = [] if after is None else [after]

    def body(*refs):
        a_ref, b_ref = refs[0], refs[1]
        o_ref = refs[2 + has_add + len(extra)]
        k = pl.program_id(2)
        p = lax.dot_general(a_ref[...].astype(BF16), b_ref[...].astype(BF16), dims, preferred_element_type=F32)

        def finish(r_):
            if has_add:
                r_ = r_ + refs[2][...]
            o_ref[...] = r_.astype(o_ref.dtype)

        if nk == 1:
            finish(p)
        else:
            acc_ref = refs[-1]

            @pl.when(k == 0)
            def _():
                acc_ref[...] = p

            @pl.when(jnp.logical_and(k > 0, k < nk - 1))
            def _():
                acc_ref[...] += p

            @pl.when(k == nk - 1)
            def _():
                finish(acc_ref[...] + p)

    in_specs = [a_spec, b_spec] + ([pl.BlockSpec((tm, tn), lambda i, j, k: (i, j))] if has_add else [])
    in_specs += [pl.BlockSpec(memory_space=pl.ANY)] * len(extra)
    args = (a, b) + ((add,) if has_add else ()) + tuple(extra)
    return pl.pallas_call(
        body, name=name, grid=grid, in_specs=in_specs, out_specs=o_spec, out_shape=out_shape,
        scratch_shapes=[pltpu.VMEM((tm, tn), F32)] if nk > 1 else [],
        compiler_params=_cparams("parallel", "parallel", "arbitrary"),
    )(*args)


def _ffn_up(name, h, wg, wu):
    m, k = h.shape
    _, _, c = wg.shape
    n = NCHIP * c
    tm, tn = _tile(m, 1024), _tile(c, 1024)

    def body(h_ref, wg_ref, wu_ref, a_ref, b_ref, s_ref):
        hv = h_ref[...]
        a = jnp.dot(hv, wg_ref[...], preferred_element_type=F32)
        b = jnp.dot(hv, wu_ref[...], preferred_element_type=F32)
        a_ref[...] = a
        b_ref[...] = b
        s_ref[...] = (_silu(a) * b).astype(s_ref.dtype)

    w_spec = pl.BlockSpec((None, k, tn), lambda i, j: (j // (c // tn), 0, j % (c // tn)))
    o_spec = pl.BlockSpec((tm, tn), lambda i, j: (i, j))
    return pl.pallas_call(
        body, name=name, grid=(m // tm, n // tn),
        in_specs=[pl.BlockSpec((tm, k), lambda i, j: (i, 0)), w_spec, w_spec],
        out_specs=[o_spec, o_spec, o_spec],
        out_shape=[jax.ShapeDtypeStruct((m, n), F32), jax.ShapeDtypeStruct((m, n), F32),
                   jax.ShapeDtypeStruct((m, n), BF16)],
        compiler_params=_cparams("parallel", "parallel"),
    )(h, wg, wu)


def _dgrad_pair(name, da, db, wg, wu):
    m = da.shape[0]
    _, r, c = wg.shape
    tm, tn = _tile(m, 1024), _tile(r, 1024)
    n1 = NCHIP

    def body(da_ref, db_ref, wg_ref, wu_ref, o_ref, acc_ref):
        k = pl.program_id(2)

        @pl.when(k == 0)
        def _():
            acc_ref[...] = lax.dot_general(da_ref[...], wg_ref[...], NT, preferred_element_type=F32)

        @pl.when(jnp.logical_and(k > 0, k < n1))
        def _():
            acc_ref[...] += lax.dot_general(da_ref[...], wg_ref[...], NT, preferred_element_type=F32)

        @pl.when(jnp.logical_and(k >= n1, k < 2 * n1 - 1))
        def _():
            acc_ref[...] += lax.dot_general(db_ref[...], wu_ref[...], NT, preferred_element_type=F32)

        @pl.when(k == 2 * n1 - 1)
        def _():
            o_ref[...] = acc_ref[...] + lax.dot_general(db_ref[...], wu_ref[...], NT, preferred_element_type=F32)

    first = lambda k: jnp.minimum(k, n1 - 1)
    second = lambda k: jnp.maximum(k - n1, 0)
    return pl.pallas_call(
        body, name=name, grid=(m // tm, r // tn, 2 * n1),
        in_specs=[pl.BlockSpec((tm, c), lambda i, j, k: (i, first(k))),
                  pl.BlockSpec((tm, c), lambda i, j, k: (i, second(k))),
                  pl.BlockSpec((None, tn, c), lambda i, j, k: (first(k), j, 0)),
                  pl.BlockSpec((None, tn, c), lambda i, j, k: (second(k), j, 0))],
        out_specs=pl.BlockSpec((tm, tn), lambda i, j, k: (i, j)),
        out_shape=jax.ShapeDtypeStruct((m, r), F32),
        scratch_shapes=[pltpu.VMEM((tm, tn), F32)],
        compiler_params=_cparams("parallel", "parallel", "arbitrary"),
    )(da, db, wg, wu)


def _dgrad_glu(name, df, wd, a, b, after=None):
    m = df.shape[0]
    _, r, c = wd.shape
    tm, tn = _tile(m, 512), _tile(r, 1024)
    extra = [] if after is None else [after]

    def body(df_ref, w_ref, a_ref, b_ref, *rest):
        da_ref, db_ref = rest[-2], rest[-1]
        ds = lax.dot_general(df_ref[...], w_ref[...], NT, preferred_element_type=F32)
        _, vjp = jax.vjp(_f_glu, a_ref[...], b_ref[...])
        da, db = vjp(ds)
        da_ref[...] = da.astype(da_ref.dtype)
        db_ref[...] = db.astype(db_ref.dtype)

    t_spec = pl.BlockSpec((tm, tn), lambda i, j: (i, j))
    sds = jax.ShapeDtypeStruct((m, NCHIP * r), BF16)
    return pl.pallas_call(
        body, name=name, grid=(m // tm, NCHIP * r // tn),
        in_specs=[pl.BlockSpec((tm, c), lambda i, j: (i, 0)),
                  pl.BlockSpec((None, tn, c), lambda i, j: (j // (r // tn), j % (r // tn), 0)), t_spec, t_spec]
        + [pl.BlockSpec(memory_space=pl.ANY)] * len(extra),
        out_specs=[t_spec, t_spec], out_shape=[sds, sds],
        compiler_params=_cparams("parallel", "parallel"),
    )(df, wd, a, b, *extra)


def _sgu_fwd(name, proj, b_in, lng, lnb, wm, bst, tr):
    n_rows = proj.shape[0]

    def body(uv_ref, buv_ref, g_ref, b_ref, wm_ref, bst_ref, ya_ref):
        u, vln = _f_mixa(uv_ref[...], buv_ref[...], g_ref[...], b_ref[...])
        vb = vln.astype(BF16)
        for ck in range(tr // CHUNK):
            rs = slice(ck * CHUNK, (ck + 1) * CHUNK)
            for h in range(HEADS):
                cs = slice(h * CHUNK, (h + 1) * CHUNK)
                vm = jnp.dot(wm_ref[h], vb[rs, cs], preferred_element_type=F32) + bst_ref[:, cs]
                ya_ref[rs, cs] = (u[rs, cs] * vm).astype(ya_ref.dtype)

    return pl.pallas_call(
        body, name=name, grid=(n_rows // tr,),
        in_specs=[pl.BlockSpec((tr, 2 * D), lambda i: (i, 0)), pl.BlockSpec((1, 2 * D), lambda i: (0, 0)),
                  pl.BlockSpec((1, D), lambda i: (0, 0)), pl.BlockSpec((1, D), lambda i: (0, 0)),
                  pl.BlockSpec((HEADS, CHUNK, CHUNK), lambda i: (0, 0, 0)),
                  pl.BlockSpec((CHUNK, D), lambda i: (0, 0))],
        out_specs=pl.BlockSpec((tr, D), lambda i: (i, 0)),
        out_shape=jax.ShapeDtypeStruct((n_rows, D), BF16),
        compiler_params=_cparams("parallel"),
    )(proj, b_in, lng, lnb, wm, bst)


def _sgu_bwd(name, proj, b_in, lng, lnb, wm, wmt, bst, dya, tr):
    n_rows = proj.shape[0]

    def body(uv_ref, buv_ref, g_ref, b_ref, wm_ref, wmt_ref, bst_ref, dya_ref,
             duv_ref, dbuv_ref, dg_ref, db_ref, dws_ref, dbs_ref, du_s, dvln_s):
        i = pl.program_id(0)

        @pl.when(i == 0)
        def _():
            dbuv_ref[...] = jnp.zeros(dbuv_ref.shape, F32)
            dg_ref[...] = jnp.zeros(dg_ref.shape, F32)
            db_ref[...] = jnp.zeros(db_ref.shape, F32)
            dws_ref[...] = jnp.zeros(dws_ref.shape, F32)
            dbs_ref[...] = jnp.zeros(dbs_ref.shape, F32)

        (u, vln), vjp = jax.vjp(_f_mixa, uv_ref[...], buv_ref[...], g_ref[...], b_ref[...])
        vb = vln.astype(BF16)
        dya_v = dya_ref[...].astype(F32)
        tpos = lax.broadcasted_iota(jnp.int32, (CHUNK, CHUNK), 0)
        spos = lax.broadcasted_iota(jnp.int32, (CHUNK, CHUNK), 1)
        causal = (tpos >= spos).astype(F32)
        for ck in range(tr // CHUNK):
            rs = slice(ck * CHUNK, (ck + 1) * CHUNK)
            for h in range(HEADS):
                cs = slice(h * CHUNK, (h + 1) * CHUNK)
                vbh = vb[rs, cs]
                vm = jnp.dot(wm_ref[h], vbh, preferred_element_type=F32) + bst_ref[:, cs]
                dyb = dya_v[rs, cs]
                du_s[rs, cs] = dyb * vm
                dvm = dyb * u[rs, cs]
                dvmb = dvm.astype(BF16)
                dvln_s[rs, cs] = jnp.dot(wmt_ref[h], dvmb, preferred_element_type=F32)
                dws_ref[h] += causal * lax.dot_general(dvmb, vbh, NT, preferred_element_type=F32)
                dbs_ref[:, cs] += jnp.broadcast_to(jnp.sum(dvm, axis=1, keepdims=True), (CHUNK, CHUNK))
        duv, dbuv, dg, db = vjp((du_s[...], dvln_s[...]))
        duv_ref[...] = duv.astype(duv_ref.dtype)
        dbuv_ref[...] += dbuv
        dg_ref[...] += dg
        db_ref[...] += db

    par = pl.BlockSpec((1, D), lambda i: (0, 0))
    par2 = pl.BlockSpec((1, 2 * D), lambda i: (0, 0))
    w_spec = pl.BlockSpec((HEADS, CHUNK, CHUNK), lambda i: (0, 0, 0))
    b_spec = pl.BlockSpec((CHUNK, D), lambda i: (0, 0))
    return pl.pallas_call(
        body, name=name, grid=(n_rows // tr,),
        in_specs=[pl.BlockSpec((tr, 2 * D), lambda i: (i, 0)), par2, par, par, w_spec, w_spec, b_spec,
                  pl.BlockSpec((tr, D), lambda i: (i, 0))],
        out_specs=[pl.BlockSpec((tr, 2 * D), lambda i: (i, 0)), par2, par, par, w_spec, b_spec],
        out_shape=[jax.ShapeDtypeStruct((n_rows, 2 * D), BF16), jax.ShapeDtypeStruct((1, 2 * D), F32),
                   jax.ShapeDtypeStruct((1, D), F32), jax.ShapeDtypeStruct((1, D), F32),
                   jax.ShapeDtypeStruct((HEADS, CHUNK, CHUNK), F32), jax.ShapeDtypeStruct((CHUNK, D), F32)],
        scratch_shapes=[pltpu.VMEM((tr, D), F32), pltpu.VMEM((tr, D), F32)],
        compiler_params=_cparams("arbitrary"),
    )(proj, b_in, lng, lnb, wm, wmt, bst, dya)


CT = 256
RB = 128
CV0 = 2 * D // CT
CG0 = 3 * D // CT


def _shift_bank(win):
    return [win] + [pltpu.roll(win, RB + HALO - r, 0) for r in range(1, 8)]


def _shifted(bank, sh):
    lo = 8 * (sh // 8)
    return bank[sh % 8][lo:lo + RB]


def _conv_fwd(name, proj, b_in, w, cb):
    n_rows = proj.shape[0]

    def body(cv_ref, cg_ref, bcv_ref, bcg_ref, w_ref, cb_ref, z1_ref, zp_ref):
        zp_ref[0:HALO, :] = jnp.zeros((HALO, CT), F32)
        zp_ref[HALO:, :] = (cv_ref[...] + bcv_ref[...]) * jax.nn.sigmoid(cg_ref[...] + bcg_ref[...])

        def blk(rb, carry):
            base = pl.multiple_of(rb * RB, RB)
            bank = _shift_bank(zp_ref[pl.ds(base, RB + HALO), :])
            acc = jnp.broadcast_to(cb_ref[...], (RB, CT))
            for k in range(KCONV):
                acc = acc + w_ref[k:k + 1, :] * _shifted(bank, k + 2)
            z1_ref[pl.ds(base, RB), :] = acc
            return carry

        lax.fori_loop(0, n_rows // RB, blk, 0)

    return pl.pallas_call(
        body, name=name, grid=(D // CT,),
        in_specs=[pl.BlockSpec((n_rows, CT), lambda j: (0, CV0 + j)), pl.BlockSpec((n_rows, CT), lambda j: (0, CG0 + j)),
                  pl.BlockSpec((1, CT), lambda j: (0, CV0 + j)), pl.BlockSpec((1, CT), lambda j: (0, CG0 + j)),
                  pl.BlockSpec((HALO, CT), lambda j: (0, j)), pl.BlockSpec((1, CT), lambda j: (0, j))],
        out_specs=pl.BlockSpec((n_rows, CT), lambda j: (0, j)),
        out_shape=jax.ShapeDtypeStruct((n_rows, D), F32),
        scratch_shapes=[pltpu.VMEM((n_rows + HALO, CT), F32)],
        compiler_params=_cparams("parallel"),
    )(proj, proj, b_in, b_in, w, cb)


def _conv_bwd(name, dz1, proj, b_in, w):
    n_rows = proj.shape[0]

    def body(dz_ref, cv_ref, cg_ref, bcv_ref, bcg_ref, w_ref, dcv_ref, dcg_ref, sm_ref, zp_ref, dzp_ref):
        cvb = cv_ref[...] + bcv_ref[...]
        sg = jax.nn.sigmoid(cg_ref[...] + bcg_ref[...])
        zp_ref[0:HALO, :] = jnp.zeros((HALO, CT), F32)
        zp_ref[HALO:, :] = cvb * sg
        dz = dz_ref[...]
        dzp_ref[0:n_rows, :] = dz
        dzp_ref[n_rows:, :] = jnp.zeros((HALO, CT), F32)
        sm_ref[...] = jnp.zeros(sm_ref.shape, F32)
        sm_ref[32:33, :] = jnp.sum(dz, axis=0, keepdims=True)

        def blk(rb, carry):
            base = pl.multiple_of(rb * RB, RB)
            dbank = _shift_bank(dzp_ref[pl.ds(base, RB + HALO), :])
            zbank = _shift_bank(zp_ref[pl.ds(base, RB + HALO), :])
            dzb = dbank[0][0:RB]
            acc = jnp.zeros((RB, CT), F32)
            for k in range(KCONV):
                acc = acc + w_ref[k:k + 1, :] * _shifted(dbank, KCONV - 1 - k)
                sm_ref[k:k + 1, :] += jnp.sum(dzb * _shifted(zbank, k + 2), axis=0, keepdims=True)
            dzp_ref[pl.ds(base, RB), :] = acc
            return carry

        lax.fori_loop(0, n_rows // RB, blk, 0)
        dz0 = dzp_ref[0:n_rows, :]
        dcv = dz0 * sg
        dcg = dz0 * cvb * (sg * (1.0 - sg))
        dcv_ref[...] = dcv.astype(dcv_ref.dtype)
        dcg_ref[...] = dcg.astype(dcg_ref.dtype)
        sm_ref[33:34, :] = jnp.sum(dcv, axis=0, keepdims=True)
        sm_ref[34:35, :] = jnp.sum(dcg, axis=0, keepdims=True)

    col = pl.BlockSpec((n_rows, CT), lambda j: (0, j))
    return pl.pallas_call(
        body, name=name, grid=(D // CT,),
        in_specs=[col, pl.BlockSpec((n_rows, CT), lambda j: (0, CV0 + j)), pl.BlockSpec((n_rows, CT), lambda j: (0, CG0 + j)),
                  pl.BlockSpec((1, CT), lambda j: (0, CV0 + j)), pl.BlockSpec((1, CT), lambda j: (0, CG0 + j)),
                  pl.BlockSpec((HALO, CT), lambda j: (0, j))],
        out_specs=[col, col, pl.BlockSpec((40, CT), lambda j: (0, j))],
        out_shape=[jax.ShapeDtypeStruct((n_rows, D), BF16), jax.ShapeDtypeStruct((n_rows, D), BF16),
                   jax.ShapeDtypeStruct((40, D), F32)],
        scratch_shapes=[pltpu.VMEM((n_rows + HALO, CT), F32), pltpu.VMEM((n_rows + HALO, CT), F32)],
        compiler_params=_cparams("parallel"),
    )(dz1, proj, proj, b_in, b_in, w)


ADA_TN = 768


def _split_bf16(v):
    hi = v.astype(BF16)
    return hi, (v - hi.astype(F32)).astype(BF16)


def _ada_fwd(name, c_all, w, b):
    n = w.shape[1]

    def body(c_ref, w_ref, b_ref, o_ref):
        ch, cl = _split_bf16(_silu(c_ref[...]))
        wh, wl = _split_bf16(w_ref[...])
        acc = jnp.dot(ch, wl, preferred_element_type=F32) + jnp.dot(cl, wh, preferred_element_type=F32)
        o_ref[...] = acc + jnp.dot(ch, wh, preferred_element_type=F32) + b_ref[...]

    return pl.pallas_call(
        body, name=name, grid=(n // ADA_TN,),
        in_specs=[pl.BlockSpec((8, D), lambda j: (0, 0)), pl.BlockSpec((D, ADA_TN), lambda j: (0, j)),
                  pl.BlockSpec((1, ADA_TN), lambda j: (0, j))],
        out_specs=pl.BlockSpec((8, ADA_TN), lambda j: (0, j)),
        out_shape=jax.ShapeDtypeStruct((8, n), F32),
        compiler_params=_cparams("parallel"),
    )(c_all, w, b)


def _ada_wgrad(name, c_all_t, dmod):
    n = dmod.shape[1]

    def body(ct_ref, dm_ref, o_ref):
        ca = _silu(ct_ref[...])
        acc = ca[:, 0:1] * dm_ref[0:1, :]
        for r in range(1, 8):
            acc = acc + ca[:, r:r + 1] * dm_ref[r:r + 1, :]
        o_ref[...] = acc

    return pl.pallas_call(
        body, name=name, grid=(n // ADA_TN,),
        in_specs=[pl.BlockSpec((D, 8), lambda j: (0, 0)), pl.BlockSpec((8, ADA_TN), lambda j: (0, j))],
        out_specs=pl.BlockSpec((D, ADA_TN), lambda j: (0, j)),
        out_shape=jax.ShapeDtypeStruct((D, n), F32),
        compiler_params=_cparams("parallel"),
    )(c_all_t, dmod)


def _adamw(name, w, g, m, v, tr):
    rows, cols = w.shape

    def body(w_ref, g_ref, m_ref, v_ref, g_out, d_ref, nm_ref, nv_ref):
        gv = g_ref[...]
        nm = ADAM_B1 * m_ref[...] + (1.0 - ADAM_B1) * gv
        nv = ADAM_B2 * v_ref[...] + (1.0 - ADAM_B2) * (gv * gv)
        m_hat = nm / (1.0 - ADAM_B1 ** ADAM_STEP)
        v_hat = nv / (1.0 - ADAM_B2 ** ADAM_STEP)
        d_ref[...] = -ADAM_LR * (m_hat / (jnp.sqrt(v_hat) + ADAM_EPS) + ADAM_WD * w_ref[...])
        nm_ref[...] = nm
        nv_ref[...] = nv
        g_out[...] = gv

    spec = pl.BlockSpec((tr, cols), lambda i: (i, 0))
    sds = jax.ShapeDtypeStruct((rows, cols), F32)
    return pl.pallas_call(
        body, name=name, grid=(rows // tr,), in_specs=[spec] * 4, out_specs=[spec] * 4, out_shape=[sds] * 4,
        compiler_params=_cparams("parallel"),
    )(w, g, m, v)


def _sum8(name, stacked, tr, after=None):
    n, rows, cols = stacked.shape
    extra = _afters(after)

    def body(s_ref, *rest):
        o_ref = rest[-1]
        acc = s_ref[0]
        for r in range(1, n):
            acc = acc + s_ref[r]
        o_ref[...] = acc

    return pl.pallas_call(
        body, name=name, grid=(rows // tr,),
        in_specs=[pl.BlockSpec((n, tr, cols), lambda i: (0, i, 0))] + [pl.BlockSpec(memory_space=pl.ANY)] * len(extra),
        out_specs=pl.BlockSpec((tr, cols), lambda i: (i, 0)),
        out_shape=jax.ShapeDtypeStruct((rows, cols), F32),
        compiler_params=_cparams("parallel"),
    )(stacked, *extra)


def _pair_add(name, g5, other, c_idx, tr):
    nq, _, rows, cols = g5.shape

    def body(c_ref, g_ref, o_ref, p_ref):
        p_ref[...] = (g_ref[...].astype(F32) + o_ref[...].astype(F32)).astype(p_ref.dtype)

    return pl.pallas_call(
        body, name=name,
        grid_spec=pltpu.PrefetchScalarGridSpec(
            num_scalar_prefetch=1, grid=(nq, rows // tr),
            in_specs=[pl.BlockSpec((None, None, tr, cols), lambda qi, i, cr: (qi, cr[0], i, 0)),
                      pl.BlockSpec((None, tr, cols), lambda qi, i, cr: (qi, i, 0))],
            out_specs=pl.BlockSpec((None, tr, cols), lambda qi, i, cr: (qi, i, 0))),
        out_shape=jax.ShapeDtypeStruct((nq, rows, cols), BF16),
        compiler_params=_cparams("parallel", "parallel"),
    )(c_idx, g5, other)


def _chip_add(name, p, recv, qc_idx, tr):
    _, rows, cols = p.shape

    def body(qc_ref, p_ref, r_ref, o_ref):
        acc = p_ref[...].astype(F32)
        for k in range(NCHIP - 1):
            acc = acc + r_ref[k].astype(F32)
        o_ref[...] = acc

    return pl.pallas_call(
        body, name=name,
        grid_spec=pltpu.PrefetchScalarGridSpec(
            num_scalar_prefetch=1, grid=(rows // tr,),
            in_specs=[pl.BlockSpec((None, tr, cols), lambda i, qc: (qc[0], i, 0)),
                      pl.BlockSpec((NCHIP - 1, tr, cols), lambda i, qc: (0, i, 0))],
            out_specs=pl.BlockSpec((None, tr, cols), lambda i, qc: (qc[1], i, 0))),
        out_shape=jax.ShapeDtypeStruct((2, rows, cols), F32),
        compiler_params=_cparams("parallel"),
    )(qc_idx, p, recv)


def _allgather8(name, blk):
    m_per, n = blk.shape

    def body(x_ref, out_ref, send_sems, recv_sems, local_sem):
        x, y, c = lax.axis_index("x"), lax.axis_index("y"), lax.axis_index("c")
        me, sibling = (x, y, c), (x, y, 1 - c)
        chips = [(1 - x, y), (x, 1 - y), (1 - x, 1 - y)]

        def rows(px, py, pc):
            return out_ref.at[pl.ds((4 * px + 2 * py + pc) * m_per, m_per), :]

        def copy(k, block, to, src=None):
            return pltpu.make_async_remote_copy(
                src_ref=rows(*block) if src is None else src, dst_ref=rows(*block),
                send_sem=send_sems.at[k], recv_sem=recv_sems.at[k], device_id=to, device_id_type=MESH)

        mine = pltpu.make_async_copy(x_ref, rows(*me), local_sem)
        mine.start()
        first = [copy(0, me, sibling, src=x_ref)]
        first += [copy(1 + j, me, (*chip, c), src=x_ref) for j, chip in enumerate(chips)]
        for cp in first:
            cp.start()
        passed = [copy(4 + j, (*chip, c), sibling) for j, chip in enumerate(chips)]
        for j, chip in enumerate(chips):
            copy(1 + j, (*chip, c), me).wait_recv()
            passed[j].start()
        copy(0, sibling, me).wait_recv()
        for j, chip in enumerate(chips):
            copy(4 + j, (*chip, 1 - c), me).wait_recv()
        for cp in first + passed:
            cp.wait_send()
        mine.wait()

    return pl.pallas_call(
        body, name=name,
        out_shape=jax.ShapeDtypeStruct((8 * m_per, n), blk.dtype),
        in_specs=[pl.BlockSpec(memory_space=pltpu.VMEM)],
        out_specs=pl.BlockSpec(memory_space=pltpu.VMEM),
        scratch_shapes=[pltpu.SemaphoreType.DMA((7,)), pltpu.SemaphoreType.DMA((7,)), pltpu.SemaphoreType.DMA],
        compiler_params=pltpu.CompilerParams(vmem_limit_bytes=VMEM_LIMIT),
    )(blk)


def _exchange(name, srcs, out_shapes, n_local, n_remote, plan, aliases=None, after=None):
    ni, no = len(srcs), len(out_shapes)
    extra = [] if after is None else [after]

    def body(*refs):
        ins, outs = refs[:ni], refs[ni + len(extra):ni + len(extra) + no]
        send_sems, recv_sems, local_sems = refs[ni + len(extra) + no:]
        x, y, c = lax.axis_index("x"), lax.axis_index("y"), lax.axis_index("c")
        local, remote = plan(ins, outs, x, y, c)
        assert len(local) == n_local and len(remote) == n_remote

        def rcopy(i, dst):
            s, _, peer, _, _ = remote[i]
            return pltpu.make_async_remote_copy(src_ref=s, dst_ref=dst, send_sem=send_sems.at[i],
                                                recv_sem=recv_sems.at[i], device_id=peer, device_id_type=MESH)

        lcs = [pltpu.make_async_copy(s, d, local_sems.at[i]) for i, (s, d) in enumerate(local)]
        for cp in lcs:
            cp.start()
        first = [i for i in range(n_remote) if remote[i][4] is None]
        passed = [i for i in range(n_remote) if remote[i][4] is not None]
        for i in first:
            rcopy(i, remote[i][1]).start()
        arrived = set()
        for i in passed:
            j = remote[i][4]
            rcopy(j, remote[j][3]).wait_recv()
            arrived.add(j)
            rcopy(i, remote[i][1]).start()
        for i in range(n_remote):
            if i not in arrived:
                rcopy(i, remote[i][3]).wait_recv()
        for i in range(n_remote):
            rcopy(i, remote[i][1]).wait_send()
        for cp in lcs:
            cp.wait()

    any_spec = pl.BlockSpec(memory_space=pl.ANY)
    return pl.pallas_call(
        body, name=name, out_shape=out_shapes,
        in_specs=[any_spec] * (ni + len(extra)), out_specs=[any_spec] * no,
        input_output_aliases=aliases or {},
        scratch_shapes=[pltpu.SemaphoreType.DMA((n_remote,)), pltpu.SemaphoreType.DMA((n_remote,)),
                        pltpu.SemaphoreType.DMA((max(n_local, 1),))],
    )(*srcs, *extra)


_CHIP_FLIPS = ((0, 1), (1, 0), (1, 1))


def _flip(v, f):
    return 1 - v if f else v


EFFECT = pltpu.SideEffectType.DATAFLOW_SIDE_EFFECTING


def _split_start(name, bufs, n, plan, after=None):
    nb = len(bufs)
    extra = [] if after is None else [after]

    def body(*refs):
        ins = refs[:nb]
        send_sems, recv_sems = refs[nb + len(extra)], refs[nb + len(extra) + 1]
        token = refs[-1]
        x, y, c = lax.axis_index("x"), lax.axis_index("y"), lax.axis_index("c")
        copies = plan(ins, x, y, c)
        assert len(copies) == n
        for i, (s, d, peer, _) in enumerate(copies):
            pltpu.make_async_remote_copy(src_ref=s, dst_ref=d, send_sem=send_sems.at[i], recv_sem=recv_sems.at[i],
                                         device_id=peer, device_id_type=MESH).start()
        token[...] = jnp.zeros_like(token)

    hbm = pl.BlockSpec(memory_space=pltpu.HBM)
    sem = pl.BlockSpec(memory_space=pltpu.SEMAPHORE)
    res = pl.pallas_call(
        body, name=name,
        out_shape=(pltpu.SemaphoreType.DMA((n,)), pltpu.SemaphoreType.DMA((n,)),
                   *[pltpu.HBM(b.shape, b.dtype) for b in bufs], jax.ShapeDtypeStruct((8, 128), F32)),
        in_specs=[hbm] * nb + [pl.BlockSpec(memory_space=pl.ANY)] * len(extra),
        out_specs=(sem, sem, *[hbm] * nb, pl.BlockSpec(memory_space=pltpu.VMEM)),
        input_output_aliases={i: 2 + i for i in range(nb)},
        compiler_params=pltpu.CompilerParams(has_side_effects=EFFECT),
    )(*[pltpu.with_memory_space_constraint(b, pltpu.HBM) for b in bufs], *extra)
    return res[0], res[1], list(res[2:2 + nb]), res[-1]


def _split_wait(name, bufs, send_sems, recv_sems, after, n, plan):
    nb = len(bufs)

    def body(*refs):
        ins = refs[:nb]
        ssem, rsem = refs[nb], refs[nb + 1]
        x, y, c = lax.axis_index("x"), lax.axis_index("y"), lax.axis_index("c")
        copies = plan(ins, x, y, c)
        assert len(copies) == n
        for i, (s, _, peer, lands) in enumerate(copies):
            cp = pltpu.make_async_remote_copy(src_ref=s, dst_ref=lands, send_sem=ssem.at[i], recv_sem=rsem.at[i],
                                              device_id=peer, device_id_type=MESH)
            cp.wait_send()
            cp.wait_recv()

    hbm = pl.BlockSpec(memory_space=pltpu.HBM)
    sem = pl.BlockSpec(memory_space=pltpu.SEMAPHORE)
    afters = list(after) if isinstance(after, (list, tuple)) else [after]
    res = pl.pallas_call(
        body, name=name,
        out_shape=[pltpu.HBM(b.shape, b.dtype) for b in bufs],
        in_specs=[hbm] * nb + [sem, sem] + [pl.BlockSpec(memory_space=pl.ANY)] * len(afters),
        out_specs=[hbm] * nb,
        input_output_aliases={i: i for i in range(nb)},
        compiler_params=pltpu.CompilerParams(has_side_effects=EFFECT),
    )(*bufs, send_sems, recv_sems, *afters)
    return list(res)


def _chips_of(x, y):
    return [(_flip(x, fx), _flip(y, fy)) for fx, fy in _CHIP_FLIPS]


def _gather_start(tag, stacks, after):
    n = len(stacks)

    def plan(refs, x, y, c):
        q = 2 * x + y
        return [(refs[a].at[q, c], refs[a].at[q, c], (px, py, c), refs[a].at[2 * px + py, c])
                for a in range(n) for px, py in _chips_of(x, y)]

    send, recv, thru, token = _split_start(f"gather_start_{tag}", stacks, 3 * n, plan, after)
    return (send, recv, thru, plan), token


def _gather_finish(tag, state, after):
    send, recv, thru, plan = state
    n = len(thru)
    landed = _split_wait(f"gather_wait_{tag}", thru, send, recv, after, 3 * n, plan)

    def pass_on(ins, outs, x, y, c):
        sib = (x, y, 1 - c)
        remote = []
        for a in range(n):
            for px, py in _chips_of(x, y):
                p = 2 * px + py
                remote.append((ins[a].at[p, c], outs[a].at[p, c], sib, outs[a].at[p, 1 - c], None))
        return [], remote

    full = _exchange(f"gather_pass_{tag}", landed, [jax.ShapeDtypeStruct(s.shape, s.dtype) for s in landed],
                     0, 3 * n, pass_on, aliases={a: a for a in range(n)})
    return [g.reshape(NCHIP, 2 * g.shape[2], g.shape[3]) for g in full]


def _gather_land(tag, state, after):
    send, recv, thru, plan = state
    n = len(thru)
    landed = _split_wait(f"gather_wait_{tag}", thru, send, recv, after, 3 * n, plan)

    def pass_plan(refs, x, y, c):
        sib = (x, y, 1 - c)
        return [(refs[a].at[2 * px + py, c], refs[a].at[2 * px + py, c], sib, refs[a].at[2 * px + py, 1 - c])
                for a in range(n) for px, py in _chips_of(x, y)]

    send, recv, thru, token = _split_start(f"gather_pass_start_{tag}", landed, 3 * n, pass_plan)
    return (send, recv, thru, pass_plan), token


def _gather_done(tag, state, after):
    send, recv, thru, plan = state
    full = _split_wait(f"gather_pass_wait_{tag}", thru, send, recv, after, 3 * len(thru), plan)
    return [g.reshape(NCHIP, 2 * g.shape[2], g.shape[3]) for g in full]


_OTHERS = tuple((fx, fy, fc) for fx in (0, 1) for fy in (0, 1) for fc in (0, 1) if fx or fy or fc)


def _allgather_start(tag, buf, after=None):
    def plan(refs, x, y, c):
        me = 4 * x + 2 * y + c
        copies = []
        for fx, fy, fc in _OTHERS:
            px, py, pc = _flip(x, fx), _flip(y, fy), _flip(c, fc)
            copies.append((refs[0].at[me], refs[0].at[me], (px, py, pc), refs[0].at[4 * px + 2 * py + pc]))
        return copies

    send, recv, thru, token = _split_start(f"{tag}_start", [buf], len(_OTHERS), plan, after)
    return (send, recv, thru, plan), token


def _allgather_done(tag, state, after):
    send, recv, thru, plan = state
    return _split_wait(f"{tag}_wait", thru, send, recv, after, len(_OTHERS), plan)[0]


def _gather_small(name, smalls):
    n = len(smalls)

    def plan(ins, outs, x, y, c):
        q = 2 * x + y
        local = [(ins[b], outs[b].at[q]) for b in range(n)]
        remote = [(ins[b], outs[b].at[q], (px, py, c), outs[b].at[2 * px + py], None)
                  for b in range(n) for px, py in _chips_of(x, y)]
        return local, remote

    return _exchange(name, smalls, [jax.ShapeDtypeStruct((NCHIP,) + s.shape, s.dtype) for s in smalls], n, 3 * n, plan)


def _rs_pair_start(tag, grads, after=None):
    n = len(grads)
    g5 = [g.reshape(NCHIP, 2, g.shape[1] // 2, g.shape[2]) for g in grads]
    half = [(g.shape[2], g.shape[3]) for g in g5]
    land = [lax.empty((NCHIP,) + h, g.dtype) for h, g in zip(half, grads)]

    def plan(refs, x, y, c):
        sib = (x, y, 1 - c)
        return [(refs[a].at[:, 1 - c], refs[n + a], sib, refs[n + a]) for a in range(n)]

    send, recv, thru, token = _split_start(f"rs_pair_start_{tag}", g5 + land, n, plan, after)
    return (send, recv, thru, plan, half), token


def _rs_chips_start(tag, state, after, c_idx):
    send, recv, thru, pair_plan, half = state
    n = len(half)
    landed = _split_wait(f"rs_pair_wait_{tag}", thru, send, recv, after, n, pair_plan)
    g5, got = landed[:n], landed[n:]
    part = [_pair_add(f"rs_pair_add_{tag}{a}", g5[a], got[a], c_idx, _tile(half[a][0], 512)) for a in range(n)]
    land = [lax.empty((NCHIP - 1,) + h, BF16) for h in half]

    def plan(refs, x, y, c):
        return [(refs[a].at[2 * px + py], refs[n + a].at[k], (px, py, c), refs[n + a].at[k])
                for a in range(n) for k, (px, py) in enumerate(_chips_of(x, y))]

    send, recv, thru, token = _split_start(f"rs_chips_start_{tag}", part + land, 3 * n, plan)
    return (send, recv, thru, plan, half), token


def _rs_finish(tag, state, after, qc_idx):
    send, recv, thru, plan, half = state
    n = len(half)
    landed = _split_wait(f"rs_chips_wait_{tag}", thru, send, recv, after, 3 * n, plan)
    red = [_chip_add(f"rs_chip_add_{tag}{a}", landed[a], landed[n + a], qc_idx, _tile(half[a][0], 512)) for a in range(n)]

    def share(ins, outs, x, y, c):
        sib = (x, y, 1 - c)
        return [], [(ins[a].at[c], outs[a].at[c], sib, outs[a].at[1 - c], None) for a in range(n)]

    full = _exchange(f"rs_share_{tag}", red, [jax.ShapeDtypeStruct((2,) + h, F32) for h in half], 0, n, share,
                     aliases={a: a for a in range(n)})
    return [f.reshape(2 * h[0], h[1]) for f, h in zip(full, half)]


def _rs_land(tag, state, after, qc_idx):
    send, recv, thru, plan, half = state
    n = len(half)
    landed = _split_wait(f"rs_chips_wait_{tag}", thru, send, recv, after, 3 * n, plan)
    red = [_chip_add(f"rs_chip_add_{tag}{a}", landed[a], landed[n + a], qc_idx, _tile(half[a][0], 512)) for a in range(n)]

    def share(refs, x, y, c):
        sib = (x, y, 1 - c)
        return [(refs[a].at[c], refs[a].at[c], sib, refs[a].at[1 - c]) for a in range(n)]

    send, recv, thru, token = _split_start(f"rs_share_start_{tag}", red, n, share)
    return (send, recv, thru, share, half), token


def _rs_done(tag, state, after):
    send, recv, thru, plan, half = state
    full = _split_wait(f"rs_share_wait_{tag}", thru, send, recv, after, len(half), plan)
    return [f.reshape(2 * h[0], h[1]) for f, h in zip(full, half)]


def _place(name, blk, idx):
    rows, cols = blk.shape

    def body(i_ref, b_ref, o_ref):
        o_ref[...] = b_ref[...]

    return pl.pallas_call(
        body, name=name,
        grid_spec=pltpu.PrefetchScalarGridSpec(
            num_scalar_prefetch=1, grid=(1,),
            in_specs=[pl.BlockSpec((rows, cols), lambda i, d: (0, 0))],
            out_specs=pl.BlockSpec((None, rows, cols), lambda i, d: (d[0], 0, 0))),
        out_shape=jax.ShapeDtypeStruct((8, rows, cols), blk.dtype),
        compiler_params=_cparams("arbitrary"),
    )(idx, blk)


def _pack(name, n_rows, pieces, after=None):
    arrays = [p[1] for p in pieces if p[0] != "zeros"]
    extra = [] if after is None else [after]

    def body(*refs):
        o_ref = refs[-1]
        o_ref[...] = jnp.zeros(o_ref.shape, F32)
        row, ai = 0, 0
        for p in pieces:
            kind = p[0]
            if kind == "zeros":
                row += p[1]
                continue
            ref = refs[len(extra) + ai]
            ai += 1
            if kind == "rows":
                r = ref.shape[0]
                o_ref[row:row + r, :] = ref[...]
            elif kind == "wide":
                r = ref.shape[1] // D
                for j in range(r):
                    o_ref[row + j:row + j + 1, :] = ref[:, j * D:(j + 1) * D]
            elif kind == "slice":
                r = p[3] - p[2]
                o_ref[row:row + r, :] = ref[p[2]:p[3], :]
            elif kind == "heads":
                r = CHUNK
                for h in range(HEADS):
                    o_ref[row:row + r, h * CHUNK:(h + 1) * CHUNK] = ref[h]
            else:
                r = 1
                o_ref[row:row + 1, 0:128] = ref[0:1, :]
            row += r
        assert row == n_rows, (row, n_rows)

    vmem = pl.BlockSpec(memory_space=pltpu.VMEM)
    return pl.pallas_call(
        body, name=name, out_shape=jax.ShapeDtypeStruct((n_rows, D), F32),
        in_specs=[pl.BlockSpec(memory_space=pl.ANY)] * len(extra) + [vmem] * len(arrays), out_specs=vmem,
        compiler_params=pltpu.CompilerParams(vmem_limit_bytes=VMEM_LIMIT),
    )(*extra, *arrays)


def _to_bf16_stack(name, w, q_idx):
    r, cols = w.shape
    half = r // 2
    tr = _tile(half, 256)
    nb = half // tr

    def body(q_ref, w_ref, o_ref):
        o_ref[...] = w_ref[...].astype(BF16)

    return pl.pallas_call(
        body, name=name,
        grid_spec=pltpu.PrefetchScalarGridSpec(
            num_scalar_prefetch=1, grid=(2 * nb,),
            in_specs=[pl.BlockSpec((tr, cols), lambda i, qr: (i, 0))],
            out_specs=pl.BlockSpec((None, None, tr, cols), lambda i, qr: (qr[0], i // nb, i % nb, 0))),
        out_shape=jax.ShapeDtypeStruct((NCHIP, 2, half, cols), BF16),
        compiler_params=_cparams("parallel"),
    )(q_idx, w)


def kernel(x, c, ada_w, ada_b, norm_ffn1, ffn1_w_gate, ffn1_w_up, ffn1_w_down, norm_mix, mix_w_in, mix_b_in, sgu_ln_g, sgu_ln_b, sgu_w_s, sgu_b_s, conv_w, conv_b, conv_ln_g, conv_ln_b, w_branch_a, w_branch_b, w_out, norm_ffn2, ffn2_w_gate, ffn2_w_up, ffn2_w_down, norm_final, loss_target, m_ada_w, m_ada_b, m_norm_ffn1, m_ffn1_w_gate, m_ffn1_w_up, m_ffn1_w_down, m_norm_mix, m_mix_w_in, m_mix_b_in, m_sgu_ln_g, m_sgu_ln_b, m_sgu_w_s, m_sgu_b_s, m_conv_w, m_conv_b, m_conv_ln_g, m_conv_ln_b, m_w_branch_a, m_w_branch_b, m_w_out, m_norm_ffn2, m_ffn2_w_gate, m_ffn2_w_up, m_ffn2_w_down, m_norm_final, v_ada_w, v_ada_b, v_norm_ffn1, v_ffn1_w_gate, v_ffn1_w_up, v_ffn1_w_down, v_norm_mix, v_mix_w_in, v_mix_b_in, v_sgu_ln_g, v_sgu_ln_b, v_sgu_w_s, v_sgu_b_s, v_conv_w, v_conv_b, v_conv_ln_g, v_conv_ln_b, v_w_branch_a, v_w_branch_b, v_w_out, v_norm_ffn2, v_ffn2_w_gate, v_ffn2_w_up, v_ffn2_w_down, v_norm_final):
    xi, yi, ci = lax.axis_index("x"), lax.axis_index("y"), lax.axis_index("c")
    q = 2 * xi + yi
    dev = 4 * xi + 2 * yi + ci
    c_idx = jnp.reshape(ci, (1,)).astype(jnp.int32)
    q_idx = jnp.reshape(q, (1,)).astype(jnp.int32)
    x0 = x[0]
    tgt = loss_target[0]
    qc_idx = jnp.stack([q, ci]).astype(jnp.int32)
    n_ada = ada_w.shape[2]

    c_all = _allgather8("ag_c", jnp.pad(c, ((0, 7), (0, 0))))[::8]
    ada_b_mine = lax.dynamic_slice(ada_b, (0, q * n_ada), (1, n_ada))
    mod_part = _ada_fwd("ada_fwd", c_all, ada_w[0], ada_b_mine)
    small_parts = _gather_small("gather_small", [mod_part, jnp.pad(conv_w[0], ((0, 1), (0, 0)))])

    big = dict(g1=ffn1_w_gate, u1=ffn1_w_up, d1=ffn1_w_down, win=mix_w_in, wa=w_branch_a, wb=w_branch_b, wo=w_out,
               g2=ffn2_w_gate, u2=ffn2_w_up, d2=ffn2_w_down)
    groups = dict(f1=("d1", "g1", "u1"), mx=("wo", "wa", "wb", "win"), f2=("d2", "g2", "u2"))
    gather_groups = dict(up1=("g1", "u1"), dn1=("d1",), win=("win",), mx=("wa", "wb", "wo"), up2=("g2", "u2"),
                         dn2=("d2",))
    gather_state = {}
    started = [small_parts[0]]
    for tag, keys in gather_groups.items():
        gather_state[tag], token = _gather_start(
            tag, [_to_bf16_stack(f"cast_{k}", big[k][0], q_idx) for k in keys], started[-1])
        started.append(token)

    mod_all = jnp.transpose(small_parts[0], (1, 0, 2)).reshape(8, N_MOD * D)
    mod = lax.dynamic_slice(mod_all, (dev, 0), (1, N_MOD * D))
    sh1, sc1, g1, sh2, sc2, g2, sh3, sc3 = [(mod, D, k) for k in range(N_MOD - 1)]
    g3 = mod[:, (N_MOD - 1) * D:]
    convw = jnp.transpose(small_parts[1], (1, 0, 2)).reshape(HALO, D)
    W = dict(zip(gather_groups["up1"], _gather_finish("up1", gather_state["up1"], [mod] + started)))

    causal = jnp.tril(jnp.ones((CHUNK, CHUNK), bool))
    wm_f = jnp.where(causal[None], sgu_w_s[0], 0.0)
    wm = wm_f.astype(BF16)
    wmt = jnp.swapaxes(wm_f, 1, 2).astype(BF16)
    bst = jnp.repeat(jnp.transpose(sgu_b_s[0]), CHUNK, axis=1)
    b_in = mix_b_in

    (h1,) = _rowwise_fwd("pre1", _f_pre, [x0], [norm_ffn1, sc1, sh1], [(D, BF16)], 256)
    a1, b1, s1 = _ffn_up("up1", h1, W["g1"], W["u1"])
    W.update(zip(gather_groups["dn1"], _gather_finish("dn1", gather_state["dn1"], s1)))
    f1 = _matmul("down1", s1, W["d1"], "fwd", False, F32)
    pass_win, token = _gather_land("win", gather_state["win"], f1)
    x1, h2 = _rowwise_fwd("res1", _f_res_pre(0.5), [x0, f1], [g1, norm_mix, sc2, sh2],
                          [(D, F32), (D, BF16)], 256, after=token)
    W.update(zip(gather_groups["win"], _gather_done("win", pass_win, x1)))
    proj = _matmul("in_proj", h2, W["win"], "fwd", True, F32)
    pass_mx, token = _gather_land("mx", gather_state["mx"], proj)
    ya = _sgu_fwd("sgu", proj, b_in, sgu_ln_g, sgu_ln_b, wm, bst, 256)
    z1 = _conv_fwd("conv", proj, b_in, convw, conv_b)
    (z3,) = _rowwise_fwd("conv_ln", _f_lnsilu, [z1], [conv_ln_g, conv_ln_b], [(D, BF16)], 256, after=token)
    W.update(zip(gather_groups["mx"], _gather_done("mx", pass_mx, z3)))
    y_a = _matmul("branch_a", ya, W["wa"], "fwd", False, F32)
    y_b = _matmul("branch_b", z3, W["wb"], "fwd", False, F32)
    pass_up2, token = _gather_land("up2", gather_state["up2"], y_b)
    (merged,) = _rowwise_fwd("merge", _f_merge, [(proj, 2 * D, 2), y_a, y_b], [(b_in, 2 * D, 2)],
                             [(D, BF16)], 256, after=token)
    y = _matmul("out_proj", merged, W["wo"], "fwd", False, F32)
    x2, h3 = _rowwise_fwd("res2", _f_res_pre(1.0), [x1, y], [g2, norm_ffn2, sc3, sh3], [(D, F32), (D, BF16)], 256)
    W.update(zip(gather_groups["up2"], _gather_done("up2", pass_up2, x2)))
    a3, b3, s3 = _ffn_up("up2", h3, W["g2"], W["u2"])
    W.update(zip(gather_groups["dn2"], _gather_finish("dn2", gather_state["dn2"], s3)))
    f3 = _matmul("down2", s3, W["d2"], "fwd", False, F32)
    dx2, df3, dg3, dnwf, loss_blk = _final("final", x2, f3, tgt, g3, norm_final.reshape(1, D), 256)

    G = {}

    def ffn_bwd(tag, df, s, a, b, h, wg, wu, wd, after=None):
        da, db = _dgrad_glu(f"d_down{tag}", df, wd, a, b, after)
        G["d" + tag] = _matmul(f"w_down{tag}", s, df, "wgrad", False, BF16)
        dh = _dgrad_pair(f"d_gate_up{tag}", da, db, wg, wu)
        G["g" + tag] = _matmul(f"w_gate{tag}", h, da, "wgrad", True, BF16)
        G["u" + tag] = _matmul(f"w_up{tag}", h, db, "wgrad", True, BF16)
        return dh

    reduced = {}
    dh3 = ffn_bwd("2", df3, s3, a3, b3, h3, W["g2"], W["u2"], W["d2"])
    pair_f2, token = _rs_pair_start("f2", [G[k] for k in groups["f2"]])
    (dx1, dy), (dg2, dnw3, dsc3, dsh3) = _rowwise_bwd(
        "res2_b", _f_res_pre(1.0), [x1, y], [g2, norm_ffn2, sc3, sh3], [dx2, dh3], [F32, BF16], 256, after=token)
    dmerged = _matmul("d_out", dy, W["wo"], "dgrad", False, F32)
    G["wo"] = _matmul("w_out", merged, dy, "wgrad", False, BF16)
    rs_f2, token = _rs_chips_start("f2", pair_f2, [dmerged, G["wo"]], c_idx)
    (dgab, dy_a, dy_b), (dbgab,) = _rowwise_bwd(
        "merge_b", _f_merge, [(proj, 2 * D, 2), y_a, y_b], [(b_in, 2 * D, 2)], [dmerged],
        [BF16, BF16, BF16], 256, after=token)
    dya = _matmul("d_a", dy_a, W["wa"], "dgrad", False, F32)
    dz3 = _matmul("d_b", dy_b, W["wb"], "dgrad", False, F32)
    G["wa"] = _matmul("w_a", ya, dy_a, "wgrad", False, BF16)
    G["wb"] = _matmul("w_b", z3, dy_b, "wgrad", False, BF16)
    (dz1,), (dclg, dclb) = _rowwise_bwd("conv_ln_b", _f_lnsilu, [z1], [conv_ln_g, conv_ln_b], [dz3], [F32], 256)
    dcv, dcg, conv_small = _conv_bwd("conv_b", dz1, proj, b_in, convw)
    duv, dbuv, dlng, dlnb, dws, dbs = _sgu_bwd("sgu_b", proj, b_in, sgu_ln_g, sgu_ln_b, wm, wmt, bst, dya, 256)
    dproj = jnp.concatenate([duv, dcv, dcg, dgab], axis=1)
    dh2 = _matmul("d_in", dproj, W["win"], "dgrad", True, F32)
    G["win"] = _matmul("w_in", h2, dproj, "wgrad", True, BF16)
    pair_mx, token = _rs_pair_start("mx", [G[k] for k in groups["mx"]])
    share_f2, token2 = _rs_land("f2", rs_f2, [dh2, token], qc_idx)

    dev_idx = jnp.reshape(dev, (1,)).astype(jnp.int32)
    early = _pack("pack_early", HALO + CHUNK, [("slice", conv_small, 0, HALO), ("heads", dws)])
    ag_early, token3 = _allgather_start("ag_early", _place("place_early", early, dev_idx), token2)

    (dx0, df1), (dg1, dnw2, dsc2, dsh2) = _rowwise_bwd(
        "res1_b", _f_res_pre(0.5), [x0, f1], [g1, norm_mix, sc2, sh2],
        [dx1, dh2], [F32, BF16], 256, after=[token, token3])
    rs_mx, token = _rs_chips_start("mx", pair_mx, df1, c_idx)
    dh1 = ffn_bwd("1", df1, s1, a1, b1, h1, W["g1"], W["u1"], W["d1"], after=token)
    pair_f1, token = _rs_pair_start("f1", [G[k] for k in groups["f1"]])
    reduced.update(zip(groups["f2"], _rs_done("f2", share_f2, dh1)))
    share_mx, token2 = _rs_land("mx", rs_mx, [dh1, token], qc_idx)
    (grad_x,), (dnw1, dsc1, dsh1) = _rowwise_bwd(
        "pre1_b", _f_pre_keep, [x0], [norm_ffn1, sc1, sh1], [dx0, dh1], [F32], 256, after=[token, token2])
    reduced.update(zip(groups["mx"], _rs_done("mx", share_mx, grad_x)))
    tot_early = _sum8("sum_early", _allgather_done("ag_early", ag_early, grad_x), HALO)

    dbs_row = jnp.transpose(dbs[:, ::CHUNK]).reshape(1, D)
    small = _pack("pack_small", HALO, [
        ("rows", dnw1), ("rows", dnw2), ("wide", dbuv), ("slice", conv_small, 33, 34), ("slice", conv_small, 34, 35),
        ("wide", dbgab), ("rows", dlng), ("rows", dlnb), ("rows", dbs_row), ("slice", conv_small, 32, 33),
        ("rows", dclg), ("rows", dclb), ("rows", dnw3), ("rows", dnwf),
        ("rows", dsh1), ("rows", dsc1), ("rows", dg1), ("rows", dsh2), ("rows", dsc2), ("rows", dg2),
        ("rows", dsh3), ("rows", dsc3), ("rows", dg3), ("lanes", loss_blk), ("zeros", 6)])
    every = _allgather8("ag_small", small)
    rs_f1, token = _rs_chips_start("f1", pair_f1, every, c_idx)
    every = every.reshape(8, HALO, D)
    tot = _sum8("sum_small", every, HALO, after=token)
    loss = tot[25, 0]
    dmod_all = every[:, 16:16 + N_MOD, :].reshape(8, N_MOD * D)
    dmod_mine = lax.dynamic_slice(dmod_all, (0, q * n_ada), (8, n_ada))
    g_ada_w = _ada_wgrad("ada_wgrad", jnp.transpose(c_all), dmod_mine)

    given = dict(ada_w=(ada_w, m_ada_w, v_ada_w), ada_b=(ada_b, m_ada_b, v_ada_b),
                 norm_ffn1=(norm_ffn1, m_norm_ffn1, v_norm_ffn1),
                 ffn1_w_gate=(ffn1_w_gate, m_ffn1_w_gate, v_ffn1_w_gate), ffn1_w_up=(ffn1_w_up, m_ffn1_w_up, v_ffn1_w_up),
                 ffn1_w_down=(ffn1_w_down, m_ffn1_w_down, v_ffn1_w_down), norm_mix=(norm_mix, m_norm_mix, v_norm_mix),
                 mix_w_in=(mix_w_in, m_mix_w_in, v_mix_w_in), mix_b_in=(mix_b_in, m_mix_b_in, v_mix_b_in),
                 sgu_ln_g=(sgu_ln_g, m_sgu_ln_g, v_sgu_ln_g), sgu_ln_b=(sgu_ln_b, m_sgu_ln_b, v_sgu_ln_b),
                 sgu_w_s=(sgu_w_s, m_sgu_w_s, v_sgu_w_s), sgu_b_s=(sgu_b_s, m_sgu_b_s, v_sgu_b_s),
                 conv_w=(conv_w, m_conv_w, v_conv_w), conv_b=(conv_b, m_conv_b, v_conv_b),
                 conv_ln_g=(conv_ln_g, m_conv_ln_g, v_conv_ln_g), conv_ln_b=(conv_ln_b, m_conv_ln_b, v_conv_ln_b),
                 w_branch_a=(w_branch_a, m_w_branch_a, v_w_branch_a), w_branch_b=(w_branch_b, m_w_branch_b, v_w_branch_b),
                 w_out=(w_out, m_w_out, v_w_out), norm_ffn2=(norm_ffn2, m_norm_ffn2, v_norm_ffn2),
                 ffn2_w_gate=(ffn2_w_gate, m_ffn2_w_gate, v_ffn2_w_gate), ffn2_w_up=(ffn2_w_up, m_ffn2_w_up, v_ffn2_w_up),
                 ffn2_w_down=(ffn2_w_down, m_ffn2_w_down, v_ffn2_w_down), norm_final=(norm_final, m_norm_final, v_norm_final))
    out = {}

    def big_update(name, g2d):
        w, m, v = given[name]
        shp = w.shape
        res = _adamw(f"adamw_{name}", w.reshape(g2d.shape), g2d, m.reshape(g2d.shape), v.reshape(g2d.shape),
                     _tile(g2d.shape[0], 512 if g2d.shape[1] <= D else 256))
        out[name] = tuple(t.reshape(shp) for t in res)

    big_names = dict(g1="ffn1_w_gate", u1="ffn1_w_up", d1="ffn1_w_down", win="mix_w_in", wa="w_branch_a",
                     wb="w_branch_b", wo="w_out", g2="ffn2_w_gate", u2="ffn2_w_up", d2="ffn2_w_down")
    for key in groups["f2"] + groups["mx"]:
        big_update(big_names[key], reduced[key])
    big_update("ada_w", g_ada_w)

    small_rows = [("norm_ffn1", 0, 1), ("norm_mix", 1, 1), ("mix_b_in", 2, 6), ("sgu_ln_g", 8, 1), ("sgu_ln_b", 9, 1),
                  ("sgu_b_s", 10, 1), ("conv_b", 11, 1), ("conv_ln_g", 12, 1), ("conv_ln_b", 13, 1),
                  ("norm_ffn2", 14, 1), ("norm_final", 15, 1), ("ada_b", 16, N_MOD)]

    def pack_rows(which, label):
        pieces = []
        for nm, _, r in small_rows:
            t = given[nm][which]
            pieces.append(("wide", t) if r > 1 else ("rows", t.reshape(1, D)))
        return _pack(f"pack_{label}", HALO, pieces + [("zeros", HALO - 16 - N_MOD)])

    small_res = _adamw("adamw_small", pack_rows(0, "w"), tot, pack_rows(1, "m"), pack_rows(2, "v"), HALO)
    for nm, r0, r in small_rows:
        shp = given[nm][0].shape
        out[nm] = tuple(t[r0:r0 + r].reshape(shp) for t in small_res)
    n_cw = conv_w.shape[2]
    g_cw = lax.dynamic_slice(tot_early, (0, q * n_cw), (HALO, n_cw))
    pad_cw = lambda t: jnp.pad(t[0], ((0, 1), (0, 0)))
    cw_res = _adamw("adamw_conv_w", pad_cw(conv_w), g_cw, pad_cw(m_conv_w), pad_cw(v_conv_w), HALO)
    out["conv_w"] = tuple(t[:KCONV][None] for t in cw_res)
    g_ws = jnp.transpose(tot_early[HALO:].reshape(CHUNK, HEADS, CHUNK), (1, 0, 2)).reshape(HEADS * CHUNK, CHUNK)
    flat_ws = lambda t: t.reshape(HEADS * CHUNK, CHUNK)
    ws_res = _adamw("adamw_sgu_w_s", flat_ws(sgu_w_s), g_ws, flat_ws(m_sgu_w_s), flat_ws(v_sgu_w_s), 512)
    out["sgu_w_s"] = tuple(t.reshape(sgu_w_s.shape) for t in ws_res)

    done = [out[n][1] for n in ("ada_w", "sgu_w_s", "conv_w", "ada_b")] + [out[big_names[k]][1] for k in groups["f2"] + groups["mx"]]
    reduced.update(zip(groups["f1"], _rs_finish("f1", rs_f1, done, qc_idx)))
    for key in groups["f1"]:
        big_update(big_names[key], reduced[key])

    order = ["ada_w", "ada_b", "norm_ffn1", "ffn1_w_gate", "ffn1_w_up", "ffn1_w_down", "norm_mix", "mix_w_in", "mix_b_in",
             "sgu_ln_g", "sgu_ln_b", "sgu_w_s", "sgu_b_s", "conv_w", "conv_b", "conv_ln_g", "conv_ln_b", "w_branch_a",
             "w_branch_b", "w_out", "norm_ffn2", "ffn2_w_gate", "ffn2_w_up", "ffn2_w_down", "norm_final"]
    return (loss, grad_x[None], *[out[n][0] for n in order], *[out[n][1] for n in order],
            *[out[n][2] for n in order], *[out[n][3] for n in order])
```

```python
import functools

import jax
import jax.numpy as jnp
from jax import lax
from jax.experimental import pallas as pl
from jax.experimental.pallas import tpu as pltpu

F32 = jnp.float32
BF16 = jnp.bfloat16
D = 1024
HEADS = 8
CHUNK = 128
KCONV = 31
HALO = 32
N_MOD = 9
EPS = 1e-6
NCHIP = 4
VMEM_LIMIT = 56 * 1024 * 1024
MESH = pl.DeviceIdType.MESH

ADAM_LR = 0.001
ADAM_B1 = 0.9
ADAM_B2 = 0.999
ADAM_EPS = 1e-08
ADAM_WD = 0.01
ADAM_STEP = 10

NN = (((1,), (0,)), ((), ()))
NT = (((1,), (1,)), ((), ()))
TN = (((0,), (0,)), ((), ()))


def _cparams(*sem):
    return pltpu.CompilerParams(dimension_semantics=sem or None, vmem_limit_bytes=VMEM_LIMIT)


def _rms(x, w):
    return x * lax.rsqrt(jnp.mean(x * x, axis=-1, keepdims=True) + EPS) * w


def _ln(x, g, b):
    mu = jnp.mean(x, axis=-1, keepdims=True)
    xc = x - mu
    var = jnp.mean(xc * xc, axis=-1, keepdims=True)
    return xc * lax.rsqrt(var + EPS) * g + b


def _silu(x):
    return x * jax.nn.sigmoid(x)


def _gelu(x):
    return x * (0.5 * (1.0 + jnp.tanh(0.7978845608028654 * (x + 0.044715 * (x * x * x)))))


def _f_pre(x, nw, sc, sh):
    return _rms(x, nw) * (1.0 + sc) + sh


def _f_pre_keep(x, nw, sc, sh):
    return x, _f_pre(x, nw, sc, sh)


def _f_res_pre(scale):
    def fn(x, f, g, nw, sc, sh):
        xn = x + (scale * g) * f
        return xn, _f_pre(xn, nw, sc, sh)
    return fn


def _f_glu(a, b):
    return _silu(a) * b


def _f_merge(gab, ya, yb, bgab):
    g = gab + bgab
    return jax.nn.sigmoid(g[:, :D]) * ya + jax.nn.sigmoid(g[:, D:]) * yb


def _f_lnsilu(z, g, b):
    return _silu(_ln(z, g, b))


def _f_mixa(uv, buv, g, b):
    t = uv + buv
    return _gelu(t[:, :D]), _ln(_gelu(t[:, D:]), g, b)


def _col(t):
    return t if isinstance(t, tuple) else (t, t.shape[-1], 0)


def _afters(after):
    if after is None:
        return []
    return list(after) if isinstance(after, (list, tuple)) else [after]


def _rowwise_fwd(name, fn, rows, params, outs, tr, after=None):
    rows = [_col(r) for r in rows]
    params = [_col(p) for p in params]
    extra = _afters(after)
    n_rows = rows[0][0].shape[0]
    nin = len(rows) + len(params)

    def body(*refs):
        res = fn(*[r[...].astype(F32) for r in refs[:nin]])
        res = res if isinstance(res, tuple) else (res,)
        for o, v in zip(refs[nin + len(extra):], res, strict=True):
            o[...] = v.astype(o.dtype)

    return pl.pallas_call(
        body, name=name, grid=(n_rows // tr,),
        in_specs=[pl.BlockSpec((tr, w), functools.partial(lambda cb, i: (i, cb), cb)) for _, w, cb in rows]
        + [pl.BlockSpec((1, w), functools.partial(lambda cb, i: (0, cb), cb)) for _, w, cb in params]
        + [pl.BlockSpec(memory_space=pl.ANY)] * len(extra),
        out_specs=[pl.BlockSpec((tr, w), lambda i: (i, 0)) for w, _ in outs],
        out_shape=[jax.ShapeDtypeStruct((n_rows, w), dt) for w, dt in outs],
        compiler_params=_cparams("parallel"),
    )(*[r[0] for r in rows], *[p[0] for p in params], *extra)


def _rowwise_bwd(name, fn, rows, params, cots, row_grads, tr, after=None):
    rows = [_col(r) for r in rows]
    params = [_col(p) for p in params]
    cots = [_col(t) for t in cots]
    extra = _afters(after)
    n_rows = rows[0][0].shape[0]
    nr, npar, nc = len(rows), len(params), len(cots)
    nin = nr + npar + nc
    n_rg = sum(dt is not None for dt in row_grads)

    def body(*refs):
        i = pl.program_id(0)
        prim = [r[...].astype(F32) for r in refs[:nr + npar]]
        ct = tuple(r[...].astype(F32) for r in refs[nr + npar:nin])
        _, vjp = jax.vjp(fn, *prim)
        g = vjp(ct if nc > 1 else ct[0])
        outs = refs[nin + len(extra):]
        oi = 0
        for j, dt in enumerate(row_grads):
            if dt is not None:
                outs[oi][...] = g[j].astype(dt)
                oi += 1
        for j in range(npar):
            acc = outs[n_rg + j]

            @pl.when(i == 0)
            def _(acc=acc):
                acc[...] = jnp.zeros(acc.shape, F32)

            acc[...] += g[nr + j]

    res = pl.pallas_call(
        body, name=name, grid=(n_rows // tr,),
        in_specs=[pl.BlockSpec((tr, w), functools.partial(lambda cb, i: (i, cb), cb)) for _, w, cb in rows]
        + [pl.BlockSpec((1, w), functools.partial(lambda cb, i: (0, cb), cb)) for _, w, cb in params]
        + [pl.BlockSpec((tr, w), functools.partial(lambda cb, i: (i, cb), cb)) for _, w, cb in cots]
        + [pl.BlockSpec(memory_space=pl.ANY)] * len(extra),
        out_specs=[pl.BlockSpec((tr, rows[j][1]), lambda i: (i, 0)) for j, dt in enumerate(row_grads) if dt is not None]
        + [pl.BlockSpec((1, w), lambda i: (0, 0)) for _, w, _ in params],
        out_shape=[jax.ShapeDtypeStruct((n_rows, rows[j][1]), dt) for j, dt in enumerate(row_grads) if dt is not None]
        + [jax.ShapeDtypeStruct((1, w), F32) for _, w, _ in params],
        compiler_params=_cparams("arbitrary"),
    )(*[r[0] for r in rows], *[p[0] for p in params], *[t[0] for t in cots], *extra)
    return res[:n_rg], res[n_rg:]


def _final(name, x, f, tgt, g, nw, tr):
    n_rows = x.shape[0]

    def body(x_ref, f_ref, t_ref, g_ref, nw_ref, dx_ref, df_ref, dg_ref, dnw_ref, loss_ref):
        i = pl.program_id(0)
        tg = t_ref[...]

        def fn(xv, fv, gv, nwv):
            e = _rms(xv + (0.5 * gv) * fv, nwv) - tg
            return 0.5 * jnp.mean(e * e, axis=-1, keepdims=True)

        per_row, vjp = jax.vjp(fn, x_ref[...], f_ref[...], g_ref[...], nw_ref[...])
        dx, df, dg, dnw = vjp(jnp.ones_like(per_row))
        dx_ref[...] = dx
        df_ref[...] = df.astype(df_ref.dtype)

        @pl.when(i == 0)
        def _():
            dg_ref[...] = jnp.zeros(dg_ref.shape, F32)
            dnw_ref[...] = jnp.zeros(dnw_ref.shape, F32)
            loss_ref[...] = jnp.zeros(loss_ref.shape, F32)

        dg_ref[...] += dg
        dnw_ref[...] += dnw
        loss_ref[...] += jnp.broadcast_to(jnp.sum(per_row, axis=0, keepdims=True), loss_ref.shape)

    row = pl.BlockSpec((tr, D), lambda i: (i, 0))
    par = pl.BlockSpec((1, D), lambda i: (0, 0))
    return pl.pallas_call(
        body, name=name, grid=(n_rows // tr,),
        in_specs=[row, row, row, par, par],
        out_specs=[row, row, par, par, pl.BlockSpec((8, 128), lambda i: (0, 0))],
        out_shape=[jax.ShapeDtypeStruct((n_rows, D), F32), jax.ShapeDtypeStruct((n_rows, D), BF16),
                   jax.ShapeDtypeStruct((1, D), F32), jax.ShapeDtypeStruct((1, D), F32),
                   jax.ShapeDtypeStruct((8, 128), F32)],
        compiler_params=_cparams("arbitrary"),
    )(x, f, tgt, g, nw)


def _tile(dim, pref):
    t = min(dim, pref)
    while dim % t:
        t //= 2
    return t


def _matmul(name, a, b, kind, col, out_dtype, add=None, after=None):
    if kind == "wgrad":
        m, kg = a.shape
        ng = b.shape[1]
        r, c = (kg, ng // NCHIP) if col else (kg // NCHIP, ng)
        tm, tn, tk = _tile(r, 1024), _tile(c, 1536), _tile(m, 2048)
        grid = (kg // tm, ng // tn, m // tk)
        a_spec = pl.BlockSpec((tk, tm), lambda i, j, k: (k, i))
        b_spec = pl.BlockSpec((tk, tn), lambda i, j, k: (k, j))
        if col:
            o_spec = pl.BlockSpec((None, tm, tn), lambda i, j, k: (j // (c // tn), i, j % (c // tn)))
        else:
            o_spec = pl.BlockSpec((None, tm, tn), lambda i, j, k: (i // (r // tm), i % (r // tm), j))
        out_shape = jax.ShapeDtypeStruct((NCHIP, r, c), out_dtype)
        dims = TN
    else:
        _, r, c = b.shape
        m = a.shape[0]
        kg, ng = (r, NCHIP * c) if col else (NCHIP * r, c)
        tm = _tile(m, 1024)
        if kind == "fwd":
            tn, tk = _tile(c, 1536), _tile(r, 2048)
            grid = (m // tm, ng // tn, kg // tk)
            a_spec = pl.BlockSpec((tm, tk), lambda i, j, k: (i, k))
            if col:
                b_spec = pl.BlockSpec((None, tk, tn), lambda i, j, k: (j // (c // tn), k, j % (c // tn)))
            else:
                b_spec = pl.BlockSpec((None, tk, tn), lambda i, j, k: (k // (r // tk), k % (r // tk), j))
            out_shape = jax.ShapeDtypeStruct((m, ng), out_dtype)
            dims = NN
        else:
            tn, tk = _tile(r, 1024), _tile(c, 2048)
            grid = (m // tm, kg // tn, ng // tk)
            a_spec = pl.BlockSpec((tm, tk), lambda i, j, k: (i, k))
            if col:
                b_spec = pl.BlockSpec((None, tn, tk), lambda i, j, k: (k // (c // tk), j, k % (c // tk)))
            else:
                b_spec = pl.BlockSpec((None, tn, tk), lambda i, j, k: (j // (r // tn), j % (r // tn), k))
            out_shape = jax.ShapeDtypeStruct((m, kg), out_dtype)
            dims = NT
        o_spec = pl.BlockSpec((tm, tn), lambda i, j, k: (i, j))
    nk = grid[2]
    has_add = add is not None
    extra = [] if after is None else [after]

    def body(*refs):
        a_ref, b_ref = refs[0], refs[1]
        o_ref = refs[2 + has_add + len(extra)]
        k = pl.program_id(2)
        p = lax.dot_general(a_ref[...].astype(BF16), b_ref[...].astype(BF16), dims, preferred_element_type=F32)

        def finish(r_):
            if has_add:
                r_ = r_ + refs[2][...]
            o_ref[...] = r_.astype(o_ref.dtype)

        if nk == 1:
            finish(p)
        else:
            acc_ref = refs[-1]

            @pl.when(k == 0)
            def _():
                acc_ref[...] = p

            @pl.when(jnp.logical_and(k > 0, k < nk - 1))
            def _():
                acc_ref[...] += p

            @pl.when(k == nk - 1)
            def _():
                finish(acc_ref[...] + p)

    in_specs = [a_spec, b_spec] + ([pl.BlockSpec((tm, tn), lambda i, j, k: (i, j))] if has_add else [])
    in_specs += [pl.BlockSpec(memory_space=pl.ANY)] * len(extra)
    args = (a, b) + ((add,) if has_add else ()) + tuple(extra)
    return pl.pallas_call(
        body, name=name, grid=grid, in_specs=in_specs, out_specs=o_spec, out_shape=out_shape,
        scratch_shapes=[pltpu.VMEM((tm, tn), F32)] if nk > 1 else [],
        compiler_params=_cparams("parallel", "parallel", "arbitrary"),
    )(*args)


def _ffn_up(name, h, wg, wu):
    m, k = h.shape
    _, _, c = wg.shape
    n = NCHIP * c
    tm, tn = _tile(m, 1024), _tile(c, 1024)

    def body(h_ref, wg_ref, wu_ref, a_ref, b_ref, s_ref):
        hv = h_ref[...]
        a = jnp.dot(hv, wg_ref[...], preferred_element_type=F32)
        b = jnp.dot(hv, wu_ref[...], preferred_element_type=F32)
        a_ref[...] = a
        b_ref[...] = b
        s_ref[...] = (_silu(a) * b).astype(s_ref.dtype)

    w_spec = pl.BlockSpec((None, k, tn), lambda i, j: (j // (c // tn), 0, j % (c // tn)))
    o_spec = pl.BlockSpec((tm, tn), lambda i, j: (i, j))
    return pl.pallas_call(
        body, name=name, grid=(m // tm, n // tn),
        in_specs=[pl.BlockSpec((tm, k), lambda i, j: (i, 0)), w_spec, w_spec],
        out_specs=[o_spec, o_spec, o_spec],
        out_shape=[jax.ShapeDtypeStruct((m, n), F32), jax.ShapeDtypeStruct((m, n), F32),
                   jax.ShapeDtypeStruct((m, n), BF16)],
        compiler_params=_cparams("parallel", "parallel"),
    )(h, wg, wu)


def _dgrad_pair(name, da, db, wg, wu):
    m = da.shape[0]
    _, r, c = wg.shape
    tm, tn = _tile(m, 1024), _tile(r, 1024)
    n1 = NCHIP

    def body(da_ref, db_ref, wg_ref, wu_ref, o_ref, acc_ref):
        k = pl.program_id(2)

        @pl.when(k == 0)
        def _():
            acc_ref[...] = lax.dot_general(da_ref[...], wg_ref[...], NT, preferred_element_type=F32)

        @pl.when(jnp.logical_and(k > 0, k < n1))
        def _():
            acc_ref[...] += lax.dot_general(da_ref[...], wg_ref[...], NT, preferred_element_type=F32)

        @pl.when(jnp.logical_and(k >= n1, k < 2 * n1 - 1))
        def _():
            acc_ref[...] += lax.dot_general(db_ref[...], wu_ref[...], NT, preferred_element_type=F32)

        @pl.when(k == 2 * n1 - 1)
        def _():
            o_ref[...] = acc_ref[...] + lax.dot_general(db_ref[...], wu_ref[...], NT, preferred_element_type=F32)

    first = lambda k: jnp.minimum(k, n1 - 1)
    second = lambda k: jnp.maximum(k - n1, 0)
    return pl.pallas_call(
        body, name=name, grid=(m // tm, r // tn, 2 * n1),
        in_specs=[pl.BlockSpec((tm, c), lambda i, j, k: (i, first(k))),
                  pl.BlockSpec((tm, c), lambda i, j, k: (i, second(k))),
                  pl.BlockSpec((None, tn, c), lambda i, j, k: (first(k), j, 0)),
                  pl.BlockSpec((None, tn, c), lambda i, j, k: (second(k), j, 0))],
        out_specs=pl.BlockSpec((tm, tn), lambda i, j, k: (i, j)),
        out_shape=jax.ShapeDtypeStruct((m, r), F32),
        scratch_shapes=[pltpu.VMEM((tm, tn), F32)],
        compiler_params=_cparams("parallel", "parallel", "arbitrary"),
    )(da, db, wg, wu)


def _dgrad_glu(name, df, wd, a, b, after=None):
    m = df.shape[0]
    _, r, c = wd.shape
    tm, tn = _tile(m, 512), _tile(r, 1024)
    extra = [] if after is None else [after]

    def body(df_ref, w_ref, a_ref, b_ref, *rest):
        da_ref, db_ref = rest[-2], rest[-1]
        ds = lax.dot_general(df_ref[...], w_ref[...], NT, preferred_element_type=F32)
        _, vjp = jax.vjp(_f_glu, a_ref[...], b_ref[...])
        da, db = vjp(ds)
        da_ref[...] = da.astype(da_ref.dtype)
        db_ref[...] = db.astype(db_ref.dtype)

    t_spec = pl.BlockSpec((tm, tn), lambda i, j: (i, j))
    sds = jax.ShapeDtypeStruct((m, NCHIP * r), BF16)
    return pl.pallas_call(
        body, name=name, grid=(m // tm, NCHIP * r // tn),
        in_specs=[pl.BlockSpec((tm, c), lambda i, j: (i, 0)),
                  pl.BlockSpec((None, tn, c), lambda i, j: (j // (r // tn), j % (r // tn), 0)), t_spec, t_spec]
        + [pl.BlockSpec(memory_space=pl.ANY)] * len(extra),
        out_specs=[t_spec, t_spec], out_shape=[sds, sds],
        compiler_params=_cparams("parallel", "parallel"),
    )(df, wd, a, b, *extra)


def _sgu_fwd(name, proj, b_in, lng, lnb, wm, bst, tr):
    n_rows = proj.shape[0]

    def body(uv_ref, buv_ref, g_ref, b_ref, wm_ref, bst_ref, ya_ref):
        u, vln = _f_mixa(uv_ref[...], buv_ref[...], g_ref[...], b_ref[...])
        vb = vln.astype(BF16)
        for ck in range(tr // CHUNK):
            rs = slice(ck * CHUNK, (ck + 1) * CHUNK)
            for h in range(HEADS):
                cs = slice(h * CHUNK, (h + 1) * CHUNK)
                vm = jnp.dot(wm_ref[h], vb[rs, cs], preferred_element_type=F32) + bst_ref[:, cs]
                ya_ref[rs, cs] = (u[rs, cs] * vm).astype(ya_ref.dtype)

    return pl.pallas_call(
        body, name=name, grid=(n_rows // tr,),
        in_specs=[pl.BlockSpec((tr, 2 * D), lambda i: (i, 0)), pl.BlockSpec((1, 2 * D), lambda i: (0, 0)),
                  pl.BlockSpec((1, D), lambda i: (0, 0)), pl.BlockSpec((1, D), lambda i: (0, 0)),
                  pl.BlockSpec((HEADS, CHUNK, CHUNK), lambda i: (0, 0, 0)),
                  pl.BlockSpec((CHUNK, D), lambda i: (0, 0))],
        out_specs=pl.BlockSpec((tr, D), lambda i: (i, 0)),
        out_shape=jax.ShapeDtypeStruct((n_rows, D), BF16),
        compiler_params=_cparams("parallel"),
    )(proj, b_in, lng, lnb, wm, bst)


def _sgu_bwd(name, proj, b_in, lng, lnb, wm, wmt, bst, dya, tr):
    n_rows = proj.shape[0]

    def body(uv_ref, buv_ref, g_ref, b_ref, wm_ref, wmt_ref, bst_ref, dya_ref,
             duv_ref, dbuv_ref, dg_ref, db_ref, dws_ref, dbs_ref, du_s, dvln_s):
        i = pl.program_id(0)

        @pl.when(i == 0)
        def _():
            dbuv_ref[...] = jnp.zeros(dbuv_ref.shape, F32)
            dg_ref[...] = jnp.zeros(dg_ref.shape, F32)
            db_ref[...] = jnp.zeros(db_ref.shape, F32)
            dws_ref[...] = jnp.zeros(dws_ref.shape, F32)
            dbs_ref[...] = jnp.zeros(dbs_ref.shape, F32)

        (u, vln), vjp = jax.vjp(_f_mixa, uv_ref[...], buv_ref[...], g_ref[...], b_ref[...])
        vb = vln.astype(BF16)
        dya_v = dya_ref[...].astype(F32)
        tpos = lax.broadcasted_iota(jnp.int32, (CHUNK, CHUNK), 0)
        spos = lax.broadcasted_iota(jnp.int32, (CHUNK, CHUNK), 1)
        causal = (tpos >= spos).astype(F32)
        for ck in range(tr // CHUNK):
            rs = slice(ck * CHUNK, (ck + 1) * CHUNK)
            for h in range(HEADS):
                cs = slice(h * CHUNK, (h + 1) * CHUNK)
                vbh = vb[rs, cs]
                vm = jnp.dot(wm_ref[h], vbh, preferred_element_type=F32) + bst_ref[:, cs]
                dyb = dya_v[rs, cs]
                du_s[rs, cs] = dyb * vm
                dvm = dyb * u[rs, cs]
                dvmb = dvm.astype(BF16)
                dvln_s[rs, cs] = jnp.dot(wmt_ref[h], dvmb, preferred_element_type=F32)
                dws_ref[h] += causal * lax.dot_general(dvmb, vbh, NT, preferred_element_type=F32)
                dbs_ref[:, cs] += jnp.broadcast_to(jnp.sum(dvm, axis=1, keepdims=True), (CHUNK, CHUNK))
        duv, dbuv, dg, db = vjp((du_s[...], dvln_s[...]))
        duv_ref[...] = duv.astype(duv_ref.dtype)
        dbuv_ref[...] += dbuv
        dg_ref[...] += dg
        db_ref[...] += db

    par = pl.BlockSpec((1, D), lambda i: (0, 0))
    par2 = pl.BlockSpec((1, 2 * D), lambda i: (0, 0))
    w_spec = pl.BlockSpec((HEADS, CHUNK, CHUNK), lambda i: (0, 0, 0))
    b_spec = pl.BlockSpec((CHUNK, D), lambda i: (0, 0))
    return pl.pallas_call(
        body, name=name, grid=(n_rows // tr,),
        in_specs=[pl.BlockSpec((tr, 2 * D), lambda i: (i, 0)), par2, par, par, w_spec, w_spec, b_spec,
                  pl.BlockSpec((tr, D), lambda i: (i, 0))],
        out_specs=[pl.BlockSpec((tr, 2 * D), lambda i: (i, 0)), par2, par, par, w_spec, b_spec],
        out_shape=[jax.ShapeDtypeStruct((n_rows, 2 * D), BF16), jax.ShapeDtypeStruct((1, 2 * D), F32),
                   jax.ShapeDtypeStruct((1, D), F32), jax.ShapeDtypeStruct((1, D), F32),
                   jax.ShapeDtypeStruct((HEADS, CHUNK, CHUNK), F32), jax.ShapeDtypeStruct((CHUNK, D), F32)],
        scratch_shapes=[pltpu.VMEM((tr, D), F32), pltpu.VMEM((tr, D), F32)],
        compiler_params=_cparams("arbitrary"),
    )(proj, b_in, lng, lnb, wm, wmt, bst, dya)


CT = 256
RB = 128
CV0 = 2 * D // CT
CG0 = 3 * D // CT


def _shift_bank(win):
    return [win] + [pltpu.roll(win, RB + HALO - r, 0) for r in range(1, 8)]


def _shifted(bank, sh):
    lo = 8 * (sh // 8)
    return bank[sh % 8][lo:lo + RB]


def _conv_fwd(name, proj, b_in, w, cb):
    n_rows = proj.shape[0]

    def body(cv_ref, cg_ref, bcv_ref, bcg_ref, w_ref, cb_ref, z1_ref, zp_ref):
        zp_ref[0:HALO, :] = jnp.zeros((HALO, CT), F32)
        zp_ref[HALO:, :] = (cv_ref[...] + bcv_ref[...]) * jax.nn.sigmoid(cg_ref[...] + bcg_ref[...])

        def blk(rb, carry):
            base = pl.multiple_of(rb * RB, RB)
            bank = _shift_bank(zp_ref[pl.ds(base, RB + HALO), :])
            acc = jnp.broadcast_to(cb_ref[...], (RB, CT))
            for k in range(KCONV):
                acc = acc + w_ref[k:k + 1, :] * _shifted(bank, k + 2)
            z1_ref[pl.ds(base, RB), :] = acc
            return carry

        lax.fori_loop(0, n_rows // RB, blk, 0)

    return pl.pallas_call(
        body, name=name, grid=(D // CT,),
        in_specs=[pl.BlockSpec((n_rows, CT), lambda j: (0, CV0 + j)), pl.BlockSpec((n_rows, CT), lambda j: (0, CG0 + j)),
                  pl.BlockSpec((1, CT), lambda j: (0, CV0 + j)), pl.BlockSpec((1, CT), lambda j: (0, CG0 + j)),
                  pl.BlockSpec((HALO, CT), lambda j: (0, j)), pl.BlockSpec((1, CT), lambda j: (0, j))],
        out_specs=pl.BlockSpec((n_rows, CT), lambda j: (0, j)),
        out_shape=jax.ShapeDtypeStruct((n_rows, D), F32),
        scratch_shapes=[pltpu.VMEM((n_rows + HALO, CT), F32)],
        compiler_params=_cparams("parallel"),
    )(proj, proj, b_in, b_in, w, cb)


def _conv_bwd(name, dz1, proj, b_in, w):
    n_rows = proj.shape[0]

    def body(dz_ref, cv_ref, cg_ref, bcv_ref, bcg_ref, w_ref, dcv_ref, dcg_ref, sm_ref, zp_ref, dzp_ref):
        cvb = cv_ref[...] + bcv_ref[...]
        sg = jax.nn.sigmoid(cg_ref[...] + bcg_ref[...])
        zp_ref[0:HALO, :] = jnp.zeros((HALO, CT), F32)
        zp_ref[HALO:, :] = cvb * sg
        dz = dz_ref[...]
        dzp_ref[0:n_rows, :] = dz
        dzp_ref[n_rows:, :] = jnp.zeros((HALO, CT), F32)
        sm_ref[...] = jnp.zeros(sm_ref.shape, F32)
        sm_ref[32:33, :] = jnp.sum(dz, axis=0, keepdims=True)

        def blk(rb, carry):
            base = pl.multiple_of(rb * RB, RB)
            dbank = _shift_bank(dzp_ref[pl.ds(base, RB + HALO), :])
            zbank = _shift_bank(zp_ref[pl.ds(base, RB + HALO), :])
            dzb = dbank[0][0:RB]
            acc = jnp.zeros((RB, CT), F32)
            for k in range(KCONV):
                acc = acc + w_ref[k:k + 1, :] * _shifted(dbank, KCONV - 1 - k)
                sm_ref[k:k + 1, :] += jnp.sum(dzb * _shifted(zbank, k + 2), axis=0, keepdims=True)
            dzp_ref[pl.ds(base, RB), :] = acc
            return carry

        lax.fori_loop(0, n_rows // RB, blk, 0)
        dz0 = dzp_ref[0:n_rows, :]
        dcv = dz0 * sg
        dcg = dz0 * cvb * (sg * (1.0 - sg))
        dcv_ref[...] = dcv.astype(dcv_ref.dtype)
        dcg_ref[...] = dcg.astype(dcg_ref.dtype)
        sm_ref[33:34, :] = jnp.sum(dcv, axis=0, keepdims=True)
        sm_ref[34:35, :] = jnp.sum(dcg, axis=0, keepdims=True)

    col = pl.BlockSpec((n_rows, CT), lambda j: (0, j))
    return pl.pallas_call(
        body, name=name, grid=(D // CT,),
        in_specs=[col, pl.BlockSpec((n_rows, CT), lambda j: (0, CV0 + j)), pl.BlockSpec((n_rows, CT), lambda j: (0, CG0 + j)),
                  pl.BlockSpec((1, CT), lambda j: (0, CV0 + j)), pl.BlockSpec((1, CT), lambda j: (0, CG0 + j)),
                  pl.BlockSpec((HALO, CT), lambda j: (0, j))],
        out_specs=[col, col, pl.BlockSpec((40, CT), lambda j: (0, j))],
        out_shape=[jax.ShapeDtypeStruct((n_rows, D), BF16), jax.ShapeDtypeStruct((n_rows, D), BF16),
                   jax.ShapeDtypeStruct((40, D), F32)],
        scratch_shapes=[pltpu.VMEM((n_rows + HALO, CT), F32), pltpu.VMEM((n_rows + HALO, CT), F32)],
        compiler_params=_cparams("parallel"),
    )(dz1, proj, proj, b_in, b_in, w)


ADA_TN = 768


def _split_bf16(v):
    hi = v.astype(BF16)
    return hi, (v - hi.astype(F32)).astype(BF16)


def _ada_fwd(name, c_all, w, b):
    n = w.shape[1]

    def body(c_ref, w_ref, b_ref, o_ref):
        ch, cl = _split_bf16(_silu(c_ref[...]))
        wh, wl = _split_bf16(w_ref[...])
        acc = jnp.dot(ch, wl, preferred_element_type=F32) + jnp.dot(cl, wh, preferred_element_type=F32)
        o_ref[...] = acc + jnp.dot(ch, wh, preferred_element_type=F32) + b_ref[...]

    return pl.pallas_call(
        body, name=name, grid=(n // ADA_TN,),
        in_specs=[pl.BlockSpec((8, D), lambda j: (0, 0)), pl.BlockSpec((D, ADA_TN), lambda j: (0, j)),
                  pl.BlockSpec((1, ADA_TN), lambda j: (0, j))],
        out_specs=pl.BlockSpec((8, ADA_TN), lambda j: (0, j)),
        out_shape=jax.ShapeDtypeStruct((8, n), F32),
        compiler_params=_cparams("parallel"),
    )(c_all, w, b)


def _ada_wgrad(name, c_all_t, dmod):
    n = dmod.shape[1]

    def body(ct_ref, dm_ref, o_ref):
        ca = _silu(ct_ref[...])
        acc = ca[:, 0:1] * dm_ref[0:1, :]
        for r in range(1, 8):
            acc = acc + ca[:, r:r + 1] * dm_ref[r:r + 1, :]
        o_ref[...] = acc

    return pl.pallas_call(
        body, name=name, grid=(n // ADA_TN,),
        in_specs=[pl.BlockSpec((D, 8), lambda j: (0, 0)), pl.BlockSpec((8, ADA_TN), lambda j: (0, j))],
        out_specs=pl.BlockSpec((D, ADA_TN), lambda j: (0, j)),
        out_shape=jax.ShapeDtypeStruct((D, n), F32),
        compiler_params=_cparams("parallel"),
    )(c_all_t, dmod)


def _adamw(name, w, g, m, v, tr):
    rows, cols = w.shape

    def body(w_ref, g_ref, m_ref, v_ref, g_out, d_ref, nm_ref, nv_ref):
        gv = g_ref[...]
        nm = ADAM_B1 * m_ref[...] + (1.0 - ADAM_B1) * gv
        nv = ADAM_B2 * v_ref[...] + (1.0 - ADAM_B2) * (gv * gv)
        m_hat = nm / (1.0 - ADAM_B1 ** ADAM_STEP)
        v_hat = nv / (1.0 - ADAM_B2 ** ADAM_STEP)
        d_ref[...] = -ADAM_LR * (m_hat / (jnp.sqrt(v_hat) + ADAM_EPS) + ADAM_WD * w_ref[...])
        nm_ref[...] = nm
        nv_ref[...] = nv
        g_out[...] = gv

    spec = pl.BlockSpec((tr, cols), lambda i: (i, 0))
    sds = jax.ShapeDtypeStruct((rows, cols), F32)
    return pl.pallas_call(
        body, name=name, grid=(rows // tr,), in_specs=[spec] * 4, out_specs=[spec] * 4, out_shape=[sds] * 4,
        compiler_params=_cparams("parallel"),
    )(w, g, m, v)


def _sum8(name, stacked, tr, after=None):
    n, rows, cols = stacked.shape
    extra = _afters(after)

    def body(s_ref, *rest):
        o_ref = rest[-1]
        acc = s_ref[0]
        for r in range(1, n):
            acc = acc + s_ref[r]
        o_ref[...] = acc

    return pl.pallas_call(
        body, name=name, grid=(rows // tr,),
        in_specs=[pl.BlockSpec((n, tr, cols), lambda i: (0, i, 0))] + [pl.BlockSpec(memory_space=pl.ANY)] * len(extra),
        out_specs=pl.BlockSpec((tr, cols), lambda i: (i, 0)),
        out_shape=jax.ShapeDtypeStruct((rows, cols), F32),
        compiler_params=_cparams("parallel"),
    )(stacked, *extra)


def _pair_add(name, g5, other, c_idx, tr):
    nq, _, rows, cols = g5.shape

    def body(c_ref, g_ref, o_ref, p_ref):
        p_ref[...] = (g_ref[...].astype(F32) + o_ref[...].astype(F32)).astype(p_ref.dtype)

    return pl.pallas_call(
        body, name=name,
        grid_spec=pltpu.PrefetchScalarGridSpec(
            num_scalar_prefetch=1, grid=(nq, rows // tr),
            in_specs=[pl.BlockSpec((None, None, tr, cols), lambda qi, i, cr: (qi, cr[0], i, 0)),
                      pl.BlockSpec((None, tr, cols), lambda qi, i, cr: (qi, i, 0))],
            out_specs=pl.BlockSpec((None, tr, cols), lambda qi, i, cr: (qi, i, 0))),
        out_shape=jax.ShapeDtypeStruct((nq, rows, cols), BF16),
        compiler_params=_cparams("parallel", "parallel"),
    )(c_idx, g5, other)


def _chip_add(name, p, recv, qc_idx, tr):
    _, rows, cols = p.shape

    def body(qc_ref, p_ref, r_ref, o_ref):
        acc = p_ref[...].astype(F32)
        for k in range(NCHIP - 1):
            acc = acc + r_ref[k].astype(F32)
        o_ref[...] = acc

    return pl.pallas_call(
        body, name=name,
        grid_spec=pltpu.PrefetchScalarGridSpec(
            num_scalar_prefetch=1, grid=(rows // tr,),
            in_specs=[pl.BlockSpec((None, tr, cols), lambda i, qc: (qc[0], i, 0)),
                      pl.BlockSpec((NCHIP - 1, tr, cols), lambda i, qc: (0, i, 0))],
            out_specs=pl.BlockSpec((None, tr, cols), lambda i, qc: (qc[1], i, 0))),
        out_shape=jax.ShapeDtypeStruct((2, rows, cols), F32),
        compiler_params=_cparams("parallel"),
    )(qc_idx, p, recv)


def _allgather8(name, blk, after=None):
    m_per, n = blk.shape
    extra = _afters(after)

    def body(x_ref, *rest):
        out_ref, send_sems, recv_sems, local_sem = rest[len(extra):]
        x, y, c = lax.axis_index("x"), lax.axis_index("y"), lax.axis_index("c")
        me, sibling = (x, y, c), (x, y, 1 - c)
        chips = [(1 - x, y), (x, 1 - y), (1 - x, 1 - y)]

        def rows(px, py, pc):
            return out_ref.at[pl.ds((4 * px + 2 * py + pc) * m_per, m_per), :]

        def copy(k, block, to, src=None):
            return pltpu.make_async_remote_copy(
                src_ref=rows(*block) if src is None else src, dst_ref=rows(*block),
                send_sem=send_sems.at[k], recv_sem=recv_sems.at[k], device_id=to, device_id_type=MESH)

        mine = pltpu.make_async_copy(x_ref, rows(*me), local_sem)
        mine.start()
        first = [copy(0, me, sibling, src=x_ref)]
        first += [copy(1 + j, me, (*chip, c), src=x_ref) for j, chip in enumerate(chips)]
        for cp in first:
            cp.start()
        passed = [copy(4 + j, (*chip, c), sibling) for j, chip in enumerate(chips)]
        for j, chip in enumerate(chips):
            copy(1 + j, (*chip, c), me).wait_recv()
            passed[j].start()
        copy(0, sibling, me).wait_recv()
        for j, chip in enumerate(chips):
            copy(4 + j, (*chip, 1 - c), me).wait_recv()
        for cp in first + passed:
            cp.wait_send()
        mine.wait()

    return pl.pallas_call(
        body, name=name,
        out_shape=jax.ShapeDtypeStruct((8 * m_per, n), blk.dtype),
        in_specs=[pl.BlockSpec(memory_space=pltpu.VMEM)] + [pl.BlockSpec(memory_space=pl.ANY)] * len(extra),
        out_specs=pl.BlockSpec(memory_space=pltpu.VMEM),
        scratch_shapes=[pltpu.SemaphoreType.DMA((7,)), pltpu.SemaphoreType.DMA((7,)), pltpu.SemaphoreType.DMA],
        compiler_params=pltpu.CompilerParams(vmem_limit_bytes=VMEM_LIMIT),
    )(blk, *extra)


def _exchange(name, srcs, out_shapes, n_local, n_remote, plan, aliases=None, after=None):
    ni, no = len(srcs), len(out_shapes)
    extra = [] if after is None else [after]

    def body(*refs):
        ins, outs = refs[:ni], refs[ni + len(extra):ni + len(extra) + no]
        send_sems, recv_sems, local_sems = refs[ni + len(extra) + no:]
        x, y, c = lax.axis_index("x"), lax.axis_index("y"), lax.axis_index("c")
        local, remote = plan(ins, outs, x, y, c)
        assert len(local) == n_local and len(remote) == n_remote

        def rcopy(i, dst):
            s, _, peer, _, _ = remote[i]
            return pltpu.make_async_remote_copy(src_ref=s, dst_ref=dst, send_sem=send_sems.at[i],
                                                recv_sem=recv_sems.at[i], device_id=peer, device_id_type=MESH)

        lcs = [pltpu.make_async_copy(s, d, local_sems.at[i]) for i, (s, d) in enumerate(local)]
        for cp in lcs:
            cp.start()
        first = [i for i in range(n_remote) if remote[i][4] is None]
        passed = [i for i in range(n_remote) if remote[i][4] is not None]
        for i in first:
            rcopy(i, remote[i][1]).start()
        arrived = set()
        for i in passed:
            j = remote[i][4]
            rcopy(j, remote[j][3]).wait_recv()
            arrived.add(j)
            rcopy(i, remote[i][1]).start()
        for i in range(n_remote):
            if i not in arrived:
                rcopy(i, remote[i][3]).wait_recv()
        for i in range(n_remote):
            rcopy(i, remote[i][1]).wait_send()
        for cp in lcs:
            cp.wait()

    any_spec = pl.BlockSpec(memory_space=pl.ANY)
    return pl.pallas_call(
        body, name=name, out_shape=out_shapes,
        in_specs=[any_spec] * (ni + len(extra)), out_specs=[any_spec] * no,
        input_output_aliases=aliases or {},
        scratch_shapes=[pltpu.SemaphoreType.DMA((n_remote,)), pltpu.SemaphoreType.DMA((n_remote,)),
                        pltpu.SemaphoreType.DMA((max(n_local, 1),))],
    )(*srcs, *extra)


_CHIP_FLIPS = ((0, 1), (1, 0), (1, 1))


def _flip(v, f):
    return 1 - v if f else v


EFFECT = pltpu.SideEffectType.DATAFLOW_SIDE_EFFECTING


def _split_start(name, bufs, n, plan, after=None):
    nb = len(bufs)
    extra = [] if after is None else [after]

    def body(*refs):
        ins = refs[:nb]
        send_sems, recv_sems = refs[nb + len(extra)], refs[nb + len(extra) + 1]
        token = refs[-1]
        x, y, c = lax.axis_index("x"), lax.axis_index("y"), lax.axis_index("c")
        copies = plan(ins, x, y, c)
        assert len(copies) == n
        for i, (s, d, peer, _) in enumerate(copies):
            pltpu.make_async_remote_copy(src_ref=s, dst_ref=d, send_sem=send_sems.at[i], recv_sem=recv_sems.at[i],
                                         device_id=peer, device_id_type=MESH).start()
        token[...] = jnp.zeros_like(token)

    hbm = pl.BlockSpec(memory_space=pltpu.HBM)
    sem = pl.BlockSpec(memory_space=pltpu.SEMAPHORE)
    res = pl.pallas_call(
        body, name=name,
        out_shape=(pltpu.SemaphoreType.DMA((n,)), pltpu.SemaphoreType.DMA((n,)),
                   *[pltpu.HBM(b.shape, b.dtype) for b in bufs], jax.ShapeDtypeStruct((8, 128), F32)),
        in_specs=[hbm] * nb + [pl.BlockSpec(memory_space=pl.ANY)] * len(extra),
        out_specs=(sem, sem, *[hbm] * nb, pl.BlockSpec(memory_space=pltpu.VMEM)),
        input_output_aliases={i: 2 + i for i in range(nb)},
        compiler_params=pltpu.CompilerParams(has_side_effects=EFFECT),
    )(*[pltpu.with_memory_space_constraint(b, pltpu.HBM) for b in bufs], *extra)
    return res[0], res[1], list(res[2:2 + nb]), res[-1]


def _split_wait(name, bufs, send_sems, recv_sems, after, n, plan):
    nb = len(bufs)

    def body(*refs):
        ins = refs[:nb]
        ssem, rsem = refs[nb], refs[nb + 1]
        x, y, c = lax.axis_index("x"), lax.axis_index("y"), lax.axis_index("c")
        copies = plan(ins, x, y, c)
        assert len(copies) == n
        for i, (s, _, peer, lands) in enumerate(copies):
            cp = pltpu.make_async_remote_copy(src_ref=s, dst_ref=lands, send_sem=ssem.at[i], recv_sem=rsem.at[i],
                                              device_id=peer, device_id_type=MESH)
            cp.wait_send()
            cp.wait_recv()

    hbm = pl.BlockSpec(memory_space=pltpu.HBM)
    sem = pl.BlockSpec(memory_space=pltpu.SEMAPHORE)
    afters = list(after) if isinstance(after, (list, tuple)) else [after]
    res = pl.pallas_call(
        body, name=name,
        out_shape=[pltpu.HBM(b.shape, b.dtype) for b in bufs],
        in_specs=[hbm] * nb + [sem, sem] + [pl.BlockSpec(memory_space=pl.ANY)] * len(afters),
        out_specs=[hbm] * nb,
        input_output_aliases={i: i for i in range(nb)},
        compiler_params=pltpu.CompilerParams(has_side_effects=EFFECT),
    )(*bufs, send_sems, recv_sems, *afters)
    return list(res)


def _chips_of(x, y):
    return [(_flip(x, fx), _flip(y, fy)) for fx, fy in _CHIP_FLIPS]


def _gather_start(tag, stacks, after):
    n = len(stacks)

    def plan(refs, x, y, c):
        q = 2 * x + y
        return [(refs[a].at[q, c], refs[a].at[q, c], (px, py, c), refs[a].at[2 * px + py, c])
                for a in range(n) for px, py in _chips_of(x, y)]

    send, recv, thru, token = _split_start(f"gather_start_{tag}", stacks, 3 * n, plan, after)
    return (send, recv, thru, plan), token


def _gather_finish(tag, state, after):
    send, recv, thru, plan = state
    n = len(thru)
    landed = _split_wait(f"gather_wait_{tag}", thru, send, recv, after, 3 * n, plan)

    def pass_on(ins, outs, x, y, c):
        sib = (x, y, 1 - c)
        remote = []
        for a in range(n):
            for px, py in _chips_of(x, y):
                p = 2 * px + py
                remote.append((ins[a].at[p, c], outs[a].at[p, c], sib, outs[a].at[p, 1 - c], None))
        return [], remote

    full = _exchange(f"gather_pass_{tag}", landed, [jax.ShapeDtypeStruct(s.shape, s.dtype) for s in landed],
                     0, 3 * n, pass_on, aliases={a: a for a in range(n)})
    return [g.reshape(NCHIP, 2 * g.shape[2], g.shape[3]) for g in full]


def _gather_land(tag, state, after):
    send, recv, thru, plan = state
    n = len(thru)
    landed = _split_wait(f"gather_wait_{tag}", thru, send, recv, after, 3 * n, plan)

    def pass_plan(refs, x, y, c):
        sib = (x, y, 1 - c)
        return [(refs[a].at[2 * px + py, c], refs[a].at[2 * px + py, c], sib, refs[a].at[2 * px + py, 1 - c])
                for a in range(n) for px, py in _chips_of(x, y)]

    send, recv, thru, token = _split_start(f"gather_pass_start_{tag}", landed, 3 * n, pass_plan)
    return (send, recv, thru, pass_plan), token


def _gather_done(tag, state, after):
    send, recv, thru, plan = state
    full = _split_wait(f"gather_pass_wait_{tag}", thru, send, recv, after, 3 * len(thru), plan)
    return [g.reshape(NCHIP, 2 * g.shape[2], g.shape[3]) for g in full]


_OTHERS = tuple((fx, fy, fc) for fx in (0, 1) for fy in (0, 1) for fc in (0, 1) if fx or fy or fc)


def _allgather_start(tag, buf, after=None):
    def plan(refs, x, y, c):
        me = 4 * x + 2 * y + c
        copies = []
        for fx, fy, fc in _OTHERS:
            px, py, pc = _flip(x, fx), _flip(y, fy), _flip(c, fc)
            copies.append((refs[0].at[me], refs[0].at[me], (px, py, pc), refs[0].at[4 * px + 2 * py + pc]))
        return copies

    send, recv, thru, token = _split_start(f"{tag}_start", [buf], len(_OTHERS), plan, after)
    return (send, recv, thru, plan), token


def _allgather_done(tag, state, after):
    send, recv, thru, plan = state
    return _split_wait(f"{tag}_wait", thru, send, recv, after, len(_OTHERS), plan)[0]


def _gather_small(name, smalls):
    n = len(smalls)

    def plan(ins, outs, x, y, c):
        q = 2 * x + y
        local = [(ins[b], outs[b].at[q]) for b in range(n)]
        remote = [(ins[b], outs[b].at[q], (px, py, c), outs[b].at[2 * px + py], None)
                  for b in range(n) for px, py in _chips_of(x, y)]
        return local, remote

    return _exchange(name, smalls, [jax.ShapeDtypeStruct((NCHIP,) + s.shape, s.dtype) for s in smalls], n, 3 * n, plan)


def _rs_pair_start(tag, grads, after=None):
    n = len(grads)
    g5 = [g.reshape(NCHIP, 2, g.shape[1] // 2, g.shape[2]) for g in grads]
    half = [(g.shape[2], g.shape[3]) for g in g5]
    land = [lax.empty((NCHIP,) + h, g.dtype) for h, g in zip(half, grads)]

    def plan(refs, x, y, c):
        sib = (x, y, 1 - c)
        return [(refs[a].at[:, 1 - c], refs[n + a], sib, refs[n + a]) for a in range(n)]

    send, recv, thru, token = _split_start(f"rs_pair_start_{tag}", g5 + land, n, plan, after)
    return (send, recv, thru, plan, half), token


def _rs_chips_start(tag, state, after, c_idx):
    send, recv, thru, pair_plan, half = state
    n = len(half)
    landed = _split_wait(f"rs_pair_wait_{tag}", thru, send, recv, after, n, pair_plan)
    g5, got = landed[:n], landed[n:]
    part = [_pair_add(f"rs_pair_add_{tag}{a}", g5[a], got[a], c_idx, _tile(half[a][0], 512)) for a in range(n)]
    land = [lax.empty((NCHIP - 1,) + h, BF16) for h in half]

    def plan(refs, x, y, c):
        return [(refs[a].at[2 * px + py], refs[n + a].at[k], (px, py, c), refs[n + a].at[k])
                for a in range(n) for k, (px, py) in enumerate(_chips_of(x, y))]

    send, recv, thru, token = _split_start(f"rs_chips_start_{tag}", part + land, 3 * n, plan)
    return (send, recv, thru, plan, half), token


def _rs_finish(tag, state, after, qc_idx):
    send, recv, thru, plan, half = state
    n = len(half)
    landed = _split_wait(f"rs_chips_wait_{tag}", thru, send, recv, after, 3 * n, plan)
    red = [_chip_add(f"rs_chip_add_{tag}{a}", landed[a], landed[n + a], qc_idx, _tile(half[a][0], 512)) for a in range(n)]

    def share(ins, outs, x, y, c):
        sib = (x, y, 1 - c)
        return [], [(ins[a].at[c], outs[a].at[c], sib, outs[a].at[1 - c], None) for a in range(n)]

    full = _exchange(f"rs_share_{tag}", red, [jax.ShapeDtypeStruct((2,) + h, F32) for h in half], 0, n, share,
                     aliases={a: a for a in range(n)})
    return [f.reshape(2 * h[0], h[1]) for f, h in zip(full, half)]


def _rs_land(tag, state, after, qc_idx):
    send, recv, thru, plan, half = state
    n = len(half)
    landed = _split_wait(f"rs_chips_wait_{tag}", thru, send, recv, after, 3 * n, plan)
    red = [_chip_add(f"rs_chip_add_{tag}{a}", landed[a], landed[n + a], qc_idx, _tile(half[a][0], 512)) for a in range(n)]

    def share(refs, x, y, c):
        sib = (x, y, 1 - c)
        return [(refs[a].at[c], refs[a].at[c], sib, refs[a].at[1 - c]) for a in range(n)]

    send, recv, thru, token = _split_start(f"rs_share_start_{tag}", red, n, share)
    return (send, recv, thru, share, half), token


def _rs_done(tag, state, after):
    send, recv, thru, plan, half = state
    full = _split_wait(f"rs_share_wait_{tag}", thru, send, recv, after, len(half), plan)
    return [f.reshape(2 * h[0], h[1]) for f, h in zip(full, half)]


def _place(name, blk, idx):
    rows, cols = blk.shape

    def body(i_ref, b_ref, o_ref):
        o_ref[...] = b_ref[...]

    return pl.pallas_call(
        body, name=name,
        grid_spec=pltpu.PrefetchScalarGridSpec(
            num_scalar_prefetch=1, grid=(1,),
            in_specs=[pl.BlockSpec((rows, cols), lambda i, d: (0, 0))],
            out_specs=pl.BlockSpec((None, rows, cols), lambda i, d: (d[0], 0, 0))),
        out_shape=jax.ShapeDtypeStruct((8, rows, cols), blk.dtype),
        compiler_params=_cparams("arbitrary"),
    )(idx, blk)


def _pack(name, n_rows, pieces, after=None):
    arrays = [p[1] for p in pieces if p[0] != "zeros"]
    extra = [] if after is None else [after]

    def body(*refs):
        o_ref = refs[-1]
        o_ref[...] = jnp.zeros(o_ref.shape, F32)
        row, ai = 0, 0
        for p in pieces:
            kind = p[0]
            if kind == "zeros":
                row += p[1]
                continue
            ref = refs[len(extra) + ai]
            ai += 1
            if kind == "rows":
                r = ref.shape[0]
                o_ref[row:row + r, :] = ref[...]
            elif kind == "wide":
                r = ref.shape[1] // D
                for j in range(r):
                    o_ref[row + j:row + j + 1, :] = ref[:, j * D:(j + 1) * D]
            elif kind == "slice":
                r = p[3] - p[2]
                o_ref[row:row + r, :] = ref[p[2]:p[3], :]
            elif kind == "heads":
                r = CHUNK
                for h in range(HEADS):
                    o_ref[row:row + r, h * CHUNK:(h + 1) * CHUNK] = ref[h]
            else:
                r = 1
                o_ref[row:row + 1, 0:128] = ref[0:1, :]
            row += r
        assert row == n_rows, (row, n_rows)

    vmem = pl.BlockSpec(memory_space=pltpu.VMEM)
    return pl.pallas_call(
        body, name=name, out_shape=jax.ShapeDtypeStruct((n_rows, D), F32),
        in_specs=[pl.BlockSpec(memory_space=pl.ANY)] * len(extra) + [vmem] * len(arrays), out_specs=vmem,
        compiler_params=pltpu.CompilerParams(vmem_limit_bytes=VMEM_LIMIT),
    )(*extra, *arrays)


def _to_bf16_stack(name, w, q_idx):
    r, cols = w.shape
    half = r // 2
    tr = _tile(half, 256)
    nb = half // tr

    def body(q_ref, w_ref, o_ref):
        o_ref[...] = w_ref[...].astype(BF16)

    return pl.pallas_call(
        body, name=name,
        grid_spec=pltpu.PrefetchScalarGridSpec(
            num_scalar_prefetch=1, grid=(2 * nb,),
            in_specs=[pl.BlockSpec((tr, cols), lambda i, qr: (i, 0))],
            out_specs=pl.BlockSpec((None, None, tr, cols), lambda i, qr: (qr[0], i // nb, i % nb, 0))),
        out_shape=jax.ShapeDtypeStruct((NCHIP, 2, half, cols), BF16),
        compiler_params=_cparams("parallel"),
    )(q_idx, w)


def kernel(x, c, ada_w, ada_b, norm_ffn1, ffn1_w_gate, ffn1_w_up, ffn1_w_down, norm_mix, mix_w_in, mix_b_in, sgu_ln_g, sgu_ln_b, sgu_w_s, sgu_b_s, conv_w, conv_b, conv_ln_g, conv_ln_b, w_branch_a, w_branch_b, w_out, norm_ffn2, ffn2_w_gate, ffn2_w_up, ffn2_w_down, norm_final, loss_target, m_ada_w, m_ada_b, m_norm_ffn1, m_ffn1_w_gate, m_ffn1_w_up, m_ffn1_w_down, m_norm_mix, m_mix_w_in, m_mix_b_in, m_sgu_ln_g, m_sgu_ln_b, m_sgu_w_s, m_sgu_b_s, m_conv_w, m_conv_b, m_conv_ln_g, m_conv_ln_b, m_w_branch_a, m_w_branch_b, m_w_out, m_norm_ffn2, m_ffn2_w_gate, m_ffn2_w_up, m_ffn2_w_down, m_norm_final, v_ada_w, v_ada_b, v_norm_ffn1, v_ffn1_w_gate, v_ffn1_w_up, v_ffn1_w_down, v_norm_mix, v_mix_w_in, v_mix_b_in, v_sgu_ln_g, v_sgu_ln_b, v_sgu_w_s, v_sgu_b_s, v_conv_w, v_conv_b, v_conv_ln_g, v_conv_ln_b, v_w_branch_a, v_w_branch_b, v_w_out, v_norm_ffn2, v_ffn2_w_gate, v_ffn2_w_up, v_ffn2_w_down, v_norm_final):
    xi, yi, ci = lax.axis_index("x"), lax.axis_index("y"), lax.axis_index("c")
    q = 2 * xi + yi
    dev = 4 * xi + 2 * yi + ci
    c_idx = jnp.reshape(ci, (1,)).astype(jnp.int32)
    q_idx = jnp.reshape(q, (1,)).astype(jnp.int32)
    x0 = x[0]
    tgt = loss_target[0]
    qc_idx = jnp.stack([q, ci]).astype(jnp.int32)
    n_ada = ada_w.shape[2]

    big = dict(g1=ffn1_w_gate, u1=ffn1_w_up, d1=ffn1_w_down, win=mix_w_in, wa=w_branch_a, wb=w_branch_b, wo=w_out,
               g2=ffn2_w_gate, u2=ffn2_w_up, d2=ffn2_w_down)
    groups = dict(f1=("d1", "g1", "u1"), mx=("wo", "wa", "wb", "win"), f2=("d2", "g2", "u2"))
    gather_groups = dict(up1=("g1", "u1"), dn1=("d1",), win=("win",), mx=("wa", "wb", "wo"), up2=("g2", "u2"),
                         dn2=("d2",))
    first_stacks = [_to_bf16_stack(f"cast_{k}", big[k][0], q_idx) for k in gather_groups["up1"]]

    c_all = _allgather8("ag_c", jnp.pad(c, ((0, 7), (0, 0))), after=first_stacks)[::8]
    ada_b_mine = lax.dynamic_slice(ada_b, (0, q * n_ada), (1, n_ada))
    mod_part = _ada_fwd("ada_fwd", c_all, ada_w[0], ada_b_mine)
    small_parts = _gather_small("gather_small", [mod_part, jnp.pad(conv_w[0], ((0, 1), (0, 0)))])

    gather_state = {}
    started = [small_parts[0]]
    for tag, keys in gather_groups.items():
        stacks = first_stacks if tag == "up1" else [_to_bf16_stack(f"cast_{k}", big[k][0], q_idx) for k in keys]
        gather_state[tag], token = _gather_start(tag, stacks, started[-1])
        started.append(token)

    mod_all = jnp.transpose(small_parts[0], (1, 0, 2)).reshape(8, N_MOD * D)
    mod = lax.dynamic_slice(mod_all, (dev, 0), (1, N_MOD * D))
    sh1, sc1, g1, sh2, sc2, g2, sh3, sc3 = [(mod, D, k) for k in range(N_MOD - 1)]
    g3 = mod[:, (N_MOD - 1) * D:]
    convw = jnp.transpose(small_parts[1], (1, 0, 2)).reshape(HALO, D)
    W = dict(zip(gather_groups["up1"], _gather_finish("up1", gather_state["up1"], [mod] + started)))

    causal = jnp.tril(jnp.ones((CHUNK, CHUNK), bool))
    wm_f = jnp.where(causal[None], sgu_w_s[0], 0.0)
    wm = wm_f.astype(BF16)
    wmt = jnp.swapaxes(wm_f, 1, 2).astype(BF16)
    bst = jnp.repeat(jnp.transpose(sgu_b_s[0]), CHUNK, axis=1)
    b_in = mix_b_in

    (h1,) = _rowwise_fwd("pre1", _f_pre, [x0], [norm_ffn1, sc1, sh1], [(D, BF16)], 256)
    a1, b1, s1 = _ffn_up("up1", h1, W["g1"], W["u1"])
    W.update(zip(gather_groups["dn1"], _gather_finish("dn1", gather_state["dn1"], s1)))
    f1 = _matmul("down1", s1, W["d1"], "fwd", False, F32)
    pass_win, token = _gather_land("win", gather_state["win"], f1)
    x1, h2 = _rowwise_fwd("res1", _f_res_pre(0.5), [x0, f1], [g1, norm_mix, sc2, sh2],
                          [(D, F32), (D, BF16)], 256, after=token)
    W.update(zip(gather_groups["win"], _gather_done("win", pass_win, x1)))
    proj = _matmul("in_proj", h2, W["win"], "fwd", True, F32)
    pass_mx, token = _gather_land("mx", gather_state["mx"], proj)
    ya = _sgu_fwd("sgu", proj, b_in, sgu_ln_g, sgu_ln_b, wm, bst, 256)
    z1 = _conv_fwd("conv", proj, b_in, convw, conv_b)
    (z3,) = _rowwise_fwd("conv_ln", _f_lnsilu, [z1], [conv_ln_g, conv_ln_b], [(D, BF16)], 256, after=token)
    W.update(zip(gather_groups["mx"], _gather_done("mx", pass_mx, z3)))
    y_a = _matmul("branch_a", ya, W["wa"], "fwd", False, F32)
    y_b = _matmul("branch_b", z3, W["wb"], "fwd", False, F32)
    pass_up2, token = _gather_land("up2", gather_state["up2"], y_b)
    (merged,) = _rowwise_fwd("merge", _f_merge, [(proj, 2 * D, 2), y_a, y_b], [(b_in, 2 * D, 2)],
                             [(D, BF16)], 256, after=token)
    y = _matmul("out_proj", merged, W["wo"], "fwd", False, F32)
    x2, h3 = _rowwise_fwd("res2", _f_res_pre(1.0), [x1, y], [g2, norm_ffn2, sc3, sh3], [(D, F32), (D, BF16)], 256)
    W.update(zip(gather_groups["up2"], _gather_done("up2", pass_up2, x2)))
    a3, b3, s3 = _ffn_up("up2", h3, W["g2"], W["u2"])
    W.update(zip(gather_groups["dn2"], _gather_finish("dn2", gather_state["dn2"], s3)))
    f3 = _matmul("down2", s3, W["d2"], "fwd", False, F32)
    dx2, df3, dg3, dnwf, loss_blk = _final("final", x2, f3, tgt, g3, norm_final.reshape(1, D), 256)

    G = {}

    def ffn_bwd(tag, df, s, a, b, h, wg, wu, wd, after=None):
        da, db = _dgrad_glu(f"d_down{tag}", df, wd, a, b, after)
        G["d" + tag] = _matmul(f"w_down{tag}", s, df, "wgrad", False, BF16)
        dh = _dgrad_pair(f"d_gate_up{tag}", da, db, wg, wu)
        G["g" + tag] = _matmul(f"w_gate{tag}", h, da, "wgrad", True, BF16)
        G["u" + tag] = _matmul(f"w_up{tag}", h, db, "wgrad", True, BF16)
        return dh

    reduced = {}
    dh3 = ffn_bwd("2", df3, s3, a3, b3, h3, W["g2"], W["u2"], W["d2"])
    pair_f2, token = _rs_pair_start("f2", [G[k] for k in groups["f2"]])
    (dx1, dy), (dg2, dnw3, dsc3, dsh3) = _rowwise_bwd(
        "res2_b", _f_res_pre(1.0), [x1, y], [g2, norm_ffn2, sc3, sh3], [dx2, dh3], [F32, BF16], 256, after=token)
    dmerged = _matmul("d_out", dy, W["wo"], "dgrad", False, F32)
    G["wo"] = _matmul("w_out", merged, dy, "wgrad", False, BF16)
    rs_f2, token = _rs_chips_start("f2", pair_f2, [dmerged, G["wo"]], c_idx)
    (dgab, dy_a, dy_b), (dbgab,) = _rowwise_bwd(
        "merge_b", _f_merge, [(proj, 2 * D, 2), y_a, y_b], [(b_in, 2 * D, 2)], [dmerged],
        [BF16, BF16, BF16], 256, after=token)
    dya = _matmul("d_a", dy_a, W["wa"], "dgrad", False, F32)
    dz3 = _matmul("d_b", dy_b, W["wb"], "dgrad", False, F32)
    G["wa"] = _matmul("w_a", ya, dy_a, "wgrad", False, BF16)
    G["wb"] = _matmul("w_b", z3, dy_b, "wgrad", False, BF16)
    (dz1,), (dclg, dclb) = _rowwise_bwd("conv_ln_b", _f_lnsilu, [z1], [conv_ln_g, conv_ln_b], [dz3], [F32], 256)
    dcv, dcg, conv_small = _conv_bwd("conv_b", dz1, proj, b_in, convw)
    duv, dbuv, dlng, dlnb, dws, dbs = _sgu_bwd("sgu_b", proj, b_in, sgu_ln_g, sgu_ln_b, wm, wmt, bst, dya, 256)
    dproj = jnp.concatenate([duv, dcv, dcg, dgab], axis=1)
    dh2 = _matmul("d_in", dproj, W["win"], "dgrad", True, F32)
    G["win"] = _matmul("w_in", h2, dproj, "wgrad", True, BF16)
    pair_mx, token = _rs_pair_start("mx", [G[k] for k in groups["mx"]])
    share_f2, token2 = _rs_land("f2", rs_f2, [dh2, token], qc_idx)

    dev_idx = jnp.reshape(dev, (1,)).astype(jnp.int32)
    early = _pack("pack_early", HALO + CHUNK, [("slice", conv_small, 0, HALO), ("heads", dws)])
    ag_early, token3 = _allgather_start("ag_early", _place("place_early", early, dev_idx), token2)

    (dx0, df1), (dg1, dnw2, dsc2, dsh2) = _rowwise_bwd(
        "res1_b", _f_res_pre(0.5), [x0, f1], [g1, norm_mix, sc2, sh2],
        [dx1, dh2], [F32, BF16], 256, after=[token, token3])
    rs_mx, token = _rs_chips_start("mx", pair_mx, df1, c_idx)
    dh1 = ffn_bwd("1", df1, s1, a1, b1, h1, W["g1"], W["u1"], W["d1"], after=token)
    pair_f1, token = _rs_pair_start("f1", [G[k] for k in groups["f1"]])
    reduced.update(zip(groups["f2"], _rs_done("f2", share_f2, dh1)))
    share_mx, token2 = _rs_land("mx", rs_mx, [dh1, token], qc_idx)
    (grad_x,), (dnw1, dsc1, dsh1) = _rowwise_bwd(
        "pre1_b", _f_pre_keep, [x0], [norm_ffn1, sc1, sh1], [dx0, dh1], [F32], 256, after=[token, token2])
    reduced.update(zip(groups["mx"], _rs_done("mx", share_mx, grad_x)))
    tot_early = _sum8("sum_early", _allgather_done("ag_early", ag_early, grad_x), HALO)

    dbs_row = jnp.transpose(dbs[:, ::CHUNK]).reshape(1, D)
    small = _pack("pack_small", HALO, [
        ("rows", dnw1), ("rows", dnw2), ("wide", dbuv), ("slice", conv_small, 33, 34), ("slice", conv_small, 34, 35),
        ("wide", dbgab), ("rows", dlng), ("rows", dlnb), ("rows", dbs_row), ("slice", conv_small, 32, 33),
        ("rows", dclg), ("rows", dclb), ("rows", dnw3), ("rows", dnwf),
        ("rows", dsh1), ("rows", dsc1), ("rows", dg1), ("rows", dsh2), ("rows", dsc2), ("rows", dg2),
        ("rows", dsh3), ("rows", dsc3), ("rows", dg3), ("lanes", loss_blk), ("zeros", 6)])
    every = _allgather8("ag_small", small)
    rs_f1, token = _rs_chips_start("f1", pair_f1, every, c_idx)
    every = every.reshape(8, HALO, D)
    tot = _sum8("sum_small", every, HALO, after=token)
    loss = tot[25, 0]
    dmod_all = every[:, 16:16 + N_MOD, :].reshape(8, N_MOD * D)
    dmod_mine = lax.dynamic_slice(dmod_all, (0, q * n_ada), (8, n_ada))
    g_ada_w = _ada_wgrad("ada_wgrad", jnp.transpose(c_all), dmod_mine)

    given = dict(ada_w=(ada_w, m_ada_w, v_ada_w), ada_b=(ada_b, m_ada_b, v_ada_b),
                 norm_ffn1=(norm_ffn1, m_norm_ffn1, v_norm_ffn1),
                 ffn1_w_gate=(ffn1_w_gate, m_ffn1_w_gate, v_ffn1_w_gate), ffn1_w_up=(ffn1_w_up, m_ffn1_w_up, v_ffn1_w_up),
                 ffn1_w_down=(ffn1_w_down, m_ffn1_w_down, v_ffn1_w_down), norm_mix=(norm_mix, m_norm_mix, v_norm_mix),
                 mix_w_in=(mix_w_in, m_mix_w_in, v_mix_w_in), mix_b_in=(mix_b_in, m_mix_b_in, v_mix_b_in),
                 sgu_ln_g=(sgu_ln_g, m_sgu_ln_g, v_sgu_ln_g), sgu_ln_b=(sgu_ln_b, m_sgu_ln_b, v_sgu_ln_b),
                 sgu_w_s=(sgu_w_s, m_sgu_w_s, v_sgu_w_s), sgu_b_s=(sgu_b_s, m_sgu_b_s, v_sgu_b_s),
                 conv_w=(conv_w, m_conv_w, v_conv_w), conv_b=(conv_b, m_conv_b, v_conv_b),
                 conv_ln_g=(conv_ln_g, m_conv_ln_g, v_conv_ln_g), conv_ln_b=(conv_ln_b, m_conv_ln_b, v_conv_ln_b),
                 w_branch_a=(w_branch_a, m_w_branch_a, v_w_branch_a), w_branch_b=(w_branch_b, m_w_branch_b, v_w_branch_b),
                 w_out=(w_out, m_w_out, v_w_out), norm_ffn2=(norm_ffn2, m_norm_ffn2, v_norm_ffn2),
                 ffn2_w_gate=(ffn2_w_gate, m_ffn2_w_gate, v_ffn2_w_gate), ffn2_w_up=(ffn2_w_up, m_ffn2_w_up, v_ffn2_w_up),
                 ffn2_w_down=(ffn2_w_down, m_ffn2_w_down, v_ffn2_w_down), norm_final=(norm_final, m_norm_final, v_norm_final))
    out = {}

    def big_update(name, g2d):
        w, m, v = given[name]
        shp = w.shape
        res = _adamw(f"adamw_{name}", w.reshape(g2d.shape), g2d, m.reshape(g2d.shape), v.reshape(g2d.shape),
                     _tile(g2d.shape[0], 512 if g2d.shape[1] <= D else 256))
        out[name] = tuple(t.reshape(shp) for t in res)

    big_names = dict(g1="ffn1_w_gate", u1="ffn1_w_up", d1="ffn1_w_down", win="mix_w_in", wa="w_branch_a",
                     wb="w_branch_b", wo="w_out", g2="ffn2_w_gate", u2="ffn2_w_up", d2="ffn2_w_down")
    for key in groups["f2"] + groups["mx"]:
        big_update(big_names[key], reduced[key])
    big_update("ada_w", g_ada_w)

    small_rows = [("norm_ffn1", 0, 1), ("norm_mix", 1, 1), ("mix_b_in", 2, 6), ("sgu_ln_g", 8, 1), ("sgu_ln_b", 9, 1),
                  ("sgu_b_s", 10, 1), ("conv_b", 11, 1), ("conv_ln_g", 12, 1), ("conv_ln_b", 13, 1),
                  ("norm_ffn2", 14, 1), ("norm_final", 15, 1), ("ada_b", 16, N_MOD)]

    def pack_rows(which, label):
        pieces = []
        for nm, _, r in small_rows:
            t = given[nm][which]
            pieces.append(("wide", t) if r > 1 else ("rows", t.reshape(1, D)))
        return _pack(f"pack_{label}", HALO, pieces + [("zeros", HALO - 16 - N_MOD)])

    small_res = _adamw("adamw_small", pack_rows(0, "w"), tot, pack_rows(1, "m"), pack_rows(2, "v"), HALO)
    for nm, r0, r in small_rows:
        shp = given[nm][0].shape
        out[nm] = tuple(t[r0:r0 + r].reshape(shp) for t in small_res)
    n_cw = conv_w.shape[2]
    g_cw = lax.dynamic_slice(tot_early, (0, q * n_cw), (HALO, n_cw))
    pad_cw = lambda t: jnp.pad(t[0], ((0, 1), (0, 0)))
    cw_res = _adamw("adamw_conv_w", pad_cw(conv_w), g_cw, pad_cw(m_conv_w), pad_cw(v_conv_w), HALO)
    out["conv_w"] = tuple(t[:KCONV][None] for t in cw_res)
    g_ws = jnp.transpose(tot_early[HALO:].reshape(CHUNK, HEADS, CHUNK), (1, 0, 2)).reshape(HEADS * CHUNK, CHUNK)
    flat_ws = lambda t: t.reshape(HEADS * CHUNK, CHUNK)
    ws_res = _adamw("adamw_sgu_w_s", flat_ws(sgu_w_s), g_ws, flat_ws(m_sgu_w_s), flat_ws(v_sgu_w_s), 512)
    out["sgu_w_s"] = tuple(t.reshape(sgu_w_s.shape) for t in ws_res)

    done = [out[n][1] for n in ("ada_w", "sgu_w_s", "conv_w", "ada_b")] + [out[big_names[k]][1] for k in groups["f2"] + groups["mx"]]
    reduced.update(zip(groups["f1"], _rs_finish("f1", rs_f1, done, qc_idx)))
    for key in groups["f1"]:
        big_update(big_names[key], reduced[key])

    order = ["ada_w", "ada_b", "norm_ffn1", "ffn1_w_gate", "ffn1_w_up", "ffn1_w_down", "norm_mix", "mix_w_in", "mix_b_in",
             "sgu_ln_g", "sgu_ln_b", "sgu_w_s", "sgu_b_s", "conv_w", "conv_b", "conv_ln_g", "conv_ln_b", "w_branch_a",
             "w_branch_b", "w_out", "norm_ffn2", "ffn2_w_gate", "ffn2_w_up", "ffn2_w_down", "norm_final"]
    return (loss, grad_x[None], *[out[n][0] for n in order], *[out[n][1] for n in order],
            *[out[n][2] for n in order], *[out[n][3] for n in order])
```

```python
import functools

import jax
import jax.numpy as jnp
from jax import lax
from jax.experimental import pallas as pl
from jax.experimental.pallas import tpu as pltpu

F32 = jnp.float32
BF16 = jnp.bfloat16
D = 1024
HEADS = 8
CHUNK = 128
KCONV = 31
HALO = 32
N_MOD = 9
EPS = 1e-6
NCHIP = 4
VMEM_LIMIT = 56 * 1024 * 1024
MESH = pl.DeviceIdType.MESH

ADAM_LR = 0.001
ADAM_B1 = 0.9
ADAM_B2 = 0.999
ADAM_EPS = 1e-08
ADAM_WD = 0.01
ADAM_STEP = 10

NN = (((1,), (0,)), ((), ()))
NT = (((1,), (1,)), ((), ()))
TN = (((0,), (0,)), ((), ()))


def _cparams(*sem):
    return pltpu.CompilerParams(dimension_semantics=sem or None, vmem_limit_bytes=VMEM_LIMIT)


def _rms(x, w):
    return x * lax.rsqrt(jnp.mean(x * x, axis=-1, keepdims=True) + EPS) * w


def _ln(x, g, b):
    mu = jnp.mean(x, axis=-1, keepdims=True)
    xc = x - mu
    var = jnp.mean(xc * xc, axis=-1, keepdims=True)
    return xc * lax.rsqrt(var + EPS) * g + b


def _silu(x):
    return x * jax.nn.sigmoid(x)


def _gelu(x):
    return x * (0.5 * (1.0 + jnp.tanh(0.7978845608028654 * (x + 0.044715 * (x * x * x)))))


def _f_pre(x, nw, sc, sh):
    return _rms(x, nw) * (1.0 + sc) + sh


def _f_pre_keep(x, nw, sc, sh):
    return x, _f_pre(x, nw, sc, sh)


def _f_res_pre(scale):
    def fn(x, f, g, nw, sc, sh):
        xn = x + (scale * g) * f
        return xn, _f_pre(xn, nw, sc, sh)
    return fn


def _f_glu(a, b):
    return _silu(a) * b


def _f_merge(gab, ya, yb, bgab):
    g = gab + bgab
    return jax.nn.sigmoid(g[:, :D]) * ya + jax.nn.sigmoid(g[:, D:]) * yb


def _f_lnsilu(z, g, b):
    return _silu(_ln(z, g, b))


def _f_mixa(uv, buv, g, b):
    t = uv + buv
    return _gelu(t[:, :D]), _ln(_gelu(t[:, D:]), g, b)


def _col(t):
    return t if isinstance(t, tuple) else (t, t.shape[-1], 0)


def _afters(after):
    if after is None:
        return []
    return list(after) if isinstance(after, (list, tuple)) else [after]


def _rowwise_fwd(name, fn, rows, params, outs, tr, after=None):
    rows = [_col(r) for r in rows]
    params = [_col(p) for p in params]
    extra = _afters(after)
    n_rows = rows[0][0].shape[0]
    nin = len(rows) + len(params)

    def body(*refs):
        res = fn(*[r[...].astype(F32) for r in refs[:nin]])
        res = res if isinstance(res, tuple) else (res,)
        for o, v in zip(refs[nin + len(extra):], res, strict=True):
            o[...] = v.astype(o.dtype)

    return pl.pallas_call(
        body, name=name, grid=(n_rows // tr,),
        in_specs=[pl.BlockSpec((tr, w), functools.partial(lambda cb, i: (i, cb), cb)) for _, w, cb in rows]
        + [pl.BlockSpec((1, w), functools.partial(lambda cb, i: (0, cb), cb)) for _, w, cb in params]
        + [pl.BlockSpec(memory_space=pl.ANY)] * len(extra),
        out_specs=[pl.BlockSpec((tr, w), lambda i: (i, 0)) for w, _ in outs],
        out_shape=[jax.ShapeDtypeStruct((n_rows, w), dt) for w, dt in outs],
        compiler_params=_cparams("parallel"),
    )(*[r[0] for r in rows], *[p[0] for p in params], *extra)


def _rowwise_bwd(name, fn, rows, params, cots, row_grads, tr, after=None):
    rows = [_col(r) for r in rows]
    params = [_col(p) for p in params]
    cots = [_col(t) for t in cots]
    extra = _afters(after)
    n_rows = rows[0][0].shape[0]
    nr, npar, nc = len(rows), len(params), len(cots)
    nin = nr + npar + nc
    n_rg = sum(dt is not None for dt in row_grads)

    def body(*refs):
        i = pl.program_id(0)
        prim = [r[...].astype(F32) for r in refs[:nr + npar]]
        ct = tuple(r[...].astype(F32) for r in refs[nr + npar:nin])
        _, vjp = jax.vjp(fn, *prim)
        g = vjp(ct if nc > 1 else ct[0])
        outs = refs[nin + len(extra):]
        oi = 0
        for j, dt in enumerate(row_grads):
            if dt is not None:
                outs[oi][...] = g[j].astype(dt)
                oi += 1
        for j in range(npar):
            acc = outs[n_rg + j]

            @pl.when(i == 0)
            def _(acc=acc):
                acc[...] = jnp.zeros(acc.shape, F32)

            acc[...] += g[nr + j]

    res = pl.pallas_call(
        body, name=name, grid=(n_rows // tr,),
        in_specs=[pl.BlockSpec((tr, w), functools.partial(lambda cb, i: (i, cb), cb)) for _, w, cb in rows]
        + [pl.BlockSpec((1, w), functools.partial(lambda cb, i: (0, cb), cb)) for _, w, cb in params]
        + [pl.BlockSpec((tr, w), functools.partial(lambda cb, i: (i, cb), cb)) for _, w, cb in cots]
        + [pl.BlockSpec(memory_space=pl.ANY)] * len(extra),
        out_specs=[pl.BlockSpec((tr, rows[j][1]), lambda i: (i, 0)) for j, dt in enumerate(row_grads) if dt is not None]
        + [pl.BlockSpec((1, w), lambda i: (0, 0)) for _, w, _ in params],
        out_shape=[jax.ShapeDtypeStruct((n_rows, rows[j][1]), dt) for j, dt in enumerate(row_grads) if dt is not None]
        + [jax.ShapeDtypeStruct((1, w), F32) for _, w, _ in params],
        compiler_params=_cparams("arbitrary"),
    )(*[r[0] for r in rows], *[p[0] for p in params], *[t[0] for t in cots], *extra)
    return res[:n_rg], res[n_rg:]


def _final(name, x, f, tgt, g, nw, tr):
    n_rows = x.shape[0]

    def body(x_ref, f_ref, t_ref, g_ref, nw_ref, dx_ref, df_ref, dg_ref, dnw_ref, loss_ref):
        i = pl.program_id(0)
        tg = t_ref[...]

        def fn(xv, fv, gv, nwv):
            e = _rms(xv + (0.5 * gv) * fv, nwv) - tg
            return 0.5 * jnp.mean(e * e, axis=-1, keepdims=True)

        per_row, vjp = jax.vjp(fn, x_ref[...], f_ref[...], g_ref[...], nw_ref[...])
        dx, df, dg, dnw = vjp(jnp.ones_like(per_row))
        dx_ref[...] = dx
        df_ref[...] = df.astype(df_ref.dtype)

        @pl.when(i == 0)
        def _():
            dg_ref[...] = jnp.zeros(dg_ref.shape, F32)
            dnw_ref[...] = jnp.zeros(dnw_ref.shape, F32)
            loss_ref[...] = jnp.zeros(loss_ref.shape, F32)

        dg_ref[...] += dg
        dnw_ref[...] += dnw
        loss_ref[...] += jnp.broadcast_to(jnp.sum(per_row, axis=0, keepdims=True), loss_ref.shape)

    row = pl.BlockSpec((tr, D), lambda i: (i, 0))
    par = pl.BlockSpec((1, D), lambda i: (0, 0))
    return pl.pallas_call(
        body, name=name, grid=(n_rows // tr,),
        in_specs=[row, row, row, par, par],
        out_specs=[row, row, par, par, pl.BlockSpec((8, 128), lambda i: (0, 0))],
        out_shape=[jax.ShapeDtypeStruct((n_rows, D), F32), jax.ShapeDtypeStruct((n_rows, D), BF16),
                   jax.ShapeDtypeStruct((1, D), F32), jax.ShapeDtypeStruct((1, D), F32),
                   jax.ShapeDtypeStruct((8, 128), F32)],
        compiler_params=_cparams("arbitrary"),
    )(x, f, tgt, g, nw)


def _tile(dim, pref):
    t = min(dim, pref)
    while dim % t:
        t //= 2
    return t


def _matmul(name, a, b, kind, col, out_dtype, add=None, after=None):
    if kind == "wgrad":
        m, kg = a.shape
        ng = b.shape[1]
        r, c = (kg, ng // NCHIP) if col else (kg // NCHIP, ng)
        tm, tn, tk = _tile(r, 1024), _tile(c, 1536), _tile(m, 2048)
        grid = (kg // tm, ng // tn, m // tk)
        a_spec = pl.BlockSpec((tk, tm), lambda i, j, k: (k, i))
        b_spec = pl.BlockSpec((tk, tn), lambda i, j, k: (k, j))
        if col:
            o_spec = pl.BlockSpec((None, tm, tn), lambda i, j, k: (j // (c // tn), i, j % (c // tn)))
        else:
            o_spec = pl.BlockSpec((None, tm, tn), lambda i, j, k: (i // (r // tm), i % (r // tm), j))
        out_shape = jax.ShapeDtypeStruct((NCHIP, r, c), out_dtype)
        dims = TN
    else:
        _, r, c = b.shape
        m = a.shape[0]
        kg, ng = (r, NCHIP * c) if col else (NCHIP * r, c)
        tm = _tile(m, 1024)
        if kind == "fwd":
            tn, tk = _tile(c, 1536), _tile(r, 2048)
            grid = (m // tm, ng // tn, kg // tk)
            a_spec = pl.BlockSpec((tm, tk), lambda i, j, k: (i, k))
            if col:
                b_spec = pl.BlockSpec((None, tk, tn), lambda i, j, k: (j // (c // tn), k, j % (c // tn)))
            else:
                b_spec = pl.BlockSpec((None, tk, tn), lambda i, j, k: (k // (r // tk), k % (r // tk), j))
            out_shape = jax.ShapeDtypeStruct((m, ng), out_dtype)
            dims = NN
        else:
            tn, tk = _tile(r, 1024), _tile(c, 2048)
            grid = (m // tm, kg // tn, ng // tk)
            a_spec = pl.BlockSpec((tm, tk), lambda i, j, k: (i, k))
            if col:
                b_spec = pl.BlockSpec((None, tn, tk), lambda i, j, k: (k // (c // tk), j, k % (c // tk)))
            else:
                b_spec = pl.BlockSpec((None, tn, tk), lambda i, j, k: (j // (r // tn), j % (r // tn), k))
            out_shape = jax.ShapeDtypeStruct((m, kg), out_dtype)
            dims = NT
        o_spec = pl.BlockSpec((tm, tn), lambda i, j, k: (i, j))
    nk = grid[2]
    has_add = add is not None
    extra = [] if after is None else [after]

    def body(*refs):
        a_ref, b_ref = refs[0], refs[1]
        o_ref = refs[2 + has_add + len(extra)]
        k = pl.program_id(2)
        p = lax.dot_general(a_ref[...].astype(BF16), b_ref[...].astype(BF16), dims, preferred_element_type=F32)

        def finish(r_):
            if has_add:
                r_ = r_ + refs[2][...]
            o_ref[...] = r_.astype(o_ref.dtype)

        if nk == 1:
            finish(p)
        else:
            acc_ref = refs[-1]

            @pl.when(k == 0)
            def _():
                acc_ref[...] = p

            @pl.when(jnp.logical_and(k > 0, k < nk - 1))
            def _():
                acc_ref[...] += p

            @pl.when(k == nk - 1)
            def _():
                finish(acc_ref[...] + p)

    in_specs = [a_spec, b_spec] + ([pl.BlockSpec((tm, tn), lambda i, j, k: (i, j))] if has_add else [])
    in_specs += [pl.BlockSpec(memory_space=pl.ANY)] * len(extra)
    args = (a, b) + ((add,) if has_add else ()) + tuple(extra)
    return pl.pallas_call(
        body, name=name, grid=grid, in_specs=in_specs, out_specs=o_spec, out_shape=out_shape,
        scratch_shapes=[pltpu.VMEM((tm, tn), F32)] if nk > 1 else [],
        compiler_params=_cparams("parallel", "parallel", "arbitrary"),
    )(*args)


def _ffn_up(name, h, wg, wu):
    m, k = h.shape
    _, _, c = wg.shape
    n = NCHIP * c
    tm, tn = _tile(m, 1024), _tile(c, 1024)

    def body(h_ref, wg_ref, wu_ref, a_ref, b_ref, s_ref):
        hv = h_ref[...]
        a = jnp.dot(hv, wg_ref[...], preferred_element_type=F32)
        b = jnp.dot(hv, wu_ref[...], preferred_element_type=F32)
        a_ref[...] = a
        b_ref[...] = b
        s_ref[...] = (_silu(a) * b).astype(s_ref.dtype)

    w_spec = pl.BlockSpec((None, k, tn), lambda i, j: (j // (c // tn), 0, j % (c // tn)))
    o_spec = pl.BlockSpec((tm, tn), lambda i, j: (i, j))
    return pl.pallas_call(
        body, name=name, grid=(m // tm, n // tn),
        in_specs=[pl.BlockSpec((tm, k), lambda i, j: (i, 0)), w_spec, w_spec],
        out_specs=[o_spec, o_spec, o_spec],
        out_shape=[jax.ShapeDtypeStruct((m, n), F32), jax.ShapeDtypeStruct((m, n), F32),
                   jax.ShapeDtypeStruct((m, n), BF16)],
        compiler_params=_cparams("parallel", "parallel"),
    )(h, wg, wu)


def _dgrad_pair(name, da, db, wg, wu):
    m = da.shape[0]
    _, r, c = wg.shape
    tm, tn = _tile(m, 1024), _tile(r, 1024)
    n1 = NCHIP

    def body(da_ref, db_ref, wg_ref, wu_ref, o_ref, acc_ref):
        k = pl.program_id(2)

        @pl.when(k == 0)
        def _():
            acc_ref[...] = lax.dot_general(da_ref[...], wg_ref[...], NT, preferred_element_type=F32)

        @pl.when(jnp.logical_and(k > 0, k < n1))
        def _():
            acc_ref[...] += lax.dot_general(da_ref[...], wg_ref[...], NT, preferred_element_type=F32)

        @pl.when(jnp.logical_and(k >= n1, k < 2 * n1 - 1))
        def _():
            acc_ref[...] += lax.dot_general(db_ref[...], wu_ref[...], NT, preferred_element_type=F32)

        @pl.when(k == 2 * n1 - 1)
        def _():
            o_ref[...] = acc_ref[...] + lax.dot_general(db_ref[...], wu_ref[...], NT, preferred_element_type=F32)

    first = lambda k: jnp.minimum(k, n1 - 1)
    second = lambda k: jnp.maximum(k - n1, 0)
    return pl.pallas_call(
        body, name=name, grid=(m // tm, r // tn, 2 * n1),
        in_specs=[pl.BlockSpec((tm, c), lambda i, j, k: (i, first(k))),
                  pl.BlockSpec((tm, c), lambda i, j, k: (i, second(k))),
                  pl.BlockSpec((None, tn, c), lambda i, j, k: (first(k), j, 0)),
                  pl.BlockSpec((None, tn, c), lambda i, j, k: (second(k), j, 0))],
        out_specs=pl.BlockSpec((tm, tn), lambda i, j, k: (i, j)),
        out_shape=jax.ShapeDtypeStruct((m, r), F32),
        scratch_shapes=[pltpu.VMEM((tm, tn), F32)],
        compiler_params=_cparams("parallel", "parallel", "arbitrary"),
    )(da, db, wg, wu)


def _dgrad_glu(name, df, wd, a, b, after=None):
    m = df.shape[0]
    _, r, c = wd.shape
    tm, tn = _tile(m, 512), _tile(r, 1024)
    extra = [] if after is None else [after]

    def body(df_ref, w_ref, a_ref, b_ref, *rest):
        da_ref, db_ref = rest[-2], rest[-1]
        ds = lax.dot_general(df_ref[...], w_ref[...], NT, preferred_element_type=F32)
        _, vjp = jax.vjp(_f_glu, a_ref[...], b_ref[...])
        da, db = vjp(ds)
        da_ref[...] = da.astype(da_ref.dtype)
        db_ref[...] = db.astype(db_ref.dtype)

    t_spec = pl.BlockSpec((tm, tn), lambda i, j: (i, j))
    sds = jax.ShapeDtypeStruct((m, NCHIP * r), BF16)
    return pl.pallas_call(
        body, name=name, grid=(m // tm, NCHIP * r // tn),
        in_specs=[pl.BlockSpec((tm, c), lambda i, j: (i, 0)),
                  pl.BlockSpec((None, tn, c), lambda i, j: (j // (r // tn), j % (r // tn), 0)), t_spec, t_spec]
        + [pl.BlockSpec(memory_space=pl.ANY)] * len(extra),
        out_specs=[t_spec, t_spec], out_shape=[sds, sds],
        compiler_params=_cparams("parallel", "parallel"),
    )(df, wd, a, b, *extra)


def _sgu_fwd(name, proj, b_in, lng, lnb, wm, bst, tr):
    n_rows = proj.shape[0]

    def body(uv_ref, buv_ref, g_ref, b_ref, wm_ref, bst_ref, ya_ref):
        u, vln = _f_mixa(uv_ref[...], buv_ref[...], g_ref[...], b_ref[...])
        vb = vln.astype(BF16)
        for ck in range(tr // CHUNK):
            rs = slice(ck * CHUNK, (ck + 1) * CHUNK)
            for h in range(HEADS):
                cs = slice(h * CHUNK, (h + 1) * CHUNK)
                vm = jnp.dot(wm_ref[h], vb[rs, cs], preferred_element_type=F32) + bst_ref[:, cs]
                ya_ref[rs, cs] = (u[rs, cs] * vm).astype(ya_ref.dtype)

    return pl.pallas_call(
        body, name=name, grid=(n_rows // tr,),
        in_specs=[pl.BlockSpec((tr, 2 * D), lambda i: (i, 0)), pl.BlockSpec((1, 2 * D), lambda i: (0, 0)),
                  pl.BlockSpec((1, D), lambda i: (0, 0)), pl.BlockSpec((1, D), lambda i: (0, 0)),
                  pl.BlockSpec((HEADS, CHUNK, CHUNK), lambda i: (0, 0, 0)),
                  pl.BlockSpec((CHUNK, D), lambda i: (0, 0))],
        out_specs=pl.BlockSpec((tr, D), lambda i: (i, 0)),
        out_shape=jax.ShapeDtypeStruct((n_rows, D), BF16),
        compiler_params=_cparams("parallel"),
    )(proj, b_in, lng, lnb, wm, bst)


def _sgu_bwd(name, proj, b_in, lng, lnb, wm, wmt, bst, dya, tr):
    n_rows = proj.shape[0]

    def body(uv_ref, buv_ref, g_ref, b_ref, wm_ref, wmt_ref, bst_ref, dya_ref,
             duv_ref, dbuv_ref, dg_ref, db_ref, dws_ref, dbs_ref, du_s, dvln_s):
        i = pl.program_id(0)

        @pl.when(i == 0)
        def _():
            dbuv_ref[...] = jnp.zeros(dbuv_ref.shape, F32)
            dg_ref[...] = jnp.zeros(dg_ref.shape, F32)
            db_ref[...] = jnp.zeros(db_ref.shape, F32)
            dws_ref[...] = jnp.zeros(dws_ref.shape, F32)
            dbs_ref[...] = jnp.zeros(dbs_ref.shape, F32)

        (u, vln), vjp = jax.vjp(_f_mixa, uv_ref[...], buv_ref[...], g_ref[...], b_ref[...])
        vb = vln.astype(BF16)
        dya_v = dya_ref[...].astype(F32)
        tpos = lax.broadcasted_iota(jnp.int32, (CHUNK, CHUNK), 0)
        spos = lax.broadcasted_iota(jnp.int32, (CHUNK, CHUNK), 1)
        causal = (tpos >= spos).astype(F32)
        for ck in range(tr // CHUNK):
            rs = slice(ck * CHUNK, (ck + 1) * CHUNK)
            for h in range(HEADS):
                cs = slice(h * CHUNK, (h + 1) * CHUNK)
                vbh = vb[rs, cs]
                vm = jnp.dot(wm_ref[h], vbh, preferred_element_type=F32) + bst_ref[:, cs]
                dyb = dya_v[rs, cs]
                du_s[rs, cs] = dyb * vm
                dvm = dyb * u[rs, cs]
                dvmb = dvm.astype(BF16)
                dvln_s[rs, cs] = jnp.dot(wmt_ref[h], dvmb, preferred_element_type=F32)
                dws_ref[h] += causal * lax.dot_general(dvmb, vbh, NT, preferred_element_type=F32)
                dbs_ref[:, cs] += jnp.broadcast_to(jnp.sum(dvm, axis=1, keepdims=True), (CHUNK, CHUNK))
        duv, dbuv, dg, db = vjp((du_s[...], dvln_s[...]))
        duv_ref[...] = duv.astype(duv_ref.dtype)
        dbuv_ref[...] += dbuv
        dg_ref[...] += dg
        db_ref[...] += db

    par = pl.BlockSpec((1, D), lambda i: (0, 0))
    par2 = pl.BlockSpec((1, 2 * D), lambda i: (0, 0))
    w_spec = pl.BlockSpec((HEADS, CHUNK, CHUNK), lambda i: (0, 0, 0))
    b_spec = pl.BlockSpec((CHUNK, D), lambda i: (0, 0))
    return pl.pallas_call(
        body, name=name, grid=(n_rows // tr,),
        in_specs=[pl.BlockSpec((tr, 2 * D), lambda i: (i, 0)), par2, par, par, w_spec, w_spec, b_spec,
                  pl.BlockSpec((tr, D), lambda i: (i, 0))],
        out_specs=[pl.BlockSpec((tr, 2 * D), lambda i: (i, 0)), par2, par, par, w_spec, b_spec],
        out_shape=[jax.ShapeDtypeStruct((n_rows, 2 * D), BF16), jax.ShapeDtypeStruct((1, 2 * D), F32),
                   jax.ShapeDtypeStruct((1, D), F32), jax.ShapeDtypeStruct((1, D), F32),
                   jax.ShapeDtypeStruct((HEADS, CHUNK, CHUNK), F32), jax.ShapeDtypeStruct((CHUNK, D), F32)],
        scratch_shapes=[pltpu.VMEM((tr, D), F32), pltpu.VMEM((tr, D), F32)],
        compiler_params=_cparams("arbitrary"),
    )(proj, b_in, lng, lnb, wm, wmt, bst, dya)


CT = 256
RB = 128
CV0 = 2 * D // CT
CG0 = 3 * D // CT


def _shift_bank(win):
    return [win] + [pltpu.roll(win, RB + HALO - r, 0) for r in range(1, 8)]


def _shifted(bank, sh):
    lo = 8 * (sh // 8)
    return bank[sh % 8][lo:lo + RB]


def _conv_fwd(name, proj, b_in, w, cb):
    n_rows = proj.shape[0]

    def body(cv_ref, cg_ref, bcv_ref, bcg_ref, w_ref, cb_ref, z1_ref, zp_ref):
        zp_ref[0:HALO, :] = jnp.zeros((HALO, CT), F32)
        zp_ref[HALO:, :] = (cv_ref[...] + bcv_ref[...]) * jax.nn.sigmoid(cg_ref[...] + bcg_ref[...])

        def blk(rb, carry):
            base = pl.multiple_of(rb * RB, RB)
            bank = _shift_bank(zp_ref[pl.ds(base, RB + HALO), :])
            acc = jnp.broadcast_to(cb_ref[...], (RB, CT))
            for k in range(KCONV):
                acc = acc + w_ref[k:k + 1, :] * _shifted(bank, k + 2)
            z1_ref[pl.ds(base, RB), :] = acc
            return carry

        lax.fori_loop(0, n_rows // RB, blk, 0)

    return pl.pallas_call(
        body, name=name, grid=(D // CT,),
        in_specs=[pl.BlockSpec((n_rows, CT), lambda j: (0, CV0 + j)), pl.BlockSpec((n_rows, CT), lambda j: (0, CG0 + j)),
                  pl.BlockSpec((1, CT), lambda j: (0, CV0 + j)), pl.BlockSpec((1, CT), lambda j: (0, CG0 + j)),
                  pl.BlockSpec((HALO, CT), lambda j: (0, j)), pl.BlockSpec((1, CT), lambda j: (0, j))],
        out_specs=pl.BlockSpec((n_rows, CT), lambda j: (0, j)),
        out_shape=jax.ShapeDtypeStruct((n_rows, D), F32),
        scratch_shapes=[pltpu.VMEM((n_rows + HALO, CT), F32)],
        compiler_params=_cparams("parallel"),
    )(proj, proj, b_in, b_in, w, cb)


def _conv_bwd(name, dz1, proj, b_in, w):
    n_rows = proj.shape[0]

    def body(dz_ref, cv_ref, cg_ref, bcv_ref, bcg_ref, w_ref, dcv_ref, dcg_ref, sm_ref, zp_ref, dzp_ref):
        cvb = cv_ref[...] + bcv_ref[...]
        sg = jax.nn.sigmoid(cg_ref[...] + bcg_ref[...])
        zp_ref[0:HALO, :] = jnp.zeros((HALO, CT), F32)
        zp_ref[HALO:, :] = cvb * sg
        dz = dz_ref[...]
        dzp_ref[0:n_rows, :] = dz
        dzp_ref[n_rows:, :] = jnp.zeros((HALO, CT), F32)
        sm_ref[...] = jnp.zeros(sm_ref.shape, F32)
        sm_ref[32:33, :] = jnp.sum(dz, axis=0, keepdims=True)

        def blk(rb, carry):
            base = pl.multiple_of(rb * RB, RB)
            dbank = _shift_bank(dzp_ref[pl.ds(base, RB + HALO), :])
            zbank = _shift_bank(zp_ref[pl.ds(base, RB + HALO), :])
            dzb = dbank[0][0:RB]
            acc = jnp.zeros((RB, CT), F32)
            for k in range(KCONV):
                acc = acc + w_ref[k:k + 1, :] * _shifted(dbank, KCONV - 1 - k)
                sm_ref[k:k + 1, :] += jnp.sum(dzb * _shifted(zbank, k + 2), axis=0, keepdims=True)
            dzp_ref[pl.ds(base, RB), :] = acc
            return carry

        lax.fori_loop(0, n_rows // RB, blk, 0)
        dz0 = dzp_ref[0:n_rows, :]
        dcv = dz0 * sg
        dcg = dz0 * cvb * (sg * (1.0 - sg))
        dcv_ref[...] = dcv.astype(dcv_ref.dtype)
        dcg_ref[...] = dcg.astype(dcg_ref.dtype)
        sm_ref[33:34, :] = jnp.sum(dcv, axis=0, keepdims=True)
        sm_ref[34:35, :] = jnp.sum(dcg, axis=0, keepdims=True)

    col = pl.BlockSpec((n_rows, CT), lambda j: (0, j))
    return pl.pallas_call(
        body, name=name, grid=(D // CT,),
        in_specs=[col, pl.BlockSpec((n_rows, CT), lambda j: (0, CV0 + j)), pl.BlockSpec((n_rows, CT), lambda j: (0, CG0 + j)),
                  pl.BlockSpec((1, CT), lambda j: (0, CV0 + j)), pl.BlockSpec((1, CT), lambda j: (0, CG0 + j)),
                  pl.BlockSpec((HALO, CT), lambda j: (0, j))],
        out_specs=[col, col, pl.BlockSpec((40, CT), lambda j: (0, j))],
        out_shape=[jax.ShapeDtypeStruct((n_rows, D), BF16), jax.ShapeDtypeStruct((n_rows, D), BF16),
                   jax.ShapeDtypeStruct((40, D), F32)],
        scratch_shapes=[pltpu.VMEM((n_rows + HALO, CT), F32), pltpu.VMEM((n_rows + HALO, CT), F32)],
        compiler_params=_cparams("parallel"),
    )(dz1, proj, proj, b_in, b_in, w)


ADA_TN = 768


def _split_bf16(v):
    hi = v.astype(BF16)
    return hi, (v - hi.astype(F32)).astype(BF16)


def _ada_fwd(name, c_all, w, b):
    n = w.shape[1]

    def body(c_ref, w_ref, b_ref, o_ref):
        ch, cl = _split_bf16(_silu(c_ref[...]))
        wh, wl = _split_bf16(w_ref[...])
        acc = jnp.dot(ch, wl, preferred_element_type=F32) + jnp.dot(cl, wh, preferred_element_type=F32)
        o_ref[...] = acc + jnp.dot(ch, wh, preferred_element_type=F32) + b_ref[...]

    return pl.pallas_call(
        body, name=name, grid=(n // ADA_TN,),
        in_specs=[pl.BlockSpec((8, D), lambda j: (0, 0)), pl.BlockSpec((D, ADA_TN), lambda j: (0, j)),
                  pl.BlockSpec((1, ADA_TN), lambda j: (0, j))],
        out_specs=pl.BlockSpec((8, ADA_TN), lambda j: (0, j)),
        out_shape=jax.ShapeDtypeStruct((8, n), F32),
        compiler_params=_cparams("parallel"),
    )(c_all, w, b)


def _ada_wgrad(name, c_all_t, dmod):
    n = dmod.shape[1]

    def body(ct_ref, dm_ref, o_ref):
        ca = _silu(ct_ref[...])
        acc = ca[:, 0:1] * dm_ref[0:1, :]
        for r in range(1, 8):
            acc = acc + ca[:, r:r + 1] * dm_ref[r:r + 1, :]
        o_ref[...] = acc

    return pl.pallas_call(
        body, name=name, grid=(n // ADA_TN,),
        in_specs=[pl.BlockSpec((D, 8), lambda j: (0, 0)), pl.BlockSpec((8, ADA_TN), lambda j: (0, j))],
        out_specs=pl.BlockSpec((D, ADA_TN), lambda j: (0, j)),
        out_shape=jax.ShapeDtypeStruct((D, n), F32),
        compiler_params=_cparams("parallel"),
    )(c_all_t, dmod)


def _adamw(name, w, g, m, v, tr):
    rows, cols = w.shape

    def body(w_ref, g_ref, m_ref, v_ref, g_out, d_ref, nm_ref, nv_ref):
        gv = g_ref[...]
        nm = ADAM_B1 * m_ref[...] + (1.0 - ADAM_B1) * gv
        nv = ADAM_B2 * v_ref[...] + (1.0 - ADAM_B2) * (gv * gv)
        m_hat = nm / (1.0 - ADAM_B1 ** ADAM_STEP)
        v_hat = nv / (1.0 - ADAM_B2 ** ADAM_STEP)
        d_ref[...] = -ADAM_LR * (m_hat / (jnp.sqrt(v_hat) + ADAM_EPS) + ADAM_WD * w_ref[...])
        nm_ref[...] = nm
        nv_ref[...] = nv
        g_out[...] = gv

    spec = pl.BlockSpec((tr, cols), lambda i: (i, 0))
    sds = jax.ShapeDtypeStruct((rows, cols), F32)
    return pl.pallas_call(
        body, name=name, grid=(rows // tr,), in_specs=[spec] * 4, out_specs=[spec] * 4, out_shape=[sds] * 4,
        compiler_params=_cparams("parallel"),
    )(w, g, m, v)


def _sum8(name, stacked, tr, after=None):
    n, rows, cols = stacked.shape
    extra = _afters(after)

    def body(s_ref, *rest):
        o_ref = rest[-1]
        acc = s_ref[0]
        for r in range(1, n):
            acc = acc + s_ref[r]
        o_ref[...] = acc

    return pl.pallas_call(
        body, name=name, grid=(rows // tr,),
        in_specs=[pl.BlockSpec((n, tr, cols), lambda i: (0, i, 0))] + [pl.BlockSpec(memory_space=pl.ANY)] * len(extra),
        out_specs=pl.BlockSpec((tr, cols), lambda i: (i, 0)),
        out_shape=jax.ShapeDtypeStruct((rows, cols), F32),
        compiler_params=_cparams("parallel"),
    )(stacked, *extra)


def _pair_add(name, g5, other, c_idx, tr):
    nq, _, rows, cols = g5.shape

    def body(c_ref, g_ref, o_ref, p_ref):
        p_ref[...] = (g_ref[...].astype(F32) + o_ref[...].astype(F32)).astype(p_ref.dtype)

    return pl.pallas_call(
        body, name=name,
        grid_spec=pltpu.PrefetchScalarGridSpec(
            num_scalar_prefetch=1, grid=(nq, rows // tr),
            in_specs=[pl.BlockSpec((None, None, tr, cols), lambda qi, i, cr: (qi, cr[0], i, 0)),
                      pl.BlockSpec((None, tr, cols), lambda qi, i, cr: (qi, i, 0))],
            out_specs=pl.BlockSpec((None, tr, cols), lambda qi, i, cr: (qi, i, 0))),
        out_shape=jax.ShapeDtypeStruct((nq, rows, cols), BF16),
        compiler_params=_cparams("parallel", "parallel"),
    )(c_idx, g5, other)


def _chip_add(name, p, recv, qc_idx, tr):
    _, rows, cols = p.shape

    def body(qc_ref, p_ref, r_ref, o_ref):
        acc = p_ref[...].astype(F32)
        for k in range(NCHIP - 1):
            acc = acc + r_ref[k].astype(F32)
        o_ref[...] = acc

    return pl.pallas_call(
        body, name=name,
        grid_spec=pltpu.PrefetchScalarGridSpec(
            num_scalar_prefetch=1, grid=(rows // tr,),
            in_specs=[pl.BlockSpec((None, tr, cols), lambda i, qc: (qc[0], i, 0)),
                      pl.BlockSpec((NCHIP - 1, tr, cols), lambda i, qc: (0, i, 0))],
            out_specs=pl.BlockSpec((None, tr, cols), lambda i, qc: (qc[1], i, 0))),
        out_shape=jax.ShapeDtypeStruct((2, rows, cols), F32),
        compiler_params=_cparams("parallel"),
    )(qc_idx, p, recv)


def _allgather8(name, blk, after=None):
    m_per, n = blk.shape
    extra = _afters(after)

    def body(x_ref, *rest):
        out_ref, send_sems, recv_sems, local_sem = rest[len(extra):]
        x, y, c = lax.axis_index("x"), lax.axis_index("y"), lax.axis_index("c")
        me, sibling = (x, y, c), (x, y, 1 - c)
        chips = [(1 - x, y), (x, 1 - y), (1 - x, 1 - y)]

        def rows(px, py, pc):
            return out_ref.at[pl.ds((4 * px + 2 * py + pc) * m_per, m_per), :]

        def copy(k, block, to, src=None):
            return pltpu.make_async_remote_copy(
                src_ref=rows(*block) if src is None else src, dst_ref=rows(*block),
                send_sem=send_sems.at[k], recv_sem=recv_sems.at[k], device_id=to, device_id_type=MESH)

        mine = pltpu.make_async_copy(x_ref, rows(*me), local_sem)
        mine.start()
        first = [copy(0, me, sibling, src=x_ref)]
        first += [copy(1 + j, me, (*chip, c), src=x_ref) for j, chip in enumerate(chips)]
        for cp in first:
            cp.start()
        passed = [copy(4 + j, (*chip, c), sibling) for j, chip in enumerate(chips)]
        for j, chip in enumerate(chips):
            copy(1 + j, (*chip, c), me).wait_recv()
            passed[j].start()
        copy(0, sibling, me).wait_recv()
        for j, chip in enumerate(chips):
            copy(4 + j, (*chip, 1 - c), me).wait_recv()
        for cp in first + passed:
            cp.wait_send()
        mine.wait()

    return pl.pallas_call(
        body, name=name,
        out_shape=jax.ShapeDtypeStruct((8 * m_per, n), blk.dtype),
        in_specs=[pl.BlockSpec(memory_space=pltpu.VMEM)] + [pl.BlockSpec(memory_space=pl.ANY)] * len(extra),
        out_specs=pl.BlockSpec(memory_space=pltpu.VMEM),
        scratch_shapes=[pltpu.SemaphoreType.DMA((7,)), pltpu.SemaphoreType.DMA((7,)), pltpu.SemaphoreType.DMA],
        compiler_params=pltpu.CompilerParams(vmem_limit_bytes=VMEM_LIMIT),
    )(blk, *extra)


def _exchange(name, srcs, out_shapes, n_local, n_remote, plan, aliases=None, after=None):
    ni, no = len(srcs), len(out_shapes)
    extra = [] if after is None else [after]

    def body(*refs):
        ins, outs = refs[:ni], refs[ni + len(extra):ni + len(extra) + no]
        send_sems, recv_sems, local_sems = refs[ni + len(extra) + no:]
        x, y, c = lax.axis_index("x"), lax.axis_index("y"), lax.axis_index("c")
        local, remote = plan(ins, outs, x, y, c)
        assert len(local) == n_local and len(remote) == n_remote

        def rcopy(i, dst):
            s, _, peer, _, _ = remote[i]
            return pltpu.make_async_remote_copy(src_ref=s, dst_ref=dst, send_sem=send_sems.at[i],
                                                recv_sem=recv_sems.at[i], device_id=peer, device_id_type=MESH)

        lcs = [pltpu.make_async_copy(s, d, local_sems.at[i]) for i, (s, d) in enumerate(local)]
        for cp in lcs:
            cp.start()
        first = [i for i in range(n_remote) if remote[i][4] is None]
        passed = [i for i in range(n_remote) if remote[i][4] is not None]
        for i in first:
            rcopy(i, remote[i][1]).start()
        arrived = set()
        for i in passed:
            j = remote[i][4]
            rcopy(j, remote[j][3]).wait_recv()
            arrived.add(j)
            rcopy(i, remote[i][1]).start()
        for i in range(n_remote):
            if i not in arrived:
                rcopy(i, remote[i][3]).wait_recv()
        for i in range(n_remote):
            rcopy(i, remote[i][1]).wait_send()
        for cp in lcs:
            cp.wait()

    any_spec = pl.BlockSpec(memory_space=pl.ANY)
    return pl.pallas_call(
        body, name=name, out_shape=out_shapes,
        in_specs=[any_spec] * (ni + len(extra)), out_specs=[any_spec] * no,
        input_output_aliases=aliases or {},
        scratch_shapes=[pltpu.SemaphoreType.DMA((n_remote,)), pltpu.SemaphoreType.DMA((n_remote,)),
                        pltpu.SemaphoreType.DMA((max(n_local, 1),))],
    )(*srcs, *extra)


_CHIP_FLIPS = ((0, 1), (1, 0), (1, 1))


def _flip(v, f):
    return 1 - v if f else v


EFFECT = pltpu.SideEffectType.DATAFLOW_SIDE_EFFECTING


def _split_start(name, bufs, n, plan, after=None):
    nb = len(bufs)
    extra = [] if after is None else [after]

    def body(*refs):
        ins = refs[:nb]
        send_sems, recv_sems = refs[nb + len(extra)], refs[nb + len(extra) + 1]
        token = refs[-1]
        x, y, c = lax.axis_index("x"), lax.axis_index("y"), lax.axis_index("c")
        copies = plan(ins, x, y, c)
        assert len(copies) == n
        for i, (s, d, peer, _) in enumerate(copies):
            pltpu.make_async_remote_copy(src_ref=s, dst_ref=d, send_sem=send_sems.at[i], recv_sem=recv_sems.at[i],
                                         device_id=peer, device_id_type=MESH).start()
        token[...] = jnp.zeros_like(token)

    hbm = pl.BlockSpec(memory_space=pltpu.HBM)
    sem = pl.BlockSpec(memory_space=pltpu.SEMAPHORE)
    res = pl.pallas_call(
        body, name=name,
        out_shape=(pltpu.SemaphoreType.DMA((n,)), pltpu.SemaphoreType.DMA((n,)),
                   *[pltpu.HBM(b.shape, b.dtype) for b in bufs], jax.ShapeDtypeStruct((8, 128), F32)),
        in_specs=[hbm] * nb + [pl.BlockSpec(memory_space=pl.ANY)] * len(extra),
        out_specs=(sem, sem, *[hbm] * nb, pl.BlockSpec(memory_space=pltpu.VMEM)),
        input_output_aliases={i: 2 + i for i in range(nb)},
        compiler_params=pltpu.CompilerParams(has_side_effects=EFFECT),
    )(*[pltpu.with_memory_space_constraint(b, pltpu.HBM) for b in bufs], *extra)
    return res[0], res[1], list(res[2:2 + nb]), res[-1]


def _split_wait(name, bufs, send_sems, recv_sems, after, n, plan):
    nb = len(bufs)

    def body(*refs):
        ins = refs[:nb]
        ssem, rsem = refs[nb], refs[nb + 1]
        x, y, c = lax.axis_index("x"), lax.axis_index("y"), lax.axis_index("c")
        copies = plan(ins, x, y, c)
        assert len(copies) == n
        for i, (s, _, peer, lands) in enumerate(copies):
            cp = pltpu.make_async_remote_copy(src_ref=s, dst_ref=lands, send_sem=ssem.at[i], recv_sem=rsem.at[i],
                                              device_id=peer, device_id_type=MESH)
            cp.wait_send()
            cp.wait_recv()

    hbm = pl.BlockSpec(memory_space=pltpu.HBM)
    sem = pl.BlockSpec(memory_space=pltpu.SEMAPHORE)
    afters = list(after) if isinstance(after, (list, tuple)) else [after]
    res = pl.pallas_call(
        body, name=name,
        out_shape=[pltpu.HBM(b.shape, b.dtype) for b in bufs],
        in_specs=[hbm] * nb + [sem, sem] + [pl.BlockSpec(memory_space=pl.ANY)] * len(afters),
        out_specs=[hbm] * nb,
        input_output_aliases={i: i for i in range(nb)},
        compiler_params=pltpu.CompilerParams(has_side_effects=EFFECT),
    )(*bufs, send_sems, recv_sems, *afters)
    return list(res)


def _chips_of(x, y):
    return [(_flip(x, fx), _flip(y, fy)) for fx, fy in _CHIP_FLIPS]


def _gather_start(tag, stacks, after):
    n = len(stacks)

    def plan(refs, x, y, c):
        q = 2 * x + y
        return [(refs[a].at[q, c], refs[a].at[q, c], (px, py, c), refs[a].at[2 * px + py, c])
                for a in range(n) for px, py in _chips_of(x, y)]

    send, recv, thru, token = _split_start(f"gather_start_{tag}", stacks, 3 * n, plan, after)
    return (send, recv, thru, plan), token


def _gather_finish(tag, state, after):
    send, recv, thru, plan = state
    n = len(thru)
    landed = _split_wait(f"gather_wait_{tag}", thru, send, recv, after, 3 * n, plan)

    def pass_on(ins, outs, x, y, c):
        sib = (x, y, 1 - c)
        remote = []
        for a in range(n):
            for px, py in _chips_of(x, y):
                p = 2 * px + py
                remote.append((ins[a].at[p, c], outs[a].at[p, c], sib, outs[a].at[p, 1 - c], None))
        return [], remote

    full = _exchange(f"gather_pass_{tag}", landed, [jax.ShapeDtypeStruct(s.shape, s.dtype) for s in landed],
                     0, 3 * n, pass_on, aliases={a: a for a in range(n)})
    return [g.reshape(NCHIP, 2 * g.shape[2], g.shape[3]) for g in full]


def _gather_land(tag, state, after):
    send, recv, thru, plan = state
    n = len(thru)
    landed = _split_wait(f"gather_wait_{tag}", thru, send, recv, after, 3 * n, plan)

    def pass_plan(refs, x, y, c):
        sib = (x, y, 1 - c)
        return [(refs[a].at[2 * px + py, c], refs[a].at[2 * px + py, c], sib, refs[a].at[2 * px + py, 1 - c])
                for a in range(n) for px, py in _chips_of(x, y)]

    send, recv, thru, token = _split_start(f"gather_pass_start_{tag}", landed, 3 * n, pass_plan)
    return (send, recv, thru, pass_plan), token


def _gather_done(tag, state, after):
    send, recv, thru, plan = state
    full = _split_wait(f"gather_pass_wait_{tag}", thru, send, recv, after, 3 * len(thru), plan)
    return [g.reshape(NCHIP, 2 * g.shape[2], g.shape[3]) for g in full]


_OTHERS = tuple((fx, fy, fc) for fx in (0, 1) for fy in (0, 1) for fc in (0, 1) if fx or fy or fc)


def _allgather_start(tag, buf, after=None):
    def plan(refs, x, y, c):
        me = 4 * x + 2 * y + c
        copies = []
        for fx, fy, fc in _OTHERS:
            px, py, pc = _flip(x, fx), _flip(y, fy), _flip(c, fc)
            copies.append((refs[0].at[me], refs[0].at[me], (px, py, pc), refs[0].at[4 * px + 2 * py + pc]))
        return copies

    send, recv, thru, token = _split_start(f"{tag}_start", [buf], len(_OTHERS), plan, after)
    return (send, recv, thru, plan), token


def _allgather_done(tag, state, after):
    send, recv, thru, plan = state
    return _split_wait(f"{tag}_wait", thru, send, recv, after, len(_OTHERS), plan)[0]


def _gather_small(name, smalls):
    n = len(smalls)

    def plan(ins, outs, x, y, c):
        q = 2 * x + y
        local = [(ins[b], outs[b].at[q]) for b in range(n)]
        remote = [(ins[b], outs[b].at[q], (px, py, c), outs[b].at[2 * px + py], None)
                  for b in range(n) for px, py in _chips_of(x, y)]
        return local, remote

    return _exchange(name, smalls, [jax.ShapeDtypeStruct((NCHIP,) + s.shape, s.dtype) for s in smalls], n, 3 * n, plan)


def _rs_pair_start(tag, grads, after=None):
    n = len(grads)
    g5 = [g.reshape(NCHIP, 2, g.shape[1] // 2, g.shape[2]) for g in grads]
    half = [(g.shape[2], g.shape[3]) for g in g5]
    land = [lax.empty((NCHIP,) + h, g.dtype) for h, g in zip(half, grads)]

    def plan(refs, x, y, c):
        sib = (x, y, 1 - c)
        return [(refs[a].at[:, 1 - c], refs[n + a], sib, refs[n + a]) for a in range(n)]

    send, recv, thru, token = _split_start(f"rs_pair_start_{tag}", g5 + land, n, plan, after)
    return (send, recv, thru, plan, half), token


def _rs_chips_start(tag, state, after, c_idx):
    send, recv, thru, pair_plan, half = state
    n = len(half)
    landed = _split_wait(f"rs_pair_wait_{tag}", thru, send, recv, after, n, pair_plan)
    g5, got = landed[:n], landed[n:]
    part = [_pair_add(f"rs_pair_add_{tag}{a}", g5[a], got[a], c_idx, _tile(half[a][0], 512)) for a in range(n)]
    land = [lax.empty((NCHIP - 1,) + h, BF16) for h in half]

    def plan(refs, x, y, c):
        return [(refs[a].at[2 * px + py], refs[n + a].at[k], (px, py, c), refs[n + a].at[k])
                for a in range(n) for k, (px, py) in enumerate(_chips_of(x, y))]

    send, recv, thru, token = _split_start(f"rs_chips_start_{tag}", part + land, 3 * n, plan)
    return (send, recv, thru, plan, half), token


def _rs_finish(tag, state, after, qc_idx):
    send, recv, thru, plan, half = state
    n = len(half)
    landed = _split_wait(f"rs_chips_wait_{tag}", thru, send, recv, after, 3 * n, plan)
    red = [_chip_add(f"rs_chip_add_{tag}{a}", landed[a], landed[n + a], qc_idx, _tile(half[a][0], 512)) for a in range(n)]

    def share(ins, outs, x, y, c):
        sib = (x, y, 1 - c)
        return [], [(ins[a].at[c], outs[a].at[c], sib, outs[a].at[1 - c], None) for a in range(n)]

    full = _exchange(f"rs_share_{tag}", red, [jax.ShapeDtypeStruct((2,) + h, F32) for h in half], 0, n, share,
                     aliases={a: a for a in range(n)})
    return [f.reshape(2 * h[0], h[1]) for f, h in zip(full, half)]


def _rs_land(tag, state, after, qc_idx):
    send, recv, thru, plan, half = state
    n = len(half)
    landed = _split_wait(f"rs_chips_wait_{tag}", thru, send, recv, after, 3 * n, plan)
    red = [_chip_add(f"rs_chip_add_{tag}{a}", landed[a], landed[n + a], qc_idx, _tile(half[a][0], 512)) for a in range(n)]

    def share(refs, x, y, c):
        sib = (x, y, 1 - c)
        return [(refs[a].at[c], refs[a].at[c], sib, refs[a].at[1 - c]) for a in range(n)]

    send, recv, thru, token = _split_start(f"rs_share_start_{tag}", red, n, share)
    return (send, recv, thru, share, half), token


def _rs_done(tag, state, after):
    send, recv, thru, plan, half = state
    full = _split_wait(f"rs_share_wait_{tag}", thru, send, recv, after, len(half), plan)
    return [f.reshape(2 * h[0], h[1]) for f, h in zip(full, half)]


def _place(name, blk, idx):
    rows, cols = blk.shape

    def body(i_ref, b_ref, o_ref):
        o_ref[...] = b_ref[...]

    return pl.pallas_call(
        body, name=name,
        grid_spec=pltpu.PrefetchScalarGridSpec(
            num_scalar_prefetch=1, grid=(1,),
            in_specs=[pl.BlockSpec((rows, cols), lambda i, d: (0, 0))],
            out_specs=pl.BlockSpec((None, rows, cols), lambda i, d: (d[0], 0, 0))),
        out_shape=jax.ShapeDtypeStruct((8, rows, cols), blk.dtype),
        compiler_params=_cparams("arbitrary"),
    )(idx, blk)


def _pack(name, n_rows, pieces, after=None):
    arrays = [p[1] for p in pieces if p[0] != "zeros"]
    extra = [] if after is None else [after]

    def body(*refs):
        o_ref = refs[-1]
        o_ref[...] = jnp.zeros(o_ref.shape, F32)
        row, ai = 0, 0
        for p in pieces:
            kind = p[0]
            if kind == "zeros":
                row += p[1]
                continue
            ref = refs[len(extra) + ai]
            ai += 1
            if kind == "rows":
                r = ref.shape[0]
                o_ref[row:row + r, :] = ref[...]
            elif kind == "wide":
                r = ref.shape[1] // D
                for j in range(r):
                    o_ref[row + j:row + j + 1, :] = ref[:, j * D:(j + 1) * D]
            elif kind == "slice":
                r = p[3] - p[2]
                o_ref[row:row + r, :] = ref[p[2]:p[3], :]
            elif kind == "heads":
                r = CHUNK
                for h in range(HEADS):
                    o_ref[row:row + r, h * CHUNK:(h + 1) * CHUNK] = ref[h]
            else:
                r = 1
                o_ref[row:row + 1, 0:128] = ref[0:1, :]
            row += r
        assert row == n_rows, (row, n_rows)

    vmem = pl.BlockSpec(memory_space=pltpu.VMEM)
    return pl.pallas_call(
        body, name=name, out_shape=jax.ShapeDtypeStruct((n_rows, D), F32),
        in_specs=[pl.BlockSpec(memory_space=pl.ANY)] * len(extra) + [vmem] * len(arrays), out_specs=vmem,
        compiler_params=pltpu.CompilerParams(vmem_limit_bytes=VMEM_LIMIT),
    )(*extra, *arrays)


def _to_bf16_stack(name, w, q_idx):
    r, cols = w.shape
    half = r // 2
    tr = _tile(half, 256)
    nb = half // tr

    def body(q_ref, w_ref, o_ref):
        o_ref[...] = w_ref[...].astype(BF16)

    return pl.pallas_call(
        body, name=name,
        grid_spec=pltpu.PrefetchScalarGridSpec(
            num_scalar_prefetch=1, grid=(2 * nb,),
            in_specs=[pl.BlockSpec((tr, cols), lambda i, qr: (i, 0))],
            out_specs=pl.BlockSpec((None, None, tr, cols), lambda i, qr: (qr[0], i // nb, i % nb, 0))),
        out_shape=jax.ShapeDtypeStruct((NCHIP, 2, half, cols), BF16),
        compiler_params=_cparams("parallel"),
    )(q_idx, w)


def kernel(x, c, ada_w, ada_b, norm_ffn1, ffn1_w_gate, ffn1_w_up, ffn1_w_down, norm_mix, mix_w_in, mix_b_in, sgu_ln_g, sgu_ln_b, sgu_w_s, sgu_b_s, conv_w, conv_b, conv_ln_g, conv_ln_b, w_branch_a, w_branch_b, w_out, norm_ffn2, ffn2_w_gate, ffn2_w_up, ffn2_w_down, norm_final, loss_target, m_ada_w, m_ada_b, m_norm_ffn1, m_ffn1_w_gate, m_ffn1_w_up, m_ffn1_w_down, m_norm_mix, m_mix_w_in, m_mix_b_in, m_sgu_ln_g, m_sgu_ln_b, m_sgu_w_s, m_sgu_b_s, m_conv_w, m_conv_b, m_conv_ln_g, m_conv_ln_b, m_w_branch_a, m_w_branch_b, m_w_out, m_norm_ffn2, m_ffn2_w_gate, m_ffn2_w_up, m_ffn2_w_down, m_norm_final, v_ada_w, v_ada_b, v_norm_ffn1, v_ffn1_w_gate, v_ffn1_w_up, v_ffn1_w_down, v_norm_mix, v_mix_w_in, v_mix_b_in, v_sgu_ln_g, v_sgu_ln_b, v_sgu_w_s, v_sgu_b_s, v_conv_w, v_conv_b, v_conv_ln_g, v_conv_ln_b, v_w_branch_a, v_w_branch_b, v_w_out, v_norm_ffn2, v_ffn2_w_gate, v_ffn2_w_up, v_ffn2_w_down, v_norm_final):
    xi, yi, ci = lax.axis_index("x"), lax.axis_index("y"), lax.axis_index("c")
    q = 2 * xi + yi
    dev = 4 * xi + 2 * yi + ci
    c_idx = jnp.reshape(ci, (1,)).astype(jnp.int32)
    q_idx = jnp.reshape(q, (1,)).astype(jnp.int32)
    x0 = x[0]
    tgt = loss_target[0]
    qc_idx = jnp.stack([q, ci]).astype(jnp.int32)
    n_ada = ada_w.shape[2]

    big = dict(g1=ffn1_w_gate, u1=ffn1_w_up, d1=ffn1_w_down, win=mix_w_in, wa=w_branch_a, wb=w_branch_b, wo=w_out,
               g2=ffn2_w_gate, u2=ffn2_w_up, d2=ffn2_w_down)
    groups = dict(f1=("d1", "g1", "u1"), mx=("wo", "wa", "wb", "win"), f2=("d2", "g2", "u2"))
    gather_groups = dict(up1=("g1", "u1"), dn1=("d1",), win=("win",), mx=("wa", "wb", "wo"), up2=("g2", "u2"),
                         dn2=("d2",))
    first_stacks = [_to_bf16_stack(f"cast_{k}", big[k][0], q_idx) for k in gather_groups["up1"]]

    c_all = _allgather8("ag_c", jnp.pad(c, ((0, 7), (0, 0))), after=first_stacks)[::8]
    ada_b_mine = lax.dynamic_slice(ada_b, (0, q * n_ada), (1, n_ada))
    mod_part = _ada_fwd("ada_fwd", c_all, ada_w[0], ada_b_mine)
    small_parts = _gather_small("gather_small", [mod_part, jnp.pad(conv_w[0], ((0, 1), (0, 0)))])

    gather_state = {}
    started = [small_parts[0]]
    for tag, keys in gather_groups.items():
        stacks = first_stacks if tag == "up1" else [_to_bf16_stack(f"cast_{k}", big[k][0], q_idx) for k in keys]
        gather_state[tag], token = _gather_start(tag, stacks, started[-1])
        started.append(token)

    mod_all = jnp.transpose(small_parts[0], (1, 0, 2)).reshape(8, N_MOD * D)
    mod = lax.dynamic_slice(mod_all, (dev, 0), (1, N_MOD * D))
    sh1, sc1, g1, sh2, sc2, g2, sh3, sc3 = [(mod, D, k) for k in range(N_MOD - 1)]
    g3 = mod[:, (N_MOD - 1) * D:]
    convw = jnp.transpose(small_parts[1], (1, 0, 2)).reshape(HALO, D)
    W = dict(zip(gather_groups["up1"], _gather_finish("up1", gather_state["up1"], [mod] + started)))

    causal = jnp.tril(jnp.ones((CHUNK, CHUNK), bool))
    wm_f = jnp.where(causal[None], sgu_w_s[0], 0.0)
    wm = wm_f.astype(BF16)
    wmt = jnp.swapaxes(wm_f, 1, 2).astype(BF16)
    bst = jnp.repeat(jnp.transpose(sgu_b_s[0]), CHUNK, axis=1)
    b_in = mix_b_in

    (h1,) = _rowwise_fwd("pre1", _f_pre, [x0], [norm_ffn1, sc1, sh1], [(D, BF16)], 512)
    a1, b1, s1 = _ffn_up("up1", h1, W["g1"], W["u1"])
    W.update(zip(gather_groups["dn1"], _gather_finish("dn1", gather_state["dn1"], s1)))
    f1 = _matmul("down1", s1, W["d1"], "fwd", False, F32)
    pass_win, token = _gather_land("win", gather_state["win"], f1)
    x1, h2 = _rowwise_fwd("res1", _f_res_pre(0.5), [x0, f1], [g1, norm_mix, sc2, sh2],
                          [(D, F32), (D, BF16)], 256, after=token)
    W.update(zip(gather_groups["win"], _gather_done("win", pass_win, x1)))
    proj = _matmul("in_proj", h2, W["win"], "fwd", True, F32)
    pass_mx, token = _gather_land("mx", gather_state["mx"], proj)
    ya = _sgu_fwd("sgu", proj, b_in, sgu_ln_g, sgu_ln_b, wm, bst, 256)
    z1 = _conv_fwd("conv", proj, b_in, convw, conv_b)
    (z3,) = _rowwise_fwd("conv_ln", _f_lnsilu, [z1], [conv_ln_g, conv_ln_b], [(D, BF16)], 512, after=token)
    W.update(zip(gather_groups["mx"], _gather_done("mx", pass_mx, z3)))
    y_a = _matmul("branch_a", ya, W["wa"], "fwd", False, F32)
    y_b = _matmul("branch_b", z3, W["wb"], "fwd", False, F32)
    pass_up2, token = _gather_land("up2", gather_state["up2"], y_b)
    (merged,) = _rowwise_fwd("merge", _f_merge, [(proj, 2 * D, 2), y_a, y_b], [(b_in, 2 * D, 2)],
                             [(D, BF16)], 256, after=token)
    y = _matmul("out_proj", merged, W["wo"], "fwd", False, F32)
    x2, h3 = _rowwise_fwd("res2", _f_res_pre(1.0), [x1, y], [g2, norm_ffn2, sc3, sh3], [(D, F32), (D, BF16)], 512)
    W.update(zip(gather_groups["up2"], _gather_done("up2", pass_up2, x2)))
    a3, b3, s3 = _ffn_up("up2", h3, W["g2"], W["u2"])
    W.update(zip(gather_groups["dn2"], _gather_finish("dn2", gather_state["dn2"], s3)))
    f3 = _matmul("down2", s3, W["d2"], "fwd", False, F32)
    dx2, df3, dg3, dnwf, loss_blk = _final("final", x2, f3, tgt, g3, norm_final.reshape(1, D), 512)

    G = {}

    def ffn_bwd(tag, df, s, a, b, h, wg, wu, wd, after=None):
        da, db = _dgrad_glu(f"d_down{tag}", df, wd, a, b, after)
        G["d" + tag] = _matmul(f"w_down{tag}", s, df, "wgrad", False, BF16)
        dh = _dgrad_pair(f"d_gate_up{tag}", da, db, wg, wu)
        G["g" + tag] = _matmul(f"w_gate{tag}", h, da, "wgrad", True, BF16)
        G["u" + tag] = _matmul(f"w_up{tag}", h, db, "wgrad", True, BF16)
        return dh

    reduced = {}
    dh3 = ffn_bwd("2", df3, s3, a3, b3, h3, W["g2"], W["u2"], W["d2"])
    pair_f2, token = _rs_pair_start("f2", [G[k] for k in groups["f2"]])
    (dx1, dy), (dg2, dnw3, dsc3, dsh3) = _rowwise_bwd(
        "res2_b", _f_res_pre(1.0), [x1, y], [g2, norm_ffn2, sc3, sh3], [dx2, dh3], [F32, BF16], 256, after=token)
    dmerged = _matmul("d_out", dy, W["wo"], "dgrad", False, F32)
    G["wo"] = _matmul("w_out", merged, dy, "wgrad", False, BF16)
    rs_f2, token = _rs_chips_start("f2", pair_f2, [dmerged, G["wo"]], c_idx)
    (dgab, dy_a, dy_b), (dbgab,) = _rowwise_bwd(
        "merge_b", _f_merge, [(proj, 2 * D, 2), y_a, y_b], [(b_in, 2 * D, 2)], [dmerged],
        [BF16, BF16, BF16], 256, after=token)
    dya = _matmul("d_a", dy_a, W["wa"], "dgrad", False, F32)
    dz3 = _matmul("d_b", dy_b, W["wb"], "dgrad", False, F32)
    G["wa"] = _matmul("w_a", ya, dy_a, "wgrad", False, BF16)
    G["wb"] = _matmul("w_b", z3, dy_b, "wgrad", False, BF16)
    (dz1,), (dclg, dclb) = _rowwise_bwd("conv_ln_b", _f_lnsilu, [z1], [conv_ln_g, conv_ln_b], [dz3], [F32], 512)
    dcv, dcg, conv_small = _conv_bwd("conv_b", dz1, proj, b_in, convw)
    duv, dbuv, dlng, dlnb, dws, dbs = _sgu_bwd("sgu_b", proj, b_in, sgu_ln_g, sgu_ln_b, wm, wmt, bst, dya, 256)
    dproj = jnp.concatenate([duv, dcv, dcg, dgab], axis=1)
    dh2 = _matmul("d_in", dproj, W["win"], "dgrad", True, F32)
    G["win"] = _matmul("w_in", h2, dproj, "wgrad", True, BF16)
    pair_mx, token = _rs_pair_start("mx", [G[k] for k in groups["mx"]])
    share_f2, token2 = _rs_land("f2", rs_f2, [dh2, token], qc_idx)

    dev_idx = jnp.reshape(dev, (1,)).astype(jnp.int32)
    early = _pack("pack_early", HALO + CHUNK, [("slice", conv_small, 0, HALO), ("heads", dws)])
    ag_early, token3 = _allgather_start("ag_early", _place("place_early", early, dev_idx), token2)

    (dx0, df1), (dg1, dnw2, dsc2, dsh2) = _rowwise_bwd(
        "res1_b", _f_res_pre(0.5), [x0, f1], [g1, norm_mix, sc2, sh2],
        [dx1, dh2], [F32, BF16], 256, after=[token, token3])
    rs_mx, token = _rs_chips_start("mx", pair_mx, df1, c_idx)
    dh1 = ffn_bwd("1", df1, s1, a1, b1, h1, W["g1"], W["u1"], W["d1"], after=token)
    pair_f1, token = _rs_pair_start("f1", [G[k] for k in groups["f1"]])
    reduced.update(zip(groups["f2"], _rs_done("f2", share_f2, dh1)))
    share_mx, token2 = _rs_land("mx", rs_mx, [dh1, token], qc_idx)
    (grad_x,), (dnw1, dsc1, dsh1) = _rowwise_bwd(
        "pre1_b", _f_pre_keep, [x0], [norm_ffn1, sc1, sh1], [dx0, dh1], [F32], 512, after=[token, token2])
    reduced.update(zip(groups["mx"], _rs_done("mx", share_mx, grad_x)))
    tot_early = _sum8("sum_early", _allgather_done("ag_early", ag_early, grad_x), HALO)

    dbs_row = jnp.transpose(dbs[:, ::CHUNK]).reshape(1, D)
    small = _pack("pack_small", HALO, [
        ("rows", dnw1), ("rows", dnw2), ("wide", dbuv), ("slice", conv_small, 33, 34), ("slice", conv_small, 34, 35),
        ("wide", dbgab), ("rows", dlng), ("rows", dlnb), ("rows", dbs_row), ("slice", conv_small, 32, 33),
        ("rows", dclg), ("rows", dclb), ("rows", dnw3), ("rows", dnwf),
        ("rows", dsh1), ("rows", dsc1), ("rows", dg1), ("rows", dsh2), ("rows", dsc2), ("rows", dg2),
        ("rows", dsh3), ("rows", dsc3), ("rows", dg3), ("lanes", loss_blk), ("zeros", 6)])
    every = _allgather8("ag_small", small)
    rs_f1, token = _rs_chips_start("f1", pair_f1, every, c_idx)
    every = every.reshape(8, HALO, D)
    tot = _sum8("sum_small", every, HALO, after=token)
    loss = tot[25, 0]
    dmod_all = every[:, 16:16 + N_MOD, :].reshape(8, N_MOD * D)
    dmod_mine = lax.dynamic_slice(dmod_all, (0, q * n_ada), (8, n_ada))
    g_ada_w = _ada_wgrad("ada_wgrad", jnp.transpose(c_all), dmod_mine)

    given = dict(ada_w=(ada_w, m_ada_w, v_ada_w), ada_b=(ada_b, m_ada_b, v_ada_b),
                 norm_ffn1=(norm_ffn1, m_norm_ffn1, v_norm_ffn1),
                 ffn1_w_gate=(ffn1_w_gate, m_ffn1_w_gate, v_ffn1_w_gate), ffn1_w_up=(ffn1_w_up, m_ffn1_w_up, v_ffn1_w_up),
                 ffn1_w_down=(ffn1_w_down, m_ffn1_w_down, v_ffn1_w_down), norm_mix=(norm_mix, m_norm_mix, v_norm_mix),
                 mix_w_in=(mix_w_in, m_mix_w_in, v_mix_w_in), mix_b_in=(mix_b_in, m_mix_b_in, v_mix_b_in),
                 sgu_ln_g=(sgu_ln_g, m_sgu_ln_g, v_sgu_ln_g), sgu_ln_b=(sgu_ln_b, m_sgu_ln_b, v_sgu_ln_b),
                 sgu_w_s=(sgu_w_s, m_sgu_w_s, v_sgu_w_s), sgu_b_s=(sgu_b_s, m_sgu_b_s, v_sgu_b_s),
                 conv_w=(conv_w, m_conv_w, v_conv_w), conv_b=(conv_b, m_conv_b, v_conv_b),
                 conv_ln_g=(conv_ln_g, m_conv_ln_g, v_conv_ln_g), conv_ln_b=(conv_ln_b, m_conv_ln_b, v_conv_ln_b),
                 w_branch_a=(w_branch_a, m_w_branch_a, v_w_branch_a), w_branch_b=(w_branch_b, m_w_branch_b, v_w_branch_b),
                 w_out=(w_out, m_w_out, v_w_out), norm_ffn2=(norm_ffn2, m_norm_ffn2, v_norm_ffn2),
                 ffn2_w_gate=(ffn2_w_gate, m_ffn2_w_gate, v_ffn2_w_gate), ffn2_w_up=(ffn2_w_up, m_ffn2_w_up, v_ffn2_w_up),
                 ffn2_w_down=(ffn2_w_down, m_ffn2_w_down, v_ffn2_w_down), norm_final=(norm_final, m_norm_final, v_norm_final))
    out = {}

    def big_update(name, g2d):
        w, m, v = given[name]
        shp = w.shape
        res = _adamw(f"adamw_{name}", w.reshape(g2d.shape), g2d, m.reshape(g2d.shape), v.reshape(g2d.shape),
                     _tile(g2d.shape[0], 512 if g2d.shape[1] <= D else 256))
        out[name] = tuple(t.reshape(shp) for t in res)

    big_names = dict(g1="ffn1_w_gate", u1="ffn1_w_up", d1="ffn1_w_down", win="mix_w_in", wa="w_branch_a",
                     wb="w_branch_b", wo="w_out", g2="ffn2_w_gate", u2="ffn2_w_up", d2="ffn2_w_down")
    for key in groups["f2"] + groups["mx"]:
        big_update(big_names[key], reduced[key])
    big_update("ada_w", g_ada_w)

    small_rows = [("norm_ffn1", 0, 1), ("norm_mix", 1, 1), ("mix_b_in", 2, 6), ("sgu_ln_g", 8, 1), ("sgu_ln_b", 9, 1),
                  ("sgu_b_s", 10, 1), ("conv_b", 11, 1), ("conv_ln_g", 12, 1), ("conv_ln_b", 13, 1),
                  ("norm_ffn2", 14, 1), ("norm_final", 15, 1), ("ada_b", 16, N_MOD)]

    def pack_rows(which, label):
        pieces = []
        for nm, _, r in small_rows:
            t = given[nm][which]
            pieces.append(("wide", t) if r > 1 else ("rows", t.reshape(1, D)))
        return _pack(f"pack_{label}", HALO, pieces + [("zeros", HALO - 16 - N_MOD)])

    small_res = _adamw("adamw_small", pack_rows(0, "w"), tot, pack_rows(1, "m"), pack_rows(2, "v"), HALO)
    for nm, r0, r in small_rows:
        shp = given[nm][0].shape
        out[nm] = tuple(t[r0:r0 + r].reshape(shp) for t in small_res)
    n_cw = conv_w.shape[2]
    g_cw = lax.dynamic_slice(tot_early, (0, q * n_cw), (HALO, n_cw))
    pad_cw = lambda t: jnp.pad(t[0], ((0, 1), (0, 0)))
    cw_res = _adamw("adamw_conv_w", pad_cw(conv_w), g_cw, pad_cw(m_conv_w), pad_cw(v_conv_w), HALO)
    out["conv_w"] = tuple(t[:KCONV][None] for t in cw_res)
    g_ws = jnp.transpose(tot_early[HALO:].reshape(CHUNK, HEADS, CHUNK), (1, 0, 2)).reshape(HEADS * CHUNK, CHUNK)
    flat_ws = lambda t: t.reshape(HEADS * CHUNK, CHUNK)
    ws_res = _adamw("adamw_sgu_w_s", flat_ws(sgu_w_s), g_ws, flat_ws(m_sgu_w_s), flat_ws(v_sgu_w_s), 512)
    out["sgu_w_s"] = tuple(t.reshape(sgu_w_s.shape) for t in ws_res)

    done = [out[n][1] for n in ("ada_w", "sgu_w_s", "conv_w", "ada_b")] + [out[big_names[k]][1] for k in groups["f2"] + groups["mx"]]
    reduced.update(zip(groups["f1"], _rs_finish("f1", rs_f1, done, qc_idx)))
    for key in groups["f1"]:
        big_update(big_names[key], reduced[key])

    order = ["ada_w", "ada_b", "norm_ffn1", "ffn1_w_gate", "ffn1_w_up", "ffn1_w_down", "norm_mix", "mix_w_in", "mix_b_in",
             "sgu_ln_g", "sgu_ln_b", "sgu_w_s", "sgu_b_s", "conv_w", "conv_b", "conv_ln_g", "conv_ln_b", "w_branch_a",
             "w_branch_b", "w_out", "norm_ffn2", "ffn2_w_gate", "ffn2_w_up", "ffn2_w_down", "norm_final"]
    return (loss, grad_x[None], *[out[n][0] for n in order], *[out[n][1] for n in order],
            *[out[n][2] for n in order], *[out[n][3] for n in order])
```

```python
import functools

import jax
import jax.numpy as jnp
from jax import lax
from jax.experimental import pallas as pl
from jax.experimental.pallas import tpu as pltpu

F32 = jnp.float32
BF16 = jnp.bfloat16
D = 1024
HEADS = 8
CHUNK = 128
KCONV = 31
HALO = 32
N_MOD = 9
EPS = 1e-6
NCHIP = 4
VMEM_LIMIT = 56 * 1024 * 1024
MESH = pl.DeviceIdType.MESH

ADAM_LR = 0.001
ADAM_B1 = 0.9
ADAM_B2 = 0.999
ADAM_EPS = 1e-08
ADAM_WD = 0.01
ADAM_STEP = 10

NN = (((1,), (0,)), ((), ()))
NT = (((1,), (1,)), ((), ()))
TN = (((0,), (0,)), ((), ()))


def _cparams(*sem):
    return pltpu.CompilerParams(dimension_semantics=sem or None, vmem_limit_bytes=VMEM_LIMIT)


def _rms(x, w):
    return x * lax.rsqrt(jnp.mean(x * x, axis=-1, keepdims=True) + EPS) * w


def _ln(x, g, b):
    mu = jnp.mean(x, axis=-1, keepdims=True)
    xc = x - mu
    var = jnp.mean(xc * xc, axis=-1, keepdims=True)
    return xc * lax.rsqrt(var + EPS) * g + b


def _silu(x):
    return x * jax.nn.sigmoid(x)


def _gelu(x):
    return x * (0.5 * (1.0 + jnp.tanh(0.7978845608028654 * (x + 0.044715 * (x * x * x)))))


def _f_pre(x, nw, sc, sh):
    return _rms(x, nw) * (1.0 + sc) + sh


def _f_pre_keep(x, nw, sc, sh):
    return x, _f_pre(x, nw, sc, sh)


def _f_res_pre(scale):
    def fn(x, f, g, nw, sc, sh):
        xn = x + (scale * g) * f
        return xn, _f_pre(xn, nw, sc, sh)
    return fn


def _f_glu(a, b):
    return _silu(a) * b


def _f_merge(gab, ya, yb, bgab):
    g = gab + bgab
    return jax.nn.sigmoid(g[:, :D]) * ya + jax.nn.sigmoid(g[:, D:]) * yb


def _f_lnsilu(z, g, b):
    return _silu(_ln(z, g, b))


def _f_mixa(uv, buv, g, b):
    t = uv + buv
    return _gelu(t[:, :D]), _ln(_gelu(t[:, D:]), g, b)


def _col(t):
    return t if isinstance(t, tuple) else (t, t.shape[-1], 0)


def _afters(after):
    if after is None:
        return []
    return list(after) if isinstance(after, (list, tuple)) else [after]


def _rowwise_fwd(name, fn, rows, params, outs, tr, after=None):
    rows = [_col(r) for r in rows]
    params = [_col(p) for p in params]
    extra = _afters(after)
    n_rows = rows[0][0].shape[0]
    nin = len(rows) + len(params)

    def body(*refs):
        res = fn(*[r[...].astype(F32) for r in refs[:nin]])
        res = res if isinstance(res, tuple) else (res,)
        for o, v in zip(refs[nin + len(extra):], res, strict=True):
            o[...] = v.astype(o.dtype)

    return pl.pallas_call(
        body, name=name, grid=(n_rows // tr,),
        in_specs=[pl.BlockSpec((tr, w), functools.partial(lambda cb, i: (i, cb), cb)) for _, w, cb in rows]
        + [pl.BlockSpec((1, w), functools.partial(lambda cb, i: (0, cb), cb)) for _, w, cb in params]
        + [pl.BlockSpec(memory_space=pl.ANY)] * len(extra),
        out_specs=[pl.BlockSpec((tr, w), lambda i: (i, 0)) for w, _ in outs],
        out_shape=[jax.ShapeDtypeStruct((n_rows, w), dt) for w, dt in outs],
        compiler_params=_cparams("parallel"),
    )(*[r[0] for r in rows], *[p[0] for p in params], *extra)


def _rowwise_bwd(name, fn, rows, params, cots, row_grads, tr, after=None):
    rows = [_col(r) for r in rows]
    params = [_col(p) for p in params]
    cots = [_col(t) for t in cots]
    extra = _afters(after)
    n_rows = rows[0][0].shape[0]
    nr, npar, nc = len(rows), len(params), len(cots)
    nin = nr + npar + nc
    n_rg = sum(dt is not None for dt in row_grads)

    def body(*refs):
        i = pl.program_id(0)
        prim = [r[...].astype(F32) for r in refs[:nr + npar]]
        ct = tuple(r[...].astype(F32) for r in refs[nr + npar:nin])
        _, vjp = jax.vjp(fn, *prim)
        g = vjp(ct if nc > 1 else ct[0])
        outs = refs[nin + len(extra):]
        oi = 0
        for j, dt in enumerate(row_grads):
            if dt is not None:
                outs[oi][...] = g[j].astype(dt)
                oi += 1
        for j in range(npar):
            acc = outs[n_rg + j]

            @pl.when(i == 0)
            def _(acc=acc):
                acc[...] = jnp.zeros(acc.shape, F32)

            acc[...] += g[nr + j]

    res = pl.pallas_call(
        body, name=name, grid=(n_rows // tr,),
        in_specs=[pl.BlockSpec((tr, w), functools.partial(lambda cb, i: (i, cb), cb)) for _, w, cb in rows]
        + [pl.BlockSpec((1, w), functools.partial(lambda cb, i: (0, cb), cb)) for _, w, cb in params]
        + [pl.BlockSpec((tr, w), functools.partial(lambda cb, i: (i, cb), cb)) for _, w, cb in cots]
        + [pl.BlockSpec(memory_space=pl.ANY)] * len(extra),
        out_specs=[pl.BlockSpec((tr, rows[j][1]), lambda i: (i, 0)) for j, dt in enumerate(row_grads) if dt is not None]
        + [pl.BlockSpec((1, w), lambda i: (0, 0)) for _, w, _ in params],
        out_shape=[jax.ShapeDtypeStruct((n_rows, rows[j][1]), dt) for j, dt in enumerate(row_grads) if dt is not None]
        + [jax.ShapeDtypeStruct((1, w), F32) for _, w, _ in params],
        compiler_params=_cparams("arbitrary"),
    )(*[r[0] for r in rows], *[p[0] for p in params], *[t[0] for t in cots], *extra)
    return res[:n_rg], res[n_rg:]


def _final(name, x, f, tgt, g, nw, tr):
    n_rows = x.shape[0]

    def body(x_ref, f_ref, t_ref, g_ref, nw_ref, dx_ref, df_ref, dg_ref, dnw_ref, loss_ref):
        i = pl.program_id(0)
        tg = t_ref[...]

        def fn(xv, fv, gv, nwv):
            e = _rms(xv + (0.5 * gv) * fv, nwv) - tg
            return 0.5 * jnp.mean(e * e, axis=-1, keepdims=True)

        per_row, vjp = jax.vjp(fn, x_ref[...], f_ref[...], g_ref[...], nw_ref[...])
        dx, df, dg, dnw = vjp(jnp.ones_like(per_row))
        dx_ref[...] = dx
        df_ref[...] = df.astype(df_ref.dtype)

        @pl.when(i == 0)
        def _():
            dg_ref[...] = jnp.zeros(dg_ref.shape, F32)
            dnw_ref[...] = jnp.zeros(dnw_ref.shape, F32)
            loss_ref[...] = jnp.zeros(loss_ref.shape, F32)

        dg_ref[...] += dg
        dnw_ref[...] += dnw
        loss_ref[...] += jnp.broadcast_to(jnp.sum(per_row, axis=0, keepdims=True), loss_ref.shape)

    row = pl.BlockSpec((tr, D), lambda i: (i, 0))
    par = pl.BlockSpec((1, D), lambda i: (0, 0))
    return pl.pallas_call(
        body, name=name, grid=(n_rows // tr,),
        in_specs=[row, row, row, par, par],
        out_specs=[row, row, par, par, pl.BlockSpec((8, 128), lambda i: (0, 0))],
        out_shape=[jax.ShapeDtypeStruct((n_rows, D), F32), jax.ShapeDtypeStruct((n_rows, D), BF16),
                   jax.ShapeDtypeStruct((1, D), F32), jax.ShapeDtypeStruct((1, D), F32),
                   jax.ShapeDtypeStruct((8, 128), F32)],
        compiler_params=_cparams("arbitrary"),
    )(x, f, tgt, g, nw)


def _tile(dim, pref):
    t = min(dim, pref)
    while dim % t:
        t //= 2
    return t


def _matmul(name, a, b, kind, col, out_dtype, add=None, after=None):
    if kind == "wgrad":
        m, kg = a.shape
        ng = b.shape[1]
        r, c = (kg, ng // NCHIP) if col else (kg // NCHIP, ng)
        tm, tn, tk = _tile(r, 1024), _tile(c, 1536), _tile(m, 2048)
        grid = (kg // tm, ng // tn, m // tk)
        a_spec = pl.BlockSpec((tk, tm), lambda i, j, k: (k, i))
        b_spec = pl.BlockSpec((tk, tn), lambda i, j, k: (k, j))
        if col:
            o_spec = pl.BlockSpec((None, tm, tn), lambda i, j, k: (j // (c // tn), i, j % (c // tn)))
        else:
            o_spec = pl.BlockSpec((None, tm, tn), lambda i, j, k: (i // (r // tm), i % (r // tm), j))
        out_shape = jax.ShapeDtypeStruct((NCHIP, r, c), out_dtype)
        dims = TN
    else:
        _, r, c = b.shape
        m = a.shape[0]
        kg, ng = (r, NCHIP * c) if col else (NCHIP * r, c)
        tm = _tile(m, 1024)
        if kind == "fwd":
            tn, tk = _tile(c, 1536), _tile(r, 2048)
            grid = (m // tm, ng // tn, kg // tk)
            a_spec = pl.BlockSpec((tm, tk), lambda i, j, k: (i, k))
            if col:
                b_spec = pl.BlockSpec((None, tk, tn), lambda i, j, k: (j // (c // tn), k, j % (c // tn)))
            else:
                b_spec = pl.BlockSpec((None, tk, tn), lambda i, j, k: (k // (r // tk), k % (r // tk), j))
            out_shape = jax.ShapeDtypeStruct((m, ng), out_dtype)
            dims = NN
        else:
            tn, tk = _tile(r, 1024), _tile(c, 2048)
            grid = (m // tm, kg // tn, ng // tk)
            a_spec = pl.BlockSpec((tm, tk), lambda i, j, k: (i, k))
            if col:
                b_spec = pl.BlockSpec((None, tn, tk), lambda i, j, k: (k // (c // tk), j, k % (c // tk)))
            else:
                b_spec = pl.BlockSpec((None, tn, tk), lambda i, j, k: (j // (r // tn), j % (r // tn), k))
            out_shape = jax.ShapeDtypeStruct((m, kg), out_dtype)
            dims = NT
        o_spec = pl.BlockSpec((tm, tn), lambda i, j, k: (i, j))
    nk = grid[2]
    has_add = add is not None
    extra = [] if after is None else [after]

    def body(*refs):
        a_ref, b_ref = refs[0], refs[1]
        o_ref = refs[2 + has_add + len(extra)]
        k = pl.program_id(2)
        p = lax.dot_general(a_ref[...].astype(BF16), b_ref[...].astype(BF16), dims, preferred_element_type=F32)

        def finish(r_):
            if has_add:
                r_ = r_ + refs[2][...]
            o_ref[...] = r_.astype(o_ref.dtype)

        if nk == 1:
            finish(p)
        else:
            acc_ref = refs[-1]

            @pl.when(k == 0)
            def _():
                acc_ref[...] = p

            @pl.when(jnp.logical_and(k > 0, k < nk - 1))
            def _():
                acc_ref[...] += p

            @pl.when(k == nk - 1)
            def _():
                finish(acc_ref[...] + p)

    in_specs = [a_spec, b_spec] + ([pl.BlockSpec((tm, tn), lambda i, j, k: (i, j))] if has_add else [])
    in_specs += [pl.BlockSpec(memory_space=pl.ANY)] * len(extra)
    args = (a, b) + ((add,) if has_add else ()) + tuple(extra)
    return pl.pallas_call(
        body, name=name, grid=grid, in_specs=in_specs, out_specs=o_spec, out_shape=out_shape,
        scratch_shapes=[pltpu.VMEM((tm, tn), F32)] if nk > 1 else [],
        compiler_params=_cparams("parallel", "parallel", "arbitrary"),
    )(*args)


def _ffn_up(name, h, wg, wu):
    m, k = h.shape
    _, _, c = wg.shape
    n = NCHIP * c
    tm, tn = _tile(m, 1024), _tile(c, 1024)

    def body(h_ref, wg_ref, wu_ref, a_ref, b_ref, s_ref):
        hv = h_ref[...]
        a = jnp.dot(hv, wg_ref[...], preferred_element_type=F32)
        b = jnp.dot(hv, wu_ref[...], preferred_element_type=F32)
        a_ref[...] = a
        b_ref[...] = b
        s_ref[...] = (_silu(a) * b).astype(s_ref.dtype)

    w_spec = pl.BlockSpec((None, k, tn), lambda i, j: (j // (c // tn), 0, j % (c // tn)))
    o_spec = pl.BlockSpec((tm, tn), lambda i, j: (i, j))
    return pl.pallas_call(
        body, name=name, grid=(m // tm, n // tn),
        in_specs=[pl.BlockSpec((tm, k), lambda i, j: (i, 0)), w_spec, w_spec],
        out_specs=[o_spec, o_spec, o_spec],
        out_shape=[jax.ShapeDtypeStruct((m, n), F32), jax.ShapeDtypeStruct((m, n), F32),
                   jax.ShapeDtypeStruct((m, n), BF16)],
        compiler_params=_cparams("parallel", "parallel"),
    )(h, wg, wu)


def _dgrad_pair(name, da, db, wg, wu):
    m = da.shape[0]
    _, r, c = wg.shape
    tm, tn = _tile(m, 1024), _tile(r, 1024)
    n1 = NCHIP

    def body(da_ref, db_ref, wg_ref, wu_ref, o_ref, acc_ref):
        k = pl.program_id(2)

        @pl.when(k == 0)
        def _():
            acc_ref[...] = lax.dot_general(da_ref[...], wg_ref[...], NT, preferred_element_type=F32)

        @pl.when(jnp.logical_and(k > 0, k < n1))
        def _():
            acc_ref[...] += lax.dot_general(da_ref[...], wg_ref[...], NT, preferred_element_type=F32)

        @pl.when(jnp.logical_and(k >= n1, k < 2 * n1 - 1))
        def _():
            acc_ref[...] += lax.dot_general(db_ref[...], wu_ref[...], NT, preferred_element_type=F32)

        @pl.when(k == 2 * n1 - 1)
        def _():
            o_ref[...] = acc_ref[...] + lax.dot_general(db_ref[...], wu_ref[...], NT, preferred_element_type=F32)

    first = lambda k: jnp.minimum(k, n1 - 1)
    second = lambda k: jnp.maximum(k - n1, 0)
    return pl.pallas_call(
        body, name=name, grid=(m // tm, r // tn, 2 * n1),
        in_specs=[pl.BlockSpec((tm, c), lambda i, j, k: (i, first(k))),
                  pl.BlockSpec((tm, c), lambda i, j, k: (i, second(k))),
                  pl.BlockSpec((None, tn, c), lambda i, j, k: (first(k), j, 0)),
                  pl.BlockSpec((None, tn, c), lambda i, j, k: (second(k), j, 0))],
        out_specs=pl.BlockSpec((tm, tn), lambda i, j, k: (i, j)),
        out_shape=jax.ShapeDtypeStruct((m, r), F32),
        scratch_shapes=[pltpu.VMEM((tm, tn), F32)],
        compiler_params=_cparams("parallel", "parallel", "arbitrary"),
    )(da, db, wg, wu)


def _dgrad_glu(name, df, wd, a, b, after=None):
    m = df.shape[0]
    _, r, c = wd.shape
    tm, tn = _tile(m, 512), _tile(r, 1024)
    extra = [] if after is None else [after]

    def body(df_ref, w_ref, a_ref, b_ref, *rest):
        da_ref, db_ref = rest[-2], rest[-1]
        ds = lax.dot_general(df_ref[...], w_ref[...], NT, preferred_element_type=F32)
        _, vjp = jax.vjp(_f_glu, a_ref[...], b_ref[...])
        da, db = vjp(ds)
        da_ref[...] = da.astype(da_ref.dtype)
        db_ref[...] = db.astype(db_ref.dtype)

    t_spec = pl.BlockSpec((tm, tn), lambda i, j: (i, j))
    sds = jax.ShapeDtypeStruct((m, NCHIP * r), BF16)
    return pl.pallas_call(
        body, name=name, grid=(m // tm, NCHIP * r // tn),
        in_specs=[pl.BlockSpec((tm, c), lambda i, j: (i, 0)),
                  pl.BlockSpec((None, tn, c), lambda i, j: (j // (r // tn), j % (r // tn), 0)), t_spec, t_spec]
        + [pl.BlockSpec(memory_space=pl.ANY)] * len(extra),
        out_specs=[t_spec, t_spec], out_shape=[sds, sds],
        compiler_params=_cparams("parallel", "parallel"),
    )(df, wd, a, b, *extra)


def _sgu_fwd(name, proj, b_in, lng, lnb, wm, bst, tr):
    n_rows = proj.shape[0]

    def body(uv_ref, buv_ref, g_ref, b_ref, wm_ref, bst_ref, ya_ref):
        u, vln = _f_mixa(uv_ref[...], buv_ref[...], g_ref[...], b_ref[...])
        vb = vln.astype(BF16)
        for ck in range(tr // CHUNK):
            rs = slice(ck * CHUNK, (ck + 1) * CHUNK)
            for h in range(HEADS):
                cs = slice(h * CHUNK, (h + 1) * CHUNK)
                vm = jnp.dot(wm_ref[h], vb[rs, cs], preferred_element_type=F32) + bst_ref[:, cs]
                ya_ref[rs, cs] = (u[rs, cs] * vm).astype(ya_ref.dtype)

    return pl.pallas_call(
        body, name=name, grid=(n_rows // tr,),
        in_specs=[pl.BlockSpec((tr, 2 * D), lambda i: (i, 0)), pl.BlockSpec((1, 2 * D), lambda i: (0, 0)),
                  pl.BlockSpec((1, D), lambda i: (0, 0)), pl.BlockSpec((1, D), lambda i: (0, 0)),
                  pl.BlockSpec((HEADS, CHUNK, CHUNK), lambda i: (0, 0, 0)),
                  pl.BlockSpec((CHUNK, D), lambda i: (0, 0))],
        out_specs=pl.BlockSpec((tr, D), lambda i: (i, 0)),
        out_shape=jax.ShapeDtypeStruct((n_rows, D), BF16),
        compiler_params=_cparams("parallel"),
    )(proj, b_in, lng, lnb, wm, bst)


def _sgu_bwd(name, proj, b_in, lng, lnb, wm, wmt, bst, dya, tr):
    n_rows = proj.shape[0]

    def body(uv_ref, buv_ref, g_ref, b_ref, wm_ref, wmt_ref, bst_ref, dya_ref,
             duv_ref, dbuv_ref, dg_ref, db_ref, dws_ref, dbs_ref, du_s, dvln_s):
        i = pl.program_id(0)

        @pl.when(i == 0)
        def _():
            dbuv_ref[...] = jnp.zeros(dbuv_ref.shape, F32)
            dg_ref[...] = jnp.zeros(dg_ref.shape, F32)
            db_ref[...] = jnp.zeros(db_ref.shape, F32)
            dws_ref[...] = jnp.zeros(dws_ref.shape, F32)
            dbs_ref[...] = jnp.zeros(dbs_ref.shape, F32)

        (u, vln), vjp = jax.vjp(_f_mixa, uv_ref[...], buv_ref[...], g_ref[...], b_ref[...])
        vb = vln.astype(BF16)
        dya_v = dya_ref[...].astype(F32)
        tpos = lax.broadcasted_iota(jnp.int32, (CHUNK, CHUNK), 0)
        spos = lax.broadcasted_iota(jnp.int32, (CHUNK, CHUNK), 1)
        causal = (tpos >= spos).astype(F32)
        for ck in range(tr // CHUNK):
            rs = slice(ck * CHUNK, (ck + 1) * CHUNK)
            for h in range(HEADS):
                cs = slice(h * CHUNK, (h + 1) * CHUNK)
                vbh = vb[rs, cs]
                vm = jnp.dot(wm_ref[h], vbh, preferred_element_type=F32) + bst_ref[:, cs]
                dyb = dya_v[rs, cs]
                du_s[rs, cs] = dyb * vm
                dvm = dyb * u[rs, cs]
                dvmb = dvm.astype(BF16)
                dvln_s[rs, cs] = jnp.dot(wmt_ref[h], dvmb, preferred_element_type=F32)
                dws_ref[h] += causal * lax.dot_general(dvmb, vbh, NT, preferred_element_type=F32)
                dbs_ref[:, cs] += jnp.broadcast_to(jnp.sum(dvm, axis=1, keepdims=True), (CHUNK, CHUNK))
        duv, dbuv, dg, db = vjp((du_s[...], dvln_s[...]))
        duv_ref[...] = duv.astype(duv_ref.dtype)
        dbuv_ref[...] += dbuv
        dg_ref[...] += dg
        db_ref[...] += db

    par = pl.BlockSpec((1, D), lambda i: (0, 0))
    par2 = pl.BlockSpec((1, 2 * D), lambda i: (0, 0))
    w_spec = pl.BlockSpec((HEADS, CHUNK, CHUNK), lambda i: (0, 0, 0))
    b_spec = pl.BlockSpec((CHUNK, D), lambda i: (0, 0))
    return pl.pallas_call(
        body, name=name, grid=(n_rows // tr,),
        in_specs=[pl.BlockSpec((tr, 2 * D), lambda i: (i, 0)), par2, par, par, w_spec, w_spec, b_spec,
                  pl.BlockSpec((tr, D), lambda i: (i, 0))],
        out_specs=[pl.BlockSpec((tr, 2 * D), lambda i: (i, 0)), par2, par, par, w_spec, b_spec],
        out_shape=[jax.ShapeDtypeStruct((n_rows, 2 * D), BF16), jax.ShapeDtypeStruct((1, 2 * D), F32),
                   jax.ShapeDtypeStruct((1, D), F32), jax.ShapeDtypeStruct((1, D), F32),
                   jax.ShapeDtypeStruct((HEADS, CHUNK, CHUNK), F32), jax.ShapeDtypeStruct((CHUNK, D), F32)],
        scratch_shapes=[pltpu.VMEM((tr, D), F32), pltpu.VMEM((tr, D), F32)],
        compiler_params=_cparams("arbitrary"),
    )(proj, b_in, lng, lnb, wm, wmt, bst, dya)


CT = 256
RB = 128
CV0 = 2 * D // CT
CG0 = 3 * D // CT


def _shift_bank(win):
    return [win] + [pltpu.roll(win, RB + HALO - r, 0) for r in range(1, 8)]


def _shifted(bank, sh):
    lo = 8 * (sh // 8)
    return bank[sh % 8][lo:lo + RB]


def _conv_fwd(name, proj, b_in, w, cb):
    n_rows = proj.shape[0]

    def body(cv_ref, cg_ref, bcv_ref, bcg_ref, w_ref, cb_ref, z1_ref, zp_ref):
        zp_ref[0:HALO, :] = jnp.zeros((HALO, CT), F32)
        zp_ref[HALO:, :] = (cv_ref[...] + bcv_ref[...]) * jax.nn.sigmoid(cg_ref[...] + bcg_ref[...])

        def blk(rb, carry):
            base = pl.multiple_of(rb * RB, RB)
            bank = _shift_bank(zp_ref[pl.ds(base, RB + HALO), :])
            acc = jnp.broadcast_to(cb_ref[...], (RB, CT))
            for k in range(KCONV):
                acc = acc + w_ref[k:k + 1, :] * _shifted(bank, k + 2)
            z1_ref[pl.ds(base, RB), :] = acc
            return carry

        lax.fori_loop(0, n_rows // RB, blk, 0)

    return pl.pallas_call(
        body, name=name, grid=(D // CT,),
        in_specs=[pl.BlockSpec((n_rows, CT), lambda j: (0, CV0 + j)), pl.BlockSpec((n_rows, CT), lambda j: (0, CG0 + j)),
                  pl.BlockSpec((1, CT), lambda j: (0, CV0 + j)), pl.BlockSpec((1, CT), lambda j: (0, CG0 + j)),
                  pl.BlockSpec((HALO, CT), lambda j: (0, j)), pl.BlockSpec((1, CT), lambda j: (0, j))],
        out_specs=pl.BlockSpec((n_rows, CT), lambda j: (0, j)),
        out_shape=jax.ShapeDtypeStruct((n_rows, D), F32),
        scratch_shapes=[pltpu.VMEM((n_rows + HALO, CT), F32)],
        compiler_params=_cparams("parallel"),
    )(proj, proj, b_in, b_in, w, cb)


def _conv_bwd(name, dz1, proj, b_in, w):
    n_rows = proj.shape[0]

    def body(dz_ref, cv_ref, cg_ref, bcv_ref, bcg_ref, w_ref, dcv_ref, dcg_ref, sm_ref, zp_ref, dzp_ref):
        cvb = cv_ref[...] + bcv_ref[...]
        sg = jax.nn.sigmoid(cg_ref[...] + bcg_ref[...])
        zp_ref[0:HALO, :] = jnp.zeros((HALO, CT), F32)
        zp_ref[HALO:, :] = cvb * sg
        dz = dz_ref[...]
        dzp_ref[0:n_rows, :] = dz
        dzp_ref[n_rows:, :] = jnp.zeros((HALO, CT), F32)
        sm_ref[...] = jnp.zeros(sm_ref.shape, F32)
        sm_ref[32:33, :] = jnp.sum(dz, axis=0, keepdims=True)

        def blk(rb, carry):
            base = pl.multiple_of(rb * RB, RB)
            dbank = _shift_bank(dzp_ref[pl.ds(base, RB + HALO), :])
            zbank = _shift_bank(zp_ref[pl.ds(base, RB + HALO), :])
            dzb = dbank[0][0:RB]
            acc = jnp.zeros((RB, CT), F32)
            for k in range(KCONV):
                acc = acc + w_ref[k:k + 1, :] * _shifted(dbank, KCONV - 1 - k)
                sm_ref[k:k + 1, :] += jnp.sum(dzb * _shifted(zbank, k + 2), axis=0, keepdims=True)
            dzp_ref[pl.ds(base, RB), :] = acc
            return carry

        lax.fori_loop(0, n_rows // RB, blk, 0)
        dz0 = dzp_ref[0:n_rows, :]
        dcv = dz0 * sg
        dcg = dz0 * cvb * (sg * (1.0 - sg))
        dcv_ref[...] = dcv.astype(dcv_ref.dtype)
        dcg_ref[...] = dcg.astype(dcg_ref.dtype)
        sm_ref[33:34, :] = jnp.sum(dcv, axis=0, keepdims=True)
        sm_ref[34:35, :] = jnp.sum(dcg, axis=0, keepdims=True)

    col = pl.BlockSpec((n_rows, CT), lambda j: (0, j))
    return pl.pallas_call(
        body, name=name, grid=(D // CT,),
        in_specs=[col, pl.BlockSpec((n_rows, CT), lambda j: (0, CV0 + j)), pl.BlockSpec((n_rows, CT), lambda j: (0, CG0 + j)),
                  pl.BlockSpec((1, CT), lambda j: (0, CV0 + j)), pl.BlockSpec((1, CT), lambda j: (0, CG0 + j)),
                  pl.BlockSpec((HALO, CT), lambda j: (0, j))],
        out_specs=[col, col, pl.BlockSpec((40, CT), lambda j: (0, j))],
        out_shape=[jax.ShapeDtypeStruct((n_rows, D), BF16), jax.ShapeDtypeStruct((n_rows, D), BF16),
                   jax.ShapeDtypeStruct((40, D), F32)],
        scratch_shapes=[pltpu.VMEM((n_rows + HALO, CT), F32), pltpu.VMEM((n_rows + HALO, CT), F32)],
        compiler_params=_cparams("parallel"),
    )(dz1, proj, proj, b_in, b_in, w)


ADA_TN = 768


def _split_bf16(v):
    hi = v.astype(BF16)
    return hi, (v - hi.astype(F32)).astype(BF16)


def _ada_fwd(name, c_all, w, b):
    n = w.shape[1]

    def body(c_ref, w_ref, b_ref, o_ref):
        ch, cl = _split_bf16(_silu(c_ref[...]))
        wh, wl = _split_bf16(w_ref[...])
        acc = jnp.dot(ch, wl, preferred_element_type=F32) + jnp.dot(cl, wh, preferred_element_type=F32)
        o_ref[...] = acc + jnp.dot(ch, wh, preferred_element_type=F32) + b_ref[...]

    return pl.pallas_call(
        body, name=name, grid=(n // ADA_TN,),
        in_specs=[pl.BlockSpec((8, D), lambda j: (0, 0)), pl.BlockSpec((D, ADA_TN), lambda j: (0, j)),
                  pl.BlockSpec((1, ADA_TN), lambda j: (0, j))],
        out_specs=pl.BlockSpec((8, ADA_TN), lambda j: (0, j)),
        out_shape=jax.ShapeDtypeStruct((8, n), F32),
        compiler_params=_cparams("parallel"),
    )(c_all, w, b)


def _ada_wgrad(name, c_all_t, dmod):
    n = dmod.shape[1]

    def body(ct_ref, dm_ref, o_ref):
        ca = _silu(ct_ref[...])
        acc = ca[:, 0:1] * dm_ref[0:1, :]
        for r in range(1, 8):
            acc = acc + ca[:, r:r + 1] * dm_ref[r:r + 1, :]
        o_ref[...] = acc

    return pl.pallas_call(
        body, name=name, grid=(n // ADA_TN,),
        in_specs=[pl.BlockSpec((D, 8), lambda j: (0, 0)), pl.BlockSpec((8, ADA_TN), lambda j: (0, j))],
        out_specs=pl.BlockSpec((D, ADA_TN), lambda j: (0, j)),
        out_shape=jax.ShapeDtypeStruct((D, n), F32),
        compiler_params=_cparams("parallel"),
    )(c_all_t, dmod)


def _adamw(name, w, g, m, v, tr):
    rows, cols = w.shape

    def body(w_ref, g_ref, m_ref, v_ref, g_out, d_ref, nm_ref, nv_ref):
        gv = g_ref[...]
        nm = ADAM_B1 * m_ref[...] + (1.0 - ADAM_B1) * gv
        nv = ADAM_B2 * v_ref[...] + (1.0 - ADAM_B2) * (gv * gv)
        m_hat = nm / (1.0 - ADAM_B1 ** ADAM_STEP)
        v_hat = nv / (1.0 - ADAM_B2 ** ADAM_STEP)
        d_ref[...] = -ADAM_LR * (m_hat / (jnp.sqrt(v_hat) + ADAM_EPS) + ADAM_WD * w_ref[...])
        nm_ref[...] = nm
        nv_ref[...] = nv
        g_out[...] = gv

    spec = pl.BlockSpec((tr, cols), lambda i: (i, 0))
    sds = jax.ShapeDtypeStruct((rows, cols), F32)
    return pl.pallas_call(
        body, name=name, grid=(rows // tr,), in_specs=[spec] * 4, out_specs=[spec] * 4, out_shape=[sds] * 4,
        compiler_params=_cparams("parallel"),
    )(w, g, m, v)


def _sum8(name, stacked, tr, after=None):
    n, rows, cols = stacked.shape
    extra = _afters(after)

    def body(s_ref, *rest):
        o_ref = rest[-1]
        acc = s_ref[0]
        for r in range(1, n):
            acc = acc + s_ref[r]
        o_ref[...] = acc

    return pl.pallas_call(
        body, name=name, grid=(rows // tr,),
        in_specs=[pl.BlockSpec((n, tr, cols), lambda i: (0, i, 0))] + [pl.BlockSpec(memory_space=pl.ANY)] * len(extra),
        out_specs=pl.BlockSpec((tr, cols), lambda i: (i, 0)),
        out_shape=jax.ShapeDtypeStruct((rows, cols), F32),
        compiler_params=_cparams("parallel"),
    )(stacked, *extra)


def _pair_add(name, g5, other, c_idx, tr):
    nq, _, rows, cols = g5.shape

    def body(c_ref, g_ref, o_ref, p_ref):
        p_ref[...] = (g_ref[...].astype(F32) + o_ref[...].astype(F32)).astype(p_ref.dtype)

    return pl.pallas_call(
        body, name=name,
        grid_spec=pltpu.PrefetchScalarGridSpec(
            num_scalar_prefetch=1, grid=(nq, rows // tr),
            in_specs=[pl.BlockSpec((None, None, tr, cols), lambda qi, i, cr: (qi, cr[0], i, 0)),
                      pl.BlockSpec((None, tr, cols), lambda qi, i, cr: (qi, i, 0))],
            out_specs=pl.BlockSpec((None, tr, cols), lambda qi, i, cr: (qi, i, 0))),
        out_shape=jax.ShapeDtypeStruct((nq, rows, cols), BF16),
        compiler_params=_cparams("parallel", "parallel"),
    )(c_idx, g5, other)


def _chip_add(name, p, recv, qc_idx, tr):
    _, rows, cols = p.shape

    def body(qc_ref, p_ref, r_ref, o_ref):
        acc = p_ref[...].astype(F32)
        for k in range(NCHIP - 1):
            acc = acc + r_ref[k].astype(F32)
        o_ref[...] = acc

    return pl.pallas_call(
        body, name=name,
        grid_spec=pltpu.PrefetchScalarGridSpec(
            num_scalar_prefetch=1, grid=(rows // tr,),
            in_specs=[pl.BlockSpec((None, tr, cols), lambda i, qc: (qc[0], i, 0)),
                      pl.BlockSpec((NCHIP - 1, tr, cols), lambda i, qc: (0, i, 0))],
            out_specs=pl.BlockSpec((None, tr, cols), lambda i, qc: (qc[1], i, 0))),
        out_shape=jax.ShapeDtypeStruct((2, rows, cols), F32),
        compiler_params=_cparams("parallel"),
    )(qc_idx, p, recv)


def _allgather8(name, blk, after=None):
    m_per, n = blk.shape
    extra = _afters(after)

    def body(x_ref, *rest):
        out_ref, send_sems, recv_sems, local_sem = rest[len(extra):]
        x, y, c = lax.axis_index("x"), lax.axis_index("y"), lax.axis_index("c")
        me, sibling = (x, y, c), (x, y, 1 - c)
        chips = [(1 - x, y), (x, 1 - y), (1 - x, 1 - y)]

        def rows(px, py, pc):
            return out_ref.at[pl.ds((4 * px + 2 * py + pc) * m_per, m_per), :]

        def copy(k, block, to, src=None):
            return pltpu.make_async_remote_copy(
                src_ref=rows(*block) if src is None else src, dst_ref=rows(*block),
                send_sem=send_sems.at[k], recv_sem=recv_sems.at[k], device_id=to, device_id_type=MESH)

        mine = pltpu.make_async_copy(x_ref, rows(*me), local_sem)
        mine.start()
        first = [copy(0, me, sibling, src=x_ref)]
        first += [copy(1 + j, me, (*chip, c), src=x_ref) for j, chip in enumerate(chips)]
        for cp in first:
            cp.start()
        passed = [copy(4 + j, (*chip, c), sibling) for j, chip in enumerate(chips)]
        for j, chip in enumerate(chips):
            copy(1 + j, (*chip, c), me).wait_recv()
            passed[j].start()
        copy(0, sibling, me).wait_recv()
        for j, chip in enumerate(chips):
            copy(4 + j, (*chip, 1 - c), me).wait_recv()
        for cp in first + passed:
            cp.wait_send()
        mine.wait()

    return pl.pallas_call(
        body, name=name,
        out_shape=jax.ShapeDtypeStruct((8 * m_per, n), blk.dtype),
        in_specs=[pl.BlockSpec(memory_space=pltpu.VMEM)] + [pl.BlockSpec(memory_space=pl.ANY)] * len(extra),
        out_specs=pl.BlockSpec(memory_space=pltpu.VMEM),
        scratch_shapes=[pltpu.SemaphoreType.DMA((7,)), pltpu.SemaphoreType.DMA((7,)), pltpu.SemaphoreType.DMA],
        compiler_params=pltpu.CompilerParams(vmem_limit_bytes=VMEM_LIMIT),
    )(blk, *extra)


def _exchange(name, srcs, out_shapes, n_local, n_remote, plan, aliases=None, after=None):
    ni, no = len(srcs), len(out_shapes)
    extra = [] if after is None else [after]

    def body(*refs):
        ins, outs = refs[:ni], refs[ni + len(extra):ni + len(extra) + no]
        send_sems, recv_sems, local_sems = refs[ni + len(extra) + no:]
        x, y, c = lax.axis_index("x"), lax.axis_index("y"), lax.axis_index("c")
        local, remote = plan(ins, outs, x, y, c)
        assert len(local) == n_local and len(remote) == n_remote

        def rcopy(i, dst):
            s, _, peer, _, _ = remote[i]
            return pltpu.make_async_remote_copy(src_ref=s, dst_ref=dst, send_sem=send_sems.at[i],
                                                recv_sem=recv_sems.at[i], device_id=peer, device_id_type=MESH)

        lcs = [pltpu.make_async_copy(s, d, local_sems.at[i]) for i, (s, d) in enumerate(local)]
        for cp in lcs:
            cp.start()
        first = [i for i in range(n_remote) if remote[i][4] is None]
        passed = [i for i in range(n_remote) if remote[i][4] is not None]
        for i in first:
            rcopy(i, remote[i][1]).start()
        arrived = set()
        for i in passed:
            j = remote[i][4]
            rcopy(j, remote[j][3]).wait_recv()
            arrived.add(j)
            rcopy(i, remote[i][1]).start()
        for i in range(n_remote):
            if i not in arrived:
                rcopy(i, remote[i][3]).wait_recv()
        for i in range(n_remote):
            rcopy(i, remote[i][1]).wait_send()
        for cp in lcs:
            cp.wait()

    any_spec = pl.BlockSpec(memory_space=pl.ANY)
    return pl.pallas_call(
        body, name=name, out_shape=out_shapes,
        in_specs=[any_spec] * (ni + len(extra)), out_specs=[any_spec] * no,
        input_output_aliases=aliases or {},
        scratch_shapes=[pltpu.SemaphoreType.DMA((n_remote,)), pltpu.SemaphoreType.DMA((n_remote,)),
                        pltpu.SemaphoreType.DMA((max(n_local, 1),))],
    )(*srcs, *extra)


_CHIP_FLIPS = ((0, 1), (1, 0), (1, 1))


def _flip(v, f):
    return 1 - v if f else v


EFFECT = pltpu.SideEffectType.DATAFLOW_SIDE_EFFECTING


def _split_start(name, bufs, n, plan, after=None):
    nb = len(bufs)
    extra = [] if after is None else [after]

    def body(*refs):
        ins = refs[:nb]
        send_sems, recv_sems = refs[nb + len(extra)], refs[nb + len(extra) + 1]
        token = refs[-1]
        x, y, c = lax.axis_index("x"), lax.axis_index("y"), lax.axis_index("c")
        copies = plan(ins, x, y, c)
        assert len(copies) == n
        for i, (s, d, peer, _) in enumerate(copies):
            pltpu.make_async_remote_copy(src_ref=s, dst_ref=d, send_sem=send_sems.at[i], recv_sem=recv_sems.at[i],
                                         device_id=peer, device_id_type=MESH).start()
        token[...] = jnp.zeros_like(token)

    hbm = pl.BlockSpec(memory_space=pltpu.HBM)
    sem = pl.BlockSpec(memory_space=pltpu.SEMAPHORE)
    res = pl.pallas_call(
        body, name=name,
        out_shape=(pltpu.SemaphoreType.DMA((n,)), pltpu.SemaphoreType.DMA((n,)),
                   *[pltpu.HBM(b.shape, b.dtype) for b in bufs], jax.ShapeDtypeStruct((8, 128), F32)),
        in_specs=[hbm] * nb + [pl.BlockSpec(memory_space=pl.ANY)] * len(extra),
        out_specs=(sem, sem, *[hbm] * nb, pl.BlockSpec(memory_space=pltpu.VMEM)),
        input_output_aliases={i: 2 + i for i in range(nb)},
        compiler_params=pltpu.CompilerParams(has_side_effects=EFFECT),
    )(*[pltpu.with_memory_space_constraint(b, pltpu.HBM) for b in bufs], *extra)
    return res[0], res[1], list(res[2:2 + nb]), res[-1]


def _split_wait(name, bufs, send_sems, recv_sems, after, n, plan):
    nb = len(bufs)

    def body(*refs):
        ins = refs[:nb]
        ssem, rsem = refs[nb], refs[nb + 1]
        x, y, c = lax.axis_index("x"), lax.axis_index("y"), lax.axis_index("c")
        copies = plan(ins, x, y, c)
        assert len(copies) == n
        for i, (s, _, peer, lands) in enumerate(copies):
            cp = pltpu.make_async_remote_copy(src_ref=s, dst_ref=lands, send_sem=ssem.at[i], recv_sem=rsem.at[i],
                                              device_id=peer, device_id_type=MESH)
            cp.wait_send()
            cp.wait_recv()

    hbm = pl.BlockSpec(memory_space=pltpu.HBM)
    sem = pl.BlockSpec(memory_space=pltpu.SEMAPHORE)
    afters = list(after) if isinstance(after, (list, tuple)) else [after]
    res = pl.pallas_call(
        body, name=name,
        out_shape=[pltpu.HBM(b.shape, b.dtype) for b in bufs],
        in_specs=[hbm] * nb + [sem, sem] + [pl.BlockSpec(memory_space=pl.ANY)] * len(afters),
        out_specs=[hbm] * nb,
        input_output_aliases={i: i for i in range(nb)},
        compiler_params=pltpu.CompilerParams(has_side_effects=EFFECT),
    )(*bufs, send_sems, recv_sems, *afters)
    return list(res)


def _chips_of(x, y):
    return [(_flip(x, fx), _flip(y, fy)) for fx, fy in _CHIP_FLIPS]


def _gather_start(tag, stacks, after):
    n = len(stacks)

    def plan(refs, x, y, c):
        q = 2 * x + y
        return [(refs[a].at[q, c], refs[a].at[q, c], (px, py, c), refs[a].at[2 * px + py, c])
                for a in range(n) for px, py in _chips_of(x, y)]

    send, recv, thru, token = _split_start(f"gather_start_{tag}", stacks, 3 * n, plan, after)
    return (send, recv, thru, plan), token


def _gather_finish(tag, state, after):
    send, recv, thru, plan = state
    n = len(thru)
    landed = _split_wait(f"gather_wait_{tag}", thru, send, recv, after, 3 * n, plan)

    def pass_on(ins, outs, x, y, c):
        sib = (x, y, 1 - c)
        remote = []
        for a in range(n):
            for px, py in _chips_of(x, y):
                p = 2 * px + py
                remote.append((ins[a].at[p, c], outs[a].at[p, c], sib, outs[a].at[p, 1 - c], None))
        return [], remote

    full = _exchange(f"gather_pass_{tag}", landed, [jax.ShapeDtypeStruct(s.shape, s.dtype) for s in landed],
                     0, 3 * n, pass_on, aliases={a: a for a in range(n)})
    return [g.reshape(NCHIP, 2 * g.shape[2], g.shape[3]) for g in full]


def _gather_land(tag, state, after):
    send, recv, thru, plan = state
    n = len(thru)
    landed = _split_wait(f"gather_wait_{tag}", thru, send, recv, after, 3 * n, plan)

    def pass_plan(refs, x, y, c):
        sib = (x, y, 1 - c)
        return [(refs[a].at[2 * px + py, c], refs[a].at[2 * px + py, c], sib, refs[a].at[2 * px + py, 1 - c])
                for a in range(n) for px, py in _chips_of(x, y)]

    send, recv, thru, token = _split_start(f"gather_pass_start_{tag}", landed, 3 * n, pass_plan)
    return (send, recv, thru, pass_plan), token


def _gather_done(tag, state, after):
    send, recv, thru, plan = state
    full = _split_wait(f"gather_pass_wait_{tag}", thru, send, recv, after, 3 * len(thru), plan)
    return [g.reshape(NCHIP, 2 * g.shape[2], g.shape[3]) for g in full]


_OTHERS = tuple((fx, fy, fc) for fx in (0, 1) for fy in (0, 1) for fc in (0, 1) if fx or fy or fc)


def _allgather_start(tag, buf, after=None):
    def plan(refs, x, y, c):
        me = 4 * x + 2 * y + c
        copies = []
        for fx, fy, fc in _OTHERS:
            px, py, pc = _flip(x, fx), _flip(y, fy), _flip(c, fc)
            copies.append((refs[0].at[me], refs[0].at[me], (px, py, pc), refs[0].at[4 * px + 2 * py + pc]))
        return copies

    send, recv, thru, token = _split_start(f"{tag}_start", [buf], len(_OTHERS), plan, after)
    return (send, recv, thru, plan), token


def _allgather_done(tag, state, after):
    send, recv, thru, plan = state
    return _split_wait(f"{tag}_wait", thru, send, recv, after, len(_OTHERS), plan)[0]


def _gather_small(name, smalls):
    n = len(smalls)

    def plan(ins, outs, x, y, c):
        q = 2 * x + y
        local = [(ins[b], outs[b].at[q]) for b in range(n)]
        remote = [(ins[b], outs[b].at[q], (px, py, c), outs[b].at[2 * px + py], None)
                  for b in range(n) for px, py in _chips_of(x, y)]
        return local, remote

    return _exchange(name, smalls, [jax.ShapeDtypeStruct((NCHIP,) + s.shape, s.dtype) for s in smalls], n, 3 * n, plan)


def _rs_pair_start(tag, grads, after=None):
    n = len(grads)
    g5 = [g.reshape(NCHIP, 2, g.shape[1] // 2, g.shape[2]) for g in grads]
    half = [(g.shape[2], g.shape[3]) for g in g5]
    land = [lax.empty((NCHIP,) + h, g.dtype) for h, g in zip(half, grads)]

    def plan(refs, x, y, c):
        sib = (x, y, 1 - c)
        return [(refs[a].at[:, 1 - c], refs[n + a], sib, refs[n + a]) for a in range(n)]

    send, recv, thru, token = _split_start(f"rs_pair_start_{tag}", g5 + land, n, plan, after)
    return (send, recv, thru, plan, half), token


def _rs_chips_start(tag, state, after, c_idx):
    send, recv, thru, pair_plan, half = state
    n = len(half)
    landed = _split_wait(f"rs_pair_wait_{tag}", thru, send, recv, after, n, pair_plan)
    g5, got = landed[:n], landed[n:]
    part = [_pair_add(f"rs_pair_add_{tag}{a}", g5[a], got[a], c_idx, _tile(half[a][0], 512)) for a in range(n)]
    land = [lax.empty((NCHIP - 1,) + h, BF16) for h in half]

    def plan(refs, x, y, c):
        return [(refs[a].at[2 * px + py], refs[n + a].at[k], (px, py, c), refs[n + a].at[k])
                for a in range(n) for k, (px, py) in enumerate(_chips_of(x, y))]

    send, recv, thru, token = _split_start(f"rs_chips_start_{tag}", part + land, 3 * n, plan)
    return (send, recv, thru, plan, half), token


def _rs_finish(tag, state, after, qc_idx):
    send, recv, thru, plan, half = state
    n = len(half)
    landed = _split_wait(f"rs_chips_wait_{tag}", thru, send, recv, after, 3 * n, plan)
    red = [_chip_add(f"rs_chip_add_{tag}{a}", landed[a], landed[n + a], qc_idx, _tile(half[a][0], 512)) for a in range(n)]

    def share(ins, outs, x, y, c):
        sib = (x, y, 1 - c)
        return [], [(ins[a].at[c], outs[a].at[c], sib, outs[a].at[1 - c], None) for a in range(n)]

    full = _exchange(f"rs_share_{tag}", red, [jax.ShapeDtypeStruct((2,) + h, F32) for h in half], 0, n, share,
                     aliases={a: a for a in range(n)})
    return [f.reshape(2 * h[0], h[1]) for f, h in zip(full, half)]


def _rs_land(tag, state, after, qc_idx):
    send, recv, thru, plan, half = state
    n = len(half)
    landed = _split_wait(f"rs_chips_wait_{tag}", thru, send, recv, after, 3 * n, plan)
    red = [_chip_add(f"rs_chip_add_{tag}{a}", landed[a], landed[n + a], qc_idx, _tile(half[a][0], 512)) for a in range(n)]

    def share(refs, x, y, c):
        sib = (x, y, 1 - c)
        return [(refs[a].at[c], refs[a].at[c], sib, refs[a].at[1 - c]) for a in range(n)]

    send, recv, thru, token = _split_start(f"rs_share_start_{tag}", red, n, share)
    return (send, recv, thru, share, half), token


def _rs_done(tag, state, after):
    send, recv, thru, plan, half = state
    full = _split_wait(f"rs_share_wait_{tag}", thru, send, recv, after, len(half), plan)
    return [f.reshape(2 * h[0], h[1]) for f, h in zip(full, half)]


def _place(name, blk, idx):
    rows, cols = blk.shape

    def body(i_ref, b_ref, o_ref):
        o_ref[...] = b_ref[...]

    return pl.pallas_call(
        body, name=name,
        grid_spec=pltpu.PrefetchScalarGridSpec(
            num_scalar_prefetch=1, grid=(1,),
            in_specs=[pl.BlockSpec((rows, cols), lambda i, d: (0, 0))],
            out_specs=pl.BlockSpec((None, rows, cols), lambda i, d: (d[0], 0, 0))),
        out_shape=jax.ShapeDtypeStruct((8, rows, cols), blk.dtype),
        compiler_params=_cparams("arbitrary"),
    )(idx, blk)


def _pack(name, n_rows, pieces, after=None):
    arrays = [p[1] for p in pieces if p[0] != "zeros"]
    extra = [] if after is None else [after]

    def body(*refs):
        o_ref = refs[-1]
        o_ref[...] = jnp.zeros(o_ref.shape, F32)
        row, ai = 0, 0
        for p in pieces:
            kind = p[0]
            if kind == "zeros":
                row += p[1]
                continue
            ref = refs[len(extra) + ai]
            ai += 1
            if kind == "rows":
                r = ref.shape[0]
                o_ref[row:row + r, :] = ref[...]
            elif kind == "wide":
                r = ref.shape[1] // D
                for j in range(r):
                    o_ref[row + j:row + j + 1, :] = ref[:, j * D:(j + 1) * D]
            elif kind == "slice":
                r = p[3] - p[2]
                o_ref[row:row + r, :] = ref[p[2]:p[3], :]
            elif kind == "heads":
                r = CHUNK
                for h in range(HEADS):
                    o_ref[row:row + r, h * CHUNK:(h + 1) * CHUNK] = ref[h]
            else:
                r = 1
                o_ref[row:row + 1, 0:128] = ref[0:1, :]
            row += r
        assert row == n_rows, (row, n_rows)

    vmem = pl.BlockSpec(memory_space=pltpu.VMEM)
    return pl.pallas_call(
        body, name=name, out_shape=jax.ShapeDtypeStruct((n_rows, D), F32),
        in_specs=[pl.BlockSpec(memory_space=pl.ANY)] * len(extra) + [vmem] * len(arrays), out_specs=vmem,
        compiler_params=pltpu.CompilerParams(vmem_limit_bytes=VMEM_LIMIT),
    )(*extra, *arrays)


def _to_bf16_stack(name, w, q_idx):
    r, cols = w.shape
    half = r // 2
    tr = _tile(half, 256)
    nb = half // tr

    def body(q_ref, w_ref, o_ref):
        o_ref[...] = w_ref[...].astype(BF16)

    return pl.pallas_call(
        body, name=name,
        grid_spec=pltpu.PrefetchScalarGridSpec(
            num_scalar_prefetch=1, grid=(2 * nb,),
            in_specs=[pl.BlockSpec((tr, cols), lambda i, qr: (i, 0))],
            out_specs=pl.BlockSpec((None, None, tr, cols), lambda i, qr: (qr[0], i // nb, i % nb, 0))),
        out_shape=jax.ShapeDtypeStruct((NCHIP, 2, half, cols), BF16),
        compiler_params=_cparams("parallel"),
    )(q_idx, w)


def kernel(x, c, ada_w, ada_b, norm_ffn1, ffn1_w_gate, ffn1_w_up, ffn1_w_down, norm_mix, mix_w_in, mix_b_in, sgu_ln_g, sgu_ln_b, sgu_w_s, sgu_b_s, conv_w, conv_b, conv_ln_g, conv_ln_b, w_branch_a, w_branch_b, w_out, norm_ffn2, ffn2_w_gate, ffn2_w_up, ffn2_w_down, norm_final, loss_target, m_ada_w, m_ada_b, m_norm_ffn1, m_ffn1_w_gate, m_ffn1_w_up, m_ffn1_w_down, m_norm_mix, m_mix_w_in, m_mix_b_in, m_sgu_ln_g, m_sgu_ln_b, m_sgu_w_s, m_sgu_b_s, m_conv_w, m_conv_b, m_conv_ln_g, m_conv_ln_b, m_w_branch_a, m_w_branch_b, m_w_out, m_norm_ffn2, m_ffn2_w_gate, m_ffn2_w_up, m_ffn2_w_down, m_norm_final, v_ada_w, v_ada_b, v_norm_ffn1, v_ffn1_w_gate, v_ffn1_w_up, v_ffn1_w_down, v_norm_mix, v_mix_w_in, v_mix_b_in, v_sgu_ln_g, v_sgu_ln_b, v_sgu_w_s, v_sgu_b_s, v_conv_w, v_conv_b, v_conv_ln_g, v_conv_ln_b, v_w_branch_a, v_w_branch_b, v_w_out, v_norm_ffn2, v_ffn2_w_gate, v_ffn2_w_up, v_ffn2_w_down, v_norm_final):
    xi, yi, ci = lax.axis_index("x"), lax.axis_index("y"), lax.axis_index("c")
    q = 2 * xi + yi
    dev = 4 * xi + 2 * yi + ci
    c_idx = jnp.reshape(ci, (1,)).astype(jnp.int32)
    q_idx = jnp.reshape(q, (1,)).astype(jnp.int32)
    x0 = x[0]
    tgt = loss_target[0]
    qc_idx = jnp.stack([q, ci]).astype(jnp.int32)
    n_ada = ada_w.shape[2]

    big = dict(g1=ffn1_w_gate, u1=ffn1_w_up, d1=ffn1_w_down, win=mix_w_in, wa=w_branch_a, wb=w_branch_b, wo=w_out,
               g2=ffn2_w_gate, u2=ffn2_w_up, d2=ffn2_w_down)
    groups = dict(f1=("d1", "g1", "u1"), mx=("wo", "wa", "wb", "win"), f2=("d2", "g2", "u2"))
    gather_groups = dict(up1=("g1", "u1"), dn1=("d1",), win=("win",), mx=("wa", "wb", "wo"), up2=("g2", "u2"),
                         dn2=("d2",))
    first_stacks = [_to_bf16_stack(f"cast_{k}", big[k][0], q_idx) for k in gather_groups["up1"]]

    c_all = _allgather8("ag_c", jnp.pad(c, ((0, 7), (0, 0))), after=first_stacks)[::8]
    ada_b_mine = lax.dynamic_slice(ada_b, (0, q * n_ada), (1, n_ada))
    mod_part = _ada_fwd("ada_fwd", c_all, ada_w[0], ada_b_mine)
    small_parts = _gather_small("gather_small", [mod_part, jnp.pad(conv_w[0], ((0, 1), (0, 0)))])

    gather_state = {}
    started = [small_parts[0]]
    for tag, keys in gather_groups.items():
        stacks = first_stacks if tag == "up1" else [_to_bf16_stack(f"cast_{k}", big[k][0], q_idx) for k in keys]
        gather_state[tag], token = _gather_start(tag, stacks, started[-1])
        started.append(token)

    mod_all = jnp.transpose(small_parts[0], (1, 0, 2)).reshape(8, N_MOD * D)
    mod = lax.dynamic_slice(mod_all, (dev, 0), (1, N_MOD * D))
    sh1, sc1, g1, sh2, sc2, g2, sh3, sc3 = [(mod, D, k) for k in range(N_MOD - 1)]
    g3 = mod[:, (N_MOD - 1) * D:]
    convw = jnp.transpose(small_parts[1], (1, 0, 2)).reshape(HALO, D)
    (h1,) = _rowwise_fwd("pre1", _f_pre, [x0], [norm_ffn1, sc1, sh1], [(D, BF16)], 512)
    W = dict(zip(gather_groups["up1"], _gather_finish("up1", gather_state["up1"], [h1] + started)))

    causal = jnp.tril(jnp.ones((CHUNK, CHUNK), bool))
    wm_f = jnp.where(causal[None], sgu_w_s[0], 0.0)
    wm = wm_f.astype(BF16)
    wmt = jnp.swapaxes(wm_f, 1, 2).astype(BF16)
    bst = jnp.repeat(jnp.transpose(sgu_b_s[0]), CHUNK, axis=1)
    b_in = mix_b_in

    a1, b1, s1 = _ffn_up("up1", h1, W["g1"], W["u1"])
    W.update(zip(gather_groups["dn1"], _gather_finish("dn1", gather_state["dn1"], s1)))
    f1 = _matmul("down1", s1, W["d1"], "fwd", False, F32)
    pass_win, token = _gather_land("win", gather_state["win"], f1)
    x1, h2 = _rowwise_fwd("res1", _f_res_pre(0.5), [x0, f1], [g1, norm_mix, sc2, sh2],
                          [(D, F32), (D, BF16)], 256, after=token)
    W.update(zip(gather_groups["win"], _gather_done("win", pass_win, x1)))
    proj = _matmul("in_proj", h2, W["win"], "fwd", True, F32)
    pass_mx, token = _gather_land("mx", gather_state["mx"], proj)
    ya = _sgu_fwd("sgu", proj, b_in, sgu_ln_g, sgu_ln_b, wm, bst, 256)
    z1 = _conv_fwd("conv", proj, b_in, convw, conv_b)
    (z3,) = _rowwise_fwd("conv_ln", _f_lnsilu, [z1], [conv_ln_g, conv_ln_b], [(D, BF16)], 512, after=token)
    W.update(zip(gather_groups["mx"], _gather_done("mx", pass_mx, z3)))
    y_a = _matmul("branch_a", ya, W["wa"], "fwd", False, F32)
    y_b = _matmul("branch_b", z3, W["wb"], "fwd", False, F32)
    pass_up2, token = _gather_land("up2", gather_state["up2"], y_b)
    (merged,) = _rowwise_fwd("merge", _f_merge, [(proj, 2 * D, 2), y_a, y_b], [(b_in, 2 * D, 2)],
                             [(D, BF16)], 256, after=token)
    y = _matmul("out_proj", merged, W["wo"], "fwd", False, F32)
    x2, h3 = _rowwise_fwd("res2", _f_res_pre(1.0), [x1, y], [g2, norm_ffn2, sc3, sh3], [(D, F32), (D, BF16)], 512)
    W.update(zip(gather_groups["up2"], _gather_done("up2", pass_up2, x2)))
    a3, b3, s3 = _ffn_up("up2", h3, W["g2"], W["u2"])
    W.update(zip(gather_groups["dn2"], _gather_finish("dn2", gather_state["dn2"], s3)))
    f3 = _matmul("down2", s3, W["d2"], "fwd", False, F32)
    dx2, df3, dg3, dnwf, loss_blk = _final("final", x2, f3, tgt, g3, norm_final.reshape(1, D), 512)

    G = {}

    def ffn_bwd(tag, df, s, a, b, h, wg, wu, wd, after=None):
        da, db = _dgrad_glu(f"d_down{tag}", df, wd, a, b, after)
        G["d" + tag] = _matmul(f"w_down{tag}", s, df, "wgrad", False, BF16)
        dh = _dgrad_pair(f"d_gate_up{tag}", da, db, wg, wu)
        G["g" + tag] = _matmul(f"w_gate{tag}", h, da, "wgrad", True, BF16)
        G["u" + tag] = _matmul(f"w_up{tag}", h, db, "wgrad", True, BF16)
        return dh

    reduced = {}
    dh3 = ffn_bwd("2", df3, s3, a3, b3, h3, W["g2"], W["u2"], W["d2"])
    pair_f2, token = _rs_pair_start("f2", [G[k] for k in groups["f2"]])
    (dx1, dy), (dg2, dnw3, dsc3, dsh3) = _rowwise_bwd(
        "res2_b", _f_res_pre(1.0), [x1, y], [g2, norm_ffn2, sc3, sh3], [dx2, dh3], [F32, BF16], 256, after=token)
    dmerged = _matmul("d_out", dy, W["wo"], "dgrad", False, F32)
    G["wo"] = _matmul("w_out", merged, dy, "wgrad", False, BF16)
    rs_f2, token = _rs_chips_start("f2", pair_f2, [dmerged, G["wo"]], c_idx)
    (dgab, dy_a, dy_b), (dbgab,) = _rowwise_bwd(
        "merge_b", _f_merge, [(proj, 2 * D, 2), y_a, y_b], [(b_in, 2 * D, 2)], [dmerged],
        [BF16, BF16, BF16], 256, after=token)
    dya = _matmul("d_a", dy_a, W["wa"], "dgrad", False, F32)
    dz3 = _matmul("d_b", dy_b, W["wb"], "dgrad", False, F32)
    G["wa"] = _matmul("w_a", ya, dy_a, "wgrad", False, BF16)
    G["wb"] = _matmul("w_b", z3, dy_b, "wgrad", False, BF16)
    (dz1,), (dclg, dclb) = _rowwise_bwd("conv_ln_b", _f_lnsilu, [z1], [conv_ln_g, conv_ln_b], [dz3], [F32], 512)
    dcv, dcg, conv_small = _conv_bwd("conv_b", dz1, proj, b_in, convw)
    duv, dbuv, dlng, dlnb, dws, dbs = _sgu_bwd("sgu_b", proj, b_in, sgu_ln_g, sgu_ln_b, wm, wmt, bst, dya, 256)
    dproj = jnp.concatenate([duv, dcv, dcg, dgab], axis=1)
    dh2 = _matmul("d_in", dproj, W["win"], "dgrad", True, F32)
    G["win"] = _matmul("w_in", h2, dproj, "wgrad", True, BF16)
    pair_mx, token = _rs_pair_start("mx", [G[k] for k in groups["mx"]])
    share_f2, token2 = _rs_land("f2", rs_f2, [dh2, token], qc_idx)

    dev_idx = jnp.reshape(dev, (1,)).astype(jnp.int32)
    early = _pack("pack_early", HALO + CHUNK, [("slice", conv_small, 0, HALO), ("heads", dws)])
    ag_early, token3 = _allgather_start("ag_early", _place("place_early", early, dev_idx), token2)

    (dx0, df1), (dg1, dnw2, dsc2, dsh2) = _rowwise_bwd(
        "res1_b", _f_res_pre(0.5), [x0, f1], [g1, norm_mix, sc2, sh2],
        [dx1, dh2], [F32, BF16], 256, after=[token, token3])
    rs_mx, token = _rs_chips_start("mx", pair_mx, df1, c_idx)
    dh1 = ffn_bwd("1", df1, s1, a1, b1, h1, W["g1"], W["u1"], W["d1"], after=token)
    pair_f1, token = _rs_pair_start("f1", [G[k] for k in groups["f1"]])
    reduced.update(zip(groups["f2"], _rs_done("f2", share_f2, dh1)))
    share_mx, token2 = _rs_land("mx", rs_mx, [dh1, token], qc_idx)
    (grad_x,), (dnw1, dsc1, dsh1) = _rowwise_bwd(
        "pre1_b", _f_pre_keep, [x0], [norm_ffn1, sc1, sh1], [dx0, dh1], [F32], 512, after=[token, token2])
    reduced.update(zip(groups["mx"], _rs_done("mx", share_mx, grad_x)))
    tot_early = _sum8("sum_early", _allgather_done("ag_early", ag_early, grad_x), HALO)

    dbs_row = jnp.transpose(dbs[:, ::CHUNK]).reshape(1, D)
    small = _pack("pack_small", HALO, [
        ("rows", dnw1), ("rows", dnw2), ("wide", dbuv), ("slice", conv_small, 33, 34), ("slice", conv_small, 34, 35),
        ("wide", dbgab), ("rows", dlng), ("rows", dlnb), ("rows", dbs_row), ("slice", conv_small, 32, 33),
        ("rows", dclg), ("rows", dclb), ("rows", dnw3), ("rows", dnwf),
        ("rows", dsh1), ("rows", dsc1), ("rows", dg1), ("rows", dsh2), ("rows", dsc2), ("rows", dg2),
        ("rows", dsh3), ("rows", dsc3), ("rows", dg3), ("lanes", loss_blk), ("zeros", 6)])
    every = _allgather8("ag_small", small)
    rs_f1, token = _rs_chips_start("f1", pair_f1, every, c_idx)
    every = every.reshape(8, HALO, D)
    tot = _sum8("sum_small", every, HALO, after=token)
    loss = tot[25, 0]
    dmod_all = every[:, 16:16 + N_MOD, :].reshape(8, N_MOD * D)
    dmod_mine = lax.dynamic_slice(dmod_all, (0, q * n_ada), (8, n_ada))
    g_ada_w = _ada_wgrad("ada_wgrad", jnp.transpose(c_all), dmod_mine)

    given = dict(ada_w=(ada_w, m_ada_w, v_ada_w), ada_b=(ada_b, m_ada_b, v_ada_b),
                 norm_ffn1=(norm_ffn1, m_norm_ffn1, v_norm_ffn1),
                 ffn1_w_gate=(ffn1_w_gate, m_ffn1_w_gate, v_ffn1_w_gate), ffn1_w_up=(ffn1_w_up, m_ffn1_w_up, v_ffn1_w_up),
                 ffn1_w_down=(ffn1_w_down, m_ffn1_w_down, v_ffn1_w_down), norm_mix=(norm_mix, m_norm_mix, v_norm_mix),
                 mix_w_in=(mix_w_in, m_mix_w_in, v_mix_w_in), mix_b_in=(mix_b_in, m_mix_b_in, v_mix_b_in),
                 sgu_ln_g=(sgu_ln_g, m_sgu_ln_g, v_sgu_ln_g), sgu_ln_b=(sgu_ln_b, m_sgu_ln_b, v_sgu_ln_b),
                 sgu_w_s=(sgu_w_s, m_sgu_w_s, v_sgu_w_s), sgu_b_s=(sgu_b_s, m_sgu_b_s, v_sgu_b_s),
                 conv_w=(conv_w, m_conv_w, v_conv_w), conv_b=(conv_b, m_conv_b, v_conv_b),
                 conv_ln_g=(conv_ln_g, m_conv_ln_g, v_conv_ln_g), conv_ln_b=(conv_ln_b, m_conv_ln_b, v_conv_ln_b),
                 w_branch_a=(w_branch_a, m_w_branch_a, v_w_branch_a), w_branch_b=(w_branch_b, m_w_branch_b, v_w_branch_b),
                 w_out=(w_out, m_w_out, v_w_out), norm_ffn2=(norm_ffn2, m_norm_ffn2, v_norm_ffn2),
                 ffn2_w_gate=(ffn2_w_gate, m_ffn2_w_gate, v_ffn2_w_gate), ffn2_w_up=(ffn2_w_up, m_ffn2_w_up, v_ffn2_w_up),
                 ffn2_w_down=(ffn2_w_down, m_ffn2_w_down, v_ffn2_w_down), norm_final=(norm_final, m_norm_final, v_norm_final))
    out = {}

    def big_update(name, g2d):
        w, m, v = given[name]
        shp = w.shape
        res = _adamw(f"adamw_{name}", w.reshape(g2d.shape), g2d, m.reshape(g2d.shape), v.reshape(g2d.shape),
                     _tile(g2d.shape[0], 512 if g2d.shape[1] <= D else 256))
        out[name] = tuple(t.reshape(shp) for t in res)

    big_names = dict(g1="ffn1_w_gate", u1="ffn1_w_up", d1="ffn1_w_down", win="mix_w_in", wa="w_branch_a",
                     wb="w_branch_b", wo="w_out", g2="ffn2_w_gate", u2="ffn2_w_up", d2="ffn2_w_down")
    for key in groups["f2"] + groups["mx"]:
        big_update(big_names[key], reduced[key])
    big_update("ada_w", g_ada_w)

    small_rows = [("norm_ffn1", 0, 1), ("norm_mix", 1, 1), ("mix_b_in", 2, 6), ("sgu_ln_g", 8, 1), ("sgu_ln_b", 9, 1),
                  ("sgu_b_s", 10, 1), ("conv_b", 11, 1), ("conv_ln_g", 12, 1), ("conv_ln_b", 13, 1),
                  ("norm_ffn2", 14, 1), ("norm_final", 15, 1), ("ada_b", 16, N_MOD)]

    def pack_rows(which, label):
        pieces = []
        for nm, _, r in small_rows:
            t = given[nm][which]
            pieces.append(("wide", t) if r > 1 else ("rows", t.reshape(1, D)))
        return _pack(f"pack_{label}", HALO, pieces + [("zeros", HALO - 16 - N_MOD)])

    small_res = _adamw("adamw_small", pack_rows(0, "w"), tot, pack_rows(1, "m"), pack_rows(2, "v"), HALO)
    for nm, r0, r in small_rows:
        shp = given[nm][0].shape
        out[nm] = tuple(t[r0:r0 + r].reshape(shp) for t in small_res)
    n_cw = conv_w.shape[2]
    g_cw = lax.dynamic_slice(tot_early, (0, q * n_cw), (HALO, n_cw))
    pad_cw = lambda t: jnp.pad(t[0], ((0, 1), (0, 0)))
    cw_res = _adamw("adamw_conv_w", pad_cw(conv_w), g_cw, pad_cw(m_conv_w), pad_cw(v_conv_w), HALO)
    out["conv_w"] = tuple(t[:KCONV][None] for t in cw_res)
    g_ws = jnp.transpose(tot_early[HALO:].reshape(CHUNK, HEADS, CHUNK), (1, 0, 2)).reshape(HEADS * CHUNK, CHUNK)
    flat_ws = lambda t: t.reshape(HEADS * CHUNK, CHUNK)
    ws_res = _adamw("adamw_sgu_w_s", flat_ws(sgu_w_s), g_ws, flat_ws(m_sgu_w_s), flat_ws(v_sgu_w_s), 512)
    out["sgu_w_s"] = tuple(t.reshape(sgu_w_s.shape) for t in ws_res)

    done = [out[n][1] for n in ("ada_w", "sgu_w_s", "conv_w", "ada_b")] + [out[big_names[k]][1] for k in groups["f2"] + groups["mx"]]
    reduced.update(zip(groups["f1"], _rs_finish("f1", rs_f1, done, qc_idx)))
    for key in groups["f1"]:
        big_update(big_names[key], reduced[key])

    order = ["ada_w", "ada_b", "norm_ffn1", "ffn1_w_gate", "ffn1_w_up", "ffn1_w_down", "norm_mix", "mix_w_in", "mix_b_in",
             "sgu_ln_g", "sgu_ln_b", "sgu_w_s", "sgu_b_s", "conv_w", "conv_b", "conv_ln_g", "conv_ln_b", "w_branch_a",
             "w_branch_b", "w_out", "norm_ffn2", "ffn2_w_gate", "ffn2_w_up", "ffn2_w_down", "norm_final"]
    return (loss, grad_x[None], *[out[n][0] for n in order], *[out[n][1] for n in order],
            *[out[n][2] for n in order], *[out[n][3] for n in order])
```

```python
import functools

import jax
import jax.numpy as jnp
from jax import lax
from jax.experimental import pallas as pl
from jax.experimental.pallas import tpu as pltpu

F32 = jnp.float32
BF16 = jnp.bfloat16
D = 1024
HEADS = 8
CHUNK = 128
KCONV = 31
HALO = 32
N_MOD = 9
EPS = 1e-6
NCHIP = 4
VMEM_LIMIT = 56 * 1024 * 1024
MESH = pl.DeviceIdType.MESH

ADAM_LR = 0.001
ADAM_B1 = 0.9
ADAM_B2 = 0.999
ADAM_EPS = 1e-08
ADAM_WD = 0.01
ADAM_STEP = 10

NN = (((1,), (0,)), ((), ()))
NT = (((1,), (1,)), ((), ()))
TN = (((0,), (0,)), ((), ()))


def _cparams(*sem):
    return pltpu.CompilerParams(dimension_semantics=sem or None, vmem_limit_bytes=VMEM_LIMIT)


def _rms(x, w):
    return x * lax.rsqrt(jnp.mean(x * x, axis=-1, keepdims=True) + EPS) * w


def _ln(x, g, b):
    mu = jnp.mean(x, axis=-1, keepdims=True)
    xc = x - mu
    var = jnp.mean(xc * xc, axis=-1, keepdims=True)
    return xc * lax.rsqrt(var + EPS) * g + b


def _silu(x):
    return x * jax.nn.sigmoid(x)


def _gelu(x):
    return x * (0.5 * (1.0 + jnp.tanh(0.7978845608028654 * (x + 0.044715 * (x * x * x)))))


def _f_pre(x, nw, sc, sh):
    return _rms(x, nw) * (1.0 + sc) + sh


def _f_pre_keep(x, nw, sc, sh):
    return x, _f_pre(x, nw, sc, sh)


def _f_res_pre(scale):
    def fn(x, f, g, nw, sc, sh):
        xn = x + (scale * g) * f
        return xn, _f_pre(xn, nw, sc, sh)
    return fn


def _f_glu(a, b):
    return _silu(a) * b


def _f_merge(gab, ya, yb, bgab):
    g = gab + bgab
    return jax.nn.sigmoid(g[:, :D]) * ya + jax.nn.sigmoid(g[:, D:]) * yb


def _f_lnsilu(z, g, b):
    return _silu(_ln(z, g, b))


def _f_mixa(uv, buv, g, b):
    t = uv + buv
    return _gelu(t[:, :D]), _ln(_gelu(t[:, D:]), g, b)


def _col(t):
    return t if isinstance(t, tuple) else (t, t.shape[-1], 0)


def _afters(after):
    if after is None:
        return []
    return list(after) if isinstance(after, (list, tuple)) else [after]


def _rowwise_fwd(name, fn, rows, params, outs, tr, after=None):
    rows = [_col(r) for r in rows]
    params = [_col(p) for p in params]
    extra = _afters(after)
    n_rows = rows[0][0].shape[0]
    nin = len(rows) + len(params)

    def body(*refs):
        res = fn(*[r[...].astype(F32) for r in refs[:nin]])
        res = res if isinstance(res, tuple) else (res,)
        for o, v in zip(refs[nin + len(extra):], res, strict=True):
            o[...] = v.astype(o.dtype)

    return pl.pallas_call(
        body, name=name, grid=(n_rows // tr,),
        in_specs=[pl.BlockSpec((tr, w), functools.partial(lambda cb, i: (i, cb), cb)) for _, w, cb in rows]
        + [pl.BlockSpec((1, w), functools.partial(lambda cb, i: (0, cb), cb)) for _, w, cb in params]
        + [pl.BlockSpec(memory_space=pl.ANY)] * len(extra),
        out_specs=[pl.BlockSpec((tr, w), lambda i: (i, 0)) for w, _ in outs],
        out_shape=[jax.ShapeDtypeStruct((n_rows, w), dt) for w, dt in outs],
        compiler_params=_cparams("parallel"),
    )(*[r[0] for r in rows], *[p[0] for p in params], *extra)


def _rowwise_bwd(name, fn, rows, params, cots, row_grads, tr, after=None):
    rows = [_col(r) for r in rows]
    params = [_col(p) for p in params]
    cots = [_col(t) for t in cots]
    extra = _afters(after)
    n_rows = rows[0][0].shape[0]
    nr, npar, nc = len(rows), len(params), len(cots)
    nin = nr + npar + nc
    n_rg = sum(dt is not None for dt in row_grads)

    def body(*refs):
        i = pl.program_id(0)
        prim = [r[...].astype(F32) for r in refs[:nr + npar]]
        ct = tuple(r[...].astype(F32) for r in refs[nr + npar:nin])
        _, vjp = jax.vjp(fn, *prim)
        g = vjp(ct if nc > 1 else ct[0])
        outs = refs[nin + len(extra):]
        oi = 0
        for j, dt in enumerate(row_grads):
            if dt is not None:
                outs[oi][...] = g[j].astype(dt)
                oi += 1
        for j in range(npar):
            acc = outs[n_rg + j]

            @pl.when(i == 0)
            def _(acc=acc):
                acc[...] = jnp.zeros(acc.shape, F32)

            acc[...] += g[nr + j]

    res = pl.pallas_call(
        body, name=name, grid=(n_rows // tr,),
        in_specs=[pl.BlockSpec((tr, w), functools.partial(lambda cb, i: (i, cb), cb)) for _, w, cb in rows]
        + [pl.BlockSpec((1, w), functools.partial(lambda cb, i: (0, cb), cb)) for _, w, cb in params]
        + [pl.BlockSpec((tr, w), functools.partial(lambda cb, i: (i, cb), cb)) for _, w, cb in cots]
        + [pl.BlockSpec(memory_space=pl.ANY)] * len(extra),
        out_specs=[pl.BlockSpec((tr, rows[j][1]), lambda i: (i, 0)) for j, dt in enumerate(row_grads) if dt is not None]
        + [pl.BlockSpec((1, w), lambda i: (0, 0)) for _, w, _ in params],
        out_shape=[jax.ShapeDtypeStruct((n_rows, rows[j][1]), dt) for j, dt in enumerate(row_grads) if dt is not None]
        + [jax.ShapeDtypeStruct((1, w), F32) for _, w, _ in params],
        compiler_params=_cparams("arbitrary"),
    )(*[r[0] for r in rows], *[p[0] for p in params], *[t[0] for t in cots], *extra)
    return res[:n_rg], res[n_rg:]


def _final(name, x, f, tgt, g, nw, tr):
    n_rows = x.shape[0]

    def body(x_ref, f_ref, t_ref, g_ref, nw_ref, dx_ref, df_ref, dg_ref, dnw_ref, loss_ref):
        i = pl.program_id(0)
        tg = t_ref[...]

        def fn(xv, fv, gv, nwv):
            e = _rms(xv + (0.5 * gv) * fv, nwv) - tg
            return 0.5 * jnp.mean(e * e, axis=-1, keepdims=True)

        per_row, vjp = jax.vjp(fn, x_ref[...], f_ref[...], g_ref[...], nw_ref[...])
        dx, df, dg, dnw = vjp(jnp.ones_like(per_row))
        dx_ref[...] = dx
        df_ref[...] = df.astype(df_ref.dtype)

        @pl.when(i == 0)
        def _():
            dg_ref[...] = jnp.zeros(dg_ref.shape, F32)
            dnw_ref[...] = jnp.zeros(dnw_ref.shape, F32)
            loss_ref[...] = jnp.zeros(loss_ref.shape, F32)

        dg_ref[...] += dg
        dnw_ref[...] += dnw
        loss_ref[...] += jnp.broadcast_to(jnp.sum(per_row, axis=0, keepdims=True), loss_ref.shape)

    row = pl.BlockSpec((tr, D), lambda i: (i, 0))
    par = pl.BlockSpec((1, D), lambda i: (0, 0))
    return pl.pallas_call(
        body, name=name, grid=(n_rows // tr,),
        in_specs=[row, row, row, par, par],
        out_specs=[row, row, par, par, pl.BlockSpec((8, 128), lambda i: (0, 0))],
        out_shape=[jax.ShapeDtypeStruct((n_rows, D), F32), jax.ShapeDtypeStruct((n_rows, D), BF16),
                   jax.ShapeDtypeStruct((1, D), F32), jax.ShapeDtypeStruct((1, D), F32),
                   jax.ShapeDtypeStruct((8, 128), F32)],
        compiler_params=_cparams("arbitrary"),
    )(x, f, tgt, g, nw)


def _tile(dim, pref):
    t = min(dim, pref)
    while dim % t:
        t //= 2
    return t


def _matmul(name, a, b, kind, col, out_dtype, add=None, after=None):
    if kind == "wgrad":
        m, kg = a.shape
        ng = b.shape[1]
        r, c = (kg, ng // NCHIP) if col else (kg // NCHIP, ng)
        tm, tn, tk = _tile(r, 1024), _tile(c, 1536), _tile(m, 2048)
        grid = (kg // tm, ng // tn, m // tk)
        a_spec = pl.BlockSpec((tk, tm), lambda i, j, k: (k, i))
        b_spec = pl.BlockSpec((tk, tn), lambda i, j, k: (k, j))
        if col:
            o_spec = pl.BlockSpec((None, tm, tn), lambda i, j, k: (j // (c // tn), i, j % (c // tn)))
        else:
            o_spec = pl.BlockSpec((None, tm, tn), lambda i, j, k: (i // (r // tm), i % (r // tm), j))
        out_shape = jax.ShapeDtypeStruct((NCHIP, r, c), out_dtype)
        dims = TN
    else:
        _, r, c = b.shape
        m = a.shape[0]
        kg, ng = (r, NCHIP * c) if col else (NCHIP * r, c)
        tm = _tile(m, 1024)
        if kind == "fwd":
            tn, tk = _tile(c, 1536), _tile(r, 2048)
            grid = (m // tm, ng // tn, kg // tk)
            a_spec = pl.BlockSpec((tm, tk), lambda i, j, k: (i, k))
            if col:
                b_spec = pl.BlockSpec((None, tk, tn), lambda i, j, k: (j // (c // tn), k, j % (c // tn)))
            else:
                b_spec = pl.BlockSpec((None, tk, tn), lambda i, j, k: (k // (r // tk), k % (r // tk), j))
            out_shape = jax.ShapeDtypeStruct((m, ng), out_dtype)
            dims = NN
        else:
            tn, tk = _tile(r, 1024), _tile(c, 2048)
            grid = (m // tm, kg // tn, ng // tk)
            a_spec = pl.BlockSpec((tm, tk), lambda i, j, k: (i, k))
            if col:
                b_spec = pl.BlockSpec((None, tn, tk), lambda i, j, k: (k // (c // tk), j, k % (c // tk)))
            else:
                b_spec = pl.BlockSpec((None, tn, tk), lambda i, j, k: (j // (r // tn), j % (r // tn), k))
            out_shape = jax.ShapeDtypeStruct((m, kg), out_dtype)
            dims = NT
        o_spec = pl.BlockSpec((tm, tn), lambda i, j, k: (i, j))
    nk = grid[2]
    has_add = add is not None
    extra = [] if after is None else [after]

    def body(*refs):
        a_ref, b_ref = refs[0], refs[1]
        o_ref = refs[2 + has_add + len(extra)]
        k = pl.program_id(2)
        p = lax.dot_general(a_ref[...].astype(BF16), b_ref[...].astype(BF16), dims, preferred_element_type=F32)

        def finish(r_):
            if has_add:
                r_ = r_ + refs[2][...]
            o_ref[...] = r_.astype(o_ref.dtype)

        if nk == 1:
            finish(p)
        else:
            acc_ref = refs[-1]

            @pl.when(k == 0)
            def _():
                acc_ref[...] = p

            @pl.when(jnp.logical_and(k > 0, k < nk - 1))
            def _():
                acc_ref[...] += p

            @pl.when(k == nk - 1)
            def _():
                finish(acc_ref[...] + p)

    in_specs = [a_spec, b_spec] + ([pl.BlockSpec((tm, tn), lambda i, j, k: (i, j))] if has_add else [])
    in_specs += [pl.BlockSpec(memory_space=pl.ANY)] * len(extra)
    args = (a, b) + ((add,) if has_add else ()) + tuple(extra)
    return pl.pallas_call(
        body, name=name, grid=grid, in_specs=in_specs, out_specs=o_spec, out_shape=out_shape,
        scratch_shapes=[pltpu.VMEM((tm, tn), F32)] if nk > 1 else [],
        compiler_params=_cparams("parallel", "parallel", "arbitrary"),
    )(*args)


def _ffn_up(name, h, wg, wu):
    m, k = h.shape
    _, _, c = wg.shape
    n = NCHIP * c
    tm, tn = _tile(m, 1024), _tile(c, 1024)

    def body(h_ref, wg_ref, wu_ref, a_ref, b_ref, s_ref):
        hv = h_ref[...]
        a = jnp.dot(hv, wg_ref[...], preferred_element_type=F32)
        b = jnp.dot(hv, wu_ref[...], preferred_element_type=F32)
        a_ref[...] = a
        b_ref[...] = b
        s_ref[...] = (_silu(a) * b).astype(s_ref.dtype)

    w_spec = pl.BlockSpec((None, k, tn), lambda i, j: (j // (c // tn), 0, j % (c // tn)))
    o_spec = pl.BlockSpec((tm, tn), lambda i, j: (i, j))
    return pl.pallas_call(
        body, name=name, grid=(m // tm, n // tn),
        in_specs=[pl.BlockSpec((tm, k), lambda i, j: (i, 0)), w_spec, w_spec],
        out_specs=[o_spec, o_spec, o_spec],
        out_shape=[jax.ShapeDtypeStruct((m, n), F32), jax.ShapeDtypeStruct((m, n), F32),
                   jax.ShapeDtypeStruct((m, n), BF16)],
        compiler_params=_cparams("parallel", "parallel"),
    )(h, wg, wu)


def _dgrad_pair(name, da, db, wg, wu):
    m = da.shape[0]
    _, r, c = wg.shape
    tm, tn = _tile(m, 1024), _tile(r, 1024)
    n1 = NCHIP

    def body(da_ref, db_ref, wg_ref, wu_ref, o_ref, acc_ref):
        k = pl.program_id(2)

        @pl.when(k == 0)
        def _():
            acc_ref[...] = lax.dot_general(da_ref[...], wg_ref[...], NT, preferred_element_type=F32)

        @pl.when(jnp.logical_and(k > 0, k < n1))
        def _():
            acc_ref[...] += lax.dot_general(da_ref[...], wg_ref[...], NT, preferred_element_type=F32)

        @pl.when(jnp.logical_and(k >= n1, k < 2 * n1 - 1))
        def _():
            acc_ref[...] += lax.dot_general(db_ref[...], wu_ref[...], NT, preferred_element_type=F32)

        @pl.when(k == 2 * n1 - 1)
        def _():
            o_ref[...] = acc_ref[...] + lax.dot_general(db_ref[...], wu_ref[...], NT, preferred_element_type=F32)

    first = lambda k: jnp.minimum(k, n1 - 1)
    second = lambda k: jnp.maximum(k - n1, 0)
    return pl.pallas_call(
        body, name=name, grid=(m // tm, r // tn, 2 * n1),
        in_specs=[pl.BlockSpec((tm, c), lambda i, j, k: (i, first(k))),
                  pl.BlockSpec((tm, c), lambda i, j, k: (i, second(k))),
                  pl.BlockSpec((None, tn, c), lambda i, j, k: (first(k), j, 0)),
                  pl.BlockSpec((None, tn, c), lambda i, j, k: (second(k), j, 0))],
        out_specs=pl.BlockSpec((tm, tn), lambda i, j, k: (i, j)),
        out_shape=jax.ShapeDtypeStruct((m, r), F32),
        scratch_shapes=[pltpu.VMEM((tm, tn), F32)],
        compiler_params=_cparams("parallel", "parallel", "arbitrary"),
    )(da, db, wg, wu)


def _dgrad_glu(name, df, wd, a, b, after=None):
    m = df.shape[0]
    _, r, c = wd.shape
    tm, tn = _tile(m, 512), _tile(r, 1024)
    extra = [] if after is None else [after]

    def body(df_ref, w_ref, a_ref, b_ref, *rest):
        da_ref, db_ref = rest[-2], rest[-1]
        ds = lax.dot_general(df_ref[...], w_ref[...], NT, preferred_element_type=F32)
        a = a_ref[...]
        sig = jax.nn.sigmoid(a)
        t = a * sig
        da_ref[...] = (ds * b_ref[...] * (sig + t * (1.0 - sig))).astype(da_ref.dtype)
        db_ref[...] = (ds * t).astype(db_ref.dtype)

    t_spec = pl.BlockSpec((tm, tn), lambda i, j: (i, j))
    sds = jax.ShapeDtypeStruct((m, NCHIP * r), BF16)
    return pl.pallas_call(
        body, name=name, grid=(m // tm, NCHIP * r // tn),
        in_specs=[pl.BlockSpec((tm, c), lambda i, j: (i, 0)),
                  pl.BlockSpec((None, tn, c), lambda i, j: (j // (r // tn), j % (r // tn), 0)), t_spec, t_spec]
        + [pl.BlockSpec(memory_space=pl.ANY)] * len(extra),
        out_specs=[t_spec, t_spec], out_shape=[sds, sds],
        compiler_params=_cparams("parallel", "parallel"),
    )(df, wd, a, b, *extra)


def _sgu_fwd(name, proj, b_in, lng, lnb, wm, bst, tr):
    n_rows = proj.shape[0]

    def body(uv_ref, buv_ref, g_ref, b_ref, wm_ref, bst_ref, ya_ref):
        u, vln = _f_mixa(uv_ref[...], buv_ref[...], g_ref[...], b_ref[...])
        vb = vln.astype(BF16)
        for ck in range(tr // CHUNK):
            rs = slice(ck * CHUNK, (ck + 1) * CHUNK)
            for h in range(HEADS):
                cs = slice(h * CHUNK, (h + 1) * CHUNK)
                vm = jnp.dot(wm_ref[h], vb[rs, cs], preferred_element_type=F32) + bst_ref[:, cs]
                ya_ref[rs, cs] = (u[rs, cs] * vm).astype(ya_ref.dtype)

    return pl.pallas_call(
        body, name=name, grid=(n_rows // tr,),
        in_specs=[pl.BlockSpec((tr, 2 * D), lambda i: (i, 0)), pl.BlockSpec((1, 2 * D), lambda i: (0, 0)),
                  pl.BlockSpec((1, D), lambda i: (0, 0)), pl.BlockSpec((1, D), lambda i: (0, 0)),
                  pl.BlockSpec((HEADS, CHUNK, CHUNK), lambda i: (0, 0, 0)),
                  pl.BlockSpec((CHUNK, D), lambda i: (0, 0))],
        out_specs=pl.BlockSpec((tr, D), lambda i: (i, 0)),
        out_shape=jax.ShapeDtypeStruct((n_rows, D), BF16),
        compiler_params=_cparams("parallel"),
    )(proj, b_in, lng, lnb, wm, bst)


def _sgu_bwd(name, proj, b_in, lng, lnb, wm, wmt, bst, dya, tr):
    n_rows = proj.shape[0]

    def body(uv_ref, buv_ref, g_ref, b_ref, wm_ref, wmt_ref, bst_ref, dya_ref,
             duv_ref, dbuv_ref, dg_ref, db_ref, dws_ref, dbs_ref, du_s, dvln_s):
        i = pl.program_id(0)

        @pl.when(i == 0)
        def _():
            dbuv_ref[...] = jnp.zeros(dbuv_ref.shape, F32)
            dg_ref[...] = jnp.zeros(dg_ref.shape, F32)
            db_ref[...] = jnp.zeros(db_ref.shape, F32)
            dws_ref[...] = jnp.zeros(dws_ref.shape, F32)
            dbs_ref[...] = jnp.zeros(dbs_ref.shape, F32)

        (u, vln), vjp = jax.vjp(_f_mixa, uv_ref[...], buv_ref[...], g_ref[...], b_ref[...])
        vb = vln.astype(BF16)
        dya_v = dya_ref[...].astype(F32)
        tpos = lax.broadcasted_iota(jnp.int32, (CHUNK, CHUNK), 0)
        spos = lax.broadcasted_iota(jnp.int32, (CHUNK, CHUNK), 1)
        causal = (tpos >= spos).astype(F32)
        for ck in range(tr // CHUNK):
            rs = slice(ck * CHUNK, (ck + 1) * CHUNK)
            for h in range(HEADS):
                cs = slice(h * CHUNK, (h + 1) * CHUNK)
                vbh = vb[rs, cs]
                vm = jnp.dot(wm_ref[h], vbh, preferred_element_type=F32) + bst_ref[:, cs]
                dyb = dya_v[rs, cs]
                du_s[rs, cs] = dyb * vm
                dvm = dyb * u[rs, cs]
                dvmb = dvm.astype(BF16)
                dvln_s[rs, cs] = jnp.dot(wmt_ref[h], dvmb, preferred_element_type=F32)
                dws_ref[h] += causal * lax.dot_general(dvmb, vbh, NT, preferred_element_type=F32)
                dbs_ref[:, cs] += jnp.broadcast_to(jnp.sum(dvm, axis=1, keepdims=True), (CHUNK, CHUNK))
        duv, dbuv, dg, db = vjp((du_s[...], dvln_s[...]))
        duv_ref[...] = duv.astype(duv_ref.dtype)
        dbuv_ref[...] += dbuv
        dg_ref[...] += dg
        db_ref[...] += db

    par = pl.BlockSpec((1, D), lambda i: (0, 0))
    par2 = pl.BlockSpec((1, 2 * D), lambda i: (0, 0))
    w_spec = pl.BlockSpec((HEADS, CHUNK, CHUNK), lambda i: (0, 0, 0))
    b_spec = pl.BlockSpec((CHUNK, D), lambda i: (0, 0))
    return pl.pallas_call(
        body, name=name, grid=(n_rows // tr,),
        in_specs=[pl.BlockSpec((tr, 2 * D), lambda i: (i, 0)), par2, par, par, w_spec, w_spec, b_spec,
                  pl.BlockSpec((tr, D), lambda i: (i, 0))],
        out_specs=[pl.BlockSpec((tr, 2 * D), lambda i: (i, 0)), par2, par, par, w_spec, b_spec],
        out_shape=[jax.ShapeDtypeStruct((n_rows, 2 * D), BF16), jax.ShapeDtypeStruct((1, 2 * D), F32),
                   jax.ShapeDtypeStruct((1, D), F32), jax.ShapeDtypeStruct((1, D), F32),
                   jax.ShapeDtypeStruct((HEADS, CHUNK, CHUNK), F32), jax.ShapeDtypeStruct((CHUNK, D), F32)],
        scratch_shapes=[pltpu.VMEM((tr, D), F32), pltpu.VMEM((tr, D), F32)],
        compiler_params=_cparams("arbitrary"),
    )(proj, b_in, lng, lnb, wm, wmt, bst, dya)


CT = 256
RB = 128
CV0 = 2 * D // CT
CG0 = 3 * D // CT


def _shift_bank(win):
    return [win] + [pltpu.roll(win, RB + HALO - r, 0) for r in range(1, 8)]


def _shifted(bank, sh):
    lo = 8 * (sh // 8)
    return bank[sh % 8][lo:lo + RB]


def _conv_fwd(name, proj, b_in, w, cb):
    n_rows = proj.shape[0]

    def body(cv_ref, cg_ref, bcv_ref, bcg_ref, w_ref, cb_ref, z1_ref, zp_ref):
        zp_ref[0:HALO, :] = jnp.zeros((HALO, CT), F32)
        zp_ref[HALO:, :] = (cv_ref[...] + bcv_ref[...]) * jax.nn.sigmoid(cg_ref[...] + bcg_ref[...])

        def blk(rb, carry):
            base = pl.multiple_of(rb * RB, RB)
            bank = _shift_bank(zp_ref[pl.ds(base, RB + HALO), :])
            acc = jnp.broadcast_to(cb_ref[...], (RB, CT))
            for k in range(KCONV):
                acc = acc + w_ref[k:k + 1, :] * _shifted(bank, k + 2)
            z1_ref[pl.ds(base, RB), :] = acc
            return carry

        lax.fori_loop(0, n_rows // RB, blk, 0)

    return pl.pallas_call(
        body, name=name, grid=(D // CT,),
        in_specs=[pl.BlockSpec((n_rows, CT), lambda j: (0, CV0 + j)), pl.BlockSpec((n_rows, CT), lambda j: (0, CG0 + j)),
                  pl.BlockSpec((1, CT), lambda j: (0, CV0 + j)), pl.BlockSpec((1, CT), lambda j: (0, CG0 + j)),
                  pl.BlockSpec((HALO, CT), lambda j: (0, j)), pl.BlockSpec((1, CT), lambda j: (0, j))],
        out_specs=pl.BlockSpec((n_rows, CT), lambda j: (0, j)),
        out_shape=jax.ShapeDtypeStruct((n_rows, D), F32),
        scratch_shapes=[pltpu.VMEM((n_rows + HALO, CT), F32)],
        compiler_params=_cparams("parallel"),
    )(proj, proj, b_in, b_in, w, cb)


def _conv_bwd(name, dz1, proj, b_in, w):
    n_rows = proj.shape[0]

    def body(dz_ref, cv_ref, cg_ref, bcv_ref, bcg_ref, w_ref, dcv_ref, dcg_ref, sm_ref, zp_ref, dzp_ref):
        cvb = cv_ref[...] + bcv_ref[...]
        sg = jax.nn.sigmoid(cg_ref[...] + bcg_ref[...])
        zp_ref[0:HALO, :] = jnp.zeros((HALO, CT), F32)
        zp_ref[HALO:, :] = cvb * sg
        dz = dz_ref[...]
        dzp_ref[0:n_rows, :] = dz
        dzp_ref[n_rows:, :] = jnp.zeros((HALO, CT), F32)
        sm_ref[...] = jnp.zeros(sm_ref.shape, F32)
        sm_ref[32:33, :] = jnp.sum(dz, axis=0, keepdims=True)

        def blk(rb, carry):
            base = pl.multiple_of(rb * RB, RB)
            dbank = _shift_bank(dzp_ref[pl.ds(base, RB + HALO), :])
            zbank = _shift_bank(zp_ref[pl.ds(base, RB + HALO), :])
            dzb = dbank[0][0:RB]
            acc = jnp.zeros((RB, CT), F32)
            for k in range(KCONV):
                acc = acc + w_ref[k:k + 1, :] * _shifted(dbank, KCONV - 1 - k)
                sm_ref[k:k + 1, :] += jnp.sum(dzb * _shifted(zbank, k + 2), axis=0, keepdims=True)
            dzp_ref[pl.ds(base, RB), :] = acc
            return carry

        lax.fori_loop(0, n_rows // RB, blk, 0)
        dz0 = dzp_ref[0:n_rows, :]
        dcv = dz0 * sg
        dcg = dz0 * cvb * (sg * (1.0 - sg))
        dcv_ref[...] = dcv.astype(dcv_ref.dtype)
        dcg_ref[...] = dcg.astype(dcg_ref.dtype)
        sm_ref[33:34, :] = jnp.sum(dcv, axis=0, keepdims=True)
        sm_ref[34:35, :] = jnp.sum(dcg, axis=0, keepdims=True)

    col = pl.BlockSpec((n_rows, CT), lambda j: (0, j))
    return pl.pallas_call(
        body, name=name, grid=(D // CT,),
        in_specs=[col, pl.BlockSpec((n_rows, CT), lambda j: (0, CV0 + j)), pl.BlockSpec((n_rows, CT), lambda j: (0, CG0 + j)),
                  pl.BlockSpec((1, CT), lambda j: (0, CV0 + j)), pl.BlockSpec((1, CT), lambda j: (0, CG0 + j)),
                  pl.BlockSpec((HALO, CT), lambda j: (0, j))],
        out_specs=[col, col, pl.BlockSpec((40, CT), lambda j: (0, j))],
        out_shape=[jax.ShapeDtypeStruct((n_rows, D), BF16), jax.ShapeDtypeStruct((n_rows, D), BF16),
                   jax.ShapeDtypeStruct((40, D), F32)],
        scratch_shapes=[pltpu.VMEM((n_rows + HALO, CT), F32), pltpu.VMEM((n_rows + HALO, CT), F32)],
        compiler_params=_cparams("parallel"),
    )(dz1, proj, proj, b_in, b_in, w)


ADA_TN = 768


def _split_bf16(v):
    hi = v.astype(BF16)
    return hi, (v - hi.astype(F32)).astype(BF16)


def _ada_fwd(name, c_all, w, b):
    n = w.shape[1]

    def body(c_ref, w_ref, b_ref, o_ref):
        ch, cl = _split_bf16(_silu(c_ref[...]))
        wh, wl = _split_bf16(w_ref[...])
        acc = jnp.dot(ch, wl, preferred_element_type=F32) + jnp.dot(cl, wh, preferred_element_type=F32)
        o_ref[...] = acc + jnp.dot(ch, wh, preferred_element_type=F32) + b_ref[...]

    return pl.pallas_call(
        body, name=name, grid=(n // ADA_TN,),
        in_specs=[pl.BlockSpec((8, D), lambda j: (0, 0)), pl.BlockSpec((D, ADA_TN), lambda j: (0, j)),
                  pl.BlockSpec((1, ADA_TN), lambda j: (0, j))],
        out_specs=pl.BlockSpec((8, ADA_TN), lambda j: (0, j)),
        out_shape=jax.ShapeDtypeStruct((8, n), F32),
        compiler_params=_cparams("parallel"),
    )(c_all, w, b)


def _ada_wgrad(name, c_all_t, dmod):
    n = dmod.shape[1]

    def body(ct_ref, dm_ref, o_ref):
        ca = _silu(ct_ref[...])
        acc = ca[:, 0:1] * dm_ref[0:1, :]
        for r in range(1, 8):
            acc = acc + ca[:, r:r + 1] * dm_ref[r:r + 1, :]
        o_ref[...] = acc

    return pl.pallas_call(
        body, name=name, grid=(n // ADA_TN,),
        in_specs=[pl.BlockSpec((D, 8), lambda j: (0, 0)), pl.BlockSpec((8, ADA_TN), lambda j: (0, j))],
        out_specs=pl.BlockSpec((D, ADA_TN), lambda j: (0, j)),
        out_shape=jax.ShapeDtypeStruct((D, n), F32),
        compiler_params=_cparams("parallel"),
    )(c_all_t, dmod)


def _adamw(name, w, g, m, v, tr):
    rows, cols = w.shape

    def body(w_ref, g_ref, m_ref, v_ref, g_out, d_ref, nm_ref, nv_ref):
        gv = g_ref[...]
        nm = ADAM_B1 * m_ref[...] + (1.0 - ADAM_B1) * gv
        nv = ADAM_B2 * v_ref[...] + (1.0 - ADAM_B2) * (gv * gv)
        m_hat = nm / (1.0 - ADAM_B1 ** ADAM_STEP)
        v_hat = nv / (1.0 - ADAM_B2 ** ADAM_STEP)
        d_ref[...] = -ADAM_LR * (m_hat / (jnp.sqrt(v_hat) + ADAM_EPS) + ADAM_WD * w_ref[...])
        nm_ref[...] = nm
        nv_ref[...] = nv
        g_out[...] = gv

    spec = pl.BlockSpec((tr, cols), lambda i: (i, 0))
    sds = jax.ShapeDtypeStruct((rows, cols), F32)
    return pl.pallas_call(
        body, name=name, grid=(rows // tr,), in_specs=[spec] * 4, out_specs=[spec] * 4, out_shape=[sds] * 4,
        compiler_params=_cparams("parallel"),
    )(w, g, m, v)


def _sum8(name, stacked, tr, after=None):
    n, rows, cols = stacked.shape
    extra = _afters(after)

    def body(s_ref, *rest):
        o_ref = rest[-1]
        acc = s_ref[0]
        for r in range(1, n):
            acc = acc + s_ref[r]
        o_ref[...] = acc

    return pl.pallas_call(
        body, name=name, grid=(rows // tr,),
        in_specs=[pl.BlockSpec((n, tr, cols), lambda i: (0, i, 0))] + [pl.BlockSpec(memory_space=pl.ANY)] * len(extra),
        out_specs=pl.BlockSpec((tr, cols), lambda i: (i, 0)),
        out_shape=jax.ShapeDtypeStruct((rows, cols), F32),
        compiler_params=_cparams("parallel"),
    )(stacked, *extra)


def _pair_add(name, g5, other, c_idx, tr):
    nq, _, rows, cols = g5.shape

    def body(c_ref, g_ref, o_ref, p_ref):
        p_ref[...] = (g_ref[...].astype(F32) + o_ref[...].astype(F32)).astype(p_ref.dtype)

    return pl.pallas_call(
        body, name=name,
        grid_spec=pltpu.PrefetchScalarGridSpec(
            num_scalar_prefetch=1, grid=(nq, rows // tr),
            in_specs=[pl.BlockSpec((None, None, tr, cols), lambda qi, i, cr: (qi, cr[0], i, 0)),
                      pl.BlockSpec((None, tr, cols), lambda qi, i, cr: (qi, i, 0))],
            out_specs=pl.BlockSpec((None, tr, cols), lambda qi, i, cr: (qi, i, 0))),
        out_shape=jax.ShapeDtypeStruct((nq, rows, cols), BF16),
        compiler_params=_cparams("parallel", "parallel"),
    )(c_idx, g5, other)


def _chip_add(name, p, recv, qc_idx, tr):
    _, rows, cols = p.shape

    def body(qc_ref, p_ref, r_ref, o_ref):
        acc = p_ref[...].astype(F32)
        for k in range(NCHIP - 1):
            acc = acc + r_ref[k].astype(F32)
        o_ref[...] = acc

    return pl.pallas_call(
        body, name=name,
        grid_spec=pltpu.PrefetchScalarGridSpec(
            num_scalar_prefetch=1, grid=(rows // tr,),
            in_specs=[pl.BlockSpec((None, tr, cols), lambda i, qc: (qc[0], i, 0)),
                      pl.BlockSpec((NCHIP - 1, tr, cols), lambda i, qc: (0, i, 0))],
            out_specs=pl.BlockSpec((None, tr, cols), lambda i, qc: (qc[1], i, 0))),
        out_shape=jax.ShapeDtypeStruct((2, rows, cols), F32),
        compiler_params=_cparams("parallel"),
    )(qc_idx, p, recv)


def _allgather8(name, blk, after=None):
    m_per, n = blk.shape
    extra = _afters(after)

    def body(x_ref, *rest):
        out_ref, send_sems, recv_sems, local_sem = rest[len(extra):]
        x, y, c = lax.axis_index("x"), lax.axis_index("y"), lax.axis_index("c")
        me, sibling = (x, y, c), (x, y, 1 - c)
        chips = [(1 - x, y), (x, 1 - y), (1 - x, 1 - y)]

        def rows(px, py, pc):
            return out_ref.at[pl.ds((4 * px + 2 * py + pc) * m_per, m_per), :]

        def copy(k, block, to, src=None):
            return pltpu.make_async_remote_copy(
                src_ref=rows(*block) if src is None else src, dst_ref=rows(*block),
                send_sem=send_sems.at[k], recv_sem=recv_sems.at[k], device_id=to, device_id_type=MESH)

        mine = pltpu.make_async_copy(x_ref, rows(*me), local_sem)
        mine.start()
        first = [copy(0, me, sibling, src=x_ref)]
        first += [copy(1 + j, me, (*chip, c), src=x_ref) for j, chip in enumerate(chips)]
        for cp in first:
            cp.start()
        passed = [copy(4 + j, (*chip, c), sibling) for j, chip in enumerate(chips)]
        for j, chip in enumerate(chips):
            copy(1 + j, (*chip, c), me).wait_recv()
            passed[j].start()
        copy(0, sibling, me).wait_recv()
        for j, chip in enumerate(chips):
            copy(4 + j, (*chip, 1 - c), me).wait_recv()
        for cp in first + passed:
            cp.wait_send()
        mine.wait()

    return pl.pallas_call(
        body, name=name,
        out_shape=jax.ShapeDtypeStruct((8 * m_per, n), blk.dtype),
        in_specs=[pl.BlockSpec(memory_space=pltpu.VMEM)] + [pl.BlockSpec(memory_space=pl.ANY)] * len(extra),
        out_specs=pl.BlockSpec(memory_space=pltpu.VMEM),
        scratch_shapes=[pltpu.SemaphoreType.DMA((7,)), pltpu.SemaphoreType.DMA((7,)), pltpu.SemaphoreType.DMA],
        compiler_params=pltpu.CompilerParams(vmem_limit_bytes=VMEM_LIMIT),
    )(blk, *extra)


def _exchange(name, srcs, out_shapes, n_local, n_remote, plan, aliases=None, after=None):
    ni, no = len(srcs), len(out_shapes)
    extra = [] if after is None else [after]

    def body(*refs):
        ins, outs = refs[:ni], refs[ni + len(extra):ni + len(extra) + no]
        send_sems, recv_sems, local_sems = refs[ni + len(extra) + no:]
        x, y, c = lax.axis_index("x"), lax.axis_index("y"), lax.axis_index("c")
        local, remote = plan(ins, outs, x, y, c)
        assert len(local) == n_local and len(remote) == n_remote

        def rcopy(i, dst):
            s, _, peer, _, _ = remote[i]
            return pltpu.make_async_remote_copy(src_ref=s, dst_ref=dst, send_sem=send_sems.at[i],
                                                recv_sem=recv_sems.at[i], device_id=peer, device_id_type=MESH)

        lcs = [pltpu.make_async_copy(s, d, local_sems.at[i]) for i, (s, d) in enumerate(local)]
        for cp in lcs:
            cp.start()
        first = [i for i in range(n_remote) if remote[i][4] is None]
        passed = [i for i in range(n_remote) if remote[i][4] is not None]
        for i in first:
            rcopy(i, remote[i][1]).start()
        arrived = set()
        for i in passed:
            j = remote[i][4]
            rcopy(j, remote[j][3]).wait_recv()
            arrived.add(j)
            rcopy(i, remote[i][1]).start()
        for i in range(n_remote):
            if i not in arrived:
                rcopy(i, remote[i][3]).wait_recv()
        for i in range(n_remote):
            rcopy(i, remote[i][1]).wait_send()
        for cp in lcs:
            cp.wait()

    any_spec = pl.BlockSpec(memory_space=pl.ANY)
    return pl.pallas_call(
        body, name=name, out_shape=out_shapes,
        in_specs=[any_spec] * (ni + len(extra)), out_specs=[any_spec] * no,
        input_output_aliases=aliases or {},
        scratch_shapes=[pltpu.SemaphoreType.DMA((n_remote,)), pltpu.SemaphoreType.DMA((n_remote,)),
                        pltpu.SemaphoreType.DMA((max(n_local, 1),))],
    )(*srcs, *extra)


_CHIP_FLIPS = ((0, 1), (1, 0), (1, 1))


def _flip(v, f):
    return 1 - v if f else v


EFFECT = pltpu.SideEffectType.DATAFLOW_SIDE_EFFECTING


def _split_start(name, bufs, n, plan, after=None):
    nb = len(bufs)
    extra = [] if after is None else [after]

    def body(*refs):
        ins = refs[:nb]
        send_sems, recv_sems = refs[nb + len(extra)], refs[nb + len(extra) + 1]
        token = refs[-1]
        x, y, c = lax.axis_index("x"), lax.axis_index("y"), lax.axis_index("c")
        copies = plan(ins, x, y, c)
        assert len(copies) == n
        for i, (s, d, peer, _) in enumerate(copies):
            pltpu.make_async_remote_copy(src_ref=s, dst_ref=d, send_sem=send_sems.at[i], recv_sem=recv_sems.at[i],
                                         device_id=peer, device_id_type=MESH).start()
        token[...] = jnp.zeros_like(token)

    hbm = pl.BlockSpec(memory_space=pltpu.HBM)
    sem = pl.BlockSpec(memory_space=pltpu.SEMAPHORE)
    res = pl.pallas_call(
        body, name=name,
        out_shape=(pltpu.SemaphoreType.DMA((n,)), pltpu.SemaphoreType.DMA((n,)),
                   *[pltpu.HBM(b.shape, b.dtype) for b in bufs], jax.ShapeDtypeStruct((8, 128), F32)),
        in_specs=[hbm] * nb + [pl.BlockSpec(memory_space=pl.ANY)] * len(extra),
        out_specs=(sem, sem, *[hbm] * nb, pl.BlockSpec(memory_space=pltpu.VMEM)),
        input_output_aliases={i: 2 + i for i in range(nb)},
        compiler_params=pltpu.CompilerParams(has_side_effects=EFFECT),
    )(*[pltpu.with_memory_space_constraint(b, pltpu.HBM) for b in bufs], *extra)
    return res[0], res[1], list(res[2:2 + nb]), res[-1]


def _split_wait(name, bufs, send_sems, recv_sems, after, n, plan):
    nb = len(bufs)

    def body(*refs):
        ins = refs[:nb]
        ssem, rsem = refs[nb], refs[nb + 1]
        x, y, c = lax.axis_index("x"), lax.axis_index("y"), lax.axis_index("c")
        copies = plan(ins, x, y, c)
        assert len(copies) == n
        for i, (s, _, peer, lands) in enumerate(copies):
            cp = pltpu.make_async_remote_copy(src_ref=s, dst_ref=lands, send_sem=ssem.at[i], recv_sem=rsem.at[i],
                                              device_id=peer, device_id_type=MESH)
            cp.wait_send()
            cp.wait_recv()

    hbm = pl.BlockSpec(memory_space=pltpu.HBM)
    sem = pl.BlockSpec(memory_space=pltpu.SEMAPHORE)
    afters = list(after) if isinstance(after, (list, tuple)) else [after]
    res = pl.pallas_call(
        body, name=name,
        out_shape=[pltpu.HBM(b.shape, b.dtype) for b in bufs],
        in_specs=[hbm] * nb + [sem, sem] + [pl.BlockSpec(memory_space=pl.ANY)] * len(afters),
        out_specs=[hbm] * nb,
        input_output_aliases={i: i for i in range(nb)},
        compiler_params=pltpu.CompilerParams(has_side_effects=EFFECT),
    )(*bufs, send_sems, recv_sems, *afters)
    return list(res)


def _chips_of(x, y):
    return [(_flip(x, fx), _flip(y, fy)) for fx, fy in _CHIP_FLIPS]


def _gather_start(tag, stacks, after):
    n = len(stacks)

    def plan(refs, x, y, c):
        q = 2 * x + y
        return [(refs[a].at[q, c], refs[a].at[q, c], (px, py, c), refs[a].at[2 * px + py, c])
                for a in range(n) for px, py in _chips_of(x, y)]

    send, recv, thru, token = _split_start(f"gather_start_{tag}", stacks, 3 * n, plan, after)
    return (send, recv, thru, plan), token


def _gather_finish(tag, state, after):
    send, recv, thru, plan = state
    n = len(thru)
    landed = _split_wait(f"gather_wait_{tag}", thru, send, recv, after, 3 * n, plan)

    def pass_on(ins, outs, x, y, c):
        sib = (x, y, 1 - c)
        remote = []
        for a in range(n):
            for px, py in _chips_of(x, y):
                p = 2 * px + py
                remote.append((ins[a].at[p, c], outs[a].at[p, c], sib, outs[a].at[p, 1 - c], None))
        return [], remote

    full = _exchange(f"gather_pass_{tag}", landed, [jax.ShapeDtypeStruct(s.shape, s.dtype) for s in landed],
                     0, 3 * n, pass_on, aliases={a: a for a in range(n)})
    return [g.reshape(NCHIP, 2 * g.shape[2], g.shape[3]) for g in full]


def _gather_land(tag, state, after):
    send, recv, thru, plan = state
    n = len(thru)
    landed = _split_wait(f"gather_wait_{tag}", thru, send, recv, after, 3 * n, plan)

    def pass_plan(refs, x, y, c):
        sib = (x, y, 1 - c)
        return [(refs[a].at[2 * px + py, c], refs[a].at[2 * px + py, c], sib, refs[a].at[2 * px + py, 1 - c])
                for a in range(n) for px, py in _chips_of(x, y)]

    send, recv, thru, token = _split_start(f"gather_pass_start_{tag}", landed, 3 * n, pass_plan)
    return (send, recv, thru, pass_plan), token


def _gather_done(tag, state, after):
    send, recv, thru, plan = state
    full = _split_wait(f"gather_pass_wait_{tag}", thru, send, recv, after, 3 * len(thru), plan)
    return [g.reshape(NCHIP, 2 * g.shape[2], g.shape[3]) for g in full]


_OTHERS = tuple((fx, fy, fc) for fx in (0, 1) for fy in (0, 1) for fc in (0, 1) if fx or fy or fc)


def _allgather_start(tag, buf, after=None):
    def plan(refs, x, y, c):
        me = 4 * x + 2 * y + c
        copies = []
        for fx, fy, fc in _OTHERS:
            px, py, pc = _flip(x, fx), _flip(y, fy), _flip(c, fc)
            copies.append((refs[0].at[me], refs[0].at[me], (px, py, pc), refs[0].at[4 * px + 2 * py + pc]))
        return copies

    send, recv, thru, token = _split_start(f"{tag}_start", [buf], len(_OTHERS), plan, after)
    return (send, recv, thru, plan), token


def _allgather_done(tag, state, after):
    send, recv, thru, plan = state
    return _split_wait(f"{tag}_wait", thru, send, recv, after, len(_OTHERS), plan)[0]


def _gather_small(name, smalls):
    n = len(smalls)

    def plan(ins, outs, x, y, c):
        q = 2 * x + y
        local = [(ins[b], outs[b].at[q]) for b in range(n)]
        remote = [(ins[b], outs[b].at[q], (px, py, c), outs[b].at[2 * px + py], None)
                  for b in range(n) for px, py in _chips_of(x, y)]
        return local, remote

    return _exchange(name, smalls, [jax.ShapeDtypeStruct((NCHIP,) + s.shape, s.dtype) for s in smalls], n, 3 * n, plan)


def _rs_pair_start(tag, grads, after=None):
    n = len(grads)
    g5 = [g.reshape(NCHIP, 2, g.shape[1] // 2, g.shape[2]) for g in grads]
    half = [(g.shape[2], g.shape[3]) for g in g5]
    land = [lax.empty((NCHIP,) + h, g.dtype) for h, g in zip(half, grads)]

    def plan(refs, x, y, c):
        sib = (x, y, 1 - c)
        return [(refs[a].at[:, 1 - c], refs[n + a], sib, refs[n + a]) for a in range(n)]

    send, recv, thru, token = _split_start(f"rs_pair_start_{tag}", g5 + land, n, plan, after)
    return (send, recv, thru, plan, half), token


def _rs_chips_start(tag, state, after, c_idx):
    send, recv, thru, pair_plan, half = state
    n = len(half)
    landed = _split_wait(f"rs_pair_wait_{tag}", thru, send, recv, after, n, pair_plan)
    g5, got = landed[:n], landed[n:]
    part = [_pair_add(f"rs_pair_add_{tag}{a}", g5[a], got[a], c_idx, _tile(half[a][0], 512)) for a in range(n)]
    land = [lax.empty((NCHIP - 1,) + h, BF16) for h in half]

    def plan(refs, x, y, c):
        return [(refs[a].at[2 * px + py], refs[n + a].at[k], (px, py, c), refs[n + a].at[k])
                for a in range(n) for k, (px, py) in enumerate(_chips_of(x, y))]

    send, recv, thru, token = _split_start(f"rs_chips_start_{tag}", part + land, 3 * n, plan)
    return (send, recv, thru, plan, half), token


def _rs_finish(tag, state, after, qc_idx):
    send, recv, thru, plan, half = state
    n = len(half)
    landed = _split_wait(f"rs_chips_wait_{tag}", thru, send, recv, after, 3 * n, plan)
    red = [_chip_add(f"rs_chip_add_{tag}{a}", landed[a], landed[n + a], qc_idx, _tile(half[a][0], 512)) for a in range(n)]

    def share(ins, outs, x, y, c):
        sib = (x, y, 1 - c)
        return [], [(ins[a].at[c], outs[a].at[c], sib, outs[a].at[1 - c], None) for a in range(n)]

    full = _exchange(f"rs_share_{tag}", red, [jax.ShapeDtypeStruct((2,) + h, F32) for h in half], 0, n, share,
                     aliases={a: a for a in range(n)})
    return [f.reshape(2 * h[0], h[1]) for f, h in zip(full, half)]


def _rs_land(tag, state, after, qc_idx):
    send, recv, thru, plan, half = state
    n = len(half)
    landed = _split_wait(f"rs_chips_wait_{tag}", thru, send, recv, after, 3 * n, plan)
    red = [_chip_add(f"rs_chip_add_{tag}{a}", landed[a], landed[n + a], qc_idx, _tile(half[a][0], 512)) for a in range(n)]

    def share(refs, x, y, c):
        sib = (x, y, 1 - c)
        return [(refs[a].at[c], refs[a].at[c], sib, refs[a].at[1 - c]) for a in range(n)]

    send, recv, thru, token = _split_start(f"rs_share_start_{tag}", red, n, share)
    return (send, recv, thru, share, half), token


def _rs_done(tag, state, after):
    send, recv, thru, plan, half = state
    full = _split_wait(f"rs_share_wait_{tag}", thru, send, recv, after, len(half), plan)
    return [f.reshape(2 * h[0], h[1]) for f, h in zip(full, half)]


def _place(name, blk, idx):
    rows, cols = blk.shape

    def body(i_ref, b_ref, o_ref):
        o_ref[...] = b_ref[...]

    return pl.pallas_call(
        body, name=name,
        grid_spec=pltpu.PrefetchScalarGridSpec(
            num_scalar_prefetch=1, grid=(1,),
            in_specs=[pl.BlockSpec((rows, cols), lambda i, d: (0, 0))],
            out_specs=pl.BlockSpec((None, rows, cols), lambda i, d: (d[0], 0, 0))),
        out_shape=jax.ShapeDtypeStruct((8, rows, cols), blk.dtype),
        compiler_params=_cparams("arbitrary"),
    )(idx, blk)


def _pack(name, n_rows, pieces, after=None):
    arrays = [p[1] for p in pieces if p[0] != "zeros"]
    extra = [] if after is None else [after]

    def body(*refs):
        o_ref = refs[-1]
        o_ref[...] = jnp.zeros(o_ref.shape, F32)
        row, ai = 0, 0
        for p in pieces:
            kind = p[0]
            if kind == "zeros":
                row += p[1]
                continue
            ref = refs[len(extra) + ai]
            ai += 1
            if kind == "rows":
                r = ref.shape[0]
                o_ref[row:row + r, :] = ref[...]
            elif kind == "wide":
                r = ref.shape[1] // D
                for j in range(r):
                    o_ref[row + j:row + j + 1, :] = ref[:, j * D:(j + 1) * D]
            elif kind == "slice":
                r = p[3] - p[2]
                o_ref[row:row + r, :] = ref[p[2]:p[3], :]
            elif kind == "heads":
                r = CHUNK
                for h in range(HEADS):
                    o_ref[row:row + r, h * CHUNK:(h + 1) * CHUNK] = ref[h]
            else:
                r = 1
                o_ref[row:row + 1, 0:128] = ref[0:1, :]
            row += r
        assert row == n_rows, (row, n_rows)

    vmem = pl.BlockSpec(memory_space=pltpu.VMEM)
    return pl.pallas_call(
        body, name=name, out_shape=jax.ShapeDtypeStruct((n_rows, D), F32),
        in_specs=[pl.BlockSpec(memory_space=pl.ANY)] * len(extra) + [vmem] * len(arrays), out_specs=vmem,
        compiler_params=pltpu.CompilerParams(vmem_limit_bytes=VMEM_LIMIT),
    )(*extra, *arrays)


def _to_bf16_stack(name, w, q_idx):
    r, cols = w.shape
    half = r // 2
    tr = _tile(half, 256)
    nb = half // tr

    def body(q_ref, w_ref, o_ref):
        o_ref[...] = w_ref[...].astype(BF16)

    return pl.pallas_call(
        body, name=name,
        grid_spec=pltpu.PrefetchScalarGridSpec(
            num_scalar_prefetch=1, grid=(2 * nb,),
            in_specs=[pl.BlockSpec((tr, cols), lambda i, qr: (i, 0))],
            out_specs=pl.BlockSpec((None, None, tr, cols), lambda i, qr: (qr[0], i // nb, i % nb, 0))),
        out_shape=jax.ShapeDtypeStruct((NCHIP, 2, half, cols), BF16),
        compiler_params=_cparams("parallel"),
    )(q_idx, w)


def kernel(x, c, ada_w, ada_b, norm_ffn1, ffn1_w_gate, ffn1_w_up, ffn1_w_down, norm_mix, mix_w_in, mix_b_in, sgu_ln_g, sgu_ln_b, sgu_w_s, sgu_b_s, conv_w, conv_b, conv_ln_g, conv_ln_b, w_branch_a, w_branch_b, w_out, norm_ffn2, ffn2_w_gate, ffn2_w_up, ffn2_w_down, norm_final, loss_target, m_ada_w, m_ada_b, m_norm_ffn1, m_ffn1_w_gate, m_ffn1_w_up, m_ffn1_w_down, m_norm_mix, m_mix_w_in, m_mix_b_in, m_sgu_ln_g, m_sgu_ln_b, m_sgu_w_s, m_sgu_b_s, m_conv_w, m_conv_b, m_conv_ln_g, m_conv_ln_b, m_w_branch_a, m_w_branch_b, m_w_out, m_norm_ffn2, m_ffn2_w_gate, m_ffn2_w_up, m_ffn2_w_down, m_norm_final, v_ada_w, v_ada_b, v_norm_ffn1, v_ffn1_w_gate, v_ffn1_w_up, v_ffn1_w_down, v_norm_mix, v_mix_w_in, v_mix_b_in, v_sgu_ln_g, v_sgu_ln_b, v_sgu_w_s, v_sgu_b_s, v_conv_w, v_conv_b, v_conv_ln_g, v_conv_ln_b, v_w_branch_a, v_w_branch_b, v_w_out, v_norm_ffn2, v_ffn2_w_gate, v_ffn2_w_up, v_ffn2_w_down, v_norm_final):
    xi, yi, ci = lax.axis_index("x"), lax.axis_index("y"), lax.axis_index("c")
    q = 2 * xi + yi
    dev = 4 * xi + 2 * yi + ci
    c_idx = jnp.reshape(ci, (1,)).astype(jnp.int32)
    q_idx = jnp.reshape(q, (1,)).astype(jnp.int32)
    x0 = x[0]
    tgt = loss_target[0]
    qc_idx = jnp.stack([q, ci]).astype(jnp.int32)
    n_ada = ada_w.shape[2]

    big = dict(g1=ffn1_w_gate, u1=ffn1_w_up, d1=ffn1_w_down, win=mix_w_in, wa=w_branch_a, wb=w_branch_b, wo=w_out,
               g2=ffn2_w_gate, u2=ffn2_w_up, d2=ffn2_w_down)
    groups = dict(f1=("d1", "g1", "u1"), mx=("wo", "wa", "wb", "win"), f2=("d2", "g2", "u2"))
    gather_groups = dict(up1=("g1", "u1"), dn1=("d1",), win=("win",), mx=("wa", "wb", "wo"), up2=("g2", "u2"),
                         dn2=("d2",))
    first_stacks = [_to_bf16_stack(f"cast_{k}", big[k][0], q_idx) for k in gather_groups["up1"]]

    c_all = _allgather8("ag_c", jnp.pad(c, ((0, 7), (0, 0))), after=first_stacks)[::8]
    ada_b_mine = lax.dynamic_slice(ada_b, (0, q * n_ada), (1, n_ada))
    mod_part = _ada_fwd("ada_fwd", c_all, ada_w[0], ada_b_mine)
    small_parts = _gather_small("gather_small", [mod_part, jnp.pad(conv_w[0], ((0, 1), (0, 0)))])

    gather_state = {}
    started = [small_parts[0]]
    for tag, keys in gather_groups.items():
        stacks = first_stacks if tag == "up1" else [_to_bf16_stack(f"cast_{k}", big[k][0], q_idx) for k in keys]
        gather_state[tag], token = _gather_start(tag, stacks, started[-1])
        started.append(token)

    mod_all = jnp.transpose(small_parts[0], (1, 0, 2)).reshape(8, N_MOD * D)
    mod = lax.dynamic_slice(mod_all, (dev, 0), (1, N_MOD * D))
    sh1, sc1, g1, sh2, sc2, g2, sh3, sc3 = [(mod, D, k) for k in range(N_MOD - 1)]
    g3 = mod[:, (N_MOD - 1) * D:]
    convw = jnp.transpose(small_parts[1], (1, 0, 2)).reshape(HALO, D)
    (h1,) = _rowwise_fwd("pre1", _f_pre, [x0], [norm_ffn1, sc1, sh1], [(D, BF16)], 512)
    W = dict(zip(gather_groups["up1"], _gather_finish("up1", gather_state["up1"], [h1] + started)))

    causal = jnp.tril(jnp.ones((CHUNK, CHUNK), bool))
    wm_f = jnp.where(causal[None], sgu_w_s[0], 0.0)
    wm = wm_f.astype(BF16)
    wmt = jnp.swapaxes(wm_f, 1, 2).astype(BF16)
    bst = jnp.repeat(jnp.transpose(sgu_b_s[0]), CHUNK, axis=1)
    b_in = mix_b_in

    a1, b1, s1 = _ffn_up("up1", h1, W["g1"], W["u1"])
    W.update(zip(gather_groups["dn1"], _gather_finish("dn1", gather_state["dn1"], s1)))
    f1 = _matmul("down1", s1, W["d1"], "fwd", False, F32)
    pass_win, token = _gather_land("win", gather_state["win"], f1)
    x1, h2 = _rowwise_fwd("res1", _f_res_pre(0.5), [x0, f1], [g1, norm_mix, sc2, sh2],
                          [(D, F32), (D, BF16)], 256, after=token)
    W.update(zip(gather_groups["win"], _gather_done("win", pass_win, x1)))
    proj = _matmul("in_proj", h2, W["win"], "fwd", True, F32)
    pass_mx, token = _gather_land("mx", gather_state["mx"], proj)
    ya = _sgu_fwd("sgu", proj, b_in, sgu_ln_g, sgu_ln_b, wm, bst, 256)
    z1 = _conv_fwd("conv", proj, b_in, convw, conv_b)
    (z3,) = _rowwise_fwd("conv_ln", _f_lnsilu, [z1], [conv_ln_g, conv_ln_b], [(D, BF16)], 512, after=token)
    W.update(zip(gather_groups["mx"], _gather_done("mx", pass_mx, z3)))
    y_a = _matmul("branch_a", ya, W["wa"], "fwd", False, F32)
    y_b = _matmul("branch_b", z3, W["wb"], "fwd", False, F32)
    pass_up2, token = _gather_land("up2", gather_state["up2"], y_b)
    (merged,) = _rowwise_fwd("merge", _f_merge, [(proj, 2 * D, 2), y_a, y_b], [(b_in, 2 * D, 2)],
                             [(D, BF16)], 256, after=token)
    y = _matmul("out_proj", merged, W["wo"], "fwd", False, F32)
    x2, h3 = _rowwise_fwd("res2", _f_res_pre(1.0), [x1, y], [g2, norm_ffn2, sc3, sh3], [(D, F32), (D, BF16)], 512)
    W.update(zip(gather_groups["up2"], _gather_done("up2", pass_up2, x2)))
    a3, b3, s3 = _ffn_up("up2", h3, W["g2"], W["u2"])
    W.update(zip(gather_groups["dn2"], _gather_finish("dn2", gather_state["dn2"], s3)))
    f3 = _matmul("down2", s3, W["d2"], "fwd", False, F32)
    dx2, df3, dg3, dnwf, loss_blk = _final("final", x2, f3, tgt, g3, norm_final.reshape(1, D), 512)

    G = {}

    def ffn_bwd(tag, df, s, a, b, h, wg, wu, wd, after=None):
        da, db = _dgrad_glu(f"d_down{tag}", df, wd, a, b, after)
        G["d" + tag] = _matmul(f"w_down{tag}", s, df, "wgrad", False, BF16)
        dh = _dgrad_pair(f"d_gate_up{tag}", da, db, wg, wu)
        G["g" + tag] = _matmul(f"w_gate{tag}", h, da, "wgrad", True, BF16)
        G["u" + tag] = _matmul(f"w_up{tag}", h, db, "wgrad", True, BF16)
        return dh

    reduced = {}
    dh3 = ffn_bwd("2", df3, s3, a3, b3, h3, W["g2"], W["u2"], W["d2"])
    pair_f2, token = _rs_pair_start("f2", [G[k] for k in groups["f2"]])
    (dx1, dy), (dg2, dnw3, dsc3, dsh3) = _rowwise_bwd(
        "res2_b", _f_res_pre(1.0), [x1, y], [g2, norm_ffn2, sc3, sh3], [dx2, dh3], [F32, BF16], 256, after=token)
    dmerged = _matmul("d_out", dy, W["wo"], "dgrad", False, F32)
    G["wo"] = _matmul("w_out", merged, dy, "wgrad", False, BF16)
    rs_f2, token = _rs_chips_start("f2", pair_f2, [dmerged, G["wo"]], c_idx)
    (dgab, dy_a, dy_b), (dbgab,) = _rowwise_bwd(
        "merge_b", _f_merge, [(proj, 2 * D, 2), y_a, y_b], [(b_in, 2 * D, 2)], [dmerged],
        [BF16, BF16, BF16], 256, after=token)
    dya = _matmul("d_a", dy_a, W["wa"], "dgrad", False, F32)
    dz3 = _matmul("d_b", dy_b, W["wb"], "dgrad", False, F32)
    G["wa"] = _matmul("w_a", ya, dy_a, "wgrad", False, BF16)
    G["wb"] = _matmul("w_b", z3, dy_b, "wgrad", False, BF16)
    (dz1,), (dclg, dclb) = _rowwise_bwd("conv_ln_b", _f_lnsilu, [z1], [conv_ln_g, conv_ln_b], [dz3], [F32], 512)
    dcv, dcg, conv_small = _conv_bwd("conv_b", dz1, proj, b_in, convw)
    duv, dbuv, dlng, dlnb, dws, dbs = _sgu_bwd("sgu_b", proj, b_in, sgu_ln_g, sgu_ln_b, wm, wmt, bst, dya, 256)
    dproj = jnp.concatenate([duv, dcv, dcg, dgab], axis=1)
    dh2 = _matmul("d_in", dproj, W["win"], "dgrad", True, F32)
    G["win"] = _matmul("w_in", h2, dproj, "wgrad", True, BF16)
    pair_mx, token = _rs_pair_start("mx", [G[k] for k in groups["mx"]])
    share_f2, token2 = _rs_land("f2", rs_f2, [dh2, token], qc_idx)

    dev_idx = jnp.reshape(dev, (1,)).astype(jnp.int32)
    early = _pack("pack_early", HALO + CHUNK, [("slice", conv_small, 0, HALO), ("heads", dws)])
    ag_early, token3 = _allgather_start("ag_early", _place("place_early", early, dev_idx), token2)

    (dx0, df1), (dg1, dnw2, dsc2, dsh2) = _rowwise_bwd(
        "res1_b", _f_res_pre(0.5), [x0, f1], [g1, norm_mix, sc2, sh2],
        [dx1, dh2], [F32, BF16], 256, after=[token, token3])
    rs_mx, token = _rs_chips_start("mx", pair_mx, df1, c_idx)
    dh1 = ffn_bwd("1", df1, s1, a1, b1, h1, W["g1"], W["u1"], W["d1"], after=token)
    pair_f1, token = _rs_pair_start("f1", [G[k] for k in groups["f1"]])
    reduced.update(zip(groups["f2"], _rs_done("f2", share_f2, dh1)))
    share_mx, token2 = _rs_land("mx", rs_mx, [dh1, token], qc_idx)
    (grad_x,), (dnw1, dsc1, dsh1) = _rowwise_bwd(
        "pre1_b", _f_pre_keep, [x0], [norm_ffn1, sc1, sh1], [dx0, dh1], [F32], 512, after=[token, token2])
    reduced.update(zip(groups["mx"], _rs_done("mx", share_mx, grad_x)))
    tot_early = _sum8("sum_early", _allgather_done("ag_early", ag_early, grad_x), HALO)

    dbs_row = jnp.transpose(dbs[:, ::CHUNK]).reshape(1, D)
    small = _pack("pack_small", HALO, [
        ("rows", dnw1), ("rows", dnw2), ("wide", dbuv), ("slice", conv_small, 33, 34), ("slice", conv_small, 34, 35),
        ("wide", dbgab), ("rows", dlng), ("rows", dlnb), ("rows", dbs_row), ("slice", conv_small, 32, 33),
        ("rows", dclg), ("rows", dclb), ("rows", dnw3), ("rows", dnwf),
        ("rows", dsh1), ("rows", dsc1), ("rows", dg1), ("rows", dsh2), ("rows", dsc2), ("rows", dg2),
        ("rows", dsh3), ("rows", dsc3), ("rows", dg3), ("lanes", loss_blk), ("zeros", 6)])
    every = _allgather8("ag_small", small)
    rs_f1, token = _rs_chips_start("f1", pair_f1, every, c_idx)
    every = every.reshape(8, HALO, D)
    tot = _sum8("sum_small", every, HALO, after=token)
    loss = tot[25, 0]
    dmod_all = every[:, 16:16 + N_MOD, :].reshape(8, N_MOD * D)
    dmod_mine = lax.dynamic_slice(dmod_all, (0, q * n_ada), (8, n_ada))
    g_ada_w = _ada_wgrad("ada_wgrad", jnp.transpose(c_all), dmod_mine)

    given = dict(ada_w=(ada_w, m_ada_w, v_ada_w), ada_b=(ada_b, m_ada_b, v_ada_b),
                 norm_ffn1=(norm_ffn1, m_norm_ffn1, v_norm_ffn1),
                 ffn1_w_gate=(ffn1_w_gate, m_ffn1_w_gate, v_ffn1_w_gate), ffn1_w_up=(ffn1_w_up, m_ffn1_w_up, v_ffn1_w_up),
                 ffn1_w_down=(ffn1_w_down, m_ffn1_w_down, v_ffn1_w_down), norm_mix=(norm_mix, m_norm_mix, v_norm_mix),
                 mix_w_in=(mix_w_in, m_mix_w_in, v_mix_w_in), mix_b_in=(mix_b_in, m_mix_b_in, v_mix_b_in),
                 sgu_ln_g=(sgu_ln_g, m_sgu_ln_g, v_sgu_ln_g), sgu_ln_b=(sgu_ln_b, m_sgu_ln_b, v_sgu_ln_b),
                 sgu_w_s=(sgu_w_s, m_sgu_w_s, v_sgu_w_s), sgu_b_s=(sgu_b_s, m_sgu_b_s, v_sgu_b_s),
                 conv_w=(conv_w, m_conv_w, v_conv_w), conv_b=(conv_b, m_conv_b, v_conv_b),
                 conv_ln_g=(conv_ln_g, m_conv_ln_g, v_conv_ln_g), conv_ln_b=(conv_ln_b, m_conv_ln_b, v_conv_ln_b),
                 w_branch_a=(w_branch_a, m_w_branch_a, v_w_branch_a), w_branch_b=(w_branch_b, m_w_branch_b, v_w_branch_b),
                 w_out=(w_out, m_w_out, v_w_out), norm_ffn2=(norm_ffn2, m_norm_ffn2, v_norm_ffn2),
                 ffn2_w_gate=(ffn2_w_gate, m_ffn2_w_gate, v_ffn2_w_gate), ffn2_w_up=(ffn2_w_up, m_ffn2_w_up, v_ffn2_w_up),
                 ffn2_w_down=(ffn2_w_down, m_ffn2_w_down, v_ffn2_w_down), norm_final=(norm_final, m_norm_final, v_norm_final))
    out = {}

    def big_update(name, g2d):
        w, m, v = given[name]
        shp = w.shape
        res = _adamw(f"adamw_{name}", w.reshape(g2d.shape), g2d, m.reshape(g2d.shape), v.reshape(g2d.shape),
                     _tile(g2d.shape[0], 512 if g2d.shape[1] <= D else 256))
        out[name] = tuple(t.reshape(shp) for t in res)

    big_names = dict(g1="ffn1_w_gate", u1="ffn1_w_up", d1="ffn1_w_down", win="mix_w_in", wa="w_branch_a",
                     wb="w_branch_b", wo="w_out", g2="ffn2_w_gate", u2="ffn2_w_up", d2="ffn2_w_down")
    for key in groups["f2"] + groups["mx"]:
        big_update(big_names[key], reduced[key])
    big_update("ada_w", g_ada_w)

    small_rows = [("norm_ffn1", 0, 1), ("norm_mix", 1, 1), ("mix_b_in", 2, 6), ("sgu_ln_g", 8, 1), ("sgu_ln_b", 9, 1),
                  ("sgu_b_s", 10, 1), ("conv_b", 11, 1), ("conv_ln_g", 12, 1), ("conv_ln_b", 13, 1),
                  ("norm_ffn2", 14, 1), ("norm_final", 15, 1), ("ada_b", 16, N_MOD)]

    def pack_rows(which, label):
        pieces = []
        for nm, _, r in small_rows:
            t = given[nm][which]
            pieces.append(("wide", t) if r > 1 else ("rows", t.reshape(1, D)))
        return _pack(f"pack_{label}", HALO, pieces + [("zeros", HALO - 16 - N_MOD)])

    small_res = _adamw("adamw_small", pack_rows(0, "w"), tot, pack_rows(1, "m"), pack_rows(2, "v"), HALO)
    for nm, r0, r in small_rows:
        shp = given[nm][0].shape
        out[nm] = tuple(t[r0:r0 + r].reshape(shp) for t in small_res)
    n_cw = conv_w.shape[2]
    g_cw = lax.dynamic_slice(tot_early, (0, q * n_cw), (HALO, n_cw))
    pad_cw = lambda t: jnp.pad(t[0], ((0, 1), (0, 0)))
    cw_res = _adamw("adamw_conv_w", pad_cw(conv_w), g_cw, pad_cw(m_conv_w), pad_cw(v_conv_w), HALO)
    out["conv_w"] = tuple(t[:KCONV][None] for t in cw_res)
    g_ws = jnp.transpose(tot_early[HALO:].reshape(CHUNK, HEADS, CHUNK), (1, 0, 2)).reshape(HEADS * CHUNK, CHUNK)
    flat_ws = lambda t: t.reshape(HEADS * CHUNK, CHUNK)
    ws_res = _adamw("adamw_sgu_w_s", flat_ws(sgu_w_s), g_ws, flat_ws(m_sgu_w_s), flat_ws(v_sgu_w_s), 512)
    out["sgu_w_s"] = tuple(t.reshape(sgu_w_s.shape) for t in ws_res)

    done = [out[n][1] for n in ("ada_w", "sgu_w_s", "conv_w", "ada_b")] + [out[big_names[k]][1] for k in groups["f2"] + groups["mx"]]
    reduced.update(zip(groups["f1"], _rs_finish("f1", rs_f1, done, qc_idx)))
    for key in groups["f1"]:
        big_update(big_names[key], reduced[key])

    order = ["ada_w", "ada_b", "norm_ffn1", "ffn1_w_gate", "ffn1_w_up", "ffn1_w_down", "norm_mix", "mix_w_in", "mix_b_in",
             "sgu_ln_g", "sgu_ln_b", "sgu_w_s", "sgu_b_s", "conv_w", "conv_b", "conv_ln_g", "conv_ln_b", "w_branch_a",
             "w_branch_b", "w_out", "norm_ffn2", "ffn2_w_gate", "ffn2_w_up", "ffn2_w_down", "norm_final"]
    return (loss, grad_x[None], *[out[n][0] for n in order], *[out[n][1] for n in order],
            *[out[n][2] for n in order], *[out[n][3] for n in order])
```
